```python
import jax, jax.numpy as jnp
from jax import lax
import numpy as np

D_MODEL = 1024
BATCH = 8
SEQ = 2048
DEPTH = 1

FOX_HEADS = 8
FOX_HEAD_DIM = 64
FOX_WIDTH = FOX_HEADS * FOX_HEAD_DIM
Q_BLOCK = 128
HGRN_HEADS = 8
HGRN_KEY_DIM = 64
HGRN_VAL_DIM = 64
HGRN_WIDTH = HGRN_HEADS * HGRN_KEY_DIM
CHUNK = 64
N_GROUPS = 8
EXPERTS_PER_GROUP = 8
N_EXPERTS = N_GROUPS * EXPERTS_PER_GROUP
TOP_K_IN_GROUP = 2
EXPERT_FF = 256
ALPHA = (2 * DEPTH) ** 0.25
BETA = (8 * DEPTH) ** -0.25
LN_EPS = 1e-5
RMS_EPS = 1e-6
IN_WIDTHS = (FOX_WIDTH, FOX_WIDTH, FOX_WIDTH, FOX_HEADS,
             HGRN_WIDTH, HGRN_WIDTH, HGRN_HEADS * HGRN_VAL_DIM, HGRN_HEADS * HGRN_VAL_DIM,
             D_MODEL, D_MODEL)
IN_DIM = sum(IN_WIDTHS)

kernel_name = "fox_hgrn2_gated_hier_moe_deepnorm_adaln"


def layer_norm(x, g, b):
    xf = x.astype(jnp.float32)
    mu = jnp.mean(xf, axis=-1, keepdims=True)
    var = jnp.mean(jnp.square(xf - mu), axis=-1, keepdims=True)
    return ((xf - mu) * lax.rsqrt(var + LN_EPS) * g + b).astype(x.dtype)


def fox_attention(q, k, v, log_f):
    B, H, S, Dh = q.shape
    cum = jnp.cumsum(log_f, axis=-1)
    scale = Dh ** -0.5
    key_pos = jnp.arange(S)

    def block(i):
        start = i * Q_BLOCK
        qb = lax.dynamic_slice_in_dim(q, start, Q_BLOCK, axis=2)
        cb = lax.dynamic_slice_in_dim(cum, start, Q_BLOCK, axis=2)
        logits = (jnp.einsum('bhqd,bhkd->bhqk', qb, k).astype(jnp.float32) * scale
                  + cb[..., :, None] - cum[..., None, :])
        q_pos = start + jnp.arange(Q_BLOCK)
        logits = jnp.where(q_pos[:, None] >= key_pos[None, :], logits, -jnp.inf)
        p = jax.nn.softmax(logits, axis=-1).astype(v.dtype)
        return jnp.einsum('bhqk,bhkd->bhqd', p, v)

    out = lax.map(block, jnp.arange(S // Q_BLOCK))
    return jnp.moveaxis(out, 0, 2).reshape(B, H, S, Dh)


def hgrn2_chunked(q, k, v, log_f):
    B, H, S, dk = q.shape
    dv = v.shape[-1]
    nc = S // CHUNK

    def to_chunks(a):
        return jnp.moveaxis(a.astype(jnp.float32).reshape(B, H, nc, CHUNK, a.shape[-1]), 2, 0)

    causal = jnp.tril(jnp.ones((CHUNK, CHUNK), dtype=bool))

    def step(state, inp):
        qc, kc, vc, lfc = inp
        b = jnp.cumsum(lfc, axis=2)
        diff = b[:, :, :, None, :] - b[:, :, None, :, :]
        decay = jnp.exp(jnp.where(causal[:, :, None], diff, -jnp.inf))
        scores = jnp.einsum('bhtk,bhsk,bhtsk->bhts', qc, kc, decay)
        intra = jnp.einsum('bhts,bhsv->bhtv', scores, vc)
        inter = jnp.einsum('bhtk,bhkv->bhtv', qc * jnp.exp(b), state)
        b_last = b[:, :, -1, :]
        new_state = (jnp.exp(b_last)[..., None] * state
                     + jnp.einsum('bhsk,bhsv->bhkv', kc * jnp.exp(b_last[:, :, None, :] - b), vc))
        return new_state, intra + inter

    state0 = jnp.zeros((B, H, dk, dv), jnp.float32)
    _, out = lax.scan(step, state0, (to_chunks(q), to_chunks(k), to_chunks(v), to_chunks(log_f)))
    return jnp.moveaxis(out, 0, 2).reshape(B, H, S, dv)


def token_mixer(h, w_in, b_fox_forget, lower_bound, hgrn_norm_w, w_up_fox, w_up_hgrn, w_out):
    B, S, _ = h.shape
    proj = h @ w_in
    offsets = np.cumsum(IN_WIDTHS)[:-1].tolist()
    fq, fk, fv, ff, hq, hf, hi, hg, gate_fox, gate_hgrn = jnp.split(proj, offsets, axis=-1)

    def heads(a, n):
        return a.reshape(B, S, n, -1).transpose(0, 2, 1, 3)

    log_f_fox = jax.nn.log_sigmoid(ff.astype(jnp.float32) + b_fox_forget).transpose(0, 2, 1)
    y_fox = fox_attention(heads(fq, FOX_HEADS), heads(fk, FOX_HEADS), heads(fv, FOX_HEADS), log_f_fox)
    y_fox = y_fox.transpose(0, 2, 1, 3).reshape(B, S, FOX_WIDTH)

    f_h = lower_bound + (1.0 - lower_bound) * jax.nn.sigmoid(hf.astype(jnp.float32))
    q_h = jax.nn.silu(hq)
    o = hgrn2_chunked(heads(q_h, HGRN_HEADS), heads(1.0 - f_h, HGRN_HEADS),
                      heads(hi, HGRN_HEADS), heads(jnp.log(f_h), HGRN_HEADS))
    o = o.transpose(0, 2, 1, 3)
    o = o * lax.rsqrt(jnp.mean(jnp.square(o), axis=-1, keepdims=True) + RMS_EPS)
    o = (o.reshape(B, S, HGRN_HEADS * HGRN_VAL_DIM) * hgrn_norm_w).astype(h.dtype) * jax.nn.silu(hg)

    merged = jax.nn.sigmoid(gate_fox) * (y_fox @ w_up_fox) + jax.nn.sigmoid(gate_hgrn) * (o @ w_up_hgrn)
    return merged @ w_out


def hier_moe(h, w_rg, b_rg, w_re, b_re, w_g, w_u, w_d):
    B, S, D = h.shape
    t = h.reshape(-1, D)
    g_prob = jax.nn.softmax((t @ w_rg).astype(jnp.float32) + b_rg, axis=-1)
    g_w, g_idx = lax.top_k(g_prob, 1)
    e_logits = ((t @ w_re).astype(jnp.float32) + b_re).reshape(-1, N_GROUPS, EXPERTS_PER_GROUP)
    e_in_group = jnp.take_along_axis(e_logits, g_idx[:, :, None], axis=1)[:, 0]
    e_prob = jax.nn.softmax(e_in_group, axis=-1)
    e_w, e_idx = lax.top_k(e_prob, TOP_K_IN_GROUP)
    e_w = e_w / jnp.sum(e_w, axis=-1, keepdims=True) * g_w
    expert_id = g_idx * EXPERTS_PER_GROUP + e_idx
    combine = jnp.sum(jax.nn.one_hot(expert_id, N_EXPERTS, dtype=jnp.float32) * e_w[..., None], axis=1)

    out = jnp.zeros((t.shape[0], D), jnp.float32)
    for grp in range(N_GROUPS):
        sl = slice(grp * EXPERTS_PER_GROUP, (grp + 1) * EXPERTS_PER_GROUP)
        hid = (jax.nn.silu(jnp.einsum('td,edf->tef', t, w_g[sl]))
               * jnp.einsum('td,edf->tef', t, w_u[sl]))
        hid = hid * combine[:, sl, None].astype(hid.dtype)
        out = out + jnp.einsum('tef,efd->td', hid, w_d[sl]).astype(jnp.float32)
    return out.reshape(B, S, D).astype(h.dtype)


def setup_inputs(seed: int = 0) -> dict:
    key = jax.random.key(seed)
    ks = jax.random.split(key, 24)
    D, L = D_MODEL, DEPTH
    nrm = jax.random.normal
    col_scale = jnp.concatenate([
        jnp.full((FOX_WIDTH * 2,), 1.0), jnp.full((FOX_WIDTH,), BETA), jnp.full((FOX_HEADS,), 1.0),
        jnp.full((HGRN_WIDTH * 2,), 1.0), jnp.full((HGRN_HEADS * HGRN_VAL_DIM,), BETA),
        jnp.full((HGRN_HEADS * HGRN_VAL_DIM + 2 * D,), 1.0)])
    return {
        "x": nrm(ks[0], (BATCH, SEQ, D), jnp.float32),
        "c": nrm(ks[1], (BATCH, D), jnp.float32),
        "w_ada": nrm(ks[2], (L, D, 6 * D), jnp.float32) * (0.5 * D ** -0.5),
        "b_ada": nrm(ks[3], (L, 6 * D), jnp.float32) * 0.02,
        "w_in": nrm(ks[4], (L, D, IN_DIM), jnp.float32) * (D ** -0.5) * col_scale,
        "b_fox_forget": 2.0 + 0.5 * nrm(ks[5], (L, FOX_HEADS), jnp.float32),
        "hgrn_lb_logits": 0.5 * nrm(ks[6], (L + 1, HGRN_WIDTH), jnp.float32),
        "hgrn_norm_w": 1.0 + 0.05 * nrm(ks[7], (L, HGRN_HEADS * HGRN_VAL_DIM), jnp.float32),
        "w_up_fox": nrm(ks[8], (L, FOX_WIDTH, D), jnp.float32) * FOX_WIDTH ** -0.5,
        "w_up_hgrn": nrm(ks[9], (L, HGRN_HEADS * HGRN_VAL_DIM, D), jnp.float32) * (HGRN_HEADS * HGRN_VAL_DIM) ** -0.5,
        "w_out": nrm(ks[10], (L, D, D), jnp.float32) * (D ** -0.5) * BETA,
        "ln1_g": 1.0 + 0.05 * nrm(ks[11], (L, D), jnp.float32),
        "ln1_b": 0.02 * nrm(ks[12], (L, D), jnp.float32),
        "w_router_group": nrm(ks[13], (L, D, N_GROUPS), jnp.float32) * D ** -0.5,
        "b_router_group": 0.01 * nrm(ks[14], (L, N_GROUPS), jnp.float32),
        "w_router_expert": nrm(ks[15], (L, D, N_EXPERTS), jnp.float32) * D ** -0.5,
        "b_router_expert": 0.01 * nrm(ks[16], (L, N_EXPERTS), jnp.float32),
        "w_expert_gate": nrm(ks[17], (L, N_EXPERTS, D, EXPERT_FF), jnp.float32) * D ** -0.5,
        "w_expert_up": nrm(ks[18], (L, N_EXPERTS, D, EXPERT_FF), jnp.float32) * D ** -0.5,
        "w_expert_down": nrm(ks[19], (L, N_EXPERTS, EXPERT_FF, D), jnp.float32) * (EXPERT_FF ** -0.5) * BETA,
        "ln2_g": 1.0 + 0.05 * nrm(ks[20], (L, D), jnp.float32),
        "ln2_b": 0.02 * nrm(ks[21], (L, D), jnp.float32),
    }


def reference(x, c, w_ada, b_ada, w_in, b_fox_forget, hgrn_lb_logits, hgrn_norm_w, w_up_fox, w_up_hgrn,
              w_out, ln1_g, ln1_b, w_router_group, b_router_group, w_router_expert, b_router_expert,
              w_expert_gate, w_expert_up, w_expert_down, ln2_g, ln2_b):
    lower_bounds = jnp.cumsum(jax.nn.softmax(hgrn_lb_logits.astype(jnp.float32), axis=0), axis=0)
    c_act = jax.nn.silu(c)
    for l in range(DEPTH):
        ada = c_act @ w_ada[l] + b_ada[l]
        sh1, sc1, g1, sh2, sc2, g2 = jnp.split(ada, 6, axis=-1)
        h = x * (1.0 + sc1[:, None, :]) + sh1[:, None, :]
        y = token_mixer(h, w_in[l], b_fox_forget[l], lower_bounds[l], hgrn_norm_w[l],
                        w_up_fox[l], w_up_hgrn[l], w_out[l])
        x = layer_norm(ALPHA * x + g1[:, None, :] * y, ln1_g[l], ln1_b[l])
        h = x * (1.0 + sc2[:, None, :]) + sh2[:, None, :]
        y = hier_moe(h, w_router_group[l], b_router_group[l], w_router_expert[l], b_router_expert[l],
                     w_expert_gate[l], w_expert_up[l], w_expert_down[l])
        x = layer_norm(ALPHA * x + g2[:, None, :] * y, ln2_g[l], ln2_b[l])
    return x
```

```python
import functools

import jax
import jax.numpy as jnp
from jax import lax
from jax.experimental import pallas as pl
from jax.experimental.pallas import tpu as pltpu

F32 = jnp.float32
BF16 = jnp.bfloat16
HIGHEST = lax.Precision.HIGHEST

LANES = 128
HEAD_DIM = 64
LN_EPS = 1e-5
RMS_EPS = 1e-6
NEG_BIG = -1e30
HCHUNK = 16
VMEM_LIMIT = 56 * 1024 * 1024


def _cparams(sem, vmem=VMEM_LIMIT):
    return pltpu.CompilerParams(dimension_semantics=sem, vmem_limit_bytes=vmem)


def _sigmoid(x):
    return 1.0 / (1.0 + jnp.exp(-x))


def _silu(x):
    return x * _sigmoid(x)


def _ada_kernel(c_ref, w_ref, b_ref, o_ref):
    c = c_ref[...]
    o_ref[...] = jnp.dot(_silu(c), w_ref[...], precision=HIGHEST,
                         preferred_element_type=F32) + b_ref[...]


def _ada(c, w_ada, b_ada):
    B, D = c.shape
    N = w_ada.shape[1]
    tn = 1024
    return pl.pallas_call(
        _ada_kernel,
        out_shape=jax.ShapeDtypeStruct((B, N), F32),
        grid=(N // tn,),
        in_specs=[pl.BlockSpec((B, D), lambda j: (0, 0)),
                  pl.BlockSpec((D, tn), lambda j: (0, j)),
                  pl.BlockSpec((1, tn), lambda j: (0, j))],
        out_specs=pl.BlockSpec((B, tn), lambda j: (0, j)),
        compiler_params=_cparams(("arbitrary",)),
    )(c, w_ada, b_ada.reshape(1, N))


def _inproj_kernel(x_ref, sc_ref, sh_ref, w_ref,
                   fq_ref, fk_ref, fv_ref, ff_ref, hq_ref, hf_ref, hi_ref, hg_ref, gf_ref, gh_ref,
                   *, segs, q_scale):
    h = (x_ref[...] * (1.0 + sc_ref[...]) + sh_ref[...]).astype(BF16)
    outs = (fq_ref, fk_ref, fv_ref, ff_ref, hq_ref, hf_ref, hi_ref, hg_ref, gf_ref, gh_ref)
    for idx, (o_ref, (a, b)) in enumerate(zip(outs, segs)):
        r = jnp.dot(h, w_ref[:, a:b], preferred_element_type=F32)
        if idx == 0:
            r = r * q_scale
        o_ref[...] = r.astype(o_ref.dtype)


def _inproj(x, sc1, sh1, w_packed, segs, tm=256):
    B, S, D = x.shape
    widths = [b - a for a, b in segs]
    dtypes = [BF16, BF16, BF16, F32, BF16, F32, BF16, BF16, BF16, BF16]
    out_shape = tuple(jax.ShapeDtypeStruct((B, S, w), dt) for w, dt in zip(widths, dtypes))
    out_specs = tuple(pl.BlockSpec((None, tm, w), lambda b, i: (b, i, 0)) for w in widths)
    vec = pl.BlockSpec((None, 1, D), lambda b, i: (b, 0, 0))
    return pl.pallas_call(
        functools.partial(_inproj_kernel, segs=tuple(segs), q_scale=HEAD_DIM ** -0.5),
        out_shape=out_shape,
        grid=(B, S // tm),
        in_specs=[pl.BlockSpec((None, tm, D), lambda b, i: (b, i, 0)), vec, vec,
                  pl.BlockSpec(w_packed.shape, lambda b, i: (0, 0))],
        out_specs=out_specs,
        compiler_params=_cparams(("parallel", "parallel")),
    )(x, sc1, sh1, w_packed)


def _foxcum_kernel(ff_ref, b_ref, o_ref, *, nheads):
    z = ff_ref[...] + b_ref[...]
    lf = jnp.minimum(z, 0.0) - jnp.log(1.0 + jnp.exp(-jnp.abs(z)))
    lft = lf.T[:nheads, :]
    S = lft.shape[1]
    r = lax.broadcasted_iota(jnp.int32, (LANES, LANES), 0)
    c = lax.broadcasted_iota(jnp.int32, (LANES, LANES), 1)
    upper = (r <= c).astype(F32)
    carry = jnp.zeros((nheads, 1), F32)
    for j in range(S // LANES):
        blk = jnp.dot(lft[:, j * LANES:(j + 1) * LANES], upper, precision=HIGHEST,
                      preferred_element_type=F32) + carry
        o_ref[:, j * LANES:(j + 1) * LANES] = blk
        carry = blk[:, LANES - 1:LANES]


def _foxcum(ffp, bias_p, nheads):
    B, S, _ = ffp.shape
    return pl.pallas_call(
        functools.partial(_foxcum_kernel, nheads=nheads),
        out_shape=jax.ShapeDtypeStruct((B, nheads, S), F32),
        grid=(B,),
        in_specs=[pl.BlockSpec((None, S, LANES), lambda b: (b, 0, 0)),
                  pl.BlockSpec((1, LANES), lambda b: (0, 0))],
        out_specs=pl.BlockSpec((None, nheads, S), lambda b: (b, 0, 0)),
        compiler_params=_cparams(("parallel",)),
    )(ffp, bias_p)


def _fox_kernel(q_ref, k_ref, v_ref, c_ref, o_ref, *, tq):
    qi = pl.program_id(2)
    q2 = q_ref[...]
    lane = lax.broadcasted_iota(jnp.int32, (tq, LANES), 1)
    row = lax.broadcasted_iota(jnp.int32, (tq, tq), 0)
    col = lax.broadcasted_iota(jnp.int32, (tq, tq), 1)
    heads = []
    for h in range(2):
        in_head = (lane < HEAD_DIM) if h == 0 else (lane >= HEAD_DIM)
        qh = jnp.where(in_head, q2, jnp.zeros_like(q2))

        def step(j, carry, masked, qh=qh, h=h):
            m, l, acc = carry
            k0 = pl.multiple_of(j * tq, tq)
            ks = k_ref[pl.ds(k0, tq), :]
            vs = v_ref[pl.ds(k0, tq), :]
            s = lax.dot_general(qh, ks, (((1,), (1,)), ((), ())), preferred_element_type=F32)
            s = s - c_ref[h:h + 1, pl.ds(k0, tq)]
            if masked:
                s = jnp.where(row >= col, s, NEG_BIG)
            m_new = jnp.maximum(m, jnp.max(s, axis=1, keepdims=True))
            alpha = jnp.exp(m - m_new)
            p = jnp.exp(s - m_new)
            l = alpha * l + jnp.sum(p, axis=1, keepdims=True)
            acc = alpha * acc + jnp.dot(p.astype(BF16), vs, preferred_element_type=F32)
            return m_new, l, acc

        init = (jnp.full((tq, 1), NEG_BIG, F32), jnp.zeros((tq, 1), F32), jnp.zeros((tq, LANES), F32))
        carry = lax.fori_loop(0, qi, functools.partial(step, masked=False), init)
        _, l, acc = step(qi, carry, True)
        heads.append(acc / l)
    o_ref[...] = jnp.where(lane < HEAD_DIM, heads[0], heads[1]).astype(o_ref.dtype)


def _fox(fq, fk, fv, cum4, tq=256):
    B, S, W = fq.shape
    npairs = W // LANES
    return pl.pallas_call(
        functools.partial(_fox_kernel, tq=tq),
        out_shape=jax.ShapeDtypeStruct((B, S, W), BF16),
        grid=(B, npairs, S // tq),
        in_specs=[pl.BlockSpec((None, tq, LANES), lambda b, p, i: (b, i, p)),
                  pl.BlockSpec((None, S, LANES), lambda b, p, i: (b, 0, p)),
                  pl.BlockSpec((None, S, LANES), lambda b, p, i: (b, 0, p)),
                  pl.BlockSpec((None, None, 2, S), lambda b, p, i: (b, p, 0, 0))],
        out_specs=pl.BlockSpec((None, tq, LANES), lambda b, p, i: (b, i, p)),
        compiler_params=_cparams(("parallel", "parallel", "arbitrary")),
    )(fq, fk, fv, cum4)


def _hgrn_kernel(hq_ref, hf_ref, hi_ref, hg_ref, lb_ref, nw_ref, o_ref,
                 a_sc, qt_sc, kt_sc, kk_sc, qq_sc, p_sc, s_sc, o_sc, st_sc):
    S = hq_ref.shape[0]
    C = HCHUNK
    nchunks = S // C

    lg = lb_ref[...]
    e = jnp.exp(lg - jnp.max(lg, axis=0, keepdims=True))
    lb = e[0:1, :] / jnp.sum(e, axis=0, keepdims=True)

    f = lb + (1.0 - lb) * _sigmoid(hf_ref[...])
    lf = jnp.log(f)
    kk = 1.0 - f
    qq = _silu(hq_ref[...].astype(F32))

    rmod = lax.broadcasted_iota(jnp.int32, (S, LANES), 0) & (C - 1)
    a = lf
    d = 1
    while d < C:
        a = a + jnp.where(rmod >= d, pltpu.roll(a, d, axis=0), 0.0)
        d *= 2
    a3 = a.reshape(nchunks, C, LANES)
    alast = jnp.broadcast_to(a3[:, C - 1:C, :], (nchunks, C, LANES)).reshape(S, LANES)
    a_sc[...] = a
    kk_sc[...] = kk
    qq_sc[...] = qq
    qt_sc[...] = (qq * jnp.exp(a)).astype(BF16)
    kt_sc[...] = (kk * jnp.exp(alast - a)).astype(BF16)

    lane = lax.broadcasted_iota(jnp.int32, (C, LANES), 1)
    trow = lax.broadcasted_iota(jnp.int32, (C, LANES), 0)

    def gen(c, _):
        r0 = pl.multiple_of(c * C, C)
        ac = a_sc[pl.ds(r0, C), :]
        qc = qq_sc[pl.ds(r0, C), :]
        kc = kk_sc[pl.ds(r0, C), :]
        for s in range(C):
            dec = jnp.exp(jnp.minimum(ac - ac[s:s + 1, :], 0.0))
            p = jnp.where(trow >= s, qc * (kc[s:s + 1, :] * dec), 0.0)
            p_sc[pl.ds(r0, C), s * LANES:(s + 1) * LANES] = p.astype(BF16)
        return 0

    lax.fori_loop(0, nchunks, gen, 0)

    er = lax.broadcasted_iota(jnp.int32, (C * LANES, LANES), 0)
    ec = lax.broadcasted_iota(jnp.int32, (C * LANES, LANES), 1)
    emat = (ec == ((er & (LANES - 1)) // HEAD_DIM) * C + er // LANES).astype(BF16)
    rb = 256

    def red(i, _):
        r0 = pl.multiple_of(i * rb, rb)
        s_sc[pl.ds(r0, rb), :] = jnp.dot(p_sc[pl.ds(r0, rb), :], emat, preferred_element_type=F32)
        return 0

    lax.fori_loop(0, S // rb, red, 0)

    sr = lax.broadcasted_iota(jnp.int32, (LANES, LANES), 0)
    scn = lax.broadcasted_iota(jnp.int32, (LANES, LANES), 1)
    same_head = (sr // HEAD_DIM) == (scn // HEAD_DIM)
    st_sc[...] = jnp.zeros((LANES, LANES), F32)

    def rec(c, _):
        r0 = pl.multiple_of(c * C, C)
        st = st_sc[...]
        vc = hi_ref[pl.ds(r0, C), :]
        o_inter = lax.dot_general(qt_sc[pl.ds(r0, C), :], st.astype(BF16),
                                  (((1,), (1,)), ((), ())), preferred_element_type=F32)
        sc = s_sc[pl.ds(r0, C), :][:, :2 * C].astype(BF16)
        v2 = jnp.concatenate([jnp.where(lane < HEAD_DIM, vc, jnp.zeros_like(vc)),
                              jnp.where(lane >= HEAD_DIM, vc, jnp.zeros_like(vc))], axis=0)
        o_intra = jnp.dot(sc, v2, preferred_element_type=F32)
        o_sc[pl.ds(r0, C), :] = o_inter + o_intra
        upd = lax.dot_general(vc, kt_sc[pl.ds(r0, C), :], (((0,), (0,)), ((), ())),
                              preferred_element_type=F32)
        dec = jnp.exp(a_sc[pl.ds(r0 + C - 1, 1), :])
        st_sc[...] = jnp.where(same_head, st * dec + upd, 0.0)
        return 0

    lax.fori_loop(0, nchunks, rec, 0)

    o = o_sc[...]
    ones_head = jnp.where(same_head, 1.0 / HEAD_DIM, 0.0).astype(F32)
    ms = jnp.dot(o * o, ones_head, precision=HIGHEST, preferred_element_type=F32)
    y = o * lax.rsqrt(ms + RMS_EPS) * nw_ref[...]
    o_ref[...] = (y * _silu(hg_ref[...].astype(F32))).astype(o_ref.dtype)


def _hgrn(hq, hf, hi, hg, lb_logits, norm_w):
    B, S, W = hq.shape
    npairs = W // LANES
    nrows = lb_logits.shape[0]
    seq = pl.BlockSpec((None, S, LANES), lambda b, p: (b, 0, p))
    return pl.pallas_call(
        _hgrn_kernel,
        out_shape=jax.ShapeDtypeStruct((B, S, W), BF16),
        grid=(B, npairs),
        in_specs=[seq, seq, seq, seq,
                  pl.BlockSpec((nrows, LANES), lambda b, p: (0, p)),
                  pl.BlockSpec((1, LANES), lambda b, p: (0, p))],
        out_specs=seq,
        scratch_shapes=[pltpu.VMEM((S, LANES), F32),
                        pltpu.VMEM((S, LANES), BF16),
                        pltpu.VMEM((S, LANES), BF16),
                        pltpu.VMEM((S, LANES), F32),
                        pltpu.VMEM((S, LANES), F32),
                        pltpu.VMEM((S, HCHUNK * LANES), BF16),
                        pltpu.VMEM((S, LANES), F32),
                        pltpu.VMEM((S, LANES), F32),
                        pltpu.VMEM((LANES, LANES), F32)],
        compiler_params=_cparams(("parallel", "parallel")),
    )(hq, hf, hi, hg, lb_logits, norm_w.reshape(1, W))


def _layer_norm(v, g, b):
    mu = jnp.mean(v, axis=-1, keepdims=True)
    d = v - mu
    var = jnp.mean(d * d, axis=-1, keepdims=True)
    return d * lax.rsqrt(var + LN_EPS) * g + b


def _mix_kernel(yf_ref, oh_ref, gf_ref, gh_ref, x_ref, g1_ref, sc2_ref, sh2_ref,
                wuf_ref, wuh_ref, wo_ref, lg_ref, lbias_ref, wr_ref, br_ref,
                x1_ref, h2_ref, ri_ref, cnt_ref, carry_sc, *, alpha, ngroups, nper):
    first = (pl.program_id(0) == 0) & (pl.program_id(1) == 0)

    @pl.when(first)
    def _():
        carry_sc[...] = jnp.zeros_like(carry_sc)

    tm = x_ref.shape[0]
    yf = jnp.dot(yf_ref[...], wuf_ref[...], preferred_element_type=F32)
    yh = jnp.dot(oh_ref[...], wuh_ref[...], preferred_element_type=F32)
    merged = _sigmoid(gf_ref[...].astype(F32)) * yf + _sigmoid(gh_ref[...].astype(F32)) * yh
    y = jnp.dot(merged.astype(BF16), wo_ref[...], preferred_element_type=F32)
    x1 = _layer_norm(alpha * x_ref[...] + g1_ref[...] * y, lg_ref[...], lbias_ref[...])
    x1_ref[...] = x1
    h2 = x1 * (1.0 + sc2_ref[...]) + sh2_ref[...]
    h2_ref[...] = h2

    logits = jnp.dot(h2, wr_ref[...], precision=HIGHEST, preferred_element_type=F32) + br_ref[...]
    lane = lax.broadcasted_iota(jnp.int32, (tm, LANES), 1)
    big = jnp.int32(1 << 20)

    def argmax_first(vals, mask):
        mx = jnp.max(jnp.where(mask, vals, -jnp.inf), axis=1, keepdims=True)
        idx = jnp.min(jnp.where(mask & (vals == mx), lane, big), axis=1, keepdims=True)
        return mx, idx

    gmask = lane < ngroups
    gmax = jnp.max(jnp.where(gmask, logits, -jnp.inf), axis=1, keepdims=True)
    gexp = jnp.where(gmask, jnp.exp(logits - gmax), 0.0)
    gprob = gexp / jnp.sum(gexp, axis=1, keepdims=True)
    g_w, g_idx = argmax_first(gprob, gmask)

    lo = ngroups + g_idx * nper
    emask = (lane >= lo) & (lane < lo + nper)
    emax = jnp.max(jnp.where(emask, logits, -jnp.inf), axis=1, keepdims=True)
    eexp = jnp.where(emask, jnp.exp(logits - emax), 0.0)
    eprob = eexp / jnp.sum(eexp, axis=1, keepdims=True)
    p0, i0 = argmax_first(eprob, emask)
    p1, i1 = argmax_first(eprob, emask & (lane != i0))
    den = p0 + p1
    w0 = p0 / den * g_w
    w1 = p1 / den * g_w
    e0 = i0 - ngroups
    e1 = i1 - ngroups

    oh = ((lane == e0) | (lane == e1)).astype(F32)
    r = lax.broadcasted_iota(jnp.int32, (tm, tm), 0)
    c = lax.broadcasted_iota(jnp.int32, (tm, tm), 1)
    strict_lower = (c < r).astype(BF16)
    before = jnp.dot(strict_lower, oh.astype(BF16), preferred_element_type=F32) + carry_sc[...]
    rank0 = jnp.sum(jnp.where(lane == e0, before, 0.0), axis=1, keepdims=True)
    rank1 = jnp.sum(jnp.where(lane == e1, before, 0.0), axis=1, keepdims=True)
    carry_sc[...] = carry_sc[...] + jnp.sum(oh, axis=0, keepdims=True)
    cnt_ref[...] = carry_sc[...]

    info = jnp.where(lane == 0, w0, 0.0)
    info = jnp.where(lane == 1, w1, info)
    info = jnp.where(lane == 2, e0.astype(F32), info)
    info = jnp.where(lane == 3, e1.astype(F32), info)
    info = jnp.where(lane == 4, rank0, info)
    info = jnp.where(lane == 5, rank1, info)
    ri_ref[...] = info


def _mix(yf, oh, gf, gh, x, g1, sc2, sh2, wuf, wuh, wo, ln_g, ln_b, wr, br, alpha, ngroups, nper, tm=256):
    B, S, D = x.shape
    W = yf.shape[2]
    tok = lambda w: pl.BlockSpec((None, tm, w), lambda b, i: (b, i, 0))
    vec = pl.BlockSpec((None, 1, D), lambda b, i: (b, 0, 0))
    full = lambda a: pl.BlockSpec(a.shape, lambda b, i: (0,) * a.ndim)
    return pl.pallas_call(
        functools.partial(_mix_kernel, alpha=alpha, ngroups=ngroups, nper=nper),
        out_shape=(jax.ShapeDtypeStruct((B, S, D), F32),
                   jax.ShapeDtypeStruct((B, S, D), F32),
                   jax.ShapeDtypeStruct((B, S, LANES), F32),
                   jax.ShapeDtypeStruct((1, LANES), F32)),
        grid=(B, S // tm),
        in_specs=[tok(W), tok(W), tok(D), tok(D), tok(D), vec, vec, vec,
                  full(wuf), full(wuh), full(wo), full(ln_g), full(ln_b), full(wr), full(br)],
        out_specs=(tok(D), tok(D), tok(LANES), pl.BlockSpec((1, LANES), lambda b, i: (0, 0))),
        scratch_shapes=[pltpu.VMEM((1, LANES), F32)],
        compiler_params=_cparams(("arbitrary", "arbitrary")),
    )(yf, oh, gf, gh, x, g1, sc2, sh2, wuf, wuh, wo, ln_g, ln_b, wr, br)


def _experts_kernel(te_ref, tv_ref, src_ref, h_hbm, wg_ref, wu_ref, wd_ref, o_ref, xbuf, sem, *, tm):
    i = pl.program_id(0)

    @pl.when(tv_ref[i] == 0)
    def _():
        o_ref[...] = jnp.zeros_like(o_ref)

    @pl.when(tv_ref[i] != 0)
    def _():
        base = i * tm

        def issue(r, _):
            tok = src_ref[base + r]
            pltpu.make_async_copy(h_hbm.at[pl.ds(tok, 1), :], xbuf.at[pl.ds(r, 1), :], sem).start()
            return 0

        lax.fori_loop(0, tm, issue, 0)
        pltpu.make_async_copy(h_hbm.at[pl.ds(0, tm), :], xbuf, sem).wait()
        x = xbuf[...].astype(BF16)
        g = jnp.dot(x, wg_ref[...].astype(BF16), preferred_element_type=F32)
        u = jnp.dot(x, wu_ref[...].astype(BF16), preferred_element_type=F32)
        hid = (_silu(g) * u).astype(BF16)
        o_ref[...] = jnp.dot(hid, wd_ref[...].astype(BF16), preferred_element_type=F32)


def _experts(tile_expert, tile_valid, src_tok, h2, wg, wu, wd, tm):
    T, D = h2.shape
    E, _, FF = wg.shape
    ntiles = tile_expert.shape[0]
    grid_spec = pltpu.PrefetchScalarGridSpec(
        num_scalar_prefetch=3,
        grid=(ntiles,),
        in_specs=[pl.BlockSpec(memory_space=pl.ANY),
                  pl.BlockSpec((None, D, FF), lambda i, te, tv, src: (te[i], 0, 0)),
                  pl.BlockSpec((None, D, FF), lambda i, te, tv, src: (te[i], 0, 0)),
                  pl.BlockSpec((None, FF, D), lambda i, te, tv, src: (te[i], 0, 0))],
        out_specs=pl.BlockSpec((tm, D), lambda i, te, tv, src: (i, 0)),
        scratch_shapes=[pltpu.VMEM((tm, D), F32), pltpu.SemaphoreType.DMA(())],
    )
    return pl.pallas_call(
        functools.partial(_experts_kernel, tm=tm),
        out_shape=jax.ShapeDtypeStruct((ntiles * tm, D), F32),
        grid_spec=grid_spec,
        compiler_params=_cparams(("arbitrary",)),
    )(tile_expert, tile_valid, src_tok, h2, wg, wu, wd)


def _combine_kernel(pos_ref, ys_hbm, x1_ref, ri_ref, g2_ref, lg_ref, lb_ref, o_ref, ybuf, sem, *, alpha, tm, tiles_per_b):
    t0 = (pl.program_id(0) * tiles_per_b + pl.program_id(1)) * tm

    def issue(r, _):
        for k in range(2):
            p = pos_ref[(t0 + r) * 2 + k]
            pltpu.make_async_copy(ys_hbm.at[pl.ds(p, 1), :], ybuf.at[k, pl.ds(r, 1), :], sem).start()
        return 0

    lax.fori_loop(0, tm, issue, 0)
    for k in range(2):
        pltpu.make_async_copy(ys_hbm.at[pl.ds(0, tm), :], ybuf.at[k], sem).wait()
    ri = ri_ref[...]
    y = ri[:, 0:1] * ybuf[0] + ri[:, 1:2] * ybuf[1]
    o_ref[...] = _layer_norm(alpha * x1_ref[...] + g2_ref[...] * y, lg_ref[...], lb_ref[...])


def _combine(pos, ys, x1, rinfo, g2, ln_g, ln_b, alpha, tm=256):
    B, S, D = x1.shape
    grid_spec = pltpu.PrefetchScalarGridSpec(
        num_scalar_prefetch=1,
        grid=(B, S // tm),
        in_specs=[pl.BlockSpec(memory_space=pl.ANY),
                  pl.BlockSpec((None, tm, D), lambda b, i, pos: (b, i, 0)),
                  pl.BlockSpec((None, tm, LANES), lambda b, i, pos: (b, i, 0)),
                  pl.BlockSpec((None, 1, D), lambda b, i, pos: (b, 0, 0)),
                  pl.BlockSpec((1, D), lambda b, i, pos: (0, 0)),
                  pl.BlockSpec((1, D), lambda b, i, pos: (0, 0))],
        out_specs=pl.BlockSpec((None, tm, D), lambda b, i, pos: (b, i, 0)),
        scratch_shapes=[pltpu.VMEM((2, tm, D), F32), pltpu.SemaphoreType.DMA(())],
    )
    return pl.pallas_call(
        functools.partial(_combine_kernel, alpha=alpha, tm=tm, tiles_per_b=S // tm),
        out_shape=jax.ShapeDtypeStruct((B, S, D), F32),
        grid_spec=grid_spec,
        compiler_params=_cparams(("arbitrary", "arbitrary")),
    )(pos, ys, x1, rinfo, g2, ln_g, ln_b)


def kernel(x, c, w_ada, b_ada, w_in, b_fox_forget, hgrn_lb_logits, hgrn_norm_w, w_up_fox, w_up_hgrn, w_out,
           ln1_g, ln1_b, w_router_group, b_router_group, w_router_expert, b_router_expert,
           w_expert_gate, w_expert_up, w_expert_down, ln2_g, ln2_b):
    B, S, D = x.shape
    depth = w_ada.shape[0]
    assert depth == 1, "single-layer block"
    fox_heads = b_fox_forget.shape[1]
    fox_w = fox_heads * HEAD_DIM
    hgrn_w = hgrn_norm_w.shape[1]
    ngroups = w_router_group.shape[2]
    nexp = w_router_expert.shape[2]
    nper = nexp // ngroups
    alpha = (2 * depth) ** 0.25
    T = B * S

    ada = _ada(c, w_ada[0], b_ada[0])
    sh1, sc1, g1, sh2, sc2, g2 = [a.reshape(B, 1, D) for a in jnp.split(ada, 6, axis=-1)]

    wi = w_in[0]
    o_ff = 3 * fox_w
    w_packed = jnp.concatenate(
        [wi[:, :o_ff + fox_heads], jnp.zeros((D, LANES - fox_heads), wi.dtype), wi[:, o_ff + fox_heads:]],
        axis=1).astype(BF16)
    widths = [fox_w, fox_w, fox_w, LANES, hgrn_w, hgrn_w, hgrn_w, hgrn_w, D, D]
    segs, off = [], 0
    for w in widths:
        segs.append((off, off + w))
        off += w
    fq, fk, fv, ffp, hq, hf, hi, hg, gf, gh = _inproj(x, sc1, sh1, w_packed, segs)

    bias_p = jnp.zeros((1, LANES), F32).at[0, :fox_heads].set(b_fox_forget[0])
    cum = _foxcum(ffp, bias_p, fox_heads)
    y_fox = _fox(fq, fk, fv, cum.reshape(B, fox_heads // 2, 2, S))

    o_h = _hgrn(hq, hf, hi, hg, hgrn_lb_logits, hgrn_norm_w[0:1].reshape(1, hgrn_w)[0])

    wr = jnp.zeros((D, LANES), F32).at[:, :ngroups].set(w_router_group[0]).at[:, ngroups:ngroups + nexp].set(
        w_router_expert[0])
    br = jnp.zeros((1, LANES), F32).at[0, :ngroups].set(b_router_group[0]).at[0, ngroups:ngroups + nexp].set(
        b_router_expert[0])
    x1, h2, rinfo, counts = _mix(
        y_fox, o_h, gf, gh, x, g1, sc2, sh2,
        w_up_fox[0].astype(BF16), w_up_hgrn[0].astype(BF16), w_out[0].astype(BF16),
        ln1_g[0].reshape(1, D), ln1_b[0].reshape(1, D), wr, br, alpha, ngroups, nper)

    tm_e = 256
    ntiles = (2 * T) // tm_e + nexp
    cnt = counts[0, :nexp].astype(jnp.int32)
    padded = ((cnt + tm_e - 1) // tm_e) * tm_e
    ends = jnp.cumsum(padded)
    starts = ends - padded
    ri = rinfo.reshape(T, LANES)
    eid = ri[:, 2:4].astype(jnp.int32)
    rank = ri[:, 4:6].astype(jnp.int32)
    pos = (starts[eid] + rank).reshape(-1)
    tile_start = jnp.arange(ntiles, dtype=jnp.int32) * tm_e
    tile_expert = jnp.minimum(jnp.searchsorted(ends, tile_start, side="right"), nexp - 1).astype(jnp.int32)
    tile_valid = (tile_start < ends[-1]).astype(jnp.int32)
    tok_ids = jnp.repeat(jnp.arange(T, dtype=jnp.int32), 2)
    src_tok = jnp.zeros((ntiles * tm_e,), jnp.int32).at[pos].set(tok_ids)

    ys = _experts(tile_expert, tile_valid, src_tok, h2.reshape(T, D),
                  w_expert_gate[0], w_expert_up[0], w_expert_down[0], tm_e)
    return _combine(pos, ys, x1, rinfo, g2, ln2_g[0].reshape(1, D), ln2_b[0].reshape(1, D), alpha)
```

```python
import functools

import jax
import jax.numpy as jnp
from jax import lax
from jax.experimental import pallas as pl
from jax.experimental.pallas import tpu as pltpu

F32 = jnp.float32
BF16 = jnp.bfloat16
HIGHEST = lax.Precision.HIGHEST

LANES = 128
HEAD_DIM = 64
LN_EPS = 1e-5
RMS_EPS = 1e-6
NEG_BIG = -1e30
HCHUNK = 16
VMEM_LIMIT = 56 * 1024 * 1024


def _cparams(sem, vmem=VMEM_LIMIT):
    return pltpu.CompilerParams(dimension_semantics=sem, vmem_limit_bytes=vmem)


def _sigmoid(x):
    return 1.0 / (1.0 + jnp.exp(-x))


def _silu(x):
    return x * _sigmoid(x)


def _ada_kernel(c_ref, w_ref, b_ref, o_ref):
    c = c_ref[...]
    o_ref[...] = jnp.dot(_silu(c), w_ref[...], precision=HIGHEST,
                         preferred_element_type=F32) + b_ref[...]


def _ada(c, w_ada, b_ada):
    B, D = c.shape
    N = w_ada.shape[1]
    tn = 1024
    return pl.pallas_call(
        _ada_kernel,
        out_shape=jax.ShapeDtypeStruct((B, N), F32),
        grid=(N // tn,),
        in_specs=[pl.BlockSpec((B, D), lambda j: (0, 0)),
                  pl.BlockSpec((D, tn), lambda j: (0, j)),
                  pl.BlockSpec((1, tn), lambda j: (0, j))],
        out_specs=pl.BlockSpec((B, tn), lambda j: (0, j)),
        compiler_params=_cparams(("arbitrary",)),
    )(c, w_ada, b_ada.reshape(1, N))


def _inproj_kernel(x_ref, sc_ref, sh_ref, w_ref,
                   fq_ref, fk_ref, fv_ref, ff_ref, hq_ref, hf_ref, hi_ref, hg_ref, gf_ref, gh_ref,
                   *, segs, q_scale):
    h = (x_ref[...] * (1.0 + sc_ref[...]) + sh_ref[...]).astype(BF16)
    outs = (fq_ref, fk_ref, fv_ref, ff_ref, hq_ref, hf_ref, hi_ref, hg_ref, gf_ref, gh_ref)
    for idx, (o_ref, (a, b)) in enumerate(zip(outs, segs)):
        r = jnp.dot(h, w_ref[:, a:b], preferred_element_type=F32)
        if idx == 0:
            r = r * q_scale
        o_ref[...] = r.astype(o_ref.dtype)


def _inproj(x, sc1, sh1, w_packed, segs, tm=256):
    B, S, D = x.shape
    widths = [b - a for a, b in segs]
    dtypes = [BF16, BF16, BF16, F32, BF16, F32, BF16, BF16, BF16, BF16]
    out_shape = tuple(jax.ShapeDtypeStruct((B, S, w), dt) for w, dt in zip(widths, dtypes))
    out_specs = tuple(pl.BlockSpec((None, tm, w), lambda b, i: (b, i, 0)) for w in widths)
    vec = pl.BlockSpec((None, 1, D), lambda b, i: (b, 0, 0))
    return pl.pallas_call(
        functools.partial(_inproj_kernel, segs=tuple(segs), q_scale=HEAD_DIM ** -0.5),
        out_shape=out_shape,
        grid=(B, S // tm),
        in_specs=[pl.BlockSpec((None, tm, D), lambda b, i: (b, i, 0)), vec, vec,
                  pl.BlockSpec(w_packed.shape, lambda b, i: (0, 0))],
        out_specs=out_specs,
        compiler_params=_cparams(("parallel", "parallel")),
    )(x, sc1, sh1, w_packed)


def _foxcum_kernel(ff_ref, b_ref, o_ref, *, nheads):
    z = ff_ref[...] + b_ref[...]
    lf = jnp.minimum(z, 0.0) - jnp.log(1.0 + jnp.exp(-jnp.abs(z)))
    lft = lf.T[:nheads, :]
    S = lft.shape[1]
    r = lax.broadcasted_iota(jnp.int32, (LANES, LANES), 0)
    c = lax.broadcasted_iota(jnp.int32, (LANES, LANES), 1)
    upper = (r <= c).astype(F32)
    carry = jnp.zeros((nheads, 1), F32)
    for j in range(S // LANES):
        blk = jnp.dot(lft[:, j * LANES:(j + 1) * LANES], upper, precision=HIGHEST,
                      preferred_element_type=F32) + carry
        o_ref[:, j * LANES:(j + 1) * LANES] = blk
        carry = blk[:, LANES - 1:LANES]


def _foxcum(ffp, bias_p, nheads):
    B, S, _ = ffp.shape
    return pl.pallas_call(
        functools.partial(_foxcum_kernel, nheads=nheads),
        out_shape=jax.ShapeDtypeStruct((B, nheads, S), F32),
        grid=(B,),
        in_specs=[pl.BlockSpec((None, S, LANES), lambda b: (b, 0, 0)),
                  pl.BlockSpec((1, LANES), lambda b: (0, 0))],
        out_specs=pl.BlockSpec((None, nheads, S), lambda b: (b, 0, 0)),
        compiler_params=_cparams(("parallel",)),
    )(ffp, bias_p)


def _fox_kernel(q_ref, k_ref, v_ref, c_ref, o_ref, *, tq):
    qi = pl.program_id(2)
    q2 = q_ref[...]
    lane = lax.broadcasted_iota(jnp.int32, (tq, LANES), 1)
    row = lax.broadcasted_iota(jnp.int32, (tq, tq), 0)
    col = lax.broadcasted_iota(jnp.int32, (tq, tq), 1)
    heads = []
    for h in range(2):
        in_head = (lane < HEAD_DIM) if h == 0 else (lane >= HEAD_DIM)
        qh = jnp.where(in_head, q2, jnp.zeros_like(q2))

        def step(j, carry, masked, qh=qh, h=h):
            m, l, acc = carry
            k0 = pl.multiple_of(j * tq, tq)
            ks = k_ref[pl.ds(k0, tq), :]
            vs = v_ref[pl.ds(k0, tq), :]
            s = lax.dot_general(qh, ks, (((1,), (1,)), ((), ())), preferred_element_type=F32)
            s = s - c_ref[h:h + 1, pl.ds(k0, tq)]
            if masked:
                s = jnp.where(row >= col, s, NEG_BIG)
            m_new = jnp.maximum(m, jnp.max(s, axis=1, keepdims=True))
            alpha = jnp.exp(m - m_new)
            p = jnp.exp(s - m_new)
            l = alpha * l + jnp.sum(p, axis=1, keepdims=True)
            acc = alpha * acc + jnp.dot(p.astype(BF16), vs, preferred_element_type=F32)
            return m_new, l, acc

        init = (jnp.full((tq, 1), NEG_BIG, F32), jnp.zeros((tq, 1), F32), jnp.zeros((tq, LANES), F32))
        carry = lax.fori_loop(0, qi, functools.partial(step, masked=False), init)
        _, l, acc = step(qi, carry, True)
        heads.append(acc / l)
    o_ref[...] = jnp.where(lane < HEAD_DIM, heads[0], heads[1]).astype(o_ref.dtype)


def _fox(fq, fk, fv, cum4, tq=256):
    B, S, W = fq.shape
    npairs = W // LANES
    return pl.pallas_call(
        functools.partial(_fox_kernel, tq=tq),
        out_shape=jax.ShapeDtypeStruct((B, S, W), BF16),
        grid=(B, npairs, S // tq),
        in_specs=[pl.BlockSpec((None, tq, LANES), lambda b, p, i: (b, i, p)),
                  pl.BlockSpec((None, S, LANES), lambda b, p, i: (b, 0, p)),
                  pl.BlockSpec((None, S, LANES), lambda b, p, i: (b, 0, p)),
                  pl.BlockSpec((None, None, 2, S), lambda b, p, i: (b, p, 0, 0))],
        out_specs=pl.BlockSpec((None, tq, LANES), lambda b, p, i: (b, i, p)),
        compiler_params=_cparams(("parallel", "parallel", "arbitrary")),
    )(fq, fk, fv, cum4)


def _hgrn_kernel(hq_ref, hf_ref, hi_ref, hg_ref, lb_ref, nw_ref, o_ref,
                 a_sc, qt_sc, kt_sc, kk_sc, qq_sc, p_sc, s_sc, o_sc, st_sc):
    S = hq_ref.shape[0]
    C = HCHUNK
    nchunks = S // C

    lg = lb_ref[...]
    e = jnp.exp(lg - jnp.max(lg, axis=0, keepdims=True))
    lb = e[0:1, :] / jnp.sum(e, axis=0, keepdims=True)

    f = lb + (1.0 - lb) * _sigmoid(hf_ref[...])
    lf = jnp.log(f)
    kk = 1.0 - f
    qq = _silu(hq_ref[...].astype(F32))

    rmod = lax.broadcasted_iota(jnp.int32, (S, LANES), 0) & (C - 1)
    a = lf
    d = 1
    while d < C:
        a = a + jnp.where(rmod >= d, pltpu.roll(a, d, axis=0), 0.0)
        d *= 2
    a3 = a.reshape(nchunks, C, LANES)
    alast = jnp.broadcast_to(a3[:, C - 1:C, :], (nchunks, C, LANES)).reshape(S, LANES)
    a_sc[...] = a
    kk_sc[...] = kk
    qq_sc[...] = qq
    qt_sc[...] = (qq * jnp.exp(a)).astype(BF16)
    kt_sc[...] = (kk * jnp.exp(alast - a)).astype(BF16)

    lane = lax.broadcasted_iota(jnp.int32, (C, LANES), 1)
    trow = lax.broadcasted_iota(jnp.int32, (C, LANES), 0)

    def gen(c, _):
        r0 = pl.multiple_of(c * C, C)
        ac = a_sc[pl.ds(r0, C), :]
        qc = qq_sc[pl.ds(r0, C), :]
        kc = kk_sc[pl.ds(r0, C), :]
        for s in range(C):
            dec = jnp.exp(jnp.minimum(ac - ac[s:s + 1, :], 0.0))
            p = jnp.where(trow >= s, qc * (kc[s:s + 1, :] * dec), 0.0)
            p_sc[pl.ds(r0, C), s * LANES:(s + 1) * LANES] = p.astype(BF16)
        return 0

    lax.fori_loop(0, nchunks, gen, 0)

    er = lax.broadcasted_iota(jnp.int32, (C * LANES, LANES), 0)
    ec = lax.broadcasted_iota(jnp.int32, (C * LANES, LANES), 1)
    emat = (ec == ((er & (LANES - 1)) // HEAD_DIM) * C + er // LANES).astype(BF16)
    rb = 256

    def red(i, _):
        r0 = pl.multiple_of(i * rb, rb)
        s_sc[pl.ds(r0, rb), :] = jnp.dot(p_sc[pl.ds(r0, rb), :], emat, preferred_element_type=F32)
        return 0

    lax.fori_loop(0, S // rb, red, 0)

    sr = lax.broadcasted_iota(jnp.int32, (LANES, LANES), 0)
    scn = lax.broadcasted_iota(jnp.int32, (LANES, LANES), 1)
    same_head = (sr // HEAD_DIM) == (scn // HEAD_DIM)
    st_sc[...] = jnp.zeros((LANES, LANES), F32)

    def rec(c, _):
        r0 = pl.multiple_of(c * C, C)
        st = st_sc[...]
        vc = hi_ref[pl.ds(r0, C), :]
        o_inter = lax.dot_general(qt_sc[pl.ds(r0, C), :], st.astype(BF16),
                                  (((1,), (1,)), ((), ())), preferred_element_type=F32)
        sc = s_sc[pl.ds(r0, C), :][:, :2 * C].astype(BF16)
        v2 = jnp.concatenate([jnp.where(lane < HEAD_DIM, vc, jnp.zeros_like(vc)),
                              jnp.where(lane >= HEAD_DIM, vc, jnp.zeros_like(vc))], axis=0)
        o_intra = jnp.dot(sc, v2, preferred_element_type=F32)
        o_sc[pl.ds(r0, C), :] = o_inter + o_intra
        upd = lax.dot_general(vc, kt_sc[pl.ds(r0, C), :], (((0,), (0,)), ((), ())),
                              preferred_element_type=F32)
        dec = jnp.exp(a_sc[pl.ds(r0 + C - 1, 1), :])
        st_sc[...] = jnp.where(same_head, st * dec + upd, 0.0)
        return 0

    lax.fori_loop(0, nchunks, rec, 0)

    o = o_sc[...]
    ones_head = jnp.where(same_head, 1.0 / HEAD_DIM, 0.0).astype(F32)
    ms = jnp.dot(o * o, ones_head, precision=HIGHEST, preferred_element_type=F32)
    y = o * lax.rsqrt(ms + RMS_EPS) * nw_ref[...]
    o_ref[...] = (y * _silu(hg_ref[...].astype(F32))).astype(o_ref.dtype)


def _hgrn(hq, hf, hi, hg, lb_logits, norm_w):
    B, S, W = hq.shape
    npairs = W // LANES
    nrows = lb_logits.shape[0]
    seq = pl.BlockSpec((None, S, LANES), lambda b, p: (b, 0, p))
    return pl.pallas_call(
        _hgrn_kernel,
        out_shape=jax.ShapeDtypeStruct((B, S, W), BF16),
        grid=(B, npairs),
        in_specs=[seq, seq, seq, seq,
                  pl.BlockSpec((nrows, LANES), lambda b, p: (0, p)),
                  pl.BlockSpec((1, LANES), lambda b, p: (0, p))],
        out_specs=seq,
        scratch_shapes=[pltpu.VMEM((S, LANES), F32),
                        pltpu.VMEM((S, LANES), BF16),
                        pltpu.VMEM((S, LANES), BF16),
                        pltpu.VMEM((S, LANES), F32),
                        pltpu.VMEM((S, LANES), F32),
                        pltpu.VMEM((S, HCHUNK * LANES), BF16),
                        pltpu.VMEM((S, LANES), F32),
                        pltpu.VMEM((S, LANES), F32),
                        pltpu.VMEM((LANES, LANES), F32)],
        compiler_params=_cparams(("parallel", "parallel")),
    )(hq, hf, hi, hg, lb_logits, norm_w.reshape(1, W))


def _layer_norm(v, g, b):
    mu = jnp.mean(v, axis=-1, keepdims=True)
    d = v - mu
    var = jnp.mean(d * d, axis=-1, keepdims=True)
    return d * lax.rsqrt(var + LN_EPS) * g + b


def _store_row_tiles(ref, val):
    for j in range(val.shape[1] // LANES):
        ref[:, j, :] = val[:, j * LANES:(j + 1) * LANES]


def _load_row_tiles(ref):
    return jnp.concatenate([ref[:, j, :] for j in range(ref.shape[1])], axis=1)


def _mix_kernel(yf_ref, oh_ref, gf_ref, gh_ref, x_ref, g1_ref, sc2_ref, sh2_ref,
                wuf_ref, wuh_ref, wo_ref, lg_ref, lbias_ref, wr_ref, br_ref,
                x1_ref, h2_ref, ri_ref, cnt_ref, carry_sc, *, alpha, ngroups, nper):
    first = (pl.program_id(0) == 0) & (pl.program_id(1) == 0)

    @pl.when(first)
    def _():
        carry_sc[...] = jnp.zeros_like(carry_sc)

    tm = x_ref.shape[0]
    yf = jnp.dot(yf_ref[...], wuf_ref[...], preferred_element_type=F32)
    yh = jnp.dot(oh_ref[...], wuh_ref[...], preferred_element_type=F32)
    merged = _sigmoid(gf_ref[...].astype(F32)) * yf + _sigmoid(gh_ref[...].astype(F32)) * yh
    y = jnp.dot(merged.astype(BF16), wo_ref[...], preferred_element_type=F32)
    x1 = _layer_norm(alpha * x_ref[...] + g1_ref[...] * y, lg_ref[...], lbias_ref[...])
    x1_ref[...] = x1
    h2 = x1 * (1.0 + sc2_ref[...]) + sh2_ref[...]
    _store_row_tiles(h2_ref, h2)

    logits = jnp.dot(h2, wr_ref[...], precision=HIGHEST, preferred_element_type=F32) + br_ref[...]
    lane = lax.broadcasted_iota(jnp.int32, (tm, LANES), 1)
    big = jnp.int32(1 << 20)

    def argmax_first(vals, mask):
        mx = jnp.max(jnp.where(mask, vals, -jnp.inf), axis=1, keepdims=True)
        idx = jnp.min(jnp.where(mask & (vals == mx), lane, big), axis=1, keepdims=True)
        return mx, idx

    gmask = lane < ngroups
    gmax = jnp.max(jnp.where(gmask, logits, -jnp.inf), axis=1, keepdims=True)
    gexp = jnp.where(gmask, jnp.exp(logits - gmax), 0.0)
    gprob = gexp / jnp.sum(gexp, axis=1, keepdims=True)
    g_w, g_idx = argmax_first(gprob, gmask)

    lo = ngroups + g_idx * nper
    emask = (lane >= lo) & (lane < lo + nper)
    emax = jnp.max(jnp.where(emask, logits, -jnp.inf), axis=1, keepdims=True)
    eexp = jnp.where(emask, jnp.exp(logits - emax), 0.0)
    eprob = eexp / jnp.sum(eexp, axis=1, keepdims=True)
    p0, i0 = argmax_first(eprob, emask)
    p1, i1 = argmax_first(eprob, emask & (lane != i0))
    den = p0 + p1
    w0 = p0 / den * g_w
    w1 = p1 / den * g_w
    e0 = i0 - ngroups
    e1 = i1 - ngroups

    oh = ((lane == e0) | (lane == e1)).astype(F32)
    r = lax.broadcasted_iota(jnp.int32, (tm, tm), 0)
    c = lax.broadcasted_iota(jnp.int32, (tm, tm), 1)
    strict_lower = (c < r).astype(BF16)
    before = jnp.dot(strict_lower, oh.astype(BF16), preferred_element_type=F32) + carry_sc[...]
    rank0 = jnp.sum(jnp.where(lane == e0, before, 0.0), axis=1, keepdims=True)
    rank1 = jnp.sum(jnp.where(lane == e1, before, 0.0), axis=1, keepdims=True)
    carry_sc[...] = carry_sc[...] + jnp.sum(oh, axis=0, keepdims=True)
    cnt_ref[...] = carry_sc[...]

    info = jnp.where(lane == 0, w0, 0.0)
    info = jnp.where(lane == 1, w1, info)
    info = jnp.where(lane == 2, e0.astype(F32), info)
    info = jnp.where(lane == 3, e1.astype(F32), info)
    info = jnp.where(lane == 4, rank0, info)
    info = jnp.where(lane == 5, rank1, info)
    ri_ref[...] = info


def _mix(yf, oh, gf, gh, x, g1, sc2, sh2, wuf, wuh, wo, ln_g, ln_b, wr, br, alpha, ngroups, nper, tm=256):
    B, S, D = x.shape
    W = yf.shape[2]
    tok = lambda w: pl.BlockSpec((None, tm, w), lambda b, i: (b, i, 0))
    vec = pl.BlockSpec((None, 1, D), lambda b, i: (b, 0, 0))
    full = lambda a: pl.BlockSpec(a.shape, lambda b, i: (0,) * a.ndim)
    return pl.pallas_call(
        functools.partial(_mix_kernel, alpha=alpha, ngroups=ngroups, nper=nper),
        out_shape=(jax.ShapeDtypeStruct((B, S, D), F32),
                   jax.ShapeDtypeStruct((B * S, D // LANES, LANES), F32),
                   jax.ShapeDtypeStruct((B, S, LANES), F32),
                   jax.ShapeDtypeStruct((1, LANES), F32)),
        grid=(B, S // tm),
        in_specs=[tok(W), tok(W), tok(D), tok(D), tok(D), vec, vec, vec,
                  full(wuf), full(wuh), full(wo), full(ln_g), full(ln_b), full(wr), full(br)],
        out_specs=(tok(D),
                   pl.BlockSpec((tm, D // LANES, LANES), lambda b, i: (b * (S // tm) + i, 0, 0)),
                   tok(LANES), pl.BlockSpec((1, LANES), lambda b, i: (0, 0))),
        scratch_shapes=[pltpu.VMEM((1, LANES), F32)],
        compiler_params=_cparams(("arbitrary", "arbitrary")),
    )(yf, oh, gf, gh, x, g1, sc2, sh2, wuf, wuh, wo, ln_g, ln_b, wr, br)


def _experts_kernel(te_ref, tv_ref, src_ref, h_hbm, wg_ref, wu_ref, wd_ref, o_ref, xa, xb, sems, *, tm):
    i = pl.program_id(0)
    last = pl.num_programs(0) - 1
    valid = tv_ref[i] != 0
    bufs = (xa, xb)

    def issue(tile, slot):
        base = tile * tm
        for r in range(tm):
            pltpu.make_async_copy(h_hbm.at[src_ref[base + r]], bufs[slot].at[r], sems.at[slot]).start()

    def wait(slot):
        pltpu.make_async_copy(h_hbm.at[pl.ds(0, tm)], bufs[slot], sems.at[slot]).wait()

    @pl.when((i == 0) & valid)
    def _():
        issue(0, 0)

    def step(slot):
        wait(slot)
        issue(jnp.minimum(i + 1, last), 1 - slot)
        x = _load_row_tiles(bufs[slot]).astype(BF16)
        g = jnp.dot(x, wg_ref[...].astype(BF16), preferred_element_type=F32)
        u = jnp.dot(x, wu_ref[...].astype(BF16), preferred_element_type=F32)
        hid = (_silu(g) * u).astype(BF16)
        _store_row_tiles(o_ref, jnp.dot(hid, wd_ref[...].astype(BF16), preferred_element_type=F32))

        @pl.when(i == last)
        def _():
            wait(1 - slot)

    for slot in range(2):
        pl.when(valid & (i % 2 == slot))(functools.partial(step, slot))

        @pl.when(jnp.logical_not(valid) & (i % 2 == slot) & (tv_ref[jnp.maximum(i - 1, 0)] != 0) & (i > 0))
        def _(slot=slot):
            wait(slot)

    @pl.when(jnp.logical_not(valid))
    def _():
        o_ref[...] = jnp.zeros_like(o_ref)


def _experts(tile_expert, tile_valid, src_tok, h2, wg, wu, wd, tm):
    T, DT, _ = h2.shape
    D = DT * LANES
    E, _, FF = wg.shape
    ntiles = tile_expert.shape[0]
    grid_spec = pltpu.PrefetchScalarGridSpec(
        num_scalar_prefetch=3,
        grid=(ntiles,),
        in_specs=[pl.BlockSpec(memory_space=pl.ANY),
                  pl.BlockSpec((None, D, FF), lambda i, te, tv, src: (te[i], 0, 0)),
                  pl.BlockSpec((None, D, FF), lambda i, te, tv, src: (te[i], 0, 0)),
                  pl.BlockSpec((None, FF, D), lambda i, te, tv, src: (te[i], 0, 0))],
        out_specs=pl.BlockSpec((tm, DT, LANES), lambda i, te, tv, src: (i, 0, 0)),
        scratch_shapes=[pltpu.VMEM((tm, DT, LANES), F32), pltpu.VMEM((tm, DT, LANES), F32),
                        pltpu.SemaphoreType.DMA((2,))],
    )
    return pl.pallas_call(
        functools.partial(_experts_kernel, tm=tm),
        out_shape=jax.ShapeDtypeStruct((ntiles * tm, DT, LANES), F32),
        grid_spec=grid_spec,
        compiler_params=_cparams(("arbitrary",)),
    )(tile_expert, tile_valid, src_tok, h2, wg, wu, wd)


def _combine_kernel(pos_ref, ys_hbm, x1_ref, ri_ref, g2_ref, lg_ref, lb_ref, o_ref, ya, yb, sems, *, alpha, tm):
    i = pl.program_id(0)
    last = pl.num_programs(0) - 1
    bufs = (ya, yb)

    def issue(tile, slot):
        base = tile * (2 * tm)
        for r in range(tm):
            for k in range(2):
                pltpu.make_async_copy(ys_hbm.at[pos_ref[base + 2 * r + k]], bufs[slot].at[k, r],
                                      sems.at[slot]).start()

    def wait(slot):
        for k in range(2):
            pltpu.make_async_copy(ys_hbm.at[pl.ds(0, tm)], bufs[slot].at[k], sems.at[slot]).wait()

    @pl.when(i == 0)
    def _():
        issue(0, 0)

    def step(slot):
        wait(slot)
        issue(jnp.minimum(i + 1, last), 1 - slot)
        ri = ri_ref[...]
        y = ri[:, 0:1] * _load_row_tiles(bufs[slot].at[0]) + ri[:, 1:2] * _load_row_tiles(bufs[slot].at[1])
        o_ref[...] = _layer_norm(alpha * x1_ref[...] + g2_ref[...] * y, lg_ref[...], lb_ref[...])

        @pl.when(i == last)
        def _():
            wait(1 - slot)

    for slot in range(2):
        pl.when(i % 2 == slot)(functools.partial(step, slot))


def _combine(pos, ys, x1, rinfo, g2, ln_g, ln_b, alpha, tm=128):
    B, S, D = x1.shape
    DT = D // LANES
    nb = S // tm
    grid_spec = pltpu.PrefetchScalarGridSpec(
        num_scalar_prefetch=1,
        grid=(B * nb,),
        in_specs=[pl.BlockSpec(memory_space=pl.ANY),
                  pl.BlockSpec((None, tm, D), lambda t, pos: (t // nb, t % nb, 0)),
                  pl.BlockSpec((None, tm, LANES), lambda t, pos: (t // nb, t % nb, 0)),
                  pl.BlockSpec((None, 1, D), lambda t, pos: (t // nb, 0, 0)),
                  pl.BlockSpec((1, D), lambda t, pos: (0, 0)),
                  pl.BlockSpec((1, D), lambda t, pos: (0, 0))],
        out_specs=pl.BlockSpec((None, tm, D), lambda t, pos: (t // nb, t % nb, 0)),
        scratch_shapes=[pltpu.VMEM((2, tm, DT, LANES), F32), pltpu.VMEM((2, tm, DT, LANES), F32),
                        pltpu.SemaphoreType.DMA((2,))],
    )
    return pl.pallas_call(
        functools.partial(_combine_kernel, alpha=alpha, tm=tm),
        out_shape=jax.ShapeDtypeStruct((B, S, D), F32),
        grid_spec=grid_spec,
        compiler_params=_cparams(("arbitrary",)),
    )(pos, ys, x1, rinfo, g2, ln_g, ln_b)


def kernel(x, c, w_ada, b_ada, w_in, b_fox_forget, hgrn_lb_logits, hgrn_norm_w, w_up_fox, w_up_hgrn, w_out,
           ln1_g, ln1_b, w_router_group, b_router_group, w_router_expert, b_router_expert,
           w_expert_gate, w_expert_up, w_expert_down, ln2_g, ln2_b):
    B, S, D = x.shape
    depth = w_ada.shape[0]
    assert depth == 1, "single-layer block"
    fox_heads = b_fox_forget.shape[1]
    fox_w = fox_heads * HEAD_DIM
    hgrn_w = hgrn_norm_w.shape[1]
    ngroups = w_router_group.shape[2]
    nexp = w_router_expert.shape[2]
    nper = nexp // ngroups
    alpha = (2 * depth) ** 0.25
    T = B * S

    ada = _ada(c, w_ada[0], b_ada[0])
    sh1, sc1, g1, sh2, sc2, g2 = [a.reshape(B, 1, D) for a in jnp.split(ada, 6, axis=-1)]

    wi = w_in[0]
    o_ff = 3 * fox_w
    w_packed = jnp.concatenate(
        [wi[:, :o_ff + fox_heads], jnp.zeros((D, LANES - fox_heads), wi.dtype), wi[:, o_ff + fox_heads:]],
        axis=1).astype(BF16)
    widths = [fox_w, fox_w, fox_w, LANES, hgrn_w, hgrn_w, hgrn_w, hgrn_w, D, D]
    segs, off = [], 0
    for w in widths:
        segs.append((off, off + w))
        off += w
    fq, fk, fv, ffp, hq, hf, hi, hg, gf, gh = _inproj(x, sc1, sh1, w_packed, segs)

    bias_p = jnp.zeros((1, LANES), F32).at[0, :fox_heads].set(b_fox_forget[0])
    cum = _foxcum(ffp, bias_p, fox_heads)
    y_fox = _fox(fq, fk, fv, cum.reshape(B, fox_heads // 2, 2, S))

    o_h = _hgrn(hq, hf, hi, hg, hgrn_lb_logits, hgrn_norm_w[0])

    wr = jnp.zeros((D, LANES), F32).at[:, :ngroups].set(w_router_group[0]).at[:, ngroups:ngroups + nexp].set(
        w_router_expert[0])
    br = jnp.zeros((1, LANES), F32).at[0, :ngroups].set(b_router_group[0]).at[0, ngroups:ngroups + nexp].set(
        b_router_expert[0])
    x1, h2, rinfo, counts = _mix(
        y_fox, o_h, gf, gh, x, g1, sc2, sh2,
        w_up_fox[0].astype(BF16), w_up_hgrn[0].astype(BF16), w_out[0].astype(BF16),
        ln1_g[0].reshape(1, D), ln1_b[0].reshape(1, D), wr, br, alpha, ngroups, nper)

    tm_e = 256
    ntiles = (2 * T) // tm_e + nexp
    cnt = counts[0, :nexp].astype(jnp.int32)
    padded = ((cnt + tm_e - 1) // tm_e) * tm_e
    ends = jnp.cumsum(padded)
    starts = ends - padded
    ri = rinfo.reshape(T, LANES)
    eid = ri[:, 2:4].astype(jnp.int32)
    rank = ri[:, 4:6].astype(jnp.int32)
    pos = (starts[eid] + rank).reshape(-1)
    tile_start = jnp.arange(ntiles, dtype=jnp.int32) * tm_e
    tile_expert = jnp.minimum(jnp.sum((tile_start[:, None] >= ends[None, :]).astype(jnp.int32), axis=1), nexp - 1)
    tile_valid = (tile_start < ends[-1]).astype(jnp.int32)
    tok_ids = jnp.repeat(jnp.arange(T, dtype=jnp.int32), 2)
    src_tok = jnp.zeros((ntiles * tm_e,), jnp.int32).at[pos].set(tok_ids)

    ys = _experts(tile_expert, tile_valid, src_tok, h2,
                  w_expert_gate[0], w_expert_up[0], w_expert_down[0], tm_e)
    return _combine(pos, ys, x1, rinfo, g2, ln2_g[0].reshape(1, D), ln2_b[0].reshape(1, D), alpha)
```

```python
import functools

import jax
import jax.numpy as jnp
from jax import lax
from jax.experimental import pallas as pl
from jax.experimental.pallas import tpu as pltpu

F32 = jnp.float32
BF16 = jnp.bfloat16
HIGHEST = lax.Precision.HIGHEST

LANES = 128
HEAD_DIM = 64
LN_EPS = 1e-5
RMS_EPS = 1e-6
NEG_BIG = -1e30
HCHUNK = 16
ROW_TILE = 8
VMEM_LIMIT = 56 * 1024 * 1024


def _cparams(sem, vmem=VMEM_LIMIT):
    return pltpu.CompilerParams(dimension_semantics=sem, vmem_limit_bytes=vmem)


def _sigmoid(x):
    return 1.0 / (1.0 + jnp.exp(-x))


def _silu(x):
    return x * _sigmoid(x)


def _ada_kernel(c_ref, w_ref, b_ref, o_ref):
    c = c_ref[...]
    o_ref[...] = jnp.dot(_silu(c), w_ref[...], precision=HIGHEST,
                         preferred_element_type=F32) + b_ref[...]


def _ada(c, w_ada, b_ada):
    B, D = c.shape
    N = w_ada.shape[1]
    tn = 1024
    return pl.pallas_call(
        _ada_kernel,
        out_shape=jax.ShapeDtypeStruct((B, N), F32),
        grid=(N // tn,),
        in_specs=[pl.BlockSpec((B, D), lambda j: (0, 0)),
                  pl.BlockSpec((D, tn), lambda j: (0, j)),
                  pl.BlockSpec((1, tn), lambda j: (0, j))],
        out_specs=pl.BlockSpec((B, tn), lambda j: (0, j)),
        compiler_params=_cparams(("arbitrary",)),
    )(c, w_ada, b_ada.reshape(1, N))


def _inproj_kernel(x_ref, sc_ref, sh_ref, w_ref,
                   fq_ref, fk_ref, fv_ref, ff_ref, hq_ref, hf_ref, hi_ref, hg_ref, gf_ref, gh_ref,
                   *, segs, q_scale):
    h = (x_ref[...] * (1.0 + sc_ref[...]) + sh_ref[...]).astype(BF16)
    outs = (fq_ref, fk_ref, fv_ref, ff_ref, hq_ref, hf_ref, hi_ref, hg_ref, gf_ref, gh_ref)
    for idx, (o_ref, (a, b)) in enumerate(zip(outs, segs)):
        r = jnp.dot(h, w_ref[:, a:b], preferred_element_type=F32)
        if idx == 0:
            r = r * q_scale
        o_ref[...] = r.astype(o_ref.dtype)


def _inproj(x, sc1, sh1, w_packed, segs, tm=256):
    B, S, D = x.shape
    widths = [b - a for a, b in segs]
    dtypes = [BF16, BF16, BF16, F32, BF16, F32, BF16, BF16, BF16, BF16]
    out_shape = tuple(jax.ShapeDtypeStruct((B, S, w), dt) for w, dt in zip(widths, dtypes))
    out_specs = tuple(pl.BlockSpec((None, tm, w), lambda b, i: (b, i, 0)) for w in widths)
    vec = pl.BlockSpec((None, 1, D), lambda b, i: (b, 0, 0))
    return pl.pallas_call(
        functools.partial(_inproj_kernel, segs=tuple(segs), q_scale=HEAD_DIM ** -0.5),
        out_shape=out_shape,
        grid=(B, S // tm),
        in_specs=[pl.BlockSpec((None, tm, D), lambda b, i: (b, i, 0)), vec, vec,
                  pl.BlockSpec(w_packed.shape, lambda b, i: (0, 0))],
        out_specs=out_specs,
        compiler_params=_cparams(("parallel", "parallel")),
    )(x, sc1, sh1, w_packed)


def _foxcum_kernel(ff_ref, b_ref, o_ref, *, nheads):
    z = ff_ref[...] + b_ref[...]
    lf = jnp.minimum(z, 0.0) - jnp.log(1.0 + jnp.exp(-jnp.abs(z)))
    lft = lf.T[:nheads, :]
    S = lft.shape[1]
    r = lax.broadcasted_iota(jnp.int32, (LANES, LANES), 0)
    c = lax.broadcasted_iota(jnp.int32, (LANES, LANES), 1)
    upper = (r <= c).astype(F32)
    carry = jnp.zeros((nheads, 1), F32)
    for j in range(S // LANES):
        blk = jnp.dot(lft[:, j * LANES:(j + 1) * LANES], upper, precision=HIGHEST,
                      preferred_element_type=F32) + carry
        o_ref[:, j * LANES:(j + 1) * LANES] = blk
        carry = blk[:, LANES - 1:LANES]


def _foxcum(ffp, bias_p, nheads):
    B, S, _ = ffp.shape
    return pl.pallas_call(
        functools.partial(_foxcum_kernel, nheads=nheads),
        out_shape=jax.ShapeDtypeStruct((B, nheads, S), F32),
        grid=(B,),
        in_specs=[pl.BlockSpec((None, S, LANES), lambda b: (b, 0, 0)),
                  pl.BlockSpec((1, LANES), lambda b: (0, 0))],
        out_specs=pl.BlockSpec((None, nheads, S), lambda b: (b, 0, 0)),
        compiler_params=_cparams(("parallel",)),
    )(ffp, bias_p)


def _fox_kernel(q_ref, k_ref, v_ref, c_ref, o_ref, *, tq):
    qi = pl.program_id(2)
    q2 = q_ref[...]
    lane = lax.broadcasted_iota(jnp.int32, (tq, LANES), 1)
    row = lax.broadcasted_iota(jnp.int32, (tq, tq), 0)
    col = lax.broadcasted_iota(jnp.int32, (tq, tq), 1)
    heads = []
    for h in range(2):
        in_head = (lane < HEAD_DIM) if h == 0 else (lane >= HEAD_DIM)
        qh = jnp.where(in_head, q2, jnp.zeros_like(q2))

        def step(j, carry, masked, qh=qh, h=h):
            m, l, acc = carry
            k0 = pl.multiple_of(j * tq, tq)
            ks = k_ref[pl.ds(k0, tq), :]
            vs = v_ref[pl.ds(k0, tq), :]
            s = lax.dot_general(qh, ks, (((1,), (1,)), ((), ())), preferred_element_type=F32)
            s = s - c_ref[h:h + 1, pl.ds(k0, tq)]
            if masked:
                s = jnp.where(row >= col, s, NEG_BIG)
            m_new = jnp.maximum(m, jnp.max(s, axis=1, keepdims=True))
            alpha = jnp.exp(m - m_new)
            p = jnp.exp(s - m_new)
            l = alpha * l + jnp.sum(p, axis=1, keepdims=True)
            acc = alpha * acc + jnp.dot(p.astype(BF16), vs, preferred_element_type=F32)
            return m_new, l, acc

        init = (jnp.full((tq, 1), NEG_BIG, F32), jnp.zeros((tq, 1), F32), jnp.zeros((tq, LANES), F32))
        carry = lax.fori_loop(0, qi, functools.partial(step, masked=False), init)
        _, l, acc = step(qi, carry, True)
        heads.append(acc / l)
    o_ref[...] = jnp.where(lane < HEAD_DIM, heads[0], heads[1]).astype(o_ref.dtype)


def _fox(fq, fk, fv, cum4, tq=256):
    B, S, W = fq.shape
    npairs = W // LANES
    return pl.pallas_call(
        functools.partial(_fox_kernel, tq=tq),
        out_shape=jax.ShapeDtypeStruct((B, S, W), BF16),
        grid=(B, npairs, S // tq),
        in_specs=[pl.BlockSpec((None, tq, LANES), lambda b, p, i: (b, i, p)),
                  pl.BlockSpec((None, S, LANES), lambda b, p, i: (b, 0, p)),
                  pl.BlockSpec((None, S, LANES), lambda b, p, i: (b, 0, p)),
                  pl.BlockSpec((None, None, 2, S), lambda b, p, i: (b, p, 0, 0))],
        out_specs=pl.BlockSpec((None, tq, LANES), lambda b, p, i: (b, i, p)),
        compiler_params=_cparams(("parallel", "parallel", "arbitrary")),
    )(fq, fk, fv, cum4)


def _hgrn_kernel(hq_ref, hf_ref, hi_ref, hg_ref, lb_ref, nw_ref, o_ref,
                 a_sc, qt_sc, kt_sc, kk_sc, qq_sc, p_sc, s_sc, o_sc, st_sc):
    S = hq_ref.shape[0]
    C = HCHUNK
    nchunks = S // C

    lg = lb_ref[...]
    e = jnp.exp(lg - jnp.max(lg, axis=0, keepdims=True))
    lb = e[0:1, :] / jnp.sum(e, axis=0, keepdims=True)

    f = lb + (1.0 - lb) * _sigmoid(hf_ref[...])
    lf = jnp.log(f)
    kk = 1.0 - f
    qq = _silu(hq_ref[...].astype(F32))

    rmod = lax.broadcasted_iota(jnp.int32, (S, LANES), 0) & (C - 1)
    a = lf
    d = 1
    while d < C:
        a = a + jnp.where(rmod >= d, pltpu.roll(a, d, axis=0), 0.0)
        d *= 2
    a3 = a.reshape(nchunks, C, LANES)
    alast = jnp.broadcast_to(a3[:, C - 1:C, :], (nchunks, C, LANES)).reshape(S, LANES)
    a_sc[...] = a
    kk_sc[...] = kk
    qq_sc[...] = qq
    qt_sc[...] = (qq * jnp.exp(a)).astype(BF16)
    kt_sc[...] = (kk * jnp.exp(alast - a)).astype(BF16)

    lane = lax.broadcasted_iota(jnp.int32, (C, LANES), 1)
    trow = lax.broadcasted_iota(jnp.int32, (C, LANES), 0)

    def gen(c, _):
        r0 = pl.multiple_of(c * C, C)
        ac = a_sc[pl.ds(r0, C), :]
        qc = qq_sc[pl.ds(r0, C), :]
        kc = kk_sc[pl.ds(r0, C), :]
        for s in range(C):
            dec = jnp.exp(jnp.minimum(ac - ac[s:s + 1, :], 0.0))
            p = jnp.where(trow >= s, qc * (kc[s:s + 1, :] * dec), 0.0)
            p_sc[pl.ds(r0, C), s * LANES:(s + 1) * LANES] = p.astype(BF16)
        return 0

    lax.fori_loop(0, nchunks, gen, 0)

    er = lax.broadcasted_iota(jnp.int32, (C * LANES, LANES), 0)
    ec = lax.broadcasted_iota(jnp.int32, (C * LANES, LANES), 1)
    emat = (ec == ((er & (LANES - 1)) // HEAD_DIM) * C + er // LANES).astype(BF16)
    rb = 256

    def red(i, _):
        r0 = pl.multiple_of(i * rb, rb)
        s_sc[pl.ds(r0, rb), :] = jnp.dot(p_sc[pl.ds(r0, rb), :], emat, preferred_element_type=F32)
        return 0

    lax.fori_loop(0, S // rb, red, 0)

    sr = lax.broadcasted_iota(jnp.int32, (LANES, LANES), 0)
    scn = lax.broadcasted_iota(jnp.int32, (LANES, LANES), 1)
    same_head = (sr // HEAD_DIM) == (scn // HEAD_DIM)
    st_sc[...] = jnp.zeros((LANES, LANES), F32)

    def rec(c, _):
        r0 = pl.multiple_of(c * C, C)
        st = st_sc[...]
        vc = hi_ref[pl.ds(r0, C), :]
        o_inter = lax.dot_general(qt_sc[pl.ds(r0, C), :], st.astype(BF16),
                                  (((1,), (1,)), ((), ())), preferred_element_type=F32)
        sc = s_sc[pl.ds(r0, C), :][:, :2 * C].astype(BF16)
        v2 = jnp.concatenate([jnp.where(lane < HEAD_DIM, vc, jnp.zeros_like(vc)),
                              jnp.where(lane >= HEAD_DIM, vc, jnp.zeros_like(vc))], axis=0)
        o_intra = jnp.dot(sc, v2, preferred_element_type=F32)
        o_sc[pl.ds(r0, C), :] = o_inter + o_intra
        upd = lax.dot_general(vc, kt_sc[pl.ds(r0, C), :], (((0,), (0,)), ((), ())),
                              preferred_element_type=F32)
        dec = jnp.exp(a_sc[pl.ds(r0 + C - 1, 1), :])
        st_sc[...] = jnp.where(same_head, st * dec + upd, 0.0)
        return 0

    lax.fori_loop(0, nchunks, rec, 0)

    o = o_sc[...]
    ones_head = jnp.where(same_head, 1.0 / HEAD_DIM, 0.0).astype(F32)
    ms = jnp.dot(o * o, ones_head, precision=HIGHEST, preferred_element_type=F32)
    y = o * lax.rsqrt(ms + RMS_EPS) * nw_ref[...]
    o_ref[...] = (y * _silu(hg_ref[...].astype(F32))).astype(o_ref.dtype)


def _hgrn(hq, hf, hi, hg, lb_logits, norm_w):
    B, S, W = hq.shape
    npairs = W // LANES
    nrows = lb_logits.shape[0]
    seq = pl.BlockSpec((None, S, LANES), lambda b, p: (b, 0, p))
    return pl.pallas_call(
        _hgrn_kernel,
        out_shape=jax.ShapeDtypeStruct((B, S, W), BF16),
        grid=(B, npairs),
        in_specs=[seq, seq, seq, seq,
                  pl.BlockSpec((nrows, LANES), lambda b, p: (0, p)),
                  pl.BlockSpec((1, LANES), lambda b, p: (0, p))],
        out_specs=seq,
        scratch_shapes=[pltpu.VMEM((S, LANES), F32),
                        pltpu.VMEM((S, LANES), BF16),
                        pltpu.VMEM((S, LANES), BF16),
                        pltpu.VMEM((S, LANES), F32),
                        pltpu.VMEM((S, LANES), F32),
                        pltpu.VMEM((S, HCHUNK * LANES), BF16),
                        pltpu.VMEM((S, LANES), F32),
                        pltpu.VMEM((S, LANES), F32),
                        pltpu.VMEM((LANES, LANES), F32)],
        compiler_params=_cparams(("parallel", "parallel")),
    )(hq, hf, hi, hg, lb_logits, norm_w.reshape(1, W))


def _layer_norm(v, g, b):
    mu = jnp.mean(v, axis=-1, keepdims=True)
    d = v - mu
    var = jnp.mean(d * d, axis=-1, keepdims=True)
    return d * lax.rsqrt(var + LN_EPS) * g + b


def _store_row_tiles(ref, val):
    n, d = val.shape
    dt = d // LANES
    for j in range(dt):
        ref[pl.ds(j, n, stride=dt), :] = val[:, j * LANES:(j + 1) * LANES]


def _load_row_tiles(ref):
    dt = ROW_TILE
    n = ref.shape[0] // dt
    return jnp.concatenate([ref[pl.ds(j, n, stride=dt), :] for j in range(dt)], axis=1)


def _mix_kernel(yf_ref, oh_ref, gf_ref, gh_ref, x_ref, g1_ref, sc2_ref, sh2_ref,
                wuf_ref, wuh_ref, wo_ref, lg_ref, lbias_ref, wr_ref, br_ref,
                x1_ref, h2_ref, ri_ref, cnt_ref, carry_sc, *, alpha, ngroups, nper):
    first = (pl.program_id(0) == 0) & (pl.program_id(1) == 0)

    @pl.when(first)
    def _():
        carry_sc[...] = jnp.zeros_like(carry_sc)

    tm = x_ref.shape[0]
    yf = jnp.dot(yf_ref[...], wuf_ref[...], preferred_element_type=F32)
    yh = jnp.dot(oh_ref[...], wuh_ref[...], preferred_element_type=F32)
    merged = _sigmoid(gf_ref[...].astype(F32)) * yf + _sigmoid(gh_ref[...].astype(F32)) * yh
    y = jnp.dot(merged.astype(BF16), wo_ref[...], preferred_element_type=F32)
    x1 = _layer_norm(alpha * x_ref[...] + g1_ref[...] * y, lg_ref[...], lbias_ref[...])
    x1_ref[...] = x1
    h2 = x1 * (1.0 + sc2_ref[...]) + sh2_ref[...]
    _store_row_tiles(h2_ref, h2)

    logits = jnp.dot(h2, wr_ref[...], precision=HIGHEST, preferred_element_type=F32) + br_ref[...]
    lane = lax.broadcasted_iota(jnp.int32, (tm, LANES), 1)
    big = jnp.int32(1 << 20)

    def argmax_first(vals, mask):
        mx = jnp.max(jnp.where(mask, vals, -jnp.inf), axis=1, keepdims=True)
        idx = jnp.min(jnp.where(mask & (vals == mx), lane, big), axis=1, keepdims=True)
        return mx, idx

    gmask = lane < ngroups
    gmax = jnp.max(jnp.where(gmask, logits, -jnp.inf), axis=1, keepdims=True)
    gexp = jnp.where(gmask, jnp.exp(logits - gmax), 0.0)
    gprob = gexp / jnp.sum(gexp, axis=1, keepdims=True)
    g_w, g_idx = argmax_first(gprob, gmask)

    lo = ngroups + g_idx * nper
    emask = (lane >= lo) & (lane < lo + nper)
    emax = jnp.max(jnp.where(emask, logits, -jnp.inf), axis=1, keepdims=True)
    eexp = jnp.where(emask, jnp.exp(logits - emax), 0.0)
    eprob = eexp / jnp.sum(eexp, axis=1, keepdims=True)
    p0, i0 = argmax_first(eprob, emask)
    p1, i1 = argmax_first(eprob, emask & (lane != i0))
    den = p0 + p1
    w0 = p0 / den * g_w
    w1 = p1 / den * g_w
    e0 = i0 - ngroups
    e1 = i1 - ngroups

    oh = ((lane == e0) | (lane == e1)).astype(F32)
    r = lax.broadcasted_iota(jnp.int32, (tm, tm), 0)
    c = lax.broadcasted_iota(jnp.int32, (tm, tm), 1)
    strict_lower = (c < r).astype(BF16)
    before = jnp.dot(strict_lower, oh.astype(BF16), preferred_element_type=F32) + carry_sc[...]
    rank0 = jnp.sum(jnp.where(lane == e0, before, 0.0), axis=1, keepdims=True)
    rank1 = jnp.sum(jnp.where(lane == e1, before, 0.0), axis=1, keepdims=True)
    carry_sc[...] = carry_sc[...] + jnp.sum(oh, axis=0, keepdims=True)
    cnt_ref[...] = carry_sc[...]

    info = jnp.where(lane == 0, w0, 0.0)
    info = jnp.where(lane == 1, w1, info)
    info = jnp.where(lane == 2, e0.astype(F32), info)
    info = jnp.where(lane == 3, e1.astype(F32), info)
    info = jnp.where(lane == 4, rank0, info)
    info = jnp.where(lane == 5, rank1, info)
    ri_ref[...] = info


def _mix(yf, oh, gf, gh, x, g1, sc2, sh2, wuf, wuh, wo, ln_g, ln_b, wr, br, alpha, ngroups, nper, tm=256):
    B, S, D = x.shape
    W = yf.shape[2]
    tok = lambda w: pl.BlockSpec((None, tm, w), lambda b, i: (b, i, 0))
    vec = pl.BlockSpec((None, 1, D), lambda b, i: (b, 0, 0))
    full = lambda a: pl.BlockSpec(a.shape, lambda b, i: (0,) * a.ndim)
    return pl.pallas_call(
        functools.partial(_mix_kernel, alpha=alpha, ngroups=ngroups, nper=nper),
        out_shape=(jax.ShapeDtypeStruct((B, S, D), F32),
                   jax.ShapeDtypeStruct((B * S * (D // LANES), LANES), F32),
                   jax.ShapeDtypeStruct((B, S, LANES), F32),
                   jax.ShapeDtypeStruct((1, LANES), F32)),
        grid=(B, S // tm),
        in_specs=[tok(W), tok(W), tok(D), tok(D), tok(D), vec, vec, vec,
                  full(wuf), full(wuh), full(wo), full(ln_g), full(ln_b), full(wr), full(br)],
        out_specs=(tok(D),
                   pl.BlockSpec((tm * (D // LANES), LANES), lambda b, i: (b * (S // tm) + i, 0)),
                   tok(LANES), pl.BlockSpec((1, LANES), lambda b, i: (0, 0))),
        scratch_shapes=[pltpu.VMEM((1, LANES), F32)],
        compiler_params=_cparams(("arbitrary", "arbitrary")),
    )(yf, oh, gf, gh, x, g1, sc2, sh2, wuf, wuh, wo, ln_g, ln_b, wr, br)


def _experts_kernel(te_ref, tv_ref, src_ref, h_hbm, wg_ref, wu_ref, wd_ref, o_ref, xa, xb, sems, *, tm):
    i = pl.program_id(0)
    last = pl.num_programs(0) - 1
    valid = tv_ref[i] != 0
    bufs = (xa, xb)

    def issue(tile, slot):
        base = tile * tm
        for r in range(tm):
            row0 = pl.multiple_of(src_ref[base + r], ROW_TILE)
            pltpu.make_async_copy(h_hbm.at[pl.ds(row0, ROW_TILE)], bufs[slot].at[pl.ds(r * ROW_TILE, ROW_TILE)],
                                  sems.at[slot]).start(priority=r % 2)

    def wait(slot):
        pltpu.make_async_copy(h_hbm.at[pl.ds(0, tm * ROW_TILE)], bufs[slot], sems.at[slot]).wait()

    @pl.when((i == 0) & valid)
    def _():
        issue(0, 0)

    def step(slot):
        wait(slot)
        issue(jnp.minimum(i + 1, last), 1 - slot)
        x = _load_row_tiles(bufs[slot]).astype(BF16)
        g = jnp.dot(x, wg_ref[...].astype(BF16), preferred_element_type=F32)
        u = jnp.dot(x, wu_ref[...].astype(BF16), preferred_element_type=F32)
        hid = (_silu(g) * u).astype(BF16)
        _store_row_tiles(o_ref, jnp.dot(hid, wd_ref[...].astype(BF16), preferred_element_type=F32))

        @pl.when(i == last)
        def _():
            wait(1 - slot)

    for slot in range(2):
        pl.when(valid & (i % 2 == slot))(functools.partial(step, slot))

        @pl.when(jnp.logical_not(valid) & (i % 2 == slot) & (tv_ref[jnp.maximum(i - 1, 0)] != 0) & (i > 0))
        def _(slot=slot):
            wait(slot)

    @pl.when(jnp.logical_not(valid))
    def _():
        o_ref[...] = jnp.zeros_like(o_ref)


def _experts(tile_expert, tile_valid, src_tok, h2, wg, wu, wd, tm):
    E, D, FF = wg.shape
    assert D == ROW_TILE * LANES
    ntiles = tile_expert.shape[0]
    grid_spec = pltpu.PrefetchScalarGridSpec(
        num_scalar_prefetch=3,
        grid=(ntiles,),
        in_specs=[pl.BlockSpec(memory_space=pl.ANY),
                  pl.BlockSpec((None, D, FF), lambda i, te, tv, src: (te[i], 0, 0)),
                  pl.BlockSpec((None, D, FF), lambda i, te, tv, src: (te[i], 0, 0)),
                  pl.BlockSpec((None, FF, D), lambda i, te, tv, src: (te[i], 0, 0))],
        out_specs=pl.BlockSpec((tm * ROW_TILE, LANES), lambda i, te, tv, src: (i, 0)),
        scratch_shapes=[pltpu.VMEM((tm * ROW_TILE, LANES), F32), pltpu.VMEM((tm * ROW_TILE, LANES), F32),
                        pltpu.SemaphoreType.DMA((2,))],
    )
    return pl.pallas_call(
        functools.partial(_experts_kernel, tm=tm),
        out_shape=jax.ShapeDtypeStruct((ntiles * tm * ROW_TILE, LANES), F32),
        grid_spec=grid_spec,
        compiler_params=_cparams(("arbitrary",)),
    )(tile_expert, tile_valid, src_tok, h2, wg, wu, wd)


def _combine_kernel(pos_ref, ys_hbm, x1_ref, ri_ref, g2_ref, lg_ref, lb_ref, o_ref, ya, yb, sems, *, alpha, tm):
    i = pl.program_id(0)
    last = pl.num_programs(0) - 1
    bufs = (ya, yb)

    def issue(tile, slot):
        base = tile * (2 * tm)
        for r in range(tm):
            for k in range(2):
                row0 = pl.multiple_of(pos_ref[base + 2 * r + k], ROW_TILE)
                pltpu.make_async_copy(ys_hbm.at[pl.ds(row0, ROW_TILE)],
                                      bufs[slot].at[k, pl.ds(r * ROW_TILE, ROW_TILE)],
                                      sems.at[slot]).start(priority=k)

    def wait(slot):
        for k in range(2):
            pltpu.make_async_copy(ys_hbm.at[pl.ds(0, tm * ROW_TILE)], bufs[slot].at[k], sems.at[slot]).wait()

    @pl.when(i == 0)
    def _():
        issue(0, 0)

    def step(slot):
        wait(slot)
        issue(jnp.minimum(i + 1, last), 1 - slot)
        ri = ri_ref[...]
        y = ri[:, 0:1] * _load_row_tiles(bufs[slot].at[0]) + ri[:, 1:2] * _load_row_tiles(bufs[slot].at[1])
        o_ref[...] = _layer_norm(alpha * x1_ref[...] + g2_ref[...] * y, lg_ref[...], lb_ref[...])

        @pl.when(i == last)
        def _():
            wait(1 - slot)

    for slot in range(2):
        pl.when(i % 2 == slot)(functools.partial(step, slot))


def _combine(pos, ys, x1, rinfo, g2, ln_g, ln_b, alpha, tm=128):
    B, S, D = x1.shape
    assert D == ROW_TILE * LANES
    nb = S // tm
    grid_spec = pltpu.PrefetchScalarGridSpec(
        num_scalar_prefetch=1,
        grid=(B * nb,),
        in_specs=[pl.BlockSpec(memory_space=pl.ANY),
                  pl.BlockSpec((None, tm, D), lambda t, pos: (t // nb, t % nb, 0)),
                  pl.BlockSpec((None, tm, LANES), lambda t, pos: (t // nb, t % nb, 0)),
                  pl.BlockSpec((None, 1, D), lambda t, pos: (t // nb, 0, 0)),
                  pl.BlockSpec((1, D), lambda t, pos: (0, 0)),
                  pl.BlockSpec((1, D), lambda t, pos: (0, 0))],
        out_specs=pl.BlockSpec((None, tm, D), lambda t, pos: (t // nb, t % nb, 0)),
        scratch_shapes=[pltpu.VMEM((2, tm * ROW_TILE, LANES), F32), pltpu.VMEM((2, tm * ROW_TILE, LANES), F32),
                        pltpu.SemaphoreType.DMA((2,))],
    )
    return pl.pallas_call(
        functools.partial(_combine_kernel, alpha=alpha, tm=tm),
        out_shape=jax.ShapeDtypeStruct((B, S, D), F32),
        grid_spec=grid_spec,
        compiler_params=_cparams(("arbitrary",)),
    )(pos, ys, x1, rinfo, g2, ln_g, ln_b)


def kernel(x, c, w_ada, b_ada, w_in, b_fox_forget, hgrn_lb_logits, hgrn_norm_w, w_up_fox, w_up_hgrn, w_out,
           ln1_g, ln1_b, w_router_group, b_router_group, w_router_expert, b_router_expert,
           w_expert_gate, w_expert_up, w_expert_down, ln2_g, ln2_b):
    B, S, D = x.shape
    depth = w_ada.shape[0]
    assert depth == 1, "single-layer block"
    fox_heads = b_fox_forget.shape[1]
    fox_w = fox_heads * HEAD_DIM
    hgrn_w = hgrn_norm_w.shape[1]
    ngroups = w_router_group.shape[2]
    nexp = w_router_expert.shape[2]
    nper = nexp // ngroups
    alpha = (2 * depth) ** 0.25
    T = B * S

    ada = _ada(c, w_ada[0], b_ada[0])
    sh1, sc1, g1, sh2, sc2, g2 = [a.reshape(B, 1, D) for a in jnp.split(ada, 6, axis=-1)]

    wi = w_in[0]
    o_ff = 3 * fox_w
    w_packed = jnp.concatenate(
        [wi[:, :o_ff + fox_heads], jnp.zeros((D, LANES - fox_heads), wi.dtype), wi[:, o_ff + fox_heads:]],
        axis=1).astype(BF16)
    widths = [fox_w, fox_w, fox_w, LANES, hgrn_w, hgrn_w, hgrn_w, hgrn_w, D, D]
    segs, off = [], 0
    for w in widths:
        segs.append((off, off + w))
        off += w
    fq, fk, fv, ffp, hq, hf, hi, hg, gf, gh = _inproj(x, sc1, sh1, w_packed, segs)

    bias_p = jnp.zeros((1, LANES), F32).at[0, :fox_heads].set(b_fox_forget[0])
    cum = _foxcum(ffp, bias_p, fox_heads)
    y_fox = _fox(fq, fk, fv, cum.reshape(B, fox_heads // 2, 2, S))

    o_h = _hgrn(hq, hf, hi, hg, hgrn_lb_logits, hgrn_norm_w[0])

    wr = jnp.zeros((D, LANES), F32).at[:, :ngroups].set(w_router_group[0]).at[:, ngroups:ngroups + nexp].set(
        w_router_expert[0])
    br = jnp.zeros((1, LANES), F32).at[0, :ngroups].set(b_router_group[0]).at[0, ngroups:ngroups + nexp].set(
        b_router_expert[0])
    x1, h2, rinfo, counts = _mix(
        y_fox, o_h, gf, gh, x, g1, sc2, sh2,
        w_up_fox[0].astype(BF16), w_up_hgrn[0].astype(BF16), w_out[0].astype(BF16),
        ln1_g[0].reshape(1, D), ln1_b[0].reshape(1, D), wr, br, alpha, ngroups, nper)

    tm_e = 256
    ntiles = (2 * T) // tm_e + nexp
    cnt = counts[0, :nexp].astype(jnp.int32)
    padded = ((cnt + tm_e - 1) // tm_e) * tm_e
    ends = jnp.cumsum(padded)
    starts = ends - padded
    ri = rinfo.reshape(T, LANES)
    eid = ri[:, 2:4].astype(jnp.int32)
    rank = ri[:, 4:6].astype(jnp.int32)
    pos = (starts[eid] + rank).reshape(-1)
    tile_start = jnp.arange(ntiles, dtype=jnp.int32) * tm_e
    tile_expert = jnp.minimum(jnp.sum((tile_start[:, None] >= ends[None, :]).astype(jnp.int32), axis=1), nexp - 1)
    tile_valid = (tile_start < ends[-1]).astype(jnp.int32)
    tok_ids = jnp.repeat(jnp.arange(T, dtype=jnp.int32), 2)
    src_row = jnp.zeros((ntiles * tm_e,), jnp.int32).at[pos].set(tok_ids * ROW_TILE)

    ys = _experts(tile_expert, tile_valid, src_row, h2,
                  w_expert_gate[0], w_expert_up[0], w_expert_down[0], tm_e)
    return _combine(pos * ROW_TILE, ys, x1, rinfo, g2, ln2_g[0].reshape(1, D), ln2_b[0].reshape(1, D), alpha)
```

```python
import functools

import jax
import jax.numpy as jnp
from jax import lax
from jax.experimental import pallas as pl
from jax.experimental.pallas import tpu as pltpu

F32 = jnp.float32
BF16 = jnp.bfloat16
HIGHEST = lax.Precision.HIGHEST

LANES = 128
HEAD_DIM = 64
LN_EPS = 1e-5
RMS_EPS = 1e-6
LOG2E = 1.4426950408889634
NEG_BIG = -1e30
HCHUNK = 16
ROW_TILE = 8
VMEM_LIMIT = 56 * 1024 * 1024


def _cparams(sem, vmem=VMEM_LIMIT):
    return pltpu.CompilerParams(dimension_semantics=sem, vmem_limit_bytes=vmem)


def _sigmoid(x):
    return 1.0 / (1.0 + jnp.exp(-x))


def _silu(x):
    return x * _sigmoid(x)


def _ada_kernel(c_ref, w_ref, b_ref, o_ref):
    c = c_ref[...]
    o_ref[...] = jnp.dot(_silu(c), w_ref[...], precision=HIGHEST,
                         preferred_element_type=F32) + b_ref[...]


def _ada(c, w_ada, b_ada):
    B, D = c.shape
    N = w_ada.shape[1]
    tn = 1024
    return pl.pallas_call(
        _ada_kernel,
        out_shape=jax.ShapeDtypeStruct((B, N), F32),
        grid=(N // tn,),
        in_specs=[pl.BlockSpec((B, D), lambda j: (0, 0)),
                  pl.BlockSpec((D, tn), lambda j: (0, j)),
                  pl.BlockSpec((1, tn), lambda j: (0, j))],
        out_specs=pl.BlockSpec((B, tn), lambda j: (0, j)),
        compiler_params=_cparams(("arbitrary",)),
    )(c, w_ada, b_ada.reshape(1, N))


def _inproj_kernel(x_ref, sc_ref, sh_ref, w_ref,
                   fq_ref, fk_ref, fv_ref, ff_ref, hq_ref, hf_ref, hi_ref, hg_ref, gf_ref, gh_ref,
                   *, segs, q_scale):
    h = (x_ref[...] * (1.0 + sc_ref[...]) + sh_ref[...]).astype(BF16)
    outs = (fq_ref, fk_ref, fv_ref, ff_ref, hq_ref, hf_ref, hi_ref, hg_ref, gf_ref, gh_ref)
    for idx, (o_ref, (a, b)) in enumerate(zip(outs, segs)):
        r = jnp.dot(h, w_ref[:, a:b], preferred_element_type=F32)
        if idx == 0:
            r = r * q_scale
        o_ref[...] = r.astype(o_ref.dtype)


def _inproj(x, sc1, sh1, w_packed, segs, tm=256):
    B, S, D = x.shape
    widths = [b - a for a, b in segs]
    dtypes = [BF16, BF16, BF16, F32, BF16, F32, BF16, BF16, BF16, BF16]
    out_shape = tuple(jax.ShapeDtypeStruct((B, S, w), dt) for w, dt in zip(widths, dtypes))
    out_specs = tuple(pl.BlockSpec((None, tm, w), lambda b, i: (b, i, 0)) for w in widths)
    vec = pl.BlockSpec((None, 1, D), lambda b, i: (b, 0, 0))
    return pl.pallas_call(
        functools.partial(_inproj_kernel, segs=tuple(segs), q_scale=HEAD_DIM ** -0.5 * LOG2E),
        out_shape=out_shape,
        grid=(B, S // tm),
        in_specs=[pl.BlockSpec((None, tm, D), lambda b, i: (b, i, 0)), vec, vec,
                  pl.BlockSpec(w_packed.shape, lambda b, i: (0, 0))],
        out_specs=out_specs,
        compiler_params=_cparams(("parallel", "parallel")),
    )(x, sc1, sh1, w_packed)


def _foxcum_kernel(ff_ref, b_ref, o_ref, *, blk):
    S = ff_ref.shape[0]
    r = lax.broadcasted_iota(jnp.int32, (blk, blk), 0)
    c = lax.broadcasted_iota(jnp.int32, (blk, blk), 1)
    lower = (r >= c).astype(F32)
    carry = jnp.zeros((1, LANES), F32)
    for j in range(S // blk):
        z = ff_ref[j * blk:(j + 1) * blk, :] + b_ref[...]
        lf = jnp.minimum(z, 0.0) - jnp.log(1.0 + jnp.exp(-jnp.abs(z)))
        cum = jnp.dot(lower, lf, precision=HIGHEST, preferred_element_type=F32) + carry
        o_ref[j * blk:(j + 1) * blk, :] = cum * LOG2E
        carry = cum[blk - 1:blk, :]


def _foxcum(ffp, bias_p, blk=256):
    B, S, _ = ffp.shape
    return pl.pallas_call(
        functools.partial(_foxcum_kernel, blk=blk),
        out_shape=jax.ShapeDtypeStruct((B, S, LANES), F32),
        grid=(B,),
        in_specs=[pl.BlockSpec((None, S, LANES), lambda b: (b, 0, 0)),
                  pl.BlockSpec((1, LANES), lambda b: (0, 0))],
        out_specs=pl.BlockSpec((None, S, LANES), lambda b: (b, 0, 0)),
        compiler_params=_cparams(("parallel",)),
    )(ffp, bias_p)


NCUM = 3


def _fox_kernel(q_ref, k_ref, v_ref, c_ref, o_ref, ka_sc, kb_sc, va_sc, vb_sc, *, tq, tk):
    p = pl.program_id(1)
    qi = pl.program_id(2)
    S = k_ref.shape[0]

    @pl.when(qi == 0)
    def _():
        lane = lax.broadcasted_iota(jnp.int32, (S, LANES), 1)
        rr = lax.broadcasted_iota(jnp.int32, (LANES, LANES), 0)
        cc = lax.broadcasted_iota(jnp.int32, (LANES, LANES), 1)
        rest = c_ref[...]
        placed = jnp.zeros((S, LANES), F32)
        for i in range(NCUM):
            piece = rest.astype(BF16)
            rest = rest - piece.astype(F32)
            sel = ((rr == 2 * p) & (cc == HEAD_DIM + i)) | ((rr == 2 * p + 1) & (cc == i))
            placed = placed + jnp.dot(piece, jnp.where(sel, 1.0, 0.0).astype(BF16), preferred_element_type=F32)
        k2 = k_ref[...].astype(F32)
        ka_sc[...] = jnp.where(lane < HEAD_DIM, k2, -placed).astype(BF16)
        kb_sc[...] = jnp.where(lane >= HEAD_DIM, k2, -placed).astype(BF16)
        vt = v_ref[...].astype(F32).T
        row = lax.broadcasted_iota(jnp.int32, (LANES, S), 0)
        va_sc[...] = jnp.where(row < HEAD_DIM, vt, jnp.where(row == HEAD_DIM, 1.0, 0.0)).astype(BF16)
        vb_sc[...] = jnp.where(row >= HEAD_DIM, vt, jnp.where(row == 0, 1.0, 0.0)).astype(BF16)

    q2 = q_ref[...].astype(F32)
    qlane = lax.broadcasted_iota(jnp.int32, (tq, LANES), 1)
    qa = jnp.where(qlane < HEAD_DIM, q2, jnp.where(qlane < HEAD_DIM + NCUM, 1.0, 0.0)).astype(BF16)
    qb = jnp.where(qlane >= HEAD_DIM, q2, jnp.where(qlane < NCUM, 1.0, 0.0)).astype(BF16)
    krow = lax.broadcasted_iota(jnp.int32, (tk, tq), 0)
    qcol = lax.broadcasted_iota(jnp.int32, (tk, tq), 1)
    nsub = tq // tk

    def block(k0, carry, diag_off):
        out = []
        for ksc, vsc, qh, (m, acc) in ((ka_sc, va_sc, qa, carry[:2]), (kb_sc, vb_sc, qb, carry[2:])):
            st = lax.dot_general(ksc[pl.ds(k0, tk), :], qh, (((1,), (1,)), ((), ())),
                                 preferred_element_type=F32)
            if diag_off is not None:
                st = jnp.where(krow + diag_off <= qcol, st, NEG_BIG)
            m_new = jnp.maximum(m, jnp.max(st, axis=0, keepdims=True))
            pt = jnp.exp2(st - m_new).astype(BF16)
            acc = jnp.exp2(m - m_new) * acc + jnp.dot(vsc[:, pl.ds(k0, tk)], pt, preferred_element_type=F32)
            out += [m_new, acc]
        return tuple(out)

    def pair(j, carry):
        k0 = pl.multiple_of(j * (2 * tk), 2 * tk)
        return block(k0 + tk, block(k0, carry, None), None)

    m0 = jnp.full((1, tq), NEG_BIG, F32)
    a0 = jnp.zeros((LANES, tq), F32)
    carry = lax.fori_loop(0, qi * (nsub // 2), pair, (m0, a0, m0, a0))
    for d in range(nsub):
        carry = block(pl.multiple_of(qi * tq + d * tk, tk), carry, d * tk)
    _, aa, _, ab = carry
    row = lax.broadcasted_iota(jnp.int32, (LANES, tq), 0)
    ot = jnp.where(row < HEAD_DIM, aa * (1.0 / aa[HEAD_DIM:HEAD_DIM + 1, :]), ab * (1.0 / ab[0:1, :]))
    o_ref[...] = ot.T.astype(o_ref.dtype)


def _fox(fq, fk, fv, cum, tq=512, tk=256):
    B, S, W = fq.shape
    assert tq % (2 * tk) == 0 and S % tq == 0
    npairs = W // LANES
    return pl.pallas_call(
        functools.partial(_fox_kernel, tq=tq, tk=tk),
        out_shape=jax.ShapeDtypeStruct((B, S, W), BF16),
        grid=(B, npairs, S // tq),
        in_specs=[pl.BlockSpec((None, tq, LANES), lambda b, p, i: (b, i, p)),
                  pl.BlockSpec((None, S, LANES), lambda b, p, i: (b, 0, p)),
                  pl.BlockSpec((None, S, LANES), lambda b, p, i: (b, 0, p)),
                  pl.BlockSpec((None, S, LANES), lambda b, p, i: (b, 0, 0))],
        out_specs=pl.BlockSpec((None, tq, LANES), lambda b, p, i: (b, i, p)),
        scratch_shapes=[pltpu.VMEM((S, LANES), BF16), pltpu.VMEM((S, LANES), BF16),
                        pltpu.VMEM((LANES, S), BF16), pltpu.VMEM((LANES, S), BF16)],
        compiler_params=_cparams(("parallel", "parallel", "arbitrary")),
    )(fq, fk, fv, cum)


def _hgrn_kernel(hq_ref, hf_ref, hi_ref, hg_ref, lb_ref, nw_ref, o_ref,
                 a_sc, qt_sc, kt_sc, kk_sc, qq_sc, p_sc, s_sc, o_sc, st_sc):
    S = hq_ref.shape[0]
    C = HCHUNK
    nchunks = S // C

    lg = lb_ref[...]
    e = jnp.exp(lg - jnp.max(lg, axis=0, keepdims=True))
    lb = e[0:1, :] / jnp.sum(e, axis=0, keepdims=True)

    f = lb + (1.0 - lb) * _sigmoid(hf_ref[...])
    lf = jnp.log(f)
    kk = 1.0 - f
    qq = _silu(hq_ref[...].astype(F32))

    rmod = lax.broadcasted_iota(jnp.int32, (S, LANES), 0) & (C - 1)
    a = lf
    d = 1
    while d < C:
        a = a + jnp.where(rmod >= d, pltpu.roll(a, d, axis=0), 0.0)
        d *= 2
    a3 = a.reshape(nchunks, C, LANES)
    alast = jnp.broadcast_to(a3[:, C - 1:C, :], (nchunks, C, LANES)).reshape(S, LANES)
    a_sc[...] = a
    kk_sc[...] = kk
    qq_sc[...] = qq
    qt_sc[...] = (qq * jnp.exp(a)).astype(BF16)
    kt_sc[...] = (kk * jnp.exp(alast - a)).astype(BF16)

    lane = lax.broadcasted_iota(jnp.int32, (C, LANES), 1)
    trow = lax.broadcasted_iota(jnp.int32, (C, LANES), 0)

    def gen(c, _):
        r0 = pl.multiple_of(c * C, C)
        ac = a_sc[pl.ds(r0, C), :]
        qc = qq_sc[pl.ds(r0, C), :]
        kc = kk_sc[pl.ds(r0, C), :]
        for s in range(C):
            dec = jnp.exp(jnp.minimum(ac - ac[s:s + 1, :], 0.0))
            p = jnp.where(trow >= s, qc * (kc[s:s + 1, :] * dec), 0.0)
            p_sc[pl.ds(r0, C), s * LANES:(s + 1) * LANES] = p.astype(BF16)
        return 0

    lax.fori_loop(0, nchunks, gen, 0)

    er = lax.broadcasted_iota(jnp.int32, (C * LANES, LANES), 0)
    ec = lax.broadcasted_iota(jnp.int32, (C * LANES, LANES), 1)
    emat = (ec == ((er & (LANES - 1)) // HEAD_DIM) * C + er // LANES).astype(BF16)
    rb = 256

    def red(i, _):
        r0 = pl.multiple_of(i * rb, rb)
        s_sc[pl.ds(r0, rb), :] = jnp.dot(p_sc[pl.ds(r0, rb), :], emat, preferred_element_type=F32)
        return 0

    lax.fori_loop(0, S // rb, red, 0)

    sr = lax.broadcasted_iota(jnp.int32, (LANES, LANES), 0)
    scn = lax.broadcasted_iota(jnp.int32, (LANES, LANES), 1)
    same_head = (sr // HEAD_DIM) == (scn // HEAD_DIM)
    st_sc[...] = jnp.zeros((LANES, LANES), F32)

    def rec(c, _):
        r0 = pl.multiple_of(c * C, C)
        st = st_sc[...]
        vc = hi_ref[pl.ds(r0, C), :]
        o_inter = lax.dot_general(qt_sc[pl.ds(r0, C), :], st.astype(BF16),
                                  (((1,), (1,)), ((), ())), preferred_element_type=F32)
        sc = s_sc[pl.ds(r0, C), :][:, :2 * C].astype(BF16)
        v2 = jnp.concatenate([jnp.where(lane < HEAD_DIM, vc, jnp.zeros_like(vc)),
                              jnp.where(lane >= HEAD_DIM, vc, jnp.zeros_like(vc))], axis=0)
        o_intra = jnp.dot(sc, v2, preferred_element_type=F32)
        o_sc[pl.ds(r0, C), :] = o_inter + o_intra
        upd = lax.dot_general(vc, kt_sc[pl.ds(r0, C), :], (((0,), (0,)), ((), ())),
                              preferred_element_type=F32)
        dec = jnp.exp(a_sc[pl.ds(r0 + C - 1, 1), :])
        st_sc[...] = jnp.where(same_head, st * dec + upd, 0.0)
        return 0

    lax.fori_loop(0, nchunks, rec, 0)

    o = o_sc[...]
    ones_head = jnp.where(same_head, 1.0 / HEAD_DIM, 0.0).astype(F32)
    ms = jnp.dot(o * o, ones_head, precision=HIGHEST, preferred_element_type=F32)
    y = o * lax.rsqrt(ms + RMS_EPS) * nw_ref[...]
    o_ref[...] = (y * _silu(hg_ref[...].astype(F32))).astype(o_ref.dtype)


def _hgrn(hq, hf, hi, hg, lb_logits, norm_w):
    B, S, W = hq.shape
    npairs = W // LANES
    nrows = lb_logits.shape[0]
    seq = pl.BlockSpec((None, S, LANES), lambda b, p: (b, 0, p))
    return pl.pallas_call(
        _hgrn_kernel,
        out_shape=jax.ShapeDtypeStruct((B, S, W), BF16),
        grid=(B, npairs),
        in_specs=[seq, seq, seq, seq,
                  pl.BlockSpec((nrows, LANES), lambda b, p: (0, p)),
                  pl.BlockSpec((1, LANES), lambda b, p: (0, p))],
        out_specs=seq,
        scratch_shapes=[pltpu.VMEM((S, LANES), F32),
                        pltpu.VMEM((S, LANES), BF16),
                        pltpu.VMEM((S, LANES), BF16),
                        pltpu.VMEM((S, LANES), F32),
                        pltpu.VMEM((S, LANES), F32),
                        pltpu.VMEM((S, HCHUNK * LANES), BF16),
                        pltpu.VMEM((S, LANES), F32),
                        pltpu.VMEM((S, LANES), F32),
                        pltpu.VMEM((LANES, LANES), F32)],
        compiler_params=_cparams(("parallel", "parallel")),
    )(hq, hf, hi, hg, lb_logits, norm_w.reshape(1, W))


def _layer_norm(v, g, b):
    mu = jnp.mean(v, axis=-1, keepdims=True)
    d = v - mu
    var = jnp.mean(d * d, axis=-1, keepdims=True)
    return d * lax.rsqrt(var + LN_EPS) * g + b


def _store_row_tiles(ref, val):
    n, d = val.shape
    dt = d // LANES
    for j in range(dt):
        ref[pl.ds(j, n, stride=dt), :] = val[:, j * LANES:(j + 1) * LANES]


def _load_row_tiles(ref):
    dt = ROW_TILE
    n = ref.shape[0] // dt
    return jnp.concatenate([ref[pl.ds(j, n, stride=dt), :] for j in range(dt)], axis=1)


def _mix_kernel(yf_ref, oh_ref, gf_ref, gh_ref, x_ref, g1_ref, sc2_ref, sh2_ref,
                wuf_ref, wuh_ref, wo_ref, lg_ref, lbias_ref, wr_ref, br_ref,
                x1_ref, h2_ref, ri_ref, cnt_ref, carry_sc, *, alpha, ngroups, nper):
    first = (pl.program_id(0) == 0) & (pl.program_id(1) == 0)

    @pl.when(first)
    def _():
        carry_sc[...] = jnp.zeros_like(carry_sc)

    tm = x_ref.shape[0]
    yf = jnp.dot(yf_ref[...], wuf_ref[...], preferred_element_type=F32)
    yh = jnp.dot(oh_ref[...], wuh_ref[...], preferred_element_type=F32)
    merged = _sigmoid(gf_ref[...].astype(F32)) * yf + _sigmoid(gh_ref[...].astype(F32)) * yh
    y = jnp.dot(merged.astype(BF16), wo_ref[...], preferred_element_type=F32)
    x1 = _layer_norm(alpha * x_ref[...] + g1_ref[...] * y, lg_ref[...], lbias_ref[...])
    x1_ref[...] = x1
    h2 = x1 * (1.0 + sc2_ref[...]) + sh2_ref[...]
    _store_row_tiles(h2_ref, h2)

    logits = jnp.dot(h2, wr_ref[...], precision=HIGHEST, preferred_element_type=F32) + br_ref[...]
    lane = lax.broadcasted_iota(jnp.int32, (tm, LANES), 1)
    big = jnp.int32(1 << 20)

    def argmax_first(vals, mask):
        mx = jnp.max(jnp.where(mask, vals, -jnp.inf), axis=1, keepdims=True)
        idx = jnp.min(jnp.where(mask & (vals == mx), lane, big), axis=1, keepdims=True)
        return mx, idx

    gmask = lane < ngroups
    gmax = jnp.max(jnp.where(gmask, logits, -jnp.inf), axis=1, keepdims=True)
    gexp = jnp.where(gmask, jnp.exp(logits - gmax), 0.0)
    gprob = gexp / jnp.sum(gexp, axis=1, keepdims=True)
    g_w, g_idx = argmax_first(gprob, gmask)

    lo = ngroups + g_idx * nper
    emask = (lane >= lo) & (lane < lo + nper)
    emax = jnp.max(jnp.where(emask, logits, -jnp.inf), axis=1, keepdims=True)
    eexp = jnp.where(emask, jnp.exp(logits - emax), 0.0)
    eprob = eexp / jnp.sum(eexp, axis=1, keepdims=True)
    p0, i0 = argmax_first(eprob, emask)
    p1, i1 = argmax_first(eprob, emask & (lane != i0))
    den = p0 + p1
    w0 = p0 / den * g_w
    w1 = p1 / den * g_w
    e0 = i0 - ngroups
    e1 = i1 - ngroups

    oh = ((lane == e0) | (lane == e1)).astype(F32)
    r = lax.broadcasted_iota(jnp.int32, (tm, tm), 0)
    c = lax.broadcasted_iota(jnp.int32, (tm, tm), 1)
    strict_lower = (c < r).astype(BF16)
    before = jnp.dot(strict_lower, oh.astype(BF16), preferred_element_type=F32) + carry_sc[...]
    rank0 = jnp.sum(jnp.where(lane == e0, before, 0.0), axis=1, keepdims=True)
    rank1 = jnp.sum(jnp.where(lane == e1, before, 0.0), axis=1, keepdims=True)
    carry_sc[...] = carry_sc[...] + jnp.sum(oh, axis=0, keepdims=True)
    cnt_ref[...] = carry_sc[...]

    info = jnp.where(lane == 0, w0, 0.0)
    info = jnp.where(lane == 1, w1, info)
    info = jnp.where(lane == 2, e0.astype(F32), info)
    info = jnp.where(lane == 3, e1.astype(F32), info)
    info = jnp.where(lane == 4, rank0, info)
    info = jnp.where(lane == 5, rank1, info)
    ri_ref[...] = info


def _mix(yf, oh, gf, gh, x, g1, sc2, sh2, wuf, wuh, wo, ln_g, ln_b, wr, br, alpha, ngroups, nper, tm=256):
    B, S, D = x.shape
    W = yf.shape[2]
    tok = lambda w: pl.BlockSpec((None, tm, w), lambda b, i: (b, i, 0))
    vec = pl.BlockSpec((None, 1, D), lambda b, i: (b, 0, 0))
    full = lambda a: pl.BlockSpec(a.shape, lambda b, i: (0,) * a.ndim)
    return pl.pallas_call(
        functools.partial(_mix_kernel, alpha=alpha, ngroups=ngroups, nper=nper),
        out_shape=(jax.ShapeDtypeStruct((B, S, D), F32),
                   jax.ShapeDtypeStruct((B * S * (D // LANES), LANES), F32),
                   jax.ShapeDtypeStruct((B, S, LANES), F32),
                   jax.ShapeDtypeStruct((1, LANES), F32)),
        grid=(B, S // tm),
        in_specs=[tok(W), tok(W), tok(D), tok(D), tok(D), vec, vec, vec,
                  full(wuf), full(wuh), full(wo), full(ln_g), full(ln_b), full(wr), full(br)],
        out_specs=(tok(D),
                   pl.BlockSpec((tm * (D // LANES), LANES), lambda b, i: (b * (S // tm) + i, 0)),
                   tok(LANES), pl.BlockSpec((1, LANES), lambda b, i: (0, 0))),
        scratch_shapes=[pltpu.VMEM((1, LANES), F32)],
        compiler_params=_cparams(("arbitrary", "arbitrary")),
    )(yf, oh, gf, gh, x, g1, sc2, sh2, wuf, wuh, wo, ln_g, ln_b, wr, br)


def _experts_kernel(te_ref, tv_ref, src_ref, h_hbm, wg_ref, wu_ref, wd_ref, o_ref, xa, xb, sems, *, tm):
    i = pl.program_id(0)
    last = pl.num_programs(0) - 1
    valid = tv_ref[i] != 0
    bufs = (xa, xb)

    def issue(tile, slot):
        base = tile * tm
        for r in range(tm):
            row0 = pl.multiple_of(src_ref[base + r], ROW_TILE)
            pltpu.make_async_copy(h_hbm.at[pl.ds(row0, ROW_TILE)], bufs[slot].at[pl.ds(r * ROW_TILE, ROW_TILE)],
                                  sems.at[slot]).start(priority=r % 2)

    def wait(slot):
        pltpu.make_async_copy(h_hbm.at[pl.ds(0, tm * ROW_TILE)], bufs[slot], sems.at[slot]).wait()

    @pl.when((i == 0) & valid)
    def _():
        issue(0, 0)

    def step(slot):
        wait(slot)
        issue(jnp.minimum(i + 1, last), 1 - slot)
        x = _load_row_tiles(bufs[slot]).astype(BF16)
        g = jnp.dot(x, wg_ref[...].astype(BF16), preferred_element_type=F32)
        u = jnp.dot(x, wu_ref[...].astype(BF16), preferred_element_type=F32)
        hid = (_silu(g) * u).astype(BF16)
        _store_row_tiles(o_ref, jnp.dot(hid, wd_ref[...].astype(BF16), preferred_element_type=F32))

        @pl.when(i == last)
        def _():
            wait(1 - slot)

    for slot in range(2):
        pl.when(valid & (i % 2 == slot))(functools.partial(step, slot))

        @pl.when(jnp.logical_not(valid) & (i % 2 == slot) & (tv_ref[jnp.maximum(i - 1, 0)] != 0) & (i > 0))
        def _(slot=slot):
            wait(slot)

    @pl.when(jnp.logical_not(valid))
    def _():
        o_ref[...] = jnp.zeros_like(o_ref)


def _experts(tile_expert, tile_valid, src_tok, h2, wg, wu, wd, tm):
    E, D, FF = wg.shape
    assert D == ROW_TILE * LANES
    ntiles = tile_expert.shape[0]
    grid_spec = pltpu.PrefetchScalarGridSpec(
        num_scalar_prefetch=3,
        grid=(ntiles,),
        in_specs=[pl.BlockSpec(memory_space=pl.ANY),
                  pl.BlockSpec((None, D, FF), lambda i, te, tv, src: (te[i], 0, 0)),
                  pl.BlockSpec((None, D, FF), lambda i, te, tv, src: (te[i], 0, 0)),
                  pl.BlockSpec((None, FF, D), lambda i, te, tv, src: (te[i], 0, 0))],
        out_specs=pl.BlockSpec((tm * ROW_TILE, LANES), lambda i, te, tv, src: (i, 0)),
        scratch_shapes=[pltpu.VMEM((tm * ROW_TILE, LANES), F32), pltpu.VMEM((tm * ROW_TILE, LANES), F32),
                        pltpu.SemaphoreType.DMA((2,))],
    )
    return pl.pallas_call(
        functools.partial(_experts_kernel, tm=tm),
        out_shape=jax.ShapeDtypeStruct((ntiles * tm * ROW_TILE, LANES), F32),
        grid_spec=grid_spec,
        compiler_params=_cparams(("arbitrary",)),
    )(tile_expert, tile_valid, src_tok, h2, wg, wu, wd)


def _combine_kernel(pos_ref, ys_hbm, x1_ref, ri_ref, g2_ref, lg_ref, lb_ref, o_ref, ya, yb, sems, *, alpha, tm):
    i = pl.program_id(0)
    last = pl.num_programs(0) - 1
    bufs = (ya, yb)

    def issue(tile, slot):
        base = tile * (2 * tm)
        for r in range(tm):
            for k in range(2):
                row0 = pl.multiple_of(pos_ref[base + 2 * r + k], ROW_TILE)
                pltpu.make_async_copy(ys_hbm.at[pl.ds(row0, ROW_TILE)],
                                      bufs[slot].at[k, pl.ds(r * ROW_TILE, ROW_TILE)],
                                      sems.at[slot]).start(priority=k)

    def wait(slot):
        for k in range(2):
            pltpu.make_async_copy(ys_hbm.at[pl.ds(0, tm * ROW_TILE)], bufs[slot].at[k], sems.at[slot]).wait()

    @pl.when(i == 0)
    def _():
        issue(0, 0)

    def step(slot):
        wait(slot)
        issue(jnp.minimum(i + 1, last), 1 - slot)
        ri = ri_ref[...]
        y = ri[:, 0:1] * _load_row_tiles(bufs[slot].at[0]) + ri[:, 1:2] * _load_row_tiles(bufs[slot].at[1])
        o_ref[...] = _layer_norm(alpha * x1_ref[...] + g2_ref[...] * y, lg_ref[...], lb_ref[...])

        @pl.when(i == last)
        def _():
            wait(1 - slot)

    for slot in range(2):
        pl.when(i % 2 == slot)(functools.partial(step, slot))


def _combine(pos, ys, x1, rinfo, g2, ln_g, ln_b, alpha, tm=128):
    B, S, D = x1.shape
    assert D == ROW_TILE * LANES
    nb = S // tm
    grid_spec = pltpu.PrefetchScalarGridSpec(
        num_scalar_prefetch=1,
        grid=(B * nb,),
        in_specs=[pl.BlockSpec(memory_space=pl.ANY),
                  pl.BlockSpec((None, tm, D), lambda t, pos: (t // nb, t % nb, 0)),
                  pl.BlockSpec((None, tm, LANES), lambda t, pos: (t // nb, t % nb, 0)),
                  pl.BlockSpec((None, 1, D), lambda t, pos: (t // nb, 0, 0)),
                  pl.BlockSpec((1, D), lambda t, pos: (0, 0)),
                  pl.BlockSpec((1, D), lambda t, pos: (0, 0))],
        out_specs=pl.BlockSpec((None, tm, D), lambda t, pos: (t // nb, t % nb, 0)),
        scratch_shapes=[pltpu.VMEM((2, tm * ROW_TILE, LANES), F32), pltpu.VMEM((2, tm * ROW_TILE, LANES), F32),
                        pltpu.SemaphoreType.DMA((2,))],
    )
    return pl.pallas_call(
        functools.partial(_combine_kernel, alpha=alpha, tm=tm),
        out_shape=jax.ShapeDtypeStruct((B, S, D), F32),
        grid_spec=grid_spec,
        compiler_params=_cparams(("arbitrary",)),
    )(pos, ys, x1, rinfo, g2, ln_g, ln_b)


def kernel(x, c, w_ada, b_ada, w_in, b_fox_forget, hgrn_lb_logits, hgrn_norm_w, w_up_fox, w_up_hgrn, w_out,
           ln1_g, ln1_b, w_router_group, b_router_group, w_router_expert, b_router_expert,
           w_expert_gate, w_expert_up, w_expert_down, ln2_g, ln2_b):
    B, S, D = x.shape
    depth = w_ada.shape[0]
    assert depth == 1, "single-layer block"
    fox_heads = b_fox_forget.shape[1]
    fox_w = fox_heads * HEAD_DIM
    hgrn_w = hgrn_norm_w.shape[1]
    ngroups = w_router_group.shape[2]
    nexp = w_router_expert.shape[2]
    nper = nexp // ngroups
    alpha = (2 * depth) ** 0.25
    T = B * S

    ada = _ada(c, w_ada[0], b_ada[0])
    sh1, sc1, g1, sh2, sc2, g2 = [a.reshape(B, 1, D) for a in jnp.split(ada, 6, axis=-1)]

    wi = w_in[0]
    o_ff = 3 * fox_w
    w_packed = jnp.concatenate(
        [wi[:, :o_ff + fox_heads], jnp.zeros((D, LANES - fox_heads), wi.dtype), wi[:, o_ff + fox_heads:]],
        axis=1).astype(BF16)
    widths = [fox_w, fox_w, fox_w, LANES, hgrn_w, hgrn_w, hgrn_w, hgrn_w, D, D]
    segs, off = [], 0
    for w in widths:
        segs.append((off, off + w))
        off += w
    fq, fk, fv, ffp, hq, hf, hi, hg, gf, gh = _inproj(x, sc1, sh1, w_packed, segs)

    bias_p = jnp.zeros((1, LANES), F32).at[0, :fox_heads].set(b_fox_forget[0])
    cum = _foxcum(ffp, bias_p)
    y_fox = _fox(fq, fk, fv, cum)

    o_h = _hgrn(hq, hf, hi, hg, hgrn_lb_logits, hgrn_norm_w[0])

    wr = jnp.zeros((D, LANES), F32).at[:, :ngroups].set(w_router_group[0]).at[:, ngroups:ngroups + nexp].set(
        w_router_expert[0])
    br = jnp.zeros((1, LANES), F32).at[0, :ngroups].set(b_router_group[0]).at[0, ngroups:ngroups + nexp].set(
        b_router_expert[0])
    x1, h2, rinfo, counts = _mix(
        y_fox, o_h, gf, gh, x, g1, sc2, sh2,
        w_up_fox[0].astype(BF16), w_up_hgrn[0].astype(BF16), w_out[0].astype(BF16),
        ln1_g[0].reshape(1, D), ln1_b[0].reshape(1, D), wr, br, alpha, ngroups, nper)

    tm_e = 256
    ntiles = (2 * T) // tm_e + nexp
    cnt = counts[0, :nexp].astype(jnp.int32)
    padded = ((cnt + tm_e - 1) // tm_e) * tm_e
    ends = jnp.cumsum(padded)
    starts = ends - padded
    ri = rinfo.reshape(T, LANES)
    eid = ri[:, 2:4].astype(jnp.int32)
    rank = ri[:, 4:6].astype(jnp.int32)
    pos = (starts[eid] + rank).reshape(-1)
    tile_start = jnp.arange(ntiles, dtype=jnp.int32) * tm_e
    tile_expert = jnp.minimum(jnp.sum((tile_start[:, None] >= ends[None, :]).astype(jnp.int32), axis=1), nexp - 1)
    tile_valid = (tile_start < ends[-1]).astype(jnp.int32)
    tok_ids = jnp.repeat(jnp.arange(T, dtype=jnp.int32), 2)
    src_row = jnp.zeros((ntiles * tm_e,), jnp.int32).at[pos].set(tok_ids * ROW_TILE)

    ys = _experts(tile_expert, tile_valid, src_row, h2,
                  w_expert_gate[0], w_expert_up[0], w_expert_down[0], tm_e)
    return _combine(pos * ROW_TILE, ys, x1, rinfo, g2, ln2_g[0].reshape(1, D), ln2_b[0].reshape(1, D), alpha)
```

```python
import functools

import jax
import jax.numpy as jnp
from jax import lax
from jax.experimental import pallas as pl
from jax.experimental.pallas import tpu as pltpu

F32 = jnp.float32
BF16 = jnp.bfloat16
HIGHEST = lax.Precision.HIGHEST

LANES = 128
HEAD_DIM = 64
LN_EPS = 1e-5
RMS_EPS = 1e-6
LOG2E = 1.4426950408889634
NEG_BIG = -1e30
HCHUNK = 16
ROW_TILE = 8
VMEM_LIMIT = 56 * 1024 * 1024


def _cparams(sem, vmem=VMEM_LIMIT):
    return pltpu.CompilerParams(dimension_semantics=sem, vmem_limit_bytes=vmem)


def _sigmoid(x):
    return 1.0 / (1.0 + jnp.exp(-x))


def _silu(x):
    return x * _sigmoid(x)


def _ada_kernel(c_ref, w_ref, b_ref, o_ref):
    c = c_ref[...]
    o_ref[...] = jnp.dot(_silu(c), w_ref[...], precision=HIGHEST,
                         preferred_element_type=F32) + b_ref[...]


def _ada(c, w_ada, b_ada):
    B, D = c.shape
    N = w_ada.shape[1]
    tn = 1024
    return pl.pallas_call(
        _ada_kernel,
        out_shape=jax.ShapeDtypeStruct((B, N), F32),
        grid=(N // tn,),
        in_specs=[pl.BlockSpec((B, D), lambda j: (0, 0)),
                  pl.BlockSpec((D, tn), lambda j: (0, j)),
                  pl.BlockSpec((1, tn), lambda j: (0, j))],
        out_specs=pl.BlockSpec((B, tn), lambda j: (0, j)),
        compiler_params=_cparams(("arbitrary",)),
    )(c, w_ada, b_ada.reshape(1, N))


def _inproj_kernel(x_ref, sc_ref, sh_ref, w_ref,
                   fq_ref, fk_ref, fv_ref, ff_ref, hq_ref, hf_ref, hi_ref, hg_ref, gf_ref, gh_ref,
                   *, segs, q_scale):
    h = (x_ref[...] * (1.0 + sc_ref[...]) + sh_ref[...]).astype(BF16)
    outs = (fq_ref, fk_ref, fv_ref, ff_ref, hq_ref, hf_ref, hi_ref, hg_ref, gf_ref, gh_ref)
    for idx, (o_ref, (a, b)) in enumerate(zip(outs, segs)):
        r = jnp.dot(h, w_ref[:, a:b], preferred_element_type=F32)
        if idx == 0:
            r = r * q_scale
        o_ref[...] = r.astype(o_ref.dtype)


def _inproj(x, sc1, sh1, w_packed, segs, tm=256):
    B, S, D = x.shape
    widths = [b - a for a, b in segs]
    dtypes = [BF16, BF16, BF16, F32, BF16, F32, BF16, BF16, BF16, BF16]
    out_shape = tuple(jax.ShapeDtypeStruct((B, S, w), dt) for w, dt in zip(widths, dtypes))
    out_specs = tuple(pl.BlockSpec((None, tm, w), lambda b, i: (b, i, 0)) for w in widths)
    vec = pl.BlockSpec((None, 1, D), lambda b, i: (b, 0, 0))
    return pl.pallas_call(
        functools.partial(_inproj_kernel, segs=tuple(segs), q_scale=HEAD_DIM ** -0.5 * LOG2E),
        out_shape=out_shape,
        grid=(B, S // tm),
        in_specs=[pl.BlockSpec((None, tm, D), lambda b, i: (b, i, 0)), vec, vec,
                  pl.BlockSpec(w_packed.shape, lambda b, i: (0, 0))],
        out_specs=out_specs,
        compiler_params=_cparams(("parallel", "parallel")),
    )(x, sc1, sh1, w_packed)


def _foxcum_kernel(ff_ref, b_ref, o_ref, *, blk):
    S = ff_ref.shape[0]
    r = lax.broadcasted_iota(jnp.int32, (blk, blk), 0)
    c = lax.broadcasted_iota(jnp.int32, (blk, blk), 1)
    lower = (r >= c).astype(F32)
    carry = jnp.zeros((1, LANES), F32)
    for j in range(S // blk):
        z = ff_ref[j * blk:(j + 1) * blk, :] + b_ref[...]
        lf = jnp.minimum(z, 0.0) - jnp.log(1.0 + jnp.exp(-jnp.abs(z)))
        cum = jnp.dot(lower, lf, precision=HIGHEST, preferred_element_type=F32) + carry
        o_ref[j * blk:(j + 1) * blk, :] = cum * LOG2E
        carry = cum[blk - 1:blk, :]


def _foxcum(ffp, bias_p, blk=256):
    B, S, _ = ffp.shape
    return pl.pallas_call(
        functools.partial(_foxcum_kernel, blk=blk),
        out_shape=jax.ShapeDtypeStruct((B, S, LANES), F32),
        grid=(B,),
        in_specs=[pl.BlockSpec((None, S, LANES), lambda b: (b, 0, 0)),
                  pl.BlockSpec((1, LANES), lambda b: (0, 0))],
        out_specs=pl.BlockSpec((None, S, LANES), lambda b: (b, 0, 0)),
        compiler_params=_cparams(("parallel",)),
    )(ffp, bias_p)


NCUM = 3


def _fox_kernel(q_ref, k_ref, v_ref, c_ref, o_ref, ka_sc, kb_sc, va_sc, vb_sc, *, tq, tk):
    p = pl.program_id(1)
    qi = pl.program_id(2)
    S = k_ref.shape[0]

    @pl.when(qi == 0)
    def _():
        lane = lax.broadcasted_iota(jnp.int32, (S, LANES), 1)
        rr = lax.broadcasted_iota(jnp.int32, (LANES, LANES), 0)
        cc = lax.broadcasted_iota(jnp.int32, (LANES, LANES), 1)
        rest = c_ref[...]
        placed = jnp.zeros((S, LANES), F32)
        for i in range(NCUM):
            piece = rest.astype(BF16)
            rest = rest - piece.astype(F32)
            sel = ((rr == 2 * p) & (cc == HEAD_DIM + i)) | ((rr == 2 * p + 1) & (cc == i))
            placed = placed + jnp.dot(piece, jnp.where(sel, 1.0, 0.0).astype(BF16), preferred_element_type=F32)
        k2 = k_ref[...].astype(F32)
        ka_sc[...] = jnp.where(lane < HEAD_DIM, k2, -placed).astype(BF16)
        kb_sc[...] = jnp.where(lane >= HEAD_DIM, k2, -placed).astype(BF16)
        vt = v_ref[...].astype(F32).T
        row = lax.broadcasted_iota(jnp.int32, (LANES, S), 0)
        va_sc[...] = jnp.where(row < HEAD_DIM, vt, jnp.where(row == HEAD_DIM, 1.0, 0.0)).astype(BF16)
        vb_sc[...] = jnp.where(row >= HEAD_DIM, vt, jnp.where(row == 0, 1.0, 0.0)).astype(BF16)

    q2 = q_ref[...].astype(F32)
    qlane = lax.broadcasted_iota(jnp.int32, (tq, LANES), 1)
    qa = jnp.where(qlane < HEAD_DIM, q2, jnp.where(qlane < HEAD_DIM + NCUM, 1.0, 0.0)).astype(BF16)
    qb = jnp.where(qlane >= HEAD_DIM, q2, jnp.where(qlane < NCUM, 1.0, 0.0)).astype(BF16)
    krow = lax.broadcasted_iota(jnp.int32, (tk, tq), 0)
    qcol = lax.broadcasted_iota(jnp.int32, (tk, tq), 1)
    nsub = tq // tk

    def block(k0, carry, diag_off):
        out = []
        for ksc, vsc, qh, (m, acc) in ((ka_sc, va_sc, qa, carry[:2]), (kb_sc, vb_sc, qb, carry[2:])):
            st = lax.dot_general(ksc[pl.ds(k0, tk), :], qh, (((1,), (1,)), ((), ())),
                                 preferred_element_type=F32)
            if diag_off is not None:
                st = jnp.where(krow + diag_off <= qcol, st, NEG_BIG)
            m_new = jnp.maximum(m, jnp.max(st, axis=0, keepdims=True))
            pt = jnp.exp2(st - m_new).astype(BF16)
            acc = jnp.exp2(m - m_new) * acc + jnp.dot(vsc[:, pl.ds(k0, tk)], pt, preferred_element_type=F32)
            out += [m_new, acc]
        return tuple(out)

    def pair(j, carry):
        k0 = pl.multiple_of(j * (2 * tk), 2 * tk)
        return block(k0 + tk, block(k0, carry, None), None)

    m0 = jnp.full((1, tq), NEG_BIG, F32)
    a0 = jnp.zeros((LANES, tq), F32)
    carry = lax.fori_loop(0, qi * (nsub // 2), pair, (m0, a0, m0, a0))
    for d in range(nsub):
        carry = block(pl.multiple_of(qi * tq + d * tk, tk), carry, d * tk)
    _, aa, _, ab = carry
    row = lax.broadcasted_iota(jnp.int32, (LANES, tq), 0)
    ot = jnp.where(row < HEAD_DIM, aa * (1.0 / aa[HEAD_DIM:HEAD_DIM + 1, :]), ab * (1.0 / ab[0:1, :]))
    o_ref[...] = ot.T.astype(o_ref.dtype)


def _fox(fq, fk, fv, cum, tq=512, tk=256):
    B, S, W = fq.shape
    assert tq % (2 * tk) == 0 and S % tq == 0
    npairs = W // LANES
    return pl.pallas_call(
        functools.partial(_fox_kernel, tq=tq, tk=tk),
        out_shape=jax.ShapeDtypeStruct((B, S, W), BF16),
        grid=(B, npairs, S // tq),
        in_specs=[pl.BlockSpec((None, tq, LANES), lambda b, p, i: (b, i, p)),
                  pl.BlockSpec((None, S, LANES), lambda b, p, i: (b, 0, p)),
                  pl.BlockSpec((None, S, LANES), lambda b, p, i: (b, 0, p)),
                  pl.BlockSpec((None, S, LANES), lambda b, p, i: (b, 0, 0))],
        out_specs=pl.BlockSpec((None, tq, LANES), lambda b, p, i: (b, i, p)),
        scratch_shapes=[pltpu.VMEM((S, LANES), BF16), pltpu.VMEM((S, LANES), BF16),
                        pltpu.VMEM((LANES, S), BF16), pltpu.VMEM((LANES, S), BF16)],
        compiler_params=_cparams(("parallel", "parallel", "arbitrary")),
    )(fq, fk, fv, cum)


def _hgrn_kernel(hq_ref, hf_ref, hi_ref, hg_ref, lb_ref, nw_ref, o_ref,
                 a_sc, qt_sc, kt_sc, kk_sc, qq_sc, p_sc, s_sc, o_sc, stb_sc, dec_sc):
    S = hq_ref.shape[0]
    C = HCHUNK
    nchunks = S // C

    lg = lb_ref[...]
    e = jnp.exp(lg - jnp.max(lg, axis=0, keepdims=True))
    lb = e[0:1, :] / jnp.sum(e, axis=0, keepdims=True)

    f = lb + (1.0 - lb) * _sigmoid(hf_ref[...])
    lf = jnp.log(f)
    kk = 1.0 - f
    qq = _silu(hq_ref[...].astype(F32))

    rmod = lax.broadcasted_iota(jnp.int32, (S, LANES), 0) & (C - 1)
    a = lf
    d = 1
    while d < C:
        a = a + jnp.where(rmod >= d, pltpu.roll(a, d, axis=0), 0.0)
        d *= 2
    a3 = a.reshape(nchunks, C, LANES)
    alast = jnp.broadcast_to(a3[:, C - 1:C, :], (nchunks, C, LANES)).reshape(S, LANES)
    a_sc[...] = a * LOG2E
    kk_sc[...] = kk
    qq_sc[...] = qq
    qt_sc[...] = (qq * jnp.exp(a)).astype(BF16)
    kt_sc[...] = (kk * jnp.exp(alast - a)).astype(BF16)
    dec_sc[...] = jnp.exp(a3[:, C - 1, :])

    lane = lax.broadcasted_iota(jnp.int32, (C, LANES), 1)
    trow = lax.broadcasted_iota(jnp.int32, (C, LANES), 0)

    def gen(c, _):
        r0 = pl.multiple_of(c * C, C)
        ac = a_sc[pl.ds(r0, C), :]
        qc = qq_sc[pl.ds(r0, C), :]
        kc = kk_sc[pl.ds(r0, C), :]
        for s in range(C):
            dec = jnp.exp2(jnp.where(trow >= s, ac - ac[s:s + 1, :], NEG_BIG))
            p_sc[pl.ds(r0, C), s * LANES:(s + 1) * LANES] = (qc * (kc[s:s + 1, :] * dec)).astype(BF16)
        return 0

    lax.fori_loop(0, nchunks, gen, 0)

    er = lax.broadcasted_iota(jnp.int32, (C * LANES, LANES), 0)
    ec = lax.broadcasted_iota(jnp.int32, (C * LANES, LANES), 1)
    emat = (ec == ((er & (LANES - 1)) // HEAD_DIM) * C + er // LANES).astype(BF16)
    rb = 256

    def red(i, _):
        r0 = pl.multiple_of(i * rb, rb)
        s_sc[pl.ds(r0, rb), :] = jnp.dot(p_sc[pl.ds(r0, rb), :], emat,
                                         preferred_element_type=F32).astype(BF16)
        return 0

    lax.fori_loop(0, S // rb, red, 0)

    sr = lax.broadcasted_iota(jnp.int32, (LANES, LANES), 0)
    scn = lax.broadcasted_iota(jnp.int32, (LANES, LANES), 1)
    same_head = (sr // HEAD_DIM) == (scn // HEAD_DIM)
    unroll = 16

    def scan(g, st):
        for u in range(unroll):
            c = g * unroll + u
            r0 = pl.multiple_of(c * C, C)
            stb_sc[c] = st.astype(BF16)
            upd = lax.dot_general(hi_ref[pl.ds(r0, C), :], kt_sc[pl.ds(r0, C), :], (((0,), (0,)), ((), ())),
                                  preferred_element_type=F32)
            st = st * dec_sc[pl.ds(c, 1), :] + jnp.where(same_head, upd, 0.0)
        return st

    lax.fori_loop(0, nchunks // unroll, scan, jnp.zeros((LANES, LANES), F32))

    def readout(g, _):
        for u in range(unroll):
            c = g * unroll + u
            r0 = pl.multiple_of(c * C, C)
            vc = hi_ref[pl.ds(r0, C), :]
            o_inter = lax.dot_general(qt_sc[pl.ds(r0, C), :], stb_sc[c],
                                      (((1,), (1,)), ((), ())), preferred_element_type=F32)
            v2 = jnp.concatenate([jnp.where(lane < HEAD_DIM, vc, jnp.zeros_like(vc)),
                                  jnp.where(lane >= HEAD_DIM, vc, jnp.zeros_like(vc))], axis=0)
            o_intra = jnp.dot(s_sc[pl.ds(r0, C), :][:, :2 * C], v2, preferred_element_type=F32)
            o_sc[pl.ds(r0, C), :] = o_inter + o_intra
        return 0

    lax.fori_loop(0, nchunks // unroll, readout, 0)

    o = o_sc[...]
    ones_head = jnp.where(same_head, 1.0 / HEAD_DIM, 0.0).astype(F32)
    ms = jnp.dot(o * o, ones_head, precision=HIGHEST, preferred_element_type=F32)
    y = o * lax.rsqrt(ms + RMS_EPS) * nw_ref[...]
    o_ref[...] = (y * _silu(hg_ref[...].astype(F32))).astype(o_ref.dtype)


def _hgrn(hq, hf, hi, hg, lb_logits, norm_w):
    B, S, W = hq.shape
    npairs = W // LANES
    nrows = lb_logits.shape[0]
    seq = pl.BlockSpec((None, S, LANES), lambda b, p: (b, 0, p))
    return pl.pallas_call(
        _hgrn_kernel,
        out_shape=jax.ShapeDtypeStruct((B, S, W), BF16),
        grid=(B, npairs),
        in_specs=[seq, seq, seq, seq,
                  pl.BlockSpec((nrows, LANES), lambda b, p: (0, p)),
                  pl.BlockSpec((1, LANES), lambda b, p: (0, p))],
        out_specs=seq,
        scratch_shapes=[pltpu.VMEM((S, LANES), F32),
                        pltpu.VMEM((S, LANES), BF16),
                        pltpu.VMEM((S, LANES), BF16),
                        pltpu.VMEM((S, LANES), F32),
                        pltpu.VMEM((S, LANES), F32),
                        pltpu.VMEM((S, HCHUNK * LANES), BF16),
                        pltpu.VMEM((S, LANES), BF16),
                        pltpu.VMEM((S, LANES), F32),
                        pltpu.VMEM((S // HCHUNK, LANES, LANES), BF16),
                        pltpu.VMEM((S // HCHUNK, LANES), F32)],
        compiler_params=_cparams(("parallel", "parallel")),
    )(hq, hf, hi, hg, lb_logits, norm_w.reshape(1, W))


def _layer_norm(v, g, b):
    mu = jnp.mean(v, axis=-1, keepdims=True)
    d = v - mu
    var = jnp.mean(d * d, axis=-1, keepdims=True)
    return d * lax.rsqrt(var + LN_EPS) * g + b


def _store_row_tiles(ref, val):
    n, d = val.shape
    dt = d // LANES
    for j in range(dt):
        ref[pl.ds(j, n, stride=dt), :] = val[:, j * LANES:(j + 1) * LANES]


def _load_row_tiles(ref):
    dt = ROW_TILE
    n = ref.shape[0] // dt
    return jnp.concatenate([ref[pl.ds(j, n, stride=dt), :] for j in range(dt)], axis=1)


def _mix_kernel(yf_ref, oh_ref, gf_ref, gh_ref, x_ref, g1_ref, sc2_ref, sh2_ref,
                wuf_ref, wuh_ref, wo_ref, lg_ref, lbias_ref, wr_ref, br_ref,
                x1_ref, h2_ref, ri_ref, cnt_ref, carry_sc, *, alpha, ngroups, nper):
    first = (pl.program_id(0) == 0) & (pl.program_id(1) == 0)

    @pl.when(first)
    def _():
        carry_sc[...] = jnp.zeros_like(carry_sc)

    tm = x_ref.shape[0]
    yf = jnp.dot(yf_ref[...], wuf_ref[...], preferred_element_type=F32)
    yh = jnp.dot(oh_ref[...], wuh_ref[...], preferred_element_type=F32)
    merged = _sigmoid(gf_ref[...].astype(F32)) * yf + _sigmoid(gh_ref[...].astype(F32)) * yh
    y = jnp.dot(merged.astype(BF16), wo_ref[...], preferred_element_type=F32)
    x1 = _layer_norm(alpha * x_ref[...] + g1_ref[...] * y, lg_ref[...], lbias_ref[...])
    x1_ref[...] = x1
    h2 = x1 * (1.0 + sc2_ref[...]) + sh2_ref[...]
    _store_row_tiles(h2_ref, h2)

    logits = jnp.dot(h2, wr_ref[...], precision=HIGHEST, preferred_element_type=F32) + br_ref[...]
    lane = lax.broadcasted_iota(jnp.int32, (tm, LANES), 1)
    big = jnp.int32(1 << 20)

    def argmax_first(vals, mask):
        mx = jnp.max(jnp.where(mask, vals, -jnp.inf), axis=1, keepdims=True)
        idx = jnp.min(jnp.where(mask & (vals == mx), lane, big), axis=1, keepdims=True)
        return mx, idx

    gmask = lane < ngroups
    gmax = jnp.max(jnp.where(gmask, logits, -jnp.inf), axis=1, keepdims=True)
    gexp = jnp.where(gmask, jnp.exp(logits - gmax), 0.0)
    gprob = gexp / jnp.sum(gexp, axis=1, keepdims=True)
    g_w, g_idx = argmax_first(gprob, gmask)

    lo = ngroups + g_idx * nper
    emask = (lane >= lo) & (lane < lo + nper)
    emax = jnp.max(jnp.where(emask, logits, -jnp.inf), axis=1, keepdims=True)
    eexp = jnp.where(emask, jnp.exp(logits - emax), 0.0)
    eprob = eexp / jnp.sum(eexp, axis=1, keepdims=True)
    p0, i0 = argmax_first(eprob, emask)
    p1, i1 = argmax_first(eprob, emask & (lane != i0))
    den = p0 + p1
    w0 = p0 / den * g_w
    w1 = p1 / den * g_w
    e0 = i0 - ngroups
    e1 = i1 - ngroups

    oh = ((lane == e0) | (lane == e1)).astype(F32)
    r = lax.broadcasted_iota(jnp.int32, (tm, tm), 0)
    c = lax.broadcasted_iota(jnp.int32, (tm, tm), 1)
    strict_lower = (c < r).astype(BF16)
    before = jnp.dot(strict_lower, oh.astype(BF16), preferred_element_type=F32) + carry_sc[...]
    rank0 = jnp.sum(jnp.where(lane == e0, before, 0.0), axis=1, keepdims=True)
    rank1 = jnp.sum(jnp.where(lane == e1, before, 0.0), axis=1, keepdims=True)
    carry_sc[...] = carry_sc[...] + jnp.sum(oh, axis=0, keepdims=True)
    cnt_ref[...] = carry_sc[...]

    info = jnp.where(lane == 0, w0, 0.0)
    info = jnp.where(lane == 1, w1, info)
    info = jnp.where(lane == 2, e0.astype(F32), info)
    info = jnp.where(lane == 3, e1.astype(F32), info)
    info = jnp.where(lane == 4, rank0, info)
    info = jnp.where(lane == 5, rank1, info)
    ri_ref[...] = info


def _mix(yf, oh, gf, gh, x, g1, sc2, sh2, wuf, wuh, wo, ln_g, ln_b, wr, br, alpha, ngroups, nper, tm=256):
    B, S, D = x.shape
    W = yf.shape[2]
    tok = lambda w: pl.BlockSpec((None, tm, w), lambda b, i: (b, i, 0))
    vec = pl.BlockSpec((None, 1, D), lambda b, i: (b, 0, 0))
    full = lambda a: pl.BlockSpec(a.shape, lambda b, i: (0,) * a.ndim)
    return pl.pallas_call(
        functools.partial(_mix_kernel, alpha=alpha, ngroups=ngroups, nper=nper),
        out_shape=(jax.ShapeDtypeStruct((B, S, D), F32),
                   jax.ShapeDtypeStruct((B * S * (D // LANES), LANES), F32),
                   jax.ShapeDtypeStruct((B, S, LANES), F32),
                   jax.ShapeDtypeStruct((1, LANES), F32)),
        grid=(B, S // tm),
        in_specs=[tok(W), tok(W), tok(D), tok(D), tok(D), vec, vec, vec,
                  full(wuf), full(wuh), full(wo), full(ln_g), full(ln_b), full(wr), full(br)],
        out_specs=(tok(D),
                   pl.BlockSpec((tm * (D // LANES), LANES), lambda b, i: (b * (S // tm) + i, 0)),
                   tok(LANES), pl.BlockSpec((1, LANES), lambda b, i: (0, 0))),
        scratch_shapes=[pltpu.VMEM((1, LANES), F32)],
        compiler_params=_cparams(("arbitrary", "arbitrary")),
    )(yf, oh, gf, gh, x, g1, sc2, sh2, wuf, wuh, wo, ln_g, ln_b, wr, br)


def _experts_kernel(te_ref, tv_ref, src_ref, h_hbm, wg_ref, wu_ref, wd_ref, o_ref, xa, xb, sems, *, tm):
    i = pl.program_id(0)
    last = pl.num_programs(0) - 1
    valid = tv_ref[i] != 0
    bufs = (xa, xb)

    def issue(tile, slot):
        base = tile * tm
        for r in range(tm):
            row0 = pl.multiple_of(src_ref[base + r], ROW_TILE)
            pltpu.make_async_copy(h_hbm.at[pl.ds(row0, ROW_TILE)], bufs[slot].at[pl.ds(r * ROW_TILE, ROW_TILE)],
                                  sems.at[slot]).start(priority=r % 2)

    def wait(slot):
        pltpu.make_async_copy(h_hbm.at[pl.ds(0, tm * ROW_TILE)], bufs[slot], sems.at[slot]).wait()

    @pl.when((i == 0) & valid)
    def _():
        issue(0, 0)

    def step(slot):
        wait(slot)
        issue(jnp.minimum(i + 1, last), 1 - slot)
        x = _load_row_tiles(bufs[slot]).astype(BF16)
        g = jnp.dot(x, wg_ref[...].astype(BF16), preferred_element_type=F32)
        u = jnp.dot(x, wu_ref[...].astype(BF16), preferred_element_type=F32)
        hid = (_silu(g) * u).astype(BF16)
        _store_row_tiles(o_ref, jnp.dot(hid, wd_ref[...].astype(BF16), preferred_element_type=F32))

        @pl.when(i == last)
        def _():
            wait(1 - slot)

    for slot in range(2):
        pl.when(valid & (i % 2 == slot))(functools.partial(step, slot))

        @pl.when(jnp.logical_not(valid) & (i % 2 == slot) & (tv_ref[jnp.maximum(i - 1, 0)] != 0) & (i > 0))
        def _(slot=slot):
            wait(slot)

    @pl.when(jnp.logical_not(valid))
    def _():
        o_ref[...] = jnp.zeros_like(o_ref)


def _experts(tile_expert, tile_valid, src_tok, h2, wg, wu, wd, tm):
    E, D, FF = wg.shape
    assert D == ROW_TILE * LANES
    ntiles = tile_expert.shape[0]
    grid_spec = pltpu.PrefetchScalarGridSpec(
        num_scalar_prefetch=3,
        grid=(ntiles,),
        in_specs=[pl.BlockSpec(memory_space=pl.ANY),
                  pl.BlockSpec((None, D, FF), lambda i, te, tv, src: (te[i], 0, 0)),
                  pl.BlockSpec((None, D, FF), lambda i, te, tv, src: (te[i], 0, 0)),
                  pl.BlockSpec((None, FF, D), lambda i, te, tv, src: (te[i], 0, 0))],
        out_specs=pl.BlockSpec((tm * ROW_TILE, LANES), lambda i, te, tv, src: (i, 0)),
        scratch_shapes=[pltpu.VMEM((tm * ROW_TILE, LANES), F32), pltpu.VMEM((tm * ROW_TILE, LANES), F32),
                        pltpu.SemaphoreType.DMA((2,))],
    )
    return pl.pallas_call(
        functools.partial(_experts_kernel, tm=tm),
        out_shape=jax.ShapeDtypeStruct((ntiles * tm * ROW_TILE, LANES), F32),
        grid_spec=grid_spec,
        compiler_params=_cparams(("arbitrary",)),
    )(tile_expert, tile_valid, src_tok, h2, wg, wu, wd)


def _combine_kernel(pos_ref, ys_hbm, x1_ref, ri_ref, g2_ref, lg_ref, lb_ref, o_ref, ya, yb, sems, *, alpha, tm):
    i = pl.program_id(0)
    last = pl.num_programs(0) - 1
    bufs = (ya, yb)

    def issue(tile, slot):
        base = tile * (2 * tm)
        for r in range(tm):
            for k in range(2):
                row0 = pl.multiple_of(pos_ref[base + 2 * r + k], ROW_TILE)
                pltpu.make_async_copy(ys_hbm.at[pl.ds(row0, ROW_TILE)],
                                      bufs[slot].at[k, pl.ds(r * ROW_TILE, ROW_TILE)],
                                      sems.at[slot]).start(priority=k)

    def wait(slot):
        for k in range(2):
            pltpu.make_async_copy(ys_hbm.at[pl.ds(0, tm * ROW_TILE)], bufs[slot].at[k], sems.at[slot]).wait()

    @pl.when(i == 0)
    def _():
        issue(0, 0)

    def step(slot):
        wait(slot)
        issue(jnp.minimum(i + 1, last), 1 - slot)
        ri = ri_ref[...]
        y = ri[:, 0:1] * _load_row_tiles(bufs[slot].at[0]) + ri[:, 1:2] * _load_row_tiles(bufs[slot].at[1])
        o_ref[...] = _layer_norm(alpha * x1_ref[...] + g2_ref[...] * y, lg_ref[...], lb_ref[...])

        @pl.when(i == last)
        def _():
            wait(1 - slot)

    for slot in range(2):
        pl.when(i % 2 == slot)(functools.partial(step, slot))


def _combine(pos, ys, x1, rinfo, g2, ln_g, ln_b, alpha, tm=128):
    B, S, D = x1.shape
    assert D == ROW_TILE * LANES
    nb = S // tm
    grid_spec = pltpu.PrefetchScalarGridSpec(
        num_scalar_prefetch=1,
        grid=(B * nb,),
        in_specs=[pl.BlockSpec(memory_space=pl.ANY),
                  pl.BlockSpec((None, tm, D), lambda t, pos: (t // nb, t % nb, 0)),
                  pl.BlockSpec((None, tm, LANES), lambda t, pos: (t // nb, t % nb, 0)),
                  pl.BlockSpec((None, 1, D), lambda t, pos: (t // nb, 0, 0)),
                  pl.BlockSpec((1, D), lambda t, pos: (0, 0)),
                  pl.BlockSpec((1, D), lambda t, pos: (0, 0))],
        out_specs=pl.BlockSpec((None, tm, D), lambda t, pos: (t // nb, t % nb, 0)),
        scratch_shapes=[pltpu.VMEM((2, tm * ROW_TILE, LANES), F32), pltpu.VMEM((2, tm * ROW_TILE, LANES), F32),
                        pltpu.SemaphoreType.DMA((2,))],
    )
    return pl.pallas_call(
        functools.partial(_combine_kernel, alpha=alpha, tm=tm),
        out_shape=jax.ShapeDtypeStruct((B, S, D), F32),
        grid_spec=grid_spec,
        compiler_params=_cparams(("arbitrary",)),
    )(pos, ys, x1, rinfo, g2, ln_g, ln_b)


def kernel(x, c, w_ada, b_ada, w_in, b_fox_forget, hgrn_lb_logits, hgrn_norm_w, w_up_fox, w_up_hgrn, w_out,
           ln1_g, ln1_b, w_router_group, b_router_group, w_router_expert, b_router_expert,
           w_expert_gate, w_expert_up, w_expert_down, ln2_g, ln2_b):
    B, S, D = x.shape
    depth = w_ada.shape[0]
    assert depth == 1, "single-layer block"
    fox_heads = b_fox_forget.shape[1]
    fox_w = fox_heads * HEAD_DIM
    hgrn_w = hgrn_norm_w.shape[1]
    ngroups = w_router_group.shape[2]
    nexp = w_router_expert.shape[2]
    nper = nexp // ngroups
    alpha = (2 * depth) ** 0.25
    T = B * S

    ada = _ada(c, w_ada[0], b_ada[0])
    sh1, sc1, g1, sh2, sc2, g2 = [a.reshape(B, 1, D) for a in jnp.split(ada, 6, axis=-1)]

    wi = w_in[0]
    o_ff = 3 * fox_w
    w_packed = jnp.concatenate(
        [wi[:, :o_ff + fox_heads], jnp.zeros((D, LANES - fox_heads), wi.dtype), wi[:, o_ff + fox_heads:]],
        axis=1).astype(BF16)
    widths = [fox_w, fox_w, fox_w, LANES, hgrn_w, hgrn_w, hgrn_w, hgrn_w, D, D]
    segs, off = [], 0
    for w in widths:
        segs.append((off, off + w))
        off += w
    fq, fk, fv, ffp, hq, hf, hi, hg, gf, gh = _inproj(x, sc1, sh1, w_packed, segs)

    bias_p = jnp.zeros((1, LANES), F32).at[0, :fox_heads].set(b_fox_forget[0])
    cum = _foxcum(ffp, bias_p)
    y_fox = _fox(fq, fk, fv, cum)

    o_h = _hgrn(hq, hf, hi, hg, hgrn_lb_logits, hgrn_norm_w[0])

    wr = jnp.zeros((D, LANES), F32).at[:, :ngroups].set(w_router_group[0]).at[:, ngroups:ngroups + nexp].set(
        w_router_expert[0])
    br = jnp.zeros((1, LANES), F32).at[0, :ngroups].set(b_router_group[0]).at[0, ngroups:ngroups + nexp].set(
        b_router_expert[0])
    x1, h2, rinfo, counts = _mix(
        y_fox, o_h, gf, gh, x, g1, sc2, sh2,
        w_up_fox[0].astype(BF16), w_up_hgrn[0].astype(BF16), w_out[0].astype(BF16),
        ln1_g[0].reshape(1, D), ln1_b[0].reshape(1, D), wr, br, alpha, ngroups, nper)

    tm_e = 256
    ntiles = (2 * T) // tm_e + nexp
    cnt = counts[0, :nexp].astype(jnp.int32)
    padded = ((cnt + tm_e - 1) // tm_e) * tm_e
    ends = jnp.cumsum(padded)
    starts = ends - padded
    ri = rinfo.reshape(T, LANES)
    eid = ri[:, 2:4].astype(jnp.int32)
    rank = ri[:, 4:6].astype(jnp.int32)
    pos = (starts[eid] + rank).reshape(-1)
    tile_start = jnp.arange(ntiles, dtype=jnp.int32) * tm_e
    tile_expert = jnp.minimum(jnp.sum((tile_start[:, None] >= ends[None, :]).astype(jnp.int32), axis=1), nexp - 1)
    tile_valid = (tile_start < ends[-1]).astype(jnp.int32)
    tok_ids = jnp.repeat(jnp.arange(T, dtype=jnp.int32), 2)
    src_row = jnp.zeros((ntiles * tm_e,), jnp.int32).at[pos].set(tok_ids * ROW_TILE)

    ys = _experts(tile_expert, tile_valid, src_row, h2,
                  w_expert_gate[0], w_expert_up[0], w_expert_down[0], tm_e)
    return _combine(pos * ROW_TILE, ys, x1, rinfo, g2, ln2_g[0].reshape(1, D), ln2_b[0].reshape(1, D), alpha)
```

```python
import functools

import jax
import jax.numpy as jnp
from jax import lax
from jax.experimental import pallas as pl
from jax.experimental.pallas import tpu as pltpu
from jax.experimental.pallas import tpu_sc as plsc

F32 = jnp.float32
BF16 = jnp.bfloat16
HIGHEST = lax.Precision.HIGHEST

LANES = 128
HEAD_DIM = 64
LN_EPS = 1e-5
RMS_EPS = 1e-6
LOG2E = 1.4426950408889634
NEG_BIG = -1e30
HCHUNK = 16
ROW_TILE = 8
VMEM_LIMIT = 56 * 1024 * 1024


def _cparams(sem, vmem=VMEM_LIMIT):
    return pltpu.CompilerParams(dimension_semantics=sem, vmem_limit_bytes=vmem)


def _sigmoid(x):
    return 1.0 / (1.0 + jnp.exp(-x))


def _silu(x):
    return x * _sigmoid(x)


def _ada_kernel(c_ref, w_ref, b_ref, o_ref):
    c = c_ref[...]
    o_ref[...] = jnp.dot(_silu(c), w_ref[...], precision=HIGHEST,
                         preferred_element_type=F32) + b_ref[...]


def _ada(c, w_ada, b_ada):
    B, D = c.shape
    N = w_ada.shape[1]
    tn = 1024
    return pl.pallas_call(
        _ada_kernel,
        out_shape=jax.ShapeDtypeStruct((B, N), F32),
        grid=(N // tn,),
        in_specs=[pl.BlockSpec((B, D), lambda j: (0, 0)),
                  pl.BlockSpec((D, tn), lambda j: (0, j)),
                  pl.BlockSpec((1, tn), lambda j: (0, j))],
        out_specs=pl.BlockSpec((B, tn), lambda j: (0, j)),
        compiler_params=_cparams(("arbitrary",)),
    )(c, w_ada, b_ada.reshape(1, N))


def _inproj_kernel(x_ref, sc_ref, sh_ref, w_ref,
                   fq_ref, fk_ref, fv_ref, ff_ref, hq_ref, hf_ref, hi_ref, hg_ref, gf_ref, gh_ref,
                   *, segs, q_scale):
    h = (x_ref[...] * (1.0 + sc_ref[...]) + sh_ref[...]).astype(BF16)
    outs = (fq_ref, fk_ref, fv_ref, ff_ref, hq_ref, hf_ref, hi_ref, hg_ref, gf_ref, gh_ref)
    for idx, (o_ref, (a, b)) in enumerate(zip(outs, segs)):
        r = jnp.dot(h, w_ref[:, a:b], preferred_element_type=F32)
        if idx == 0:
            r = r * q_scale
        o_ref[...] = r.astype(o_ref.dtype)


def _inproj(x, sc1, sh1, w_packed, segs, tm=256):
    B, S, D = x.shape
    widths = [b - a for a, b in segs]
    dtypes = [BF16, BF16, BF16, F32, BF16, F32, BF16, BF16, BF16, BF16]
    out_shape = tuple(jax.ShapeDtypeStruct((B, S, w), dt) for w, dt in zip(widths, dtypes))
    out_specs = tuple(pl.BlockSpec((None, tm, w), lambda b, i: (b, i, 0)) for w in widths)
    vec = pl.BlockSpec((None, 1, D), lambda b, i: (b, 0, 0))
    return pl.pallas_call(
        functools.partial(_inproj_kernel, segs=tuple(segs), q_scale=HEAD_DIM ** -0.5 * LOG2E),
        out_shape=out_shape,
        grid=(B, S // tm),
        in_specs=[pl.BlockSpec((None, tm, D), lambda b, i: (b, i, 0)), vec, vec,
                  pl.BlockSpec(w_packed.shape, lambda b, i: (0, 0))],
        out_specs=out_specs,
        compiler_params=_cparams(("parallel", "parallel")),
    )(x, sc1, sh1, w_packed)


def _foxcum_kernel(ff_ref, b_ref, o_ref, *, blk):
    S = ff_ref.shape[0]
    r = lax.broadcasted_iota(jnp.int32, (blk, blk), 0)
    c = lax.broadcasted_iota(jnp.int32, (blk, blk), 1)
    lower = (r >= c).astype(F32)
    carry = jnp.zeros((1, LANES), F32)
    for j in range(S // blk):
        z = ff_ref[j * blk:(j + 1) * blk, :] + b_ref[...]
        lf = jnp.minimum(z, 0.0) - jnp.log(1.0 + jnp.exp(-jnp.abs(z)))
        cum = jnp.dot(lower, lf, precision=HIGHEST, preferred_element_type=F32) + carry
        o_ref[j * blk:(j + 1) * blk, :] = cum * LOG2E
        carry = cum[blk - 1:blk, :]


def _foxcum(ffp, bias_p, blk=256):
    B, S, _ = ffp.shape
    return pl.pallas_call(
        functools.partial(_foxcum_kernel, blk=blk),
        out_shape=jax.ShapeDtypeStruct((B, S, LANES), F32),
        grid=(B,),
        in_specs=[pl.BlockSpec((None, S, LANES), lambda b: (b, 0, 0)),
                  pl.BlockSpec((1, LANES), lambda b: (0, 0))],
        out_specs=pl.BlockSpec((None, S, LANES), lambda b: (b, 0, 0)),
        compiler_params=_cparams(("parallel",)),
    )(ffp, bias_p)


NCUM = 3


def _fox_kernel(q_ref, k_ref, v_ref, c_ref, o_ref, ka_sc, kb_sc, va_sc, vb_sc, *, tq, tk):
    p = pl.program_id(1)
    qi = pl.program_id(2)
    S = k_ref.shape[0]

    @pl.when(qi == 0)
    def _():
        lane = lax.broadcasted_iota(jnp.int32, (S, LANES), 1)
        rr = lax.broadcasted_iota(jnp.int32, (LANES, LANES), 0)
        cc = lax.broadcasted_iota(jnp.int32, (LANES, LANES), 1)
        rest = c_ref[...]
        placed = jnp.zeros((S, LANES), F32)
        for i in range(NCUM):
            piece = rest.astype(BF16)
            rest = rest - piece.astype(F32)
            sel = ((rr == 2 * p) & (cc == HEAD_DIM + i)) | ((rr == 2 * p + 1) & (cc == i))
            placed = placed + jnp.dot(piece, jnp.where(sel, 1.0, 0.0).astype(BF16), preferred_element_type=F32)
        k2 = k_ref[...].astype(F32)
        ka_sc[...] = jnp.where(lane < HEAD_DIM, k2, -placed).astype(BF16)
        kb_sc[...] = jnp.where(lane >= HEAD_DIM, k2, -placed).astype(BF16)
        vt = v_ref[...].astype(F32).T
        row = lax.broadcasted_iota(jnp.int32, (LANES, S), 0)
        va_sc[...] = jnp.where(row < HEAD_DIM, vt, jnp.where(row == HEAD_DIM, 1.0, 0.0)).astype(BF16)
        vb_sc[...] = jnp.where(row >= HEAD_DIM, vt, jnp.where(row == 0, 1.0, 0.0)).astype(BF16)

    q2 = q_ref[...].astype(F32)
    qlane = lax.broadcasted_iota(jnp.int32, (tq, LANES), 1)
    qa = jnp.where(qlane < HEAD_DIM, q2, jnp.where(qlane < HEAD_DIM + NCUM, 1.0, 0.0)).astype(BF16)
    qb = jnp.where(qlane >= HEAD_DIM, q2, jnp.where(qlane < NCUM, 1.0, 0.0)).astype(BF16)
    krow = lax.broadcasted_iota(jnp.int32, (tk, tq), 0)
    qcol = lax.broadcasted_iota(jnp.int32, (tk, tq), 1)
    nsub = tq // tk

    def block(k0, carry, diag_off):
        out = []
        for ksc, vsc, qh, (m, acc) in ((ka_sc, va_sc, qa, carry[:2]), (kb_sc, vb_sc, qb, carry[2:])):
            st = lax.dot_general(ksc[pl.ds(k0, tk), :], qh, (((1,), (1,)), ((), ())),
                                 preferred_element_type=F32)
            if diag_off is not None:
                st = jnp.where(krow + diag_off <= qcol, st, NEG_BIG)
            m_new = jnp.maximum(m, jnp.max(st, axis=0, keepdims=True))
            pt = jnp.exp2(st - m_new).astype(BF16)
            acc = jnp.exp2(m - m_new) * acc + jnp.dot(vsc[:, pl.ds(k0, tk)], pt, preferred_element_type=F32)
            out += [m_new, acc]
        return tuple(out)

    def pair(j, carry):
        k0 = pl.multiple_of(j * (2 * tk), 2 * tk)
        return block(k0 + tk, block(k0, carry, None), None)

    m0 = jnp.full((1, tq), NEG_BIG, F32)
    a0 = jnp.zeros((LANES, tq), F32)
    carry = lax.fori_loop(0, qi * (nsub // 2), pair, (m0, a0, m0, a0))
    for d in range(nsub):
        carry = block(pl.multiple_of(qi * tq + d * tk, tk), carry, d * tk)
    _, aa, _, ab = carry
    row = lax.broadcasted_iota(jnp.int32, (LANES, tq), 0)
    ot = jnp.where(row < HEAD_DIM, aa * (1.0 / aa[HEAD_DIM:HEAD_DIM + 1, :]), ab * (1.0 / ab[0:1, :]))
    o_ref[...] = ot.T.astype(o_ref.dtype)


def _fox(fq, fk, fv, cum, tq=512, tk=256):
    B, S, W = fq.shape
    assert tq % (2 * tk) == 0 and S % tq == 0
    npairs = W // LANES
    return pl.pallas_call(
        functools.partial(_fox_kernel, tq=tq, tk=tk),
        out_shape=jax.ShapeDtypeStruct((B, S, W), BF16),
        grid=(B, npairs, S // tq),
        in_specs=[pl.BlockSpec((None, tq, LANES), lambda b, p, i: (b, i, p)),
                  pl.BlockSpec((None, S, LANES), lambda b, p, i: (b, 0, p)),
                  pl.BlockSpec((None, S, LANES), lambda b, p, i: (b, 0, p)),
                  pl.BlockSpec((None, S, LANES), lambda b, p, i: (b, 0, 0))],
        out_specs=pl.BlockSpec((None, tq, LANES), lambda b, p, i: (b, i, p)),
        scratch_shapes=[pltpu.VMEM((S, LANES), BF16), pltpu.VMEM((S, LANES), BF16),
                        pltpu.VMEM((LANES, S), BF16), pltpu.VMEM((LANES, S), BF16)],
        compiler_params=_cparams(("parallel", "parallel", "arbitrary")),
    )(fq, fk, fv, cum)


def _hgrn_kernel(hq_ref, hf_ref, hi_ref, hg_ref, lb_ref, nw_ref, o_ref,
                 a_sc, qt_sc, kt_sc, kk_sc, qq_sc, p_sc, s_sc, o_sc, stb_sc, dec_sc):
    S = hq_ref.shape[0]
    C = HCHUNK
    nchunks = S // C

    lg = lb_ref[...]
    e = jnp.exp(lg - jnp.max(lg, axis=0, keepdims=True))
    lb = e[0:1, :] / jnp.sum(e, axis=0, keepdims=True)

    f = lb + (1.0 - lb) * _sigmoid(hf_ref[...])
    lf = jnp.log(f)
    kk = 1.0 - f
    qq = _silu(hq_ref[...].astype(F32))

    rmod = lax.broadcasted_iota(jnp.int32, (S, LANES), 0) & (C - 1)
    a = lf
    d = 1
    while d < C:
        a = a + jnp.where(rmod >= d, pltpu.roll(a, d, axis=0), 0.0)
        d *= 2
    a3 = a.reshape(nchunks, C, LANES)
    alast = jnp.broadcast_to(a3[:, C - 1:C, :], (nchunks, C, LANES)).reshape(S, LANES)
    a_sc[...] = a * LOG2E
    kk_sc[...] = kk
    qq_sc[...] = qq
    qt_sc[...] = (qq * jnp.exp(a)).astype(BF16)
    kt_sc[...] = (kk * jnp.exp(alast - a)).astype(BF16)
    dec_sc[...] = jnp.exp(a3[:, C - 1, :])

    lane = lax.broadcasted_iota(jnp.int32, (C, LANES), 1)
    trow = lax.broadcasted_iota(jnp.int32, (C, LANES), 0)

    def gen(c, _):
        r0 = pl.multiple_of(c * C, C)
        ac = a_sc[pl.ds(r0, C), :]
        qc = qq_sc[pl.ds(r0, C), :]
        kc = kk_sc[pl.ds(r0, C), :]
        for s in range(C):
            dec = jnp.exp2(jnp.where(trow >= s, ac - ac[s:s + 1, :], NEG_BIG))
            p_sc[pl.ds(r0, C), s * LANES:(s + 1) * LANES] = (qc * (kc[s:s + 1, :] * dec)).astype(BF16)
        return 0

    lax.fori_loop(0, nchunks, gen, 0)

    er = lax.broadcasted_iota(jnp.int32, (C * LANES, LANES), 0)
    ec = lax.broadcasted_iota(jnp.int32, (C * LANES, LANES), 1)
    emat = (ec == ((er & (LANES - 1)) // HEAD_DIM) * C + er // LANES).astype(BF16)
    rb = 256

    def red(i, _):
        r0 = pl.multiple_of(i * rb, rb)
        s_sc[pl.ds(r0, rb), :] = jnp.dot(p_sc[pl.ds(r0, rb), :], emat,
                                         preferred_element_type=F32).astype(BF16)
        return 0

    lax.fori_loop(0, S // rb, red, 0)

    sr = lax.broadcasted_iota(jnp.int32, (LANES, LANES), 0)
    scn = lax.broadcasted_iota(jnp.int32, (LANES, LANES), 1)
    same_head = (sr // HEAD_DIM) == (scn // HEAD_DIM)
    unroll = 16

    def scan(g, st):
        for u in range(unroll):
            c = g * unroll + u
            r0 = pl.multiple_of(c * C, C)
            stb_sc[c] = st.astype(BF16)
            upd = lax.dot_general(hi_ref[pl.ds(r0, C), :], kt_sc[pl.ds(r0, C), :], (((0,), (0,)), ((), ())),
                                  preferred_element_type=F32)
            st = st * dec_sc[pl.ds(c, 1), :] + jnp.where(same_head, upd, 0.0)
        return st

    lax.fori_loop(0, nchunks // unroll, scan, jnp.zeros((LANES, LANES), F32))

    def readout(g, _):
        for u in range(unroll):
            c = g * unroll + u
            r0 = pl.multiple_of(c * C, C)
            vc = hi_ref[pl.ds(r0, C), :]
            o_inter = lax.dot_general(qt_sc[pl.ds(r0, C), :], stb_sc[c],
                                      (((1,), (1,)), ((), ())), preferred_element_type=F32)
            v2 = jnp.concatenate([jnp.where(lane < HEAD_DIM, vc, jnp.zeros_like(vc)),
                                  jnp.where(lane >= HEAD_DIM, vc, jnp.zeros_like(vc))], axis=0)
            o_intra = jnp.dot(s_sc[pl.ds(r0, C), :][:, :2 * C], v2, preferred_element_type=F32)
            o_sc[pl.ds(r0, C), :] = o_inter + o_intra
        return 0

    lax.fori_loop(0, nchunks // unroll, readout, 0)

    o = o_sc[...]
    ones_head = jnp.where(same_head, 1.0 / HEAD_DIM, 0.0).astype(F32)
    ms = jnp.dot(o * o, ones_head, precision=HIGHEST, preferred_element_type=F32)
    y = o * lax.rsqrt(ms + RMS_EPS) * nw_ref[...]
    o_ref[...] = (y * _silu(hg_ref[...].astype(F32))).astype(o_ref.dtype)


def _hgrn(hq, hf, hi, hg, lb_logits, norm_w):
    B, S, W = hq.shape
    npairs = W // LANES
    nrows = lb_logits.shape[0]
    seq = pl.BlockSpec((None, S, LANES), lambda b, p: (b, 0, p))
    return pl.pallas_call(
        _hgrn_kernel,
        out_shape=jax.ShapeDtypeStruct((B, S, W), BF16),
        grid=(B, npairs),
        in_specs=[seq, seq, seq, seq,
                  pl.BlockSpec((nrows, LANES), lambda b, p: (0, p)),
                  pl.BlockSpec((1, LANES), lambda b, p: (0, p))],
        out_specs=seq,
        scratch_shapes=[pltpu.VMEM((S, LANES), F32),
                        pltpu.VMEM((S, LANES), BF16),
                        pltpu.VMEM((S, LANES), BF16),
                        pltpu.VMEM((S, LANES), F32),
                        pltpu.VMEM((S, LANES), F32),
                        pltpu.VMEM((S, HCHUNK * LANES), BF16),
                        pltpu.VMEM((S, LANES), BF16),
                        pltpu.VMEM((S, LANES), F32),
                        pltpu.VMEM((S // HCHUNK, LANES, LANES), BF16),
                        pltpu.VMEM((S // HCHUNK, LANES), F32)],
        compiler_params=_cparams(("parallel", "parallel")),
    )(hq, hf, hi, hg, lb_logits, norm_w.reshape(1, W))


def _layer_norm(v, g, b):
    mu = jnp.mean(v, axis=-1, keepdims=True)
    d = v - mu
    var = jnp.mean(d * d, axis=-1, keepdims=True)
    return d * lax.rsqrt(var + LN_EPS) * g + b


def _store_row_tiles(ref, val):
    n, d = val.shape
    dt = d // LANES
    for j in range(dt):
        ref[pl.ds(j, n, stride=dt), :] = val[:, j * LANES:(j + 1) * LANES]


def _load_row_tiles(ref):
    dt = ROW_TILE
    n = ref.shape[0] // dt
    return jnp.concatenate([ref[pl.ds(j, n, stride=dt), :] for j in range(dt)], axis=1)


def _mix_kernel(yf_ref, oh_ref, gf_ref, gh_ref, x_ref, g1_ref, sc2_ref, sh2_ref,
                wuf_ref, wuh_ref, wo_ref, lg_ref, lbias_ref, wr_ref, br_ref,
                x1_ref, h2_ref, ri_ref, cnt_ref, carry_sc, *, alpha, ngroups, nper):
    first = (pl.program_id(0) == 0) & (pl.program_id(1) == 0)

    @pl.when(first)
    def _():
        carry_sc[...] = jnp.zeros_like(carry_sc)

    tm = x_ref.shape[0]
    yf = jnp.dot(yf_ref[...], wuf_ref[...], preferred_element_type=F32)
    yh = jnp.dot(oh_ref[...], wuh_ref[...], preferred_element_type=F32)
    merged = _sigmoid(gf_ref[...].astype(F32)) * yf + _sigmoid(gh_ref[...].astype(F32)) * yh
    y = jnp.dot(merged.astype(BF16), wo_ref[...], preferred_element_type=F32)
    x1 = _layer_norm(alpha * x_ref[...] + g1_ref[...] * y, lg_ref[...], lbias_ref[...])
    x1_ref[...] = x1
    h2 = x1 * (1.0 + sc2_ref[...]) + sh2_ref[...]
    _store_row_tiles(h2_ref, h2)

    logits = jnp.dot(h2, wr_ref[...], precision=HIGHEST, preferred_element_type=F32) + br_ref[...]
    lane = lax.broadcasted_iota(jnp.int32, (tm, LANES), 1)
    big = jnp.int32(1 << 20)

    def argmax_first(vals, mask):
        mx = jnp.max(jnp.where(mask, vals, -jnp.inf), axis=1, keepdims=True)
        idx = jnp.min(jnp.where(mask & (vals == mx), lane, big), axis=1, keepdims=True)
        return mx, idx

    gmask = lane < ngroups
    gmax = jnp.max(jnp.where(gmask, logits, -jnp.inf), axis=1, keepdims=True)
    gexp = jnp.where(gmask, jnp.exp(logits - gmax), 0.0)
    gprob = gexp / jnp.sum(gexp, axis=1, keepdims=True)
    g_w, g_idx = argmax_first(gprob, gmask)

    lo = ngroups + g_idx * nper
    emask = (lane >= lo) & (lane < lo + nper)
    emax = jnp.max(jnp.where(emask, logits, -jnp.inf), axis=1, keepdims=True)
    eexp = jnp.where(emask, jnp.exp(logits - emax), 0.0)
    eprob = eexp / jnp.sum(eexp, axis=1, keepdims=True)
    p0, i0 = argmax_first(eprob, emask)
    p1, i1 = argmax_first(eprob, emask & (lane != i0))
    den = p0 + p1
    w0 = p0 / den * g_w
    w1 = p1 / den * g_w
    e0 = i0 - ngroups
    e1 = i1 - ngroups

    oh = ((lane == e0) | (lane == e1)).astype(F32)
    r = lax.broadcasted_iota(jnp.int32, (tm, tm), 0)
    c = lax.broadcasted_iota(jnp.int32, (tm, tm), 1)
    strict_lower = (c < r).astype(BF16)
    before = jnp.dot(strict_lower, oh.astype(BF16), preferred_element_type=F32) + carry_sc[...]
    rank0 = jnp.sum(jnp.where(lane == e0, before, 0.0), axis=1, keepdims=True)
    rank1 = jnp.sum(jnp.where(lane == e1, before, 0.0), axis=1, keepdims=True)
    carry_sc[...] = carry_sc[...] + jnp.sum(oh, axis=0, keepdims=True)
    cnt_ref[...] = carry_sc[...]

    info = jnp.where(lane == 0, w0, 0.0)
    info = jnp.where(lane == 1, w1, info)
    info = jnp.where(lane == 2, e0.astype(F32), info)
    info = jnp.where(lane == 3, e1.astype(F32), info)
    info = jnp.where(lane == 4, rank0, info)
    info = jnp.where(lane == 5, rank1, info)
    ri_ref[...] = info


def _mix(yf, oh, gf, gh, x, g1, sc2, sh2, wuf, wuh, wo, ln_g, ln_b, wr, br, alpha, ngroups, nper, tm=256):
    B, S, D = x.shape
    W = yf.shape[2]
    tok = lambda w: pl.BlockSpec((None, tm, w), lambda b, i: (b, i, 0))
    vec = pl.BlockSpec((None, 1, D), lambda b, i: (b, 0, 0))
    full = lambda a: pl.BlockSpec(a.shape, lambda b, i: (0,) * a.ndim)
    return pl.pallas_call(
        functools.partial(_mix_kernel, alpha=alpha, ngroups=ngroups, nper=nper),
        out_shape=(jax.ShapeDtypeStruct((B, S, D), F32),
                   jax.ShapeDtypeStruct((B * S * (D // LANES), LANES), F32),
                   jax.ShapeDtypeStruct((B, S, LANES), F32),
                   jax.ShapeDtypeStruct((1, LANES), F32)),
        grid=(B, S // tm),
        in_specs=[tok(W), tok(W), tok(D), tok(D), tok(D), vec, vec, vec,
                  full(wuf), full(wuh), full(wo), full(ln_g), full(ln_b), full(wr), full(br)],
        out_specs=(tok(D),
                   pl.BlockSpec((tm * (D // LANES), LANES), lambda b, i: (b * (S // tm) + i, 0)),
                   tok(LANES), pl.BlockSpec((1, LANES), lambda b, i: (0, 0))),
        scratch_shapes=[pltpu.VMEM((1, LANES), F32)],
        compiler_params=_cparams(("arbitrary", "arbitrary")),
    )(yf, oh, gf, gh, x, g1, sc2, sh2, wuf, wuh, wo, ln_g, ln_b, wr, br)


def _sc_gather_rows(table, idx, window=LANES):
    n = idx.shape[0] * ROW_TILE
    rows = (idx[:, None] * ROW_TILE + jnp.arange(ROW_TILE, dtype=idx.dtype)[None, :]).reshape(1, n)
    mesh = plsc.VectorSubcoreMesh(core_axis_name="core", subcore_axis_name="subcore")

    @pl.kernel(out_type=jax.ShapeDtypeStruct((n, LANES), table.dtype), mesh=mesh, scratch_types=[])
    def gather(x_hbm, i_hbm, o_hbm):
        def body(i_vmem, o_vmem):
            pltpu.sync_copy(x_hbm.at[i_vmem.at[0]], o_vmem)

        pltpu.emit_pipeline(
            body,
            grid=(n // window,),
            in_specs=[pl.BlockSpec((1, window), lambda i: (0, i))],
            out_specs=[pl.BlockSpec((window, LANES), lambda i: (i, 0))],
            core_axis_name=("core", "subcore"),
            dimension_semantics=(pltpu.PARALLEL,),
        )(i_hbm, o_hbm)

    return gather(table, rows)


def _experts_kernel(te_ref, tv_ref, x_ref, wg_ref, wu_ref, wd_ref, o_ref):
    valid = tv_ref[pl.program_id(0)] != 0

    @pl.when(valid)
    def _():
        x = _load_row_tiles(x_ref).astype(BF16)
        g = jnp.dot(x, wg_ref[...].astype(BF16), preferred_element_type=F32)
        u = jnp.dot(x, wu_ref[...].astype(BF16), preferred_element_type=F32)
        hid = (_silu(g) * u).astype(BF16)
        _store_row_tiles(o_ref, jnp.dot(hid, wd_ref[...].astype(BF16), preferred_element_type=F32))

    @pl.when(jnp.logical_not(valid))
    def _():
        o_ref[...] = jnp.zeros_like(o_ref)


def _experts(tile_expert, tile_valid, xs, wg, wu, wd, tm):
    E, D, FF = wg.shape
    assert D == ROW_TILE * LANES
    ntiles = tile_expert.shape[0]
    rows = pl.BlockSpec((tm * ROW_TILE, LANES), lambda i, te, tv: (i, 0))
    grid_spec = pltpu.PrefetchScalarGridSpec(
        num_scalar_prefetch=2,
        grid=(ntiles,),
        in_specs=[rows,
                  pl.BlockSpec((None, D, FF), lambda i, te, tv: (te[i], 0, 0)),
                  pl.BlockSpec((None, D, FF), lambda i, te, tv: (te[i], 0, 0)),
                  pl.BlockSpec((None, FF, D), lambda i, te, tv: (te[i], 0, 0))],
        out_specs=rows,
    )
    return pl.pallas_call(
        _experts_kernel,
        out_shape=jax.ShapeDtypeStruct((ntiles * tm * ROW_TILE, LANES), F32),
        grid_spec=grid_spec,
        compiler_params=_cparams(("arbitrary",)),
    )(tile_expert, tile_valid, xs, wg, wu, wd)


def _combine_kernel(yg_ref, x1_ref, ri_ref, g2_ref, lg_ref, lb_ref, o_ref, *, alpha):
    tm = x1_ref.shape[0]
    halves = []
    for k in range(2):
        halves.append(jnp.concatenate(
            [yg_ref[pl.ds(k * ROW_TILE + j, tm, stride=2 * ROW_TILE), :] for j in range(ROW_TILE)], axis=1))
    ri = ri_ref[...]
    y = ri[:, 0:1] * halves[0] + ri[:, 1:2] * halves[1]
    o_ref[...] = _layer_norm(alpha * x1_ref[...] + g2_ref[...] * y, lg_ref[...], lb_ref[...])


def _combine(yg, x1, rinfo, g2, ln_g, ln_b, alpha, tm=256):
    B, S, D = x1.shape
    assert D == ROW_TILE * LANES
    nb = S // tm
    return pl.pallas_call(
        functools.partial(_combine_kernel, alpha=alpha),
        out_shape=jax.ShapeDtypeStruct((B, S, D), F32),
        grid=(B, nb),
        in_specs=[pl.BlockSpec((2 * tm * ROW_TILE, LANES), lambda b, i: (b * nb + i, 0)),
                  pl.BlockSpec((None, tm, D), lambda b, i: (b, i, 0)),
                  pl.BlockSpec((None, tm, LANES), lambda b, i: (b, i, 0)),
                  pl.BlockSpec((None, 1, D), lambda b, i: (b, 0, 0)),
                  pl.BlockSpec((1, D), lambda b, i: (0, 0)),
                  pl.BlockSpec((1, D), lambda b, i: (0, 0))],
        out_specs=pl.BlockSpec((None, tm, D), lambda b, i: (b, i, 0)),
        compiler_params=_cparams(("parallel", "parallel")),
    )(yg, x1, rinfo, g2, ln_g, ln_b)


def kernel(x, c, w_ada, b_ada, w_in, b_fox_forget, hgrn_lb_logits, hgrn_norm_w, w_up_fox, w_up_hgrn, w_out,
           ln1_g, ln1_b, w_router_group, b_router_group, w_router_expert, b_router_expert,
           w_expert_gate, w_expert_up, w_expert_down, ln2_g, ln2_b):
    B, S, D = x.shape
    depth = w_ada.shape[0]
    assert depth == 1, "single-layer block"
    fox_heads = b_fox_forget.shape[1]
    fox_w = fox_heads * HEAD_DIM
    hgrn_w = hgrn_norm_w.shape[1]
    ngroups = w_router_group.shape[2]
    nexp = w_router_expert.shape[2]
    nper = nexp // ngroups
    alpha = (2 * depth) ** 0.25
    T = B * S

    ada = _ada(c, w_ada[0], b_ada[0])
    sh1, sc1, g1, sh2, sc2, g2 = [a.reshape(B, 1, D) for a in jnp.split(ada, 6, axis=-1)]

    wi = w_in[0]
    o_ff = 3 * fox_w
    w_packed = jnp.concatenate(
        [wi[:, :o_ff + fox_heads], jnp.zeros((D, LANES - fox_heads), wi.dtype), wi[:, o_ff + fox_heads:]],
        axis=1).astype(BF16)
    widths = [fox_w, fox_w, fox_w, LANES, hgrn_w, hgrn_w, hgrn_w, hgrn_w, D, D]
    segs, off = [], 0
    for w in widths:
        segs.append((off, off + w))
        off += w
    fq, fk, fv, ffp, hq, hf, hi, hg, gf, gh = _inproj(x, sc1, sh1, w_packed, segs)

    bias_p = jnp.zeros((1, LANES), F32).at[0, :fox_heads].set(b_fox_forget[0])
    cum = _foxcum(ffp, bias_p)
    y_fox = _fox(fq, fk, fv, cum)

    o_h = _hgrn(hq, hf, hi, hg, hgrn_lb_logits, hgrn_norm_w[0])

    wr = jnp.zeros((D, LANES), F32).at[:, :ngroups].set(w_router_group[0]).at[:, ngroups:ngroups + nexp].set(
        w_router_expert[0])
    br = jnp.zeros((1, LANES), F32).at[0, :ngroups].set(b_router_group[0]).at[0, ngroups:ngroups + nexp].set(
        b_router_expert[0])
    x1, h2, rinfo, counts = _mix(
        y_fox, o_h, gf, gh, x, g1, sc2, sh2,
        w_up_fox[0].astype(BF16), w_up_hgrn[0].astype(BF16), w_out[0].astype(BF16),
        ln1_g[0].reshape(1, D), ln1_b[0].reshape(1, D), wr, br, alpha, ngroups, nper)

    tm_e = 256
    ntiles = (2 * T) // tm_e + nexp
    cnt = counts[0, :nexp].astype(jnp.int32)
    padded = ((cnt + tm_e - 1) // tm_e) * tm_e
    ends = jnp.cumsum(padded)
    starts = ends - padded
    ri = rinfo.reshape(T, LANES)
    eid = ri[:, 2:4].astype(jnp.int32)
    rank = ri[:, 4:6].astype(jnp.int32)
    pos = (starts[eid] + rank).reshape(-1)
    tile_start = jnp.arange(ntiles, dtype=jnp.int32) * tm_e
    tile_expert = jnp.minimum(jnp.sum((tile_start[:, None] >= ends[None, :]).astype(jnp.int32), axis=1), nexp - 1)
    tile_valid = (tile_start < ends[-1]).astype(jnp.int32)
    tok_ids = jnp.repeat(jnp.arange(T, dtype=jnp.int32), 2)
    nslots = ntiles * tm_e
    src_tok = (jnp.arange(nslots, dtype=jnp.int32) % T).at[pos].set(tok_ids)

    xs = _sc_gather_rows(h2, src_tok)
    ys = _experts(tile_expert, tile_valid, xs, w_expert_gate[0], w_expert_up[0], w_expert_down[0], tm_e)
    yg = _sc_gather_rows(ys, pos)
    return _combine(yg, x1, rinfo, g2, ln2_g[0].reshape(1, D), ln2_b[0].reshape(1, D), alpha)
```

```python
import functools

import jax
import jax.numpy as jnp
from jax import lax
from jax.experimental import pallas as pl
from jax.experimental.pallas import tpu as pltpu
from jax.experimental.pallas import tpu_sc as plsc

F32 = jnp.float32
BF16 = jnp.bfloat16
HIGHEST = lax.Precision.HIGHEST

LANES = 128
HEAD_DIM = 64
LN_EPS = 1e-5
RMS_EPS = 1e-6
LOG2E = 1.4426950408889634
NEG_BIG = -1e30
HCHUNK = 16
ROW_TILE = 8
VMEM_LIMIT = 56 * 1024 * 1024


def _cparams(sem, vmem=VMEM_LIMIT):
    return pltpu.CompilerParams(dimension_semantics=sem, vmem_limit_bytes=vmem)


def _sigmoid(x):
    return 1.0 / (1.0 + jnp.exp(-x))


def _silu(x):
    return x * _sigmoid(x)


def _ada_kernel(c_ref, w_ref, b_ref, o_ref):
    c = c_ref[...]
    o_ref[...] = jnp.dot(_silu(c), w_ref[...], precision=HIGHEST,
                         preferred_element_type=F32) + b_ref[...]


def _ada(c, w_ada, b_ada):
    B, D = c.shape
    N = w_ada.shape[1]
    tn = 1024
    return pl.pallas_call(
        _ada_kernel,
        out_shape=jax.ShapeDtypeStruct((B, N), F32),
        grid=(N // tn,),
        in_specs=[pl.BlockSpec((B, D), lambda j: (0, 0)),
                  pl.BlockSpec((D, tn), lambda j: (0, j)),
                  pl.BlockSpec((1, tn), lambda j: (0, j))],
        out_specs=pl.BlockSpec((B, tn), lambda j: (0, j)),
        compiler_params=_cparams(("arbitrary",)),
    )(c, w_ada, b_ada.reshape(1, N))


def _inproj_kernel(x_ref, sc_ref, sh_ref, w_ref,
                   fq_ref, fk_ref, fv_ref, ff_ref, hq_ref, hf_ref, hi_ref, hg_ref, gf_ref, gh_ref,
                   *, segs, q_scale):
    h = (x_ref[...] * (1.0 + sc_ref[...]) + sh_ref[...]).astype(BF16)
    outs = (fq_ref, fk_ref, fv_ref, ff_ref, hq_ref, hf_ref, hi_ref, hg_ref, gf_ref, gh_ref)
    for idx, (o_ref, (a, b)) in enumerate(zip(outs, segs)):
        r = jnp.dot(h, w_ref[:, a:b], preferred_element_type=F32)
        if idx == 0:
            r = r * q_scale
        o_ref[...] = r.astype(o_ref.dtype)


def _inproj(x, sc1, sh1, w_packed, segs, tm=256):
    B, S, D = x.shape
    widths = [b - a for a, b in segs]
    dtypes = [BF16, BF16, BF16, F32, BF16, F32, BF16, BF16, BF16, BF16]
    out_shape = tuple(jax.ShapeDtypeStruct((B, S, w), dt) for w, dt in zip(widths, dtypes))
    out_specs = tuple(pl.BlockSpec((None, tm, w), lambda b, i: (b, i, 0)) for w in widths)
    vec = pl.BlockSpec((None, 1, D), lambda b, i: (b, 0, 0))
    return pl.pallas_call(
        functools.partial(_inproj_kernel, segs=tuple(segs), q_scale=HEAD_DIM ** -0.5 * LOG2E),
        out_shape=out_shape,
        grid=(B, S // tm),
        in_specs=[pl.BlockSpec((None, tm, D), lambda b, i: (b, i, 0)), vec, vec,
                  pl.BlockSpec(w_packed.shape, lambda b, i: (0, 0))],
        out_specs=out_specs,
        compiler_params=_cparams(("parallel", "parallel")),
    )(x, sc1, sh1, w_packed)


def _foxcum_kernel(ff_ref, b_ref, o_ref, *, blk):
    S = ff_ref.shape[0]
    r = lax.broadcasted_iota(jnp.int32, (blk, blk), 0)
    c = lax.broadcasted_iota(jnp.int32, (blk, blk), 1)
    lower = (r >= c).astype(F32)
    carry = jnp.zeros((1, LANES), F32)
    for j in range(S // blk):
        z = ff_ref[j * blk:(j + 1) * blk, :] + b_ref[...]
        lf = jnp.minimum(z, 0.0) - jnp.log(1.0 + jnp.exp(-jnp.abs(z)))
        cum = jnp.dot(lower, lf, precision=HIGHEST, preferred_element_type=F32) + carry
        o_ref[j * blk:(j + 1) * blk, :] = cum * LOG2E
        carry = cum[blk - 1:blk, :]


def _foxcum(ffp, bias_p, blk=256):
    B, S, _ = ffp.shape
    return pl.pallas_call(
        functools.partial(_foxcum_kernel, blk=blk),
        out_shape=jax.ShapeDtypeStruct((B, S, LANES), F32),
        grid=(B,),
        in_specs=[pl.BlockSpec((None, S, LANES), lambda b: (b, 0, 0)),
                  pl.BlockSpec((1, LANES), lambda b: (0, 0))],
        out_specs=pl.BlockSpec((None, S, LANES), lambda b: (b, 0, 0)),
        compiler_params=_cparams(("parallel",)),
    )(ffp, bias_p)


NCUM = 3


def _fox_kernel(q_ref, k_ref, v_ref, c_ref, o_ref, ka_sc, kb_sc, va_sc, vb_sc, *, tq, tk):
    p = pl.program_id(1)
    qi = pl.program_id(2)
    S = k_ref.shape[0]

    @pl.when(qi == 0)
    def _():
        lane = lax.broadcasted_iota(jnp.int32, (S, LANES), 1)
        rr = lax.broadcasted_iota(jnp.int32, (LANES, LANES), 0)
        cc = lax.broadcasted_iota(jnp.int32, (LANES, LANES), 1)
        rest = c_ref[...]
        placed = jnp.zeros((S, LANES), F32)
        for i in range(NCUM):
            piece = rest.astype(BF16)
            rest = rest - piece.astype(F32)
            sel = ((rr == 2 * p) & (cc == HEAD_DIM + i)) | ((rr == 2 * p + 1) & (cc == i))
            placed = placed + jnp.dot(piece, jnp.where(sel, 1.0, 0.0).astype(BF16), preferred_element_type=F32)
        k2 = k_ref[...].astype(F32)
        ka_sc[...] = jnp.where(lane < HEAD_DIM, k2, -placed).astype(BF16)
        kb_sc[...] = jnp.where(lane >= HEAD_DIM, k2, -placed).astype(BF16)
        vt = v_ref[...].astype(F32).T
        row = lax.broadcasted_iota(jnp.int32, (LANES, S), 0)
        va_sc[...] = jnp.where(row < HEAD_DIM, vt, jnp.where(row == HEAD_DIM, 1.0, 0.0)).astype(BF16)
        vb_sc[...] = jnp.where(row >= HEAD_DIM, vt, jnp.where(row == 0, 1.0, 0.0)).astype(BF16)

    q2 = q_ref[...].astype(F32)
    qlane = lax.broadcasted_iota(jnp.int32, (tq, LANES), 1)
    qa = jnp.where(qlane < HEAD_DIM, q2, jnp.where(qlane < HEAD_DIM + NCUM, 1.0, 0.0)).astype(BF16)
    qb = jnp.where(qlane >= HEAD_DIM, q2, jnp.where(qlane < NCUM, 1.0, 0.0)).astype(BF16)
    krow = lax.broadcasted_iota(jnp.int32, (tk, tq), 0)
    qcol = lax.broadcasted_iota(jnp.int32, (tk, tq), 1)
    nsub = tq // tk

    def block(k0, carry, diag_off):
        out = []
        for ksc, vsc, qh, (m, acc) in ((ka_sc, va_sc, qa, carry[:2]), (kb_sc, vb_sc, qb, carry[2:])):
            st = lax.dot_general(ksc[pl.ds(k0, tk), :], qh, (((1,), (1,)), ((), ())),
                                 preferred_element_type=F32)
            if diag_off is not None:
                st = jnp.where(krow + diag_off <= qcol, st, NEG_BIG)
            m_new = jnp.maximum(m, jnp.max(st, axis=0, keepdims=True))
            pt = jnp.exp2(st - m_new).astype(BF16)
            acc = jnp.exp2(m - m_new) * acc + jnp.dot(vsc[:, pl.ds(k0, tk)], pt, preferred_element_type=F32)
            out += [m_new, acc]
        return tuple(out)

    def pair(j, carry):
        k0 = pl.multiple_of(j * (2 * tk), 2 * tk)
        return block(k0 + tk, block(k0, carry, None), None)

    m0 = jnp.full((1, tq), NEG_BIG, F32)
    a0 = jnp.zeros((LANES, tq), F32)
    carry = lax.fori_loop(0, qi * (nsub // 2), pair, (m0, a0, m0, a0))
    for d in range(nsub):
        carry = block(pl.multiple_of(qi * tq + d * tk, tk), carry, d * tk)
    _, aa, _, ab = carry
    row = lax.broadcasted_iota(jnp.int32, (LANES, tq), 0)
    ot = jnp.where(row < HEAD_DIM, aa * (1.0 / aa[HEAD_DIM:HEAD_DIM + 1, :]), ab * (1.0 / ab[0:1, :]))
    o_ref[...] = ot.T.astype(o_ref.dtype)


def _fox(fq, fk, fv, cum, tq=512, tk=256):
    B, S, W = fq.shape
    assert tq % (2 * tk) == 0 and S % tq == 0
    npairs = W // LANES
    return pl.pallas_call(
        functools.partial(_fox_kernel, tq=tq, tk=tk),
        out_shape=jax.ShapeDtypeStruct((B, S, W), BF16),
        grid=(B, npairs, S // tq),
        in_specs=[pl.BlockSpec((None, tq, LANES), lambda b, p, i: (b, i, p)),
                  pl.BlockSpec((None, S, LANES), lambda b, p, i: (b, 0, p)),
                  pl.BlockSpec((None, S, LANES), lambda b, p, i: (b, 0, p)),
                  pl.BlockSpec((None, S, LANES), lambda b, p, i: (b, 0, 0))],
        out_specs=pl.BlockSpec((None, tq, LANES), lambda b, p, i: (b, i, p)),
        scratch_shapes=[pltpu.VMEM((S, LANES), BF16), pltpu.VMEM((S, LANES), BF16),
                        pltpu.VMEM((LANES, S), BF16), pltpu.VMEM((LANES, S), BF16)],
        compiler_params=_cparams(("parallel", "parallel", "arbitrary")),
    )(fq, fk, fv, cum)


def _hgrn_kernel(hq_ref, hf_ref, hi_ref, hg_ref, lb_ref, nw_ref, o_ref,
                 a_sc, qt_sc, kt_sc, kk_sc, qq_sc, p_sc, s_sc, o_sc, stb_sc, dec_sc):
    S = hq_ref.shape[0]
    C = HCHUNK
    nchunks = S // C

    lg = lb_ref[...]
    e = jnp.exp(lg - jnp.max(lg, axis=0, keepdims=True))
    lb = e[0:1, :] / jnp.sum(e, axis=0, keepdims=True)

    f = lb + (1.0 - lb) * _sigmoid(hf_ref[...])
    lf = jnp.log(f)
    kk = 1.0 - f
    qq = _silu(hq_ref[...].astype(F32))

    rmod = lax.broadcasted_iota(jnp.int32, (S, LANES), 0) & (C - 1)
    a = lf
    d = 1
    while d < C:
        a = a + jnp.where(rmod >= d, pltpu.roll(a, d, axis=0), 0.0)
        d *= 2
    a3 = a.reshape(nchunks, C, LANES)
    alast = jnp.broadcast_to(a3[:, C - 1:C, :], (nchunks, C, LANES)).reshape(S, LANES)
    a_sc[...] = a * LOG2E
    kk_sc[...] = kk
    qq_sc[...] = qq
    qt_sc[...] = (qq * jnp.exp(a)).astype(BF16)
    kt_sc[...] = (kk * jnp.exp(alast - a)).astype(BF16)
    dec_sc[...] = jnp.exp(a3[:, C - 1, :])

    lane = lax.broadcasted_iota(jnp.int32, (C, LANES), 1)
    trow = lax.broadcasted_iota(jnp.int32, (C, LANES), 0)

    def gen(c, _):
        r0 = pl.multiple_of(c * C, C)
        ac = a_sc[pl.ds(r0, C), :]
        qc = qq_sc[pl.ds(r0, C), :]
        kc = kk_sc[pl.ds(r0, C), :]
        for s in range(C):
            dec = jnp.exp2(jnp.where(trow >= s, ac - ac[s:s + 1, :], NEG_BIG))
            p_sc[pl.ds(r0, C), s * LANES:(s + 1) * LANES] = (qc * (kc[s:s + 1, :] * dec)).astype(BF16)
        return 0

    lax.fori_loop(0, nchunks, gen, 0)

    er = lax.broadcasted_iota(jnp.int32, (C * LANES, LANES), 0)
    ec = lax.broadcasted_iota(jnp.int32, (C * LANES, LANES), 1)
    emat = (ec == ((er & (LANES - 1)) // HEAD_DIM) * C + er // LANES).astype(BF16)
    rb = 256

    def red(i, _):
        r0 = pl.multiple_of(i * rb, rb)
        s_sc[pl.ds(r0, rb), :] = jnp.dot(p_sc[pl.ds(r0, rb), :], emat,
                                         preferred_element_type=F32).astype(BF16)
        return 0

    lax.fori_loop(0, S // rb, red, 0)

    sr = lax.broadcasted_iota(jnp.int32, (LANES, LANES), 0)
    scn = lax.broadcasted_iota(jnp.int32, (LANES, LANES), 1)
    same_head = (sr // HEAD_DIM) == (scn // HEAD_DIM)
    unroll = 16

    def scan(g, st):
        for u in range(unroll):
            c = g * unroll + u
            r0 = pl.multiple_of(c * C, C)
            stb_sc[c] = st.astype(BF16)
            upd = lax.dot_general(hi_ref[pl.ds(r0, C), :], kt_sc[pl.ds(r0, C), :], (((0,), (0,)), ((), ())),
                                  preferred_element_type=F32)
            st = st * dec_sc[pl.ds(c, 1), :] + jnp.where(same_head, upd, 0.0)
        return st

    lax.fori_loop(0, nchunks // unroll, scan, jnp.zeros((LANES, LANES), F32))

    def readout(g, _):
        for u in range(unroll):
            c = g * unroll + u
            r0 = pl.multiple_of(c * C, C)
            vc = hi_ref[pl.ds(r0, C), :]
            o_inter = lax.dot_general(qt_sc[pl.ds(r0, C), :], stb_sc[c],
                                      (((1,), (1,)), ((), ())), preferred_element_type=F32)
            v2 = jnp.concatenate([jnp.where(lane < HEAD_DIM, vc, jnp.zeros_like(vc)),
                                  jnp.where(lane >= HEAD_DIM, vc, jnp.zeros_like(vc))], axis=0)
            o_intra = jnp.dot(s_sc[pl.ds(r0, C), :][:, :2 * C], v2, preferred_element_type=F32)
            o_sc[pl.ds(r0, C), :] = o_inter + o_intra
        return 0

    lax.fori_loop(0, nchunks // unroll, readout, 0)

    o = o_sc[...]
    ones_head = jnp.where(same_head, 1.0 / HEAD_DIM, 0.0).astype(F32)
    ms = jnp.dot(o * o, ones_head, precision=HIGHEST, preferred_element_type=F32)
    y = o * lax.rsqrt(ms + RMS_EPS) * nw_ref[...]
    o_ref[...] = (y * _silu(hg_ref[...].astype(F32))).astype(o_ref.dtype)


def _hgrn(hq, hf, hi, hg, lb_logits, norm_w):
    B, S, W = hq.shape
    npairs = W // LANES
    nrows = lb_logits.shape[0]
    seq = pl.BlockSpec((None, S, LANES), lambda b, p: (b, 0, p))
    return pl.pallas_call(
        _hgrn_kernel,
        out_shape=jax.ShapeDtypeStruct((B, S, W), BF16),
        grid=(B, npairs),
        in_specs=[seq, seq, seq, seq,
                  pl.BlockSpec((nrows, LANES), lambda b, p: (0, p)),
                  pl.BlockSpec((1, LANES), lambda b, p: (0, p))],
        out_specs=seq,
        scratch_shapes=[pltpu.VMEM((S, LANES), F32),
                        pltpu.VMEM((S, LANES), BF16),
                        pltpu.VMEM((S, LANES), BF16),
                        pltpu.VMEM((S, LANES), F32),
                        pltpu.VMEM((S, LANES), F32),
                        pltpu.VMEM((S, HCHUNK * LANES), BF16),
                        pltpu.VMEM((S, LANES), BF16),
                        pltpu.VMEM((S, LANES), F32),
                        pltpu.VMEM((S // HCHUNK, LANES, LANES), BF16),
                        pltpu.VMEM((S // HCHUNK, LANES), F32)],
        compiler_params=_cparams(("parallel", "parallel")),
    )(hq, hf, hi, hg, lb_logits, norm_w.reshape(1, W))


def _layer_norm(v, g, b):
    mu = jnp.mean(v, axis=-1, keepdims=True)
    d = v - mu
    var = jnp.mean(d * d, axis=-1, keepdims=True)
    return d * lax.rsqrt(var + LN_EPS) * g + b


def _store_chunks(ref, val):
    for j in range(ref.shape[0]):
        ref[j] = val[:, j * LANES:(j + 1) * LANES]


def _load_chunks(ref):
    return jnp.concatenate([ref[j] for j in range(ref.shape[0])], axis=1)


def _mix_kernel(yf_ref, oh_ref, gf_ref, gh_ref, x_ref, g1_ref, sc2_ref, sh2_ref,
                wuf_ref, wuh_ref, wo_ref, lg_ref, lbias_ref, wr_ref, br_ref,
                x1_ref, h2_ref, ri_ref, rt_ref, cnt_ref, carry_sc, *, alpha, ngroups, nper):
    first = (pl.program_id(0) == 0) & (pl.program_id(1) == 0)

    @pl.when(first)
    def _():
        carry_sc[...] = jnp.zeros_like(carry_sc)

    tm = x_ref.shape[0]
    yf = jnp.dot(yf_ref[...], wuf_ref[...], preferred_element_type=F32)
    yh = jnp.dot(oh_ref[...], wuh_ref[...], preferred_element_type=F32)
    merged = _sigmoid(gf_ref[...].astype(F32)) * yf + _sigmoid(gh_ref[...].astype(F32)) * yh
    y = jnp.dot(merged.astype(BF16), wo_ref[...], preferred_element_type=F32)
    x1 = _layer_norm(alpha * x_ref[...] + g1_ref[...] * y, lg_ref[...], lbias_ref[...])
    x1_ref[...] = x1
    h2 = x1 * (1.0 + sc2_ref[...]) + sh2_ref[...]
    _store_chunks(h2_ref, h2)

    logits = jnp.dot(h2, wr_ref[...], precision=HIGHEST, preferred_element_type=F32) + br_ref[...]
    lane = lax.broadcasted_iota(jnp.int32, (tm, LANES), 1)
    big = jnp.int32(1 << 20)

    def argmax_first(vals, mask):
        mx = jnp.max(jnp.where(mask, vals, -jnp.inf), axis=1, keepdims=True)
        idx = jnp.min(jnp.where(mask & (vals == mx), lane, big), axis=1, keepdims=True)
        return mx, idx

    gmask = lane < ngroups
    gmax = jnp.max(jnp.where(gmask, logits, -jnp.inf), axis=1, keepdims=True)
    gexp = jnp.where(gmask, jnp.exp(logits - gmax), 0.0)
    gprob = gexp / jnp.sum(gexp, axis=1, keepdims=True)
    g_w, g_idx = argmax_first(gprob, gmask)

    lo = ngroups + g_idx * nper
    emask = (lane >= lo) & (lane < lo + nper)
    emax = jnp.max(jnp.where(emask, logits, -jnp.inf), axis=1, keepdims=True)
    eexp = jnp.where(emask, jnp.exp(logits - emax), 0.0)
    eprob = eexp / jnp.sum(eexp, axis=1, keepdims=True)
    p0, i0 = argmax_first(eprob, emask)
    p1, i1 = argmax_first(eprob, emask & (lane != i0))
    den = p0 + p1
    w0 = p0 / den * g_w
    w1 = p1 / den * g_w
    e0 = i0 - ngroups
    e1 = i1 - ngroups

    oh = ((lane == e0) | (lane == e1)).astype(F32)
    r = lax.broadcasted_iota(jnp.int32, (tm, tm), 0)
    c = lax.broadcasted_iota(jnp.int32, (tm, tm), 1)
    strict_lower = (c < r).astype(BF16)
    before = jnp.dot(strict_lower, oh.astype(BF16), preferred_element_type=F32) + carry_sc[...]
    rank0 = jnp.sum(jnp.where(lane == e0, before, 0.0), axis=1, keepdims=True)
    rank1 = jnp.sum(jnp.where(lane == e1, before, 0.0), axis=1, keepdims=True)
    carry_sc[...] = carry_sc[...] + jnp.sum(oh, axis=0, keepdims=True)
    cnt_ref[...] = carry_sc[...]

    info = jnp.where(lane == 0, w0, 0.0)
    info = jnp.where(lane == 1, w1, info)
    info = jnp.where(lane == 2, e0.astype(F32), info)
    info = jnp.where(lane == 3, e1.astype(F32), info)
    info = jnp.where(lane == 4, rank0, info)
    info = jnp.where(lane == 5, rank1, info)
    ri_ref[...] = info
    rt_ref[...] = info.T[:ROW_TILE, :]


def _mix(yf, oh, gf, gh, x, g1, sc2, sh2, wuf, wuh, wo, ln_g, ln_b, wr, br, alpha, ngroups, nper, tm=256):
    B, S, D = x.shape
    W = yf.shape[2]
    tok = lambda w: pl.BlockSpec((None, tm, w), lambda b, i: (b, i, 0))
    vec = pl.BlockSpec((None, 1, D), lambda b, i: (b, 0, 0))
    full = lambda a: pl.BlockSpec(a.shape, lambda b, i: (0,) * a.ndim)
    return pl.pallas_call(
        functools.partial(_mix_kernel, alpha=alpha, ngroups=ngroups, nper=nper),
        out_shape=(jax.ShapeDtypeStruct((B, S, D), F32),
                   jax.ShapeDtypeStruct((D // LANES, B * S, LANES), F32),
                   jax.ShapeDtypeStruct((B, S, LANES), F32),
                   jax.ShapeDtypeStruct((ROW_TILE, B * S), F32),
                   jax.ShapeDtypeStruct((1, LANES), F32)),
        grid=(B, S // tm),
        in_specs=[tok(W), tok(W), tok(D), tok(D), tok(D), vec, vec, vec,
                  full(wuf), full(wuh), full(wo), full(ln_g), full(ln_b), full(wr), full(br)],
        out_specs=(tok(D),
                   pl.BlockSpec((D // LANES, tm, LANES), lambda b, i: (0, b * (S // tm) + i, 0)),
                   tok(LANES),
                   pl.BlockSpec((ROW_TILE, tm), lambda b, i: (0, b * (S // tm) + i)),
                   pl.BlockSpec((1, LANES), lambda b, i: (0, 0))),
        scratch_shapes=[pltpu.VMEM((1, LANES), F32)],
        compiler_params=_cparams(("arbitrary", "arbitrary")),
    )(yf, oh, gf, gh, x, g1, sc2, sh2, wuf, wuh, wo, ln_g, ln_b, wr, br)


def _sc_mesh():
    return plsc.VectorSubcoreMesh(core_axis_name="core", subcore_axis_name="subcore")


def _sc_pipeline(body, grid, in_specs, out_specs):
    return pltpu.emit_pipeline(body, grid=grid, in_specs=in_specs, out_specs=out_specs,
                               core_axis_name=("core", "subcore"),
                               dimension_semantics=(pltpu.PARALLEL,) * len(grid))


def _sc_scatter_rows(src, rows_a, rows_b, n_out):
    nj, t = rows_a.shape
    nc = t // LANES

    @pl.kernel(out_type=jax.ShapeDtypeStruct((n_out, LANES), src.dtype), mesh=_sc_mesh(), scratch_types=[])
    def scatter(x_hbm, a_hbm, b_hbm, o_hbm):
        def body(x_vmem, a_vmem, b_vmem):
            pltpu.sync_copy(x_vmem, o_hbm.at[a_vmem.at[0]])
            pltpu.sync_copy(x_vmem, o_hbm.at[b_vmem.at[0]])

        idx = pl.BlockSpec((1, LANES), lambda j, c: (j, c))
        _sc_pipeline(body, (nj, nc), [pl.BlockSpec((LANES, LANES), lambda j, c: (j * nc + c, 0)), idx, idx],
                     [])(x_hbm, a_hbm, b_hbm)

    return scatter(src, rows_a, rows_b)


def _sc_gather_rows(table, rows):
    nr, t = rows.shape
    nc = t // LANES

    @pl.kernel(out_type=jax.ShapeDtypeStruct((nr * t, LANES), table.dtype), mesh=_sc_mesh(), scratch_types=[])
    def gather(x_hbm, i_hbm, o_hbm):
        def body(i_vmem, o_vmem):
            pltpu.sync_copy(x_hbm.at[i_vmem.at[0]], o_vmem)

        _sc_pipeline(body, (nr, nc), [pl.BlockSpec((1, LANES), lambda r, c: (r, c))],
                     [pl.BlockSpec((LANES, LANES), lambda r, c: (r * nc + c, 0))])(i_hbm, o_hbm)

    return gather(table, rows)


def _experts_kernel(te_ref, tn_ref, x_ref, wg_ref, wu_ref, wd_ref, o_ref):
    nrows = tn_ref[pl.program_id(0)]

    @pl.when(nrows > 0)
    def _():
        x = _load_chunks(x_ref)
        x = jnp.where(lax.broadcasted_iota(jnp.int32, x.shape, 0) < nrows, x, 0.0).astype(BF16)
        g = jnp.dot(x, wg_ref[...].astype(BF16), preferred_element_type=F32)
        u = jnp.dot(x, wu_ref[...].astype(BF16), preferred_element_type=F32)
        hid = (_silu(g) * u).astype(BF16)
        _store_chunks(o_ref, jnp.dot(hid, wd_ref[...].astype(BF16), preferred_element_type=F32))

    @pl.when(nrows == 0)
    def _():
        o_ref[...] = jnp.zeros_like(o_ref)


def _experts(tile_expert, tile_rows, xs, wg, wu, wd, tm):
    E, D, FF = wg.shape
    dt = D // LANES
    ntiles = tile_expert.shape[0]
    rows = pl.BlockSpec((dt, tm, LANES), lambda i, te, tn: (0, i, 0))
    grid_spec = pltpu.PrefetchScalarGridSpec(
        num_scalar_prefetch=2,
        grid=(ntiles,),
        in_specs=[rows,
                  pl.BlockSpec((None, D, FF), lambda i, te, tn: (te[i], 0, 0)),
                  pl.BlockSpec((None, D, FF), lambda i, te, tn: (te[i], 0, 0)),
                  pl.BlockSpec((None, FF, D), lambda i, te, tn: (te[i], 0, 0))],
        out_specs=rows,
    )
    return pl.pallas_call(
        _experts_kernel,
        out_shape=jax.ShapeDtypeStruct((dt, ntiles * tm, LANES), F32),
        grid_spec=grid_spec,
        compiler_params=_cparams(("arbitrary",)),
    )(tile_expert, tile_rows, xs, wg, wu, wd)


def _combine_kernel(yg_ref, x1_ref, ri_ref, g2_ref, lg_ref, lb_ref, o_ref, *, alpha):
    ri = ri_ref[...]
    y = ri[:, 0:1] * _load_chunks(yg_ref.at[0]) + ri[:, 1:2] * _load_chunks(yg_ref.at[1])
    o_ref[...] = _layer_norm(alpha * x1_ref[...] + g2_ref[...] * y, lg_ref[...], lb_ref[...])


def _combine(yg, x1, rinfo, g2, ln_g, ln_b, alpha, tm=256):
    B, S, D = x1.shape
    nb = S // tm
    return pl.pallas_call(
        functools.partial(_combine_kernel, alpha=alpha),
        out_shape=jax.ShapeDtypeStruct((B, S, D), F32),
        grid=(B, nb),
        in_specs=[pl.BlockSpec((2, D // LANES, tm, LANES), lambda b, i: (0, 0, b * nb + i, 0)),
                  pl.BlockSpec((None, tm, D), lambda b, i: (b, i, 0)),
                  pl.BlockSpec((None, tm, LANES), lambda b, i: (b, i, 0)),
                  pl.BlockSpec((None, 1, D), lambda b, i: (b, 0, 0)),
                  pl.BlockSpec((1, D), lambda b, i: (0, 0)),
                  pl.BlockSpec((1, D), lambda b, i: (0, 0))],
        out_specs=pl.BlockSpec((None, tm, D), lambda b, i: (b, i, 0)),
        compiler_params=_cparams(("parallel", "parallel")),
    )(yg, x1, rinfo, g2, ln_g, ln_b)


def kernel(x, c, w_ada, b_ada, w_in, b_fox_forget, hgrn_lb_logits, hgrn_norm_w, w_up_fox, w_up_hgrn, w_out,
           ln1_g, ln1_b, w_router_group, b_router_group, w_router_expert, b_router_expert,
           w_expert_gate, w_expert_up, w_expert_down, ln2_g, ln2_b):
    B, S, D = x.shape
    depth = w_ada.shape[0]
    assert depth == 1, "single-layer block"
    fox_heads = b_fox_forget.shape[1]
    fox_w = fox_heads * HEAD_DIM
    hgrn_w = hgrn_norm_w.shape[1]
    ngroups = w_router_group.shape[2]
    nexp = w_router_expert.shape[2]
    nper = nexp // ngroups
    alpha = (2 * depth) ** 0.25
    T = B * S

    ada = _ada(c, w_ada[0], b_ada[0])
    sh1, sc1, g1, sh2, sc2, g2 = [a.reshape(B, 1, D) for a in jnp.split(ada, 6, axis=-1)]

    wi = w_in[0]
    o_ff = 3 * fox_w
    w_packed = jnp.concatenate(
        [wi[:, :o_ff + fox_heads], jnp.zeros((D, LANES - fox_heads), wi.dtype), wi[:, o_ff + fox_heads:]],
        axis=1).astype(BF16)
    widths = [fox_w, fox_w, fox_w, LANES, hgrn_w, hgrn_w, hgrn_w, hgrn_w, D, D]
    segs, off = [], 0
    for w in widths:
        segs.append((off, off + w))
        off += w
    fq, fk, fv, ffp, hq, hf, hi, hg, gf, gh = _inproj(x, sc1, sh1, w_packed, segs)

    bias_p = jnp.zeros((1, LANES), F32).at[0, :fox_heads].set(b_fox_forget[0])
    cum = _foxcum(ffp, bias_p)
    y_fox = _fox(fq, fk, fv, cum)

    o_h = _hgrn(hq, hf, hi, hg, hgrn_lb_logits, hgrn_norm_w[0])

    wr = jnp.zeros((D, LANES), F32).at[:, :ngroups].set(w_router_group[0]).at[:, ngroups:ngroups + nexp].set(
        w_router_expert[0])
    br = jnp.zeros((1, LANES), F32).at[0, :ngroups].set(b_router_group[0]).at[0, ngroups:ngroups + nexp].set(
        b_router_expert[0])
    x1, h2, rinfo, fields, counts = _mix(
        y_fox, o_h, gf, gh, x, g1, sc2, sh2,
        w_up_fox[0].astype(BF16), w_up_hgrn[0].astype(BF16), w_out[0].astype(BF16),
        ln1_g[0].reshape(1, D), ln1_b[0].reshape(1, D), wr, br, alpha, ngroups, nper)

    tm_e = 256
    dt = D // LANES
    ntiles = (2 * T) // tm_e + nexp
    nslots = ntiles * tm_e
    cnt = counts[0, :nexp].astype(jnp.int32)
    padded = ((cnt + tm_e - 1) // tm_e) * tm_e
    ends = jnp.cumsum(padded)
    starts = ends - padded
    eid = fields[2:4].astype(jnp.int32)
    rank = fields[4:6].astype(jnp.int32)
    first = jnp.sum(jnp.where(eid[None] == jnp.arange(nexp, dtype=jnp.int32)[:, None, None],
                              starts[:, None, None], 0), axis=0)
    pos = first + rank
    tile_start = jnp.arange(ntiles, dtype=jnp.int32) * tm_e
    tile_expert = jnp.minimum(jnp.sum((tile_start[:, None] >= ends[None, :]).astype(jnp.int32), axis=1), nexp - 1)
    tile_rows = jnp.clip(starts[tile_expert] + cnt[tile_expert] - tile_start, 0, tm_e)
    rows = pos[:, None, :] + (jnp.arange(dt, dtype=jnp.int32) * nslots)[None, :, None]

    xs = _sc_scatter_rows(h2.reshape(dt * T, LANES), rows[0], rows[1], dt * nslots)
    ys = _experts(tile_expert, tile_rows, xs.reshape(dt, nslots, LANES),
                  w_expert_gate[0], w_expert_up[0], w_expert_down[0], tm_e)
    yg = _sc_gather_rows(ys.reshape(dt * nslots, LANES), rows.reshape(2 * dt, T))
    return _combine(yg.reshape(2, dt, T, LANES), x1, rinfo, g2,
                    ln2_g[0].reshape(1, D), ln2_b[0].reshape(1, D), alpha)
```

```python
import functools

import jax
import jax.numpy as jnp
from jax import lax
from jax.experimental import pallas as pl
from jax.experimental.pallas import tpu as pltpu
from jax.experimental.pallas import tpu_sc as plsc

F32 = jnp.float32
BF16 = jnp.bfloat16
HIGHEST = lax.Precision.HIGHEST

LANES = 128
HEAD_DIM = 64
LN_EPS = 1e-5
RMS_EPS = 1e-6
LOG2E = 1.4426950408889634
NEG_BIG = -1e30
HCHUNK = 16
ROW_TILE = 8
VMEM_LIMIT = 56 * 1024 * 1024


def _cparams(sem, vmem=VMEM_LIMIT):
    return pltpu.CompilerParams(dimension_semantics=sem, vmem_limit_bytes=vmem)


def _sigmoid(x):
    return 1.0 / (1.0 + jnp.exp(-x))


def _silu(x):
    return x * _sigmoid(x)


def _ada_kernel(c_ref, w_ref, b_ref, o_ref):
    c = c_ref[...]
    o_ref[...] = jnp.dot(_silu(c), w_ref[...], precision=HIGHEST,
                         preferred_element_type=F32) + b_ref[...]


def _ada(c, w_ada, b_ada):
    B, D = c.shape
    N = w_ada.shape[1]
    tn = 1024
    return pl.pallas_call(
        _ada_kernel,
        out_shape=jax.ShapeDtypeStruct((B, N), F32),
        grid=(N // tn,),
        in_specs=[pl.BlockSpec((B, D), lambda j: (0, 0)),
                  pl.BlockSpec((D, tn), lambda j: (0, j)),
                  pl.BlockSpec((1, tn), lambda j: (0, j))],
        out_specs=pl.BlockSpec((B, tn), lambda j: (0, j)),
        compiler_params=_cparams(("arbitrary",)),
    )(c, w_ada, b_ada.reshape(1, N))


def _inproj_kernel(x_ref, sc_ref, sh_ref, w_ref,
                   fq_ref, fk_ref, fv_ref, ff_ref, hq_ref, hf_ref, hi_ref, hg_ref, gf_ref, gh_ref,
                   *, segs, q_scale):
    h = (x_ref[...] * (1.0 + sc_ref[...]) + sh_ref[...]).astype(BF16)
    outs = (fq_ref, fk_ref, fv_ref, ff_ref, hq_ref, hf_ref, hi_ref, hg_ref, gf_ref, gh_ref)
    for idx, (o_ref, (a, b)) in enumerate(zip(outs, segs)):
        r = jnp.dot(h, w_ref[:, a:b], preferred_element_type=F32)
        if idx == 0:
            r = r * q_scale
        o_ref[...] = r.astype(o_ref.dtype)


def _inproj(x, sc1, sh1, w_packed, segs, tm=256):
    B, S, D = x.shape
    widths = [b - a for a, b in segs]
    dtypes = [BF16, BF16, BF16, F32, BF16, F32, BF16, BF16, BF16, BF16]
    out_shape = tuple(jax.ShapeDtypeStruct((B, S, w), dt) for w, dt in zip(widths, dtypes))
    out_specs = tuple(pl.BlockSpec((None, tm, w), lambda b, i: (b, i, 0)) for w in widths)
    vec = pl.BlockSpec((None, 1, D), lambda b, i: (b, 0, 0))
    return pl.pallas_call(
        functools.partial(_inproj_kernel, segs=tuple(segs), q_scale=HEAD_DIM ** -0.5 * LOG2E),
        out_shape=out_shape,
        grid=(B, S // tm),
        in_specs=[pl.BlockSpec((None, tm, D), lambda b, i: (b, i, 0)), vec, vec,
                  pl.BlockSpec(w_packed.shape, lambda b, i: (0, 0))],
        out_specs=out_specs,
        compiler_params=_cparams(("parallel", "parallel")),
    )(x, sc1, sh1, w_packed)


def _foxcum_kernel(ff_ref, b_ref, o_ref, *, blk):
    S = ff_ref.shape[0]
    r = lax.broadcasted_iota(jnp.int32, (blk, blk), 0)
    c = lax.broadcasted_iota(jnp.int32, (blk, blk), 1)
    lower = (r >= c).astype(F32)
    carry = jnp.zeros((1, LANES), F32)
    for j in range(S // blk):
        z = ff_ref[j * blk:(j + 1) * blk, :] + b_ref[...]
        lf = jnp.minimum(z, 0.0) - jnp.log(1.0 + jnp.exp(-jnp.abs(z)))
        cum = jnp.dot(lower, lf, precision=HIGHEST, preferred_element_type=F32) + carry
        o_ref[j * blk:(j + 1) * blk, :] = cum * LOG2E
        carry = cum[blk - 1:blk, :]


def _foxcum(ffp, bias_p, blk=256):
    B, S, _ = ffp.shape
    return pl.pallas_call(
        functools.partial(_foxcum_kernel, blk=blk),
        out_shape=jax.ShapeDtypeStruct((B, S, LANES), F32),
        grid=(B,),
        in_specs=[pl.BlockSpec((None, S, LANES), lambda b: (b, 0, 0)),
                  pl.BlockSpec((1, LANES), lambda b: (0, 0))],
        out_specs=pl.BlockSpec((None, S, LANES), lambda b: (b, 0, 0)),
        compiler_params=_cparams(("parallel",)),
    )(ffp, bias_p)


NCUM = 3


def _fox_kernel(q_ref, k_ref, v_ref, c_ref, o_ref, ka_sc, kb_sc, va_sc, vb_sc, *, tq, tk):
    p = pl.program_id(1)
    qi = pl.program_id(2)
    S = k_ref.shape[0]

    @pl.when(qi == 0)
    def _():
        lane = lax.broadcasted_iota(jnp.int32, (S, LANES), 1)
        rr = lax.broadcasted_iota(jnp.int32, (LANES, LANES), 0)
        cc = lax.broadcasted_iota(jnp.int32, (LANES, LANES), 1)
        rest = c_ref[...]
        placed = jnp.zeros((S, LANES), F32)
        for i in range(NCUM):
            piece = rest.astype(BF16)
            rest = rest - piece.astype(F32)
            sel = ((rr == 2 * p) & (cc == HEAD_DIM + i)) | ((rr == 2 * p + 1) & (cc == i))
            placed = placed + jnp.dot(piece, jnp.where(sel, 1.0, 0.0).astype(BF16), preferred_element_type=F32)
        k2 = k_ref[...].astype(F32)
        ka_sc[...] = jnp.where(lane < HEAD_DIM, k2, -placed).astype(BF16)
        kb_sc[...] = jnp.where(lane >= HEAD_DIM, k2, -placed).astype(BF16)
        vt = v_ref[...].astype(F32).T
        row = lax.broadcasted_iota(jnp.int32, (LANES, S), 0)
        va_sc[...] = jnp.where(row < HEAD_DIM, vt, jnp.where(row == HEAD_DIM, 1.0, 0.0)).astype(BF16)
        vb_sc[...] = jnp.where(row >= HEAD_DIM, vt, jnp.where(row == 0, 1.0, 0.0)).astype(BF16)

    q2 = q_ref[...].astype(F32)
    qlane = lax.broadcasted_iota(jnp.int32, (tq, LANES), 1)
    qa = jnp.where(qlane < HEAD_DIM, q2, jnp.where(qlane < HEAD_DIM + NCUM, 1.0, 0.0)).astype(BF16)
    qb = jnp.where(qlane >= HEAD_DIM, q2, jnp.where(qlane < NCUM, 1.0, 0.0)).astype(BF16)
    krow = lax.broadcasted_iota(jnp.int32, (tk, tq), 0)
    qcol = lax.broadcasted_iota(jnp.int32, (tk, tq), 1)
    nsub = tq // tk

    def block(k0, carry, diag_off):
        out = []
        for ksc, vsc, qh, (m, acc) in ((ka_sc, va_sc, qa, carry[:2]), (kb_sc, vb_sc, qb, carry[2:])):
            st = lax.dot_general(ksc[pl.ds(k0, tk), :], qh, (((1,), (1,)), ((), ())),
                                 preferred_element_type=F32)
            if diag_off is not None:
                st = jnp.where(krow + diag_off <= qcol, st, NEG_BIG)
            m_new = jnp.maximum(m, jnp.max(st, axis=0, keepdims=True))
            pt = jnp.exp2(st - m_new).astype(BF16)
            acc = jnp.exp2(m - m_new) * acc + jnp.dot(vsc[:, pl.ds(k0, tk)], pt, preferred_element_type=F32)
            out += [m_new, acc]
        return tuple(out)

    def pair(j, carry):
        k0 = pl.multiple_of(j * (2 * tk), 2 * tk)
        return block(k0 + tk, block(k0, carry, None), None)

    m0 = jnp.full((1, tq), NEG_BIG, F32)
    a0 = jnp.zeros((LANES, tq), F32)
    carry = lax.fori_loop(0, qi * (nsub // 2), pair, (m0, a0, m0, a0))
    for d in range(nsub):
        carry = block(pl.multiple_of(qi * tq + d * tk, tk), carry, d * tk)
    _, aa, _, ab = carry
    row = lax.broadcasted_iota(jnp.int32, (LANES, tq), 0)
    ot = jnp.where(row < HEAD_DIM, aa * (1.0 / aa[HEAD_DIM:HEAD_DIM + 1, :]), ab * (1.0 / ab[0:1, :]))
    o_ref[...] = ot.T.astype(o_ref.dtype)


def _fox(fq, fk, fv, cum, tq=512, tk=256):
    B, S, W = fq.shape
    assert tq % (2 * tk) == 0 and S % tq == 0
    npairs = W // LANES
    return pl.pallas_call(
        functools.partial(_fox_kernel, tq=tq, tk=tk),
        out_shape=jax.ShapeDtypeStruct((B, S, W), BF16),
        grid=(B, npairs, S // tq),
        in_specs=[pl.BlockSpec((None, tq, LANES), lambda b, p, i: (b, i, p)),
                  pl.BlockSpec((None, S, LANES), lambda b, p, i: (b, 0, p)),
                  pl.BlockSpec((None, S, LANES), lambda b, p, i: (b, 0, p)),
                  pl.BlockSpec((None, S, LANES), lambda b, p, i: (b, 0, 0))],
        out_specs=pl.BlockSpec((None, tq, LANES), lambda b, p, i: (b, i, p)),
        scratch_shapes=[pltpu.VMEM((S, LANES), BF16), pltpu.VMEM((S, LANES), BF16),
                        pltpu.VMEM((LANES, S), BF16), pltpu.VMEM((LANES, S), BF16)],
        compiler_params=_cparams(("parallel", "parallel", "arbitrary")),
    )(fq, fk, fv, cum)


def _hgrn_kernel(hq_ref, hf_ref, hi_ref, hg_ref, lb_ref, nw_ref, o_ref,
                 a_sc, qt_sc, kt_sc, kk_sc, qq_sc, p_sc, s_sc, o_sc, stb_sc, dec_sc):
    S = hq_ref.shape[0]
    C = HCHUNK
    nchunks = S // C

    lg = lb_ref[...]
    e = jnp.exp(lg - jnp.max(lg, axis=0, keepdims=True))
    lb = e[0:1, :] / jnp.sum(e, axis=0, keepdims=True)

    f = lb + (1.0 - lb) * _sigmoid(hf_ref[...])
    lf = jnp.log(f)
    kk = 1.0 - f
    qq = _silu(hq_ref[...].astype(F32))

    rmod = lax.broadcasted_iota(jnp.int32, (S, LANES), 0) & (C - 1)
    a = lf
    d = 1
    while d < C:
        a = a + jnp.where(rmod >= d, pltpu.roll(a, d, axis=0), 0.0)
        d *= 2
    a3 = a.reshape(nchunks, C, LANES)
    alast = jnp.broadcast_to(a3[:, C - 1:C, :], (nchunks, C, LANES)).reshape(S, LANES)
    a_sc[...] = a * LOG2E
    kk_sc[...] = kk
    qq_sc[...] = qq
    qt_sc[...] = (qq * jnp.exp(a)).astype(BF16)
    kt_sc[...] = (kk * jnp.exp(alast - a)).astype(BF16)
    dec_sc[...] = jnp.exp(a3[:, C - 1, :])

    lane = lax.broadcasted_iota(jnp.int32, (C, LANES), 1)
    trow = lax.broadcasted_iota(jnp.int32, (C, LANES), 0)

    def gen(c, _):
        r0 = pl.multiple_of(c * C, C)
        ac = a_sc[pl.ds(r0, C), :]
        qc = qq_sc[pl.ds(r0, C), :]
        kc = kk_sc[pl.ds(r0, C), :]
        half = C // 2
        for s in range(C):
            if s < half:
                dec = jnp.exp2(jnp.where(trow >= s, ac - ac[s:s + 1, :], NEG_BIG))
                p = qc * (kc[s:s + 1, :] * dec)
            else:
                dec = jnp.exp2(jnp.where(trow[half:] >= s, ac[half:] - ac[s:s + 1, :], NEG_BIG))
                p = jnp.concatenate([jnp.zeros((half, LANES), F32), qc[half:] * (kc[s:s + 1, :] * dec)], axis=0)
            p_sc[pl.ds(r0, C), s * LANES:(s + 1) * LANES] = p.astype(BF16)
        return 0

    lax.fori_loop(0, nchunks, gen, 0)

    er = lax.broadcasted_iota(jnp.int32, (C * LANES, LANES), 0)
    ec = lax.broadcasted_iota(jnp.int32, (C * LANES, LANES), 1)
    emat = (ec == ((er & (LANES - 1)) // HEAD_DIM) * C + er // LANES).astype(BF16)
    rb = 256

    def red(i, _):
        r0 = pl.multiple_of(i * rb, rb)
        s_sc[pl.ds(r0, rb), :] = jnp.dot(p_sc[pl.ds(r0, rb), :], emat,
                                         preferred_element_type=F32).astype(BF16)
        return 0

    lax.fori_loop(0, S // rb, red, 0)

    sr = lax.broadcasted_iota(jnp.int32, (LANES, LANES), 0)
    scn = lax.broadcasted_iota(jnp.int32, (LANES, LANES), 1)
    same_head = (sr // HEAD_DIM) == (scn // HEAD_DIM)
    unroll = 16

    def scan(g, st):
        for u in range(unroll):
            c = g * unroll + u
            r0 = pl.multiple_of(c * C, C)
            stb_sc[c] = st.astype(BF16)
            upd = lax.dot_general(hi_ref[pl.ds(r0, C), :], kt_sc[pl.ds(r0, C), :], (((0,), (0,)), ((), ())),
                                  preferred_element_type=F32)
            st = st * dec_sc[pl.ds(c, 1), :] + jnp.where(same_head, upd, 0.0)
        return st

    lax.fori_loop(0, nchunks // unroll, scan, jnp.zeros((LANES, LANES), F32))

    def readout(g, _):
        for u in range(unroll):
            c = g * unroll + u
            r0 = pl.multiple_of(c * C, C)
            vc = hi_ref[pl.ds(r0, C), :]
            o_inter = lax.dot_general(qt_sc[pl.ds(r0, C), :], stb_sc[c],
                                      (((1,), (1,)), ((), ())), preferred_element_type=F32)
            v2 = jnp.concatenate([jnp.where(lane < HEAD_DIM, vc, jnp.zeros_like(vc)),
                                  jnp.where(lane >= HEAD_DIM, vc, jnp.zeros_like(vc))], axis=0)
            o_intra = jnp.dot(s_sc[pl.ds(r0, C), :][:, :2 * C], v2, preferred_element_type=F32)
            o_sc[pl.ds(r0, C), :] = o_inter + o_intra
        return 0

    lax.fori_loop(0, nchunks // unroll, readout, 0)

    o = o_sc[...]
    ones_head = jnp.where(same_head, 1.0 / HEAD_DIM, 0.0).astype(F32)
    ms = jnp.dot(o * o, ones_head, precision=HIGHEST, preferred_element_type=F32)
    y = o * lax.rsqrt(ms + RMS_EPS) * nw_ref[...]
    o_ref[...] = (y * _silu(hg_ref[...].astype(F32))).astype(o_ref.dtype)


def _hgrn(hq, hf, hi, hg, lb_logits, norm_w):
    B, S, W = hq.shape
    npairs = W // LANES
    nrows = lb_logits.shape[0]
    seq = pl.BlockSpec((None, S, LANES), lambda b, p: (b, 0, p))
    return pl.pallas_call(
        _hgrn_kernel,
        out_shape=jax.ShapeDtypeStruct((B, S, W), BF16),
        grid=(B, npairs),
        in_specs=[seq, seq, seq, seq,
                  pl.BlockSpec((nrows, LANES), lambda b, p: (0, p)),
                  pl.BlockSpec((1, LANES), lambda b, p: (0, p))],
        out_specs=seq,
        scratch_shapes=[pltpu.VMEM((S, LANES), F32),
                        pltpu.VMEM((S, LANES), BF16),
                        pltpu.VMEM((S, LANES), BF16),
                        pltpu.VMEM((S, LANES), F32),
                        pltpu.VMEM((S, LANES), F32),
                        pltpu.VMEM((S, HCHUNK * LANES), BF16),
                        pltpu.VMEM((S, LANES), BF16),
                        pltpu.VMEM((S, LANES), F32),
                        pltpu.VMEM((S // HCHUNK, LANES, LANES), BF16),
                        pltpu.VMEM((S // HCHUNK, LANES), F32)],
        compiler_params=_cparams(("parallel", "parallel")),
    )(hq, hf, hi, hg, lb_logits, norm_w.reshape(1, W))


def _layer_norm(v, g, b):
    mu = jnp.mean(v, axis=-1, keepdims=True)
    d = v - mu
    var = jnp.mean(d * d, axis=-1, keepdims=True)
    return d * lax.rsqrt(var + LN_EPS) * g + b


def _store_chunks(ref, val):
    for j in range(ref.shape[0]):
        ref[j] = val[:, j * LANES:(j + 1) * LANES]


def _load_chunks(ref):
    return jnp.concatenate([ref[j] for j in range(ref.shape[0])], axis=1)


def _mix_kernel(yf_ref, oh_ref, gf_ref, gh_ref, x_ref, g1_ref, sc2_ref, sh2_ref,
                wuf_ref, wuh_ref, wo_ref, lg_ref, lbias_ref, wr_ref, br_ref,
                x1_ref, h2_ref, ri_ref, rt_ref, cnt_ref, carry_sc, *, alpha, ngroups, nper):
    first = (pl.program_id(0) == 0) & (pl.program_id(1) == 0)

    @pl.when(first)
    def _():
        carry_sc[...] = jnp.zeros_like(carry_sc)

    tm = x_ref.shape[0]
    yf = jnp.dot(yf_ref[...], wuf_ref[...], preferred_element_type=F32)
    yh = jnp.dot(oh_ref[...], wuh_ref[...], preferred_element_type=F32)
    merged = _sigmoid(gf_ref[...].astype(F32)) * yf + _sigmoid(gh_ref[...].astype(F32)) * yh
    y = jnp.dot(merged.astype(BF16), wo_ref[...], preferred_element_type=F32)
    x1 = _layer_norm(alpha * x_ref[...] + g1_ref[...] * y, lg_ref[...], lbias_ref[...])
    x1_ref[...] = x1
    h2 = x1 * (1.0 + sc2_ref[...]) + sh2_ref[...]
    _store_chunks(h2_ref, h2)

    h_top = pltpu.bitcast(pltpu.bitcast(h2, jnp.uint32) & jnp.uint32(0xFFFF0000), F32)
    h_hi = h_top.astype(BF16)
    h_lo = (h2 - h_top).astype(BF16)
    logits = (jnp.dot(h_hi, wr_ref[0], preferred_element_type=F32)
              + jnp.dot(h_hi, wr_ref[1], preferred_element_type=F32)
              + jnp.dot(h_lo, wr_ref[0], preferred_element_type=F32)) + br_ref[...]
    lane = lax.broadcasted_iota(jnp.int32, (tm, LANES), 1)
    big = jnp.int32(1 << 20)

    def argmax_first(vals, mask):
        mx = jnp.max(jnp.where(mask, vals, -jnp.inf), axis=1, keepdims=True)
        idx = jnp.min(jnp.where(mask & (vals == mx), lane, big), axis=1, keepdims=True)
        return mx, idx

    gmask = lane < ngroups
    gmax = jnp.max(jnp.where(gmask, logits, -jnp.inf), axis=1, keepdims=True)
    gexp = jnp.where(gmask, jnp.exp(logits - gmax), 0.0)
    gprob = gexp / jnp.sum(gexp, axis=1, keepdims=True)
    g_w, g_idx = argmax_first(gprob, gmask)

    lo = ngroups + g_idx * nper
    emask = (lane >= lo) & (lane < lo + nper)
    emax = jnp.max(jnp.where(emask, logits, -jnp.inf), axis=1, keepdims=True)
    eexp = jnp.where(emask, jnp.exp(logits - emax), 0.0)
    eprob = eexp / jnp.sum(eexp, axis=1, keepdims=True)
    p0, i0 = argmax_first(eprob, emask)
    p1, i1 = argmax_first(eprob, emask & (lane != i0))
    den = p0 + p1
    w0 = p0 / den * g_w
    w1 = p1 / den * g_w
    e0 = i0 - ngroups
    e1 = i1 - ngroups

    oh = ((lane == e0) | (lane == e1)).astype(F32)
    r = lax.broadcasted_iota(jnp.int32, (tm, tm), 0)
    c = lax.broadcasted_iota(jnp.int32, (tm, tm), 1)
    strict_lower = (c < r).astype(BF16)
    before = jnp.dot(strict_lower, oh.astype(BF16), preferred_element_type=F32) + carry_sc[...]
    rank0 = jnp.sum(jnp.where(lane == e0, before, 0.0), axis=1, keepdims=True)
    rank1 = jnp.sum(jnp.where(lane == e1, before, 0.0), axis=1, keepdims=True)
    carry_sc[...] = carry_sc[...] + jnp.sum(oh, axis=0, keepdims=True)
    cnt_ref[...] = carry_sc[...]

    info = jnp.where(lane == 0, w0, 0.0)
    info = jnp.where(lane == 1, w1, info)
    info = jnp.where(lane == 2, e0.astype(F32), info)
    info = jnp.where(lane == 3, e1.astype(F32), info)
    info = jnp.where(lane == 4, rank0, info)
    info = jnp.where(lane == 5, rank1, info)
    ri_ref[...] = info
    rt_ref[...] = info.T[:ROW_TILE, :]


def _mix(yf, oh, gf, gh, x, g1, sc2, sh2, wuf, wuh, wo, ln_g, ln_b, wr, br, alpha, ngroups, nper, tm=512):
    B, S, D = x.shape
    W = yf.shape[2]
    tok = lambda w: pl.BlockSpec((None, tm, w), lambda b, i: (b, i, 0))
    vec = pl.BlockSpec((None, 1, D), lambda b, i: (b, 0, 0))
    full = lambda a: pl.BlockSpec(a.shape, lambda b, i: (0,) * a.ndim)
    return pl.pallas_call(
        functools.partial(_mix_kernel, alpha=alpha, ngroups=ngroups, nper=nper),
        out_shape=(jax.ShapeDtypeStruct((B, S, D), F32),
                   jax.ShapeDtypeStruct((D // LANES, B * S, LANES), F32),
                   jax.ShapeDtypeStruct((B, S, LANES), F32),
                   jax.ShapeDtypeStruct((ROW_TILE, B * S), F32),
                   jax.ShapeDtypeStruct((1, LANES), F32)),
        grid=(B, S // tm),
        in_specs=[tok(W), tok(W), tok(D), tok(D), tok(D), vec, vec, vec,
                  full(wuf), full(wuh), full(wo), full(ln_g), full(ln_b), full(wr), full(br)],
        out_specs=(tok(D),
                   pl.BlockSpec((D // LANES, tm, LANES), lambda b, i: (0, b * (S // tm) + i, 0)),
                   tok(LANES),
                   pl.BlockSpec((ROW_TILE, tm), lambda b, i: (0, b * (S // tm) + i)),
                   pl.BlockSpec((1, LANES), lambda b, i: (0, 0))),
        scratch_shapes=[pltpu.VMEM((1, LANES), F32)],
        compiler_params=_cparams(("arbitrary", "arbitrary")),
    )(yf, oh, gf, gh, x, g1, sc2, sh2, wuf, wuh, wo, ln_g, ln_b, wr, br)


def _sc_mesh():
    return plsc.VectorSubcoreMesh(core_axis_name="core", subcore_axis_name="subcore")


def _sc_pipeline(body, grid, in_specs, out_specs):
    return pltpu.emit_pipeline(body, grid=grid, in_specs=in_specs, out_specs=out_specs,
                               core_axis_name=("core", "subcore"),
                               dimension_semantics=(pltpu.PARALLEL,) * len(grid))


def _sc_scatter_rows(src, rows_a, rows_b, n_out):
    nj, t = rows_a.shape
    nc = t // LANES

    @pl.kernel(out_type=jax.ShapeDtypeStruct((n_out, LANES), src.dtype), mesh=_sc_mesh(), scratch_types=[])
    def scatter(x_hbm, a_hbm, b_hbm, o_hbm):
        def body(x_vmem, a_vmem, b_vmem):
            pltpu.sync_copy(x_vmem, o_hbm.at[a_vmem.at[0]])
            pltpu.sync_copy(x_vmem, o_hbm.at[b_vmem.at[0]])

        idx = pl.BlockSpec((1, LANES), lambda j, c: (j, c))
        _sc_pipeline(body, (nj, nc), [pl.BlockSpec((LANES, LANES), lambda j, c: (j * nc + c, 0)), idx, idx],
                     [])(x_hbm, a_hbm, b_hbm)

    return scatter(src, rows_a, rows_b)


def _sc_gather_rows(table, rows):
    nr, t = rows.shape
    nc = t // LANES

    @pl.kernel(out_type=jax.ShapeDtypeStruct((nr * t, LANES), table.dtype), mesh=_sc_mesh(), scratch_types=[])
    def gather(x_hbm, i_hbm, o_hbm):
        def body(i_vmem, o_vmem):
            pltpu.sync_copy(x_hbm.at[i_vmem.at[0]], o_vmem)

        _sc_pipeline(body, (nr, nc), [pl.BlockSpec((1, LANES), lambda r, c: (r, c))],
                     [pl.BlockSpec((LANES, LANES), lambda r, c: (r * nc + c, 0))])(i_hbm, o_hbm)

    return gather(table, rows)


def _experts_kernel(te_ref, tn_ref, x_ref, wg_ref, wu_ref, wd_ref, o_ref):
    nrows = tn_ref[pl.program_id(0)]

    @pl.when(nrows > 0)
    def _():
        x = _load_chunks(x_ref)
        x = jnp.where(lax.broadcasted_iota(jnp.int32, x.shape, 0) < nrows, x, 0.0).astype(BF16)
        g = jnp.dot(x, wg_ref[...].astype(BF16), preferred_element_type=F32)
        u = jnp.dot(x, wu_ref[...].astype(BF16), preferred_element_type=F32)
        hid = (_silu(g) * u).astype(BF16)
        _store_chunks(o_ref, jnp.dot(hid, wd_ref[...].astype(BF16), preferred_element_type=F32))

    @pl.when(nrows == 0)
    def _():
        o_ref[...] = jnp.zeros_like(o_ref)


def _experts(tile_expert, tile_rows, xs, wg, wu, wd, tm):
    E, D, FF = wg.shape
    dt = D // LANES
    ntiles = tile_expert.shape[0]
    rows = pl.BlockSpec((dt, tm, LANES), lambda i, te, tn: (0, i, 0))
    grid_spec = pltpu.PrefetchScalarGridSpec(
        num_scalar_prefetch=2,
        grid=(ntiles,),
        in_specs=[rows,
                  pl.BlockSpec((None, D, FF), lambda i, te, tn: (te[i], 0, 0)),
                  pl.BlockSpec((None, D, FF), lambda i, te, tn: (te[i], 0, 0)),
                  pl.BlockSpec((None, FF, D), lambda i, te, tn: (te[i], 0, 0))],
        out_specs=rows,
    )
    return pl.pallas_call(
        _experts_kernel,
        out_shape=jax.ShapeDtypeStruct((dt, ntiles * tm, LANES), F32),
        grid_spec=grid_spec,
        compiler_params=_cparams(("arbitrary",)),
    )(tile_expert, tile_rows, xs, wg, wu, wd)


def _combine_kernel(yg_ref, x1_ref, ri_ref, g2_ref, lg_ref, lb_ref, o_ref, *, alpha):
    ri = ri_ref[...]
    y = ri[:, 0:1] * _load_chunks(yg_ref.at[0]) + ri[:, 1:2] * _load_chunks(yg_ref.at[1])
    o_ref[...] = _layer_norm(alpha * x1_ref[...] + g2_ref[...] * y, lg_ref[...], lb_ref[...])


def _combine(yg, x1, rinfo, g2, ln_g, ln_b, alpha, tm=256):
    B, S, D = x1.shape
    nb = S // tm
    return pl.pallas_call(
        functools.partial(_combine_kernel, alpha=alpha),
        out_shape=jax.ShapeDtypeStruct((B, S, D), F32),
        grid=(B, nb),
        in_specs=[pl.BlockSpec((2, D // LANES, tm, LANES), lambda b, i: (0, 0, b * nb + i, 0)),
                  pl.BlockSpec((None, tm, D), lambda b, i: (b, i, 0)),
                  pl.BlockSpec((None, tm, LANES), lambda b, i: (b, i, 0)),
                  pl.BlockSpec((None, 1, D), lambda b, i: (b, 0, 0)),
                  pl.BlockSpec((1, D), lambda b, i: (0, 0)),
                  pl.BlockSpec((1, D), lambda b, i: (0, 0))],
        out_specs=pl.BlockSpec((None, tm, D), lambda b, i: (b, i, 0)),
        compiler_params=_cparams(("parallel", "parallel")),
    )(yg, x1, rinfo, g2, ln_g, ln_b)


def kernel(x, c, w_ada, b_ada, w_in, b_fox_forget, hgrn_lb_logits, hgrn_norm_w, w_up_fox, w_up_hgrn, w_out,
           ln1_g, ln1_b, w_router_group, b_router_group, w_router_expert, b_router_expert,
           w_expert_gate, w_expert_up, w_expert_down, ln2_g, ln2_b):
    B, S, D = x.shape
    depth = w_ada.shape[0]
    assert depth == 1, "single-layer block"
    fox_heads = b_fox_forget.shape[1]
    fox_w = fox_heads * HEAD_DIM
    hgrn_w = hgrn_norm_w.shape[1]
    ngroups = w_router_group.shape[2]
    nexp = w_router_expert.shape[2]
    nper = nexp // ngroups
    alpha = (2 * depth) ** 0.25
    T = B * S

    ada = _ada(c, w_ada[0], b_ada[0])
    sh1, sc1, g1, sh2, sc2, g2 = [a.reshape(B, 1, D) for a in jnp.split(ada, 6, axis=-1)]

    wi = w_in[0]
    o_ff = 3 * fox_w
    w_packed = jnp.concatenate(
        [wi[:, :o_ff + fox_heads], jnp.zeros((D, LANES - fox_heads), wi.dtype), wi[:, o_ff + fox_heads:]],
        axis=1).astype(BF16)
    widths = [fox_w, fox_w, fox_w, LANES, hgrn_w, hgrn_w, hgrn_w, hgrn_w, D, D]
    segs, off = [], 0
    for w in widths:
        segs.append((off, off + w))
        off += w
    fq, fk, fv, ffp, hq, hf, hi, hg, gf, gh = _inproj(x, sc1, sh1, w_packed, segs)

    bias_p = jnp.zeros((1, LANES), F32).at[0, :fox_heads].set(b_fox_forget[0])
    cum = _foxcum(ffp, bias_p)
    y_fox = _fox(fq, fk, fv, cum)

    o_h = _hgrn(hq, hf, hi, hg, hgrn_lb_logits, hgrn_norm_w[0])

    wr = jnp.zeros((D, LANES), F32).at[:, :ngroups].set(w_router_group[0]).at[:, ngroups:ngroups + nexp].set(
        w_router_expert[0])
    wr_hi = lax.bitcast_convert_type(lax.bitcast_convert_type(wr, jnp.uint32) & jnp.uint32(0xFFFF0000), F32)
    wr = jnp.stack([wr_hi.astype(BF16), (wr - wr_hi).astype(BF16)])
    br = jnp.zeros((1, LANES), F32).at[0, :ngroups].set(b_router_group[0]).at[0, ngroups:ngroups + nexp].set(
        b_router_expert[0])
    x1, h2, rinfo, fields, counts = _mix(
        y_fox, o_h, gf, gh, x, g1, sc2, sh2,
        w_up_fox[0].astype(BF16), w_up_hgrn[0].astype(BF16), w_out[0].astype(BF16),
        ln1_g[0].reshape(1, D), ln1_b[0].reshape(1, D), wr, br, alpha, ngroups, nper)

    tm_e = 256
    dt = D // LANES
    ntiles = (2 * T) // tm_e + nexp
    nslots = ntiles * tm_e
    cnt = counts[0, :nexp].astype(jnp.int32)
    padded = ((cnt + tm_e - 1) // tm_e) * tm_e
    ends = jnp.cumsum(padded)
    starts = ends - padded
    eid = fields[2:4].astype(jnp.int32)
    rank = fields[4:6].astype(jnp.int32)
    first = jnp.sum(jnp.where(eid[None] == jnp.arange(nexp, dtype=jnp.int32)[:, None, None],
                              starts[:, None, None], 0), axis=0)
    pos = first + rank
    tile_start = jnp.arange(ntiles, dtype=jnp.int32) * tm_e
    tile_expert = jnp.minimum(jnp.sum((tile_start[:, None] >= ends[None, :]).astype(jnp.int32), axis=1), nexp - 1)
    tile_rows = jnp.clip(starts[tile_expert] + cnt[tile_expert] - tile_start, 0, tm_e)
    rows = pos[:, None, :] + (jnp.arange(dt, dtype=jnp.int32) * nslots)[None, :, None]

    xs = _sc_scatter_rows(h2.reshape(dt * T, LANES), rows[0], rows[1], dt * nslots)
    ys = _experts(tile_expert, tile_rows, xs.reshape(dt, nslots, LANES),
                  w_expert_gate[0], w_expert_up[0], w_expert_down[0], tm_e)
    yg = _sc_gather_rows(ys.reshape(dt * nslots, LANES), rows.reshape(2 * dt, T))
    return _combine(yg.reshape(2, dt, T, LANES), x1, rinfo, g2,
                    ln2_g[0].reshape(1, D), ln2_b[0].reshape(1, D), alpha)
```

```python
import functools

import jax
import jax.numpy as jnp
from jax import lax
from jax.experimental import pallas as pl
from jax.experimental.pallas import tpu as pltpu
from jax.experimental.pallas import tpu_sc as plsc

F32 = jnp.float32
BF16 = jnp.bfloat16
HIGHEST = lax.Precision.HIGHEST

LANES = 128
HEAD_DIM = 64
LN_EPS = 1e-5
RMS_EPS = 1e-6
LOG2E = 1.4426950408889634
NEG_BIG = -1e30
HCHUNK = 16
ROW_TILE = 8
VMEM_LIMIT = 56 * 1024 * 1024


def _cparams(sem, vmem=VMEM_LIMIT):
    return pltpu.CompilerParams(dimension_semantics=sem, vmem_limit_bytes=vmem)


def _sigmoid(x):
    return 1.0 / (1.0 + jnp.exp(-x))


def _silu(x):
    return x * _sigmoid(x)


def _ada_kernel(c_ref, w_ref, b_ref, o_ref):
    c = c_ref[...]
    o_ref[...] = jnp.dot(_silu(c), w_ref[...], precision=HIGHEST,
                         preferred_element_type=F32) + b_ref[...]


def _ada(c, w_ada, b_ada):
    B, D = c.shape
    N = w_ada.shape[1]
    tn = 1024
    return pl.pallas_call(
        _ada_kernel,
        out_shape=jax.ShapeDtypeStruct((B, N), F32),
        grid=(N // tn,),
        in_specs=[pl.BlockSpec((B, D), lambda j: (0, 0)),
                  pl.BlockSpec((D, tn), lambda j: (0, j)),
                  pl.BlockSpec((1, tn), lambda j: (0, j))],
        out_specs=pl.BlockSpec((B, tn), lambda j: (0, j)),
        compiler_params=_cparams(("arbitrary",)),
    )(c, w_ada, b_ada.reshape(1, N))


def _inproj_kernel(x_ref, sc_ref, sh_ref, w_ref,
                   fq_ref, fk_ref, fv_ref, ff_ref, hq_ref, hf_ref, hi_ref, hg_ref, gf_ref, gh_ref,
                   *, segs, q_scale):
    h = (x_ref[...] * (1.0 + sc_ref[...]) + sh_ref[...]).astype(BF16)
    outs = (fq_ref, fk_ref, fv_ref, ff_ref, hq_ref, hf_ref, hi_ref, hg_ref, gf_ref, gh_ref)
    for idx, (o_ref, (a, b)) in enumerate(zip(outs, segs)):
        r = jnp.dot(h, w_ref[:, a:b], preferred_element_type=F32)
        if idx == 0:
            r = r * q_scale
        o_ref[...] = r.astype(o_ref.dtype)


def _inproj(x, sc1, sh1, w_packed, segs, tm=256):
    B, S, D = x.shape
    widths = [b - a for a, b in segs]
    dtypes = [BF16, BF16, BF16, F32, BF16, F32, BF16, BF16, BF16, BF16]
    out_shape = tuple(jax.ShapeDtypeStruct((B, S, w), dt) for w, dt in zip(widths, dtypes))
    out_specs = tuple(pl.BlockSpec((None, tm, w), lambda b, i: (b, i, 0)) for w in widths)
    vec = pl.BlockSpec((None, 1, D), lambda b, i: (b, 0, 0))
    return pl.pallas_call(
        functools.partial(_inproj_kernel, segs=tuple(segs), q_scale=HEAD_DIM ** -0.5 * LOG2E),
        out_shape=out_shape,
        grid=(B, S // tm),
        in_specs=[pl.BlockSpec((None, tm, D), lambda b, i: (b, i, 0)), vec, vec,
                  pl.BlockSpec(w_packed.shape, lambda b, i: (0, 0))],
        out_specs=out_specs,
        compiler_params=_cparams(("parallel", "parallel")),
    )(x, sc1, sh1, w_packed)


def _foxcum_kernel(ff_ref, b_ref, o_ref, *, blk):
    S = ff_ref.shape[0]
    r = lax.broadcasted_iota(jnp.int32, (blk, blk), 0)
    c = lax.broadcasted_iota(jnp.int32, (blk, blk), 1)
    lower = (r >= c).astype(F32)
    carry = jnp.zeros((1, LANES), F32)
    for j in range(S // blk):
        z = ff_ref[j * blk:(j + 1) * blk, :] + b_ref[...]
        lf = jnp.minimum(z, 0.0) - jnp.log(1.0 + jnp.exp(-jnp.abs(z)))
        cum = jnp.dot(lower, lf, precision=HIGHEST, preferred_element_type=F32) + carry
        o_ref[j * blk:(j + 1) * blk, :] = cum * LOG2E
        carry = cum[blk - 1:blk, :]


def _foxcum(ffp, bias_p, blk=256):
    B, S, _ = ffp.shape
    return pl.pallas_call(
        functools.partial(_foxcum_kernel, blk=blk),
        out_shape=jax.ShapeDtypeStruct((B, S, LANES), F32),
        grid=(B,),
        in_specs=[pl.BlockSpec((None, S, LANES), lambda b: (b, 0, 0)),
                  pl.BlockSpec((1, LANES), lambda b: (0, 0))],
        out_specs=pl.BlockSpec((None, S, LANES), lambda b: (b, 0, 0)),
        compiler_params=_cparams(("parallel",)),
    )(ffp, bias_p)


NCUM = 3


def _fox_kernel(q_ref, k_ref, v_ref, c_ref, o_ref, ka_sc, kb_sc, va_sc, vb_sc, *, tq, tk):
    p = pl.program_id(1)
    qi = pl.program_id(2)
    S = k_ref.shape[0]

    @pl.when(qi == 0)
    def _():
        lane = lax.broadcasted_iota(jnp.int32, (S, LANES), 1)
        rr = lax.broadcasted_iota(jnp.int32, (LANES, LANES), 0)
        cc = lax.broadcasted_iota(jnp.int32, (LANES, LANES), 1)
        rest = c_ref[...]
        placed = jnp.zeros((S, LANES), F32)
        for i in range(NCUM):
            piece = rest.astype(BF16)
            rest = rest - piece.astype(F32)
            sel = ((rr == 2 * p) & (cc == HEAD_DIM + i)) | ((rr == 2 * p + 1) & (cc == i))
            placed = placed + jnp.dot(piece, jnp.where(sel, 1.0, 0.0).astype(BF16), preferred_element_type=F32)
        k2 = k_ref[...].astype(F32)
        ka_sc[...] = jnp.where(lane < HEAD_DIM, k2, -placed).astype(BF16)
        kb_sc[...] = jnp.where(lane >= HEAD_DIM, k2, -placed).astype(BF16)
        vt = v_ref[...].astype(F32).T
        row = lax.broadcasted_iota(jnp.int32, (LANES, S), 0)
        va_sc[...] = jnp.where(row < HEAD_DIM, vt, jnp.where(row == HEAD_DIM, 1.0, 0.0)).astype(BF16)
        vb_sc[...] = jnp.where(row >= HEAD_DIM, vt, jnp.where(row == 0, 1.0, 0.0)).astype(BF16)

    q2 = q_ref[...].astype(F32)
    qlane = lax.broadcasted_iota(jnp.int32, (tq, LANES), 1)
    qa = jnp.where(qlane < HEAD_DIM, q2, jnp.where(qlane < HEAD_DIM + NCUM, 1.0, 0.0)).astype(BF16)
    qb = jnp.where(qlane >= HEAD_DIM, q2, jnp.where(qlane < NCUM, 1.0, 0.0)).astype(BF16)
    nsub = tq // tk

    def block(k0, carry, diag_off):
        q0 = 0 if diag_off is None else diag_off
        out = []
        for ksc, vsc, qh, (m, acc) in ((ka_sc, va_sc, qa, carry[:2]), (kb_sc, vb_sc, qb, carry[2:])):
            st = lax.dot_general(ksc[pl.ds(k0, tk), :], qh[q0:, :], (((1,), (1,)), ((), ())),
                                 preferred_element_type=F32)
            if diag_off is not None:
                st = jnp.where(lax.broadcasted_iota(jnp.int32, st.shape, 0)
                               <= lax.broadcasted_iota(jnp.int32, st.shape, 1), st, NEG_BIG)
            m_old = m[:, q0:]
            m_new = jnp.maximum(m_old, jnp.max(st, axis=0, keepdims=True))
            pt = jnp.exp2(st - m_new).astype(BF16)
            acc_new = (jnp.exp2(m_old - m_new) * acc[:, q0:]
                       + jnp.dot(vsc[:, pl.ds(k0, tk)], pt, preferred_element_type=F32))
            if q0:
                m_new = jnp.concatenate([m[:, :q0], m_new], axis=1)
                acc_new = jnp.concatenate([acc[:, :q0], acc_new], axis=1)
            out += [m_new, acc_new]
        return tuple(out)

    def group(j, carry):
        k0 = pl.multiple_of(j * (nsub * tk), nsub * tk)
        for u in range(nsub):
            carry = block(k0 + u * tk, carry, None)
        return carry

    m0 = jnp.full((1, tq), NEG_BIG, F32)
    a0 = jnp.zeros((LANES, tq), F32)
    carry = lax.fori_loop(0, qi, group, (m0, a0, m0, a0))
    for d in range(nsub):
        carry = block(pl.multiple_of(qi * tq + d * tk, tk), carry, d * tk)
    _, aa, _, ab = carry
    row = lax.broadcasted_iota(jnp.int32, (LANES, tq), 0)
    ot = jnp.where(row < HEAD_DIM, aa * (1.0 / aa[HEAD_DIM:HEAD_DIM + 1, :]), ab * (1.0 / ab[0:1, :]))
    o_ref[...] = ot.T.astype(o_ref.dtype)


def _fox(fq, fk, fv, cum, tq=1024, tk=256):
    B, S, W = fq.shape
    assert tq % (2 * tk) == 0 and S % tq == 0
    npairs = W // LANES
    return pl.pallas_call(
        functools.partial(_fox_kernel, tq=tq, tk=tk),
        out_shape=jax.ShapeDtypeStruct((B, S, W), BF16),
        grid=(B, npairs, S // tq),
        in_specs=[pl.BlockSpec((None, tq, LANES), lambda b, p, i: (b, i, p)),
                  pl.BlockSpec((None, S, LANES), lambda b, p, i: (b, 0, p)),
                  pl.BlockSpec((None, S, LANES), lambda b, p, i: (b, 0, p)),
                  pl.BlockSpec((None, S, LANES), lambda b, p, i: (b, 0, 0))],
        out_specs=pl.BlockSpec((None, tq, LANES), lambda b, p, i: (b, i, p)),
        scratch_shapes=[pltpu.VMEM((S, LANES), BF16), pltpu.VMEM((S, LANES), BF16),
                        pltpu.VMEM((LANES, S), BF16), pltpu.VMEM((LANES, S), BF16)],
        compiler_params=_cparams(("parallel", "parallel", "arbitrary")),
    )(fq, fk, fv, cum)


def _hgrn_kernel(hq_ref, hf_ref, hi_ref, hg_ref, lb_ref, nw_ref, o_ref,
                 a_sc, qt_sc, kt_sc, kk_sc, qq_sc, p_sc, s_sc, o_sc, stb_sc, dec_sc):
    S = hq_ref.shape[0]
    C = HCHUNK
    nchunks = S // C

    lg = lb_ref[...]
    e = jnp.exp(lg - jnp.max(lg, axis=0, keepdims=True))
    lb = e[0:1, :] / jnp.sum(e, axis=0, keepdims=True)

    f = lb + (1.0 - lb) * _sigmoid(hf_ref[...])
    lf = jnp.log(f)
    kk = 1.0 - f
    qq = _silu(hq_ref[...].astype(F32))

    rmod = lax.broadcasted_iota(jnp.int32, (S, LANES), 0) & (C - 1)
    a = lf
    d = 1
    while d < C:
        a = a + jnp.where(rmod >= d, pltpu.roll(a, d, axis=0), 0.0)
        d *= 2
    a3 = a.reshape(nchunks, C, LANES)
    alast = jnp.broadcast_to(a3[:, C - 1:C, :], (nchunks, C, LANES)).reshape(S, LANES)
    a_sc[...] = a * LOG2E
    kk_sc[...] = kk
    qq_sc[...] = qq
    qt_sc[...] = (qq * jnp.exp(a)).astype(BF16)
    kt_sc[...] = (kk * jnp.exp(alast - a)).astype(BF16)
    dec_sc[...] = jnp.exp(a3[:, C - 1, :])

    lane = lax.broadcasted_iota(jnp.int32, (C, LANES), 1)
    trow = lax.broadcasted_iota(jnp.int32, (C, LANES), 0)

    def gen(c, _):
        r0 = pl.multiple_of(c * C, C)
        ac = a_sc[pl.ds(r0, C), :]
        qc = qq_sc[pl.ds(r0, C), :]
        kc = kk_sc[pl.ds(r0, C), :]
        half = C // 2
        for s in range(C):
            if s < half:
                dec = jnp.exp2(jnp.where(trow >= s, ac - ac[s:s + 1, :], NEG_BIG))
                p = qc * (kc[s:s + 1, :] * dec)
            else:
                dec = jnp.exp2(jnp.where(trow[half:] >= s, ac[half:] - ac[s:s + 1, :], NEG_BIG))
                p = jnp.concatenate([jnp.zeros((half, LANES), F32), qc[half:] * (kc[s:s + 1, :] * dec)], axis=0)
            p_sc[pl.ds(r0, C), s * LANES:(s + 1) * LANES] = p.astype(BF16)
        return 0

    lax.fori_loop(0, nchunks, gen, 0)

    er = lax.broadcasted_iota(jnp.int32, (C * LANES, LANES), 0)
    ec = lax.broadcasted_iota(jnp.int32, (C * LANES, LANES), 1)
    emat = (ec == ((er & (LANES - 1)) // HEAD_DIM) * C + er // LANES).astype(BF16)
    rb = 256

    def red(i, _):
        r0 = pl.multiple_of(i * rb, rb)
        s_sc[pl.ds(r0, rb), :] = jnp.dot(p_sc[pl.ds(r0, rb), :], emat,
                                         preferred_element_type=F32).astype(BF16)
        return 0

    lax.fori_loop(0, S // rb, red, 0)

    sr = lax.broadcasted_iota(jnp.int32, (LANES, LANES), 0)
    scn = lax.broadcasted_iota(jnp.int32, (LANES, LANES), 1)
    same_head = (sr // HEAD_DIM) == (scn // HEAD_DIM)
    unroll = 16

    def scan(g, st):
        for u in range(unroll):
            c = g * unroll + u
            r0 = pl.multiple_of(c * C, C)
            stb_sc[c] = st.astype(BF16)
            upd = lax.dot_general(hi_ref[pl.ds(r0, C), :], kt_sc[pl.ds(r0, C), :], (((0,), (0,)), ((), ())),
                                  preferred_element_type=F32)
            st = st * dec_sc[pl.ds(c, 1), :] + jnp.where(same_head, upd, 0.0)
        return st

    lax.fori_loop(0, nchunks // unroll, scan, jnp.zeros((LANES, LANES), F32))

    def readout(g, _):
        for u in range(unroll):
            c = g * unroll + u
            r0 = pl.multiple_of(c * C, C)
            vc = hi_ref[pl.ds(r0, C), :]
            o_inter = lax.dot_general(qt_sc[pl.ds(r0, C), :], stb_sc[c],
                                      (((1,), (1,)), ((), ())), preferred_element_type=F32)
            v2 = jnp.concatenate([jnp.where(lane < HEAD_DIM, vc, jnp.zeros_like(vc)),
                                  jnp.where(lane >= HEAD_DIM, vc, jnp.zeros_like(vc))], axis=0)
            o_intra = jnp.dot(s_sc[pl.ds(r0, C), :][:, :2 * C], v2, preferred_element_type=F32)
            o_sc[pl.ds(r0, C), :] = o_inter + o_intra
        return 0

    lax.fori_loop(0, nchunks // unroll, readout, 0)

    o = o_sc[...]
    ones_head = jnp.where(same_head, 1.0 / HEAD_DIM, 0.0).astype(F32)
    ms = jnp.dot(o * o, ones_head, precision=HIGHEST, preferred_element_type=F32)
    y = o * lax.rsqrt(ms + RMS_EPS) * nw_ref[...]
    o_ref[...] = (y * _silu(hg_ref[...].astype(F32))).astype(o_ref.dtype)


def _hgrn(hq, hf, hi, hg, lb_logits, norm_w):
    B, S, W = hq.shape
    npairs = W // LANES
    nrows = lb_logits.shape[0]
    seq = pl.BlockSpec((None, S, LANES), lambda b, p: (b, 0, p))
    return pl.pallas_call(
        _hgrn_kernel,
        out_shape=jax.ShapeDtypeStruct((B, S, W), BF16),
        grid=(B, npairs),
        in_specs=[seq, seq, seq, seq,
                  pl.BlockSpec((nrows, LANES), lambda b, p: (0, p)),
                  pl.BlockSpec((1, LANES), lambda b, p: (0, p))],
        out_specs=seq,
        scratch_shapes=[pltpu.VMEM((S, LANES), F32),
                        pltpu.VMEM((S, LANES), BF16),
                        pltpu.VMEM((S, LANES), BF16),
                        pltpu.VMEM((S, LANES), F32),
                        pltpu.VMEM((S, LANES), F32),
                        pltpu.VMEM((S, HCHUNK * LANES), BF16),
                        pltpu.VMEM((S, LANES), BF16),
                        pltpu.VMEM((S, LANES), F32),
                        pltpu.VMEM((S // HCHUNK, LANES, LANES), BF16),
                        pltpu.VMEM((S // HCHUNK, LANES), F32)],
        compiler_params=_cparams(("parallel", "parallel")),
    )(hq, hf, hi, hg, lb_logits, norm_w.reshape(1, W))


def _layer_norm(v, g, b):
    mu = jnp.mean(v, axis=-1, keepdims=True)
    d = v - mu
    var = jnp.mean(d * d, axis=-1, keepdims=True)
    return d * lax.rsqrt(var + LN_EPS) * g + b


def _store_chunks(ref, val):
    for j in range(ref.shape[0]):
        ref[j] = val[:, j * LANES:(j + 1) * LANES]


def _load_chunks(ref):
    return jnp.concatenate([ref[j] for j in range(ref.shape[0])], axis=1)


def _mix_kernel(yf_ref, oh_ref, gf_ref, gh_ref, x_ref, g1_ref, sc2_ref, sh2_ref,
                wuf_ref, wuh_ref, wo_ref, lg_ref, lbias_ref, wr_ref, br_ref,
                x1_ref, h2_ref, ri_ref, rt_ref, cnt_ref, carry_sc, *, alpha, ngroups, nper):
    first = (pl.program_id(0) == 0) & (pl.program_id(1) == 0)

    @pl.when(first)
    def _():
        carry_sc[...] = jnp.zeros_like(carry_sc)

    tm = x_ref.shape[0]
    yf = jnp.dot(yf_ref[...], wuf_ref[...], preferred_element_type=F32)
    yh = jnp.dot(oh_ref[...], wuh_ref[...], preferred_element_type=F32)
    merged = _sigmoid(gf_ref[...].astype(F32)) * yf + _sigmoid(gh_ref[...].astype(F32)) * yh
    y = jnp.dot(merged.astype(BF16), wo_ref[...], preferred_element_type=F32)
    x1 = _layer_norm(alpha * x_ref[...] + g1_ref[...] * y, lg_ref[...], lbias_ref[...])
    x1_ref[...] = x1
    h2 = x1 * (1.0 + sc2_ref[...]) + sh2_ref[...]
    _store_chunks(h2_ref, h2)

    h_top = pltpu.bitcast(pltpu.bitcast(h2, jnp.uint32) & jnp.uint32(0xFFFF0000), F32)
    h_hi = h_top.astype(BF16)
    h_lo = (h2 - h_top).astype(BF16)
    logits = (jnp.dot(h_hi, wr_ref[0], preferred_element_type=F32)
              + jnp.dot(h_hi, wr_ref[1], preferred_element_type=F32)
              + jnp.dot(h_lo, wr_ref[0], preferred_element_type=F32)) + br_ref[...]
    lane = lax.broadcasted_iota(jnp.int32, (tm, LANES), 1)
    big = jnp.int32(1 << 20)

    def argmax_first(vals, mask):
        mx = jnp.max(jnp.where(mask, vals, -jnp.inf), axis=1, keepdims=True)
        idx = jnp.min(jnp.where(mask & (vals == mx), lane, big), axis=1, keepdims=True)
        return mx, idx

    gmask = lane < ngroups
    gmax = jnp.max(jnp.where(gmask, logits, -jnp.inf), axis=1, keepdims=True)
    gexp = jnp.where(gmask, jnp.exp(logits - gmax), 0.0)
    gprob = gexp / jnp.sum(gexp, axis=1, keepdims=True)
    g_w, g_idx = argmax_first(gprob, gmask)

    lo = ngroups + g_idx * nper
    emask = (lane >= lo) & (lane < lo + nper)
    emax = jnp.max(jnp.where(emask, logits, -jnp.inf), axis=1, keepdims=True)
    eexp = jnp.where(emask, jnp.exp(logits - emax), 0.0)
    eprob = eexp / jnp.sum(eexp, axis=1, keepdims=True)
    p0, i0 = argmax_first(eprob, emask)
    p1, i1 = argmax_first(eprob, emask & (lane != i0))
    den = p0 + p1
    w0 = p0 / den * g_w
    w1 = p1 / den * g_w
    e0 = i0 - ngroups
    e1 = i1 - ngroups

    oh = ((lane == e0) | (lane == e1)).astype(F32)
    r = lax.broadcasted_iota(jnp.int32, (tm, tm), 0)
    c = lax.broadcasted_iota(jnp.int32, (tm, tm), 1)
    strict_lower = (c < r).astype(BF16)
    before = jnp.dot(strict_lower, oh.astype(BF16), preferred_element_type=F32) + carry_sc[...]
    rank0 = jnp.sum(jnp.where(lane == e0, before, 0.0), axis=1, keepdims=True)
    rank1 = jnp.sum(jnp.where(lane == e1, before, 0.0), axis=1, keepdims=True)
    carry_sc[...] = carry_sc[...] + jnp.sum(oh, axis=0, keepdims=True)
    cnt_ref[...] = carry_sc[...]

    info = jnp.where(lane == 0, w0, 0.0)
    info = jnp.where(lane == 1, w1, info)
    info = jnp.where(lane == 2, e0.astype(F32), info)
    info = jnp.where(lane == 3, e1.astype(F32), info)
    info = jnp.where(lane == 4, rank0, info)
    info = jnp.where(lane == 5, rank1, info)
    ri_ref[...] = info
    rt_ref[...] = info.T[:ROW_TILE, :]


def _mix(yf, oh, gf, gh, x, g1, sc2, sh2, wuf, wuh, wo, ln_g, ln_b, wr, br, alpha, ngroups, nper, tm=512):
    B, S, D = x.shape
    W = yf.shape[2]
    tok = lambda w: pl.BlockSpec((None, tm, w), lambda b, i: (b, i, 0))
    vec = pl.BlockSpec((None, 1, D), lambda b, i: (b, 0, 0))
    full = lambda a: pl.BlockSpec(a.shape, lambda b, i: (0,) * a.ndim)
    return pl.pallas_call(
        functools.partial(_mix_kernel, alpha=alpha, ngroups=ngroups, nper=nper),
        out_shape=(jax.ShapeDtypeStruct((B, S, D), F32),
                   jax.ShapeDtypeStruct((D // LANES, B * S, LANES), F32),
                   jax.ShapeDtypeStruct((B, S, LANES), F32),
                   jax.ShapeDtypeStruct((ROW_TILE, B * S), F32),
                   jax.ShapeDtypeStruct((1, LANES), F32)),
        grid=(B, S // tm),
        in_specs=[tok(W), tok(W), tok(D), tok(D), tok(D), vec, vec, vec,
                  full(wuf), full(wuh), full(wo), full(ln_g), full(ln_b), full(wr), full(br)],
        out_specs=(tok(D),
                   pl.BlockSpec((D // LANES, tm, LANES), lambda b, i: (0, b * (S // tm) + i, 0)),
                   tok(LANES),
                   pl.BlockSpec((ROW_TILE, tm), lambda b, i: (0, b * (S // tm) + i)),
                   pl.BlockSpec((1, LANES), lambda b, i: (0, 0))),
        scratch_shapes=[pltpu.VMEM((1, LANES), F32)],
        compiler_params=_cparams(("arbitrary", "arbitrary")),
    )(yf, oh, gf, gh, x, g1, sc2, sh2, wuf, wuh, wo, ln_g, ln_b, wr, br)


def _sc_mesh():
    return plsc.VectorSubcoreMesh(core_axis_name="core", subcore_axis_name="subcore")


def _sc_pipeline(body, grid, in_specs, out_specs):
    return pltpu.emit_pipeline(body, grid=grid, in_specs=in_specs, out_specs=out_specs,
                               core_axis_name=("core", "subcore"),
                               dimension_semantics=(pltpu.PARALLEL,) * len(grid))


def _sc_scatter_rows(src, rows_a, rows_b, n_out):
    nj, t = rows_a.shape
    nc = t // LANES

    @pl.kernel(out_type=jax.ShapeDtypeStruct((n_out, LANES), src.dtype), mesh=_sc_mesh(), scratch_types=[])
    def scatter(x_hbm, a_hbm, b_hbm, o_hbm):
        def body(x_vmem, a_vmem, b_vmem):
            pltpu.sync_copy(x_vmem, o_hbm.at[a_vmem.at[0]])
            pltpu.sync_copy(x_vmem, o_hbm.at[b_vmem.at[0]])

        idx = pl.BlockSpec((1, LANES), lambda j, c: (j, c))
        _sc_pipeline(body, (nj, nc), [pl.BlockSpec((LANES, LANES), lambda j, c: (j * nc + c, 0)), idx, idx],
                     [])(x_hbm, a_hbm, b_hbm)

    return scatter(src, rows_a, rows_b)


def _sc_gather_rows(table, rows):
    nr, t = rows.shape
    nc = t // LANES

    @pl.kernel(out_type=jax.ShapeDtypeStruct((nr * t, LANES), table.dtype), mesh=_sc_mesh(), scratch_types=[])
    def gather(x_hbm, i_hbm, o_hbm):
        def body(i_vmem, o_vmem):
            pltpu.sync_copy(x_hbm.at[i_vmem.at[0]], o_vmem)

        _sc_pipeline(body, (nr, nc), [pl.BlockSpec((1, LANES), lambda r, c: (r, c))],
                     [pl.BlockSpec((LANES, LANES), lambda r, c: (r * nc + c, 0))])(i_hbm, o_hbm)

    return gather(table, rows)


def _experts_kernel(te_ref, tn_ref, x_ref, wg_ref, wu_ref, wd_ref, o_ref):
    nrows = tn_ref[pl.program_id(0)]

    @pl.when(nrows > 0)
    def _():
        x = _load_chunks(x_ref)
        x = jnp.where(lax.broadcasted_iota(jnp.int32, x.shape, 0) < nrows, x, 0.0).astype(BF16)
        g = jnp.dot(x, wg_ref[...].astype(BF16), preferred_element_type=F32)
        u = jnp.dot(x, wu_ref[...].astype(BF16), preferred_element_type=F32)
        hid = (_silu(g) * u).astype(BF16)
        _store_chunks(o_ref, jnp.dot(hid, wd_ref[...].astype(BF16), preferred_element_type=F32))

    @pl.when(nrows == 0)
    def _():
        o_ref[...] = jnp.zeros_like(o_ref)


def _experts(tile_expert, tile_rows, xs, wg, wu, wd, tm):
    E, D, FF = wg.shape
    dt = D // LANES
    ntiles = tile_expert.shape[0]
    rows = pl.BlockSpec((dt, tm, LANES), lambda i, te, tn: (0, i, 0))
    grid_spec = pltpu.PrefetchScalarGridSpec(
        num_scalar_prefetch=2,
        grid=(ntiles,),
        in_specs=[rows,
                  pl.BlockSpec((None, D, FF), lambda i, te, tn: (te[i], 0, 0)),
                  pl.BlockSpec((None, D, FF), lambda i, te, tn: (te[i], 0, 0)),
                  pl.BlockSpec((None, FF, D), lambda i, te, tn: (te[i], 0, 0))],
        out_specs=rows,
    )
    return pl.pallas_call(
        _experts_kernel,
        out_shape=jax.ShapeDtypeStruct((dt, ntiles * tm, LANES), F32),
        grid_spec=grid_spec,
        compiler_params=_cparams(("arbitrary",)),
    )(tile_expert, tile_rows, xs, wg, wu, wd)


def _combine_kernel(yg_ref, x1_ref, ri_ref, g2_ref, lg_ref, lb_ref, o_ref, *, alpha):
    ri = ri_ref[...]
    y = ri[:, 0:1] * _load_chunks(yg_ref.at[0]) + ri[:, 1:2] * _load_chunks(yg_ref.at[1])
    o_ref[...] = _layer_norm(alpha * x1_ref[...] + g2_ref[...] * y, lg_ref[...], lb_ref[...])


def _combine(yg, x1, rinfo, g2, ln_g, ln_b, alpha, tm=256):
    B, S, D = x1.shape
    nb = S // tm
    return pl.pallas_call(
        functools.partial(_combine_kernel, alpha=alpha),
        out_shape=jax.ShapeDtypeStruct((B, S, D), F32),
        grid=(B, nb),
        in_specs=[pl.BlockSpec((2, D // LANES, tm, LANES), lambda b, i: (0, 0, b * nb + i, 0)),
                  pl.BlockSpec((None, tm, D), lambda b, i: (b, i, 0)),
                  pl.BlockSpec((None, tm, LANES), lambda b, i: (b, i, 0)),
                  pl.BlockSpec((None, 1, D), lambda b, i: (b, 0, 0)),
                  pl.BlockSpec((1, D), lambda b, i: (0, 0)),
                  pl.BlockSpec((1, D), lambda b, i: (0, 0))],
        out_specs=pl.BlockSpec((None, tm, D), lambda b, i: (b, i, 0)),
        compiler_params=_cparams(("parallel", "parallel")),
    )(yg, x1, rinfo, g2, ln_g, ln_b)


def kernel(x, c, w_ada, b_ada, w_in, b_fox_forget, hgrn_lb_logits, hgrn_norm_w, w_up_fox, w_up_hgrn, w_out,
           ln1_g, ln1_b, w_router_group, b_router_group, w_router_expert, b_router_expert,
           w_expert_gate, w_expert_up, w_expert_down, ln2_g, ln2_b):
    B, S, D = x.shape
    depth = w_ada.shape[0]
    assert depth == 1, "single-layer block"
    fox_heads = b_fox_forget.shape[1]
    fox_w = fox_heads * HEAD_DIM
    hgrn_w = hgrn_norm_w.shape[1]
    ngroups = w_router_group.shape[2]
    nexp = w_router_expert.shape[2]
    nper = nexp // ngroups
    alpha = (2 * depth) ** 0.25
    T = B * S

    ada = _ada(c, w_ada[0], b_ada[0])
    sh1, sc1, g1, sh2, sc2, g2 = [a.reshape(B, 1, D) for a in jnp.split(ada, 6, axis=-1)]

    wi = w_in[0]
    o_ff = 3 * fox_w
    w_packed = jnp.concatenate(
        [wi[:, :o_ff + fox_heads], jnp.zeros((D, LANES - fox_heads), wi.dtype), wi[:, o_ff + fox_heads:]],
        axis=1).astype(BF16)
    widths = [fox_w, fox_w, fox_w, LANES, hgrn_w, hgrn_w, hgrn_w, hgrn_w, D, D]
    segs, off = [], 0
    for w in widths:
        segs.append((off, off + w))
        off += w
    fq, fk, fv, ffp, hq, hf, hi, hg, gf, gh = _inproj(x, sc1, sh1, w_packed, segs)

    bias_p = jnp.zeros((1, LANES), F32).at[0, :fox_heads].set(b_fox_forget[0])
    cum = _foxcum(ffp, bias_p)
    y_fox = _fox(fq, fk, fv, cum)

    o_h = _hgrn(hq, hf, hi, hg, hgrn_lb_logits, hgrn_norm_w[0])

    wr = jnp.zeros((D, LANES), F32).at[:, :ngroups].set(w_router_group[0]).at[:, ngroups:ngroups + nexp].set(
        w_router_expert[0])
    wr_hi = lax.bitcast_convert_type(lax.bitcast_convert_type(wr, jnp.uint32) & jnp.uint32(0xFFFF0000), F32)
    wr = jnp.stack([wr_hi.astype(BF16), (wr - wr_hi).astype(BF16)])
    br = jnp.zeros((1, LANES), F32).at[0, :ngroups].set(b_router_group[0]).at[0, ngroups:ngroups + nexp].set(
        b_router_expert[0])
    x1, h2, rinfo, fields, counts = _mix(
        y_fox, o_h, gf, gh, x, g1, sc2, sh2,
        w_up_fox[0].astype(BF16), w_up_hgrn[0].astype(BF16), w_out[0].astype(BF16),
        ln1_g[0].reshape(1, D), ln1_b[0].reshape(1, D), wr, br, alpha, ngroups, nper)

    tm_e = 256
    dt = D // LANES
    ntiles = (2 * T) // tm_e + nexp
    nslots = ntiles * tm_e
    cnt = counts[0, :nexp].astype(jnp.int32)
    padded = ((cnt + tm_e - 1) // tm_e) * tm_e
    ends = jnp.cumsum(padded)
    starts = ends - padded
    eid = fields[2:4].astype(jnp.int32)
    rank = fields[4:6].astype(jnp.int32)
    first = jnp.sum(jnp.where(eid[None] == jnp.arange(nexp, dtype=jnp.int32)[:, None, None],
                              starts[:, None, None], 0), axis=0)
    pos = first + rank
    tile_start = jnp.arange(ntiles, dtype=jnp.int32) * tm_e
    tile_expert = jnp.minimum(jnp.sum((tile_start[:, None] >= ends[None, :]).astype(jnp.int32), axis=1), nexp - 1)
    tile_rows = jnp.clip(starts[tile_expert] + cnt[tile_expert] - tile_start, 0, tm_e)
    rows = pos[:, None, :] + (jnp.arange(dt, dtype=jnp.int32) * nslots)[None, :, None]

    xs = _sc_scatter_rows(h2.reshape(dt * T, LANES), rows[0], rows[1], dt * nslots)
    ys = _experts(tile_expert, tile_rows, xs.reshape(dt, nslots, LANES),
                  w_expert_gate[0], w_expert_up[0], w_expert_down[0], tm_e)
    yg = _sc_gather_rows(ys.reshape(dt * nslots, LANES), rows.reshape(2 * dt, T))
    return _combine(yg.reshape(2, dt, T, LANES), x1, rinfo, g2,
                    ln2_g[0].reshape(1, D), ln2_b[0].reshape(1, D), alpha)
```

```python
import functools

import jax
import jax.numpy as jnp
from jax import lax
from jax.experimental import pallas as pl
from jax.experimental.pallas import tpu as pltpu
from jax.experimental.pallas import tpu_sc as plsc

F32 = jnp.float32
BF16 = jnp.bfloat16
HIGHEST = lax.Precision.HIGHEST

LANES = 128
HEAD_DIM = 64
LN_EPS = 1e-5
RMS_EPS = 1e-6
LOG2E = 1.4426950408889634
NEG_BIG = -1e30
HCHUNK = 16
ROW_TILE = 8
WORD_LANES = 2 * LANES
VMEM_LIMIT = 56 * 1024 * 1024


def _cparams(sem, vmem=VMEM_LIMIT):
    return pltpu.CompilerParams(dimension_semantics=sem, vmem_limit_bytes=vmem)


def _sigmoid(x):
    return 1.0 / (1.0 + jnp.exp(-x))


def _silu(x):
    return x * _sigmoid(x)


def _ada_kernel(c_ref, w_ref, b_ref, o_ref):
    c = c_ref[...]
    o_ref[...] = jnp.dot(_silu(c), w_ref[...], precision=HIGHEST,
                         preferred_element_type=F32) + b_ref[...]


def _ada(c, w_ada, b_ada):
    B, D = c.shape
    N = w_ada.shape[1]
    tn = 1024
    return pl.pallas_call(
        _ada_kernel,
        out_shape=jax.ShapeDtypeStruct((B, N), F32),
        grid=(N // tn,),
        in_specs=[pl.BlockSpec((B, D), lambda j: (0, 0)),
                  pl.BlockSpec((D, tn), lambda j: (0, j)),
                  pl.BlockSpec((1, tn), lambda j: (0, j))],
        out_specs=pl.BlockSpec((B, tn), lambda j: (0, j)),
        compiler_params=_cparams(("arbitrary",)),
    )(c, w_ada, b_ada.reshape(1, N))


def _inproj_kernel(x_ref, sc_ref, sh_ref, w_ref,
                   fq_ref, fk_ref, fv_ref, ff_ref, hq_ref, hf_ref, hi_ref, hg_ref, gf_ref, gh_ref,
                   *, segs, q_scale):
    h = (x_ref[...] * (1.0 + sc_ref[...]) + sh_ref[...]).astype(BF16)
    outs = (fq_ref, fk_ref, fv_ref, ff_ref, hq_ref, hf_ref, hi_ref, hg_ref, gf_ref, gh_ref)
    for idx, (o_ref, (a, b)) in enumerate(zip(outs, segs)):
        r = jnp.dot(h, w_ref[:, a:b], preferred_element_type=F32)
        if idx == 0:
            r = r * q_scale
        o_ref[...] = r.astype(o_ref.dtype)


def _inproj(x, sc1, sh1, w_packed, segs, tm=256):
    B, S, D = x.shape
    widths = [b - a for a, b in segs]
    dtypes = [BF16, BF16, BF16, F32, BF16, F32, BF16, BF16, BF16, BF16]
    out_shape = tuple(jax.ShapeDtypeStruct((B, S, w), dt) for w, dt in zip(widths, dtypes))
    out_specs = tuple(pl.BlockSpec((None, tm, w), lambda b, i: (b, i, 0)) for w in widths)
    vec = pl.BlockSpec((None, 1, D), lambda b, i: (b, 0, 0))
    return pl.pallas_call(
        functools.partial(_inproj_kernel, segs=tuple(segs), q_scale=HEAD_DIM ** -0.5 * LOG2E),
        out_shape=out_shape,
        grid=(B, S // tm),
        in_specs=[pl.BlockSpec((None, tm, D), lambda b, i: (b, i, 0)), vec, vec,
                  pl.BlockSpec(w_packed.shape, lambda b, i: (0, 0))],
        out_specs=out_specs,
        compiler_params=_cparams(("parallel", "parallel")),
    )(x, sc1, sh1, w_packed)


def _foxcum_kernel(ff_ref, b_ref, o_ref, *, blk):
    S = ff_ref.shape[0]
    r = lax.broadcasted_iota(jnp.int32, (blk, blk), 0)
    c = lax.broadcasted_iota(jnp.int32, (blk, blk), 1)
    lower = (r >= c).astype(F32)
    carry = jnp.zeros((1, LANES), F32)
    for j in range(S // blk):
        z = ff_ref[j * blk:(j + 1) * blk, :] + b_ref[...]
        lf = jnp.minimum(z, 0.0) - jnp.log(1.0 + jnp.exp(-jnp.abs(z)))
        cum = jnp.dot(lower, lf, precision=HIGHEST, preferred_element_type=F32) + carry
        o_ref[j * blk:(j + 1) * blk, :] = cum * LOG2E
        carry = cum[blk - 1:blk, :]


def _foxcum(ffp, bias_p, blk=256):
    B, S, _ = ffp.shape
    return pl.pallas_call(
        functools.partial(_foxcum_kernel, blk=blk),
        out_shape=jax.ShapeDtypeStruct((B, S, LANES), F32),
        grid=(B,),
        in_specs=[pl.BlockSpec((None, S, LANES), lambda b: (b, 0, 0)),
                  pl.BlockSpec((1, LANES), lambda b: (0, 0))],
        out_specs=pl.BlockSpec((None, S, LANES), lambda b: (b, 0, 0)),
        compiler_params=_cparams(("parallel",)),
    )(ffp, bias_p)


NCUM = 3


def _fox_kernel(q_ref, k_ref, v_ref, c_ref, o_ref, ka_sc, kb_sc, va_sc, vb_sc, *, tq, tk):
    p = pl.program_id(1)
    qi = pl.program_id(2)
    S = k_ref.shape[0]

    @pl.when(qi == 0)
    def _():
        lane = lax.broadcasted_iota(jnp.int32, (S, LANES), 1)
        rr = lax.broadcasted_iota(jnp.int32, (LANES, LANES), 0)
        cc = lax.broadcasted_iota(jnp.int32, (LANES, LANES), 1)
        rest = c_ref[...]
        placed = jnp.zeros((S, LANES), F32)
        for i in range(NCUM):
            piece = rest.astype(BF16)
            rest = rest - piece.astype(F32)
            sel = ((rr == 2 * p) & (cc == HEAD_DIM + i)) | ((rr == 2 * p + 1) & (cc == i))
            placed = placed + jnp.dot(piece, jnp.where(sel, 1.0, 0.0).astype(BF16), preferred_element_type=F32)
        k2 = k_ref[...].astype(F32)
        ka_sc[...] = jnp.where(lane < HEAD_DIM, k2, -placed).astype(BF16)
        kb_sc[...] = jnp.where(lane >= HEAD_DIM, k2, -placed).astype(BF16)
        vt = v_ref[...].astype(F32).T
        row = lax.broadcasted_iota(jnp.int32, (LANES, S), 0)
        va_sc[...] = jnp.where(row < HEAD_DIM, vt, jnp.where(row == HEAD_DIM, 1.0, 0.0)).astype(BF16)
        vb_sc[...] = jnp.where(row >= HEAD_DIM, vt, jnp.where(row == 0, 1.0, 0.0)).astype(BF16)

    q2 = q_ref[...].astype(F32)
    qlane = lax.broadcasted_iota(jnp.int32, (tq, LANES), 1)
    qa = jnp.where(qlane < HEAD_DIM, q2, jnp.where(qlane < HEAD_DIM + NCUM, 1.0, 0.0)).astype(BF16)
    qb = jnp.where(qlane >= HEAD_DIM, q2, jnp.where(qlane < NCUM, 1.0, 0.0)).astype(BF16)
    nsub = tq // tk

    def block(k0, carry, diag_off):
        q0 = 0 if diag_off is None else diag_off
        out = []
        for ksc, vsc, qh, (m, acc) in ((ka_sc, va_sc, qa, carry[:2]), (kb_sc, vb_sc, qb, carry[2:])):
            st = lax.dot_general(ksc[pl.ds(k0, tk), :], qh[q0:, :], (((1,), (1,)), ((), ())),
                                 preferred_element_type=F32)
            if diag_off is not None:
                st = jnp.where(lax.broadcasted_iota(jnp.int32, st.shape, 0)
                               <= lax.broadcasted_iota(jnp.int32, st.shape, 1), st, NEG_BIG)
            m_old = m[:, q0:]
            m_new = jnp.maximum(m_old, jnp.max(st, axis=0, keepdims=True))
            pt = jnp.exp2(st - m_new).astype(BF16)
            acc_new = (jnp.exp2(m_old - m_new) * acc[:, q0:]
                       + jnp.dot(vsc[:, pl.ds(k0, tk)], pt, preferred_element_type=F32))
            if q0:
                m_new = jnp.concatenate([m[:, :q0], m_new], axis=1)
                acc_new = jnp.concatenate([acc[:, :q0], acc_new], axis=1)
            out += [m_new, acc_new]
        return tuple(out)

    def group(j, carry):
        k0 = pl.multiple_of(j * (nsub * tk), nsub * tk)
        for u in range(nsub):
            carry = block(k0 + u * tk, carry, None)
        return carry

    m0 = jnp.full((1, tq), NEG_BIG, F32)
    a0 = jnp.zeros((LANES, tq), F32)
    carry = lax.fori_loop(0, qi, group, (m0, a0, m0, a0))
    for d in range(nsub):
        carry = block(pl.multiple_of(qi * tq + d * tk, tk), carry, d * tk)
    _, aa, _, ab = carry
    row = lax.broadcasted_iota(jnp.int32, (LANES, tq), 0)
    ot = jnp.where(row < HEAD_DIM, aa * (1.0 / aa[HEAD_DIM:HEAD_DIM + 1, :]), ab * (1.0 / ab[0:1, :]))
    o_ref[...] = ot.T.astype(o_ref.dtype)


def _fox(fq, fk, fv, cum, tq=1024, tk=256):
    B, S, W = fq.shape
    assert tq % (2 * tk) == 0 and S % tq == 0
    npairs = W // LANES
    return pl.pallas_call(
        functools.partial(_fox_kernel, tq=tq, tk=tk),
        out_shape=jax.ShapeDtypeStruct((B, S, W), BF16),
        grid=(B, npairs, S // tq),
        in_specs=[pl.BlockSpec((None, tq, LANES), lambda b, p, i: (b, i, p)),
                  pl.BlockSpec((None, S, LANES), lambda b, p, i: (b, 0, p)),
                  pl.BlockSpec((None, S, LANES), lambda b, p, i: (b, 0, p)),
                  pl.BlockSpec((None, S, LANES), lambda b, p, i: (b, 0, 0))],
        out_specs=pl.BlockSpec((None, tq, LANES), lambda b, p, i: (b, i, p)),
        scratch_shapes=[pltpu.VMEM((S, LANES), BF16), pltpu.VMEM((S, LANES), BF16),
                        pltpu.VMEM((LANES, S), BF16), pltpu.VMEM((LANES, S), BF16)],
        compiler_params=_cparams(("parallel", "parallel", "arbitrary")),
    )(fq, fk, fv, cum)


def _hgrn_kernel(hq_ref, hf_ref, hi_ref, hg_ref, lb_ref, nw_ref, o_ref,
                 a_sc, qt_sc, kt_sc, kk_sc, qq_sc, p_sc, s_sc, o_sc, stb_sc, dec_sc):
    S = hq_ref.shape[0]
    C = HCHUNK
    nchunks = S // C

    lg = lb_ref[...]
    e = jnp.exp(lg - jnp.max(lg, axis=0, keepdims=True))
    lb = e[0:1, :] / jnp.sum(e, axis=0, keepdims=True)

    f = lb + (1.0 - lb) * _sigmoid(hf_ref[...])
    lf = jnp.log(f)
    kk = 1.0 - f
    qq = _silu(hq_ref[...].astype(F32))

    rmod = lax.broadcasted_iota(jnp.int32, (S, LANES), 0) & (C - 1)
    a = lf
    d = 1
    while d < C:
        a = a + jnp.where(rmod >= d, pltpu.roll(a, d, axis=0), 0.0)
        d *= 2
    a3 = a.reshape(nchunks, C, LANES)
    alast = jnp.broadcast_to(a3[:, C - 1:C, :], (nchunks, C, LANES)).reshape(S, LANES)
    a_sc[...] = a * LOG2E
    kk_sc[...] = kk
    qq_sc[...] = qq
    qt_sc[...] = (qq * jnp.exp(a)).astype(BF16)
    kt_sc[...] = (kk * jnp.exp(alast - a)).astype(BF16)
    dec_sc[...] = jnp.exp(a3[:, C - 1, :])

    lane = lax.broadcasted_iota(jnp.int32, (C, LANES), 1)
    trow = lax.broadcasted_iota(jnp.int32, (C, LANES), 0)

    def gen(c, _):
        r0 = pl.multiple_of(c * C, C)
        ac = a_sc[pl.ds(r0, C), :]
        qc = qq_sc[pl.ds(r0, C), :]
        kc = kk_sc[pl.ds(r0, C), :]
        half = C // 2
        for s in range(C):
            if s < half:
                dec = jnp.exp2(jnp.where(trow >= s, ac - ac[s:s + 1, :], NEG_BIG))
                p = qc * (kc[s:s + 1, :] * dec)
            else:
                dec = jnp.exp2(jnp.where(trow[half:] >= s, ac[half:] - ac[s:s + 1, :], NEG_BIG))
                p = jnp.concatenate([jnp.zeros((half, LANES), F32), qc[half:] * (kc[s:s + 1, :] * dec)], axis=0)
            p_sc[pl.ds(r0, C), s * LANES:(s + 1) * LANES] = p.astype(BF16)
        return 0

    lax.fori_loop(0, nchunks, gen, 0)

    er = lax.broadcasted_iota(jnp.int32, (C * LANES, LANES), 0)
    ec = lax.broadcasted_iota(jnp.int32, (C * LANES, LANES), 1)
    emat = (ec == ((er & (LANES - 1)) // HEAD_DIM) * C + er // LANES).astype(BF16)
    rb = 256

    def red(i, _):
        r0 = pl.multiple_of(i * rb, rb)
        s_sc[pl.ds(r0, rb), :] = jnp.dot(p_sc[pl.ds(r0, rb), :], emat,
                                         preferred_element_type=F32).astype(BF16)
        return 0

    lax.fori_loop(0, S // rb, red, 0)

    sr = lax.broadcasted_iota(jnp.int32, (LANES, LANES), 0)
    scn = lax.broadcasted_iota(jnp.int32, (LANES, LANES), 1)
    same_head = (sr // HEAD_DIM) == (scn // HEAD_DIM)
    unroll = 16

    def scan(g, st):
        for u in range(unroll):
            c = g * unroll + u
            r0 = pl.multiple_of(c * C, C)
            stb_sc[c] = st.astype(BF16)
            upd = lax.dot_general(hi_ref[pl.ds(r0, C), :], kt_sc[pl.ds(r0, C), :], (((0,), (0,)), ((), ())),
                                  preferred_element_type=F32)
            st = st * dec_sc[pl.ds(c, 1), :] + jnp.where(same_head, upd, 0.0)
        return st

    lax.fori_loop(0, nchunks // unroll, scan, jnp.zeros((LANES, LANES), F32))

    def readout(g, _):
        for u in range(unroll):
            c = g * unroll + u
            r0 = pl.multiple_of(c * C, C)
            vc = hi_ref[pl.ds(r0, C), :]
            o_inter = lax.dot_general(qt_sc[pl.ds(r0, C), :], stb_sc[c],
                                      (((1,), (1,)), ((), ())), preferred_element_type=F32)
            v2 = jnp.concatenate([jnp.where(lane < HEAD_DIM, vc, jnp.zeros_like(vc)),
                                  jnp.where(lane >= HEAD_DIM, vc, jnp.zeros_like(vc))], axis=0)
            o_intra = jnp.dot(s_sc[pl.ds(r0, C), :][:, :2 * C], v2, preferred_element_type=F32)
            o_sc[pl.ds(r0, C), :] = o_inter + o_intra
        return 0

    lax.fori_loop(0, nchunks // unroll, readout, 0)

    o = o_sc[...]
    ones_head = jnp.where(same_head, 1.0 / HEAD_DIM, 0.0).astype(F32)
    ms = jnp.dot(o * o, ones_head, precision=HIGHEST, preferred_element_type=F32)
    y = o * lax.rsqrt(ms + RMS_EPS) * nw_ref[...]
    o_ref[...] = (y * _silu(hg_ref[...].astype(F32))).astype(o_ref.dtype)


def _hgrn(hq, hf, hi, hg, lb_logits, norm_w):
    B, S, W = hq.shape
    npairs = W // LANES
    nrows = lb_logits.shape[0]
    seq = pl.BlockSpec((None, S, LANES), lambda b, p: (b, 0, p))
    return pl.pallas_call(
        _hgrn_kernel,
        out_shape=jax.ShapeDtypeStruct((B, S, W), BF16),
        grid=(B, npairs),
        in_specs=[seq, seq, seq, seq,
                  pl.BlockSpec((nrows, LANES), lambda b, p: (0, p)),
                  pl.BlockSpec((1, LANES), lambda b, p: (0, p))],
        out_specs=seq,
        scratch_shapes=[pltpu.VMEM((S, LANES), F32),
                        pltpu.VMEM((S, LANES), BF16),
                        pltpu.VMEM((S, LANES), BF16),
                        pltpu.VMEM((S, LANES), F32),
                        pltpu.VMEM((S, LANES), F32),
                        pltpu.VMEM((S, HCHUNK * LANES), BF16),
                        pltpu.VMEM((S, LANES), BF16),
                        pltpu.VMEM((S, LANES), F32),
                        pltpu.VMEM((S // HCHUNK, LANES, LANES), BF16),
                        pltpu.VMEM((S // HCHUNK, LANES), F32)],
        compiler_params=_cparams(("parallel", "parallel")),
    )(hq, hf, hi, hg, lb_logits, norm_w.reshape(1, W))


def _layer_norm(v, g, b):
    mu = jnp.mean(v, axis=-1, keepdims=True)
    d = v - mu
    var = jnp.mean(d * d, axis=-1, keepdims=True)
    return d * lax.rsqrt(var + LN_EPS) * g + b


def _bf16_bits(x):
    u = pltpu.bitcast(x, jnp.uint32)
    return (u + jnp.uint32(0x7FFF) + ((u >> 16) & jnp.uint32(1))) & jnp.uint32(0xFFFF0000)


def _store_chunks(ref, val):
    n = ref.shape[0]
    for j in range(n):
        lo = _bf16_bits(val[:, j * LANES:(j + 1) * LANES]) >> 16
        hi = _bf16_bits(val[:, (j + n) * LANES:(j + n + 1) * LANES])
        ref[j] = pltpu.bitcast(lo | hi, F32)


def _load_chunks(ref):
    words = [pltpu.bitcast(ref[j], jnp.uint32) for j in range(ref.shape[0])]
    lo = [pltpu.bitcast(w << 16, F32) for w in words]
    hi = [pltpu.bitcast(w & jnp.uint32(0xFFFF0000), F32) for w in words]
    return jnp.concatenate(lo + hi, axis=1)


def _mix_kernel(yf_ref, oh_ref, gf_ref, gh_ref, x_ref, g1_ref, sc2_ref, sh2_ref,
                wuf_ref, wuh_ref, wo_ref, lg_ref, lbias_ref, wr_ref, br_ref,
                x1_ref, h2_ref, ri_ref, rt_ref, cnt_ref, carry_sc, *, alpha, ngroups, nper):
    first = (pl.program_id(0) == 0) & (pl.program_id(1) == 0)

    @pl.when(first)
    def _():
        carry_sc[...] = jnp.zeros_like(carry_sc)

    tm = x_ref.shape[0]
    yf = jnp.dot(yf_ref[...], wuf_ref[...], preferred_element_type=F32)
    yh = jnp.dot(oh_ref[...], wuh_ref[...], preferred_element_type=F32)
    merged = _sigmoid(gf_ref[...].astype(F32)) * yf + _sigmoid(gh_ref[...].astype(F32)) * yh
    y = jnp.dot(merged.astype(BF16), wo_ref[...], preferred_element_type=F32)
    x1 = _layer_norm(alpha * x_ref[...] + g1_ref[...] * y, lg_ref[...], lbias_ref[...])
    x1_ref[...] = x1
    h2 = x1 * (1.0 + sc2_ref[...]) + sh2_ref[...]
    _store_chunks(h2_ref, h2)

    h_top = pltpu.bitcast(pltpu.bitcast(h2, jnp.uint32) & jnp.uint32(0xFFFF0000), F32)
    h_hi = h_top.astype(BF16)
    h_lo = (h2 - h_top).astype(BF16)
    logits = (jnp.dot(h_hi, wr_ref[0], preferred_element_type=F32)
              + jnp.dot(h_hi, wr_ref[1], preferred_element_type=F32)
              + jnp.dot(h_lo, wr_ref[0], preferred_element_type=F32)) + br_ref[...]
    lane = lax.broadcasted_iota(jnp.int32, (tm, LANES), 1)
    big = jnp.int32(1 << 20)

    def argmax_first(vals, mask):
        mx = jnp.max(jnp.where(mask, vals, -jnp.inf), axis=1, keepdims=True)
        idx = jnp.min(jnp.where(mask & (vals == mx), lane, big), axis=1, keepdims=True)
        return mx, idx

    gmask = lane < ngroups
    gmax = jnp.max(jnp.where(gmask, logits, -jnp.inf), axis=1, keepdims=True)
    gexp = jnp.where(gmask, jnp.exp(logits - gmax), 0.0)
    gprob = gexp / jnp.sum(gexp, axis=1, keepdims=True)
    g_w, g_idx = argmax_first(gprob, gmask)

    lo = ngroups + g_idx * nper
    emask = (lane >= lo) & (lane < lo + nper)
    emax = jnp.max(jnp.where(emask, logits, -jnp.inf), axis=1, keepdims=True)
    eexp = jnp.where(emask, jnp.exp(logits - emax), 0.0)
    eprob = eexp / jnp.sum(eexp, axis=1, keepdims=True)
    p0, i0 = argmax_first(eprob, emask)
    p1, i1 = argmax_first(eprob, emask & (lane != i0))
    den = p0 + p1
    w0 = p0 / den * g_w
    w1 = p1 / den * g_w
    e0 = i0 - ngroups
    e1 = i1 - ngroups

    oh = ((lane == e0) | (lane == e1)).astype(F32)
    r = lax.broadcasted_iota(jnp.int32, (tm, tm), 0)
    c = lax.broadcasted_iota(jnp.int32, (tm, tm), 1)
    strict_lower = (c < r).astype(BF16)
    before = jnp.dot(strict_lower, oh.astype(BF16), preferred_element_type=F32) + carry_sc[...]
    rank0 = jnp.sum(jnp.where(lane == e0, before, 0.0), axis=1, keepdims=True)
    rank1 = jnp.sum(jnp.where(lane == e1, before, 0.0), axis=1, keepdims=True)
    carry_sc[...] = carry_sc[...] + jnp.sum(oh, axis=0, keepdims=True)
    cnt_ref[...] = carry_sc[...]

    info = jnp.where(lane == 0, w0, 0.0)
    info = jnp.where(lane == 1, w1, info)
    info = jnp.where(lane == 2, e0.astype(F32), info)
    info = jnp.where(lane == 3, e1.astype(F32), info)
    info = jnp.where(lane == 4, rank0, info)
    info = jnp.where(lane == 5, rank1, info)
    ri_ref[...] = info
    rt_ref[...] = info.T[:ROW_TILE, :]


def _mix(yf, oh, gf, gh, x, g1, sc2, sh2, wuf, wuh, wo, ln_g, ln_b, wr, br, alpha, ngroups, nper, tm=512):
    B, S, D = x.shape
    W = yf.shape[2]
    tok = lambda w: pl.BlockSpec((None, tm, w), lambda b, i: (b, i, 0))
    vec = pl.BlockSpec((None, 1, D), lambda b, i: (b, 0, 0))
    full = lambda a: pl.BlockSpec(a.shape, lambda b, i: (0,) * a.ndim)
    return pl.pallas_call(
        functools.partial(_mix_kernel, alpha=alpha, ngroups=ngroups, nper=nper),
        out_shape=(jax.ShapeDtypeStruct((B, S, D), F32),
                   jax.ShapeDtypeStruct((D // WORD_LANES, B * S, LANES), F32),
                   jax.ShapeDtypeStruct((B, S, LANES), F32),
                   jax.ShapeDtypeStruct((ROW_TILE, B * S), F32),
                   jax.ShapeDtypeStruct((1, LANES), F32)),
        grid=(B, S // tm),
        in_specs=[tok(W), tok(W), tok(D), tok(D), tok(D), vec, vec, vec,
                  full(wuf), full(wuh), full(wo), full(ln_g), full(ln_b), full(wr), full(br)],
        out_specs=(tok(D),
                   pl.BlockSpec((D // WORD_LANES, tm, LANES), lambda b, i: (0, b * (S // tm) + i, 0)),
                   tok(LANES),
                   pl.BlockSpec((ROW_TILE, tm), lambda b, i: (0, b * (S // tm) + i)),
                   pl.BlockSpec((1, LANES), lambda b, i: (0, 0))),
        scratch_shapes=[pltpu.VMEM((1, LANES), F32)],
        compiler_params=_cparams(("arbitrary", "arbitrary")),
    )(yf, oh, gf, gh, x, g1, sc2, sh2, wuf, wuh, wo, ln_g, ln_b, wr, br)


def _sc_mesh():
    return plsc.VectorSubcoreMesh(core_axis_name="core", subcore_axis_name="subcore")


def _sc_pipeline(body, grid, in_specs, out_specs):
    return pltpu.emit_pipeline(body, grid=grid, in_specs=in_specs, out_specs=out_specs,
                               core_axis_name=("core", "subcore"),
                               dimension_semantics=(pltpu.PARALLEL,) * len(grid))


def _sc_scatter_rows(src, rows_a, rows_b, n_out):
    nj, t = rows_a.shape
    nc = t // LANES

    @pl.kernel(out_type=jax.ShapeDtypeStruct((n_out, LANES), src.dtype), mesh=_sc_mesh(), scratch_types=[])
    def scatter(x_hbm, a_hbm, b_hbm, o_hbm):
        def body(x_vmem, a_vmem, b_vmem):
            pltpu.sync_copy(x_vmem, o_hbm.at[a_vmem.at[0]])
            pltpu.sync_copy(x_vmem, o_hbm.at[b_vmem.at[0]])

        idx = pl.BlockSpec((1, LANES), lambda j, c: (j, c))
        _sc_pipeline(body, (nj, nc), [pl.BlockSpec((LANES, LANES), lambda j, c: (j * nc + c, 0)), idx, idx],
                     [])(x_hbm, a_hbm, b_hbm)

    return scatter(src, rows_a, rows_b)


def _sc_gather_rows(table, rows):
    nr, t = rows.shape
    nc = t // LANES

    @pl.kernel(out_type=jax.ShapeDtypeStruct((nr * t, LANES), table.dtype), mesh=_sc_mesh(), scratch_types=[])
    def gather(x_hbm, i_hbm, o_hbm):
        def body(i_vmem, o_vmem):
            pltpu.sync_copy(x_hbm.at[i_vmem.at[0]], o_vmem)

        _sc_pipeline(body, (nr, nc), [pl.BlockSpec((1, LANES), lambda r, c: (r, c))],
                     [pl.BlockSpec((LANES, LANES), lambda r, c: (r * nc + c, 0))])(i_hbm, o_hbm)

    return gather(table, rows)


def _experts_kernel(te_ref, tn_ref, x_ref, wg_ref, wu_ref, wd_ref, o_ref):
    nrows = tn_ref[pl.program_id(0)]

    @pl.when(nrows > 0)
    def _():
        x = _load_chunks(x_ref)
        x = jnp.where(lax.broadcasted_iota(jnp.int32, x.shape, 0) < nrows, x, 0.0).astype(BF16)
        g = jnp.dot(x, wg_ref[...].astype(BF16), preferred_element_type=F32)
        u = jnp.dot(x, wu_ref[...].astype(BF16), preferred_element_type=F32)
        hid = (_silu(g) * u).astype(BF16)
        _store_chunks(o_ref, jnp.dot(hid, wd_ref[...].astype(BF16), preferred_element_type=F32))

    @pl.when(nrows == 0)
    def _():
        o_ref[...] = jnp.zeros_like(o_ref)


def _experts(tile_expert, tile_rows, xs, wg, wu, wd, tm):
    E, D, FF = wg.shape
    dt = D // WORD_LANES
    ntiles = tile_expert.shape[0]
    rows = pl.BlockSpec((dt, tm, LANES), lambda i, te, tn: (0, i, 0))
    grid_spec = pltpu.PrefetchScalarGridSpec(
        num_scalar_prefetch=2,
        grid=(ntiles,),
        in_specs=[rows,
                  pl.BlockSpec((None, D, FF), lambda i, te, tn: (te[i], 0, 0)),
                  pl.BlockSpec((None, D, FF), lambda i, te, tn: (te[i], 0, 0)),
                  pl.BlockSpec((None, FF, D), lambda i, te, tn: (te[i], 0, 0))],
        out_specs=rows,
    )
    return pl.pallas_call(
        _experts_kernel,
        out_shape=jax.ShapeDtypeStruct((dt, ntiles * tm, LANES), F32),
        grid_spec=grid_spec,
        compiler_params=_cparams(("arbitrary",)),
    )(tile_expert, tile_rows, xs, wg, wu, wd)


def _combine_kernel(yg_ref, x1_ref, ri_ref, g2_ref, lg_ref, lb_ref, o_ref, *, alpha):
    ri = ri_ref[...]
    y = ri[:, 0:1] * _load_chunks(yg_ref.at[0]) + ri[:, 1:2] * _load_chunks(yg_ref.at[1])
    o_ref[...] = _layer_norm(alpha * x1_ref[...] + g2_ref[...] * y, lg_ref[...], lb_ref[...])


def _combine(yg, x1, rinfo, g2, ln_g, ln_b, alpha, tm=256):
    B, S, D = x1.shape
    nb = S // tm
    return pl.pallas_call(
        functools.partial(_combine_kernel, alpha=alpha),
        out_shape=jax.ShapeDtypeStruct((B, S, D), F32),
        grid=(B, nb),
        in_specs=[pl.BlockSpec((2, D // WORD_LANES, tm, LANES), lambda b, i: (0, 0, b * nb + i, 0)),
                  pl.BlockSpec((None, tm, D), lambda b, i: (b, i, 0)),
                  pl.BlockSpec((None, tm, LANES), lambda b, i: (b, i, 0)),
                  pl.BlockSpec((None, 1, D), lambda b, i: (b, 0, 0)),
                  pl.BlockSpec((1, D), lambda b, i: (0, 0)),
                  pl.BlockSpec((1, D), lambda b, i: (0, 0))],
        out_specs=pl.BlockSpec((None, tm, D), lambda b, i: (b, i, 0)),
        compiler_params=_cparams(("parallel", "parallel")),
    )(yg, x1, rinfo, g2, ln_g, ln_b)


def kernel(x, c, w_ada, b_ada, w_in, b_fox_forget, hgrn_lb_logits, hgrn_norm_w, w_up_fox, w_up_hgrn, w_out,
           ln1_g, ln1_b, w_router_group, b_router_group, w_router_expert, b_router_expert,
           w_expert_gate, w_expert_up, w_expert_down, ln2_g, ln2_b):
    B, S, D = x.shape
    depth = w_ada.shape[0]
    assert depth == 1, "single-layer block"
    fox_heads = b_fox_forget.shape[1]
    fox_w = fox_heads * HEAD_DIM
    hgrn_w = hgrn_norm_w.shape[1]
    ngroups = w_router_group.shape[2]
    nexp = w_router_expert.shape[2]
    nper = nexp // ngroups
    alpha = (2 * depth) ** 0.25
    T = B * S

    ada = _ada(c, w_ada[0], b_ada[0])
    sh1, sc1, g1, sh2, sc2, g2 = [a.reshape(B, 1, D) for a in jnp.split(ada, 6, axis=-1)]

    wi = w_in[0]
    o_ff = 3 * fox_w
    w_packed = jnp.concatenate(
        [wi[:, :o_ff + fox_heads], jnp.zeros((D, LANES - fox_heads), wi.dtype), wi[:, o_ff + fox_heads:]],
        axis=1).astype(BF16)
    widths = [fox_w, fox_w, fox_w, LANES, hgrn_w, hgrn_w, hgrn_w, hgrn_w, D, D]
    segs, off = [], 0
    for w in widths:
        segs.append((off, off + w))
        off += w
    fq, fk, fv, ffp, hq, hf, hi, hg, gf, gh = _inproj(x, sc1, sh1, w_packed, segs)

    bias_p = jnp.zeros((1, LANES), F32).at[0, :fox_heads].set(b_fox_forget[0])
    cum = _foxcum(ffp, bias_p)
    y_fox = _fox(fq, fk, fv, cum)

    o_h = _hgrn(hq, hf, hi, hg, hgrn_lb_logits, hgrn_norm_w[0])

    wr = jnp.zeros((D, LANES), F32).at[:, :ngroups].set(w_router_group[0]).at[:, ngroups:ngroups + nexp].set(
        w_router_expert[0])
    wr_hi = lax.bitcast_convert_type(lax.bitcast_convert_type(wr, jnp.uint32) & jnp.uint32(0xFFFF0000), F32)
    wr = jnp.stack([wr_hi.astype(BF16), (wr - wr_hi).astype(BF16)])
    br = jnp.zeros((1, LANES), F32).at[0, :ngroups].set(b_router_group[0]).at[0, ngroups:ngroups + nexp].set(
        b_router_expert[0])
    x1, h2, rinfo, fields, counts = _mix(
        y_fox, o_h, gf, gh, x, g1, sc2, sh2,
        w_up_fox[0].astype(BF16), w_up_hgrn[0].astype(BF16), w_out[0].astype(BF16),
        ln1_g[0].reshape(1, D), ln1_b[0].reshape(1, D), wr, br, alpha, ngroups, nper)

    tm_e = 256
    dt = D // WORD_LANES
    ntiles = (2 * T) // tm_e + nexp
    nslots = ntiles * tm_e
    cnt = counts[0, :nexp].astype(jnp.int32)
    padded = ((cnt + tm_e - 1) // tm_e) * tm_e
    ends = jnp.cumsum(padded)
    starts = ends - padded
    eid = fields[2:4].astype(jnp.int32)
    rank = fields[4:6].astype(jnp.int32)
    first = jnp.sum(jnp.where(eid[None] == jnp.arange(nexp, dtype=jnp.int32)[:, None, None],
                              starts[:, None, None], 0), axis=0)
    pos = first + rank
    tile_start = jnp.arange(ntiles, dtype=jnp.int32) * tm_e
    tile_expert = jnp.minimum(jnp.sum((tile_start[:, None] >= ends[None, :]).astype(jnp.int32), axis=1), nexp - 1)
    tile_rows = jnp.clip(starts[tile_expert] + cnt[tile_expert] - tile_start, 0, tm_e)
    rows = pos[:, None, :] + (jnp.arange(dt, dtype=jnp.int32) * nslots)[None, :, None]

    xs = _sc_scatter_rows(h2.reshape(dt * T, LANES), rows[0], rows[1], dt * nslots)
    ys = _experts(tile_expert, tile_rows, xs.reshape(dt, nslots, LANES),
                  w_expert_gate[0], w_expert_up[0], w_expert_down[0], tm_e)
    yg = _sc_gather_rows(ys.reshape(dt * nslots, LANES), rows.reshape(2 * dt, T))
    return _combine(yg.reshape(2, dt, T, LANES), x1, rinfo, g2,
                    ln2_g[0].reshape(1, D), ln2_b[0].reshape(1, D), alpha)
```

```python
import functools

import jax
import jax.numpy as jnp
from jax import lax
from jax.experimental import pallas as pl
from jax.experimental.pallas import tpu as pltpu
from jax.experimental.pallas import tpu_sc as plsc

F32 = jnp.float32
BF16 = jnp.bfloat16
HIGHEST = lax.Precision.HIGHEST

LANES = 128
HEAD_DIM = 64
LN_EPS = 1e-5
RMS_EPS = 1e-6
LOG2E = 1.4426950408889634
NEG_BIG = -1e30
HCHUNK = 16
HBLOCK = 64
HGRN_SAFE_EXP = 60.0
ROW_TILE = 8
WORD_LANES = 2 * LANES
VMEM_LIMIT = 56 * 1024 * 1024


def _cparams(sem, vmem=VMEM_LIMIT):
    return pltpu.CompilerParams(dimension_semantics=sem, vmem_limit_bytes=vmem)


def _sigmoid(x):
    return 1.0 / (1.0 + jnp.exp(-x))


def _silu(x):
    return x * _sigmoid(x)


def _ada_kernel(c_ref, w_ref, b_ref, o_ref):
    c = c_ref[...]
    o_ref[...] = jnp.dot(_silu(c), w_ref[...], precision=HIGHEST,
                         preferred_element_type=F32) + b_ref[...]


def _ada(c, w_ada, b_ada):
    B, D = c.shape
    N = w_ada.shape[1]
    tn = 1024
    return pl.pallas_call(
        _ada_kernel,
        out_shape=jax.ShapeDtypeStruct((B, N), F32),
        grid=(N // tn,),
        in_specs=[pl.BlockSpec((B, D), lambda j: (0, 0)),
                  pl.BlockSpec((D, tn), lambda j: (0, j)),
                  pl.BlockSpec((1, tn), lambda j: (0, j))],
        out_specs=pl.BlockSpec((B, tn), lambda j: (0, j)),
        compiler_params=_cparams(("arbitrary",)),
    )(c, w_ada, b_ada.reshape(1, N))


def _inproj_kernel(x_ref, sc_ref, sh_ref, w_ref,
                   fq_ref, fk_ref, fv_ref, ff_ref, hq_ref, hf_ref, hi_ref, hg_ref, gf_ref, gh_ref,
                   *, segs, q_scale):
    h = (x_ref[...] * (1.0 + sc_ref[...]) + sh_ref[...]).astype(BF16)
    outs = (fq_ref, fk_ref, fv_ref, ff_ref, hq_ref, hf_ref, hi_ref, hg_ref, gf_ref, gh_ref)
    for idx, (o_ref, (a, b)) in enumerate(zip(outs, segs)):
        r = jnp.dot(h, w_ref[:, a:b], preferred_element_type=F32)
        if idx == 0:
            r = r * q_scale
        o_ref[...] = r.astype(o_ref.dtype)


def _inproj(x, sc1, sh1, w_packed, segs, tm=256):
    B, S, D = x.shape
    widths = [b - a for a, b in segs]
    dtypes = [BF16, BF16, BF16, F32, BF16, F32, BF16, BF16, BF16, BF16]
    out_shape = tuple(jax.ShapeDtypeStruct((B, S, w), dt) for w, dt in zip(widths, dtypes))
    out_specs = tuple(pl.BlockSpec((None, tm, w), lambda b, i: (b, i, 0)) for w in widths)
    vec = pl.BlockSpec((None, 1, D), lambda b, i: (b, 0, 0))
    return pl.pallas_call(
        functools.partial(_inproj_kernel, segs=tuple(segs), q_scale=HEAD_DIM ** -0.5 * LOG2E),
        out_shape=out_shape,
        grid=(B, S // tm),
        in_specs=[pl.BlockSpec((None, tm, D), lambda b, i: (b, i, 0)), vec, vec,
                  pl.BlockSpec(w_packed.shape, lambda b, i: (0, 0))],
        out_specs=out_specs,
        compiler_params=_cparams(("parallel", "parallel")),
    )(x, sc1, sh1, w_packed)


def _foxcum_kernel(ff_ref, b_ref, o_ref, *, blk):
    S = ff_ref.shape[0]
    r = lax.broadcasted_iota(jnp.int32, (blk, blk), 0)
    c = lax.broadcasted_iota(jnp.int32, (blk, blk), 1)
    lower = (r >= c).astype(F32)
    carry = jnp.zeros((1, LANES), F32)
    for j in range(S // blk):
        z = ff_ref[j * blk:(j + 1) * blk, :] + b_ref[...]
        lf = jnp.minimum(z, 0.0) - jnp.log(1.0 + jnp.exp(-jnp.abs(z)))
        cum = jnp.dot(lower, lf, precision=HIGHEST, preferred_element_type=F32) + carry
        o_ref[j * blk:(j + 1) * blk, :] = cum * LOG2E
        carry = cum[blk - 1:blk, :]


def _foxcum(ffp, bias_p, blk=256):
    B, S, _ = ffp.shape
    return pl.pallas_call(
        functools.partial(_foxcum_kernel, blk=blk),
        out_shape=jax.ShapeDtypeStruct((B, S, LANES), F32),
        grid=(B,),
        in_specs=[pl.BlockSpec((None, S, LANES), lambda b: (b, 0, 0)),
                  pl.BlockSpec((1, LANES), lambda b: (0, 0))],
        out_specs=pl.BlockSpec((None, S, LANES), lambda b: (b, 0, 0)),
        compiler_params=_cparams(("parallel",)),
    )(ffp, bias_p)


NCUM = 3


def _fox_kernel(q_ref, k_ref, v_ref, c_ref, o_ref, ka_sc, kb_sc, va_sc, vb_sc, *, tq, tk):
    p = pl.program_id(1)
    qi = pl.program_id(2)
    S = k_ref.shape[0]

    @pl.when(qi == 0)
    def _():
        lane = lax.broadcasted_iota(jnp.int32, (S, LANES), 1)
        rr = lax.broadcasted_iota(jnp.int32, (LANES, LANES), 0)
        cc = lax.broadcasted_iota(jnp.int32, (LANES, LANES), 1)
        rest = c_ref[...]
        placed = jnp.zeros((S, LANES), F32)
        for i in range(NCUM):
            piece = rest.astype(BF16)
            rest = rest - piece.astype(F32)
            sel = ((rr == 2 * p) & (cc == HEAD_DIM + i)) | ((rr == 2 * p + 1) & (cc == i))
            placed = placed + jnp.dot(piece, jnp.where(sel, 1.0, 0.0).astype(BF16), preferred_element_type=F32)
        k2 = k_ref[...].astype(F32)
        ka_sc[...] = jnp.where(lane < HEAD_DIM, k2, -placed).astype(BF16)
        kb_sc[...] = jnp.where(lane >= HEAD_DIM, k2, -placed).astype(BF16)
        vt = v_ref[...].astype(F32).T
        row = lax.broadcasted_iota(jnp.int32, (LANES, S), 0)
        va_sc[...] = jnp.where(row < HEAD_DIM, vt, jnp.where(row == HEAD_DIM, 1.0, 0.0)).astype(BF16)
        vb_sc[...] = jnp.where(row >= HEAD_DIM, vt, jnp.where(row == 0, 1.0, 0.0)).astype(BF16)

    q2 = q_ref[...].astype(F32)
    qlane = lax.broadcasted_iota(jnp.int32, (tq, LANES), 1)
    qa = jnp.where(qlane < HEAD_DIM, q2, jnp.where(qlane < HEAD_DIM + NCUM, 1.0, 0.0)).astype(BF16)
    qb = jnp.where(qlane >= HEAD_DIM, q2, jnp.where(qlane < NCUM, 1.0, 0.0)).astype(BF16)
    nsub = tq // tk

    def block(k0, carry, diag_off):
        q0 = 0 if diag_off is None else diag_off
        out = []
        for ksc, vsc, qh, (m, acc) in ((ka_sc, va_sc, qa, carry[:2]), (kb_sc, vb_sc, qb, carry[2:])):
            st = lax.dot_general(ksc[pl.ds(k0, tk), :], qh[q0:, :], (((1,), (1,)), ((), ())),
                                 preferred_element_type=F32)
            if diag_off is not None:
                st = jnp.where(lax.broadcasted_iota(jnp.int32, st.shape, 0)
                               <= lax.broadcasted_iota(jnp.int32, st.shape, 1), st, NEG_BIG)
            m_old = m[:, q0:]
            m_new = jnp.maximum(m_old, jnp.max(st, axis=0, keepdims=True))
            pt = jnp.exp2(st - m_new).astype(BF16)
            acc_new = (jnp.exp2(m_old - m_new) * acc[:, q0:]
                       + jnp.dot(vsc[:, pl.ds(k0, tk)], pt, preferred_element_type=F32))
            if q0:
                m_new = jnp.concatenate([m[:, :q0], m_new], axis=1)
                acc_new = jnp.concatenate([acc[:, :q0], acc_new], axis=1)
            out += [m_new, acc_new]
        return tuple(out)

    def group(j, carry):
        k0 = pl.multiple_of(j * (nsub * tk), nsub * tk)
        for u in range(nsub):
            carry = block(k0 + u * tk, carry, None)
        return carry

    m0 = jnp.full((1, tq), NEG_BIG, F32)
    a0 = jnp.zeros((LANES, tq), F32)
    carry = lax.fori_loop(0, qi, group, (m0, a0, m0, a0))
    for d in range(nsub):
        carry = block(pl.multiple_of(qi * tq + d * tk, tk), carry, d * tk)
    _, aa, _, ab = carry
    row = lax.broadcasted_iota(jnp.int32, (LANES, tq), 0)
    ot = jnp.where(row < HEAD_DIM, aa * (1.0 / aa[HEAD_DIM:HEAD_DIM + 1, :]), ab * (1.0 / ab[0:1, :]))
    o_ref[...] = ot.T.astype(o_ref.dtype)


def _fox(fq, fk, fv, cum, tq=1024, tk=256):
    B, S, W = fq.shape
    assert tq % (2 * tk) == 0 and S % tq == 0
    npairs = W // LANES
    return pl.pallas_call(
        functools.partial(_fox_kernel, tq=tq, tk=tk),
        out_shape=jax.ShapeDtypeStruct((B, S, W), BF16),
        grid=(B, npairs, S // tq),
        in_specs=[pl.BlockSpec((None, tq, LANES), lambda b, p, i: (b, i, p)),
                  pl.BlockSpec((None, S, LANES), lambda b, p, i: (b, 0, p)),
                  pl.BlockSpec((None, S, LANES), lambda b, p, i: (b, 0, p)),
                  pl.BlockSpec((None, S, LANES), lambda b, p, i: (b, 0, 0))],
        out_specs=pl.BlockSpec((None, tq, LANES), lambda b, p, i: (b, i, p)),
        scratch_shapes=[pltpu.VMEM((S, LANES), BF16), pltpu.VMEM((S, LANES), BF16),
                        pltpu.VMEM((LANES, S), BF16), pltpu.VMEM((LANES, S), BF16)],
        compiler_params=_cparams(("parallel", "parallel", "arbitrary")),
    )(fq, fk, fv, cum)


def _hgrn_kernel(hq_ref, hf_ref, hi_ref, hg_ref, lb_ref, nw_ref, o_ref,
                 a_sc, b_sc, kk_sc, qq_sc, o_sc, w1_sc, w2_sc, w3_sc, w4_sc, p_sc, st16_sc, dec_sc, st64_sc):
    S = hq_ref.shape[0]
    C = HCHUNK
    nchunks = S // C
    BLK = HBLOCK
    nblk = S // BLK

    lg = lb_ref[...]
    e = jnp.exp(lg - jnp.max(lg, axis=0, keepdims=True))
    lb = e[0:1, :] / jnp.sum(e, axis=0, keepdims=True)

    f = lb + (1.0 - lb) * _sigmoid(hf_ref[...])
    lf = jnp.log(f)
    kk_sc[...] = 1.0 - f
    qq_sc[...] = _silu(hq_ref[...].astype(F32))

    row = lax.broadcasted_iota(jnp.int32, (S, LANES), 0)
    rmod = row & (C - 1)
    a = lf
    d = 1
    while d < C:
        a = a + jnp.where(rmod >= d, pltpu.roll(a, d, axis=0), 0.0)
        d *= 2
    a3 = a.reshape(nchunks, C, LANES)
    alast = jnp.broadcast_to(a3[:, C - 1:C, :], (nchunks, C, LANES)).reshape(S, LANES)
    bmod = row & (BLK - 1)
    tot = alast
    d = C
    while d < BLK:
        tot = tot + jnp.where(bmod >= d, pltpu.roll(tot, d, axis=0), 0.0)
        d *= 2
    b = a + (tot - alast)
    b3 = b.reshape(nblk, BLK, LANES)
    blast = jnp.broadcast_to(b3[:, BLK - 1:BLK, :], (nblk, BLK, LANES)).reshape(S, LANES)
    a_sc[...] = a
    b_sc[...] = b
    safe = jnp.max(-blast) <= HGRN_SAFE_EXP

    lane = lax.broadcasted_iota(jnp.int32, (C, LANES), 1)
    sr = lax.broadcasted_iota(jnp.int32, (LANES, LANES), 0)
    scn = lax.broadcasted_iota(jnp.int32, (LANES, LANES), 1)
    same_head = (sr // HEAD_DIM) == (scn // HEAD_DIM)

    @pl.when(safe)
    def _factorised():
        qa_sc, qb_sc, kh_sc, ke_sc = w1_sc, w2_sc, w3_sc, w4_sc
        bb = b_sc[...]
        bl = jnp.broadcast_to(bb.reshape(nblk, BLK, LANES)[:, BLK - 1:BLK, :], (nblk, BLK, LANES)).reshape(S, LANES)
        qh = qq_sc[...] * jnp.exp(bb)
        slane = lax.broadcasted_iota(jnp.int32, (S, LANES), 1)
        qa_sc[...] = jnp.where(slane < HEAD_DIM, qh, 0.0).astype(BF16)
        qb_sc[...] = jnp.where(slane >= HEAD_DIM, qh, 0.0).astype(BF16)
        kh_sc[...] = (kk_sc[...] * jnp.exp(-bb)).astype(BF16)
        ke_sc[...] = (kk_sc[...] * jnp.exp(bl - bb)).astype(BF16)
        dec_sc[pl.ds(0, nblk), :] = jnp.exp(bb.reshape(nblk, BLK, LANES)[:, BLK - 1, :])
        unroll = 8

        def scan(g, st):
            for u in range(unroll):
                i = g * unroll + u
                r0 = pl.multiple_of(i * BLK, BLK)
                st64_sc[i] = st.astype(BF16)
                upd = lax.dot_general(hi_ref[pl.ds(r0, BLK), :], ke_sc[pl.ds(r0, BLK), :],
                                      (((0,), (0,)), ((), ())), preferred_element_type=F32)
                st = st * dec_sc[pl.ds(i, 1), :] + jnp.where(same_head, upd, 0.0)
            return st

        lax.fori_loop(0, nblk // unroll, scan, jnp.zeros((LANES, LANES), F32))
        trow = lax.broadcasted_iota(jnp.int32, (BLK, BLK), 0)
        scol = lax.broadcasted_iota(jnp.int32, (BLK, BLK), 1)
        blane = lax.broadcasted_iota(jnp.int32, (BLK, LANES), 1)
        nt = (((1,), (1,)), ((), ()))

        def readout(g, _):
            for u in range(unroll):
                i = g * unroll + u
                r0 = pl.multiple_of(i * BLK, BLK)
                vb = hi_ref[pl.ds(r0, BLK), :]
                kb = kh_sc[pl.ds(r0, BLK), :]
                qa = qa_sc[pl.ds(r0, BLK), :]
                qb = qb_sc[pl.ds(r0, BLK), :]
                outs = []
                for qx in (qa, qb):
                    sc = lax.dot_general(qx, kb, nt, preferred_element_type=F32)
                    sc = jnp.where(trow >= scol, sc, 0.0).astype(BF16)
                    outs.append(jnp.dot(sc, vb, preferred_element_type=F32))
                o_inter = lax.dot_general(qa + qb, st64_sc[i], nt, preferred_element_type=F32)
                o_sc[pl.ds(r0, BLK), :] = jnp.where(blane < HEAD_DIM, outs[0], outs[1]) + o_inter
            return 0

        lax.fori_loop(0, nblk // unroll, readout, 0)

    @pl.when(jnp.logical_not(safe))
    def _direct():
        qt_sc, kt_sc, s_sc, a2_sc = w1_sc, w2_sc, w3_sc, b_sc
        aa = a_sc[...]
        al = jnp.broadcast_to(aa.reshape(nchunks, C, LANES)[:, C - 1:C, :], (nchunks, C, LANES)).reshape(S, LANES)
        qt_sc[...] = (qq_sc[...] * jnp.exp(aa)).astype(BF16)
        kt_sc[...] = (kk_sc[...] * jnp.exp(al - aa)).astype(BF16)
        dec_sc[...] = jnp.exp(aa.reshape(nchunks, C, LANES)[:, C - 1, :])
        a2_sc[...] = aa * LOG2E
        trow = lax.broadcasted_iota(jnp.int32, (C, LANES), 0)

        def gen(c, _):
            r0 = pl.multiple_of(c * C, C)
            ac = a2_sc[pl.ds(r0, C), :]
            qc = qq_sc[pl.ds(r0, C), :]
            kc = kk_sc[pl.ds(r0, C), :]
            half = C // 2
            for s in range(C):
                if s < half:
                    dec = jnp.exp2(jnp.where(trow >= s, ac - ac[s:s + 1, :], NEG_BIG))
                    p = qc * (kc[s:s + 1, :] * dec)
                else:
                    dec = jnp.exp2(jnp.where(trow[half:] >= s, ac[half:] - ac[s:s + 1, :], NEG_BIG))
                    p = jnp.concatenate([jnp.zeros((half, LANES), F32), qc[half:] * (kc[s:s + 1, :] * dec)],
                                        axis=0)
                p_sc[pl.ds(r0, C), s * LANES:(s + 1) * LANES] = p.astype(BF16)
            return 0

        lax.fori_loop(0, nchunks, gen, 0)

        er = lax.broadcasted_iota(jnp.int32, (C * LANES, LANES), 0)
        ec = lax.broadcasted_iota(jnp.int32, (C * LANES, LANES), 1)
        emat = (ec == ((er & (LANES - 1)) // HEAD_DIM) * C + er // LANES).astype(BF16)
        rb = 256

        def red(i, _):
            r0 = pl.multiple_of(i * rb, rb)
            s_sc[pl.ds(r0, rb), :] = jnp.dot(p_sc[pl.ds(r0, rb), :], emat,
                                             preferred_element_type=F32).astype(BF16)
            return 0

        lax.fori_loop(0, S // rb, red, 0)

        unroll = 16

        def scan(g, st):
            for u in range(unroll):
                c = g * unroll + u
                r0 = pl.multiple_of(c * C, C)
                st16_sc[c] = st.astype(BF16)
                upd = lax.dot_general(hi_ref[pl.ds(r0, C), :], kt_sc[pl.ds(r0, C), :],
                                      (((0,), (0,)), ((), ())), preferred_element_type=F32)
                st = st * dec_sc[pl.ds(c, 1), :] + jnp.where(same_head, upd, 0.0)
            return st

        lax.fori_loop(0, nchunks // unroll, scan, jnp.zeros((LANES, LANES), F32))

        def readout(g, _):
            for u in range(unroll):
                c = g * unroll + u
                r0 = pl.multiple_of(c * C, C)
                vc = hi_ref[pl.ds(r0, C), :]
                o_inter = lax.dot_general(qt_sc[pl.ds(r0, C), :], st16_sc[c],
                                          (((1,), (1,)), ((), ())), preferred_element_type=F32)
                v2 = jnp.concatenate([jnp.where(lane < HEAD_DIM, vc, jnp.zeros_like(vc)),
                                      jnp.where(lane >= HEAD_DIM, vc, jnp.zeros_like(vc))], axis=0)
                o_intra = jnp.dot(s_sc[pl.ds(r0, C), :][:, :2 * C], v2, preferred_element_type=F32)
                o_sc[pl.ds(r0, C), :] = o_inter + o_intra
            return 0

        lax.fori_loop(0, nchunks // unroll, readout, 0)

    o = o_sc[...]
    ones_head = jnp.where(same_head, 1.0 / HEAD_DIM, 0.0).astype(F32)
    ms = jnp.dot(o * o, ones_head, precision=HIGHEST, preferred_element_type=F32)
    y = o * lax.rsqrt(ms + RMS_EPS) * nw_ref[...]
    o_ref[...] = (y * _silu(hg_ref[...].astype(F32))).astype(o_ref.dtype)


def _hgrn(hq, hf, hi, hg, lb_logits, norm_w):
    B, S, W = hq.shape
    npairs = W // LANES
    nrows = lb_logits.shape[0]
    seq = pl.BlockSpec((None, S, LANES), lambda b, p: (b, 0, p))
    return pl.pallas_call(
        _hgrn_kernel,
        out_shape=jax.ShapeDtypeStruct((B, S, W), BF16),
        grid=(B, npairs),
        in_specs=[seq, seq, seq, seq,
                  pl.BlockSpec((nrows, LANES), lambda b, p: (0, p)),
                  pl.BlockSpec((1, LANES), lambda b, p: (0, p))],
        out_specs=seq,
        scratch_shapes=[pltpu.VMEM((S, LANES), F32),
                        pltpu.VMEM((S, LANES), F32),
                        pltpu.VMEM((S, LANES), F32),
                        pltpu.VMEM((S, LANES), F32),
                        pltpu.VMEM((S, LANES), F32),
                        pltpu.VMEM((S, LANES), BF16),
                        pltpu.VMEM((S, LANES), BF16),
                        pltpu.VMEM((S, LANES), BF16),
                        pltpu.VMEM((S, LANES), BF16),
                        pltpu.VMEM((S, HCHUNK * LANES), BF16),
                        pltpu.VMEM((S // HCHUNK, LANES, LANES), BF16),
                        pltpu.VMEM((S // HCHUNK, LANES), F32),
                        pltpu.VMEM((S // HBLOCK, LANES, LANES), BF16)],
        compiler_params=_cparams(("parallel", "parallel")),
    )(hq, hf, hi, hg, lb_logits, norm_w.reshape(1, W))


def _layer_norm(v, g, b):
    mu = jnp.mean(v, axis=-1, keepdims=True)
    d = v - mu
    var = jnp.mean(d * d, axis=-1, keepdims=True)
    return d * lax.rsqrt(var + LN_EPS) * g + b


def _bf16_bits(x):
    u = pltpu.bitcast(x, jnp.uint32)
    return (u + jnp.uint32(0x7FFF) + ((u >> 16) & jnp.uint32(1))) & jnp.uint32(0xFFFF0000)


def _store_chunks(ref, val):
    n = ref.shape[0]
    for j in range(n):
        lo = _bf16_bits(val[:, j * LANES:(j + 1) * LANES]) >> 16
        hi = _bf16_bits(val[:, (j + n) * LANES:(j + n + 1) * LANES])
        ref[j] = pltpu.bitcast(lo | hi, F32)


def _load_chunks(ref):
    words = [pltpu.bitcast(ref[j], jnp.uint32) for j in range(ref.shape[0])]
    lo = [pltpu.bitcast(w << 16, F32) for w in words]
    hi = [pltpu.bitcast(w & jnp.uint32(0xFFFF0000), F32) for w in words]
    return jnp.concatenate(lo + hi, axis=1)


def _mix_kernel(yf_ref, oh_ref, gf_ref, gh_ref, x_ref, g1_ref, sc2_ref, sh2_ref,
                wuf_ref, wuh_ref, wo_ref, lg_ref, lbias_ref, wr_ref, br_ref,
                x1_ref, h2_ref, ri_ref, rt_ref, cnt_ref, carry_sc, *, alpha, ngroups, nper):
    first = (pl.program_id(0) == 0) & (pl.program_id(1) == 0)

    @pl.when(first)
    def _():
        carry_sc[...] = jnp.zeros_like(carry_sc)

    tm = x_ref.shape[0]
    yf = jnp.dot(yf_ref[...], wuf_ref[...], preferred_element_type=F32)
    yh = jnp.dot(oh_ref[...], wuh_ref[...], preferred_element_type=F32)
    merged = _sigmoid(gf_ref[...].astype(F32)) * yf + _sigmoid(gh_ref[...].astype(F32)) * yh
    y = jnp.dot(merged.astype(BF16), wo_ref[...], preferred_element_type=F32)
    x1 = _layer_norm(alpha * x_ref[...] + g1_ref[...] * y, lg_ref[...], lbias_ref[...])
    x1_ref[...] = x1
    h2 = x1 * (1.0 + sc2_ref[...]) + sh2_ref[...]
    _store_chunks(h2_ref, h2)

    h_top = pltpu.bitcast(pltpu.bitcast(h2, jnp.uint32) & jnp.uint32(0xFFFF0000), F32)
    h_hi = h_top.astype(BF16)
    h_lo = (h2 - h_top).astype(BF16)
    logits = (jnp.dot(h_hi, wr_ref[0], preferred_element_type=F32)
              + jnp.dot(h_hi, wr_ref[1], preferred_element_type=F32)
              + jnp.dot(h_lo, wr_ref[0], preferred_element_type=F32)) + br_ref[...]
    lane = lax.broadcasted_iota(jnp.int32, (tm, LANES), 1)
    big = jnp.int32(1 << 20)

    def argmax_first(vals, mask):
        mx = jnp.max(jnp.where(mask, vals, -jnp.inf), axis=1, keepdims=True)
        idx = jnp.min(jnp.where(mask & (vals == mx), lane, big), axis=1, keepdims=True)
        return mx, idx

    gmask = lane < ngroups
    gmax = jnp.max(jnp.where(gmask, logits, -jnp.inf), axis=1, keepdims=True)
    gexp = jnp.where(gmask, jnp.exp(logits - gmax), 0.0)
    gprob = gexp / jnp.sum(gexp, axis=1, keepdims=True)
    g_w, g_idx = argmax_first(gprob, gmask)

    lo = ngroups + g_idx * nper
    emask = (lane >= lo) & (lane < lo + nper)
    emax = jnp.max(jnp.where(emask, logits, -jnp.inf), axis=1, keepdims=True)
    eexp = jnp.where(emask, jnp.exp(logits - emax), 0.0)
    eprob = eexp / jnp.sum(eexp, axis=1, keepdims=True)
    p0, i0 = argmax_first(eprob, emask)
    p1, i1 = argmax_first(eprob, emask & (lane != i0))
    den = p0 + p1
    w0 = p0 / den * g_w
    w1 = p1 / den * g_w
    e0 = i0 - ngroups
    e1 = i1 - ngroups

    oh = ((lane == e0) | (lane == e1)).astype(F32)
    r = lax.broadcasted_iota(jnp.int32, (tm, tm), 0)
    c = lax.broadcasted_iota(jnp.int32, (tm, tm), 1)
    strict_lower = (c < r).astype(BF16)
    before = jnp.dot(strict_lower, oh.astype(BF16), preferred_element_type=F32) + carry_sc[...]
    rank0 = jnp.sum(jnp.where(lane == e0, before, 0.0), axis=1, keepdims=True)
    rank1 = jnp.sum(jnp.where(lane == e1, before, 0.0), axis=1, keepdims=True)
    carry_sc[...] = carry_sc[...] + jnp.sum(oh, axis=0, keepdims=True)
    cnt_ref[...] = carry_sc[...]

    info = jnp.where(lane == 0, w0, 0.0)
    info = jnp.where(lane == 1, w1, info)
    info = jnp.where(lane == 2, e0.astype(F32), info)
    info = jnp.where(lane == 3, e1.astype(F32), info)
    info = jnp.where(lane == 4, rank0, info)
    info = jnp.where(lane == 5, rank1, info)
    ri_ref[...] = info
    rt_ref[...] = info.T[:ROW_TILE, :]


def _mix(yf, oh, gf, gh, x, g1, sc2, sh2, wuf, wuh, wo, ln_g, ln_b, wr, br, alpha, ngroups, nper, tm=512):
    B, S, D = x.shape
    W = yf.shape[2]
    tok = lambda w: pl.BlockSpec((None, tm, w), lambda b, i: (b, i, 0))
    vec = pl.BlockSpec((None, 1, D), lambda b, i: (b, 0, 0))
    full = lambda a: pl.BlockSpec(a.shape, lambda b, i: (0,) * a.ndim)
    return pl.pallas_call(
        functools.partial(_mix_kernel, alpha=alpha, ngroups=ngroups, nper=nper),
        out_shape=(jax.ShapeDtypeStruct((B, S, D), F32),
                   jax.ShapeDtypeStruct((D // WORD_LANES, B * S, LANES), F32),
                   jax.ShapeDtypeStruct((B, S, LANES), F32),
                   jax.ShapeDtypeStruct((ROW_TILE, B * S), F32),
                   jax.ShapeDtypeStruct((1, LANES), F32)),
        grid=(B, S // tm),
        in_specs=[tok(W), tok(W), tok(D), tok(D), tok(D), vec, vec, vec,
                  full(wuf), full(wuh), full(wo), full(ln_g), full(ln_b), full(wr), full(br)],
        out_specs=(tok(D),
                   pl.BlockSpec((D // WORD_LANES, tm, LANES), lambda b, i: (0, b * (S // tm) + i, 0)),
                   tok(LANES),
                   pl.BlockSpec((ROW_TILE, tm), lambda b, i: (0, b * (S // tm) + i)),
                   pl.BlockSpec((1, LANES), lambda b, i: (0, 0))),
        scratch_shapes=[pltpu.VMEM((1, LANES), F32)],
        compiler_params=_cparams(("arbitrary", "arbitrary")),
    )(yf, oh, gf, gh, x, g1, sc2, sh2, wuf, wuh, wo, ln_g, ln_b, wr, br)


def _sc_mesh():
    return plsc.VectorSubcoreMesh(core_axis_name="core", subcore_axis_name="subcore")


def _sc_pipeline(body, grid, in_specs, out_specs):
    return pltpu.emit_pipeline(body, grid=grid, in_specs=in_specs, out_specs=out_specs,
                               core_axis_name=("core", "subcore"),
                               dimension_semantics=(pltpu.PARALLEL,) * len(grid))


def _sc_scatter_rows(src, rows_a, rows_b, n_out):
    nj, t = rows_a.shape
    nc = t // LANES

    @pl.kernel(out_type=jax.ShapeDtypeStruct((n_out, LANES), src.dtype), mesh=_sc_mesh(), scratch_types=[])
    def scatter(x_hbm, a_hbm, b_hbm, o_hbm):
        def body(x_vmem, a_vmem, b_vmem):
            pltpu.sync_copy(x_vmem, o_hbm.at[a_vmem.at[0]])
            pltpu.sync_copy(x_vmem, o_hbm.at[b_vmem.at[0]])

        idx = pl.BlockSpec((1, LANES), lambda j, c: (j, c))
        _sc_pipeline(body, (nj, nc), [pl.BlockSpec((LANES, LANES), lambda j, c: (j * nc + c, 0)), idx, idx],
                     [])(x_hbm, a_hbm, b_hbm)

    return scatter(src, rows_a, rows_b)


def _sc_gather_rows(table, rows):
    nr, t = rows.shape
    nc = t // LANES

    @pl.kernel(out_type=jax.ShapeDtypeStruct((nr * t, LANES), table.dtype), mesh=_sc_mesh(), scratch_types=[])
    def gather(x_hbm, i_hbm, o_hbm):
        def body(i_vmem, o_vmem):
            pltpu.sync_copy(x_hbm.at[i_vmem.at[0]], o_vmem)

        _sc_pipeline(body, (nr, nc), [pl.BlockSpec((1, LANES), lambda r, c: (r, c))],
                     [pl.BlockSpec((LANES, LANES), lambda r, c: (r * nc + c, 0))])(i_hbm, o_hbm)

    return gather(table, rows)


def _experts_kernel(te_ref, tn_ref, x_ref, wg_ref, wu_ref, wd_ref, o_ref):
    nrows = tn_ref[pl.program_id(0)]

    @pl.when(nrows > 0)
    def _():
        x = _load_chunks(x_ref)
        x = jnp.where(lax.broadcasted_iota(jnp.int32, x.shape, 0) < nrows, x, 0.0).astype(BF16)
        g = jnp.dot(x, wg_ref[...].astype(BF16), preferred_element_type=F32)
        u = jnp.dot(x, wu_ref[...].astype(BF16), preferred_element_type=F32)
        hid = (_silu(g) * u).astype(BF16)
        _store_chunks(o_ref, jnp.dot(hid, wd_ref[...].astype(BF16), preferred_element_type=F32))

    @pl.when(nrows == 0)
    def _():
        o_ref[...] = jnp.zeros_like(o_ref)


def _experts(tile_expert, tile_rows, xs, wg, wu, wd, tm):
    E, D, FF = wg.shape
    dt = D // WORD_LANES
    ntiles = tile_expert.shape[0]
    rows = pl.BlockSpec((dt, tm, LANES), lambda i, te, tn: (0, i, 0))
    grid_spec = pltpu.PrefetchScalarGridSpec(
        num_scalar_prefetch=2,
        grid=(ntiles,),
        in_specs=[rows,
                  pl.BlockSpec((None, D, FF), lambda i, te, tn: (te[i], 0, 0)),
                  pl.BlockSpec((None, D, FF), lambda i, te, tn: (te[i], 0, 0)),
                  pl.BlockSpec((None, FF, D), lambda i, te, tn: (te[i], 0, 0))],
        out_specs=rows,
    )
    return pl.pallas_call(
        _experts_kernel,
        out_shape=jax.ShapeDtypeStruct((dt, ntiles * tm, LANES), F32),
        grid_spec=grid_spec,
        compiler_params=_cparams(("arbitrary",)),
    )(tile_expert, tile_rows, xs, wg, wu, wd)


def _combine_kernel(yg_ref, x1_ref, ri_ref, g2_ref, lg_ref, lb_ref, o_ref, *, alpha):
    ri = ri_ref[...]
    y = ri[:, 0:1] * _load_chunks(yg_ref.at[0]) + ri[:, 1:2] * _load_chunks(yg_ref.at[1])
    o_ref[...] = _layer_norm(alpha * x1_ref[...] + g2_ref[...] * y, lg_ref[...], lb_ref[...])


def _combine(yg, x1, rinfo, g2, ln_g, ln_b, alpha, tm=256):
    B, S, D = x1.shape
    nb = S // tm
    return pl.pallas_call(
        functools.partial(_combine_kernel, alpha=alpha),
        out_shape=jax.ShapeDtypeStruct((B, S, D), F32),
        grid=(B, nb),
        in_specs=[pl.BlockSpec((2, D // WORD_LANES, tm, LANES), lambda b, i: (0, 0, b * nb + i, 0)),
                  pl.BlockSpec((None, tm, D), lambda b, i: (b, i, 0)),
                  pl.BlockSpec((None, tm, LANES), lambda b, i: (b, i, 0)),
                  pl.BlockSpec((None, 1, D), lambda b, i: (b, 0, 0)),
                  pl.BlockSpec((1, D), lambda b, i: (0, 0)),
                  pl.BlockSpec((1, D), lambda b, i: (0, 0))],
        out_specs=pl.BlockSpec((None, tm, D), lambda b, i: (b, i, 0)),
        compiler_params=_cparams(("parallel", "parallel")),
    )(yg, x1, rinfo, g2, ln_g, ln_b)


def kernel(x, c, w_ada, b_ada, w_in, b_fox_forget, hgrn_lb_logits, hgrn_norm_w, w_up_fox, w_up_hgrn, w_out,
           ln1_g, ln1_b, w_router_group, b_router_group, w_router_expert, b_router_expert,
           w_expert_gate, w_expert_up, w_expert_down, ln2_g, ln2_b):
    B, S, D = x.shape
    depth = w_ada.shape[0]
    assert depth == 1, "single-layer block"
    fox_heads = b_fox_forget.shape[1]
    fox_w = fox_heads * HEAD_DIM
    hgrn_w = hgrn_norm_w.shape[1]
    ngroups = w_router_group.shape[2]
    nexp = w_router_expert.shape[2]
    nper = nexp // ngroups
    alpha = (2 * depth) ** 0.25
    T = B * S

    ada = _ada(c, w_ada[0], b_ada[0])
    sh1, sc1, g1, sh2, sc2, g2 = [a.reshape(B, 1, D) for a in jnp.split(ada, 6, axis=-1)]

    wi = w_in[0]
    o_ff = 3 * fox_w
    w_packed = jnp.concatenate(
        [wi[:, :o_ff + fox_heads], jnp.zeros((D, LANES - fox_heads), wi.dtype), wi[:, o_ff + fox_heads:]],
        axis=1).astype(BF16)
    widths = [fox_w, fox_w, fox_w, LANES, hgrn_w, hgrn_w, hgrn_w, hgrn_w, D, D]
    segs, off = [], 0
    for w in widths:
        segs.append((off, off + w))
        off += w
    fq, fk, fv, ffp, hq, hf, hi, hg, gf, gh = _inproj(x, sc1, sh1, w_packed, segs)

    bias_p = jnp.zeros((1, LANES), F32).at[0, :fox_heads].set(b_fox_forget[0])
    cum = _foxcum(ffp, bias_p)
    y_fox = _fox(fq, fk, fv, cum)

    o_h = _hgrn(hq, hf, hi, hg, hgrn_lb_logits, hgrn_norm_w[0])

    wr = jnp.zeros((D, LANES), F32).at[:, :ngroups].set(w_router_group[0]).at[:, ngroups:ngroups + nexp].set(
        w_router_expert[0])
    wr_hi = lax.bitcast_convert_type(lax.bitcast_convert_type(wr, jnp.uint32) & jnp.uint32(0xFFFF0000), F32)
    wr = jnp.stack([wr_hi.astype(BF16), (wr - wr_hi).astype(BF16)])
    br = jnp.zeros((1, LANES), F32).at[0, :ngroups].set(b_router_group[0]).at[0, ngroups:ngroups + nexp].set(
        b_router_expert[0])
    x1, h2, rinfo, fields, counts = _mix(
        y_fox, o_h, gf, gh, x, g1, sc2, sh2,
        w_up_fox[0].astype(BF16), w_up_hgrn[0].astype(BF16), w_out[0].astype(BF16),
        ln1_g[0].reshape(1, D), ln1_b[0].reshape(1, D), wr, br, alpha, ngroups, nper)

    tm_e = 256
    dt = D // WORD_LANES
    ntiles = (2 * T) // tm_e + nexp
    nslots = ntiles * tm_e
    cnt = counts[0, :nexp].astype(jnp.int32)
    padded = ((cnt + tm_e - 1) // tm_e) * tm_e
    ends = jnp.cumsum(padded)
    starts = ends - padded
    eid = fields[2:4].astype(jnp.int32)
    rank = fields[4:6].astype(jnp.int32)
    first = jnp.sum(jnp.where(eid[None] == jnp.arange(nexp, dtype=jnp.int32)[:, None, None],
                              starts[:, None, None], 0), axis=0)
    pos = first + rank
    tile_start = jnp.arange(ntiles, dtype=jnp.int32) * tm_e
    tile_expert = jnp.minimum(jnp.sum((tile_start[:, None] >= ends[None, :]).astype(jnp.int32), axis=1), nexp - 1)
    tile_rows = jnp.clip(starts[tile_expert] + cnt[tile_expert] - tile_start, 0, tm_e)
    rows = pos[:, None, :] + (jnp.arange(dt, dtype=jnp.int32) * nslots)[None, :, None]

    xs = _sc_scatter_rows(h2.reshape(dt * T, LANES), rows[0], rows[1], dt * nslots)
    ys = _experts(tile_expert, tile_rows, xs.reshape(dt, nslots, LANES),
                  w_expert_gate[0], w_expert_up[0], w_expert_down[0], tm_e)
    yg = _sc_gather_rows(ys.reshape(dt * nslots, LANES), rows.reshape(2 * dt, T))
    return _combine(yg.reshape(2, dt, T, LANES), x1, rinfo, g2,
                    ln2_g[0].reshape(1, D), ln2_b[0].reshape(1, D), alpha)
```

```python
import functools

import jax
import jax.numpy as jnp
from jax import lax
from jax.experimental import pallas as pl
from jax.experimental.pallas import tpu as pltpu
from jax.experimental.pallas import tpu_sc as plsc

F32 = jnp.float32
BF16 = jnp.bfloat16
HIGHEST = lax.Precision.HIGHEST

LANES = 128
HEAD_DIM = 64
LN_EPS = 1e-5
RMS_EPS = 1e-6
LOG2E = 1.4426950408889634
NEG_BIG = -1e30
HCHUNK = 16
HBLOCK = 64
HGRN_SAFE_EXP = 60.0
ROW_TILE = 8
WORD_LANES = 2 * LANES
VMEM_LIMIT = 56 * 1024 * 1024


def _cparams(sem, vmem=VMEM_LIMIT):
    return pltpu.CompilerParams(dimension_semantics=sem, vmem_limit_bytes=vmem)


def _sigmoid(x):
    return 1.0 / (1.0 + jnp.exp(-x))


def _silu(x):
    return x * _sigmoid(x)


def _ada_kernel(c_ref, w_ref, b_ref, o_ref):
    c = c_ref[...]
    o_ref[...] = jnp.dot(_silu(c), w_ref[...], precision=HIGHEST,
                         preferred_element_type=F32) + b_ref[...]


def _ada(c, w_ada, b_ada):
    B, D = c.shape
    N = w_ada.shape[1]
    tn = 1024
    return pl.pallas_call(
        _ada_kernel,
        out_shape=jax.ShapeDtypeStruct((B, N), F32),
        grid=(N // tn,),
        in_specs=[pl.BlockSpec((B, D), lambda j: (0, 0)),
                  pl.BlockSpec((D, tn), lambda j: (0, j)),
                  pl.BlockSpec((1, tn), lambda j: (0, j))],
        out_specs=pl.BlockSpec((B, tn), lambda j: (0, j)),
        compiler_params=_cparams(("arbitrary",)),
    )(c, w_ada, b_ada.reshape(1, N))


def _inproj_kernel(x_ref, sc_ref, sh_ref, w_ref,
                   fq_ref, fk_ref, fv_ref, ff_ref, hq_ref, hf_ref, hi_ref, hg_ref, gf_ref, gh_ref,
                   *, segs, q_scale):
    h = (x_ref[...] * (1.0 + sc_ref[...]) + sh_ref[...]).astype(BF16)
    outs = (fq_ref, fk_ref, fv_ref, ff_ref, hq_ref, hf_ref, hi_ref, hg_ref, gf_ref, gh_ref)
    for idx, (o_ref, (a, b)) in enumerate(zip(outs, segs)):
        r = jnp.dot(h, w_ref[:, a:b], preferred_element_type=F32)
        if idx == 0:
            r = r * q_scale
        o_ref[...] = r.astype(o_ref.dtype)


def _inproj(x, sc1, sh1, w_packed, segs, tm=256):
    B, S, D = x.shape
    widths = [b - a for a, b in segs]
    dtypes = [BF16, BF16, BF16, F32, BF16, F32, BF16, BF16, BF16, BF16]
    out_shape = tuple(jax.ShapeDtypeStruct((B, S, w), dt) for w, dt in zip(widths, dtypes))
    out_specs = tuple(pl.BlockSpec((None, tm, w), lambda b, i: (b, i, 0)) for w in widths)
    vec = pl.BlockSpec((None, 1, D), lambda b, i: (b, 0, 0))
    return pl.pallas_call(
        functools.partial(_inproj_kernel, segs=tuple(segs), q_scale=HEAD_DIM ** -0.5 * LOG2E),
        out_shape=out_shape,
        grid=(B, S // tm),
        in_specs=[pl.BlockSpec((None, tm, D), lambda b, i: (b, i, 0)), vec, vec,
                  pl.BlockSpec(w_packed.shape, lambda b, i: (0, 0))],
        out_specs=out_specs,
        compiler_params=_cparams(("parallel", "parallel")),
    )(x, sc1, sh1, w_packed)


def _foxcum_kernel(ff_ref, b_ref, o_ref, *, blk):
    S = ff_ref.shape[0]
    r = lax.broadcasted_iota(jnp.int32, (blk, blk), 0)
    c = lax.broadcasted_iota(jnp.int32, (blk, blk), 1)
    lower = (r >= c).astype(F32)
    carry = jnp.zeros((1, LANES), F32)
    for j in range(S // blk):
        z = ff_ref[j * blk:(j + 1) * blk, :] + b_ref[...]
        lf = jnp.minimum(z, 0.0) - jnp.log(1.0 + jnp.exp(-jnp.abs(z)))
        cum = jnp.dot(lower, lf, precision=HIGHEST, preferred_element_type=F32) + carry
        o_ref[j * blk:(j + 1) * blk, :] = cum * LOG2E
        carry = cum[blk - 1:blk, :]


def _foxcum(ffp, bias_p, blk=256):
    B, S, _ = ffp.shape
    return pl.pallas_call(
        functools.partial(_foxcum_kernel, blk=blk),
        out_shape=jax.ShapeDtypeStruct((B, S, LANES), F32),
        grid=(B,),
        in_specs=[pl.BlockSpec((None, S, LANES), lambda b: (b, 0, 0)),
                  pl.BlockSpec((1, LANES), lambda b: (0, 0))],
        out_specs=pl.BlockSpec((None, S, LANES), lambda b: (b, 0, 0)),
        compiler_params=_cparams(("parallel",)),
    )(ffp, bias_p)


NCUM = 3


def _fox_kernel(q_ref, k_ref, v_ref, c_ref, o_ref, ka_sc, kb_sc, va_sc, vb_sc, *, tq, tk):
    p = pl.program_id(1)
    qi = pl.program_id(2)
    S = k_ref.shape[0]

    @pl.when(qi == 0)
    def _():
        lane = lax.broadcasted_iota(jnp.int32, (S, LANES), 1)
        rr = lax.broadcasted_iota(jnp.int32, (LANES, LANES), 0)
        cc = lax.broadcasted_iota(jnp.int32, (LANES, LANES), 1)
        rest = c_ref[...]
        placed = jnp.zeros((S, LANES), F32)
        for i in range(NCUM):
            piece = rest.astype(BF16)
            rest = rest - piece.astype(F32)
            sel = ((rr == 2 * p) & (cc == HEAD_DIM + i)) | ((rr == 2 * p + 1) & (cc == i))
            placed = placed + jnp.dot(piece, jnp.where(sel, 1.0, 0.0).astype(BF16), preferred_element_type=F32)
        k2 = k_ref[...].astype(F32)
        ka_sc[...] = jnp.where(lane < HEAD_DIM, k2, -placed).astype(BF16)
        kb_sc[...] = jnp.where(lane >= HEAD_DIM, k2, -placed).astype(BF16)
        vt = v_ref[...].astype(F32).T
        row = lax.broadcasted_iota(jnp.int32, (LANES, S), 0)
        va_sc[...] = jnp.where(row < HEAD_DIM, vt, jnp.where(row == HEAD_DIM, 1.0, 0.0)).astype(BF16)
        vb_sc[...] = jnp.where(row >= HEAD_DIM, vt, jnp.where(row == 0, 1.0, 0.0)).astype(BF16)

    q2 = q_ref[...].astype(F32)
    qlane = lax.broadcasted_iota(jnp.int32, (tq, LANES), 1)
    qa = jnp.where(qlane < HEAD_DIM, q2, jnp.where(qlane < HEAD_DIM + NCUM, 1.0, 0.0)).astype(BF16)
    qb = jnp.where(qlane >= HEAD_DIM, q2, jnp.where(qlane < NCUM, 1.0, 0.0)).astype(BF16)
    nsub = tq // tk

    def block(k0, carry, diag_off):
        q0 = 0 if diag_off is None else diag_off
        out = []
        for ksc, vsc, qh, (m, acc) in ((ka_sc, va_sc, qa, carry[:2]), (kb_sc, vb_sc, qb, carry[2:])):
            st = lax.dot_general(ksc[pl.ds(k0, tk), :], qh[q0:, :], (((1,), (1,)), ((), ())),
                                 preferred_element_type=F32)
            if diag_off is not None:
                st = jnp.where(lax.broadcasted_iota(jnp.int32, st.shape, 0)
                               <= lax.broadcasted_iota(jnp.int32, st.shape, 1), st, NEG_BIG)
            m_old = m[:, q0:]
            m_new = jnp.maximum(m_old, jnp.max(st, axis=0, keepdims=True))
            pt = jnp.exp2(st - m_new).astype(BF16)
            acc_new = (jnp.exp2(m_old - m_new) * acc[:, q0:]
                       + jnp.dot(vsc[:, pl.ds(k0, tk)], pt, preferred_element_type=F32))
            if q0:
                m_new = jnp.concatenate([m[:, :q0], m_new], axis=1)
                acc_new = jnp.concatenate([acc[:, :q0], acc_new], axis=1)
            out += [m_new, acc_new]
        return tuple(out)

    def group(j, carry):
        k0 = pl.multiple_of(j * (nsub * tk), nsub * tk)
        for u in range(nsub):
            carry = block(k0 + u * tk, carry, None)
        return carry

    m0 = jnp.full((1, tq), NEG_BIG, F32)
    a0 = jnp.zeros((LANES, tq), F32)
    carry = lax.fori_loop(0, qi, group, (m0, a0, m0, a0))
    for d in range(nsub):
        carry = block(pl.multiple_of(qi * tq + d * tk, tk), carry, d * tk)
    _, aa, _, ab = carry
    row = lax.broadcasted_iota(jnp.int32, (LANES, tq), 0)
    ot = jnp.where(row < HEAD_DIM, aa * (1.0 / aa[HEAD_DIM:HEAD_DIM + 1, :]), ab * (1.0 / ab[0:1, :]))
    o_ref[...] = ot.T.astype(o_ref.dtype)


def _fox(fq, fk, fv, cum, tq=1024, tk=256):
    B, S, W = fq.shape
    assert tq % (2 * tk) == 0 and S % tq == 0
    npairs = W // LANES
    return pl.pallas_call(
        functools.partial(_fox_kernel, tq=tq, tk=tk),
        out_shape=jax.ShapeDtypeStruct((B, S, W), BF16),
        grid=(B, npairs, S // tq),
        in_specs=[pl.BlockSpec((None, tq, LANES), lambda b, p, i: (b, i, p)),
                  pl.BlockSpec((None, S, LANES), lambda b, p, i: (b, 0, p)),
                  pl.BlockSpec((None, S, LANES), lambda b, p, i: (b, 0, p)),
                  pl.BlockSpec((None, S, LANES), lambda b, p, i: (b, 0, 0))],
        out_specs=pl.BlockSpec((None, tq, LANES), lambda b, p, i: (b, i, p)),
        scratch_shapes=[pltpu.VMEM((S, LANES), BF16), pltpu.VMEM((S, LANES), BF16),
                        pltpu.VMEM((LANES, S), BF16), pltpu.VMEM((LANES, S), BF16)],
        compiler_params=_cparams(("parallel", "parallel", "arbitrary")),
    )(fq, fk, fv, cum)


def _hgrn_kernel(hq_ref, hf_ref, hi_ref, hg_ref, lb_ref, nw_ref, o_ref,
                 a_sc, b_sc, kk_sc, qq_sc, o_sc, w1_sc, w2_sc, w3_sc, w4_sc, w5_sc, w6_sc,
                 p_sc, st16_sc, dec_sc, st64_sc):
    S = hq_ref.shape[0]
    C = HCHUNK
    nchunks = S // C
    BLK = HBLOCK
    nblk = S // BLK

    lg = lb_ref[...]
    e = jnp.exp(lg - jnp.max(lg, axis=0, keepdims=True))
    lb = e[0:1, :] / jnp.sum(e, axis=0, keepdims=True)

    f = lb + (1.0 - lb) * _sigmoid(hf_ref[...])
    lf = jnp.log(f)
    kk_sc[...] = 1.0 - f
    qq_sc[...] = _silu(hq_ref[...].astype(F32))

    row = lax.broadcasted_iota(jnp.int32, (S, LANES), 0)
    rmod = row & (C - 1)
    a = lf
    d = 1
    while d < C:
        a = a + jnp.where(rmod >= d, pltpu.roll(a, d, axis=0), 0.0)
        d *= 2
    a3 = a.reshape(nchunks, C, LANES)
    alast = jnp.broadcast_to(a3[:, C - 1:C, :], (nchunks, C, LANES)).reshape(S, LANES)
    bmod = row & (BLK - 1)
    tot = alast
    d = C
    while d < BLK:
        tot = tot + jnp.where(bmod >= d, pltpu.roll(tot, d, axis=0), 0.0)
        d *= 2
    b = a + (tot - alast)
    b3 = b.reshape(nblk, BLK, LANES)
    blast = jnp.broadcast_to(b3[:, BLK - 1:BLK, :], (nblk, BLK, LANES)).reshape(S, LANES)
    a_sc[...] = a
    b_sc[...] = b
    safe = jnp.max(-blast) <= HGRN_SAFE_EXP

    lane = lax.broadcasted_iota(jnp.int32, (C, LANES), 1)
    sr = lax.broadcasted_iota(jnp.int32, (LANES, LANES), 0)
    scn = lax.broadcasted_iota(jnp.int32, (LANES, LANES), 1)
    same_head = (sr // HEAD_DIM) == (scn // HEAD_DIM)

    @pl.when(safe)
    def _factorised():
        qa_sc, qb_sc, kh_sc, ke_sc, qd_sc, k2_sc = w1_sc, w2_sc, w3_sc, w4_sc, w5_sc, w6_sc
        SB = 2 * BLK
        nsb = S // SB
        bb = b_sc[...]
        bl = jnp.broadcast_to(bb.reshape(nblk, BLK, LANES)[:, BLK - 1:BLK, :], (nblk, BLK, LANES)).reshape(S, LANES)
        second = (row & BLK) != 0
        d_prev = jnp.exp(pltpu.roll(bl, BLK, axis=0))
        d_next = jnp.exp(pltpu.roll(bl, S - BLK, axis=0))
        qh = qq_sc[...] * jnp.exp(bb)
        slane = lax.broadcasted_iota(jnp.int32, (S, LANES), 1)
        qa_sc[...] = jnp.where(slane < HEAD_DIM, qh, 0.0).astype(BF16)
        qb_sc[...] = jnp.where(slane >= HEAD_DIM, qh, 0.0).astype(BF16)
        qd_sc[...] = (qh * jnp.where(second, d_prev, 1.0)).astype(BF16)
        kh_sc[...] = (kk_sc[...] * jnp.exp(-bb)).astype(BF16)
        ke = kk_sc[...] * jnp.exp(bl - bb)
        ke_sc[...] = ke.astype(BF16)
        k2_sc[...] = (ke * jnp.where(second, 1.0, d_next)).astype(BF16)
        bl3 = bl.reshape(nsb, SB, LANES)
        dec_sc[pl.ds(0, nsb), :] = jnp.exp(bl3[:, 0, :] + bl3[:, BLK, :])
        unroll = 4
        tn = (((0,), (0,)), ((), ()))
        nt = (((1,), (1,)), ((), ()))

        def scan(g, st):
            for u in range(unroll):
                i = g * unroll + u
                r0 = pl.multiple_of(i * SB, SB)
                st64_sc[i] = st.astype(BF16)
                upd = lax.dot_general(hi_ref[pl.ds(r0, SB), :], k2_sc[pl.ds(r0, SB), :], tn,
                                      preferred_element_type=F32)
                st = st * dec_sc[pl.ds(i, 1), :] + jnp.where(same_head, upd, 0.0)
            return st

        lax.fori_loop(0, nsb // unroll, scan, jnp.zeros((LANES, LANES), F32))

        r = lax.broadcasted_iota(jnp.int32, (2 * SB, 2 * SB), 0)
        c = lax.broadcasted_iota(jnp.int32, (2 * SB, 2 * SB), 1)
        t = r & (SB - 1)
        visible = (((c < SB) & ((t & BLK) == (c & BLK)) & ((t & (BLK - 1)) >= (c & (BLK - 1))))
                   | ((c >= SB) & (c < SB + BLK) & (t >= BLK)))
        plane = lax.broadcasted_iota(jnp.int32, (SB, LANES), 1)
        pad = jnp.zeros((BLK, LANES), BF16)

        def readout(g, _):
            for u in range(unroll):
                i = g * unroll + u
                r0 = pl.multiple_of(i * SB, SB)
                vb = hi_ref[pl.ds(r0, SB), :]
                q2 = jnp.concatenate([qa_sc[pl.ds(r0, SB), :], qb_sc[pl.ds(r0, SB), :]], axis=0)
                kext = jnp.concatenate([kh_sc[pl.ds(r0, SB), :], ke_sc[pl.ds(r0, BLK), :], pad], axis=0)
                vext = jnp.concatenate([vb, vb[:BLK], pad], axis=0)
                sc = lax.dot_general(q2, kext, nt, preferred_element_type=F32)
                sc = jnp.where(visible, sc, 0.0).astype(BF16)
                out = jnp.dot(sc, vext, preferred_element_type=F32)
                o_inter = lax.dot_general(qd_sc[pl.ds(r0, SB), :], st64_sc[i], nt, preferred_element_type=F32)
                o_sc[pl.ds(r0, SB), :] = jnp.where(plane < HEAD_DIM, out[:SB], out[SB:]) + o_inter
            return 0

        lax.fori_loop(0, nsb // unroll, readout, 0)

    @pl.when(jnp.logical_not(safe))
    def _direct():
        qt_sc, kt_sc, s_sc, a2_sc = w1_sc, w2_sc, w3_sc, b_sc
        aa = a_sc[...]
        al = jnp.broadcast_to(aa.reshape(nchunks, C, LANES)[:, C - 1:C, :], (nchunks, C, LANES)).reshape(S, LANES)
        qt_sc[...] = (qq_sc[...] * jnp.exp(aa)).astype(BF16)
        kt_sc[...] = (kk_sc[...] * jnp.exp(al - aa)).astype(BF16)
        dec_sc[...] = jnp.exp(aa.reshape(nchunks, C, LANES)[:, C - 1, :])
        a2_sc[...] = aa * LOG2E
        trow = lax.broadcasted_iota(jnp.int32, (C, LANES), 0)

        def gen(c, _):
            r0 = pl.multiple_of(c * C, C)
            ac = a2_sc[pl.ds(r0, C), :]
            qc = qq_sc[pl.ds(r0, C), :]
            kc = kk_sc[pl.ds(r0, C), :]
            half = C // 2
            for s in range(C):
                if s < half:
                    dec = jnp.exp2(jnp.where(trow >= s, ac - ac[s:s + 1, :], NEG_BIG))
                    p = qc * (kc[s:s + 1, :] * dec)
                else:
                    dec = jnp.exp2(jnp.where(trow[half:] >= s, ac[half:] - ac[s:s + 1, :], NEG_BIG))
                    p = jnp.concatenate([jnp.zeros((half, LANES), F32), qc[half:] * (kc[s:s + 1, :] * dec)],
                                        axis=0)
                p_sc[pl.ds(r0, C), s * LANES:(s + 1) * LANES] = p.astype(BF16)
            return 0

        lax.fori_loop(0, nchunks, gen, 0)

        er = lax.broadcasted_iota(jnp.int32, (C * LANES, LANES), 0)
        ec = lax.broadcasted_iota(jnp.int32, (C * LANES, LANES), 1)
        emat = (ec == ((er & (LANES - 1)) // HEAD_DIM) * C + er // LANES).astype(BF16)
        rb = 256

        def red(i, _):
            r0 = pl.multiple_of(i * rb, rb)
            s_sc[pl.ds(r0, rb), :] = jnp.dot(p_sc[pl.ds(r0, rb), :], emat,
                                             preferred_element_type=F32).astype(BF16)
            return 0

        lax.fori_loop(0, S // rb, red, 0)

        unroll = 16

        def scan(g, st):
            for u in range(unroll):
                c = g * unroll + u
                r0 = pl.multiple_of(c * C, C)
                st16_sc[c] = st.astype(BF16)
                upd = lax.dot_general(hi_ref[pl.ds(r0, C), :], kt_sc[pl.ds(r0, C), :],
                                      (((0,), (0,)), ((), ())), preferred_element_type=F32)
                st = st * dec_sc[pl.ds(c, 1), :] + jnp.where(same_head, upd, 0.0)
            return st

        lax.fori_loop(0, nchunks // unroll, scan, jnp.zeros((LANES, LANES), F32))

        def readout(g, _):
            for u in range(unroll):
                c = g * unroll + u
                r0 = pl.multiple_of(c * C, C)
                vc = hi_ref[pl.ds(r0, C), :]
                o_inter = lax.dot_general(qt_sc[pl.ds(r0, C), :], st16_sc[c],
                                          (((1,), (1,)), ((), ())), preferred_element_type=F32)
                v2 = jnp.concatenate([jnp.where(lane < HEAD_DIM, vc, jnp.zeros_like(vc)),
                                      jnp.where(lane >= HEAD_DIM, vc, jnp.zeros_like(vc))], axis=0)
                o_intra = jnp.dot(s_sc[pl.ds(r0, C), :][:, :2 * C], v2, preferred_element_type=F32)
                o_sc[pl.ds(r0, C), :] = o_inter + o_intra
            return 0

        lax.fori_loop(0, nchunks // unroll, readout, 0)

    o = o_sc[...]
    ones_head = jnp.where(same_head, 1.0 / HEAD_DIM, 0.0).astype(F32)
    ms = jnp.dot(o * o, ones_head, precision=HIGHEST, preferred_element_type=F32)
    y = o * lax.rsqrt(ms + RMS_EPS) * nw_ref[...]
    o_ref[...] = (y * _silu(hg_ref[...].astype(F32))).astype(o_ref.dtype)


def _hgrn(hq, hf, hi, hg, lb_logits, norm_w):
    B, S, W = hq.shape
    npairs = W // LANES
    nrows = lb_logits.shape[0]
    seq = pl.BlockSpec((None, S, LANES), lambda b, p: (b, 0, p))
    return pl.pallas_call(
        _hgrn_kernel,
        out_shape=jax.ShapeDtypeStruct((B, S, W), BF16),
        grid=(B, npairs),
        in_specs=[seq, seq, seq, seq,
                  pl.BlockSpec((nrows, LANES), lambda b, p: (0, p)),
                  pl.BlockSpec((1, LANES), lambda b, p: (0, p))],
        out_specs=seq,
        scratch_shapes=[pltpu.VMEM((S, LANES), F32),
                        pltpu.VMEM((S, LANES), F32),
                        pltpu.VMEM((S, LANES), F32),
                        pltpu.VMEM((S, LANES), F32),
                        pltpu.VMEM((S, LANES), F32),
                        pltpu.VMEM((S, LANES), BF16),
                        pltpu.VMEM((S, LANES), BF16),
                        pltpu.VMEM((S, LANES), BF16),
                        pltpu.VMEM((S, LANES), BF16),
                        pltpu.VMEM((S, LANES), BF16),
                        pltpu.VMEM((S, LANES), BF16),
                        pltpu.VMEM((S, HCHUNK * LANES), BF16),
                        pltpu.VMEM((S // HCHUNK, LANES, LANES), BF16),
                        pltpu.VMEM((S // HCHUNK, LANES), F32),
                        pltpu.VMEM((S // HBLOCK, LANES, LANES), BF16)],
        compiler_params=_cparams(("parallel", "parallel")),
    )(hq, hf, hi, hg, lb_logits, norm_w.reshape(1, W))


def _layer_norm(v, g, b):
    mu = jnp.mean(v, axis=-1, keepdims=True)
    d = v - mu
    var = jnp.mean(d * d, axis=-1, keepdims=True)
    return d * lax.rsqrt(var + LN_EPS) * g + b


def _bf16_bits(x):
    u = pltpu.bitcast(x, jnp.uint32)
    return (u + jnp.uint32(0x7FFF) + ((u >> 16) & jnp.uint32(1))) & jnp.uint32(0xFFFF0000)


def _store_chunks(ref, val):
    n = ref.shape[0]
    for j in range(n):
        lo = _bf16_bits(val[:, j * LANES:(j + 1) * LANES]) >> 16
        hi = _bf16_bits(val[:, (j + n) * LANES:(j + n + 1) * LANES])
        ref[j] = pltpu.bitcast(lo | hi, F32)


def _load_chunks(ref):
    words = [pltpu.bitcast(ref[j], jnp.uint32) for j in range(ref.shape[0])]
    lo = [pltpu.bitcast(w << 16, F32) for w in words]
    hi = [pltpu.bitcast(w & jnp.uint32(0xFFFF0000), F32) for w in words]
    return jnp.concatenate(lo + hi, axis=1)


def _mix_kernel(yf_ref, oh_ref, gf_ref, gh_ref, x_ref, g1_ref, sc2_ref, sh2_ref,
                wuf_ref, wuh_ref, wo_ref, lg_ref, lbias_ref, wr_ref, br_ref,
                x1_ref, h2_ref, ri_ref, rt_ref, cnt_ref, carry_sc, *, alpha, ngroups, nper):
    first = (pl.program_id(0) == 0) & (pl.program_id(1) == 0)

    @pl.when(first)
    def _():
        carry_sc[...] = jnp.zeros_like(carry_sc)

    tm = x_ref.shape[0]
    yf = jnp.dot(yf_ref[...], wuf_ref[...], preferred_element_type=F32)
    yh = jnp.dot(oh_ref[...], wuh_ref[...], preferred_element_type=F32)
    merged = _sigmoid(gf_ref[...].astype(F32)) * yf + _sigmoid(gh_ref[...].astype(F32)) * yh
    y = jnp.dot(merged.astype(BF16), wo_ref[...], preferred_element_type=F32)
    x1 = _layer_norm(alpha * x_ref[...] + g1_ref[...] * y, lg_ref[...], lbias_ref[...])
    x1_ref[...] = x1
    h2 = x1 * (1.0 + sc2_ref[...]) + sh2_ref[...]
    _store_chunks(h2_ref, h2)

    h_top = pltpu.bitcast(pltpu.bitcast(h2, jnp.uint32) & jnp.uint32(0xFFFF0000), F32)
    h_hi = h_top.astype(BF16)
    h_lo = (h2 - h_top).astype(BF16)
    logits = (jnp.dot(h_hi, wr_ref[0], preferred_element_type=F32)
              + jnp.dot(h_hi, wr_ref[1], preferred_element_type=F32)
              + jnp.dot(h_lo, wr_ref[0], preferred_element_type=F32)) + br_ref[...]
    lane = lax.broadcasted_iota(jnp.int32, (tm, LANES), 1)
    big = jnp.int32(1 << 20)

    def argmax_first(vals, mask):
        mx = jnp.max(jnp.where(mask, vals, -jnp.inf), axis=1, keepdims=True)
        idx = jnp.min(jnp.where(mask & (vals == mx), lane, big), axis=1, keepdims=True)
        return mx, idx

    gmask = lane < ngroups
    gmax = jnp.max(jnp.where(gmask, logits, -jnp.inf), axis=1, keepdims=True)
    gexp = jnp.where(gmask, jnp.exp(logits - gmax), 0.0)
    gprob = gexp / jnp.sum(gexp, axis=1, keepdims=True)
    g_w, g_idx = argmax_first(gprob, gmask)

    lo = ngroups + g_idx * nper
    emask = (lane >= lo) & (lane < lo + nper)
    emax = jnp.max(jnp.where(emask, logits, -jnp.inf), axis=1, keepdims=True)
    eexp = jnp.where(emask, jnp.exp(logits - emax), 0.0)
    eprob = eexp / jnp.sum(eexp, axis=1, keepdims=True)
    p0, i0 = argmax_first(eprob, emask)
    p1, i1 = argmax_first(eprob, emask & (lane != i0))
    den = p0 + p1
    w0 = p0 / den * g_w
    w1 = p1 / den * g_w
    e0 = i0 - ngroups
    e1 = i1 - ngroups

    oh = ((lane == e0) | (lane == e1)).astype(F32)
    r = lax.broadcasted_iota(jnp.int32, (tm, tm), 0)
    c = lax.broadcasted_iota(jnp.int32, (tm, tm), 1)
    strict_lower = (c < r).astype(BF16)
    before = jnp.dot(strict_lower, oh.astype(BF16), preferred_element_type=F32) + carry_sc[...]
    rank0 = jnp.sum(jnp.where(lane == e0, before, 0.0), axis=1, keepdims=True)
    rank1 = jnp.sum(jnp.where(lane == e1, before, 0.0), axis=1, keepdims=True)
    carry_sc[...] = carry_sc[...] + jnp.sum(oh, axis=0, keepdims=True)
    cnt_ref[...] = carry_sc[...]

    info = jnp.where(lane == 0, w0, 0.0)
    info = jnp.where(lane == 1, w1, info)
    info = jnp.where(lane == 2, e0.astype(F32), info)
    info = jnp.where(lane == 3, e1.astype(F32), info)
    info = jnp.where(lane == 4, rank0, info)
    info = jnp.where(lane == 5, rank1, info)
    ri_ref[...] = info
    rt_ref[...] = info.T[:ROW_TILE, :]


def _mix(yf, oh, gf, gh, x, g1, sc2, sh2, wuf, wuh, wo, ln_g, ln_b, wr, br, alpha, ngroups, nper, tm=512):
    B, S, D = x.shape
    W = yf.shape[2]
    tok = lambda w: pl.BlockSpec((None, tm, w), lambda b, i: (b, i, 0))
    vec = pl.BlockSpec((None, 1, D), lambda b, i: (b, 0, 0))
    full = lambda a: pl.BlockSpec(a.shape, lambda b, i: (0,) * a.ndim)
    return pl.pallas_call(
        functools.partial(_mix_kernel, alpha=alpha, ngroups=ngroups, nper=nper),
        out_shape=(jax.ShapeDtypeStruct((B, S, D), F32),
                   jax.ShapeDtypeStruct((D // WORD_LANES, B * S, LANES), F32),
                   jax.ShapeDtypeStruct((B, S, LANES), F32),
                   jax.ShapeDtypeStruct((ROW_TILE, B * S), F32),
                   jax.ShapeDtypeStruct((1, LANES), F32)),
        grid=(B, S // tm),
        in_specs=[tok(W), tok(W), tok(D), tok(D), tok(D), vec, vec, vec,
                  full(wuf), full(wuh), full(wo), full(ln_g), full(ln_b), full(wr), full(br)],
        out_specs=(tok(D),
                   pl.BlockSpec((D // WORD_LANES, tm, LANES), lambda b, i: (0, b * (S // tm) + i, 0)),
                   tok(LANES),
                   pl.BlockSpec((ROW_TILE, tm), lambda b, i: (0, b * (S // tm) + i)),
                   pl.BlockSpec((1, LANES), lambda b, i: (0, 0))),
        scratch_shapes=[pltpu.VMEM((1, LANES), F32)],
        compiler_params=_cparams(("arbitrary", "arbitrary")),
    )(yf, oh, gf, gh, x, g1, sc2, sh2, wuf, wuh, wo, ln_g, ln_b, wr, br)


def _sc_mesh():
    return plsc.VectorSubcoreMesh(core_axis_name="core", subcore_axis_name="subcore")


def _sc_pipeline(body, grid, in_specs, out_specs):
    return pltpu.emit_pipeline(body, grid=grid, in_specs=in_specs, out_specs=out_specs,
                               core_axis_name=("core", "subcore"),
                               dimension_semantics=(pltpu.PARALLEL,) * len(grid))


def _sc_scatter_rows(src, rows_a, rows_b, n_out):
    nj, t = rows_a.shape
    nc = t // LANES

    @pl.kernel(out_type=jax.ShapeDtypeStruct((n_out, LANES), src.dtype), mesh=_sc_mesh(), scratch_types=[])
    def scatter(x_hbm, a_hbm, b_hbm, o_hbm):
        def body(x_vmem, a_vmem, b_vmem):
            pltpu.sync_copy(x_vmem, o_hbm.at[a_vmem.at[0]])
            pltpu.sync_copy(x_vmem, o_hbm.at[b_vmem.at[0]])

        idx = pl.BlockSpec((1, LANES), lambda j, c: (j, c))
        _sc_pipeline(body, (nj, nc), [pl.BlockSpec((LANES, LANES), lambda j, c: (j * nc + c, 0)), idx, idx],
                     [])(x_hbm, a_hbm, b_hbm)

    return scatter(src, rows_a, rows_b)


def _sc_gather_rows(table, rows):
    nr, t = rows.shape
    nc = t // LANES

    @pl.kernel(out_type=jax.ShapeDtypeStruct((nr * t, LANES), table.dtype), mesh=_sc_mesh(), scratch_types=[])
    def gather(x_hbm, i_hbm, o_hbm):
        def body(i_vmem, o_vmem):
            pltpu.sync_copy(x_hbm.at[i_vmem.at[0]], o_vmem)

        _sc_pipeline(body, (nr, nc), [pl.BlockSpec((1, LANES), lambda r, c: (r, c))],
                     [pl.BlockSpec((LANES, LANES), lambda r, c: (r * nc + c, 0))])(i_hbm, o_hbm)

    return gather(table, rows)


def _experts_kernel(te_ref, tn_ref, x_ref, wg_ref, wu_ref, wd_ref, o_ref):
    nrows = tn_ref[pl.program_id(0)]

    @pl.when(nrows > 0)
    def _():
        x = _load_chunks(x_ref)
        x = jnp.where(lax.broadcasted_iota(jnp.int32, x.shape, 0) < nrows, x, 0.0).astype(BF16)
        g = jnp.dot(x, wg_ref[...].astype(BF16), preferred_element_type=F32)
        u = jnp.dot(x, wu_ref[...].astype(BF16), preferred_element_type=F32)
        hid = (_silu(g) * u).astype(BF16)
        _store_chunks(o_ref, jnp.dot(hid, wd_ref[...].astype(BF16), preferred_element_type=F32))

    @pl.when(nrows == 0)
    def _():
        o_ref[...] = jnp.zeros_like(o_ref)


def _experts(tile_expert, tile_rows, xs, wg, wu, wd, tm):
    E, D, FF = wg.shape
    dt = D // WORD_LANES
    ntiles = tile_expert.shape[0]
    rows = pl.BlockSpec((dt, tm, LANES), lambda i, te, tn: (0, i, 0))
    grid_spec = pltpu.PrefetchScalarGridSpec(
        num_scalar_prefetch=2,
        grid=(ntiles,),
        in_specs=[rows,
                  pl.BlockSpec((None, D, FF), lambda i, te, tn: (te[i], 0, 0)),
                  pl.BlockSpec((None, D, FF), lambda i, te, tn: (te[i], 0, 0)),
                  pl.BlockSpec((None, FF, D), lambda i, te, tn: (te[i], 0, 0))],
        out_specs=rows,
    )
    return pl.pallas_call(
        _experts_kernel,
        out_shape=jax.ShapeDtypeStruct((dt, ntiles * tm, LANES), F32),
        grid_spec=grid_spec,
        compiler_params=_cparams(("arbitrary",)),
    )(tile_expert, tile_rows, xs, wg, wu, wd)


def _combine_kernel(yg_ref, x1_ref, ri_ref, g2_ref, lg_ref, lb_ref, o_ref, *, alpha):
    ri = ri_ref[...]
    y = ri[:, 0:1] * _load_chunks(yg_ref.at[0]) + ri[:, 1:2] * _load_chunks(yg_ref.at[1])
    o_ref[...] = _layer_norm(alpha * x1_ref[...] + g2_ref[...] * y, lg_ref[...], lb_ref[...])


def _combine(yg, x1, rinfo, g2, ln_g, ln_b, alpha, tm=256):
    B, S, D = x1.shape
    nb = S // tm
    return pl.pallas_call(
        functools.partial(_combine_kernel, alpha=alpha),
        out_shape=jax.ShapeDtypeStruct((B, S, D), F32),
        grid=(B, nb),
        in_specs=[pl.BlockSpec((2, D // WORD_LANES, tm, LANES), lambda b, i: (0, 0, b * nb + i, 0)),
                  pl.BlockSpec((None, tm, D), lambda b, i: (b, i, 0)),
                  pl.BlockSpec((None, tm, LANES), lambda b, i: (b, i, 0)),
                  pl.BlockSpec((None, 1, D), lambda b, i: (b, 0, 0)),
                  pl.BlockSpec((1, D), lambda b, i: (0, 0)),
                  pl.BlockSpec((1, D), lambda b, i: (0, 0))],
        out_specs=pl.BlockSpec((None, tm, D), lambda b, i: (b, i, 0)),
        compiler_params=_cparams(("parallel", "parallel")),
    )(yg, x1, rinfo, g2, ln_g, ln_b)


def kernel(x, c, w_ada, b_ada, w_in, b_fox_forget, hgrn_lb_logits, hgrn_norm_w, w_up_fox, w_up_hgrn, w_out,
           ln1_g, ln1_b, w_router_group, b_router_group, w_router_expert, b_router_expert,
           w_expert_gate, w_expert_up, w_expert_down, ln2_g, ln2_b):
    B, S, D = x.shape
    depth = w_ada.shape[0]
    assert depth == 1, "single-layer block"
    fox_heads = b_fox_forget.shape[1]
    fox_w = fox_heads * HEAD_DIM
    hgrn_w = hgrn_norm_w.shape[1]
    ngroups = w_router_group.shape[2]
    nexp = w_router_expert.shape[2]
    nper = nexp // ngroups
    alpha = (2 * depth) ** 0.25
    T = B * S

    ada = _ada(c, w_ada[0], b_ada[0])
    sh1, sc1, g1, sh2, sc2, g2 = [a.reshape(B, 1, D) for a in jnp.split(ada, 6, axis=-1)]

    wi = w_in[0]
    o_ff = 3 * fox_w
    w_packed = jnp.concatenate(
        [wi[:, :o_ff + fox_heads], jnp.zeros((D, LANES - fox_heads), wi.dtype), wi[:, o_ff + fox_heads:]],
        axis=1).astype(BF16)
    widths = [fox_w, fox_w, fox_w, LANES, hgrn_w, hgrn_w, hgrn_w, hgrn_w, D, D]
    segs, off = [], 0
    for w in widths:
        segs.append((off, off + w))
        off += w
    fq, fk, fv, ffp, hq, hf, hi, hg, gf, gh = _inproj(x, sc1, sh1, w_packed, segs)

    bias_p = jnp.zeros((1, LANES), F32).at[0, :fox_heads].set(b_fox_forget[0])
    cum = _foxcum(ffp, bias_p)
    y_fox = _fox(fq, fk, fv, cum)

    o_h = _hgrn(hq, hf, hi, hg, hgrn_lb_logits, hgrn_norm_w[0])

    wr = jnp.zeros((D, LANES), F32).at[:, :ngroups].set(w_router_group[0]).at[:, ngroups:ngroups + nexp].set(
        w_router_expert[0])
    wr_hi = lax.bitcast_convert_type(lax.bitcast_convert_type(wr, jnp.uint32) & jnp.uint32(0xFFFF0000), F32)
    wr = jnp.stack([wr_hi.astype(BF16), (wr - wr_hi).astype(BF16)])
    br = jnp.zeros((1, LANES), F32).at[0, :ngroups].set(b_router_group[0]).at[0, ngroups:ngroups + nexp].set(
        b_router_expert[0])
    x1, h2, rinfo, fields, counts = _mix(
        y_fox, o_h, gf, gh, x, g1, sc2, sh2,
        w_up_fox[0].astype(BF16), w_up_hgrn[0].astype(BF16), w_out[0].astype(BF16),
        ln1_g[0].reshape(1, D), ln1_b[0].reshape(1, D), wr, br, alpha, ngroups, nper)

    tm_e = 256
    dt = D // WORD_LANES
    ntiles = (2 * T) // tm_e + nexp
    nslots = ntiles * tm_e
    cnt = counts[0, :nexp].astype(jnp.int32)
    padded = ((cnt + tm_e - 1) // tm_e) * tm_e
    ends = jnp.cumsum(padded)
    starts = ends - padded
    eid = fields[2:4].astype(jnp.int32)
    rank = fields[4:6].astype(jnp.int32)
    first = jnp.sum(jnp.where(eid[None] == jnp.arange(nexp, dtype=jnp.int32)[:, None, None],
                              starts[:, None, None], 0), axis=0)
    pos = first + rank
    tile_start = jnp.arange(ntiles, dtype=jnp.int32) * tm_e
    tile_expert = jnp.minimum(jnp.sum((tile_start[:, None] >= ends[None, :]).astype(jnp.int32), axis=1), nexp - 1)
    tile_rows = jnp.clip(starts[tile_expert] + cnt[tile_expert] - tile_start, 0, tm_e)
    rows = pos[:, None, :] + (jnp.arange(dt, dtype=jnp.int32) * nslots)[None, :, None]

    xs = _sc_scatter_rows(h2.reshape(dt * T, LANES), rows[0], rows[1], dt * nslots)
    ys = _experts(tile_expert, tile_rows, xs.reshape(dt, nslots, LANES),
                  w_expert_gate[0], w_expert_up[0], w_expert_down[0], tm_e)
    yg = _sc_gather_rows(ys.reshape(dt * nslots, LANES), rows.reshape(2 * dt, T))
    return _combine(yg.reshape(2, dt, T, LANES), x1, rinfo, g2,
                    ln2_g[0].reshape(1, D), ln2_b[0].reshape(1, D), alpha)
```

```python
import functools

import jax
import jax.numpy as jnp
from jax import lax
from jax.experimental import pallas as pl
from jax.experimental.pallas import tpu as pltpu
from jax.experimental.pallas import tpu_sc as plsc

F32 = jnp.float32
BF16 = jnp.bfloat16
HIGHEST = lax.Precision.HIGHEST

LANES = 128
HEAD_DIM = 64
LN_EPS = 1e-5
RMS_EPS = 1e-6
LOG2E = 1.4426950408889634
NEG_BIG = -1e30
HCHUNK = 16
HBLOCK = 64
HGRN_SAFE_EXP = 60.0
ROW_TILE = 8
WORD_LANES = 2 * LANES
VMEM_LIMIT = 56 * 1024 * 1024


def _cparams(sem, vmem=VMEM_LIMIT):
    return pltpu.CompilerParams(dimension_semantics=sem, vmem_limit_bytes=vmem)


def _sigmoid(x):
    return 1.0 / (1.0 + jnp.exp(-x))


def _silu(x):
    return x * _sigmoid(x)


def _ada_kernel(c_ref, w_ref, b_ref, o_ref):
    c = c_ref[...]
    o_ref[...] = jnp.dot(_silu(c), w_ref[...], precision=HIGHEST,
                         preferred_element_type=F32) + b_ref[...]


def _ada(c, w_ada, b_ada):
    B, D = c.shape
    N = w_ada.shape[1]
    tn = 1024
    return pl.pallas_call(
        _ada_kernel,
        out_shape=jax.ShapeDtypeStruct((B, N), F32),
        grid=(N // tn,),
        in_specs=[pl.BlockSpec((B, D), lambda j: (0, 0)),
                  pl.BlockSpec((D, tn), lambda j: (0, j)),
                  pl.BlockSpec((1, tn), lambda j: (0, j))],
        out_specs=pl.BlockSpec((B, tn), lambda j: (0, j)),
        compiler_params=_cparams(("arbitrary",)),
    )(c, w_ada, b_ada.reshape(1, N))


def _inproj_kernel(x_ref, sc_ref, sh_ref, w_ref,
                   fq_ref, fk_ref, fv_ref, ff_ref, hq_ref, hf_ref, hi_ref, hg_ref, gf_ref, gh_ref,
                   *, segs, q_scale):
    h = (x_ref[...] * (1.0 + sc_ref[...]) + sh_ref[...]).astype(BF16)
    outs = (fq_ref, fk_ref, fv_ref, ff_ref, hq_ref, hf_ref, hi_ref, hg_ref, gf_ref, gh_ref)
    for idx, (o_ref, (a, b)) in enumerate(zip(outs, segs)):
        r = jnp.dot(h, w_ref[:, a:b], preferred_element_type=F32)
        if idx == 0:
            r = r * q_scale
        o_ref[...] = r.astype(o_ref.dtype)


def _inproj(x, sc1, sh1, w_packed, segs, tm=256):
    B, S, D = x.shape
    widths = [b - a for a, b in segs]
    dtypes = [BF16, BF16, BF16, F32, BF16, F32, BF16, BF16, BF16, BF16]
    out_shape = tuple(jax.ShapeDtypeStruct((B, S, w), dt) for w, dt in zip(widths, dtypes))
    out_specs = tuple(pl.BlockSpec((None, tm, w), lambda b, i: (b, i, 0)) for w in widths)
    vec = pl.BlockSpec((None, 1, D), lambda b, i: (b, 0, 0))
    return pl.pallas_call(
        functools.partial(_inproj_kernel, segs=tuple(segs), q_scale=HEAD_DIM ** -0.5 * LOG2E),
        out_shape=out_shape,
        grid=(B, S // tm),
        in_specs=[pl.BlockSpec((None, tm, D), lambda b, i: (b, i, 0)), vec, vec,
                  pl.BlockSpec(w_packed.shape, lambda b, i: (0, 0))],
        out_specs=out_specs,
        compiler_params=_cparams(("parallel", "parallel")),
    )(x, sc1, sh1, w_packed)


def _foxcum_kernel(ff_ref, b_ref, o_ref, *, blk):
    S = ff_ref.shape[0]
    r = lax.broadcasted_iota(jnp.int32, (blk, blk), 0)
    c = lax.broadcasted_iota(jnp.int32, (blk, blk), 1)
    lower = (r >= c).astype(F32)
    carry = jnp.zeros((1, LANES), F32)
    for j in range(S // blk):
        z = ff_ref[j * blk:(j + 1) * blk, :] + b_ref[...]
        lf = jnp.minimum(z, 0.0) - jnp.log(1.0 + jnp.exp(-jnp.abs(z)))
        cum = jnp.dot(lower, lf, precision=HIGHEST, preferred_element_type=F32) + carry
        o_ref[j * blk:(j + 1) * blk, :] = cum * LOG2E
        carry = cum[blk - 1:blk, :]


def _foxcum(ffp, bias_p, blk=256):
    B, S, _ = ffp.shape
    return pl.pallas_call(
        functools.partial(_foxcum_kernel, blk=blk),
        out_shape=jax.ShapeDtypeStruct((B, S, LANES), F32),
        grid=(B,),
        in_specs=[pl.BlockSpec((None, S, LANES), lambda b: (b, 0, 0)),
                  pl.BlockSpec((1, LANES), lambda b: (0, 0))],
        out_specs=pl.BlockSpec((None, S, LANES), lambda b: (b, 0, 0)),
        compiler_params=_cparams(("parallel",)),
    )(ffp, bias_p)


NCUM = 3


def _fox_kernel(q_ref, k_ref, v_ref, c_ref, o_ref, ka_sc, kb_sc, va_sc, vb_sc, *, tq, tk):
    p = pl.program_id(1)
    qi = pl.program_id(2)
    S = k_ref.shape[0]

    @pl.when(qi == 0)
    def _():
        lane = lax.broadcasted_iota(jnp.int32, (S, LANES), 1)
        rr = lax.broadcasted_iota(jnp.int32, (LANES, LANES), 0)
        cc = lax.broadcasted_iota(jnp.int32, (LANES, LANES), 1)
        rest = c_ref[...]
        placed = jnp.zeros((S, LANES), F32)
        for i in range(NCUM):
            piece = rest.astype(BF16)
            rest = rest - piece.astype(F32)
            sel = ((rr == 2 * p) & (cc == HEAD_DIM + i)) | ((rr == 2 * p + 1) & (cc == i))
            placed = placed + jnp.dot(piece, jnp.where(sel, 1.0, 0.0).astype(BF16), preferred_element_type=F32)
        k2 = k_ref[...].astype(F32)
        ka_sc[...] = jnp.where(lane < HEAD_DIM, k2, -placed).astype(BF16)
        kb_sc[...] = jnp.where(lane >= HEAD_DIM, k2, -placed).astype(BF16)
        vt = v_ref[...].astype(F32).T
        row = lax.broadcasted_iota(jnp.int32, (LANES, S), 0)
        va_sc[...] = jnp.where(row < HEAD_DIM, vt, jnp.where(row == HEAD_DIM, 1.0, 0.0)).astype(BF16)
        vb_sc[...] = jnp.where(row >= HEAD_DIM, vt, jnp.where(row == 0, 1.0, 0.0)).astype(BF16)

    q2 = q_ref[...].astype(F32)
    qlane = lax.broadcasted_iota(jnp.int32, (tq, LANES), 1)
    qa = jnp.where(qlane < HEAD_DIM, q2, jnp.where(qlane < HEAD_DIM + NCUM, 1.0, 0.0)).astype(BF16)
    qb = jnp.where(qlane >= HEAD_DIM, q2, jnp.where(qlane < NCUM, 1.0, 0.0)).astype(BF16)
    nsub = tq // tk

    def block(k0, carry, diag_off):
        q0 = 0 if diag_off is None else diag_off
        out = []
        for ksc, vsc, qh, (m, acc) in ((ka_sc, va_sc, qa, carry[:2]), (kb_sc, vb_sc, qb, carry[2:])):
            st = lax.dot_general(ksc[pl.ds(k0, tk), :], qh[q0:, :], (((1,), (1,)), ((), ())),
                                 preferred_element_type=F32)
            if diag_off is not None:
                st = jnp.where(lax.broadcasted_iota(jnp.int32, st.shape, 0)
                               <= lax.broadcasted_iota(jnp.int32, st.shape, 1), st, NEG_BIG)
            m_old = m[:, q0:]
            m_new = jnp.maximum(m_old, jnp.max(st, axis=0, keepdims=True))
            pt = jnp.exp2(st - m_new).astype(BF16)
            acc_new = (jnp.exp2(m_old - m_new) * acc[:, q0:]
                       + jnp.dot(vsc[:, pl.ds(k0, tk)], pt, preferred_element_type=F32))
            if q0:
                m_new = jnp.concatenate([m[:, :q0], m_new], axis=1)
                acc_new = jnp.concatenate([acc[:, :q0], acc_new], axis=1)
            out += [m_new, acc_new]
        return tuple(out)

    def group(j, carry):
        k0 = pl.multiple_of(j * (nsub * tk), nsub * tk)
        for u in range(nsub):
            carry = block(k0 + u * tk, carry, None)
        return carry

    m0 = jnp.full((1, tq), NEG_BIG, F32)
    a0 = jnp.zeros((LANES, tq), F32)
    carry = lax.fori_loop(0, qi, group, (m0, a0, m0, a0))
    for d in range(nsub):
        carry = block(pl.multiple_of(qi * tq + d * tk, tk), carry, d * tk)
    _, aa, _, ab = carry
    row = lax.broadcasted_iota(jnp.int32, (LANES, tq), 0)
    ot = jnp.where(row < HEAD_DIM, aa * (1.0 / aa[HEAD_DIM:HEAD_DIM + 1, :]), ab * (1.0 / ab[0:1, :]))
    o_ref[...] = ot.T.astype(o_ref.dtype)


def _fox(fq, fk, fv, cum, tq=1024, tk=256):
    B, S, W = fq.shape
    assert tq % (2 * tk) == 0 and S % tq == 0
    npairs = W // LANES
    return pl.pallas_call(
        functools.partial(_fox_kernel, tq=tq, tk=tk),
        out_shape=jax.ShapeDtypeStruct((B, S, W), BF16),
        grid=(B, npairs, S // tq),
        in_specs=[pl.BlockSpec((None, tq, LANES), lambda b, p, i: (b, i, p)),
                  pl.BlockSpec((None, S, LANES), lambda b, p, i: (b, 0, p)),
                  pl.BlockSpec((None, S, LANES), lambda b, p, i: (b, 0, p)),
                  pl.BlockSpec((None, S, LANES), lambda b, p, i: (b, 0, 0))],
        out_specs=pl.BlockSpec((None, tq, LANES), lambda b, p, i: (b, i, p)),
        scratch_shapes=[pltpu.VMEM((S, LANES), BF16), pltpu.VMEM((S, LANES), BF16),
                        pltpu.VMEM((LANES, S), BF16), pltpu.VMEM((LANES, S), BF16)],
        compiler_params=_cparams(("parallel", "parallel", "arbitrary")),
    )(fq, fk, fv, cum)


def _hgrn_kernel(hq_ref, hf_ref, hi_ref, hg_ref, lb_ref, nw_ref, o_ref,
                 a_sc, b_sc, kk_sc, qq_sc, o_sc, w1_sc, w2_sc, w3_sc, w4_sc, w5_sc, w6_sc,
                 p_sc, st16_sc, dec_sc, st64_sc):
    S = hq_ref.shape[0]
    C = HCHUNK
    nchunks = S // C
    BLK = HBLOCK
    nblk = S // BLK

    lg = lb_ref[...]
    e = jnp.exp(lg - jnp.max(lg, axis=0, keepdims=True))
    lb = e[0:1, :] / jnp.sum(e, axis=0, keepdims=True)

    f = lb + (1.0 - lb) * _sigmoid(hf_ref[...])
    lf = jnp.log(f)
    kk_sc[...] = 1.0 - f
    qq_sc[...] = _silu(hq_ref[...].astype(F32))

    row = lax.broadcasted_iota(jnp.int32, (S, LANES), 0)
    rmod = row & (C - 1)
    a = lf
    d = 1
    while d < C:
        a = a + jnp.where(rmod >= d, pltpu.roll(a, d, axis=0), 0.0)
        d *= 2
    a3 = a.reshape(nchunks, C, LANES)
    alast = jnp.broadcast_to(a3[:, C - 1:C, :], (nchunks, C, LANES)).reshape(S, LANES)
    bmod = row & (BLK - 1)
    tot = alast
    d = C
    while d < BLK:
        tot = tot + jnp.where(bmod >= d, pltpu.roll(tot, d, axis=0), 0.0)
        d *= 2
    b = a + (tot - alast)
    b3 = b.reshape(nblk, BLK, LANES)
    blast = jnp.broadcast_to(b3[:, BLK - 1:BLK, :], (nblk, BLK, LANES)).reshape(S, LANES)
    a_sc[...] = a
    b_sc[...] = b
    safe = jnp.max(-blast) <= HGRN_SAFE_EXP

    lane = lax.broadcasted_iota(jnp.int32, (C, LANES), 1)
    sr = lax.broadcasted_iota(jnp.int32, (LANES, LANES), 0)
    scn = lax.broadcasted_iota(jnp.int32, (LANES, LANES), 1)
    same_head = (sr // HEAD_DIM) == (scn // HEAD_DIM)

    @pl.when(safe)
    def _factorised():
        qa_sc, qb_sc, kh_sc, ke_sc, qd_sc, k2_sc = w1_sc, w2_sc, w3_sc, w4_sc, w5_sc, w6_sc
        SB = 2 * BLK
        nsb = S // SB
        bb = b_sc[...]
        bl = jnp.broadcast_to(bb.reshape(nblk, BLK, LANES)[:, BLK - 1:BLK, :], (nblk, BLK, LANES)).reshape(S, LANES)
        second = (row & BLK) != 0
        d_prev = jnp.exp(pltpu.roll(bl, BLK, axis=0))
        d_next = jnp.exp(pltpu.roll(bl, S - BLK, axis=0))
        qh = qq_sc[...] * jnp.exp(bb)
        slane = lax.broadcasted_iota(jnp.int32, (S, LANES), 1)
        qa_sc[...] = jnp.where(slane < HEAD_DIM, qh, 0.0).astype(BF16)
        qb_sc[...] = jnp.where(slane >= HEAD_DIM, qh, 0.0).astype(BF16)
        qd_sc[...] = (qh * jnp.where(second, d_prev, 1.0)).astype(BF16)
        kh_sc[...] = (kk_sc[...] * jnp.exp(-bb)).astype(BF16)
        ke = kk_sc[...] * jnp.exp(bl - bb)
        ke_sc[...] = ke.astype(BF16)
        k2_sc[...] = (ke * jnp.where(second, 1.0, d_next)).astype(BF16)
        bl3 = bl.reshape(nsb, SB, LANES)
        dec_sc[pl.ds(0, nsb), :] = jnp.exp(bl3[:, 0, :] + bl3[:, BLK, :])
        unroll = 4
        tn = (((0,), (0,)), ((), ()))
        nt = (((1,), (1,)), ((), ()))

        def scan(g, st):
            for u in range(unroll):
                i = g * unroll + u
                r0 = pl.multiple_of(i * SB, SB)
                st64_sc[i] = st.astype(BF16)
                upd = lax.dot_general(hi_ref[pl.ds(r0, SB), :], k2_sc[pl.ds(r0, SB), :], tn,
                                      preferred_element_type=F32)
                st = st * dec_sc[pl.ds(i, 1), :] + jnp.where(same_head, upd, 0.0)
            return st

        lax.fori_loop(0, nsb // unroll, scan, jnp.zeros((LANES, LANES), F32))

        r = lax.broadcasted_iota(jnp.int32, (2 * SB, 2 * SB), 0)
        c = lax.broadcasted_iota(jnp.int32, (2 * SB, 2 * SB), 1)
        t = r & (SB - 1)
        visible = (((c < SB) & ((t & BLK) == (c & BLK)) & ((t & (BLK - 1)) >= (c & (BLK - 1))))
                   | ((c >= SB) & (c < SB + BLK) & (t >= BLK)))
        plane = lax.broadcasted_iota(jnp.int32, (SB, LANES), 1)
        pad = jnp.zeros((BLK, LANES), BF16)

        def readout(g, _):
            for u in range(unroll):
                i = g * unroll + u
                r0 = pl.multiple_of(i * SB, SB)
                vb = hi_ref[pl.ds(r0, SB), :]
                q2 = jnp.concatenate([qa_sc[pl.ds(r0, SB), :], qb_sc[pl.ds(r0, SB), :]], axis=0)
                kext = jnp.concatenate([kh_sc[pl.ds(r0, SB), :], ke_sc[pl.ds(r0, BLK), :], pad], axis=0)
                vext = jnp.concatenate([vb, vb[:BLK], pad], axis=0)
                sc = lax.dot_general(q2, kext, nt, preferred_element_type=F32)
                sc = jnp.where(visible, sc, 0.0).astype(BF16)
                out = jnp.dot(sc, vext, preferred_element_type=F32)
                o_inter = lax.dot_general(qd_sc[pl.ds(r0, SB), :], st64_sc[i], nt, preferred_element_type=F32)
                o_sc[pl.ds(r0, SB), :] = jnp.where(plane < HEAD_DIM, out[:SB], out[SB:]) + o_inter
            return 0

        lax.fori_loop(0, nsb // unroll, readout, 0)

    @pl.when(jnp.logical_not(safe))
    def _direct():
        qt_sc, kt_sc, s_sc, a2_sc = w1_sc, w2_sc, w3_sc, b_sc
        aa = a_sc[...]
        al = jnp.broadcast_to(aa.reshape(nchunks, C, LANES)[:, C - 1:C, :], (nchunks, C, LANES)).reshape(S, LANES)
        qt_sc[...] = (qq_sc[...] * jnp.exp(aa)).astype(BF16)
        kt_sc[...] = (kk_sc[...] * jnp.exp(al - aa)).astype(BF16)
        dec_sc[...] = jnp.exp(aa.reshape(nchunks, C, LANES)[:, C - 1, :])
        a2_sc[...] = aa * LOG2E
        trow = lax.broadcasted_iota(jnp.int32, (C, LANES), 0)

        def gen(c, _):
            r0 = pl.multiple_of(c * C, C)
            ac = a2_sc[pl.ds(r0, C), :]
            qc = qq_sc[pl.ds(r0, C), :]
            kc = kk_sc[pl.ds(r0, C), :]
            half = C // 2
            for s in range(C):
                if s < half:
                    dec = jnp.exp2(jnp.where(trow >= s, ac - ac[s:s + 1, :], NEG_BIG))
                    p = qc * (kc[s:s + 1, :] * dec)
                else:
                    dec = jnp.exp2(jnp.where(trow[half:] >= s, ac[half:] - ac[s:s + 1, :], NEG_BIG))
                    p = jnp.concatenate([jnp.zeros((half, LANES), F32), qc[half:] * (kc[s:s + 1, :] * dec)],
                                        axis=0)
                p_sc[pl.ds(r0, C), s * LANES:(s + 1) * LANES] = p.astype(BF16)
            return 0

        lax.fori_loop(0, nchunks, gen, 0)

        er = lax.broadcasted_iota(jnp.int32, (C * LANES, LANES), 0)
        ec = lax.broadcasted_iota(jnp.int32, (C * LANES, LANES), 1)
        emat = (ec == ((er & (LANES - 1)) // HEAD_DIM) * C + er // LANES).astype(BF16)
        rb = 256

        def red(i, _):
            r0 = pl.multiple_of(i * rb, rb)
            s_sc[pl.ds(r0, rb), :] = jnp.dot(p_sc[pl.ds(r0, rb), :], emat,
                                             preferred_element_type=F32).astype(BF16)
            return 0

        lax.fori_loop(0, S // rb, red, 0)

        unroll = 16

        def scan(g, st):
            for u in range(unroll):
                c = g * unroll + u
                r0 = pl.multiple_of(c * C, C)
                st16_sc[c] = st.astype(BF16)
                upd = lax.dot_general(hi_ref[pl.ds(r0, C), :], kt_sc[pl.ds(r0, C), :],
                                      (((0,), (0,)), ((), ())), preferred_element_type=F32)
                st = st * dec_sc[pl.ds(c, 1), :] + jnp.where(same_head, upd, 0.0)
            return st

        lax.fori_loop(0, nchunks // unroll, scan, jnp.zeros((LANES, LANES), F32))

        def readout(g, _):
            for u in range(unroll):
                c = g * unroll + u
                r0 = pl.multiple_of(c * C, C)
                vc = hi_ref[pl.ds(r0, C), :]
                o_inter = lax.dot_general(qt_sc[pl.ds(r0, C), :], st16_sc[c],
                                          (((1,), (1,)), ((), ())), preferred_element_type=F32)
                v2 = jnp.concatenate([jnp.where(lane < HEAD_DIM, vc, jnp.zeros_like(vc)),
                                      jnp.where(lane >= HEAD_DIM, vc, jnp.zeros_like(vc))], axis=0)
                o_intra = jnp.dot(s_sc[pl.ds(r0, C), :][:, :2 * C], v2, preferred_element_type=F32)
                o_sc[pl.ds(r0, C), :] = o_inter + o_intra
            return 0

        lax.fori_loop(0, nchunks // unroll, readout, 0)

    o = o_sc[...]
    ones_head = jnp.where(same_head, 1.0 / HEAD_DIM, 0.0).astype(F32)
    ms = jnp.dot(o * o, ones_head, precision=HIGHEST, preferred_element_type=F32)
    y = o * lax.rsqrt(ms + RMS_EPS) * nw_ref[...]
    o_ref[...] = (y * _silu(hg_ref[...].astype(F32))).astype(o_ref.dtype)


def _hgrn(hq, hf, hi, hg, lb_logits, norm_w):
    B, S, W = hq.shape
    npairs = W // LANES
    nrows = lb_logits.shape[0]
    seq = pl.BlockSpec((None, S, LANES), lambda b, p: (b, 0, p))
    return pl.pallas_call(
        _hgrn_kernel,
        out_shape=jax.ShapeDtypeStruct((B, S, W), BF16),
        grid=(B, npairs),
        in_specs=[seq, seq, seq, seq,
                  pl.BlockSpec((nrows, LANES), lambda b, p: (0, p)),
                  pl.BlockSpec((1, LANES), lambda b, p: (0, p))],
        out_specs=seq,
        scratch_shapes=[pltpu.VMEM((S, LANES), F32),
                        pltpu.VMEM((S, LANES), F32),
                        pltpu.VMEM((S, LANES), F32),
                        pltpu.VMEM((S, LANES), F32),
                        pltpu.VMEM((S, LANES), F32),
                        pltpu.VMEM((S, LANES), BF16),
                        pltpu.VMEM((S, LANES), BF16),
                        pltpu.VMEM((S, LANES), BF16),
                        pltpu.VMEM((S, LANES), BF16),
                        pltpu.VMEM((S, LANES), BF16),
                        pltpu.VMEM((S, LANES), BF16),
                        pltpu.VMEM((S, HCHUNK * LANES), BF16),
                        pltpu.VMEM((S // HCHUNK, LANES, LANES), BF16),
                        pltpu.VMEM((S // HCHUNK, LANES), F32),
                        pltpu.VMEM((S // HBLOCK, LANES, LANES), BF16)],
        compiler_params=_cparams(("parallel", "parallel")),
    )(hq, hf, hi, hg, lb_logits, norm_w.reshape(1, W))


def _layer_norm(v, g, b):
    mu = jnp.mean(v, axis=-1, keepdims=True)
    d = v - mu
    var = jnp.mean(d * d, axis=-1, keepdims=True)
    return d * lax.rsqrt(var + LN_EPS) * g + b


def _bf16_bits(x):
    u = pltpu.bitcast(x, jnp.uint32)
    return (u + jnp.uint32(0x7FFF) + ((u >> 16) & jnp.uint32(1))) & jnp.uint32(0xFFFF0000)


def _store_chunks(ref, val):
    n = ref.shape[0]
    for j in range(n):
        lo = _bf16_bits(val[:, j * LANES:(j + 1) * LANES]) >> 16
        hi = _bf16_bits(val[:, (j + n) * LANES:(j + n + 1) * LANES])
        ref[j] = pltpu.bitcast(lo | hi, F32)


def _load_chunks(ref):
    words = [pltpu.bitcast(ref[j], jnp.uint32) for j in range(ref.shape[0])]
    lo = [pltpu.bitcast(w << 16, F32) for w in words]
    hi = [pltpu.bitcast(w & jnp.uint32(0xFFFF0000), F32) for w in words]
    return jnp.concatenate(lo + hi, axis=1)


def _mix_kernel(yf_ref, oh_ref, gf_ref, gh_ref, x_ref, g1_ref, sc2_ref, sh2_ref,
                wuf_ref, wuh_ref, wo_ref, lg_ref, lbias_ref, wr_ref, br_ref,
                x1_ref, h2_ref, ri_ref, rt_ref, cnt_ref, carry_sc, *, alpha, ngroups, nper):
    first = (pl.program_id(0) == 0) & (pl.program_id(1) == 0)

    @pl.when(first)
    def _():
        carry_sc[...] = jnp.zeros_like(carry_sc)

    tm = x_ref.shape[0]
    yf = jnp.dot(yf_ref[...], wuf_ref[...], preferred_element_type=F32)
    yh = jnp.dot(oh_ref[...], wuh_ref[...], preferred_element_type=F32)
    merged = _sigmoid(gf_ref[...].astype(F32)) * yf + _sigmoid(gh_ref[...].astype(F32)) * yh
    y = jnp.dot(merged.astype(BF16), wo_ref[...], preferred_element_type=F32)
    x1 = _layer_norm(alpha * x_ref[...] + g1_ref[...] * y, lg_ref[...], lbias_ref[...])
    x1_ref[...] = x1
    h2 = x1 * (1.0 + sc2_ref[...]) + sh2_ref[...]
    _store_chunks(h2_ref, h2)

    h_top = pltpu.bitcast(pltpu.bitcast(h2, jnp.uint32) & jnp.uint32(0xFFFF0000), F32)
    h_hi = h_top.astype(BF16)
    h_lo = (h2 - h_top).astype(BF16)
    logits = (jnp.dot(h_hi, wr_ref[0], preferred_element_type=F32)
              + jnp.dot(h_hi, wr_ref[1], preferred_element_type=F32)
              + jnp.dot(h_lo, wr_ref[0], preferred_element_type=F32)) + br_ref[...]
    lane = lax.broadcasted_iota(jnp.int32, (tm, LANES), 1)
    big = jnp.int32(1 << 20)

    def argmax_first(vals, mask):
        mx = jnp.max(jnp.where(mask, vals, -jnp.inf), axis=1, keepdims=True)
        idx = jnp.min(jnp.where(mask & (vals == mx), lane, big), axis=1, keepdims=True)
        return mx, idx

    gmask = lane < ngroups
    gmax = jnp.max(jnp.where(gmask, logits, -jnp.inf), axis=1, keepdims=True)
    gexp = jnp.where(gmask, jnp.exp(logits - gmax), 0.0)
    gprob = gexp / jnp.sum(gexp, axis=1, keepdims=True)
    g_w, g_idx = argmax_first(gprob, gmask)

    lo = ngroups + g_idx * nper
    emask = (lane >= lo) & (lane < lo + nper)
    emax = jnp.max(jnp.where(emask, logits, -jnp.inf), axis=1, keepdims=True)
    eexp = jnp.where(emask, jnp.exp(logits - emax), 0.0)
    eprob = eexp / jnp.sum(eexp, axis=1, keepdims=True)
    p0, i0 = argmax_first(eprob, emask)
    p1, i1 = argmax_first(eprob, emask & (lane != i0))
    den = p0 + p1
    w0 = p0 / den * g_w
    w1 = p1 / den * g_w
    e0 = i0 - ngroups
    e1 = i1 - ngroups

    oh = ((lane == e0) | (lane == e1)).astype(F32)
    r = lax.broadcasted_iota(jnp.int32, (tm, tm), 0)
    c = lax.broadcasted_iota(jnp.int32, (tm, tm), 1)
    strict_lower = (c < r).astype(BF16)
    before = jnp.dot(strict_lower, oh.astype(BF16), preferred_element_type=F32) + carry_sc[...]
    rank0 = jnp.sum(jnp.where(lane == e0, before, 0.0), axis=1, keepdims=True)
    rank1 = jnp.sum(jnp.where(lane == e1, before, 0.0), axis=1, keepdims=True)
    carry_sc[...] = carry_sc[...] + jnp.sum(oh, axis=0, keepdims=True)
    cnt_ref[...] = carry_sc[...]

    info = jnp.where(lane == 0, w0, 0.0)
    info = jnp.where(lane == 1, w1, info)
    info = jnp.where(lane == 2, e0.astype(F32), info)
    info = jnp.where(lane == 3, e1.astype(F32), info)
    info = jnp.where(lane == 4, rank0, info)
    info = jnp.where(lane == 5, rank1, info)
    ri_ref[...] = info
    rt_ref[...] = info.T[:ROW_TILE, :]


def _mix(yf, oh, gf, gh, x, g1, sc2, sh2, wuf, wuh, wo, ln_g, ln_b, wr, br, alpha, ngroups, nper, tm=512):
    B, S, D = x.shape
    W = yf.shape[2]
    tok = lambda w: pl.BlockSpec((None, tm, w), lambda b, i: (b, i, 0))
    vec = pl.BlockSpec((None, 1, D), lambda b, i: (b, 0, 0))
    full = lambda a: pl.BlockSpec(a.shape, lambda b, i: (0,) * a.ndim)
    return pl.pallas_call(
        functools.partial(_mix_kernel, alpha=alpha, ngroups=ngroups, nper=nper),
        out_shape=(jax.ShapeDtypeStruct((B, S, D), F32),
                   jax.ShapeDtypeStruct((D // WORD_LANES, B * S, LANES), F32),
                   jax.ShapeDtypeStruct((B, S, LANES), F32),
                   jax.ShapeDtypeStruct((ROW_TILE, B * S), F32),
                   jax.ShapeDtypeStruct((1, LANES), F32)),
        grid=(B, S // tm),
        in_specs=[tok(W), tok(W), tok(D), tok(D), tok(D), vec, vec, vec,
                  full(wuf), full(wuh), full(wo), full(ln_g), full(ln_b), full(wr), full(br)],
        out_specs=(tok(D),
                   pl.BlockSpec((D // WORD_LANES, tm, LANES), lambda b, i: (0, b * (S // tm) + i, 0)),
                   tok(LANES),
                   pl.BlockSpec((ROW_TILE, tm), lambda b, i: (0, b * (S // tm) + i)),
                   pl.BlockSpec((1, LANES), lambda b, i: (0, 0))),
        scratch_shapes=[pltpu.VMEM((1, LANES), F32)],
        compiler_params=_cparams(("arbitrary", "arbitrary")),
    )(yf, oh, gf, gh, x, g1, sc2, sh2, wuf, wuh, wo, ln_g, ln_b, wr, br)


def _sc_mesh():
    return plsc.VectorSubcoreMesh(core_axis_name="core", subcore_axis_name="subcore")


def _sc_pipeline(body, grid, in_specs, out_specs):
    return pltpu.emit_pipeline(body, grid=grid, in_specs=in_specs, out_specs=out_specs,
                               core_axis_name=("core", "subcore"),
                               dimension_semantics=(pltpu.PARALLEL,) * len(grid))


def _sc_scatter_rows(src, rows_a, rows_b, n_out):
    nj, t = rows_a.shape
    nc = t // LANES

    @pl.kernel(out_type=jax.ShapeDtypeStruct((n_out, LANES), src.dtype), mesh=_sc_mesh(), scratch_types=[])
    def scatter(x_hbm, a_hbm, b_hbm, o_hbm):
        def body(x_vmem, a_vmem, b_vmem):
            pltpu.sync_copy(x_vmem, o_hbm.at[a_vmem.at[0]])
            pltpu.sync_copy(x_vmem, o_hbm.at[b_vmem.at[0]])

        idx = pl.BlockSpec((1, LANES), lambda j, c: (j, c))
        _sc_pipeline(body, (nj, nc), [pl.BlockSpec((LANES, LANES), lambda j, c: (j * nc + c, 0)), idx, idx],
                     [])(x_hbm, a_hbm, b_hbm)

    return scatter(src, rows_a, rows_b)


def _sc_gather_rows(table, rows):
    nr, t = rows.shape
    nc = t // LANES

    @pl.kernel(out_type=jax.ShapeDtypeStruct((nr * t, LANES), table.dtype), mesh=_sc_mesh(), scratch_types=[])
    def gather(x_hbm, i_hbm, o_hbm):
        def body(i_vmem, o_vmem):
            pltpu.sync_copy(x_hbm.at[i_vmem.at[0]], o_vmem)

        _sc_pipeline(body, (nr, nc), [pl.BlockSpec((1, LANES), lambda r, c: (r, c))],
                     [pl.BlockSpec((LANES, LANES), lambda r, c: (r * nc + c, 0))])(i_hbm, o_hbm)

    return gather(table, rows)


def _experts_kernel(te_ref, tn_ref, x_ref, wg_ref, wu_ref, wd_ref, o_ref, wgb_sc, wub_sc, wdb_sc):
    i = pl.program_id(0)
    nrows = tn_ref[i]

    @pl.when((nrows > 0) & ((i == 0) | (te_ref[i] != te_ref[jnp.maximum(i - 1, 0)])))
    def _():
        wgb_sc[...] = wg_ref[...].astype(BF16)
        wub_sc[...] = wu_ref[...].astype(BF16)
        wdb_sc[...] = wd_ref[...].astype(BF16)

    @pl.when(nrows > 0)
    def _():
        x = _load_chunks(x_ref)
        x = jnp.where(lax.broadcasted_iota(jnp.int32, x.shape, 0) < nrows, x, 0.0).astype(BF16)
        g = jnp.dot(x, wgb_sc[...], preferred_element_type=F32)
        u = jnp.dot(x, wub_sc[...], preferred_element_type=F32)
        hid = (_silu(g) * u).astype(BF16)
        _store_chunks(o_ref, jnp.dot(hid, wdb_sc[...], preferred_element_type=F32))

    @pl.when(nrows == 0)
    def _():
        o_ref[...] = jnp.zeros_like(o_ref)


def _experts(tile_expert, tile_rows, xs, wg, wu, wd, tm):
    E, D, FF = wg.shape
    dt = D // WORD_LANES
    ntiles = tile_expert.shape[0]
    rows = pl.BlockSpec((dt, tm, LANES), lambda i, te, tn: (0, i, 0))
    grid_spec = pltpu.PrefetchScalarGridSpec(
        num_scalar_prefetch=2,
        grid=(ntiles,),
        in_specs=[rows,
                  pl.BlockSpec((None, D, FF), lambda i, te, tn: (te[i], 0, 0)),
                  pl.BlockSpec((None, D, FF), lambda i, te, tn: (te[i], 0, 0)),
                  pl.BlockSpec((None, FF, D), lambda i, te, tn: (te[i], 0, 0))],
        out_specs=rows,
        scratch_shapes=[pltpu.VMEM((D, FF), BF16), pltpu.VMEM((D, FF), BF16), pltpu.VMEM((FF, D), BF16)],
    )
    return pl.pallas_call(
        _experts_kernel,
        out_shape=jax.ShapeDtypeStruct((dt, ntiles * tm, LANES), F32),
        grid_spec=grid_spec,
        compiler_params=_cparams(("arbitrary",)),
    )(tile_expert, tile_rows, xs, wg, wu, wd)


def _combine_kernel(yg_ref, x1_ref, ri_ref, g2_ref, lg_ref, lb_ref, o_ref, *, alpha):
    ri = ri_ref[...]
    y = ri[:, 0:1] * _load_chunks(yg_ref.at[0]) + ri[:, 1:2] * _load_chunks(yg_ref.at[1])
    o_ref[...] = _layer_norm(alpha * x1_ref[...] + g2_ref[...] * y, lg_ref[...], lb_ref[...])


def _combine_kernel_into(prev_ref, *refs, alpha):
    del prev_ref
    _combine_kernel(*refs, alpha=alpha)


def _combine(yg, x1, rinfo, g2, ln_g, ln_b, alpha, b0, nbatch, prev=None, tm=256):
    B, S, D = x1.shape
    nb = S // tm
    in_specs = [pl.BlockSpec((2, D // WORD_LANES, tm, LANES), lambda b, i: (0, 0, b * nb + i, 0)),
                pl.BlockSpec((None, tm, D), lambda b, i: (b0 + b, i, 0)),
                pl.BlockSpec((None, tm, LANES), lambda b, i: (b0 + b, i, 0)),
                pl.BlockSpec((None, 1, D), lambda b, i: (b0 + b, 0, 0)),
                pl.BlockSpec((1, D), lambda b, i: (0, 0)),
                pl.BlockSpec((1, D), lambda b, i: (0, 0))]
    args = (yg, x1, rinfo, g2, ln_g, ln_b)
    body = functools.partial(_combine_kernel, alpha=alpha)
    aliases = {}
    if prev is not None:
        in_specs = [pl.BlockSpec(memory_space=pl.ANY)] + in_specs
        args = (prev,) + args
        body = functools.partial(_combine_kernel_into, alpha=alpha)
        aliases = {0: 0}
    return pl.pallas_call(
        body,
        out_shape=jax.ShapeDtypeStruct((B, S, D), F32),
        grid=(nbatch, nb),
        in_specs=in_specs,
        out_specs=pl.BlockSpec((None, tm, D), lambda b, i: (b0 + b, i, 0)),
        input_output_aliases=aliases,
        compiler_params=_cparams(("parallel", "parallel")),
    )(*args)


def kernel(x, c, w_ada, b_ada, w_in, b_fox_forget, hgrn_lb_logits, hgrn_norm_w, w_up_fox, w_up_hgrn, w_out,
           ln1_g, ln1_b, w_router_group, b_router_group, w_router_expert, b_router_expert,
           w_expert_gate, w_expert_up, w_expert_down, ln2_g, ln2_b):
    B, S, D = x.shape
    depth = w_ada.shape[0]
    assert depth == 1, "single-layer block"
    fox_heads = b_fox_forget.shape[1]
    fox_w = fox_heads * HEAD_DIM
    hgrn_w = hgrn_norm_w.shape[1]
    ngroups = w_router_group.shape[2]
    nexp = w_router_expert.shape[2]
    nper = nexp // ngroups
    alpha = (2 * depth) ** 0.25
    T = B * S

    ada = _ada(c, w_ada[0], b_ada[0])
    sh1, sc1, g1, sh2, sc2, g2 = [a.reshape(B, 1, D) for a in jnp.split(ada, 6, axis=-1)]

    wi = w_in[0]
    o_ff = 3 * fox_w
    w_packed = jnp.concatenate(
        [wi[:, :o_ff + fox_heads], jnp.zeros((D, LANES - fox_heads), wi.dtype), wi[:, o_ff + fox_heads:]],
        axis=1).astype(BF16)
    widths = [fox_w, fox_w, fox_w, LANES, hgrn_w, hgrn_w, hgrn_w, hgrn_w, D, D]
    segs, off = [], 0
    for w in widths:
        segs.append((off, off + w))
        off += w
    fq, fk, fv, ffp, hq, hf, hi, hg, gf, gh = _inproj(x, sc1, sh1, w_packed, segs)

    bias_p = jnp.zeros((1, LANES), F32).at[0, :fox_heads].set(b_fox_forget[0])
    cum = _foxcum(ffp, bias_p)
    y_fox = _fox(fq, fk, fv, cum)

    o_h = _hgrn(hq, hf, hi, hg, hgrn_lb_logits, hgrn_norm_w[0])

    wr = jnp.zeros((D, LANES), F32).at[:, :ngroups].set(w_router_group[0]).at[:, ngroups:ngroups + nexp].set(
        w_router_expert[0])
    wr_hi = lax.bitcast_convert_type(lax.bitcast_convert_type(wr, jnp.uint32) & jnp.uint32(0xFFFF0000), F32)
    wr = jnp.stack([wr_hi.astype(BF16), (wr - wr_hi).astype(BF16)])
    br = jnp.zeros((1, LANES), F32).at[0, :ngroups].set(b_router_group[0]).at[0, ngroups:ngroups + nexp].set(
        b_router_expert[0])
    x1, h2, rinfo, fields, counts = _mix(
        y_fox, o_h, gf, gh, x, g1, sc2, sh2,
        w_up_fox[0].astype(BF16), w_up_hgrn[0].astype(BF16), w_out[0].astype(BF16),
        ln1_g[0].reshape(1, D), ln1_b[0].reshape(1, D), wr, br, alpha, ngroups, nper)

    tm_e = 256
    dt = D // WORD_LANES
    ntiles = (2 * T) // tm_e + nexp
    nslots = ntiles * tm_e
    cnt = counts[0, :nexp].astype(jnp.int32)
    padded = ((cnt + tm_e - 1) // tm_e) * tm_e
    ends = jnp.cumsum(padded)
    starts = ends - padded
    eid = fields[2:4].astype(jnp.int32)
    rank = fields[4:6].astype(jnp.int32)
    first = jnp.sum(jnp.where(eid[None] == jnp.arange(nexp, dtype=jnp.int32)[:, None, None],
                              starts[:, None, None], 0), axis=0)
    pos = first + rank
    tile_start = jnp.arange(ntiles, dtype=jnp.int32) * tm_e
    tile_expert = jnp.minimum(jnp.sum((tile_start[:, None] >= ends[None, :]).astype(jnp.int32), axis=1), nexp - 1)
    tile_rows = jnp.clip(starts[tile_expert] + cnt[tile_expert] - tile_start, 0, tm_e)
    rows = pos[:, None, :] + (jnp.arange(dt, dtype=jnp.int32) * nslots)[None, :, None]

    xs = _sc_scatter_rows(h2.reshape(dt * T, LANES), rows[0], rows[1], dt * nslots)
    ys = _experts(tile_expert, tile_rows, xs.reshape(dt, nslots, LANES),
                  w_expert_gate[0], w_expert_up[0], w_expert_down[0], tm_e)
    ngroups_out = 2 if B % 2 == 0 else 1
    gb = B // ngroups_out
    out = None
    for gi in range(ngroups_out):
        t0, t1 = gi * gb * S, (gi + 1) * gb * S
        yg = _sc_gather_rows(ys.reshape(dt * nslots, LANES), rows[:, :, t0:t1].reshape(2 * dt, t1 - t0))
        out = _combine(yg.reshape(2, dt, t1 - t0, LANES), x1, rinfo, g2,
                       ln2_g[0].reshape(1, D), ln2_b[0].reshape(1, D), alpha, gi * gb, gb, prev=out)
    return out
```

```python
import functools

import jax
import jax.numpy as jnp
from jax import lax
from jax.experimental import pallas as pl
from jax.experimental.pallas import tpu as pltpu
from jax.experimental.pallas import tpu_sc as plsc

F32 = jnp.float32
BF16 = jnp.bfloat16
HIGHEST = lax.Precision.HIGHEST

LANES = 128
HEAD_DIM = 64
LN_EPS = 1e-5
RMS_EPS = 1e-6
LOG2E = 1.4426950408889634
NEG_BIG = -1e30
HCHUNK = 16
HBLOCK = 64
HGRN_SAFE_EXP = 60.0
ROW_TILE = 8
WORD_LANES = 2 * LANES
VMEM_LIMIT = 56 * 1024 * 1024


def _cparams(sem, vmem=VMEM_LIMIT):
    return pltpu.CompilerParams(dimension_semantics=sem, vmem_limit_bytes=vmem)


def _sigmoid(x):
    return 0.5 * jnp.tanh(0.5 * x) + 0.5


def _silu(x):
    return x * _sigmoid(x)


def _ada_kernel(c_ref, w_ref, b_ref, o_ref):
    c = c_ref[...]
    o_ref[...] = jnp.dot(_silu(c), w_ref[...], precision=HIGHEST,
                         preferred_element_type=F32) + b_ref[...]


def _ada(c, w_ada, b_ada):
    B, D = c.shape
    N = w_ada.shape[1]
    tn = 1024
    return pl.pallas_call(
        _ada_kernel,
        out_shape=jax.ShapeDtypeStruct((B, N), F32),
        grid=(N // tn,),
        in_specs=[pl.BlockSpec((B, D), lambda j: (0, 0)),
                  pl.BlockSpec((D, tn), lambda j: (0, j)),
                  pl.BlockSpec((1, tn), lambda j: (0, j))],
        out_specs=pl.BlockSpec((B, tn), lambda j: (0, j)),
        compiler_params=_cparams(("arbitrary",)),
    )(c, w_ada, b_ada.reshape(1, N))


N_FOX_SEGS = 4


def _inproj_kernel(x_ref, sc_ref, sh_ref, wf_ref, wr_ref,
                   fq_ref, fk_ref, fv_ref, ff_ref, hq_ref, hf_ref, hi_ref, hg_ref, gf_ref, gh_ref,
                   *, segs, q_scale):
    h = (x_ref[...] * (1.0 + sc_ref[...]) + sh_ref[...]).astype(BF16)
    outs = (fq_ref, fk_ref, fv_ref, ff_ref, hq_ref, hf_ref, hi_ref, hg_ref, gf_ref, gh_ref)
    for idx, (o_ref, (a, b)) in enumerate(zip(outs, segs)):
        w_ref = wf_ref if idx < N_FOX_SEGS else wr_ref
        r = jnp.dot(h, w_ref[:, a:b], preferred_element_type=F32)
        if idx == 0:
            r = r * q_scale
        o_ref[...] = r.astype(o_ref.dtype)


def _inproj(x, sc1, sh1, w_fox, w_rest, segs, tm=256):
    B, S, D = x.shape
    widths = [b - a for a, b in segs]
    dtypes = [BF16, BF16, BF16, F32, BF16, F32, BF16, BF16, BF16, BF16]
    out_shape = tuple(jax.ShapeDtypeStruct((B, S, w), dt) for w, dt in zip(widths, dtypes))
    out_specs = tuple(pl.BlockSpec((None, tm, w), lambda b, i: (b, i, 0)) for w in widths)
    vec = pl.BlockSpec((None, 1, D), lambda b, i: (b, 0, 0))
    return pl.pallas_call(
        functools.partial(_inproj_kernel, segs=tuple(segs), q_scale=HEAD_DIM ** -0.5 * LOG2E),
        out_shape=out_shape,
        grid=(B, S // tm),
        in_specs=[pl.BlockSpec((None, tm, D), lambda b, i: (b, i, 0)), vec, vec,
                  pl.BlockSpec(w_fox.shape, lambda b, i: (0, 0)),
                  pl.BlockSpec(w_rest.shape, lambda b, i: (0, 0))],
        out_specs=out_specs,
        compiler_params=_cparams(("parallel", "parallel")),
    )(x, sc1, sh1, w_fox, w_rest)


def _foxcum_kernel(ff_ref, b_ref, o_ref, *, blk):
    S = ff_ref.shape[0]
    r = lax.broadcasted_iota(jnp.int32, (blk, blk), 0)
    c = lax.broadcasted_iota(jnp.int32, (blk, blk), 1)
    lower = (r >= c).astype(F32)
    carry = jnp.zeros((1, LANES), F32)
    for j in range(S // blk):
        z = ff_ref[j * blk:(j + 1) * blk, :] + b_ref[...]
        lf = jnp.minimum(z, 0.0) - jnp.log(1.0 + jnp.exp(-jnp.abs(z)))
        cum = jnp.dot(lower, lf, precision=HIGHEST, preferred_element_type=F32) + carry
        o_ref[j * blk:(j + 1) * blk, :] = cum * LOG2E
        carry = cum[blk - 1:blk, :]


def _foxcum(ffp, bias_p, blk=256):
    B, S, _ = ffp.shape
    return pl.pallas_call(
        functools.partial(_foxcum_kernel, blk=blk),
        out_shape=jax.ShapeDtypeStruct((B, S, LANES), F32),
        grid=(B,),
        in_specs=[pl.BlockSpec((None, S, LANES), lambda b: (b, 0, 0)),
                  pl.BlockSpec((1, LANES), lambda b: (0, 0))],
        out_specs=pl.BlockSpec((None, S, LANES), lambda b: (b, 0, 0)),
        compiler_params=_cparams(("parallel",)),
    )(ffp, bias_p)


NCUM = 3


def _fox_kernel(q_ref, k_ref, v_ref, c_ref, o_ref, ka_sc, kb_sc, va_sc, vb_sc, *, tq, tk):
    p = pl.program_id(1)
    qi = pl.program_id(2)
    S = k_ref.shape[0]

    @pl.when(qi == 0)
    def _():
        lane = lax.broadcasted_iota(jnp.int32, (S, LANES), 1)
        rr = lax.broadcasted_iota(jnp.int32, (LANES, LANES), 0)
        cc = lax.broadcasted_iota(jnp.int32, (LANES, LANES), 1)
        rest = c_ref[...]
        placed = jnp.zeros((S, LANES), F32)
        for i in range(NCUM):
            piece = rest.astype(BF16)
            rest = rest - piece.astype(F32)
            sel = ((rr == 2 * p) & (cc == HEAD_DIM + i)) | ((rr == 2 * p + 1) & (cc == i))
            placed = placed + jnp.dot(piece, jnp.where(sel, 1.0, 0.0).astype(BF16), preferred_element_type=F32)
        k2 = k_ref[...].astype(F32)
        ka_sc[...] = jnp.where(lane < HEAD_DIM, k2, -placed).astype(BF16)
        kb_sc[...] = jnp.where(lane >= HEAD_DIM, k2, -placed).astype(BF16)
        vt = v_ref[...].astype(F32).T
        row = lax.broadcasted_iota(jnp.int32, (LANES, S), 0)
        va_sc[...] = jnp.where(row < HEAD_DIM, vt, jnp.where(row == HEAD_DIM, 1.0, 0.0)).astype(BF16)
        vb_sc[...] = jnp.where(row >= HEAD_DIM, vt, jnp.where(row == 0, 1.0, 0.0)).astype(BF16)

    q2 = q_ref[...].astype(F32)
    qlane = lax.broadcasted_iota(jnp.int32, (tq, LANES), 1)
    qa = jnp.where(qlane < HEAD_DIM, q2, jnp.where(qlane < HEAD_DIM + NCUM, 1.0, 0.0)).astype(BF16)
    qb = jnp.where(qlane >= HEAD_DIM, q2, jnp.where(qlane < NCUM, 1.0, 0.0)).astype(BF16)
    nsub = tq // tk

    def block(k0, carry, diag_off):
        q0 = 0 if diag_off is None else diag_off
        out = []
        for ksc, vsc, qh, (m, acc) in ((ka_sc, va_sc, qa, carry[:2]), (kb_sc, vb_sc, qb, carry[2:])):
            st = lax.dot_general(ksc[pl.ds(k0, tk), :], qh[q0:, :], (((1,), (1,)), ((), ())),
                                 preferred_element_type=F32)
            if diag_off is not None:
                st = jnp.where(lax.broadcasted_iota(jnp.int32, st.shape, 0)
                               <= lax.broadcasted_iota(jnp.int32, st.shape, 1), st, NEG_BIG)
            m_old = m[:, q0:]
            m_new = jnp.maximum(m_old, jnp.max(st, axis=0, keepdims=True))
            pt = jnp.exp2(st - m_new).astype(BF16)
            acc_new = (jnp.exp2(m_old - m_new) * acc[:, q0:]
                       + jnp.dot(vsc[:, pl.ds(k0, tk)], pt, preferred_element_type=F32))
            if q0:
                m_new = jnp.concatenate([m[:, :q0], m_new], axis=1)
                acc_new = jnp.concatenate([acc[:, :q0], acc_new], axis=1)
            out += [m_new, acc_new]
        return tuple(out)

    def group(j, carry):
        k0 = pl.multiple_of(j * (nsub * tk), nsub * tk)
        for u in range(nsub):
            carry = block(k0 + u * tk, carry, None)
        return carry

    m0 = jnp.full((1, tq), NEG_BIG, F32)
    a0 = jnp.zeros((LANES, tq), F32)
    carry = lax.fori_loop(0, qi, group, (m0, a0, m0, a0))
    for d in range(nsub):
        carry = block(pl.multiple_of(qi * tq + d * tk, tk), carry, d * tk)
    _, aa, _, ab = carry
    row = lax.broadcasted_iota(jnp.int32, (LANES, tq), 0)
    ot = jnp.where(row < HEAD_DIM, aa * (1.0 / aa[HEAD_DIM:HEAD_DIM + 1, :]), ab * (1.0 / ab[0:1, :]))
    o_ref[...] = ot.T.astype(o_ref.dtype)


def _fox(fq, fk, fv, cum, tq=1024, tk=256):
    B, S, W = fq.shape
    assert tq % (2 * tk) == 0 and S % tq == 0
    npairs = W // LANES
    return pl.pallas_call(
        functools.partial(_fox_kernel, tq=tq, tk=tk),
        out_shape=jax.ShapeDtypeStruct((B, S, W), BF16),
        grid=(B, npairs, S // tq),
        in_specs=[pl.BlockSpec((None, tq, LANES), lambda b, p, i: (b, i, p)),
                  pl.BlockSpec((None, S, LANES), lambda b, p, i: (b, 0, p)),
                  pl.BlockSpec((None, S, LANES), lambda b, p, i: (b, 0, p)),
                  pl.BlockSpec((None, S, LANES), lambda b, p, i: (b, 0, 0))],
        out_specs=pl.BlockSpec((None, tq, LANES), lambda b, p, i: (b, i, p)),
        scratch_shapes=[pltpu.VMEM((S, LANES), BF16), pltpu.VMEM((S, LANES), BF16),
                        pltpu.VMEM((LANES, S), BF16), pltpu.VMEM((LANES, S), BF16)],
        compiler_params=_cparams(("parallel", "parallel", "arbitrary")),
    )(fq, fk, fv, cum)


def _hgrn_kernel(hq_ref, hf_ref, hi_ref, hg_ref, lb_ref, nw_ref, o_ref,
                 a_sc, b_sc, kk_sc, qq_sc, o_sc, w1_sc, w2_sc, w3_sc, w4_sc, w5_sc, w6_sc,
                 p_sc, st16_sc, dec_sc, st64_sc):
    S = hq_ref.shape[0]
    C = HCHUNK
    nchunks = S // C
    BLK = HBLOCK
    nblk = S // BLK

    lg = lb_ref[...]
    e = jnp.exp(lg - jnp.max(lg, axis=0, keepdims=True))
    lb = e[0:1, :] / jnp.sum(e, axis=0, keepdims=True)

    f = lb + (1.0 - lb) * _sigmoid(hf_ref[...])
    lf = jnp.log(f)
    kk_sc[...] = 1.0 - f
    qq_sc[...] = _silu(hq_ref[...].astype(F32))

    row = lax.broadcasted_iota(jnp.int32, (S, LANES), 0)
    rmod = row & (C - 1)
    a = lf
    d = 1
    while d < C:
        a = a + jnp.where(rmod >= d, pltpu.roll(a, d, axis=0), 0.0)
        d *= 2
    a3 = a.reshape(nchunks, C, LANES)
    alast = jnp.broadcast_to(a3[:, C - 1:C, :], (nchunks, C, LANES)).reshape(S, LANES)
    bmod = row & (BLK - 1)
    tot = alast
    d = C
    while d < BLK:
        tot = tot + jnp.where(bmod >= d, pltpu.roll(tot, d, axis=0), 0.0)
        d *= 2
    b = a + (tot - alast)
    b3 = b.reshape(nblk, BLK, LANES)
    blast = jnp.broadcast_to(b3[:, BLK - 1:BLK, :], (nblk, BLK, LANES)).reshape(S, LANES)
    a_sc[...] = a
    b_sc[...] = b
    safe = jnp.max(-blast) <= HGRN_SAFE_EXP

    lane = lax.broadcasted_iota(jnp.int32, (C, LANES), 1)
    sr = lax.broadcasted_iota(jnp.int32, (LANES, LANES), 0)
    scn = lax.broadcasted_iota(jnp.int32, (LANES, LANES), 1)
    same_head = (sr // HEAD_DIM) == (scn // HEAD_DIM)

    @pl.when(safe)
    def _factorised():
        qa_sc, qb_sc, kh_sc, ke_sc, qd_sc, k2_sc = w1_sc, w2_sc, w3_sc, w4_sc, w5_sc, w6_sc
        SB = 2 * BLK
        nsb = S // SB
        bb = b_sc[...]
        dblk = jnp.exp(bb.reshape(nblk, BLK, LANES)[:, BLK - 1:BLK, :])
        dfull = jnp.broadcast_to(dblk, (nblk, BLK, LANES)).reshape(S, LANES)
        second = (row & BLK) != 0
        d_prev = pltpu.roll(dfull, BLK, axis=0)
        d_next = pltpu.roll(dfull, S - BLK, axis=0)
        qh = qq_sc[...] * jnp.exp(bb)
        slane = lax.broadcasted_iota(jnp.int32, (S, LANES), 1)
        qa_sc[...] = jnp.where(slane < HEAD_DIM, qh, 0.0).astype(BF16)
        qb_sc[...] = jnp.where(slane >= HEAD_DIM, qh, 0.0).astype(BF16)
        qd_sc[...] = (qh * jnp.where(second, d_prev, 1.0)).astype(BF16)
        kh = kk_sc[...] * jnp.exp(-bb)
        kh_sc[...] = kh.astype(BF16)
        ke = kh * dfull
        ke_sc[...] = ke.astype(BF16)
        k2_sc[...] = (ke * jnp.where(second, 1.0, d_next)).astype(BF16)
        d3 = dfull.reshape(nsb, SB, LANES)
        dec_sc[pl.ds(0, nsb), :] = d3[:, 0, :] * d3[:, BLK, :]
        unroll = 4
        tn = (((0,), (0,)), ((), ()))
        nt = (((1,), (1,)), ((), ()))

        def scan(g, st):
            for u in range(unroll):
                i = g * unroll + u
                r0 = pl.multiple_of(i * SB, SB)
                st64_sc[i] = st.astype(BF16)
                upd = lax.dot_general(hi_ref[pl.ds(r0, SB), :], k2_sc[pl.ds(r0, SB), :], tn,
                                      preferred_element_type=F32)
                st = st * dec_sc[pl.ds(i, 1), :] + jnp.where(same_head, upd, 0.0)
            return st

        lax.fori_loop(0, nsb // unroll, scan, jnp.zeros((LANES, LANES), F32))

        r = lax.broadcasted_iota(jnp.int32, (2 * SB, 2 * SB), 0)
        c = lax.broadcasted_iota(jnp.int32, (2 * SB, 2 * SB), 1)
        t = r & (SB - 1)
        visible = (((c < SB) & ((t & BLK) == (c & BLK)) & ((t & (BLK - 1)) >= (c & (BLK - 1))))
                   | ((c >= SB) & (c < SB + BLK) & (t >= BLK)))
        plane = lax.broadcasted_iota(jnp.int32, (SB, LANES), 1)
        pad = jnp.zeros((BLK, LANES), BF16)

        def readout(g, _):
            for u in range(unroll):
                i = g * unroll + u
                r0 = pl.multiple_of(i * SB, SB)
                vb = hi_ref[pl.ds(r0, SB), :]
                q2 = jnp.concatenate([qa_sc[pl.ds(r0, SB), :], qb_sc[pl.ds(r0, SB), :]], axis=0)
                kext = jnp.concatenate([kh_sc[pl.ds(r0, SB), :], ke_sc[pl.ds(r0, BLK), :], pad], axis=0)
                vext = jnp.concatenate([vb, vb[:BLK], pad], axis=0)
                sc = lax.dot_general(q2, kext, nt, preferred_element_type=F32)
                sc = jnp.where(visible, sc, 0.0).astype(BF16)
                out = jnp.dot(sc, vext, preferred_element_type=F32)
                o_inter = lax.dot_general(qd_sc[pl.ds(r0, SB), :], st64_sc[i], nt, preferred_element_type=F32)
                o_sc[pl.ds(r0, SB), :] = jnp.where(plane < HEAD_DIM, out[:SB], out[SB:]) + o_inter
            return 0

        lax.fori_loop(0, nsb // unroll, readout, 0)

    @pl.when(jnp.logical_not(safe))
    def _direct():
        qt_sc, kt_sc, s_sc, a2_sc = w1_sc, w2_sc, w3_sc, b_sc
        aa = a_sc[...]
        al = jnp.broadcast_to(aa.reshape(nchunks, C, LANES)[:, C - 1:C, :], (nchunks, C, LANES)).reshape(S, LANES)
        qt_sc[...] = (qq_sc[...] * jnp.exp(aa)).astype(BF16)
        kt_sc[...] = (kk_sc[...] * jnp.exp(al - aa)).astype(BF16)
        dec_sc[...] = jnp.exp(aa.reshape(nchunks, C, LANES)[:, C - 1, :])
        a2_sc[...] = aa * LOG2E
        trow = lax.broadcasted_iota(jnp.int32, (C, LANES), 0)

        def gen(c, _):
            r0 = pl.multiple_of(c * C, C)
            ac = a2_sc[pl.ds(r0, C), :]
            qc = qq_sc[pl.ds(r0, C), :]
            kc = kk_sc[pl.ds(r0, C), :]
            half = C // 2
            for s in range(C):
                if s < half:
                    dec = jnp.exp2(jnp.where(trow >= s, ac - ac[s:s + 1, :], NEG_BIG))
                    p = qc * (kc[s:s + 1, :] * dec)
                else:
                    dec = jnp.exp2(jnp.where(trow[half:] >= s, ac[half:] - ac[s:s + 1, :], NEG_BIG))
                    p = jnp.concatenate([jnp.zeros((half, LANES), F32), qc[half:] * (kc[s:s + 1, :] * dec)],
                                        axis=0)
                p_sc[pl.ds(r0, C), s * LANES:(s + 1) * LANES] = p.astype(BF16)
            return 0

        lax.fori_loop(0, nchunks, gen, 0)

        er = lax.broadcasted_iota(jnp.int32, (C * LANES, LANES), 0)
        ec = lax.broadcasted_iota(jnp.int32, (C * LANES, LANES), 1)
        emat = (ec == ((er & (LANES - 1)) // HEAD_DIM) * C + er // LANES).astype(BF16)
        rb = 256

        def red(i, _):
            r0 = pl.multiple_of(i * rb, rb)
            s_sc[pl.ds(r0, rb), :] = jnp.dot(p_sc[pl.ds(r0, rb), :], emat,
                                             preferred_element_type=F32).astype(BF16)
            return 0

        lax.fori_loop(0, S // rb, red, 0)

        unroll = 16

        def scan(g, st):
            for u in range(unroll):
                c = g * unroll + u
                r0 = pl.multiple_of(c * C, C)
                st16_sc[c] = st.astype(BF16)
                upd = lax.dot_general(hi_ref[pl.ds(r0, C), :], kt_sc[pl.ds(r0, C), :],
                                      (((0,), (0,)), ((), ())), preferred_element_type=F32)
                st = st * dec_sc[pl.ds(c, 1), :] + jnp.where(same_head, upd, 0.0)
            return st

        lax.fori_loop(0, nchunks // unroll, scan, jnp.zeros((LANES, LANES), F32))

        def readout(g, _):
            for u in range(unroll):
                c = g * unroll + u
                r0 = pl.multiple_of(c * C, C)
                vc = hi_ref[pl.ds(r0, C), :]
                o_inter = lax.dot_general(qt_sc[pl.ds(r0, C), :], st16_sc[c],
                                          (((1,), (1,)), ((), ())), preferred_element_type=F32)
                v2 = jnp.concatenate([jnp.where(lane < HEAD_DIM, vc, jnp.zeros_like(vc)),
                                      jnp.where(lane >= HEAD_DIM, vc, jnp.zeros_like(vc))], axis=0)
                o_intra = jnp.dot(s_sc[pl.ds(r0, C), :][:, :2 * C], v2, preferred_element_type=F32)
                o_sc[pl.ds(r0, C), :] = o_inter + o_intra
            return 0

        lax.fori_loop(0, nchunks // unroll, readout, 0)

    o = o_sc[...]
    ones_head = jnp.where(same_head, 1.0, 0.0).astype(BF16)
    sq = o * o
    sq_top = pltpu.bitcast(pltpu.bitcast(sq, jnp.uint32) & jnp.uint32(0xFFFF0000), F32)
    ms = (jnp.dot(sq_top.astype(BF16), ones_head, preferred_element_type=F32)
          + jnp.dot((sq - sq_top).astype(BF16), ones_head, preferred_element_type=F32)) * (1.0 / HEAD_DIM)
    y = o * lax.rsqrt(ms + RMS_EPS) * nw_ref[...]
    o_ref[...] = (y * _silu(hg_ref[...].astype(F32))).astype(o_ref.dtype)


def _hgrn(hq, hf, hi, hg, lb_logits, norm_w):
    B, S, W = hq.shape
    npairs = W // LANES
    nrows = lb_logits.shape[0]
    seq = pl.BlockSpec((None, S, LANES), lambda b, p: (b, 0, p))
    return pl.pallas_call(
        _hgrn_kernel,
        out_shape=jax.ShapeDtypeStruct((B, S, W), BF16),
        grid=(B, npairs),
        in_specs=[seq, seq, seq, seq,
                  pl.BlockSpec((nrows, LANES), lambda b, p: (0, p)),
                  pl.BlockSpec((1, LANES), lambda b, p: (0, p))],
        out_specs=seq,
        scratch_shapes=[pltpu.VMEM((S, LANES), F32),
                        pltpu.VMEM((S, LANES), F32),
                        pltpu.VMEM((S, LANES), F32),
                        pltpu.VMEM((S, LANES), F32),
                        pltpu.VMEM((S, LANES), F32),
                        pltpu.VMEM((S, LANES), BF16),
                        pltpu.VMEM((S, LANES), BF16),
                        pltpu.VMEM((S, LANES), BF16),
                        pltpu.VMEM((S, LANES), BF16),
                        pltpu.VMEM((S, LANES), BF16),
                        pltpu.VMEM((S, LANES), BF16),
                        pltpu.VMEM((S, HCHUNK * LANES), BF16),
                        pltpu.VMEM((S // HCHUNK, LANES, LANES), BF16),
                        pltpu.VMEM((S // HCHUNK, LANES), F32),
                        pltpu.VMEM((S // HBLOCK, LANES, LANES), BF16)],
        compiler_params=_cparams(("parallel", "parallel")),
    )(hq, hf, hi, hg, lb_logits, norm_w.reshape(1, W))


def _layer_norm(v, g, b):
    mu = jnp.mean(v, axis=-1, keepdims=True)
    d = v - mu
    var = jnp.mean(d * d, axis=-1, keepdims=True)
    return d * lax.rsqrt(var + LN_EPS) * g + b


def _bf16_bits(x):
    u = pltpu.bitcast(x, jnp.uint32)
    return (u + jnp.uint32(0x7FFF) + ((u >> 16) & jnp.uint32(1))) & jnp.uint32(0xFFFF0000)


def _store_chunks(ref, val):
    n = ref.shape[0]
    for j in range(n):
        lo = _bf16_bits(val[:, j * LANES:(j + 1) * LANES]) >> 16
        hi = _bf16_bits(val[:, (j + n) * LANES:(j + n + 1) * LANES])
        ref[j] = pltpu.bitcast(lo | hi, F32)


def _load_chunks(ref):
    words = [pltpu.bitcast(ref[j], jnp.uint32) for j in range(ref.shape[0])]
    lo = [pltpu.bitcast(w << 16, F32) for w in words]
    hi = [pltpu.bitcast(w & jnp.uint32(0xFFFF0000), F32) for w in words]
    return jnp.concatenate(lo + hi, axis=1)


def _mix_kernel(yf_ref, oh_ref, gf_ref, gh_ref, x_ref, g1_ref, sc2_ref, sh2_ref,
                wuf_ref, wuh_ref, wo_ref, lg_ref, lbias_ref, wr_ref, br_ref,
                x1_ref, h2_ref, ri_ref, rt_ref, cnt_ref, carry_sc, *, alpha, ngroups, nper):
    first = (pl.program_id(0) == 0) & (pl.program_id(1) == 0)

    @pl.when(first)
    def _():
        carry_sc[...] = jnp.zeros_like(carry_sc)

    tm = x_ref.shape[0]
    yf = jnp.dot(yf_ref[...], wuf_ref[...], preferred_element_type=F32)
    yh = jnp.dot(oh_ref[...], wuh_ref[...], preferred_element_type=F32)
    merged = _sigmoid(gf_ref[...].astype(F32)) * yf + _sigmoid(gh_ref[...].astype(F32)) * yh
    y = jnp.dot(merged.astype(BF16), wo_ref[...], preferred_element_type=F32)
    x1 = _layer_norm(alpha * x_ref[...] + g1_ref[...] * y, lg_ref[...], lbias_ref[...])
    x1_ref[...] = x1
    h2 = x1 * (1.0 + sc2_ref[...]) + sh2_ref[...]
    _store_chunks(h2_ref, h2)

    h_top = pltpu.bitcast(pltpu.bitcast(h2, jnp.uint32) & jnp.uint32(0xFFFF0000), F32)
    h_hi = h_top.astype(BF16)
    h_lo = (h2 - h_top).astype(BF16)
    logits = (jnp.dot(h_hi, wr_ref[0], preferred_element_type=F32)
              + jnp.dot(h_hi, wr_ref[1], preferred_element_type=F32)
              + jnp.dot(h_lo, wr_ref[0], preferred_element_type=F32)) + br_ref[...]
    lane = lax.broadcasted_iota(jnp.int32, (tm, LANES), 1)
    big = jnp.int32(1 << 20)

    def argmax_first(vals, mask):
        mx = jnp.max(jnp.where(mask, vals, -jnp.inf), axis=1, keepdims=True)
        idx = jnp.min(jnp.where(mask & (vals == mx), lane, big), axis=1, keepdims=True)
        return mx, idx

    gmask = lane < ngroups
    gmax = jnp.max(jnp.where(gmask, logits, -jnp.inf), axis=1, keepdims=True)
    gexp = jnp.where(gmask, jnp.exp(logits - gmax), 0.0)
    gprob = gexp / jnp.sum(gexp, axis=1, keepdims=True)
    g_w, g_idx = argmax_first(gprob, gmask)

    lo = ngroups + g_idx * nper
    emask = (lane >= lo) & (lane < lo + nper)
    emax = jnp.max(jnp.where(emask, logits, -jnp.inf), axis=1, keepdims=True)
    eexp = jnp.where(emask, jnp.exp(logits - emax), 0.0)
    eprob = eexp / jnp.sum(eexp, axis=1, keepdims=True)
    p0, i0 = argmax_first(eprob, emask)
    p1, i1 = argmax_first(eprob, emask & (lane != i0))
    den = p0 + p1
    w0 = p0 / den * g_w
    w1 = p1 / den * g_w
    e0 = i0 - ngroups
    e1 = i1 - ngroups

    oh = ((lane == e0) | (lane == e1)).astype(F32)
    r = lax.broadcasted_iota(jnp.int32, (tm, tm), 0)
    c = lax.broadcasted_iota(jnp.int32, (tm, tm), 1)
    strict_lower = (c < r).astype(BF16)
    before = jnp.dot(strict_lower, oh.astype(BF16), preferred_element_type=F32) + carry_sc[...]
    rank0 = jnp.sum(jnp.where(lane == e0, before, 0.0), axis=1, keepdims=True)
    rank1 = jnp.sum(jnp.where(lane == e1, before, 0.0), axis=1, keepdims=True)
    carry_sc[...] = carry_sc[...] + jnp.sum(oh, axis=0, keepdims=True)
    cnt_ref[...] = carry_sc[...]

    info = jnp.where(lane == 0, w0, 0.0)
    info = jnp.where(lane == 1, w1, info)
    info = jnp.where(lane == 2, e0.astype(F32), info)
    info = jnp.where(lane == 3, e1.astype(F32), info)
    info = jnp.where(lane == 4, rank0, info)
    info = jnp.where(lane == 5, rank1, info)
    ri_ref[...] = info
    rt_ref[...] = info.T[:ROW_TILE, :]


def _mix(yf, oh, gf, gh, x, g1, sc2, sh2, wuf, wuh, wo, ln_g, ln_b, wr, br, alpha, ngroups, nper, tm=512):
    B, S, D = x.shape
    W = yf.shape[2]
    tok = lambda w: pl.BlockSpec((None, tm, w), lambda b, i: (b, i, 0))
    vec = pl.BlockSpec((None, 1, D), lambda b, i: (b, 0, 0))
    full = lambda a: pl.BlockSpec(a.shape, lambda b, i: (0,) * a.ndim)
    return pl.pallas_call(
        functools.partial(_mix_kernel, alpha=alpha, ngroups=ngroups, nper=nper),
        out_shape=(jax.ShapeDtypeStruct((B, S, D), F32),
                   jax.ShapeDtypeStruct((D // WORD_LANES, B * S, LANES), F32),
                   jax.ShapeDtypeStruct((B, S, LANES), F32),
                   jax.ShapeDtypeStruct((ROW_TILE, B * S), F32),
                   jax.ShapeDtypeStruct((1, LANES), F32)),
        grid=(B, S // tm),
        in_specs=[tok(W), tok(W), tok(D), tok(D), tok(D), vec, vec, vec,
                  full(wuf), full(wuh), full(wo), full(ln_g), full(ln_b), full(wr), full(br)],
        out_specs=(tok(D),
                   pl.BlockSpec((D // WORD_LANES, tm, LANES), lambda b, i: (0, b * (S // tm) + i, 0)),
                   tok(LANES),
                   pl.BlockSpec((ROW_TILE, tm), lambda b, i: (0, b * (S // tm) + i)),
                   pl.BlockSpec((1, LANES), lambda b, i: (0, 0))),
        scratch_shapes=[pltpu.VMEM((1, LANES), F32)],
        compiler_params=_cparams(("arbitrary", "arbitrary")),
    )(yf, oh, gf, gh, x, g1, sc2, sh2, wuf, wuh, wo, ln_g, ln_b, wr, br)


def _sc_mesh():
    return plsc.VectorSubcoreMesh(core_axis_name="core", subcore_axis_name="subcore")


def _sc_pipeline(body, grid, in_specs, out_specs):
    return pltpu.emit_pipeline(body, grid=grid, in_specs=in_specs, out_specs=out_specs,
                               core_axis_name=("core", "subcore"),
                               dimension_semantics=(pltpu.PARALLEL,) * len(grid))


def _sc_scatter_rows(src, rows_a, rows_b, n_out):
    nj, t = rows_a.shape
    nc = t // LANES

    @pl.kernel(out_type=jax.ShapeDtypeStruct((n_out, LANES), src.dtype), mesh=_sc_mesh(), scratch_types=[])
    def scatter(x_hbm, a_hbm, b_hbm, o_hbm):
        def body(x_vmem, a_vmem, b_vmem):
            pltpu.sync_copy(x_vmem, o_hbm.at[a_vmem.at[0]])
            pltpu.sync_copy(x_vmem, o_hbm.at[b_vmem.at[0]])

        idx = pl.BlockSpec((1, LANES), lambda j, c: (j, c))
        _sc_pipeline(body, (nj, nc), [pl.BlockSpec((LANES, LANES), lambda j, c: (j * nc + c, 0)), idx, idx],
                     [])(x_hbm, a_hbm, b_hbm)

    return scatter(src, rows_a, rows_b)


def _sc_gather_rows(table, rows):
    nr, t = rows.shape
    nc = t // LANES

    @pl.kernel(out_type=jax.ShapeDtypeStruct((nr * t, LANES), table.dtype), mesh=_sc_mesh(), scratch_types=[])
    def gather(x_hbm, i_hbm, o_hbm):
        def body(i_vmem, o_vmem):
            pltpu.sync_copy(x_hbm.at[i_vmem.at[0]], o_vmem)

        _sc_pipeline(body, (nr, nc), [pl.BlockSpec((1, LANES), lambda r, c: (r, c))],
                     [pl.BlockSpec((LANES, LANES), lambda r, c: (r * nc + c, 0))])(i_hbm, o_hbm)

    return gather(table, rows)


def _experts_kernel(te_ref, tn_ref, x_ref, wg_ref, wu_ref, wd_ref, o_ref):
    nrows = tn_ref[pl.program_id(0)]

    @pl.when(nrows > 0)
    def _():
        x = _load_chunks(x_ref)
        x = jnp.where(lax.broadcasted_iota(jnp.int32, x.shape, 0) < nrows, x, 0.0).astype(BF16)
        g = jnp.dot(x, wg_ref[...].astype(BF16), preferred_element_type=F32)
        u = jnp.dot(x, wu_ref[...].astype(BF16), preferred_element_type=F32)
        hid = (_silu(g) * u).astype(BF16)
        _store_chunks(o_ref, jnp.dot(hid, wd_ref[...].astype(BF16), preferred_element_type=F32))

    @pl.when(nrows == 0)
    def _():
        o_ref[...] = jnp.zeros_like(o_ref)


def _experts(tile_expert, tile_rows, xs, wg, wu, wd, tm):
    E, D, FF = wg.shape
    dt = D // WORD_LANES
    ntiles = tile_expert.shape[0]
    rows = pl.BlockSpec((dt, tm, LANES), lambda i, te, tn: (0, i, 0))
    grid_spec = pltpu.PrefetchScalarGridSpec(
        num_scalar_prefetch=2,
        grid=(ntiles,),
        in_specs=[rows,
                  pl.BlockSpec((None, D, FF), lambda i, te, tn: (te[i], 0, 0)),
                  pl.BlockSpec((None, D, FF), lambda i, te, tn: (te[i], 0, 0)),
                  pl.BlockSpec((None, FF, D), lambda i, te, tn: (te[i], 0, 0))],
        out_specs=rows,
    )
    return pl.pallas_call(
        _experts_kernel,
        out_shape=jax.ShapeDtypeStruct((dt, ntiles * tm, LANES), F32),
        grid_spec=grid_spec,
        compiler_params=_cparams(("arbitrary",)),
    )(tile_expert, tile_rows, xs, wg, wu, wd)


def _combine_kernel(yg_ref, x1_ref, ri_ref, g2_ref, lg_ref, lb_ref, o_ref, *, alpha):
    ri = ri_ref[...]
    y = ri[:, 0:1] * _load_chunks(yg_ref.at[0]) + ri[:, 1:2] * _load_chunks(yg_ref.at[1])
    o_ref[...] = _layer_norm(alpha * x1_ref[...] + g2_ref[...] * y, lg_ref[...], lb_ref[...])


def _combine(yg, x1, rinfo, g2, ln_g, ln_b, alpha, tm=256):
    B, S, D = x1.shape
    nb = S // tm
    return pl.pallas_call(
        functools.partial(_combine_kernel, alpha=alpha),
        out_shape=jax.ShapeDtypeStruct((B, S, D), F32),
        grid=(B, nb),
        in_specs=[pl.BlockSpec((2, D // WORD_LANES, tm, LANES), lambda b, i: (0, 0, b * nb + i, 0)),
                  pl.BlockSpec((None, tm, D), lambda b, i: (b, i, 0)),
                  pl.BlockSpec((None, tm, LANES), lambda b, i: (b, i, 0)),
                  pl.BlockSpec((None, 1, D), lambda b, i: (b, 0, 0)),
                  pl.BlockSpec((1, D), lambda b, i: (0, 0)),
                  pl.BlockSpec((1, D), lambda b, i: (0, 0))],
        out_specs=pl.BlockSpec((None, tm, D), lambda b, i: (b, i, 0)),
        compiler_params=_cparams(("parallel", "parallel")),
    )(yg, x1, rinfo, g2, ln_g, ln_b)


def kernel(x, c, w_ada, b_ada, w_in, b_fox_forget, hgrn_lb_logits, hgrn_norm_w, w_up_fox, w_up_hgrn, w_out,
           ln1_g, ln1_b, w_router_group, b_router_group, w_router_expert, b_router_expert,
           w_expert_gate, w_expert_up, w_expert_down, ln2_g, ln2_b):
    B, S, D = x.shape
    depth = w_ada.shape[0]
    assert depth == 1, "single-layer block"
    fox_heads = b_fox_forget.shape[1]
    fox_w = fox_heads * HEAD_DIM
    hgrn_w = hgrn_norm_w.shape[1]
    ngroups = w_router_group.shape[2]
    nexp = w_router_expert.shape[2]
    nper = nexp // ngroups
    alpha = (2 * depth) ** 0.25
    T = B * S

    ada = _ada(c, w_ada[0], b_ada[0])
    sh1, sc1, g1, sh2, sc2, g2 = [a.reshape(B, 1, D) for a in jnp.split(ada, 6, axis=-1)]

    wi = w_in[0]
    o_ff = 3 * fox_w
    w_fox = jnp.pad(wi[:, :o_ff + fox_heads], ((0, 0), (0, LANES - fox_heads))).astype(BF16)
    w_rest = wi[:, o_ff + fox_heads:].astype(BF16)
    widths = [fox_w, fox_w, fox_w, LANES, hgrn_w, hgrn_w, hgrn_w, hgrn_w, D, D]
    segs, off = [], 0
    for n, w in enumerate(widths):
        if n == 4:
            off = 0
        segs.append((off, off + w))
        off += w
    fq, fk, fv, ffp, hq, hf, hi, hg, gf, gh = _inproj(x, sc1, sh1, w_fox, w_rest, segs)

    bias_p = jnp.zeros((1, LANES), F32).at[0, :fox_heads].set(b_fox_forget[0])
    cum = _foxcum(ffp, bias_p)
    y_fox = _fox(fq, fk, fv, cum)

    o_h = _hgrn(hq, hf, hi, hg, hgrn_lb_logits, hgrn_norm_w[0])

    wr = jnp.zeros((D, LANES), F32).at[:, :ngroups].set(w_router_group[0]).at[:, ngroups:ngroups + nexp].set(
        w_router_expert[0])
    wr_hi = lax.bitcast_convert_type(lax.bitcast_convert_type(wr, jnp.uint32) & jnp.uint32(0xFFFF0000), F32)
    wr = jnp.stack([wr_hi.astype(BF16), (wr - wr_hi).astype(BF16)])
    br = jnp.zeros((1, LANES), F32).at[0, :ngroups].set(b_router_group[0]).at[0, ngroups:ngroups + nexp].set(
        b_router_expert[0])
    x1, h2, rinfo, fields, counts = _mix(
        y_fox, o_h, gf, gh, x, g1, sc2, sh2,
        w_up_fox[0].astype(BF16), w_up_hgrn[0].astype(BF16), w_out[0].astype(BF16),
        ln1_g[0].reshape(1, D), ln1_b[0].reshape(1, D), wr, br, alpha, ngroups, nper)

    tm_e = 512
    dt = D // WORD_LANES
    ntiles = (2 * T) // tm_e + nexp
    nslots = ntiles * tm_e
    cnt = counts[0, :nexp].astype(jnp.int32)
    padded = ((cnt + tm_e - 1) // tm_e) * tm_e
    ends = jnp.cumsum(padded)
    starts = ends - padded
    eid = fields[2:4].astype(jnp.int32)
    rank = fields[4:6].astype(jnp.int32)
    first = jnp.sum(jnp.where(eid[None] == jnp.arange(nexp, dtype=jnp.int32)[:, None, None],
                              starts[:, None, None], 0), axis=0)
    pos = first + rank
    tile_start = jnp.arange(ntiles, dtype=jnp.int32) * tm_e
    tile_expert = jnp.minimum(jnp.sum((tile_start[:, None] >= ends[None, :]).astype(jnp.int32), axis=1), nexp - 1)
    tile_rows = jnp.clip(starts[tile_expert] + cnt[tile_expert] - tile_start, 0, tm_e)
    rows = pos[:, None, :] + (jnp.arange(dt, dtype=jnp.int32) * nslots)[None, :, None]

    xs = _sc_scatter_rows(h2.reshape(dt * T, LANES), rows[0], rows[1], dt * nslots)
    ys = _experts(tile_expert, tile_rows, xs.reshape(dt, nslots, LANES),
                  w_expert_gate[0], w_expert_up[0], w_expert_down[0], tm_e)
    yg = _sc_gather_rows(ys.reshape(dt * nslots, LANES), rows.reshape(2 * dt, T))
    return _combine(yg.reshape(2, dt, T, LANES), x1, rinfo, g2,
                    ln2_g[0].reshape(1, D), ln2_b[0].reshape(1, D), alpha)
```

```python
import functools

import jax
import jax.numpy as jnp
from jax import lax
from jax.experimental import pallas as pl
from jax.experimental.pallas import tpu as pltpu
from jax.experimental.pallas import tpu_sc as plsc

F32 = jnp.float32
BF16 = jnp.bfloat16
HIGHEST = lax.Precision.HIGHEST

LANES = 128
HEAD_DIM = 64
LN_EPS = 1e-5
RMS_EPS = 1e-6
LOG2E = 1.4426950408889634
NEG_BIG = -1e30
HCHUNK = 16
HBLOCK = 64
HGRN_SAFE_EXP = 60.0
ROW_TILE = 8
WORD_LANES = 2 * LANES
VMEM_LIMIT = 56 * 1024 * 1024


def _cparams(sem, vmem=VMEM_LIMIT):
    return pltpu.CompilerParams(dimension_semantics=sem, vmem_limit_bytes=vmem)


def _sigmoid(x):
    return 0.5 * jnp.tanh(0.5 * x) + 0.5


def _silu(x):
    return x * _sigmoid(x)


def _ada_kernel(c_ref, w_ref, b_ref, o_ref):
    c = c_ref[...]
    o_ref[...] = jnp.dot(_silu(c), w_ref[...], precision=HIGHEST,
                         preferred_element_type=F32) + b_ref[...]


def _ada(c, w_ada, b_ada):
    B, D = c.shape
    N = w_ada.shape[1]
    tn = 1024
    return pl.pallas_call(
        _ada_kernel,
        out_shape=jax.ShapeDtypeStruct((B, N), F32),
        grid=(N // tn,),
        in_specs=[pl.BlockSpec((B, D), lambda j: (0, 0)),
                  pl.BlockSpec((D, tn), lambda j: (0, j)),
                  pl.BlockSpec((1, tn), lambda j: (0, j))],
        out_specs=pl.BlockSpec((B, tn), lambda j: (0, j)),
        compiler_params=_cparams(("arbitrary",)),
    )(c, w_ada, b_ada.reshape(1, N))


N_FOX_SEGS = 4


def _inproj_kernel(x_ref, sc_ref, sh_ref, wf_ref, wr_ref,
                   fq_ref, fk_ref, fv_ref, ff_ref, hq_ref, hf_ref, hi_ref, hg_ref, gf_ref, gh_ref,
                   *, segs, q_scale):
    h = (x_ref[...] * (1.0 + sc_ref[...]) + sh_ref[...]).astype(BF16)
    outs = (fq_ref, fk_ref, fv_ref, ff_ref, hq_ref, hf_ref, hi_ref, hg_ref, gf_ref, gh_ref)
    for idx, (o_ref, (a, b)) in enumerate(zip(outs, segs)):
        w_ref = wf_ref if idx < N_FOX_SEGS else wr_ref
        r = jnp.dot(h, w_ref[:, a:b], preferred_element_type=F32)
        if idx == 0:
            r = r * q_scale
        o_ref[...] = r.astype(o_ref.dtype)


def _inproj(x, sc1, sh1, w_fox, w_rest, segs, tm=256):
    B, S, D = x.shape
    widths = [b - a for a, b in segs]
    dtypes = [BF16, BF16, BF16, F32, BF16, F32, BF16, BF16, BF16, BF16]
    out_shape = tuple(jax.ShapeDtypeStruct((B, S, w), dt) for w, dt in zip(widths, dtypes))
    out_specs = tuple(pl.BlockSpec((None, tm, w), lambda b, i: (b, i, 0)) for w in widths)
    vec = pl.BlockSpec((None, 1, D), lambda b, i: (b, 0, 0))
    return pl.pallas_call(
        functools.partial(_inproj_kernel, segs=tuple(segs), q_scale=HEAD_DIM ** -0.5 * LOG2E),
        out_shape=out_shape,
        grid=(B, S // tm),
        in_specs=[pl.BlockSpec((None, tm, D), lambda b, i: (b, i, 0)), vec, vec,
                  pl.BlockSpec(w_fox.shape, lambda b, i: (0, 0)),
                  pl.BlockSpec(w_rest.shape, lambda b, i: (0, 0))],
        out_specs=out_specs,
        compiler_params=_cparams(("parallel", "parallel")),
    )(x, sc1, sh1, w_fox, w_rest)


def _foxcum_kernel(ff_ref, b_ref, o_ref, *, blk):
    S = ff_ref.shape[0]
    r = lax.broadcasted_iota(jnp.int32, (blk, blk), 0)
    c = lax.broadcasted_iota(jnp.int32, (blk, blk), 1)
    lower = (r >= c).astype(F32)
    carry = jnp.zeros((1, LANES), F32)
    for j in range(S // blk):
        z = ff_ref[j * blk:(j + 1) * blk, :] + b_ref[...]
        lf = jnp.minimum(z, 0.0) - jnp.log(1.0 + jnp.exp(-jnp.abs(z)))
        cum = jnp.dot(lower, lf, precision=HIGHEST, preferred_element_type=F32) + carry
        o_ref[j * blk:(j + 1) * blk, :] = cum * LOG2E
        carry = cum[blk - 1:blk, :]


def _foxcum(ffp, bias_p, blk=256):
    B, S, _ = ffp.shape
    return pl.pallas_call(
        functools.partial(_foxcum_kernel, blk=blk),
        out_shape=jax.ShapeDtypeStruct((B, S, LANES), F32),
        grid=(B,),
        in_specs=[pl.BlockSpec((None, S, LANES), lambda b: (b, 0, 0)),
                  pl.BlockSpec((1, LANES), lambda b: (0, 0))],
        out_specs=pl.BlockSpec((None, S, LANES), lambda b: (b, 0, 0)),
        compiler_params=_cparams(("parallel",)),
    )(ffp, bias_p)


NCUM = 3


def _fox_kernel(q_ref, k_ref, v_ref, c_ref, o_ref, ka_sc, kb_sc, va_sc, vb_sc, *, tq, tk):
    p = pl.program_id(1)
    qi = pl.program_id(2)
    S = k_ref.shape[0]

    @pl.when(qi == 0)
    def _():
        lane = lax.broadcasted_iota(jnp.int32, (S, LANES), 1)
        rr = lax.broadcasted_iota(jnp.int32, (LANES, LANES), 0)
        cc = lax.broadcasted_iota(jnp.int32, (LANES, LANES), 1)
        rest = c_ref[...]
        placed = jnp.zeros((S, LANES), F32)
        for i in range(NCUM):
            piece = rest.astype(BF16)
            rest = rest - piece.astype(F32)
            sel = ((rr == 2 * p) & (cc == HEAD_DIM + i)) | ((rr == 2 * p + 1) & (cc == i))
            placed = placed + jnp.dot(piece, jnp.where(sel, 1.0, 0.0).astype(BF16), preferred_element_type=F32)
        k2 = k_ref[...].astype(F32)
        ka_sc[...] = jnp.where(lane < HEAD_DIM, k2, -placed).astype(BF16)
        kb_sc[...] = jnp.where(lane >= HEAD_DIM, k2, -placed).astype(BF16)
        vt = v_ref[...].astype(F32).T
        row = lax.broadcasted_iota(jnp.int32, (LANES, S), 0)
        va_sc[...] = jnp.where(row < HEAD_DIM, vt, jnp.where(row == HEAD_DIM, 1.0, 0.0)).astype(BF16)
        vb_sc[...] = jnp.where(row >= HEAD_DIM, vt, jnp.where(row == 0, 1.0, 0.0)).astype(BF16)

    q2 = q_ref[...].astype(F32)
    qlane = lax.broadcasted_iota(jnp.int32, (tq, LANES), 1)
    qa = jnp.where(qlane < HEAD_DIM, q2, jnp.where(qlane < HEAD_DIM + NCUM, 1.0, 0.0)).astype(BF16)
    qb = jnp.where(qlane >= HEAD_DIM, q2, jnp.where(qlane < NCUM, 1.0, 0.0)).astype(BF16)
    nsub = tq // tk

    def block(k0, carry, diag_off):
        q0 = 0 if diag_off is None else diag_off
        out = []
        for ksc, vsc, qh, (m, acc) in ((ka_sc, va_sc, qa, carry[:2]), (kb_sc, vb_sc, qb, carry[2:])):
            st = lax.dot_general(ksc[pl.ds(k0, tk), :], qh[q0:, :], (((1,), (1,)), ((), ())),
                                 preferred_element_type=F32)
            if diag_off is not None:
                st = jnp.where(lax.broadcasted_iota(jnp.int32, st.shape, 0)
                               <= lax.broadcasted_iota(jnp.int32, st.shape, 1), st, NEG_BIG)
            m_old = m[:, q0:]
            m_new = jnp.maximum(m_old, jnp.max(st, axis=0, keepdims=True))
            pt = jnp.exp2(st - m_new).astype(BF16)
            acc_new = (jnp.exp2(m_old - m_new) * acc[:, q0:]
                       + jnp.dot(vsc[:, pl.ds(k0, tk)], pt, preferred_element_type=F32))
            if q0:
                m_new = jnp.concatenate([m[:, :q0], m_new], axis=1)
                acc_new = jnp.concatenate([acc[:, :q0], acc_new], axis=1)
            out += [m_new, acc_new]
        return tuple(out)

    def group(j, carry):
        k0 = pl.multiple_of(j * (nsub * tk), nsub * tk)
        for u in range(nsub):
            carry = block(k0 + u * tk, carry, None)
        return carry

    m0 = jnp.full((1, tq), NEG_BIG, F32)
    a0 = jnp.zeros((LANES, tq), F32)
    carry = lax.fori_loop(0, qi, group, (m0, a0, m0, a0))
    for d in range(nsub):
        carry = block(pl.multiple_of(qi * tq + d * tk, tk), carry, d * tk)
    _, aa, _, ab = carry
    row = lax.broadcasted_iota(jnp.int32, (LANES, tq), 0)
    ot = jnp.where(row < HEAD_DIM, aa * (1.0 / aa[HEAD_DIM:HEAD_DIM + 1, :]), ab * (1.0 / ab[0:1, :]))
    o_ref[...] = ot.T.astype(o_ref.dtype)


def _fox(fq, fk, fv, cum, tq=1024, tk=512):
    B, S, W = fq.shape
    assert tq % (2 * tk) == 0 and S % tq == 0
    npairs = W // LANES
    return pl.pallas_call(
        functools.partial(_fox_kernel, tq=tq, tk=tk),
        out_shape=jax.ShapeDtypeStruct((B, S, W), BF16),
        grid=(B, npairs, S // tq),
        in_specs=[pl.BlockSpec((None, tq, LANES), lambda b, p, i: (b, i, p)),
                  pl.BlockSpec((None, S, LANES), lambda b, p, i: (b, 0, p)),
                  pl.BlockSpec((None, S, LANES), lambda b, p, i: (b, 0, p)),
                  pl.BlockSpec((None, S, LANES), lambda b, p, i: (b, 0, 0))],
        out_specs=pl.BlockSpec((None, tq, LANES), lambda b, p, i: (b, i, p)),
        scratch_shapes=[pltpu.VMEM((S, LANES), BF16), pltpu.VMEM((S, LANES), BF16),
                        pltpu.VMEM((LANES, S), BF16), pltpu.VMEM((LANES, S), BF16)],
        compiler_params=_cparams(("parallel", "parallel", "arbitrary")),
    )(fq, fk, fv, cum)


def _hgrn_kernel(hq_ref, hf_ref, hi_ref, hg_ref, lb_ref, nw_ref, o_ref,
                 a_sc, b_sc, kk_sc, qq_sc, o_sc, w1_sc, w2_sc, w3_sc, w4_sc, w5_sc, w6_sc,
                 p_sc, st16_sc, dec_sc, st64_sc):
    S = hq_ref.shape[0]
    C = HCHUNK
    nchunks = S // C
    BLK = HBLOCK
    nblk = S // BLK

    lg = lb_ref[...]
    e = jnp.exp(lg - jnp.max(lg, axis=0, keepdims=True))
    lb = e[0:1, :] / jnp.sum(e, axis=0, keepdims=True)

    f = lb + (1.0 - lb) * _sigmoid(hf_ref[...])
    lf = jnp.log(f)
    kk_sc[...] = 1.0 - f
    qq_sc[...] = _silu(hq_ref[...].astype(F32))

    row = lax.broadcasted_iota(jnp.int32, (S, LANES), 0)
    rmod = row & (C - 1)
    a = lf
    d = 1
    while d < C:
        a = a + jnp.where(rmod >= d, pltpu.roll(a, d, axis=0), 0.0)
        d *= 2
    a3 = a.reshape(nchunks, C, LANES)
    alast = jnp.broadcast_to(a3[:, C - 1:C, :], (nchunks, C, LANES)).reshape(S, LANES)
    bmod = row & (BLK - 1)
    tot = alast
    d = C
    while d < BLK:
        tot = tot + jnp.where(bmod >= d, pltpu.roll(tot, d, axis=0), 0.0)
        d *= 2
    b = a + (tot - alast)
    b3 = b.reshape(nblk, BLK, LANES)
    blast = jnp.broadcast_to(b3[:, BLK - 1:BLK, :], (nblk, BLK, LANES)).reshape(S, LANES)
    a_sc[...] = a
    b_sc[...] = b
    safe = jnp.max(-blast) <= HGRN_SAFE_EXP

    lane = lax.broadcasted_iota(jnp.int32, (C, LANES), 1)
    sr = lax.broadcasted_iota(jnp.int32, (LANES, LANES), 0)
    scn = lax.broadcasted_iota(jnp.int32, (LANES, LANES), 1)
    same_head = (sr // HEAD_DIM) == (scn // HEAD_DIM)

    @pl.when(safe)
    def _factorised():
        qa_sc, qb_sc, kh_sc, ke_sc, qd_sc, k2_sc = w1_sc, w2_sc, w3_sc, w4_sc, w5_sc, w6_sc
        SB = 2 * BLK
        nsb = S // SB
        bb = b_sc[...]
        dblk = jnp.exp(bb.reshape(nblk, BLK, LANES)[:, BLK - 1:BLK, :])
        dfull = jnp.broadcast_to(dblk, (nblk, BLK, LANES)).reshape(S, LANES)
        second = (row & BLK) != 0
        d_prev = pltpu.roll(dfull, BLK, axis=0)
        d_next = pltpu.roll(dfull, S - BLK, axis=0)
        qh = qq_sc[...] * jnp.exp(bb)
        slane = lax.broadcasted_iota(jnp.int32, (S, LANES), 1)
        qa_sc[...] = jnp.where(slane < HEAD_DIM, qh, 0.0).astype(BF16)
        qb_sc[...] = jnp.where(slane >= HEAD_DIM, qh, 0.0).astype(BF16)
        qd_sc[...] = (qh * jnp.where(second, d_prev, 1.0)).astype(BF16)
        kh = kk_sc[...] * jnp.exp(-bb)
        kh_sc[...] = kh.astype(BF16)
        ke = kh * dfull
        ke_sc[...] = ke.astype(BF16)
        k2_sc[...] = (ke * jnp.where(second, 1.0, d_next)).astype(BF16)
        d3 = dfull.reshape(nsb, SB, LANES)
        dec_sc[pl.ds(0, nsb), :] = d3[:, 0, :] * d3[:, BLK, :]
        unroll = 4
        tn = (((0,), (0,)), ((), ()))
        nt = (((1,), (1,)), ((), ()))

        def scan(g, st):
            for u in range(unroll):
                i = g * unroll + u
                r0 = pl.multiple_of(i * SB, SB)
                st64_sc[i] = st.astype(BF16)
                upd = lax.dot_general(hi_ref[pl.ds(r0, SB), :], k2_sc[pl.ds(r0, SB), :], tn,
                                      preferred_element_type=F32)
                st = st * dec_sc[pl.ds(i, 1), :] + jnp.where(same_head, upd, 0.0)
            return st

        lax.fori_loop(0, nsb // unroll, scan, jnp.zeros((LANES, LANES), F32))

        r = lax.broadcasted_iota(jnp.int32, (2 * SB, 2 * SB), 0)
        c = lax.broadcasted_iota(jnp.int32, (2 * SB, 2 * SB), 1)
        t = r & (SB - 1)
        visible = (((c < SB) & ((t & BLK) == (c & BLK)) & ((t & (BLK - 1)) >= (c & (BLK - 1))))
                   | ((c >= SB) & (c < SB + BLK) & (t >= BLK)))
        plane = lax.broadcasted_iota(jnp.int32, (SB, LANES), 1)
        pad = jnp.zeros((BLK, LANES), BF16)

        def readout(g, _):
            for u in range(unroll):
                i = g * unroll + u
                r0 = pl.multiple_of(i * SB, SB)
                vb = hi_ref[pl.ds(r0, SB), :]
                q2 = jnp.concatenate([qa_sc[pl.ds(r0, SB), :], qb_sc[pl.ds(r0, SB), :]], axis=0)
                kext = jnp.concatenate([kh_sc[pl.ds(r0, SB), :], ke_sc[pl.ds(r0, BLK), :], pad], axis=0)
                vext = jnp.concatenate([vb, vb[:BLK], pad], axis=0)
                sc = lax.dot_general(q2, kext, nt, preferred_element_type=F32)
                sc = jnp.where(visible, sc, 0.0).astype(BF16)
                out = jnp.dot(sc, vext, preferred_element_type=F32)
                o_inter = lax.dot_general(qd_sc[pl.ds(r0, SB), :], st64_sc[i], nt, preferred_element_type=F32)
                o_sc[pl.ds(r0, SB), :] = jnp.where(plane < HEAD_DIM, out[:SB], out[SB:]) + o_inter
            return 0

        lax.fori_loop(0, nsb // unroll, readout, 0)

    @pl.when(jnp.logical_not(safe))
    def _direct():
        qt_sc, kt_sc, s_sc, a2_sc = w1_sc, w2_sc, w3_sc, b_sc
        aa = a_sc[...]
        al = jnp.broadcast_to(aa.reshape(nchunks, C, LANES)[:, C - 1:C, :], (nchunks, C, LANES)).reshape(S, LANES)
        qt_sc[...] = (qq_sc[...] * jnp.exp(aa)).astype(BF16)
        kt_sc[...] = (kk_sc[...] * jnp.exp(al - aa)).astype(BF16)
        dec_sc[...] = jnp.exp(aa.reshape(nchunks, C, LANES)[:, C - 1, :])
        a2_sc[...] = aa * LOG2E
        trow = lax.broadcasted_iota(jnp.int32, (C, LANES), 0)

        def gen(c, _):
            r0 = pl.multiple_of(c * C, C)
            ac = a2_sc[pl.ds(r0, C), :]
            qc = qq_sc[pl.ds(r0, C), :]
            kc = kk_sc[pl.ds(r0, C), :]
            half = C // 2
            for s in range(C):
                if s < half:
                    dec = jnp.exp2(jnp.where(trow >= s, ac - ac[s:s + 1, :], NEG_BIG))
                    p = qc * (kc[s:s + 1, :] * dec)
                else:
                    dec = jnp.exp2(jnp.where(trow[half:] >= s, ac[half:] - ac[s:s + 1, :], NEG_BIG))
                    p = jnp.concatenate([jnp.zeros((half, LANES), F32), qc[half:] * (kc[s:s + 1, :] * dec)],
                                        axis=0)
                p_sc[pl.ds(r0, C), s * LANES:(s + 1) * LANES] = p.astype(BF16)
            return 0

        lax.fori_loop(0, nchunks, gen, 0)

        er = lax.broadcasted_iota(jnp.int32, (C * LANES, LANES), 0)
        ec = lax.broadcasted_iota(jnp.int32, (C * LANES, LANES), 1)
        emat = (ec == ((er & (LANES - 1)) // HEAD_DIM) * C + er // LANES).astype(BF16)
        rb = 256

        def red(i, _):
            r0 = pl.multiple_of(i * rb, rb)
            s_sc[pl.ds(r0, rb), :] = jnp.dot(p_sc[pl.ds(r0, rb), :], emat,
                                             preferred_element_type=F32).astype(BF16)
            return 0

        lax.fori_loop(0, S // rb, red, 0)

        unroll = 16

        def scan(g, st):
            for u in range(unroll):
                c = g * unroll + u
                r0 = pl.multiple_of(c * C, C)
                st16_sc[c] = st.astype(BF16)
                upd = lax.dot_general(hi_ref[pl.ds(r0, C), :], kt_sc[pl.ds(r0, C), :],
                                      (((0,), (0,)), ((), ())), preferred_element_type=F32)
                st = st * dec_sc[pl.ds(c, 1), :] + jnp.where(same_head, upd, 0.0)
            return st

        lax.fori_loop(0, nchunks // unroll, scan, jnp.zeros((LANES, LANES), F32))

        def readout(g, _):
            for u in range(unroll):
                c = g * unroll + u
                r0 = pl.multiple_of(c * C, C)
                vc = hi_ref[pl.ds(r0, C), :]
                o_inter = lax.dot_general(qt_sc[pl.ds(r0, C), :], st16_sc[c],
                                          (((1,), (1,)), ((), ())), preferred_element_type=F32)
                v2 = jnp.concatenate([jnp.where(lane < HEAD_DIM, vc, jnp.zeros_like(vc)),
                                      jnp.where(lane >= HEAD_DIM, vc, jnp.zeros_like(vc))], axis=0)
                o_intra = jnp.dot(s_sc[pl.ds(r0, C), :][:, :2 * C], v2, preferred_element_type=F32)
                o_sc[pl.ds(r0, C), :] = o_inter + o_intra
            return 0

        lax.fori_loop(0, nchunks // unroll, readout, 0)

    o = o_sc[...]
    ones_head = jnp.where(same_head, 1.0, 0.0).astype(BF16)
    sq = o * o
    sq_top = pltpu.bitcast(pltpu.bitcast(sq, jnp.uint32) & jnp.uint32(0xFFFF0000), F32)
    ms = (jnp.dot(sq_top.astype(BF16), ones_head, preferred_element_type=F32)
          + jnp.dot((sq - sq_top).astype(BF16), ones_head, preferred_element_type=F32)) * (1.0 / HEAD_DIM)
    y = o * lax.rsqrt(ms + RMS_EPS) * nw_ref[...]
    o_ref[...] = (y * _silu(hg_ref[...].astype(F32))).astype(o_ref.dtype)


def _hgrn(hq, hf, hi, hg, lb_logits, norm_w):
    B, S, W = hq.shape
    npairs = W // LANES
    nrows = lb_logits.shape[0]
    seq = pl.BlockSpec((None, S, LANES), lambda b, p: (b, 0, p))
    return pl.pallas_call(
        _hgrn_kernel,
        out_shape=jax.ShapeDtypeStruct((B, S, W), BF16),
        grid=(B, npairs),
        in_specs=[seq, seq, seq, seq,
                  pl.BlockSpec((nrows, LANES), lambda b, p: (0, p)),
                  pl.BlockSpec((1, LANES), lambda b, p: (0, p))],
        out_specs=seq,
        scratch_shapes=[pltpu.VMEM((S, LANES), F32),
                        pltpu.VMEM((S, LANES), F32),
                        pltpu.VMEM((S, LANES), F32),
                        pltpu.VMEM((S, LANES), F32),
                        pltpu.VMEM((S, LANES), F32),
                        pltpu.VMEM((S, LANES), BF16),
                        pltpu.VMEM((S, LANES), BF16),
                        pltpu.VMEM((S, LANES), BF16),
                        pltpu.VMEM((S, LANES), BF16),
                        pltpu.VMEM((S, LANES), BF16),
                        pltpu.VMEM((S, LANES), BF16),
                        pltpu.VMEM((S, HCHUNK * LANES), BF16),
                        pltpu.VMEM((S // HCHUNK, LANES, LANES), BF16),
                        pltpu.VMEM((S // HCHUNK, LANES), F32),
                        pltpu.VMEM((S // HBLOCK, LANES, LANES), BF16)],
        compiler_params=_cparams(("parallel", "parallel")),
    )(hq, hf, hi, hg, lb_logits, norm_w.reshape(1, W))


def _layer_norm(v, g, b):
    mu = jnp.mean(v, axis=-1, keepdims=True)
    d = v - mu
    var = jnp.mean(d * d, axis=-1, keepdims=True)
    return d * lax.rsqrt(var + LN_EPS) * g + b


def _bf16_bits(x):
    u = pltpu.bitcast(x, jnp.uint32)
    return (u + jnp.uint32(0x7FFF) + ((u >> 16) & jnp.uint32(1))) & jnp.uint32(0xFFFF0000)


def _store_chunks(ref, val):
    n = ref.shape[0]
    for j in range(n):
        lo = _bf16_bits(val[:, j * LANES:(j + 1) * LANES]) >> 16
        hi = _bf16_bits(val[:, (j + n) * LANES:(j + n + 1) * LANES])
        ref[j] = pltpu.bitcast(lo | hi, F32)


def _load_chunks(ref):
    words = [pltpu.bitcast(ref[j], jnp.uint32) for j in range(ref.shape[0])]
    lo = [pltpu.bitcast(w << 16, F32) for w in words]
    hi = [pltpu.bitcast(w & jnp.uint32(0xFFFF0000), F32) for w in words]
    return jnp.concatenate(lo + hi, axis=1)


def _mix_kernel(yf_ref, oh_ref, gf_ref, gh_ref, x_ref, g1_ref, sc2_ref, sh2_ref,
                wuf_ref, wuh_ref, wo_ref, lg_ref, lbias_ref, wr_ref, br_ref,
                x1_ref, h2_ref, ri_ref, rt_ref, cnt_ref, carry_sc, *, alpha, ngroups, nper):
    first = (pl.program_id(0) == 0) & (pl.program_id(1) == 0)

    @pl.when(first)
    def _():
        carry_sc[...] = jnp.zeros_like(carry_sc)

    tm = x_ref.shape[0]
    yf = jnp.dot(yf_ref[...], wuf_ref[...], preferred_element_type=F32)
    yh = jnp.dot(oh_ref[...], wuh_ref[...], preferred_element_type=F32)
    merged = _sigmoid(gf_ref[...].astype(F32)) * yf + _sigmoid(gh_ref[...].astype(F32)) * yh
    y = jnp.dot(merged.astype(BF16), wo_ref[...], preferred_element_type=F32)
    x1 = _layer_norm(alpha * x_ref[...] + g1_ref[...] * y, lg_ref[...], lbias_ref[...])
    x1_ref[...] = x1
    h2 = x1 * (1.0 + sc2_ref[...]) + sh2_ref[...]
    _store_chunks(h2_ref, h2)

    h_top = pltpu.bitcast(pltpu.bitcast(h2, jnp.uint32) & jnp.uint32(0xFFFF0000), F32)
    h_hi = h_top.astype(BF16)
    h_lo = (h2 - h_top).astype(BF16)
    hh = jnp.dot(h_hi, wr_ref[...], preferred_element_type=F32)
    logits = (hh[:, :LANES] + hh[:, LANES:]
              + jnp.dot(h_lo, wr_ref[:, :LANES], preferred_element_type=F32)) + br_ref[...]
    lane = lax.broadcasted_iota(jnp.int32, (tm, LANES), 1)
    big = jnp.int32(1 << 20)

    def argmax_first(vals, mask):
        mx = jnp.max(jnp.where(mask, vals, -jnp.inf), axis=1, keepdims=True)
        idx = jnp.min(jnp.where(mask & (vals == mx), lane, big), axis=1, keepdims=True)
        return mx, idx

    gmask = lane < ngroups
    gmax = jnp.max(jnp.where(gmask, logits, -jnp.inf), axis=1, keepdims=True)
    gexp = jnp.where(gmask, jnp.exp(logits - gmax), 0.0)
    gprob = gexp / jnp.sum(gexp, axis=1, keepdims=True)
    g_w, g_idx = argmax_first(gprob, gmask)

    lo = ngroups + g_idx * nper
    emask = (lane >= lo) & (lane < lo + nper)
    emax = jnp.max(jnp.where(emask, logits, -jnp.inf), axis=1, keepdims=True)
    eexp = jnp.where(emask, jnp.exp(logits - emax), 0.0)
    eprob = eexp / jnp.sum(eexp, axis=1, keepdims=True)
    p0, i0 = argmax_first(eprob, emask)
    p1, i1 = argmax_first(eprob, emask & (lane != i0))
    den = p0 + p1
    w0 = p0 / den * g_w
    w1 = p1 / den * g_w
    e0 = i0 - ngroups
    e1 = i1 - ngroups

    oh = ((lane == e0) | (lane == e1)).astype(F32)
    r = lax.broadcasted_iota(jnp.int32, (tm, tm), 0)
    c = lax.broadcasted_iota(jnp.int32, (tm, tm), 1)
    strict_lower = (c < r).astype(BF16)
    before = jnp.dot(strict_lower, oh.astype(BF16), preferred_element_type=F32) + carry_sc[...]
    rank0 = jnp.sum(jnp.where(lane == e0, before, 0.0), axis=1, keepdims=True)
    rank1 = jnp.sum(jnp.where(lane == e1, before, 0.0), axis=1, keepdims=True)
    carry_sc[...] = carry_sc[...] + jnp.sum(oh, axis=0, keepdims=True)
    cnt_ref[...] = carry_sc[...]

    info = jnp.where(lane == 0, w0, 0.0)
    info = jnp.where(lane == 1, w1, info)
    info = jnp.where(lane == 2, e0.astype(F32), info)
    info = jnp.where(lane == 3, e1.astype(F32), info)
    info = jnp.where(lane == 4, rank0, info)
    info = jnp.where(lane == 5, rank1, info)
    ri_ref[...] = info
    rt_ref[...] = info.T[:ROW_TILE, :]


def _mix(yf, oh, gf, gh, x, g1, sc2, sh2, wuf, wuh, wo, ln_g, ln_b, wr, br, alpha, ngroups, nper, tm=512):
    B, S, D = x.shape
    W = yf.shape[2]
    tok = lambda w: pl.BlockSpec((None, tm, w), lambda b, i: (b, i, 0))
    vec = pl.BlockSpec((None, 1, D), lambda b, i: (b, 0, 0))
    full = lambda a: pl.BlockSpec(a.shape, lambda b, i: (0,) * a.ndim)
    return pl.pallas_call(
        functools.partial(_mix_kernel, alpha=alpha, ngroups=ngroups, nper=nper),
        out_shape=(jax.ShapeDtypeStruct((B, S, D), F32),
                   jax.ShapeDtypeStruct((D // WORD_LANES, B * S, LANES), F32),
                   jax.ShapeDtypeStruct((B, S, LANES), F32),
                   jax.ShapeDtypeStruct((ROW_TILE, B * S), F32),
                   jax.ShapeDtypeStruct((1, LANES), F32)),
        grid=(B, S // tm),
        in_specs=[tok(W), tok(W), tok(D), tok(D), tok(D), vec, vec, vec,
                  full(wuf), full(wuh), full(wo), full(ln_g), full(ln_b), full(wr), full(br)],
        out_specs=(tok(D),
                   pl.BlockSpec((D // WORD_LANES, tm, LANES), lambda b, i: (0, b * (S // tm) + i, 0)),
                   tok(LANES),
                   pl.BlockSpec((ROW_TILE, tm), lambda b, i: (0, b * (S // tm) + i)),
                   pl.BlockSpec((1, LANES), lambda b, i: (0, 0))),
        scratch_shapes=[pltpu.VMEM((1, LANES), F32)],
        compiler_params=_cparams(("arbitrary", "arbitrary")),
    )(yf, oh, gf, gh, x, g1, sc2, sh2, wuf, wuh, wo, ln_g, ln_b, wr, br)


def _sc_mesh():
    return plsc.VectorSubcoreMesh(core_axis_name="core", subcore_axis_name="subcore")


def _sc_pipeline(body, grid, in_specs, out_specs):
    return pltpu.emit_pipeline(body, grid=grid, in_specs=in_specs, out_specs=out_specs,
                               core_axis_name=("core", "subcore"),
                               dimension_semantics=(pltpu.PARALLEL,) * len(grid))


def _sc_scatter_rows(src, rows_a, rows_b, n_out):
    nj, t = rows_a.shape
    nc = t // LANES

    @pl.kernel(out_type=jax.ShapeDtypeStruct((n_out, LANES), src.dtype), mesh=_sc_mesh(), scratch_types=[])
    def scatter(x_hbm, a_hbm, b_hbm, o_hbm):
        def body(x_vmem, a_vmem, b_vmem):
            pltpu.sync_copy(x_vmem, o_hbm.at[a_vmem.at[0]])
            pltpu.sync_copy(x_vmem, o_hbm.at[b_vmem.at[0]])

        idx = pl.BlockSpec((1, LANES), lambda j, c: (j, c))
        _sc_pipeline(body, (nj, nc), [pl.BlockSpec((LANES, LANES), lambda j, c: (j * nc + c, 0)), idx, idx],
                     [])(x_hbm, a_hbm, b_hbm)

    return scatter(src, rows_a, rows_b)


def _sc_gather_rows(table, rows):
    nr, t = rows.shape
    nc = t // LANES

    @pl.kernel(out_type=jax.ShapeDtypeStruct((nr * t, LANES), table.dtype), mesh=_sc_mesh(), scratch_types=[])
    def gather(x_hbm, i_hbm, o_hbm):
        def body(i_vmem, o_vmem):
            pltpu.sync_copy(x_hbm.at[i_vmem.at[0]], o_vmem)

        _sc_pipeline(body, (nr, nc), [pl.BlockSpec((1, LANES), lambda r, c: (r, c))],
                     [pl.BlockSpec((LANES, LANES), lambda r, c: (r * nc + c, 0))])(i_hbm, o_hbm)

    return gather(table, rows)


def _experts_kernel(te_ref, tn_ref, tb_ref, x_ref, wg_ref, wu_ref, wd_ref, o_ref):
    del tb_ref
    nrows = tn_ref[pl.program_id(0)]

    @pl.when(nrows > 0)
    def _():
        x = _load_chunks(x_ref)
        x = jnp.where(lax.broadcasted_iota(jnp.int32, x.shape, 0) < nrows, x, 0.0).astype(BF16)
        g = jnp.dot(x, wg_ref[...].astype(BF16), preferred_element_type=F32)
        u = jnp.dot(x, wu_ref[...].astype(BF16), preferred_element_type=F32)
        hid = (_silu(g) * u).astype(BF16)
        _store_chunks(o_ref, jnp.dot(hid, wd_ref[...].astype(BF16), preferred_element_type=F32))


def _experts(tile_expert, tile_rows, tile_block, xs, wg, wu, wd, tm):
    E, D, FF = wg.shape
    dt = D // WORD_LANES
    ntiles = tile_expert.shape[0]
    rows = pl.BlockSpec((dt, tm, LANES), lambda i, te, tn, tb: (0, tb[i], 0))
    grid_spec = pltpu.PrefetchScalarGridSpec(
        num_scalar_prefetch=3,
        grid=(ntiles,),
        in_specs=[rows,
                  pl.BlockSpec((None, D, FF), lambda i, te, tn, tb: (te[i], 0, 0)),
                  pl.BlockSpec((None, D, FF), lambda i, te, tn, tb: (te[i], 0, 0)),
                  pl.BlockSpec((None, FF, D), lambda i, te, tn, tb: (te[i], 0, 0))],
        out_specs=rows,
    )
    return pl.pallas_call(
        _experts_kernel,
        out_shape=jax.ShapeDtypeStruct((dt, ntiles * tm, LANES), F32),
        grid_spec=grid_spec,
        compiler_params=_cparams(("arbitrary",)),
    )(tile_expert, tile_rows, tile_block, xs, wg, wu, wd)


def _combine_kernel(yg_ref, x1_ref, ri_ref, g2_ref, lg_ref, lb_ref, o_ref, *, alpha):
    ri = ri_ref[...]
    y = ri[:, 0:1] * _load_chunks(yg_ref.at[0]) + ri[:, 1:2] * _load_chunks(yg_ref.at[1])
    o_ref[...] = _layer_norm(alpha * x1_ref[...] + g2_ref[...] * y, lg_ref[...], lb_ref[...])


def _combine(yg, x1, rinfo, g2, ln_g, ln_b, alpha, tm=256):
    B, S, D = x1.shape
    nb = S // tm
    return pl.pallas_call(
        functools.partial(_combine_kernel, alpha=alpha),
        out_shape=jax.ShapeDtypeStruct((B, S, D), F32),
        grid=(B, nb),
        in_specs=[pl.BlockSpec((2, D // WORD_LANES, tm, LANES), lambda b, i: (0, 0, b * nb + i, 0)),
                  pl.BlockSpec((None, tm, D), lambda b, i: (b, i, 0)),
                  pl.BlockSpec((None, tm, LANES), lambda b, i: (b, i, 0)),
                  pl.BlockSpec((None, 1, D), lambda b, i: (b, 0, 0)),
                  pl.BlockSpec((1, D), lambda b, i: (0, 0)),
                  pl.BlockSpec((1, D), lambda b, i: (0, 0))],
        out_specs=pl.BlockSpec((None, tm, D), lambda b, i: (b, i, 0)),
        compiler_params=_cparams(("parallel", "parallel")),
    )(yg, x1, rinfo, g2, ln_g, ln_b)


def kernel(x, c, w_ada, b_ada, w_in, b_fox_forget, hgrn_lb_logits, hgrn_norm_w, w_up_fox, w_up_hgrn, w_out,
           ln1_g, ln1_b, w_router_group, b_router_group, w_router_expert, b_router_expert,
           w_expert_gate, w_expert_up, w_expert_down, ln2_g, ln2_b):
    B, S, D = x.shape
    depth = w_ada.shape[0]
    assert depth == 1, "single-layer block"
    fox_heads = b_fox_forget.shape[1]
    fox_w = fox_heads * HEAD_DIM
    hgrn_w = hgrn_norm_w.shape[1]
    ngroups = w_router_group.shape[2]
    nexp = w_router_expert.shape[2]
    nper = nexp // ngroups
    alpha = (2 * depth) ** 0.25
    T = B * S

    ada = _ada(c, w_ada[0], b_ada[0])
    sh1, sc1, g1, sh2, sc2, g2 = [a.reshape(B, 1, D) for a in jnp.split(ada, 6, axis=-1)]

    wi = w_in[0]
    o_ff = 3 * fox_w
    w_fox = jnp.pad(wi[:, :o_ff + fox_heads], ((0, 0), (0, LANES - fox_heads))).astype(BF16)
    w_rest = wi[:, o_ff + fox_heads:].astype(BF16)
    widths = [fox_w, fox_w, fox_w, LANES, hgrn_w, hgrn_w, hgrn_w, hgrn_w, D, D]
    segs, off = [], 0
    for n, w in enumerate(widths):
        if n == 4:
            off = 0
        segs.append((off, off + w))
        off += w
    fq, fk, fv, ffp, hq, hf, hi, hg, gf, gh = _inproj(x, sc1, sh1, w_fox, w_rest, segs)

    bias_p = jnp.zeros((1, LANES), F32).at[0, :fox_heads].set(b_fox_forget[0])
    cum = _foxcum(ffp, bias_p)
    y_fox = _fox(fq, fk, fv, cum)

    o_h = _hgrn(hq, hf, hi, hg, hgrn_lb_logits, hgrn_norm_w[0])

    wr = jnp.zeros((D, LANES), F32).at[:, :ngroups].set(w_router_group[0]).at[:, ngroups:ngroups + nexp].set(
        w_router_expert[0])
    wr_hi = lax.bitcast_convert_type(lax.bitcast_convert_type(wr, jnp.uint32) & jnp.uint32(0xFFFF0000), F32)
    wr = jnp.concatenate([wr_hi.astype(BF16), (wr - wr_hi).astype(BF16)], axis=1)
    br = jnp.zeros((1, LANES), F32).at[0, :ngroups].set(b_router_group[0]).at[0, ngroups:ngroups + nexp].set(
        b_router_expert[0])
    x1, h2, rinfo, fields, counts = _mix(
        y_fox, o_h, gf, gh, x, g1, sc2, sh2,
        w_up_fox[0].astype(BF16), w_up_hgrn[0].astype(BF16), w_out[0].astype(BF16),
        ln1_g[0].reshape(1, D), ln1_b[0].reshape(1, D), wr, br, alpha, ngroups, nper)

    tm_e = 512
    dt = D // WORD_LANES
    ntiles = (2 * T) // tm_e + nexp
    nslots = ntiles * tm_e
    cnt = counts[0, :nexp].astype(jnp.int32)
    padded = ((cnt + tm_e - 1) // tm_e) * tm_e
    ends = jnp.cumsum(padded)
    starts = ends - padded
    eid = fields[2:4].astype(jnp.int32)
    rank = fields[4:6].astype(jnp.int32)
    first = jnp.sum(jnp.where(eid[None] == jnp.arange(nexp, dtype=jnp.int32)[:, None, None],
                              starts[:, None, None], 0), axis=0)
    pos = first + rank
    tile_start = jnp.arange(ntiles, dtype=jnp.int32) * tm_e
    tile_block = jnp.minimum(jnp.arange(ntiles, dtype=jnp.int32), ends[-1] // tm_e - 1)
    tile_expert = jnp.minimum(jnp.sum((tile_start[:, None] >= ends[None, :]).astype(jnp.int32), axis=1), nexp - 1)
    tile_rows = jnp.clip(starts[tile_expert] + cnt[tile_expert] - tile_start, 0, tm_e)
    tile_expert = tile_expert[tile_block]
    rows = pos[:, None, :] + (jnp.arange(dt, dtype=jnp.int32) * nslots)[None, :, None]

    xs = _sc_scatter_rows(h2.reshape(dt * T, LANES), rows[0], rows[1], dt * nslots)
    ys = _experts(tile_expert, tile_rows, tile_block, xs.reshape(dt, nslots, LANES),
                  w_expert_gate[0], w_expert_up[0], w_expert_down[0], tm_e)
    yg = _sc_gather_rows(ys.reshape(dt * nslots, LANES), rows.reshape(2 * dt, T))
    return _combine(yg.reshape(2, dt, T, LANES), x1, rinfo, g2,
                    ln2_g[0].reshape(1, D), ln2_b[0].reshape(1, D), alpha)
```

```python
import functools

import jax
import jax.numpy as jnp
from jax import lax
from jax.experimental import pallas as pl
from jax.experimental.pallas import tpu as pltpu
from jax.experimental.pallas import tpu_sc as plsc

F32 = jnp.float32
BF16 = jnp.bfloat16
HIGHEST = lax.Precision.HIGHEST

LANES = 128
HEAD_DIM = 64
LN_EPS = 1e-5
RMS_EPS = 1e-6
LOG2E = 1.4426950408889634
NEG_BIG = -1e30
HCHUNK = 16
HBLOCK = 64
HGRN_SAFE_EXP = 60.0
ROW_TILE = 8
WORD_LANES = 2 * LANES
VMEM_LIMIT = 56 * 1024 * 1024


def _cparams(sem, vmem=VMEM_LIMIT):
    return pltpu.CompilerParams(dimension_semantics=sem, vmem_limit_bytes=vmem)


def _sigmoid(x):
    return 0.5 * jnp.tanh(0.5 * x) + 0.5


def _silu(x):
    return x * _sigmoid(x)


def _ada_kernel(c_ref, w_ref, b_ref, o_ref):
    c = c_ref[...]
    o_ref[...] = jnp.dot(_silu(c), w_ref[...], precision=HIGHEST,
                         preferred_element_type=F32) + b_ref[...]


def _ada(c, w_ada, b_ada):
    B, D = c.shape
    N = w_ada.shape[1]
    tn = N // 2
    return pl.pallas_call(
        _ada_kernel,
        out_shape=jax.ShapeDtypeStruct((B, N), F32),
        grid=(N // tn,),
        in_specs=[pl.BlockSpec((B, D), lambda j: (0, 0)),
                  pl.BlockSpec((D, tn), lambda j: (0, j)),
                  pl.BlockSpec((1, tn), lambda j: (0, j))],
        out_specs=pl.BlockSpec((B, tn), lambda j: (0, j)),
        compiler_params=_cparams(("arbitrary",)),
    )(c, w_ada, b_ada.reshape(1, N))


N_FOX_SEGS = 4


FF_SEG = 3


def _inproj_kernel(x_ref, sc_ref, sh_ref, fb_ref, wf_ref, wr_ref,
                   fq_ref, fk_ref, fv_ref, fc_ref, hq_ref, hf_ref, hi_ref, hg_ref, gf_ref, gh_ref,
                   carry_sc, *, segs, q_scale):
    @pl.when(pl.program_id(1) == 0)
    def _():
        carry_sc[...] = jnp.zeros_like(carry_sc)

    h = (x_ref[...] * (1.0 + sc_ref[...]) + sh_ref[...]).astype(BF16)
    outs = (fq_ref, fk_ref, fv_ref, fc_ref, hq_ref, hf_ref, hi_ref, hg_ref, gf_ref, gh_ref)
    for idx, (o_ref, (a, b)) in enumerate(zip(outs, segs)):
        w_ref = wf_ref if idx < N_FOX_SEGS else wr_ref
        r = jnp.dot(h, w_ref[:, a:b], preferred_element_type=F32)
        if idx == 0:
            r = r * q_scale
        if idx == FF_SEG:
            tm = r.shape[0]
            z = r + fb_ref[...]
            lf = jnp.minimum(z, 0.0) - jnp.log(1.0 + jnp.exp(-jnp.abs(z)))
            lower = (lax.broadcasted_iota(jnp.int32, (tm, tm), 0)
                     >= lax.broadcasted_iota(jnp.int32, (tm, tm), 1)).astype(F32)
            cum = jnp.dot(lower, lf, precision=HIGHEST, preferred_element_type=F32) + carry_sc[...]
            carry_sc[...] = cum[tm - 1:tm, :]
            r = cum * LOG2E
        o_ref[...] = r.astype(o_ref.dtype)


def _inproj(x, sc1, sh1, fbias, w_fox, w_rest, segs, tm=256):
    B, S, D = x.shape
    widths = [b - a for a, b in segs]
    dtypes = [BF16, BF16, BF16, F32, BF16, F32, BF16, BF16, BF16, BF16]
    out_shape = tuple(jax.ShapeDtypeStruct((B, S, w), dt) for w, dt in zip(widths, dtypes))
    out_specs = tuple(pl.BlockSpec((None, tm, w), lambda b, i: (b, i, 0)) for w in widths)
    vec = pl.BlockSpec((None, 1, D), lambda b, i: (b, 0, 0))
    return pl.pallas_call(
        functools.partial(_inproj_kernel, segs=tuple(segs), q_scale=HEAD_DIM ** -0.5 * LOG2E),
        out_shape=out_shape,
        grid=(B, S // tm),
        in_specs=[pl.BlockSpec((None, tm, D), lambda b, i: (b, i, 0)), vec, vec,
                  pl.BlockSpec((1, LANES), lambda b, i: (0, 0)),
                  pl.BlockSpec(w_fox.shape, lambda b, i: (0, 0)),
                  pl.BlockSpec(w_rest.shape, lambda b, i: (0, 0))],
        out_specs=out_specs,
        scratch_shapes=[pltpu.VMEM((1, LANES), F32)],
        compiler_params=_cparams(("parallel", "arbitrary")),
    )(x, sc1, sh1, fbias, w_fox, w_rest)


NCUM = 3


def _fox_kernel(q_ref, k_ref, v_ref, c_ref, o_ref, ka_sc, kb_sc, va_sc, vb_sc, *, tq, tk):
    p = pl.program_id(1)
    qi = pl.program_id(2)
    S = k_ref.shape[0]

    @pl.when(qi == 0)
    def _():
        lane = lax.broadcasted_iota(jnp.int32, (S, LANES), 1)
        rr = lax.broadcasted_iota(jnp.int32, (LANES, LANES), 0)
        cc = lax.broadcasted_iota(jnp.int32, (LANES, LANES), 1)
        rest = c_ref[...]
        placed = jnp.zeros((S, LANES), F32)
        for i in range(NCUM):
            piece = rest.astype(BF16)
            rest = rest - piece.astype(F32)
            sel = ((rr == 2 * p) & (cc == HEAD_DIM + i)) | ((rr == 2 * p + 1) & (cc == i))
            placed = placed + jnp.dot(piece, jnp.where(sel, 1.0, 0.0).astype(BF16), preferred_element_type=F32)
        k2 = k_ref[...].astype(F32)
        ka_sc[...] = jnp.where(lane < HEAD_DIM, k2, -placed).astype(BF16)
        kb_sc[...] = jnp.where(lane >= HEAD_DIM, k2, -placed).astype(BF16)
        vt = v_ref[...].astype(F32).T
        row = lax.broadcasted_iota(jnp.int32, (LANES, S), 0)
        va_sc[...] = jnp.where(row < HEAD_DIM, vt, jnp.where(row == HEAD_DIM, 1.0, 0.0)).astype(BF16)
        vb_sc[...] = jnp.where(row >= HEAD_DIM, vt, jnp.where(row == 0, 1.0, 0.0)).astype(BF16)

    q2 = q_ref[...].astype(F32)
    qlane = lax.broadcasted_iota(jnp.int32, (tq, LANES), 1)
    qa = jnp.where(qlane < HEAD_DIM, q2, jnp.where(qlane < HEAD_DIM + NCUM, 1.0, 0.0)).astype(BF16)
    qb = jnp.where(qlane >= HEAD_DIM, q2, jnp.where(qlane < NCUM, 1.0, 0.0)).astype(BF16)
    nsub = tq // tk

    def block(k0, carry, diag_off):
        q0 = 0 if diag_off is None else diag_off
        out = []
        for ksc, vsc, qh, (m, acc) in ((ka_sc, va_sc, qa, carry[:2]), (kb_sc, vb_sc, qb, carry[2:])):
            st = lax.dot_general(ksc[pl.ds(k0, tk), :], qh[q0:, :], (((1,), (1,)), ((), ())),
                                 preferred_element_type=F32)
            if diag_off is not None:
                st = jnp.where(lax.broadcasted_iota(jnp.int32, st.shape, 0)
                               <= lax.broadcasted_iota(jnp.int32, st.shape, 1), st, NEG_BIG)
            m_old = m[:, q0:]
            m_new = jnp.maximum(m_old, jnp.max(st, axis=0, keepdims=True))
            pt = jnp.exp2(st - m_new).astype(BF16)
            acc_new = (jnp.exp2(m_old - m_new) * acc[:, q0:]
                       + jnp.dot(vsc[:, pl.ds(k0, tk)], pt, preferred_element_type=F32))
            if q0:
                m_new = jnp.concatenate([m[:, :q0], m_new], axis=1)
                acc_new = jnp.concatenate([acc[:, :q0], acc_new], axis=1)
            out += [m_new, acc_new]
        return tuple(out)

    def group(j, carry):
        k0 = pl.multiple_of(j * (nsub * tk), nsub * tk)
        for u in range(nsub):
            carry = block(k0 + u * tk, carry, None)
        return carry

    m0 = jnp.full((1, tq), NEG_BIG, F32)
    a0 = jnp.zeros((LANES, tq), F32)
    carry = lax.fori_loop(0, qi, group, (m0, a0, m0, a0))
    for d in range(nsub):
        carry = block(pl.multiple_of(qi * tq + d * tk, tk), carry, d * tk)
    _, aa, _, ab = carry
    row = lax.broadcasted_iota(jnp.int32, (LANES, tq), 0)
    ot = jnp.where(row < HEAD_DIM, aa * (1.0 / aa[HEAD_DIM:HEAD_DIM + 1, :]), ab * (1.0 / ab[0:1, :]))
    o_ref[...] = ot.T.astype(o_ref.dtype)


def _fox(fq, fk, fv, cum, tq=1024, tk=512):
    B, S, W = fq.shape
    assert tq % (2 * tk) == 0 and S % tq == 0
    npairs = W // LANES
    return pl.pallas_call(
        functools.partial(_fox_kernel, tq=tq, tk=tk),
        out_shape=jax.ShapeDtypeStruct((B, S, W), BF16),
        grid=(B, npairs, S // tq),
        in_specs=[pl.BlockSpec((None, tq, LANES), lambda b, p, i: (b, i, p)),
                  pl.BlockSpec((None, S, LANES), lambda b, p, i: (b, 0, p)),
                  pl.BlockSpec((None, S, LANES), lambda b, p, i: (b, 0, p)),
                  pl.BlockSpec((None, S, LANES), lambda b, p, i: (b, 0, 0))],
        out_specs=pl.BlockSpec((None, tq, LANES), lambda b, p, i: (b, i, p)),
        scratch_shapes=[pltpu.VMEM((S, LANES), BF16), pltpu.VMEM((S, LANES), BF16),
                        pltpu.VMEM((LANES, S), BF16), pltpu.VMEM((LANES, S), BF16)],
        compiler_params=_cparams(("parallel", "parallel", "arbitrary")),
    )(fq, fk, fv, cum)


def _hgrn_kernel(hq_ref, hf_ref, hi_ref, hg_ref, lb_ref, nw_ref, o_ref,
                 a_sc, b_sc, kk_sc, qq_sc, o_sc, w1_sc, w2_sc, w3_sc, w4_sc, w5_sc, w6_sc,
                 p_sc, st16_sc, dec_sc, st64_sc):
    S = hq_ref.shape[0]
    C = HCHUNK
    nchunks = S // C
    BLK = HBLOCK
    nblk = S // BLK

    lg = lb_ref[...]
    e = jnp.exp(lg - jnp.max(lg, axis=0, keepdims=True))
    lb = e[0:1, :] / jnp.sum(e, axis=0, keepdims=True)

    f = lb + (1.0 - lb) * _sigmoid(hf_ref[...])
    lf = jnp.log(f)
    kk_sc[...] = 1.0 - f
    qq_sc[...] = _silu(hq_ref[...].astype(F32))

    row = lax.broadcasted_iota(jnp.int32, (S, LANES), 0)
    rmod = row & (C - 1)
    a = lf
    d = 1
    while d < C:
        a = a + jnp.where(rmod >= d, pltpu.roll(a, d, axis=0), 0.0)
        d *= 2
    a3 = a.reshape(nchunks, C, LANES)
    alast = jnp.broadcast_to(a3[:, C - 1:C, :], (nchunks, C, LANES)).reshape(S, LANES)
    bmod = row & (BLK - 1)
    tot = alast
    d = C
    while d < BLK:
        tot = tot + jnp.where(bmod >= d, pltpu.roll(tot, d, axis=0), 0.0)
        d *= 2
    b = a + (tot - alast)
    b3 = b.reshape(nblk, BLK, LANES)
    blast = jnp.broadcast_to(b3[:, BLK - 1:BLK, :], (nblk, BLK, LANES)).reshape(S, LANES)
    a_sc[...] = a
    b_sc[...] = b
    safe = jnp.max(-blast) <= HGRN_SAFE_EXP

    lane = lax.broadcasted_iota(jnp.int32, (C, LANES), 1)
    sr = lax.broadcasted_iota(jnp.int32, (LANES, LANES), 0)
    scn = lax.broadcasted_iota(jnp.int32, (LANES, LANES), 1)
    same_head = (sr // HEAD_DIM) == (scn // HEAD_DIM)

    @pl.when(safe)
    def _factorised():
        qa_sc, qb_sc, kh_sc, ke_sc, qd_sc, k2_sc = w1_sc, w2_sc, w3_sc, w4_sc, w5_sc, w6_sc
        SB = 2 * BLK
        nsb = S // SB
        bb = b_sc[...]
        dblk = jnp.exp(bb.reshape(nblk, BLK, LANES)[:, BLK - 1:BLK, :])
        dfull = jnp.broadcast_to(dblk, (nblk, BLK, LANES)).reshape(S, LANES)
        second = (row & BLK) != 0
        d_prev = pltpu.roll(dfull, BLK, axis=0)
        d_next = pltpu.roll(dfull, S - BLK, axis=0)
        qh = qq_sc[...] * jnp.exp(bb)
        slane = lax.broadcasted_iota(jnp.int32, (S, LANES), 1)
        qa_sc[...] = jnp.where(slane < HEAD_DIM, qh, 0.0).astype(BF16)
        qb_sc[...] = jnp.where(slane >= HEAD_DIM, qh, 0.0).astype(BF16)
        qd_sc[...] = (qh * jnp.where(second, d_prev, 1.0)).astype(BF16)
        kh = kk_sc[...] * jnp.exp(-bb)
        kh_sc[...] = kh.astype(BF16)
        ke = kh * dfull
        ke_sc[...] = ke.astype(BF16)
        k2_sc[...] = (ke * jnp.where(second, 1.0, d_next)).astype(BF16)
        d3 = dfull.reshape(nsb, SB, LANES)
        dec_sc[pl.ds(0, nsb), :] = d3[:, 0, :] * d3[:, BLK, :]
        unroll = 4
        tn = (((0,), (0,)), ((), ()))
        nt = (((1,), (1,)), ((), ()))

        def scan(g, st):
            for u in range(unroll):
                i = g * unroll + u
                r0 = pl.multiple_of(i * SB, SB)
                st64_sc[i] = st.astype(BF16)
                upd = lax.dot_general(hi_ref[pl.ds(r0, SB), :], k2_sc[pl.ds(r0, SB), :], tn,
                                      preferred_element_type=F32)
                st = st * dec_sc[pl.ds(i, 1), :] + jnp.where(same_head, upd, 0.0)
            return st

        lax.fori_loop(0, nsb // unroll, scan, jnp.zeros((LANES, LANES), F32))

        r = lax.broadcasted_iota(jnp.int32, (2 * SB, 2 * SB), 0)
        c = lax.broadcasted_iota(jnp.int32, (2 * SB, 2 * SB), 1)
        t = r & (SB - 1)
        visible = (((c < SB) & ((t & BLK) == (c & BLK)) & ((t & (BLK - 1)) >= (c & (BLK - 1))))
                   | ((c >= SB) & (c < SB + BLK) & (t >= BLK)))
        plane = lax.broadcasted_iota(jnp.int32, (SB, LANES), 1)
        pad = jnp.zeros((BLK, LANES), BF16)

        def readout(g, _):
            for u in range(unroll):
                i = g * unroll + u
                r0 = pl.multiple_of(i * SB, SB)
                vb = hi_ref[pl.ds(r0, SB), :]
                q2 = jnp.concatenate([qa_sc[pl.ds(r0, SB), :], qb_sc[pl.ds(r0, SB), :]], axis=0)
                kext = jnp.concatenate([kh_sc[pl.ds(r0, SB), :], ke_sc[pl.ds(r0, BLK), :], pad], axis=0)
                vext = jnp.concatenate([vb, vb[:BLK], pad], axis=0)
                sc = lax.dot_general(q2, kext, nt, preferred_element_type=F32)
                sc = jnp.where(visible, sc, 0.0).astype(BF16)
                out = jnp.dot(sc, vext, preferred_element_type=F32)
                o_inter = lax.dot_general(qd_sc[pl.ds(r0, SB), :], st64_sc[i], nt, preferred_element_type=F32)
                o_sc[pl.ds(r0, SB), :] = jnp.where(plane < HEAD_DIM, out[:SB], out[SB:]) + o_inter
            return 0

        lax.fori_loop(0, nsb // unroll, readout, 0)

    @pl.when(jnp.logical_not(safe))
    def _direct():
        qt_sc, kt_sc, s_sc, a2_sc = w1_sc, w2_sc, w3_sc, b_sc
        aa = a_sc[...]
        al = jnp.broadcast_to(aa.reshape(nchunks, C, LANES)[:, C - 1:C, :], (nchunks, C, LANES)).reshape(S, LANES)
        qt_sc[...] = (qq_sc[...] * jnp.exp(aa)).astype(BF16)
        kt_sc[...] = (kk_sc[...] * jnp.exp(al - aa)).astype(BF16)
        dec_sc[...] = jnp.exp(aa.reshape(nchunks, C, LANES)[:, C - 1, :])
        a2_sc[...] = aa * LOG2E
        trow = lax.broadcasted_iota(jnp.int32, (C, LANES), 0)

        def gen(c, _):
            r0 = pl.multiple_of(c * C, C)
            ac = a2_sc[pl.ds(r0, C), :]
            qc = qq_sc[pl.ds(r0, C), :]
            kc = kk_sc[pl.ds(r0, C), :]
            half = C // 2
            for s in range(C):
                if s < half:
                    dec = jnp.exp2(jnp.where(trow >= s, ac - ac[s:s + 1, :], NEG_BIG))
                    p = qc * (kc[s:s + 1, :] * dec)
                else:
                    dec = jnp.exp2(jnp.where(trow[half:] >= s, ac[half:] - ac[s:s + 1, :], NEG_BIG))
                    p = jnp.concatenate([jnp.zeros((half, LANES), F32), qc[half:] * (kc[s:s + 1, :] * dec)],
                                        axis=0)
                p_sc[pl.ds(r0, C), s * LANES:(s + 1) * LANES] = p.astype(BF16)
            return 0

        lax.fori_loop(0, nchunks, gen, 0)

        er = lax.broadcasted_iota(jnp.int32, (C * LANES, LANES), 0)
        ec = lax.broadcasted_iota(jnp.int32, (C * LANES, LANES), 1)
        emat = (ec == ((er & (LANES - 1)) // HEAD_DIM) * C + er // LANES).astype(BF16)
        rb = 256

        def red(i, _):
            r0 = pl.multiple_of(i * rb, rb)
            s_sc[pl.ds(r0, rb), :] = jnp.dot(p_sc[pl.ds(r0, rb), :], emat,
                                             preferred_element_type=F32).astype(BF16)
            return 0

        lax.fori_loop(0, S // rb, red, 0)

        unroll = 16

        def scan(g, st):
            for u in range(unroll):
                c = g * unroll + u
                r0 = pl.multiple_of(c * C, C)
                st16_sc[c] = st.astype(BF16)
                upd = lax.dot_general(hi_ref[pl.ds(r0, C), :], kt_sc[pl.ds(r0, C), :],
                                      (((0,), (0,)), ((), ())), preferred_element_type=F32)
                st = st * dec_sc[pl.ds(c, 1), :] + jnp.where(same_head, upd, 0.0)
            return st

        lax.fori_loop(0, nchunks // unroll, scan, jnp.zeros((LANES, LANES), F32))

        def readout(g, _):
            for u in range(unroll):
                c = g * unroll + u
                r0 = pl.multiple_of(c * C, C)
                vc = hi_ref[pl.ds(r0, C), :]
                o_inter = lax.dot_general(qt_sc[pl.ds(r0, C), :], st16_sc[c],
                                          (((1,), (1,)), ((), ())), preferred_element_type=F32)
                v2 = jnp.concatenate([jnp.where(lane < HEAD_DIM, vc, jnp.zeros_like(vc)),
                                      jnp.where(lane >= HEAD_DIM, vc, jnp.zeros_like(vc))], axis=0)
                o_intra = jnp.dot(s_sc[pl.ds(r0, C), :][:, :2 * C], v2, preferred_element_type=F32)
                o_sc[pl.ds(r0, C), :] = o_inter + o_intra
            return 0

        lax.fori_loop(0, nchunks // unroll, readout, 0)

    o = o_sc[...]
    ones_head = jnp.where(same_head, 1.0, 0.0).astype(BF16)
    sq = o * o
    sq_top = pltpu.bitcast(pltpu.bitcast(sq, jnp.uint32) & jnp.uint32(0xFFFF0000), F32)
    ms = (jnp.dot(sq_top.astype(BF16), ones_head, preferred_element_type=F32)
          + jnp.dot((sq - sq_top).astype(BF16), ones_head, preferred_element_type=F32)) * (1.0 / HEAD_DIM)
    y = o * lax.rsqrt(ms + RMS_EPS) * nw_ref[...]
    o_ref[...] = (y * _silu(hg_ref[...].astype(F32))).astype(o_ref.dtype)


def _hgrn(hq, hf, hi, hg, lb_logits, norm_w):
    B, S, W = hq.shape
    npairs = W // LANES
    nrows = lb_logits.shape[0]
    seq = pl.BlockSpec((None, S, LANES), lambda b, p: (b, 0, p))
    return pl.pallas_call(
        _hgrn_kernel,
        out_shape=jax.ShapeDtypeStruct((B, S, W), BF16),
        grid=(B, npairs),
        in_specs=[seq, seq, seq, seq,
                  pl.BlockSpec((nrows, LANES), lambda b, p: (0, p)),
                  pl.BlockSpec((1, LANES), lambda b, p: (0, p))],
        out_specs=seq,
        scratch_shapes=[pltpu.VMEM((S, LANES), F32),
                        pltpu.VMEM((S, LANES), F32),
                        pltpu.VMEM((S, LANES), F32),
                        pltpu.VMEM((S, LANES), F32),
                        pltpu.VMEM((S, LANES), F32),
                        pltpu.VMEM((S, LANES), BF16),
                        pltpu.VMEM((S, LANES), BF16),
                        pltpu.VMEM((S, LANES), BF16),
                        pltpu.VMEM((S, LANES), BF16),
                        pltpu.VMEM((S, LANES), BF16),
                        pltpu.VMEM((S, LANES), BF16),
                        pltpu.VMEM((S, HCHUNK * LANES), BF16),
                        pltpu.VMEM((S // HCHUNK, LANES, LANES), BF16),
                        pltpu.VMEM((S // HCHUNK, LANES), F32),
                        pltpu.VMEM((S // HBLOCK, LANES, LANES), BF16)],
        compiler_params=_cparams(("parallel", "parallel")),
    )(hq, hf, hi, hg, lb_logits, norm_w.reshape(1, W))


def _layer_norm(v, g, b):
    mu = jnp.mean(v, axis=-1, keepdims=True)
    d = v - mu
    var = jnp.mean(d * d, axis=-1, keepdims=True)
    return d * lax.rsqrt(var + LN_EPS) * g + b


def _bf16_bits(x):
    u = pltpu.bitcast(x, jnp.uint32)
    return (u + jnp.uint32(0x7FFF) + ((u >> 16) & jnp.uint32(1))) & jnp.uint32(0xFFFF0000)


def _store_chunks(ref, val):
    n = ref.shape[0]
    for j in range(n):
        lo = _bf16_bits(val[:, j * LANES:(j + 1) * LANES]) >> 16
        hi = _bf16_bits(val[:, (j + n) * LANES:(j + n + 1) * LANES])
        ref[j] = pltpu.bitcast(lo | hi, F32)


def _load_chunks(ref):
    words = [pltpu.bitcast(ref[j], jnp.uint32) for j in range(ref.shape[0])]
    lo = [pltpu.bitcast(w << 16, F32) for w in words]
    hi = [pltpu.bitcast(w & jnp.uint32(0xFFFF0000), F32) for w in words]
    return jnp.concatenate(lo + hi, axis=1)


def _mix_kernel(yf_ref, oh_ref, gf_ref, gh_ref, x_ref, g1_ref, sc2_ref, sh2_ref,
                wuf_ref, wuh_ref, wo_ref, lg_ref, lbias_ref, wr_ref, br_ref,
                x1_ref, h2_ref, ri_ref, rt_ref, cnt_ref, carry_sc, *, alpha, ngroups, nper):
    first = (pl.program_id(0) == 0) & (pl.program_id(1) == 0)

    @pl.when(first)
    def _():
        carry_sc[...] = jnp.zeros_like(carry_sc)

    tm = x_ref.shape[0]
    yf = jnp.dot(yf_ref[...], wuf_ref[...], preferred_element_type=F32)
    yh = jnp.dot(oh_ref[...], wuh_ref[...], preferred_element_type=F32)
    merged = _sigmoid(gf_ref[...].astype(F32)) * yf + _sigmoid(gh_ref[...].astype(F32)) * yh
    y = jnp.dot(merged.astype(BF16), wo_ref[...], preferred_element_type=F32)
    x1 = _layer_norm(alpha * x_ref[...] + g1_ref[...] * y, lg_ref[...], lbias_ref[...])
    x1_ref[...] = x1
    h2 = x1 * (1.0 + sc2_ref[...]) + sh2_ref[...]
    _store_chunks(h2_ref, h2)

    h_top = pltpu.bitcast(pltpu.bitcast(h2, jnp.uint32) & jnp.uint32(0xFFFF0000), F32)
    h_hi = h_top.astype(BF16)
    h_lo = (h2 - h_top).astype(BF16)
    hh = jnp.dot(h_hi, wr_ref[...], preferred_element_type=F32)
    logits = (hh[:, :LANES] + hh[:, LANES:]
              + jnp.dot(h_lo, wr_ref[:, :LANES], preferred_element_type=F32)) + br_ref[...]
    lane = lax.broadcasted_iota(jnp.int32, (tm, LANES), 1)
    big = jnp.int32(1 << 20)

    def argmax_first(vals, mask):
        mx = jnp.max(jnp.where(mask, vals, -jnp.inf), axis=1, keepdims=True)
        idx = jnp.min(jnp.where(mask & (vals == mx), lane, big), axis=1, keepdims=True)
        return mx, idx

    gmask = lane < ngroups
    gmax = jnp.max(jnp.where(gmask, logits, -jnp.inf), axis=1, keepdims=True)
    gexp = jnp.where(gmask, jnp.exp(logits - gmax), 0.0)
    gprob = gexp / jnp.sum(gexp, axis=1, keepdims=True)
    g_w, g_idx = argmax_first(gprob, gmask)

    lo = ngroups + g_idx * nper
    emask = (lane >= lo) & (lane < lo + nper)
    emax = jnp.max(jnp.where(emask, logits, -jnp.inf), axis=1, keepdims=True)
    eexp = jnp.where(emask, jnp.exp(logits - emax), 0.0)
    eprob = eexp / jnp.sum(eexp, axis=1, keepdims=True)
    p0, i0 = argmax_first(eprob, emask)
    p1, i1 = argmax_first(eprob, emask & (lane != i0))
    den = p0 + p1
    w0 = p0 / den * g_w
    w1 = p1 / den * g_w
    e0 = i0 - ngroups
    e1 = i1 - ngroups

    oh = ((lane == e0) | (lane == e1)).astype(F32)
    r = lax.broadcasted_iota(jnp.int32, (tm, tm), 0)
    c = lax.broadcasted_iota(jnp.int32, (tm, tm), 1)
    strict_lower = (c < r).astype(BF16)
    before = jnp.dot(strict_lower, oh.astype(BF16), preferred_element_type=F32) + carry_sc[...]
    rank0 = jnp.sum(jnp.where(lane == e0, before, 0.0), axis=1, keepdims=True)
    rank1 = jnp.sum(jnp.where(lane == e1, before, 0.0), axis=1, keepdims=True)
    carry_sc[...] = carry_sc[...] + jnp.sum(oh, axis=0, keepdims=True)
    cnt_ref[...] = carry_sc[...]

    info = jnp.where(lane == 0, w0, 0.0)
    info = jnp.where(lane == 1, w1, info)
    info = jnp.where(lane == 2, e0.astype(F32), info)
    info = jnp.where(lane == 3, e1.astype(F32), info)
    info = jnp.where(lane == 4, rank0, info)
    info = jnp.where(lane == 5, rank1, info)
    ri_ref[...] = info
    rt_ref[...] = info.T[:ROW_TILE, :]


def _mix(yf, oh, gf, gh, x, g1, sc2, sh2, wuf, wuh, wo, ln_g, ln_b, wr, br, alpha, ngroups, nper, tm=512):
    B, S, D = x.shape
    W = yf.shape[2]
    tok = lambda w: pl.BlockSpec((None, tm, w), lambda b, i: (b, i, 0))
    vec = pl.BlockSpec((None, 1, D), lambda b, i: (b, 0, 0))
    full = lambda a: pl.BlockSpec(a.shape, lambda b, i: (0,) * a.ndim)
    return pl.pallas_call(
        functools.partial(_mix_kernel, alpha=alpha, ngroups=ngroups, nper=nper),
        out_shape=(jax.ShapeDtypeStruct((B, S, D), F32),
                   jax.ShapeDtypeStruct((D // WORD_LANES, B * S, LANES), F32),
                   jax.ShapeDtypeStruct((B, S, LANES), F32),
                   jax.ShapeDtypeStruct((ROW_TILE, B * S), F32),
                   jax.ShapeDtypeStruct((1, LANES), F32)),
        grid=(B, S // tm),
        in_specs=[tok(W), tok(W), tok(D), tok(D), tok(D), vec, vec, vec,
                  full(wuf), full(wuh), full(wo), full(ln_g), full(ln_b), full(wr), full(br)],
        out_specs=(tok(D),
                   pl.BlockSpec((D // WORD_LANES, tm, LANES), lambda b, i: (0, b * (S // tm) + i, 0)),
                   tok(LANES),
                   pl.BlockSpec((ROW_TILE, tm), lambda b, i: (0, b * (S // tm) + i)),
                   pl.BlockSpec((1, LANES), lambda b, i: (0, 0))),
        scratch_shapes=[pltpu.VMEM((1, LANES), F32)],
        compiler_params=_cparams(("arbitrary", "arbitrary")),
    )(yf, oh, gf, gh, x, g1, sc2, sh2, wuf, wuh, wo, ln_g, ln_b, wr, br)


def _sc_mesh():
    return plsc.VectorSubcoreMesh(core_axis_name="core", subcore_axis_name="subcore")


def _sc_pipeline(body, grid, in_specs, out_specs):
    return pltpu.emit_pipeline(body, grid=grid, in_specs=in_specs, out_specs=out_specs,
                               core_axis_name=("core", "subcore"),
                               dimension_semantics=(pltpu.PARALLEL,) * len(grid))


def _sc_scatter_rows(src, rows_a, rows_b, n_out):
    nj, t = rows_a.shape
    nc = t // LANES

    @pl.kernel(out_type=jax.ShapeDtypeStruct((n_out, LANES), src.dtype), mesh=_sc_mesh(), scratch_types=[])
    def scatter(x_hbm, a_hbm, b_hbm, o_hbm):
        def body(x_vmem, a_vmem, b_vmem):
            pltpu.sync_copy(x_vmem, o_hbm.at[a_vmem.at[0]])
            pltpu.sync_copy(x_vmem, o_hbm.at[b_vmem.at[0]])

        idx = pl.BlockSpec((1, LANES), lambda j, c: (j, c))
        _sc_pipeline(body, (nj, nc), [pl.BlockSpec((LANES, LANES), lambda j, c: (j * nc + c, 0)), idx, idx],
                     [])(x_hbm, a_hbm, b_hbm)

    return scatter(src, rows_a, rows_b)


def _sc_gather_rows(table, rows):
    nr, t = rows.shape
    nc = t // LANES

    @pl.kernel(out_type=jax.ShapeDtypeStruct((nr * t, LANES), table.dtype), mesh=_sc_mesh(), scratch_types=[])
    def gather(x_hbm, i_hbm, o_hbm):
        def body(i_vmem, o_vmem):
            pltpu.sync_copy(x_hbm.at[i_vmem.at[0]], o_vmem)

        _sc_pipeline(body, (nr, nc), [pl.BlockSpec((1, LANES), lambda r, c: (r, c))],
                     [pl.BlockSpec((LANES, LANES), lambda r, c: (r * nc + c, 0))])(i_hbm, o_hbm)

    return gather(table, rows)


def _experts_kernel(te_ref, tn_ref, tb_ref, x_ref, wg_ref, wu_ref, wd_ref, o_ref):
    del tb_ref
    nrows = tn_ref[pl.program_id(0)]

    @pl.when(nrows > 0)
    def _():
        x = _load_chunks(x_ref)
        x = jnp.where(lax.broadcasted_iota(jnp.int32, x.shape, 0) < nrows, x, 0.0).astype(BF16)
        g = jnp.dot(x, wg_ref[...].astype(BF16), preferred_element_type=F32)
        u = jnp.dot(x, wu_ref[...].astype(BF16), preferred_element_type=F32)
        hid = (_silu(g) * u).astype(BF16)
        _store_chunks(o_ref, jnp.dot(hid, wd_ref[...].astype(BF16), preferred_element_type=F32))


def _experts(tile_expert, tile_rows, tile_block, xs, wg, wu, wd, tm):
    E, D, FF = wg.shape
    dt = D // WORD_LANES
    ntiles = tile_expert.shape[0]
    rows = pl.BlockSpec((dt, tm, LANES), lambda i, te, tn, tb: (0, tb[i], 0))
    grid_spec = pltpu.PrefetchScalarGridSpec(
        num_scalar_prefetch=3,
        grid=(ntiles,),
        in_specs=[rows,
                  pl.BlockSpec((None, D, FF), lambda i, te, tn, tb: (te[i], 0, 0)),
                  pl.BlockSpec((None, D, FF), lambda i, te, tn, tb: (te[i], 0, 0)),
                  pl.BlockSpec((None, FF, D), lambda i, te, tn, tb: (te[i], 0, 0))],
        out_specs=rows,
    )
    return pl.pallas_call(
        _experts_kernel,
        out_shape=jax.ShapeDtypeStruct((dt, ntiles * tm, LANES), F32),
        grid_spec=grid_spec,
        compiler_params=_cparams(("arbitrary",)),
    )(tile_expert, tile_rows, tile_block, xs, wg, wu, wd)


def _combine_kernel(yg_ref, x1_ref, ri_ref, g2_ref, lg_ref, lb_ref, o_ref, *, alpha):
    ri = ri_ref[...]
    y = ri[:, 0:1] * _load_chunks(yg_ref.at[0]) + ri[:, 1:2] * _load_chunks(yg_ref.at[1])
    o_ref[...] = _layer_norm(alpha * x1_ref[...] + g2_ref[...] * y, lg_ref[...], lb_ref[...])


def _combine(yg, x1, rinfo, g2, ln_g, ln_b, alpha, tm=256):
    B, S, D = x1.shape
    nb = S // tm
    return pl.pallas_call(
        functools.partial(_combine_kernel, alpha=alpha),
        out_shape=jax.ShapeDtypeStruct((B, S, D), F32),
        grid=(B, nb),
        in_specs=[pl.BlockSpec((2, D // WORD_LANES, tm, LANES), lambda b, i: (0, 0, b * nb + i, 0)),
                  pl.BlockSpec((None, tm, D), lambda b, i: (b, i, 0)),
                  pl.BlockSpec((None, tm, LANES), lambda b, i: (b, i, 0)),
                  pl.BlockSpec((None, 1, D), lambda b, i: (b, 0, 0)),
                  pl.BlockSpec((1, D), lambda b, i: (0, 0)),
                  pl.BlockSpec((1, D), lambda b, i: (0, 0))],
        out_specs=pl.BlockSpec((None, tm, D), lambda b, i: (b, i, 0)),
        compiler_params=_cparams(("parallel", "parallel")),
    )(yg, x1, rinfo, g2, ln_g, ln_b)


def kernel(x, c, w_ada, b_ada, w_in, b_fox_forget, hgrn_lb_logits, hgrn_norm_w, w_up_fox, w_up_hgrn, w_out,
           ln1_g, ln1_b, w_router_group, b_router_group, w_router_expert, b_router_expert,
           w_expert_gate, w_expert_up, w_expert_down, ln2_g, ln2_b):
    B, S, D = x.shape
    depth = w_ada.shape[0]
    assert depth == 1, "single-layer block"
    fox_heads = b_fox_forget.shape[1]
    fox_w = fox_heads * HEAD_DIM
    hgrn_w = hgrn_norm_w.shape[1]
    ngroups = w_router_group.shape[2]
    nexp = w_router_expert.shape[2]
    nper = nexp // ngroups
    alpha = (2 * depth) ** 0.25
    T = B * S

    ada = _ada(c, w_ada[0], b_ada[0])
    sh1, sc1, g1, sh2, sc2, g2 = [a.reshape(B, 1, D) for a in jnp.split(ada, 6, axis=-1)]

    wi = w_in[0]
    o_ff = 3 * fox_w
    w_fox = jnp.pad(wi[:, :o_ff + fox_heads], ((0, 0), (0, LANES - fox_heads))).astype(BF16)
    w_rest = wi[:, o_ff + fox_heads:].astype(BF16)
    widths = [fox_w, fox_w, fox_w, LANES, hgrn_w, hgrn_w, hgrn_w, hgrn_w, D, D]
    segs, off = [], 0
    for n, w in enumerate(widths):
        if n == 4:
            off = 0
        segs.append((off, off + w))
        off += w
    bias_p = jnp.zeros((1, LANES), F32).at[0, :fox_heads].set(b_fox_forget[0])
    fq, fk, fv, cum, hq, hf, hi, hg, gf, gh = _inproj(x, sc1, sh1, bias_p, w_fox, w_rest, segs)
    y_fox = _fox(fq, fk, fv, cum)

    o_h = _hgrn(hq, hf, hi, hg, hgrn_lb_logits, hgrn_norm_w[0])

    wr = jnp.zeros((D, LANES), F32).at[:, :ngroups].set(w_router_group[0]).at[:, ngroups:ngroups + nexp].set(
        w_router_expert[0])
    wr_hi = lax.bitcast_convert_type(lax.bitcast_convert_type(wr, jnp.uint32) & jnp.uint32(0xFFFF0000), F32)
    wr = jnp.concatenate([wr_hi.astype(BF16), (wr - wr_hi).astype(BF16)], axis=1)
    br = jnp.zeros((1, LANES), F32).at[0, :ngroups].set(b_router_group[0]).at[0, ngroups:ngroups + nexp].set(
        b_router_expert[0])
    x1, h2, rinfo, fields, counts = _mix(
        y_fox, o_h, gf, gh, x, g1, sc2, sh2,
        w_up_fox[0].astype(BF16), w_up_hgrn[0].astype(BF16), w_out[0].astype(BF16),
        ln1_g[0].reshape(1, D), ln1_b[0].reshape(1, D), wr, br, alpha, ngroups, nper)

    tm_e = 512
    dt = D // WORD_LANES
    ntiles = (2 * T) // tm_e + nexp
    nslots = ntiles * tm_e
    cnt = counts[0, :nexp].astype(jnp.int32)
    padded = ((cnt + tm_e - 1) // tm_e) * tm_e
    ends = jnp.cumsum(padded)
    starts = ends - padded
    eid = fields[2:4].astype(jnp.int32)
    rank = fields[4:6].astype(jnp.int32)
    first = jnp.sum(jnp.where(eid[None] == jnp.arange(nexp, dtype=jnp.int32)[:, None, None],
                              starts[:, None, None], 0), axis=0)
    pos = first + rank
    tile_start = jnp.arange(ntiles, dtype=jnp.int32) * tm_e
    tile_block = jnp.minimum(jnp.arange(ntiles, dtype=jnp.int32), ends[-1] // tm_e - 1)
    tile_expert = jnp.minimum(jnp.sum((tile_start[:, None] >= ends[None, :]).astype(jnp.int32), axis=1), nexp - 1)
    tile_rows = jnp.clip(starts[tile_expert] + cnt[tile_expert] - tile_start, 0, tm_e)
    tile_expert = tile_expert[tile_block]
    rows = pos[:, None, :] + (jnp.arange(dt, dtype=jnp.int32) * nslots)[None, :, None]

    xs = _sc_scatter_rows(h2.reshape(dt * T, LANES), rows[0], rows[1], dt * nslots)
    ys = _experts(tile_expert, tile_rows, tile_block, xs.reshape(dt, nslots, LANES),
                  w_expert_gate[0], w_expert_up[0], w_expert_down[0], tm_e)
    yg = _sc_gather_rows(ys.reshape(dt * nslots, LANES), rows.reshape(2 * dt, T))
    return _combine(yg.reshape(2, dt, T, LANES), x1, rinfo, g2,
                    ln2_g[0].reshape(1, D), ln2_b[0].reshape(1, D), alpha)
```

```python
import functools

import jax
import jax.numpy as jnp
from jax import lax
from jax.experimental import pallas as pl
from jax.experimental.pallas import tpu as pltpu
from jax.experimental.pallas import tpu_sc as plsc

F32 = jnp.float32
BF16 = jnp.bfloat16
HIGHEST = lax.Precision.HIGHEST

LANES = 128
HEAD_DIM = 64
LN_EPS = 1e-5
RMS_EPS = 1e-6
LOG2E = 1.4426950408889634
NEG_BIG = -1e30
HCHUNK = 16
HBLOCK = 64
HGRN_SAFE_EXP = 60.0
ROW_TILE = 8
WORD_LANES = 2 * LANES
VMEM_LIMIT = 56 * 1024 * 1024


def _cparams(sem, vmem=VMEM_LIMIT):
    return pltpu.CompilerParams(dimension_semantics=sem, vmem_limit_bytes=vmem)


def _sigmoid(x):
    return 0.5 * jnp.tanh(0.5 * x) + 0.5


def _silu(x):
    return x * _sigmoid(x)


def _ada_kernel(c_ref, w_ref, b_ref, o_ref):
    c = c_ref[...]
    o_ref[...] = jnp.dot(_silu(c), w_ref[...], precision=HIGHEST,
                         preferred_element_type=F32) + b_ref[...]


def _ada(c, w_ada, b_ada):
    B, D = c.shape
    N = w_ada.shape[1]
    tn = 1024
    return pl.pallas_call(
        _ada_kernel,
        out_shape=jax.ShapeDtypeStruct((B, N), F32),
        grid=(N // tn,),
        in_specs=[pl.BlockSpec((B, D), lambda j: (0, 0)),
                  pl.BlockSpec((D, tn), lambda j: (0, j)),
                  pl.BlockSpec((1, tn), lambda j: (0, j))],
        out_specs=pl.BlockSpec((B, tn), lambda j: (0, j)),
        compiler_params=_cparams(("arbitrary",)),
    )(c, w_ada, b_ada.reshape(1, N))


N_FOX_SEGS = 4


FF_SEG = 3


def _inproj_kernel(x_ref, sc_ref, sh_ref, fb_ref, wf_ref, wr_ref,
                   fq_ref, fk_ref, fv_ref, fc_ref, hq_ref, hf_ref, hi_ref, hg_ref, gf_ref, gh_ref,
                   carry_sc, *, segs, q_scale):
    @pl.when(pl.program_id(1) == 0)
    def _():
        carry_sc[...] = jnp.zeros_like(carry_sc)

    h = (x_ref[...] * (1.0 + sc_ref[...]) + sh_ref[...]).astype(BF16)
    outs = (fq_ref, fk_ref, fv_ref, fc_ref, hq_ref, hf_ref, hi_ref, hg_ref, gf_ref, gh_ref)
    for idx, (o_ref, (a, b)) in enumerate(zip(outs, segs)):
        w_ref = wf_ref if idx < N_FOX_SEGS else wr_ref
        r = jnp.dot(h, w_ref[:, a:b], preferred_element_type=F32)
        if idx == 0:
            r = r * q_scale
        if idx == FF_SEG:
            tm = r.shape[0]
            z = r + fb_ref[...]
            lf = jnp.minimum(z, 0.0) - jnp.log(1.0 + jnp.exp(-jnp.abs(z)))
            lower = jnp.where(lax.broadcasted_iota(jnp.int32, (tm, tm), 0)
                              >= lax.broadcasted_iota(jnp.int32, (tm, tm), 1), 1.0, 0.0).astype(BF16)
            pieces, rest = [], lf
            for _ in range(NCUM):
                top = pltpu.bitcast(pltpu.bitcast(rest, jnp.uint32) & jnp.uint32(0xFFFF0000), F32)
                pieces.append(top.astype(BF16))
                rest = rest - top
            c3 = jnp.dot(lower, jnp.concatenate(pieces, axis=1), preferred_element_type=F32)
            cum = c3[:, :LANES] + c3[:, LANES:2 * LANES] + c3[:, 2 * LANES:] + carry_sc[...]
            carry_sc[...] = cum[tm - 1:tm, :]
            r = cum * LOG2E
        o_ref[...] = r.astype(o_ref.dtype)


def _inproj(x, sc1, sh1, fbias, w_fox, w_rest, segs, tm=256):
    B, S, D = x.shape
    widths = [b - a for a, b in segs]
    dtypes = [BF16, BF16, BF16, F32, BF16, F32, BF16, BF16, BF16, BF16]
    out_shape = tuple(jax.ShapeDtypeStruct((B, S, w), dt) for w, dt in zip(widths, dtypes))
    out_specs = tuple(pl.BlockSpec((None, tm, w), lambda b, i: (b, i, 0)) for w in widths)
    vec = pl.BlockSpec((None, 1, D), lambda b, i: (b, 0, 0))
    return pl.pallas_call(
        functools.partial(_inproj_kernel, segs=tuple(segs), q_scale=HEAD_DIM ** -0.5 * LOG2E),
        out_shape=out_shape,
        grid=(B, S // tm),
        in_specs=[pl.BlockSpec((None, tm, D), lambda b, i: (b, i, 0)), vec, vec,
                  pl.BlockSpec((1, LANES), lambda b, i: (0, 0)),
                  pl.BlockSpec(w_fox.shape, lambda b, i: (0, 0)),
                  pl.BlockSpec(w_rest.shape, lambda b, i: (0, 0))],
        out_specs=out_specs,
        scratch_shapes=[pltpu.VMEM((1, LANES), F32)],
        compiler_params=_cparams(("parallel", "arbitrary")),
    )(x, sc1, sh1, fbias, w_fox, w_rest)


NCUM = 3


def _fox_kernel(q_ref, k_ref, v_ref, c_ref, o_ref, ka_sc, kb_sc, va_sc, vb_sc, *, tq, tk):
    p = pl.program_id(1)
    qi = pl.program_id(2)
    S = k_ref.shape[0]

    @pl.when(qi == 0)
    def _():
        lane = lax.broadcasted_iota(jnp.int32, (S, LANES), 1)
        rr = lax.broadcasted_iota(jnp.int32, (LANES, LANES), 0)
        cc = lax.broadcasted_iota(jnp.int32, (LANES, LANES), 1)
        rest = c_ref[...]
        placed = jnp.zeros((S, LANES), F32)
        for i in range(NCUM):
            piece = rest.astype(BF16)
            rest = rest - piece.astype(F32)
            sel = ((rr == 2 * p) & (cc == HEAD_DIM + i)) | ((rr == 2 * p + 1) & (cc == i))
            placed = placed + jnp.dot(piece, jnp.where(sel, 1.0, 0.0).astype(BF16), preferred_element_type=F32)
        k2 = k_ref[...].astype(F32)
        ka_sc[...] = jnp.where(lane < HEAD_DIM, k2, -placed).astype(BF16)
        kb_sc[...] = jnp.where(lane >= HEAD_DIM, k2, -placed).astype(BF16)
        vt = v_ref[...].astype(F32).T
        row = lax.broadcasted_iota(jnp.int32, (LANES, S), 0)
        va_sc[...] = jnp.where(row < HEAD_DIM, vt, jnp.where(row == HEAD_DIM, 1.0, 0.0)).astype(BF16)
        vb_sc[...] = jnp.where(row >= HEAD_DIM, vt, jnp.where(row == 0, 1.0, 0.0)).astype(BF16)

    q2 = q_ref[...].astype(F32)
    qlane = lax.broadcasted_iota(jnp.int32, (tq, LANES), 1)
    qa = jnp.where(qlane < HEAD_DIM, q2, jnp.where(qlane < HEAD_DIM + NCUM, 1.0, 0.0)).astype(BF16)
    qb = jnp.where(qlane >= HEAD_DIM, q2, jnp.where(qlane < NCUM, 1.0, 0.0)).astype(BF16)
    nsub = tq // tk

    def block(k0, carry, diag_off):
        q0 = 0 if diag_off is None else diag_off
        out = []
        for ksc, vsc, qh, (m, acc) in ((ka_sc, va_sc, qa, carry[:2]), (kb_sc, vb_sc, qb, carry[2:])):
            st = lax.dot_general(ksc[pl.ds(k0, tk), :], qh[q0:, :], (((1,), (1,)), ((), ())),
                                 preferred_element_type=F32)
            if diag_off is not None:
                st = jnp.where(lax.broadcasted_iota(jnp.int32, st.shape, 0)
                               <= lax.broadcasted_iota(jnp.int32, st.shape, 1), st, NEG_BIG)
            m_old = m[:, q0:]
            m_new = jnp.maximum(m_old, jnp.max(st, axis=0, keepdims=True))
            pt = jnp.exp2(st - m_new).astype(BF16)
            acc_new = (jnp.exp2(m_old - m_new) * acc[:, q0:]
                       + jnp.dot(vsc[:, pl.ds(k0, tk)], pt, preferred_element_type=F32))
            if q0:
                m_new = jnp.concatenate([m[:, :q0], m_new], axis=1)
                acc_new = jnp.concatenate([acc[:, :q0], acc_new], axis=1)
            out += [m_new, acc_new]
        return tuple(out)

    def group(j, carry):
        k0 = pl.multiple_of(j * (nsub * tk), nsub * tk)
        for u in range(nsub):
            carry = block(k0 + u * tk, carry, None)
        return carry

    m0 = jnp.full((1, tq), NEG_BIG, F32)
    a0 = jnp.zeros((LANES, tq), F32)
    carry = lax.fori_loop(0, qi, group, (m0, a0, m0, a0))
    for d in range(nsub):
        carry = block(pl.multiple_of(qi * tq + d * tk, tk), carry, d * tk)
    _, aa, _, ab = carry
    row = lax.broadcasted_iota(jnp.int32, (LANES, tq), 0)
    ot = jnp.where(row < HEAD_DIM, aa * (1.0 / aa[HEAD_DIM:HEAD_DIM + 1, :]), ab * (1.0 / ab[0:1, :]))
    o_ref[...] = ot.T.astype(o_ref.dtype)


def _fox(fq, fk, fv, cum, tq=1024, tk=512):
    B, S, W = fq.shape
    assert tq % (2 * tk) == 0 and S % tq == 0
    npairs = W // LANES
    return pl.pallas_call(
        functools.partial(_fox_kernel, tq=tq, tk=tk),
        out_shape=jax.ShapeDtypeStruct((B, S, W), BF16),
        grid=(B, npairs, S // tq),
        in_specs=[pl.BlockSpec((None, tq, LANES), lambda b, p, i: (b, i, p)),
                  pl.BlockSpec((None, S, LANES), lambda b, p, i: (b, 0, p)),
                  pl.BlockSpec((None, S, LANES), lambda b, p, i: (b, 0, p)),
                  pl.BlockSpec((None, S, LANES), lambda b, p, i: (b, 0, 0))],
        out_specs=pl.BlockSpec((None, tq, LANES), lambda b, p, i: (b, i, p)),
        scratch_shapes=[pltpu.VMEM((S, LANES), BF16), pltpu.VMEM((S, LANES), BF16),
                        pltpu.VMEM((LANES, S), BF16), pltpu.VMEM((LANES, S), BF16)],
        compiler_params=_cparams(("parallel", "parallel", "arbitrary")),
    )(fq, fk, fv, cum)


def _hgrn_kernel(hq_ref, hf_ref, hi_ref, hg_ref, lb_ref, nw_ref, o_ref,
                 a_sc, b_sc, kk_sc, qq_sc, o_sc, w1_sc, w2_sc, w3_sc, w4_sc, w5_sc, w6_sc,
                 p_sc, st16_sc, dec_sc, st64_sc):
    S = hq_ref.shape[0]
    C = HCHUNK
    nchunks = S // C
    BLK = HBLOCK
    nblk = S // BLK

    lg = lb_ref[...]
    e = jnp.exp(lg - jnp.max(lg, axis=0, keepdims=True))
    lb = e[0:1, :] / jnp.sum(e, axis=0, keepdims=True)

    f = lb + (1.0 - lb) * _sigmoid(hf_ref[...])
    lf = jnp.log(f)
    kk_sc[...] = 1.0 - f
    qq_sc[...] = _silu(hq_ref[...].astype(F32))

    row = lax.broadcasted_iota(jnp.int32, (S, LANES), 0)
    rmod = row & (C - 1)
    a = lf
    d = 1
    while d < C:
        a = a + jnp.where(rmod >= d, pltpu.roll(a, d, axis=0), 0.0)
        d *= 2
    a3 = a.reshape(nchunks, C, LANES)
    alast = jnp.broadcast_to(a3[:, C - 1:C, :], (nchunks, C, LANES)).reshape(S, LANES)
    bmod = row & (BLK - 1)
    tot = alast
    d = C
    while d < BLK:
        tot = tot + jnp.where(bmod >= d, pltpu.roll(tot, d, axis=0), 0.0)
        d *= 2
    b = a + (tot - alast)
    b3 = b.reshape(nblk, BLK, LANES)
    blast = jnp.broadcast_to(b3[:, BLK - 1:BLK, :], (nblk, BLK, LANES)).reshape(S, LANES)
    a_sc[...] = a
    b_sc[...] = b
    safe = jnp.max(-blast) <= HGRN_SAFE_EXP

    lane = lax.broadcasted_iota(jnp.int32, (C, LANES), 1)
    sr = lax.broadcasted_iota(jnp.int32, (LANES, LANES), 0)
    scn = lax.broadcasted_iota(jnp.int32, (LANES, LANES), 1)
    same_head = (sr // HEAD_DIM) == (scn // HEAD_DIM)

    @pl.when(safe)
    def _factorised():
        qa_sc, qb_sc, kh_sc, ke_sc, qd_sc, k2_sc = w1_sc, w2_sc, w3_sc, w4_sc, w5_sc, w6_sc
        SB = 2 * BLK
        nsb = S // SB
        bb = b_sc[...]
        dblk = jnp.exp(bb.reshape(nblk, BLK, LANES)[:, BLK - 1:BLK, :])
        dfull = jnp.broadcast_to(dblk, (nblk, BLK, LANES)).reshape(S, LANES)
        second = (row & BLK) != 0
        d_prev = pltpu.roll(dfull, BLK, axis=0)
        d_next = pltpu.roll(dfull, S - BLK, axis=0)
        qh = qq_sc[...] * jnp.exp(bb)
        slane = lax.broadcasted_iota(jnp.int32, (S, LANES), 1)
        qa_sc[...] = jnp.where(slane < HEAD_DIM, qh, 0.0).astype(BF16)
        qb_sc[...] = jnp.where(slane >= HEAD_DIM, qh, 0.0).astype(BF16)
        qd_sc[...] = (qh * jnp.where(second, d_prev, 1.0)).astype(BF16)
        kh = kk_sc[...] * jnp.exp(-bb)
        kh_sc[...] = kh.astype(BF16)
        ke = kh * dfull
        ke_sc[...] = ke.astype(BF16)
        k2_sc[...] = (ke * jnp.where(second, 1.0, d_next)).astype(BF16)
        d3 = dfull.reshape(nsb, SB, LANES)
        dec_sc[pl.ds(0, nsb), :] = d3[:, 0, :] * d3[:, BLK, :]
        unroll = 4
        tn = (((0,), (0,)), ((), ()))
        nt = (((1,), (1,)), ((), ()))

        def scan(g, st):
            for u in range(unroll):
                i = g * unroll + u
                r0 = pl.multiple_of(i * SB, SB)
                st64_sc[i] = st.astype(BF16)
                upd = lax.dot_general(hi_ref[pl.ds(r0, SB), :], k2_sc[pl.ds(r0, SB), :], tn,
                                      preferred_element_type=F32)
                st = st * dec_sc[pl.ds(i, 1), :] + jnp.where(same_head, upd, 0.0)
            return st

        lax.fori_loop(0, nsb // unroll, scan, jnp.zeros((LANES, LANES), F32))

        r = lax.broadcasted_iota(jnp.int32, (2 * SB, 2 * SB), 0)
        c = lax.broadcasted_iota(jnp.int32, (2 * SB, 2 * SB), 1)
        t = r & (SB - 1)
        visible = (((c < SB) & ((t & BLK) == (c & BLK)) & ((t & (BLK - 1)) >= (c & (BLK - 1))))
                   | ((c >= SB) & (c < SB + BLK) & (t >= BLK)))
        plane = lax.broadcasted_iota(jnp.int32, (SB, LANES), 1)
        pad = jnp.zeros((BLK, LANES), BF16)

        def readout(g, _):
            for u in range(unroll):
                i = g * unroll + u
                r0 = pl.multiple_of(i * SB, SB)
                vb = hi_ref[pl.ds(r0, SB), :]
                q2 = jnp.concatenate([qa_sc[pl.ds(r0, SB), :], qb_sc[pl.ds(r0, SB), :]], axis=0)
                kext = jnp.concatenate([kh_sc[pl.ds(r0, SB), :], ke_sc[pl.ds(r0, BLK), :], pad], axis=0)
                vext = jnp.concatenate([vb, vb[:BLK], pad], axis=0)
                sc = lax.dot_general(q2, kext, nt, preferred_element_type=F32)
                sc = jnp.where(visible, sc, 0.0).astype(BF16)
                out = jnp.dot(sc, vext, preferred_element_type=F32)
                o_inter = lax.dot_general(qd_sc[pl.ds(r0, SB), :], st64_sc[i], nt, preferred_element_type=F32)
                o_sc[pl.ds(r0, SB), :] = jnp.where(plane < HEAD_DIM, out[:SB], out[SB:]) + o_inter
            return 0

        lax.fori_loop(0, nsb // unroll, readout, 0)

    @pl.when(jnp.logical_not(safe))
    def _direct():
        qt_sc, kt_sc, s_sc, a2_sc = w1_sc, w2_sc, w3_sc, b_sc
        aa = a_sc[...]
        al = jnp.broadcast_to(aa.reshape(nchunks, C, LANES)[:, C - 1:C, :], (nchunks, C, LANES)).reshape(S, LANES)
        qt_sc[...] = (qq_sc[...] * jnp.exp(aa)).astype(BF16)
        kt_sc[...] = (kk_sc[...] * jnp.exp(al - aa)).astype(BF16)
        dec_sc[...] = jnp.exp(aa.reshape(nchunks, C, LANES)[:, C - 1, :])
        a2_sc[...] = aa * LOG2E
        trow = lax.broadcasted_iota(jnp.int32, (C, LANES), 0)

        def gen(c, _):
            r0 = pl.multiple_of(c * C, C)
            ac = a2_sc[pl.ds(r0, C), :]
            qc = qq_sc[pl.ds(r0, C), :]
            kc = kk_sc[pl.ds(r0, C), :]
            half = C // 2
            for s in range(C):
                if s < half:
                    dec = jnp.exp2(jnp.where(trow >= s, ac - ac[s:s + 1, :], NEG_BIG))
                    p = qc * (kc[s:s + 1, :] * dec)
                else:
                    dec = jnp.exp2(jnp.where(trow[half:] >= s, ac[half:] - ac[s:s + 1, :], NEG_BIG))
                    p = jnp.concatenate([jnp.zeros((half, LANES), F32), qc[half:] * (kc[s:s + 1, :] * dec)],
                                        axis=0)
                p_sc[pl.ds(r0, C), s * LANES:(s + 1) * LANES] = p.astype(BF16)
            return 0

        lax.fori_loop(0, nchunks, gen, 0)

        er = lax.broadcasted_iota(jnp.int32, (C * LANES, LANES), 0)
        ec = lax.broadcasted_iota(jnp.int32, (C * LANES, LANES), 1)
        emat = (ec == ((er & (LANES - 1)) // HEAD_DIM) * C + er // LANES).astype(BF16)
        rb = 256

        def red(i, _):
            r0 = pl.multiple_of(i * rb, rb)
            s_sc[pl.ds(r0, rb), :] = jnp.dot(p_sc[pl.ds(r0, rb), :], emat,
                                             preferred_element_type=F32).astype(BF16)
            return 0

        lax.fori_loop(0, S // rb, red, 0)

        unroll = 16

        def scan(g, st):
            for u in range(unroll):
                c = g * unroll + u
                r0 = pl.multiple_of(c * C, C)
                st16_sc[c] = st.astype(BF16)
                upd = lax.dot_general(hi_ref[pl.ds(r0, C), :], kt_sc[pl.ds(r0, C), :],
                                      (((0,), (0,)), ((), ())), preferred_element_type=F32)
                st = st * dec_sc[pl.ds(c, 1), :] + jnp.where(same_head, upd, 0.0)
            return st

        lax.fori_loop(0, nchunks // unroll, scan, jnp.zeros((LANES, LANES), F32))

        def readout(g, _):
            for u in range(unroll):
                c = g * unroll + u
                r0 = pl.multiple_of(c * C, C)
                vc = hi_ref[pl.ds(r0, C), :]
                o_inter = lax.dot_general(qt_sc[pl.ds(r0, C), :], st16_sc[c],
                                          (((1,), (1,)), ((), ())), preferred_element_type=F32)
                v2 = jnp.concatenate([jnp.where(lane < HEAD_DIM, vc, jnp.zeros_like(vc)),
                                      jnp.where(lane >= HEAD_DIM, vc, jnp.zeros_like(vc))], axis=0)
                o_intra = jnp.dot(s_sc[pl.ds(r0, C), :][:, :2 * C], v2, preferred_element_type=F32)
                o_sc[pl.ds(r0, C), :] = o_inter + o_intra
            return 0

        lax.fori_loop(0, nchunks // unroll, readout, 0)

    o = o_sc[...]
    ones_head = jnp.where(same_head, 1.0, 0.0).astype(BF16)
    sq = o * o
    sq_top = pltpu.bitcast(pltpu.bitcast(sq, jnp.uint32) & jnp.uint32(0xFFFF0000), F32)
    ms = (jnp.dot(sq_top.astype(BF16), ones_head, preferred_element_type=F32)
          + jnp.dot((sq - sq_top).astype(BF16), ones_head, preferred_element_type=F32)) * (1.0 / HEAD_DIM)
    y = o * lax.rsqrt(ms + RMS_EPS) * nw_ref[...]
    o_ref[...] = (y * _silu(hg_ref[...].astype(F32))).astype(o_ref.dtype)


def _hgrn(hq, hf, hi, hg, lb_logits, norm_w):
    B, S, W = hq.shape
    npairs = W // LANES
    nrows = lb_logits.shape[0]
    seq = pl.BlockSpec((None, S, LANES), lambda b, p: (b, 0, p))
    return pl.pallas_call(
        _hgrn_kernel,
        out_shape=jax.ShapeDtypeStruct((B, S, W), BF16),
        grid=(B, npairs),
        in_specs=[seq, seq, seq, seq,
                  pl.BlockSpec((nrows, LANES), lambda b, p: (0, p)),
                  pl.BlockSpec((1, LANES), lambda b, p: (0, p))],
        out_specs=seq,
        scratch_shapes=[pltpu.VMEM((S, LANES), F32),
                        pltpu.VMEM((S, LANES), F32),
                        pltpu.VMEM((S, LANES), F32),
                        pltpu.VMEM((S, LANES), F32),
                        pltpu.VMEM((S, LANES), F32),
                        pltpu.VMEM((S, LANES), BF16),
                        pltpu.VMEM((S, LANES), BF16),
                        pltpu.VMEM((S, LANES), BF16),
                        pltpu.VMEM((S, LANES), BF16),
                        pltpu.VMEM((S, LANES), BF16),
                        pltpu.VMEM((S, LANES), BF16),
                        pltpu.VMEM((S, HCHUNK * LANES), BF16),
                        pltpu.VMEM((S // HCHUNK, LANES, LANES), BF16),
                        pltpu.VMEM((S // HCHUNK, LANES), F32),
                        pltpu.VMEM((S // HBLOCK, LANES, LANES), BF16)],
        compiler_params=_cparams(("parallel", "parallel")),
    )(hq, hf, hi, hg, lb_logits, norm_w.reshape(1, W))


def _layer_norm(v, g, b):
    mu = jnp.mean(v, axis=-1, keepdims=True)
    d = v - mu
    var = jnp.mean(d * d, axis=-1, keepdims=True)
    return d * lax.rsqrt(var + LN_EPS) * g + b


def _bf16_bits(x):
    u = pltpu.bitcast(x, jnp.uint32)
    return (u + jnp.uint32(0x7FFF) + ((u >> 16) & jnp.uint32(1))) & jnp.uint32(0xFFFF0000)


def _store_chunks(ref, val):
    n = ref.shape[0]
    for j in range(n):
        lo = _bf16_bits(val[:, j * LANES:(j + 1) * LANES]) >> 16
        hi = _bf16_bits(val[:, (j + n) * LANES:(j + n + 1) * LANES])
        ref[j] = pltpu.bitcast(lo | hi, F32)


def _load_chunks(ref):
    words = [pltpu.bitcast(ref[j], jnp.uint32) for j in range(ref.shape[0])]
    lo = [pltpu.bitcast(w << 16, F32) for w in words]
    hi = [pltpu.bitcast(w & jnp.uint32(0xFFFF0000), F32) for w in words]
    return jnp.concatenate(lo + hi, axis=1)


def _mix_kernel(yf_ref, oh_ref, gf_ref, gh_ref, x_ref, g1_ref, sc2_ref, sh2_ref,
                wuf_ref, wuh_ref, wo_ref, lg_ref, lbias_ref, wr_ref, br_ref,
                x1_ref, h2_ref, ri_ref, rt_ref, cnt_ref, carry_sc, *, alpha, ngroups, nper):
    first = (pl.program_id(0) == 0) & (pl.program_id(1) == 0)

    @pl.when(first)
    def _():
        carry_sc[...] = jnp.zeros_like(carry_sc)

    tm = x_ref.shape[0]
    yf = jnp.dot(yf_ref[...], wuf_ref[...], preferred_element_type=F32)
    yh = jnp.dot(oh_ref[...], wuh_ref[...], preferred_element_type=F32)
    merged = _sigmoid(gf_ref[...].astype(F32)) * yf + _sigmoid(gh_ref[...].astype(F32)) * yh
    y = jnp.dot(merged.astype(BF16), wo_ref[...], preferred_element_type=F32)
    x1 = _layer_norm(alpha * x_ref[...] + g1_ref[...] * y, lg_ref[...], lbias_ref[...])
    x1_ref[...] = x1
    h2 = x1 * (1.0 + sc2_ref[...]) + sh2_ref[...]
    _store_chunks(h2_ref, h2)

    h_top = pltpu.bitcast(pltpu.bitcast(h2, jnp.uint32) & jnp.uint32(0xFFFF0000), F32)
    h_hi = h_top.astype(BF16)
    h_lo = (h2 - h_top).astype(BF16)
    hh = jnp.dot(h_hi, wr_ref[...], preferred_element_type=F32)
    logits = (hh[:, :LANES] + hh[:, LANES:]
              + jnp.dot(h_lo, wr_ref[:, :LANES], preferred_element_type=F32)) + br_ref[...]
    lane = lax.broadcasted_iota(jnp.int32, (tm, LANES), 1)
    big = jnp.int32(1 << 20)

    def argmax_first(vals, mask):
        mx = jnp.max(jnp.where(mask, vals, -jnp.inf), axis=1, keepdims=True)
        idx = jnp.min(jnp.where(mask & (vals == mx), lane, big), axis=1, keepdims=True)
        return mx, idx

    gmask = lane < ngroups
    gmax = jnp.max(jnp.where(gmask, logits, -jnp.inf), axis=1, keepdims=True)
    gexp = jnp.where(gmask, jnp.exp(logits - gmax), 0.0)
    gprob = gexp / jnp.sum(gexp, axis=1, keepdims=True)
    g_w, g_idx = argmax_first(gprob, gmask)

    lo = ngroups + g_idx * nper
    emask = (lane >= lo) & (lane < lo + nper)
    emax = jnp.max(jnp.where(emask, logits, -jnp.inf), axis=1, keepdims=True)
    eexp = jnp.where(emask, jnp.exp(logits - emax), 0.0)
    eprob = eexp / jnp.sum(eexp, axis=1, keepdims=True)
    p0, i0 = argmax_first(eprob, emask)
    p1, i1 = argmax_first(eprob, emask & (lane != i0))
    den = p0 + p1
    w0 = p0 / den * g_w
    w1 = p1 / den * g_w
    e0 = i0 - ngroups
    e1 = i1 - ngroups

    oh = ((lane == e0) | (lane == e1)).astype(F32)
    r = lax.broadcasted_iota(jnp.int32, (tm, tm), 0)
    c = lax.broadcasted_iota(jnp.int32, (tm, tm), 1)
    strict_lower = (c < r).astype(BF16)
    before = jnp.dot(strict_lower, oh.astype(BF16), preferred_element_type=F32) + carry_sc[...]
    rank0 = jnp.sum(jnp.where(lane == e0, before, 0.0), axis=1, keepdims=True)
    rank1 = jnp.sum(jnp.where(lane == e1, before, 0.0), axis=1, keepdims=True)
    carry_sc[...] = carry_sc[...] + jnp.sum(oh, axis=0, keepdims=True)
    cnt_ref[...] = carry_sc[...]

    info = jnp.where(lane == 0, w0, 0.0)
    info = jnp.where(lane == 1, w1, info)
    info = jnp.where(lane == 2, e0.astype(F32), info)
    info = jnp.where(lane == 3, e1.astype(F32), info)
    info = jnp.where(lane == 4, rank0, info)
    info = jnp.where(lane == 5, rank1, info)
    ri_ref[...] = info
    rt_ref[...] = info.T[:ROW_TILE, :]


def _mix(yf, oh, gf, gh, x, g1, sc2, sh2, wuf, wuh, wo, ln_g, ln_b, wr, br, alpha, ngroups, nper, tm=512):
    B, S, D = x.shape
    W = yf.shape[2]
    tok = lambda w: pl.BlockSpec((None, tm, w), lambda b, i: (b, i, 0))
    vec = pl.BlockSpec((None, 1, D), lambda b, i: (b, 0, 0))
    full = lambda a: pl.BlockSpec(a.shape, lambda b, i: (0,) * a.ndim)
    return pl.pallas_call(
        functools.partial(_mix_kernel, alpha=alpha, ngroups=ngroups, nper=nper),
        out_shape=(jax.ShapeDtypeStruct((B, S, D), F32),
                   jax.ShapeDtypeStruct((D // WORD_LANES, B * S, LANES), F32),
                   jax.ShapeDtypeStruct((B, S, LANES), F32),
                   jax.ShapeDtypeStruct((ROW_TILE, B * S), F32),
                   jax.ShapeDtypeStruct((1, LANES), F32)),
        grid=(B, S // tm),
        in_specs=[tok(W), tok(W), tok(D), tok(D), tok(D), vec, vec, vec,
                  full(wuf), full(wuh), full(wo), full(ln_g), full(ln_b), full(wr), full(br)],
        out_specs=(tok(D),
                   pl.BlockSpec((D // WORD_LANES, tm, LANES), lambda b, i: (0, b * (S // tm) + i, 0)),
                   tok(LANES),
                   pl.BlockSpec((ROW_TILE, tm), lambda b, i: (0, b * (S // tm) + i)),
                   pl.BlockSpec((1, LANES), lambda b, i: (0, 0))),
        scratch_shapes=[pltpu.VMEM((1, LANES), F32)],
        compiler_params=_cparams(("arbitrary", "arbitrary")),
    )(yf, oh, gf, gh, x, g1, sc2, sh2, wuf, wuh, wo, ln_g, ln_b, wr, br)


def _sc_mesh():
    return plsc.VectorSubcoreMesh(core_axis_name="core", subcore_axis_name="subcore")


def _sc_pipeline(body, grid, in_specs, out_specs):
    return pltpu.emit_pipeline(body, grid=grid, in_specs=in_specs, out_specs=out_specs,
                               core_axis_name=("core", "subcore"),
                               dimension_semantics=(pltpu.PARALLEL,) * len(grid))


def _sc_scatter_rows(src, rows_a, rows_b, n_out):
    nj, t = rows_a.shape
    nc = t // LANES

    @pl.kernel(out_type=jax.ShapeDtypeStruct((n_out, LANES), src.dtype), mesh=_sc_mesh(), scratch_types=[])
    def scatter(x_hbm, a_hbm, b_hbm, o_hbm):
        def body(x_vmem, a_vmem, b_vmem):
            pltpu.sync_copy(x_vmem, o_hbm.at[a_vmem.at[0]])
            pltpu.sync_copy(x_vmem, o_hbm.at[b_vmem.at[0]])

        idx = pl.BlockSpec((1, LANES), lambda j, c: (j, c))
        _sc_pipeline(body, (nj, nc), [pl.BlockSpec((LANES, LANES), lambda j, c: (j * nc + c, 0)), idx, idx],
                     [])(x_hbm, a_hbm, b_hbm)

    return scatter(src, rows_a, rows_b)


def _sc_gather_rows(table, rows):
    nr, t = rows.shape
    nc = t // LANES

    @pl.kernel(out_type=jax.ShapeDtypeStruct((nr * t, LANES), table.dtype), mesh=_sc_mesh(), scratch_types=[])
    def gather(x_hbm, i_hbm, o_hbm):
        def body(i_vmem, o_vmem):
            pltpu.sync_copy(x_hbm.at[i_vmem.at[0]], o_vmem)

        _sc_pipeline(body, (nr, nc), [pl.BlockSpec((1, LANES), lambda r, c: (r, c))],
                     [pl.BlockSpec((LANES, LANES), lambda r, c: (r * nc + c, 0))])(i_hbm, o_hbm)

    return gather(table, rows)


def _experts_kernel(te_ref, tn_ref, tb_ref, x_ref, wg_ref, wu_ref, wd_ref, o_ref):
    del tb_ref
    nrows = tn_ref[pl.program_id(0)]

    @pl.when(nrows > 0)
    def _():
        x = _load_chunks(x_ref)
        x = jnp.where(lax.broadcasted_iota(jnp.int32, x.shape, 0) < nrows, x, 0.0).astype(BF16)
        g = jnp.dot(x, wg_ref[...].astype(BF16), preferred_element_type=F32)
        u = jnp.dot(x, wu_ref[...].astype(BF16), preferred_element_type=F32)
        hid = (_silu(g) * u).astype(BF16)
        _store_chunks(o_ref, jnp.dot(hid, wd_ref[...].astype(BF16), preferred_element_type=F32))


def _experts(tile_expert, tile_rows, tile_block, xs, wg, wu, wd, tm):
    E, D, FF = wg.shape
    dt = D // WORD_LANES
    ntiles = tile_expert.shape[0]
    rows = pl.BlockSpec((dt, tm, LANES), lambda i, te, tn, tb: (0, tb[i], 0))
    grid_spec = pltpu.PrefetchScalarGridSpec(
        num_scalar_prefetch=3,
        grid=(ntiles,),
        in_specs=[rows,
                  pl.BlockSpec((None, D, FF), lambda i, te, tn, tb: (te[i], 0, 0)),
                  pl.BlockSpec((None, D, FF), lambda i, te, tn, tb: (te[i], 0, 0)),
                  pl.BlockSpec((None, FF, D), lambda i, te, tn, tb: (te[i], 0, 0))],
        out_specs=rows,
    )
    return pl.pallas_call(
        _experts_kernel,
        out_shape=jax.ShapeDtypeStruct((dt, ntiles * tm, LANES), F32),
        grid_spec=grid_spec,
        compiler_params=_cparams(("arbitrary",)),
    )(tile_expert, tile_rows, tile_block, xs, wg, wu, wd)


def _combine_kernel(yg_ref, x1_ref, ri_ref, g2_ref, lg_ref, lb_ref, o_ref, *, alpha):
    ri = ri_ref[...]
    y = ri[:, 0:1] * _load_chunks(yg_ref.at[0]) + ri[:, 1:2] * _load_chunks(yg_ref.at[1])
    o_ref[...] = _layer_norm(alpha * x1_ref[...] + g2_ref[...] * y, lg_ref[...], lb_ref[...])


def _combine(yg, x1, rinfo, g2, ln_g, ln_b, alpha, tm=256):
    B, S, D = x1.shape
    nb = S // tm
    return pl.pallas_call(
        functools.partial(_combine_kernel, alpha=alpha),
        out_shape=jax.ShapeDtypeStruct((B, S, D), F32),
        grid=(B, nb),
        in_specs=[pl.BlockSpec((2, D // WORD_LANES, tm, LANES), lambda b, i: (0, 0, b * nb + i, 0)),
                  pl.BlockSpec((None, tm, D), lambda b, i: (b, i, 0)),
                  pl.BlockSpec((None, tm, LANES), lambda b, i: (b, i, 0)),
                  pl.BlockSpec((None, 1, D), lambda b, i: (b, 0, 0)),
                  pl.BlockSpec((1, D), lambda b, i: (0, 0)),
                  pl.BlockSpec((1, D), lambda b, i: (0, 0))],
        out_specs=pl.BlockSpec((None, tm, D), lambda b, i: (b, i, 0)),
        compiler_params=_cparams(("parallel", "parallel")),
    )(yg, x1, rinfo, g2, ln_g, ln_b)


def kernel(x, c, w_ada, b_ada, w_in, b_fox_forget, hgrn_lb_logits, hgrn_norm_w, w_up_fox, w_up_hgrn, w_out,
           ln1_g, ln1_b, w_router_group, b_router_group, w_router_expert, b_router_expert,
           w_expert_gate, w_expert_up, w_expert_down, ln2_g, ln2_b):
    B, S, D = x.shape
    depth = w_ada.shape[0]
    assert depth == 1, "single-layer block"
    fox_heads = b_fox_forget.shape[1]
    fox_w = fox_heads * HEAD_DIM
    hgrn_w = hgrn_norm_w.shape[1]
    ngroups = w_router_group.shape[2]
    nexp = w_router_expert.shape[2]
    nper = nexp // ngroups
    alpha = (2 * depth) ** 0.25
    T = B * S

    ada = _ada(c, w_ada[0], b_ada[0])
    sh1, sc1, g1, sh2, sc2, g2 = [a.reshape(B, 1, D) for a in jnp.split(ada, 6, axis=-1)]

    wi = w_in[0]
    o_ff = 3 * fox_w
    w_fox = jnp.pad(wi[:, :o_ff + fox_heads], ((0, 0), (0, LANES - fox_heads))).astype(BF16)
    w_rest = wi[:, o_ff + fox_heads:].astype(BF16)
    widths = [fox_w, fox_w, fox_w, LANES, hgrn_w, hgrn_w, hgrn_w, hgrn_w, D, D]
    segs, off = [], 0
    for n, w in enumerate(widths):
        if n == 4:
            off = 0
        segs.append((off, off + w))
        off += w
    bias_p = jnp.zeros((1, LANES), F32).at[0, :fox_heads].set(b_fox_forget[0])
    fq, fk, fv, cum, hq, hf, hi, hg, gf, gh = _inproj(x, sc1, sh1, bias_p, w_fox, w_rest, segs)
    y_fox = _fox(fq, fk, fv, cum)

    o_h = _hgrn(hq, hf, hi, hg, hgrn_lb_logits, hgrn_norm_w[0])

    wr = jnp.zeros((D, LANES), F32).at[:, :ngroups].set(w_router_group[0]).at[:, ngroups:ngroups + nexp].set(
        w_router_expert[0])
    wr_hi = lax.bitcast_convert_type(lax.bitcast_convert_type(wr, jnp.uint32) & jnp.uint32(0xFFFF0000), F32)
    wr = jnp.concatenate([wr_hi.astype(BF16), (wr - wr_hi).astype(BF16)], axis=1)
    br = jnp.zeros((1, LANES), F32).at[0, :ngroups].set(b_router_group[0]).at[0, ngroups:ngroups + nexp].set(
        b_router_expert[0])
    x1, h2, rinfo, fields, counts = _mix(
        y_fox, o_h, gf, gh, x, g1, sc2, sh2,
        w_up_fox[0].astype(BF16), w_up_hgrn[0].astype(BF16), w_out[0].astype(BF16),
        ln1_g[0].reshape(1, D), ln1_b[0].reshape(1, D), wr, br, alpha, ngroups, nper)

    tm_e = 512
    dt = D // WORD_LANES
    ntiles = (2 * T) // tm_e + nexp
    nslots = ntiles * tm_e
    cnt = counts[0, :nexp].astype(jnp.int32)
    padded = ((cnt + tm_e - 1) // tm_e) * tm_e
    ends = jnp.cumsum(padded)
    starts = ends - padded
    eid = fields[2:4].astype(jnp.int32)
    rank = fields[4:6].astype(jnp.int32)
    first = jnp.sum(jnp.where(eid[None] == jnp.arange(nexp, dtype=jnp.int32)[:, None, None],
                              starts[:, None, None], 0), axis=0)
    pos = first + rank
    tile_start = jnp.arange(ntiles, dtype=jnp.int32) * tm_e
    tile_block = jnp.minimum(jnp.arange(ntiles, dtype=jnp.int32), ends[-1] // tm_e - 1)
    tile_expert = jnp.minimum(jnp.sum((tile_start[:, None] >= ends[None, :]).astype(jnp.int32), axis=1), nexp - 1)
    tile_rows = jnp.clip(starts[tile_expert] + cnt[tile_expert] - tile_start, 0, tm_e)
    tile_expert = tile_expert[tile_block]
    rows = pos[:, None, :] + (jnp.arange(dt, dtype=jnp.int32) * nslots)[None, :, None]

    xs = _sc_scatter_rows(h2.reshape(dt * T, LANES), rows[0], rows[1], dt * nslots)
    ys = _experts(tile_expert, tile_rows, tile_block, xs.reshape(dt, nslots, LANES),
                  w_expert_gate[0], w_expert_up[0], w_expert_down[0], tm_e)
    yg = _sc_gather_rows(ys.reshape(dt * nslots, LANES), rows.reshape(2 * dt, T))
    return _combine(yg.reshape(2, dt, T, LANES), x1, rinfo, g2,
                    ln2_g[0].reshape(1, D), ln2_b[0].reshape(1, D), alpha)
```

```python
import functools

import jax
import jax.numpy as jnp
from jax import lax
from jax.experimental import pallas as pl
from jax.experimental.pallas import tpu as pltpu
from jax.experimental.pallas import tpu_sc as plsc

F32 = jnp.float32
BF16 = jnp.bfloat16
HIGHEST = lax.Precision.HIGHEST

LANES = 128
HEAD_DIM = 64
LN_EPS = 1e-5
RMS_EPS = 1e-6
LOG2E = 1.4426950408889634
NEG_BIG = -1e30
HCHUNK = 16
HBLOCK = 64
HGRN_SAFE_EXP = 60.0
ROW_TILE = 8
WORD_LANES = 2 * LANES
VMEM_LIMIT = 56 * 1024 * 1024


def _cparams(sem, vmem=VMEM_LIMIT):
    return pltpu.CompilerParams(dimension_semantics=sem, vmem_limit_bytes=vmem)


def _sigmoid(x):
    return 0.5 * jnp.tanh(0.5 * x) + 0.5


def _silu(x):
    return x * _sigmoid(x)


def _ada_kernel(c_ref, w_ref, b_ref, o_ref):
    c = c_ref[...]
    o_ref[...] = jnp.dot(_silu(c), w_ref[...], precision=HIGHEST,
                         preferred_element_type=F32) + b_ref[...]


def _ada(c, w_ada, b_ada):
    B, D = c.shape
    N = w_ada.shape[1]
    tn = 1024
    return pl.pallas_call(
        _ada_kernel,
        out_shape=jax.ShapeDtypeStruct((B, N), F32),
        grid=(N // tn,),
        in_specs=[pl.BlockSpec((B, D), lambda j: (0, 0)),
                  pl.BlockSpec((D, tn), lambda j: (0, j)),
                  pl.BlockSpec((1, tn), lambda j: (0, j))],
        out_specs=pl.BlockSpec((B, tn), lambda j: (0, j)),
        compiler_params=_cparams(("arbitrary",)),
    )(c, w_ada, b_ada.reshape(1, N))


N_FOX_SEGS = 4


def _inproj_kernel(x_ref, sc_ref, sh_ref, wf_ref, wr_ref,
                   fq_ref, fk_ref, fv_ref, ff_ref, hq_ref, hf_ref, hi_ref, hg_ref, gf_ref, gh_ref,
                   *, segs, q_scale):
    h = (x_ref[...] * (1.0 + sc_ref[...]) + sh_ref[...]).astype(BF16)
    outs = (fq_ref, fk_ref, fv_ref, ff_ref, hq_ref, hf_ref, hi_ref, hg_ref, gf_ref, gh_ref)
    for idx, (o_ref, (a, b)) in enumerate(zip(outs, segs)):
        w_ref = wf_ref if idx < N_FOX_SEGS else wr_ref
        r = jnp.dot(h, w_ref[:, a:b], preferred_element_type=F32)
        if idx == 0:
            r = r * q_scale
        o_ref[...] = r.astype(o_ref.dtype)


def _inproj(x, sc1, sh1, w_fox, w_rest, segs, tm=256):
    B, S, D = x.shape
    widths = [b - a for a, b in segs]
    dtypes = [BF16, BF16, BF16, F32, BF16, F32, BF16, BF16, BF16, BF16]
    out_shape = tuple(jax.ShapeDtypeStruct((B, S, w), dt) for w, dt in zip(widths, dtypes))
    out_specs = tuple(pl.BlockSpec((None, tm, w), lambda b, i: (b, i, 0)) for w in widths)
    vec = pl.BlockSpec((None, 1, D), lambda b, i: (b, 0, 0))
    return pl.pallas_call(
        functools.partial(_inproj_kernel, segs=tuple(segs), q_scale=HEAD_DIM ** -0.5 * LOG2E),
        out_shape=out_shape,
        grid=(B, S // tm),
        in_specs=[pl.BlockSpec((None, tm, D), lambda b, i: (b, i, 0)), vec, vec,
                  pl.BlockSpec(w_fox.shape, lambda b, i: (0, 0)),
                  pl.BlockSpec(w_rest.shape, lambda b, i: (0, 0))],
        out_specs=out_specs,
        compiler_params=_cparams(("parallel", "parallel")),
    )(x, sc1, sh1, w_fox, w_rest)


def _foxcum_kernel(ff_ref, b_ref, o_ref, *, blk):
    S = ff_ref.shape[0]
    r = lax.broadcasted_iota(jnp.int32, (blk, blk), 0)
    c = lax.broadcasted_iota(jnp.int32, (blk, blk), 1)
    lower = (r >= c).astype(F32)
    carry = jnp.zeros((1, LANES), F32)
    for j in range(S // blk):
        z = ff_ref[j * blk:(j + 1) * blk, :] + b_ref[...]
        lf = jnp.minimum(z, 0.0) - jnp.log(1.0 + jnp.exp(-jnp.abs(z)))
        cum = jnp.dot(lower, lf, precision=HIGHEST, preferred_element_type=F32) + carry
        o_ref[j * blk:(j + 1) * blk, :] = cum * LOG2E
        carry = cum[blk - 1:blk, :]


def _foxcum(ffp, bias_p, blk=256):
    B, S, _ = ffp.shape
    return pl.pallas_call(
        functools.partial(_foxcum_kernel, blk=blk),
        out_shape=jax.ShapeDtypeStruct((B, S, LANES), F32),
        grid=(B,),
        in_specs=[pl.BlockSpec((None, S, LANES), lambda b: (b, 0, 0)),
                  pl.BlockSpec((1, LANES), lambda b: (0, 0))],
        out_specs=pl.BlockSpec((None, S, LANES), lambda b: (b, 0, 0)),
        compiler_params=_cparams(("parallel",)),
    )(ffp, bias_p)


NCUM = 3


def _fox_kernel(q_ref, k_ref, v_ref, c_ref, o_ref, ka_sc, kb_sc, va_sc, vb_sc, *, tq, tk):
    p = pl.program_id(1)
    qi = pl.program_id(2)
    S = k_ref.shape[0]

    @pl.when(qi == 0)
    def _():
        lane = lax.broadcasted_iota(jnp.int32, (S, LANES), 1)
        rr = lax.broadcasted_iota(jnp.int32, (LANES, LANES), 0)
        cc = lax.broadcasted_iota(jnp.int32, (LANES, LANES), 1)
        rest = c_ref[...]
        placed = jnp.zeros((S, LANES), F32)
        for i in range(NCUM):
            piece = rest.astype(BF16)
            rest = rest - piece.astype(F32)
            sel = ((rr == 2 * p) & (cc == HEAD_DIM + i)) | ((rr == 2 * p + 1) & (cc == i))
            placed = placed + jnp.dot(piece, jnp.where(sel, 1.0, 0.0).astype(BF16), preferred_element_type=F32)
        k2 = k_ref[...].astype(F32)
        ka_sc[...] = jnp.where(lane < HEAD_DIM, k2, -placed).astype(BF16)
        kb_sc[...] = jnp.where(lane >= HEAD_DIM, k2, -placed).astype(BF16)
        vt = v_ref[...].astype(F32).T
        row = lax.broadcasted_iota(jnp.int32, (LANES, S), 0)
        va_sc[...] = jnp.where(row < HEAD_DIM, vt, jnp.where(row == HEAD_DIM, 1.0, 0.0)).astype(BF16)
        vb_sc[...] = jnp.where(row >= HEAD_DIM, vt, jnp.where(row == 0, 1.0, 0.0)).astype(BF16)

    q2 = q_ref[...].astype(F32)
    qlane = lax.broadcasted_iota(jnp.int32, (tq, LANES), 1)
    qa = jnp.where(qlane < HEAD_DIM, q2, jnp.where(qlane < HEAD_DIM + NCUM, 1.0, 0.0)).astype(BF16)
    qb = jnp.where(qlane >= HEAD_DIM, q2, jnp.where(qlane < NCUM, 1.0, 0.0)).astype(BF16)
    nsub = tq // tk

    def block(k0, carry, diag_off):
        q0 = 0 if diag_off is None else diag_off
        out = []
        for ksc, vsc, qh, (m, acc) in ((ka_sc, va_sc, qa, carry[:2]), (kb_sc, vb_sc, qb, carry[2:])):
            st = lax.dot_general(ksc[pl.ds(k0, tk), :], qh[q0:, :], (((1,), (1,)), ((), ())),
                                 preferred_element_type=F32)
            if diag_off is not None:
                st = jnp.where(lax.broadcasted_iota(jnp.int32, st.shape, 0)
                               <= lax.broadcasted_iota(jnp.int32, st.shape, 1), st, NEG_BIG)
            m_old = m[:, q0:]
            m_new = jnp.maximum(m_old, jnp.max(st, axis=0, keepdims=True))
            pt = jnp.exp2(st - m_new).astype(BF16)
            acc_new = (jnp.exp2(m_old - m_new) * acc[:, q0:]
                       + jnp.dot(vsc[:, pl.ds(k0, tk)], pt, preferred_element_type=F32))
            if q0:
                m_new = jnp.concatenate([m[:, :q0], m_new], axis=1)
                acc_new = jnp.concatenate([acc[:, :q0], acc_new], axis=1)
            out += [m_new, acc_new]
        return tuple(out)

    def group(j, carry):
        k0 = pl.multiple_of(j * (nsub * tk), nsub * tk)
        for u in range(nsub):
            carry = block(k0 + u * tk, carry, None)
        return carry

    m0 = jnp.full((1, tq), NEG_BIG, F32)
    a0 = jnp.zeros((LANES, tq), F32)
    carry = lax.fori_loop(0, qi, group, (m0, a0, m0, a0))
    for d in range(nsub):
        carry = block(pl.multiple_of(qi * tq + d * tk, tk), carry, d * tk)
    _, aa, _, ab = carry
    row = lax.broadcasted_iota(jnp.int32, (LANES, tq), 0)
    ot = jnp.where(row < HEAD_DIM, aa * (1.0 / aa[HEAD_DIM:HEAD_DIM + 1, :]), ab * (1.0 / ab[0:1, :]))
    o_ref[...] = ot.T.astype(o_ref.dtype)


def _fox(fq, fk, fv, cum, tq=1024, tk=512):
    B, S, W = fq.shape
    assert tq % (2 * tk) == 0 and S % tq == 0
    npairs = W // LANES
    return pl.pallas_call(
        functools.partial(_fox_kernel, tq=tq, tk=tk),
        out_shape=jax.ShapeDtypeStruct((B, S, W), BF16),
        grid=(B, npairs, S // tq),
        in_specs=[pl.BlockSpec((None, tq, LANES), lambda b, p, i: (b, i, p)),
                  pl.BlockSpec((None, S, LANES), lambda b, p, i: (b, 0, p)),
                  pl.BlockSpec((None, S, LANES), lambda b, p, i: (b, 0, p)),
                  pl.BlockSpec((None, S, LANES), lambda b, p, i: (b, 0, 0))],
        out_specs=pl.BlockSpec((None, tq, LANES), lambda b, p, i: (b, i, p)),
        scratch_shapes=[pltpu.VMEM((S, LANES), BF16), pltpu.VMEM((S, LANES), BF16),
                        pltpu.VMEM((LANES, S), BF16), pltpu.VMEM((LANES, S), BF16)],
        compiler_params=_cparams(("parallel", "parallel", "arbitrary")),
    )(fq, fk, fv, cum)


def _hgrn_kernel(hq_ref, hf_ref, hi_ref, hg_ref, lb_ref, nw_ref, o_ref,
                 b_sc, kk_sc, qq_sc, o_sc, w1_sc, w2_sc, w3_sc, w4_sc, w5_sc,
                 p_sc, st16_sc, dec_sc, st64_sc):
    S = hq_ref.shape[0]
    C = HCHUNK
    nchunks = S // C
    BLK = HBLOCK
    nblk = S // BLK

    lg = lb_ref[...]
    e = jnp.exp(lg - jnp.max(lg, axis=0, keepdims=True))
    lb = e[0:1, :] / jnp.sum(e, axis=0, keepdims=True)

    f = lb + (1.0 - lb) * _sigmoid(hf_ref[...])
    lf = jnp.log(f)
    kk_sc[...] = 1.0 - f
    qq_sc[...] = _silu(hq_ref[...].astype(F32))

    row = lax.broadcasted_iota(jnp.int32, (S, LANES), 0)
    rb = 4 * BLK
    tr = lax.broadcasted_iota(jnp.int32, (rb, rb), 0)
    tc = lax.broadcasted_iota(jnp.int32, (rb, rb), 1)
    tri = jnp.where(((tr & -BLK) == (tc & -BLK)) & (tc <= tr), 1.0, 0.0).astype(BF16)
    pieces, rest = [], lf
    for _ in range(NCUM):
        top = pltpu.bitcast(pltpu.bitcast(rest, jnp.uint32) & jnp.uint32(0xFFFF0000), F32)
        pieces.append(top.astype(BF16))
        rest = rest - top
    lf3 = jnp.concatenate(pieces, axis=1)
    for j in range(S // rb):
        c3 = jnp.dot(tri, lf3[j * rb:(j + 1) * rb, :], preferred_element_type=F32)
        b_sc[j * rb:(j + 1) * rb, :] = c3[:, :LANES] + c3[:, LANES:2 * LANES] + c3[:, 2 * LANES:]
    safe = jnp.max(-b_sc[...].reshape(nblk, BLK, LANES)[:, BLK - 1, :]) <= HGRN_SAFE_EXP

    lane = lax.broadcasted_iota(jnp.int32, (C, LANES), 1)
    sr = lax.broadcasted_iota(jnp.int32, (LANES, LANES), 0)
    scn = lax.broadcasted_iota(jnp.int32, (LANES, LANES), 1)
    same_head = (sr // HEAD_DIM) == (scn // HEAD_DIM)

    @pl.when(safe)
    def _factorised():
        qh_sc, kh_sc, ke_sc, qd_sc, k2_sc = w1_sc, w2_sc, w3_sc, w4_sc, w5_sc
        SB = 2 * BLK
        nsb = S // SB
        bb = b_sc[...]
        dblk = jnp.exp(bb.reshape(nblk, BLK, LANES)[:, BLK - 1:BLK, :])
        dfull = jnp.broadcast_to(dblk, (nblk, BLK, LANES)).reshape(S, LANES)
        second = (row & BLK) != 0
        d_prev = pltpu.roll(dfull, BLK, axis=0)
        d_next = pltpu.roll(dfull, S - BLK, axis=0)
        qh = qq_sc[...] * jnp.exp(bb)
        qh_sc[...] = qh.astype(BF16)
        qd_sc[...] = (qh * jnp.where(second, d_prev, 1.0)).astype(BF16)
        kh = kk_sc[...] * jnp.exp(-bb)
        kh_sc[...] = kh.astype(BF16)
        ke = kh * dfull
        ke_sc[...] = ke.astype(BF16)
        k2_sc[...] = (ke * jnp.where(second, 1.0, d_next)).astype(BF16)
        d3 = dfull.reshape(nsb, SB, LANES)
        dec_sc[pl.ds(0, nsb), :] = d3[:, 0, :] * d3[:, BLK, :]
        unroll = 4
        tn = (((0,), (0,)), ((), ()))
        nt = (((1,), (1,)), ((), ()))

        def scan(g, st):
            for u in range(unroll):
                i = g * unroll + u
                r0 = pl.multiple_of(i * SB, SB)
                st64_sc[i] = st.astype(BF16)
                upd = lax.dot_general(hi_ref[pl.ds(r0, SB), :], k2_sc[pl.ds(r0, SB), :], tn,
                                      preferred_element_type=F32)
                st = st * dec_sc[pl.ds(i, 1), :] + jnp.where(same_head, upd, 0.0)
            return st

        lax.fori_loop(0, nsb // unroll, scan, jnp.zeros((LANES, LANES), F32))

        r = lax.broadcasted_iota(jnp.int32, (2 * SB, 2 * SB), 0)
        c = lax.broadcasted_iota(jnp.int32, (2 * SB, 2 * SB), 1)
        t = r & (SB - 1)
        visible = (((c < SB) & ((t & BLK) == (c & BLK)) & ((t & (BLK - 1)) >= (c & (BLK - 1))))
                   | ((c >= SB) & (c < SB + BLK) & (t >= BLK)))
        plane = lax.broadcasted_iota(jnp.int32, (SB, LANES), 1)
        pad = jnp.zeros((BLK, LANES), BF16)

        def readout(g, _):
            for u in range(unroll):
                i = g * unroll + u
                r0 = pl.multiple_of(i * SB, SB)
                vb = hi_ref[pl.ds(r0, SB), :]
                qh2 = qh_sc[pl.ds(r0, SB), :]
                q2 = jnp.concatenate([jnp.where(plane < HEAD_DIM, qh2, jnp.zeros_like(qh2)),
                                      jnp.where(plane >= HEAD_DIM, qh2, jnp.zeros_like(qh2))], axis=0)
                kext = jnp.concatenate([kh_sc[pl.ds(r0, SB), :], ke_sc[pl.ds(r0, BLK), :], pad], axis=0)
                vext = jnp.concatenate([vb, vb[:BLK], pad], axis=0)
                sc = lax.dot_general(q2, kext, nt, preferred_element_type=F32)
                sc = jnp.where(visible, sc, 0.0).astype(BF16)
                out = jnp.dot(sc, vext, preferred_element_type=F32)
                o_inter = lax.dot_general(qd_sc[pl.ds(r0, SB), :], st64_sc[i], nt, preferred_element_type=F32)
                o_sc[pl.ds(r0, SB), :] = jnp.where(plane < HEAD_DIM, out[:SB], out[SB:]) + o_inter
            return 0

        lax.fori_loop(0, nsb // unroll, readout, 0)

    @pl.when(jnp.logical_not(safe))
    def _direct():
        qt_sc, kt_sc, s_sc, a2_sc = w1_sc, w2_sc, w3_sc, b_sc
        bb = b_sc[...]
        cl = jnp.broadcast_to(bb.reshape(nchunks, C, LANES)[:, C - 1:C, :], (nchunks, C, LANES)).reshape(S, LANES)
        aa = bb - jnp.where((row & (BLK - 1)) >= C, pltpu.roll(cl, C, axis=0), 0.0)
        al = jnp.broadcast_to(aa.reshape(nchunks, C, LANES)[:, C - 1:C, :], (nchunks, C, LANES)).reshape(S, LANES)
        qt_sc[...] = (qq_sc[...] * jnp.exp(aa)).astype(BF16)
        kt_sc[...] = (kk_sc[...] * jnp.exp(al - aa)).astype(BF16)
        dec_sc[...] = jnp.exp(aa.reshape(nchunks, C, LANES)[:, C - 1, :])
        a2_sc[...] = aa * LOG2E
        trow = lax.broadcasted_iota(jnp.int32, (C, LANES), 0)

        def gen(c, _):
            r0 = pl.multiple_of(c * C, C)
            ac = a2_sc[pl.ds(r0, C), :]
            qc = qq_sc[pl.ds(r0, C), :]
            kc = kk_sc[pl.ds(r0, C), :]
            half = C // 2
            for s in range(C):
                if s < half:
                    dec = jnp.exp2(jnp.where(trow >= s, ac - ac[s:s + 1, :], NEG_BIG))
                    p = qc * (kc[s:s + 1, :] * dec)
                else:
                    dec = jnp.exp2(jnp.where(trow[half:] >= s, ac[half:] - ac[s:s + 1, :], NEG_BIG))
                    p = jnp.concatenate([jnp.zeros((half, LANES), F32), qc[half:] * (kc[s:s + 1, :] * dec)],
                                        axis=0)
                p_sc[pl.ds(r0, C), s * LANES:(s + 1) * LANES] = p.astype(BF16)
            return 0

        lax.fori_loop(0, nchunks, gen, 0)

        er = lax.broadcasted_iota(jnp.int32, (C * LANES, LANES), 0)
        ec = lax.broadcasted_iota(jnp.int32, (C * LANES, LANES), 1)
        emat = (ec == ((er & (LANES - 1)) // HEAD_DIM) * C + er // LANES).astype(BF16)
        rb = 256

        def red(i, _):
            r0 = pl.multiple_of(i * rb, rb)
            s_sc[pl.ds(r0, rb), :] = jnp.dot(p_sc[pl.ds(r0, rb), :], emat,
                                             preferred_element_type=F32).astype(BF16)
            return 0

        lax.fori_loop(0, S // rb, red, 0)

        unroll = 16

        def scan(g, st):
            for u in range(unroll):
                c = g * unroll + u
                r0 = pl.multiple_of(c * C, C)
                st16_sc[c] = st.astype(BF16)
                upd = lax.dot_general(hi_ref[pl.ds(r0, C), :], kt_sc[pl.ds(r0, C), :],
                                      (((0,), (0,)), ((), ())), preferred_element_type=F32)
                st = st * dec_sc[pl.ds(c, 1), :] + jnp.where(same_head, upd, 0.0)
            return st

        lax.fori_loop(0, nchunks // unroll, scan, jnp.zeros((LANES, LANES), F32))

        def readout(g, _):
            for u in range(unroll):
                c = g * unroll + u
                r0 = pl.multiple_of(c * C, C)
                vc = hi_ref[pl.ds(r0, C), :]
                o_inter = lax.dot_general(qt_sc[pl.ds(r0, C), :], st16_sc[c],
                                          (((1,), (1,)), ((), ())), preferred_element_type=F32)
                v2 = jnp.concatenate([jnp.where(lane < HEAD_DIM, vc, jnp.zeros_like(vc)),
                                      jnp.where(lane >= HEAD_DIM, vc, jnp.zeros_like(vc))], axis=0)
                o_intra = jnp.dot(s_sc[pl.ds(r0, C), :][:, :2 * C], v2, preferred_element_type=F32)
                o_sc[pl.ds(r0, C), :] = o_inter + o_intra
            return 0

        lax.fori_loop(0, nchunks // unroll, readout, 0)

    o = o_sc[...]
    ones_head = jnp.where(same_head, 1.0, 0.0).astype(BF16)
    sq = o * o
    sq_top = pltpu.bitcast(pltpu.bitcast(sq, jnp.uint32) & jnp.uint32(0xFFFF0000), F32)
    ms = (jnp.dot(sq_top.astype(BF16), ones_head, preferred_element_type=F32)
          + jnp.dot((sq - sq_top).astype(BF16), ones_head, preferred_element_type=F32)) * (1.0 / HEAD_DIM)
    y = o * lax.rsqrt(ms + RMS_EPS) * nw_ref[...]
    o_ref[...] = (y * _silu(hg_ref[...].astype(F32))).astype(o_ref.dtype)


def _hgrn(hq, hf, hi, hg, lb_logits, norm_w):
    B, S, W = hq.shape
    npairs = W // LANES
    nrows = lb_logits.shape[0]
    seq = pl.BlockSpec((None, S, LANES), lambda b, p: (b, 0, p))
    return pl.pallas_call(
        _hgrn_kernel,
        out_shape=jax.ShapeDtypeStruct((B, S, W), BF16),
        grid=(B, npairs),
        in_specs=[seq, seq, seq, seq,
                  pl.BlockSpec((nrows, LANES), lambda b, p: (0, p)),
                  pl.BlockSpec((1, LANES), lambda b, p: (0, p))],
        out_specs=seq,
        scratch_shapes=[pltpu.VMEM((S, LANES), F32),
                        pltpu.VMEM((S, LANES), F32),
                        pltpu.VMEM((S, LANES), F32),
                        pltpu.VMEM((S, LANES), F32),
                        pltpu.VMEM((S, LANES), BF16),
                        pltpu.VMEM((S, LANES), BF16),
                        pltpu.VMEM((S, LANES), BF16),
                        pltpu.VMEM((S, LANES), BF16),
                        pltpu.VMEM((S, LANES), BF16),
                        pltpu.VMEM((S, HCHUNK * LANES), BF16),
                        pltpu.VMEM((S // HCHUNK, LANES, LANES), BF16),
                        pltpu.VMEM((S // HCHUNK, LANES), F32),
                        pltpu.VMEM((S // HBLOCK, LANES, LANES), BF16)],
        compiler_params=_cparams(("parallel", "parallel")),
    )(hq, hf, hi, hg, lb_logits, norm_w.reshape(1, W))


def _layer_norm(v, g, b):
    mu = jnp.mean(v, axis=-1, keepdims=True)
    d = v - mu
    var = jnp.mean(d * d, axis=-1, keepdims=True)
    return d * lax.rsqrt(var + LN_EPS) * g + b


def _bf16_bits(x):
    u = pltpu.bitcast(x, jnp.uint32)
    return (u + jnp.uint32(0x7FFF) + ((u >> 16) & jnp.uint32(1))) & jnp.uint32(0xFFFF0000)


def _store_chunks(ref, val):
    n = ref.shape[0]
    for j in range(n):
        lo = _bf16_bits(val[:, j * LANES:(j + 1) * LANES]) >> 16
        hi = _bf16_bits(val[:, (j + n) * LANES:(j + n + 1) * LANES])
        ref[j] = pltpu.bitcast(lo | hi, F32)


def _load_chunks(ref):
    words = [pltpu.bitcast(ref[j], jnp.uint32) for j in range(ref.shape[0])]
    lo = [pltpu.bitcast(w << 16, F32) for w in words]
    hi = [pltpu.bitcast(w & jnp.uint32(0xFFFF0000), F32) for w in words]
    return jnp.concatenate(lo + hi, axis=1)


def _mix_kernel(yf_ref, oh_ref, gf_ref, gh_ref, x_ref, g1_ref, sc2_ref, sh2_ref,
                wuf_ref, wuh_ref, wo_ref, lg_ref, lbias_ref, wr_ref, br_ref,
                x1_ref, h2_ref, ri_ref, rt_ref, cnt_ref, carry_sc, *, alpha, ngroups, nper):
    first = (pl.program_id(0) == 0) & (pl.program_id(1) == 0)

    @pl.when(first)
    def _():
        carry_sc[...] = jnp.zeros_like(carry_sc)

    tm = x_ref.shape[0]
    yf = jnp.dot(yf_ref[...], wuf_ref[...], preferred_element_type=F32)
    yh = jnp.dot(oh_ref[...], wuh_ref[...], preferred_element_type=F32)
    merged = _sigmoid(gf_ref[...].astype(F32)) * yf + _sigmoid(gh_ref[...].astype(F32)) * yh
    y = jnp.dot(merged.astype(BF16), wo_ref[...], preferred_element_type=F32)
    x1 = _layer_norm(alpha * x_ref[...] + g1_ref[...] * y, lg_ref[...], lbias_ref[...])
    x1_ref[...] = x1
    h2 = x1 * (1.0 + sc2_ref[...]) + sh2_ref[...]
    _store_chunks(h2_ref, h2)

    h_top = pltpu.bitcast(pltpu.bitcast(h2, jnp.uint32) & jnp.uint32(0xFFFF0000), F32)
    h_hi = h_top.astype(BF16)
    h_lo = (h2 - h_top).astype(BF16)
    hh = jnp.dot(h_hi, wr_ref[...], preferred_element_type=F32)
    logits = (hh[:, :LANES] + hh[:, LANES:]
              + jnp.dot(h_lo, wr_ref[:, :LANES], preferred_element_type=F32)) + br_ref[...]
    lane = lax.broadcasted_iota(jnp.int32, (tm, LANES), 1)
    big = jnp.int32(1 << 20)

    def argmax_first(vals, mask):
        mx = jnp.max(jnp.where(mask, vals, -jnp.inf), axis=1, keepdims=True)
        idx = jnp.min(jnp.where(mask & (vals == mx), lane, big), axis=1, keepdims=True)
        return mx, idx

    gmask = lane < ngroups
    gmax = jnp.max(jnp.where(gmask, logits, -jnp.inf), axis=1, keepdims=True)
    gexp = jnp.where(gmask, jnp.exp(logits - gmax), 0.0)
    gprob = gexp / jnp.sum(gexp, axis=1, keepdims=True)
    g_w, g_idx = argmax_first(gprob, gmask)

    lo = ngroups + g_idx * nper
    emask = (lane >= lo) & (lane < lo + nper)
    emax = jnp.max(jnp.where(emask, logits, -jnp.inf), axis=1, keepdims=True)
    eexp = jnp.where(emask, jnp.exp(logits - emax), 0.0)
    eprob = eexp / jnp.sum(eexp, axis=1, keepdims=True)
    p0, i0 = argmax_first(eprob, emask)
    p1, i1 = argmax_first(eprob, emask & (lane != i0))
    den = p0 + p1
    w0 = p0 / den * g_w
    w1 = p1 / den * g_w
    e0 = i0 - ngroups
    e1 = i1 - ngroups

    oh = ((lane == e0) | (lane == e1)).astype(F32)
    r = lax.broadcasted_iota(jnp.int32, (tm, tm), 0)
    c = lax.broadcasted_iota(jnp.int32, (tm, tm), 1)
    strict_lower = (c < r).astype(BF16)
    before = jnp.dot(strict_lower, oh.astype(BF16), preferred_element_type=F32) + carry_sc[...]
    rank0 = jnp.sum(jnp.where(lane == e0, before, 0.0), axis=1, keepdims=True)
    rank1 = jnp.sum(jnp.where(lane == e1, before, 0.0), axis=1, keepdims=True)
    carry_sc[...] = carry_sc[...] + jnp.sum(oh, axis=0, keepdims=True)
    cnt_ref[...] = carry_sc[...]

    info = jnp.where(lane == 0, w0, 0.0)
    info = jnp.where(lane == 1, w1, info)
    info = jnp.where(lane == 2, e0.astype(F32), info)
    info = jnp.where(lane == 3, e1.astype(F32), info)
    info = jnp.where(lane == 4, rank0, info)
    info = jnp.where(lane == 5, rank1, info)
    ri_ref[...] = info
    rt_ref[...] = info.T[:ROW_TILE, :]


def _mix(yf, oh, gf, gh, x, g1, sc2, sh2, wuf, wuh, wo, ln_g, ln_b, wr, br, alpha, ngroups, nper, tm=512):
    B, S, D = x.shape
    W = yf.shape[2]
    tok = lambda w: pl.BlockSpec((None, tm, w), lambda b, i: (b, i, 0))
    vec = pl.BlockSpec((None, 1, D), lambda b, i: (b, 0, 0))
    full = lambda a: pl.BlockSpec(a.shape, lambda b, i: (0,) * a.ndim)
    return pl.pallas_call(
        functools.partial(_mix_kernel, alpha=alpha, ngroups=ngroups, nper=nper),
        out_shape=(jax.ShapeDtypeStruct((B, S, D), F32),
                   jax.ShapeDtypeStruct((D // WORD_LANES, B * S, LANES), F32),
                   jax.ShapeDtypeStruct((B, S, LANES), F32),
                   jax.ShapeDtypeStruct((ROW_TILE, B * S), F32),
                   jax.ShapeDtypeStruct((1, LANES), F32)),
        grid=(B, S // tm),
        in_specs=[tok(W), tok(W), tok(D), tok(D), tok(D), vec, vec, vec,
                  full(wuf), full(wuh), full(wo), full(ln_g), full(ln_b), full(wr), full(br)],
        out_specs=(tok(D),
                   pl.BlockSpec((D // WORD_LANES, tm, LANES), lambda b, i: (0, b * (S // tm) + i, 0)),
                   tok(LANES),
                   pl.BlockSpec((ROW_TILE, tm), lambda b, i: (0, b * (S // tm) + i)),
                   pl.BlockSpec((1, LANES), lambda b, i: (0, 0))),
        scratch_shapes=[pltpu.VMEM((1, LANES), F32)],
        compiler_params=_cparams(("arbitrary", "arbitrary")),
    )(yf, oh, gf, gh, x, g1, sc2, sh2, wuf, wuh, wo, ln_g, ln_b, wr, br)


def _sc_mesh():
    return plsc.VectorSubcoreMesh(core_axis_name="core", subcore_axis_name="subcore")


def _sc_pipeline(body, grid, in_specs, out_specs):
    return pltpu.emit_pipeline(body, grid=grid, in_specs=in_specs, out_specs=out_specs,
                               core_axis_name=("core", "subcore"),
                               dimension_semantics=(pltpu.PARALLEL,) * len(grid))


def _sc_scatter_rows(src, rows_a, rows_b, n_out):
    nj, t = rows_a.shape
    nc = t // LANES

    @pl.kernel(out_type=jax.ShapeDtypeStruct((n_out, LANES), src.dtype), mesh=_sc_mesh(), scratch_types=[])
    def scatter(x_hbm, a_hbm, b_hbm, o_hbm):
        def body(x_vmem, a_vmem, b_vmem):
            pltpu.sync_copy(x_vmem, o_hbm.at[a_vmem.at[0]])
            pltpu.sync_copy(x_vmem, o_hbm.at[b_vmem.at[0]])

        idx = pl.BlockSpec((1, LANES), lambda j, c: (j, c))
        _sc_pipeline(body, (nj, nc), [pl.BlockSpec((LANES, LANES), lambda j, c: (j * nc + c, 0)), idx, idx],
                     [])(x_hbm, a_hbm, b_hbm)

    return scatter(src, rows_a, rows_b)


def _sc_gather_rows(table, rows):
    nr, t = rows.shape
    nc = t // LANES

    @pl.kernel(out_type=jax.ShapeDtypeStruct((nr * t, LANES), table.dtype), mesh=_sc_mesh(), scratch_types=[])
    def gather(x_hbm, i_hbm, o_hbm):
        def body(i_vmem, o_vmem):
            pltpu.sync_copy(x_hbm.at[i_vmem.at[0]], o_vmem)

        _sc_pipeline(body, (nr, nc), [pl.BlockSpec((1, LANES), lambda r, c: (r, c))],
                     [pl.BlockSpec((LANES, LANES), lambda r, c: (r * nc + c, 0))])(i_hbm, o_hbm)

    return gather(table, rows)


def _experts_kernel(te_ref, tn_ref, tb_ref, x_ref, wg_ref, wu_ref, wd_ref, o_ref):
    del tb_ref
    nrows = tn_ref[pl.program_id(0)]

    @pl.when(nrows > 0)
    def _():
        x = _load_chunks(x_ref)
        x = jnp.where(lax.broadcasted_iota(jnp.int32, x.shape, 0) < nrows, x, 0.0).astype(BF16)
        g = jnp.dot(x, wg_ref[...].astype(BF16), preferred_element_type=F32)
        u = jnp.dot(x, wu_ref[...].astype(BF16), preferred_element_type=F32)
        hid = (_silu(g) * u).astype(BF16)
        _store_chunks(o_ref, jnp.dot(hid, wd_ref[...].astype(BF16), preferred_element_type=F32))


def _experts(tile_expert, tile_rows, tile_block, xs, wg, wu, wd, tm):
    E, D, FF = wg.shape
    dt = D // WORD_LANES
    ntiles = tile_expert.shape[0]
    rows = pl.BlockSpec((dt, tm, LANES), lambda i, te, tn, tb: (0, tb[i], 0))
    grid_spec = pltpu.PrefetchScalarGridSpec(
        num_scalar_prefetch=3,
        grid=(ntiles,),
        in_specs=[rows,
                  pl.BlockSpec((None, D, FF), lambda i, te, tn, tb: (te[i], 0, 0)),
                  pl.BlockSpec((None, D, FF), lambda i, te, tn, tb: (te[i], 0, 0)),
                  pl.BlockSpec((None, FF, D), lambda i, te, tn, tb: (te[i], 0, 0))],
        out_specs=rows,
    )
    return pl.pallas_call(
        _experts_kernel,
        out_shape=jax.ShapeDtypeStruct((dt, ntiles * tm, LANES), F32),
        grid_spec=grid_spec,
        compiler_params=_cparams(("arbitrary",)),
    )(tile_expert, tile_rows, tile_block, xs, wg, wu, wd)


def _combine_kernel(yg_ref, x1_ref, ri_ref, g2_ref, lg_ref, lb_ref, o_ref, *, alpha):
    ri = ri_ref[...]
    y = ri[:, 0:1] * _load_chunks(yg_ref.at[0]) + ri[:, 1:2] * _load_chunks(yg_ref.at[1])
    o_ref[...] = _layer_norm(alpha * x1_ref[...] + g2_ref[...] * y, lg_ref[...], lb_ref[...])


def _combine(yg, x1, rinfo, g2, ln_g, ln_b, alpha, tm=512):
    B, S, D = x1.shape
    nb = S // tm
    return pl.pallas_call(
        functools.partial(_combine_kernel, alpha=alpha),
        out_shape=jax.ShapeDtypeStruct((B, S, D), F32),
        grid=(B, nb),
        in_specs=[pl.BlockSpec((2, D // WORD_LANES, tm, LANES), lambda b, i: (0, 0, b * nb + i, 0)),
                  pl.BlockSpec((None, tm, D), lambda b, i: (b, i, 0)),
                  pl.BlockSpec((None, tm, LANES), lambda b, i: (b, i, 0)),
                  pl.BlockSpec((None, 1, D), lambda b, i: (b, 0, 0)),
                  pl.BlockSpec((1, D), lambda b, i: (0, 0)),
                  pl.BlockSpec((1, D), lambda b, i: (0, 0))],
        out_specs=pl.BlockSpec((None, tm, D), lambda b, i: (b, i, 0)),
        compiler_params=_cparams(("parallel", "parallel")),
    )(yg, x1, rinfo, g2, ln_g, ln_b)


def kernel(x, c, w_ada, b_ada, w_in, b_fox_forget, hgrn_lb_logits, hgrn_norm_w, w_up_fox, w_up_hgrn, w_out,
           ln1_g, ln1_b, w_router_group, b_router_group, w_router_expert, b_router_expert,
           w_expert_gate, w_expert_up, w_expert_down, ln2_g, ln2_b):
    B, S, D = x.shape
    depth = w_ada.shape[0]
    assert depth == 1, "single-layer block"
    fox_heads = b_fox_forget.shape[1]
    fox_w = fox_heads * HEAD_DIM
    hgrn_w = hgrn_norm_w.shape[1]
    ngroups = w_router_group.shape[2]
    nexp = w_router_expert.shape[2]
    nper = nexp // ngroups
    alpha = (2 * depth) ** 0.25
    T = B * S

    ada = _ada(c, w_ada[0], b_ada[0])
    sh1, sc1, g1, sh2, sc2, g2 = [a.reshape(B, 1, D) for a in jnp.split(ada, 6, axis=-1)]

    wi = w_in[0]
    o_ff = 3 * fox_w
    w_fox = jnp.pad(wi[:, :o_ff + fox_heads], ((0, 0), (0, LANES - fox_heads))).astype(BF16)
    w_rest = wi[:, o_ff + fox_heads:].astype(BF16)
    widths = [fox_w, fox_w, fox_w, LANES, hgrn_w, hgrn_w, hgrn_w, hgrn_w, D, D]
    segs, off = [], 0
    for n, w in enumerate(widths):
        if n == 4:
            off = 0
        segs.append((off, off + w))
        off += w
    fq, fk, fv, ffp, hq, hf, hi, hg, gf, gh = _inproj(x, sc1, sh1, w_fox, w_rest, segs)

    bias_p = jnp.zeros((1, LANES), F32).at[0, :fox_heads].set(b_fox_forget[0])
    cum = _foxcum(ffp, bias_p)
    y_fox = _fox(fq, fk, fv, cum)

    o_h = _hgrn(hq, hf, hi, hg, hgrn_lb_logits, hgrn_norm_w[0])

    wr = jnp.zeros((D, LANES), F32).at[:, :ngroups].set(w_router_group[0]).at[:, ngroups:ngroups + nexp].set(
        w_router_expert[0])
    wr_hi = lax.bitcast_convert_type(lax.bitcast_convert_type(wr, jnp.uint32) & jnp.uint32(0xFFFF0000), F32)
    wr = jnp.concatenate([wr_hi.astype(BF16), (wr - wr_hi).astype(BF16)], axis=1)
    br = jnp.zeros((1, LANES), F32).at[0, :ngroups].set(b_router_group[0]).at[0, ngroups:ngroups + nexp].set(
        b_router_expert[0])
    x1, h2, rinfo, fields, counts = _mix(
        y_fox, o_h, gf, gh, x, g1, sc2, sh2,
        w_up_fox[0].astype(BF16), w_up_hgrn[0].astype(BF16), w_out[0].astype(BF16),
        ln1_g[0].reshape(1, D), ln1_b[0].reshape(1, D), wr, br, alpha, ngroups, nper)

    tm_e = 512
    dt = D // WORD_LANES
    ntiles = (2 * T) // tm_e + nexp
    nslots = ntiles * tm_e
    cnt = counts[0, :nexp].astype(jnp.int32)
    padded = ((cnt + tm_e - 1) // tm_e) * tm_e
    ends = jnp.cumsum(padded)
    starts = ends - padded
    eid = fields[2:4].astype(jnp.int32)
    rank = fields[4:6].astype(jnp.int32)
    first = jnp.sum(jnp.where(eid[None] == jnp.arange(nexp, dtype=jnp.int32)[:, None, None],
                              starts[:, None, None], 0), axis=0)
    pos = first + rank
    tile_start = jnp.arange(ntiles, dtype=jnp.int32) * tm_e
    tile_block = jnp.minimum(jnp.arange(ntiles, dtype=jnp.int32), ends[-1] // tm_e - 1)
    tile_expert = jnp.minimum(jnp.sum((tile_start[:, None] >= ends[None, :]).astype(jnp.int32), axis=1), nexp - 1)
    tile_rows = jnp.clip(starts[tile_expert] + cnt[tile_expert] - tile_start, 0, tm_e)
    tile_expert = tile_expert[tile_block]
    rows = pos[:, None, :] + (jnp.arange(dt, dtype=jnp.int32) * nslots)[None, :, None]

    xs = _sc_scatter_rows(h2.reshape(dt * T, LANES), rows[0], rows[1], dt * nslots)
    ys = _experts(tile_expert, tile_rows, tile_block, xs.reshape(dt, nslots, LANES),
                  w_expert_gate[0], w_expert_up[0], w_expert_down[0], tm_e)
    yg = _sc_gather_rows(ys.reshape(dt * nslots, LANES), rows.reshape(2 * dt, T))
    return _combine(yg.reshape(2, dt, T, LANES), x1, rinfo, g2,
                    ln2_g[0].reshape(1, D), ln2_b[0].reshape(1, D), alpha)
```

```python
import functools

import jax
import jax.numpy as jnp
from jax import lax
from jax.experimental import pallas as pl
from jax.experimental.pallas import tpu as pltpu
from jax.experimental.pallas import tpu_sc as plsc

F32 = jnp.float32
BF16 = jnp.bfloat16
HIGHEST = lax.Precision.HIGHEST

LANES = 128
HEAD_DIM = 64
LN_EPS = 1e-5
RMS_EPS = 1e-6
LOG2E = 1.4426950408889634
NEG_BIG = -1e30
HCHUNK = 16
HBLOCK = 64
HGRN_SAFE_EXP = 60.0
ROW_TILE = 8
WORD_LANES = 2 * LANES
VMEM_LIMIT = 56 * 1024 * 1024


def _cparams(sem, vmem=VMEM_LIMIT):
    return pltpu.CompilerParams(dimension_semantics=sem, vmem_limit_bytes=vmem)


def _sigmoid(x):
    return 0.5 * jnp.tanh(0.5 * x) + 0.5


def _silu(x):
    return x * _sigmoid(x)


def _ada_kernel(c_ref, w_ref, b_ref, o_ref):
    c = c_ref[...]
    o_ref[...] = jnp.dot(_silu(c), w_ref[...], precision=HIGHEST,
                         preferred_element_type=F32) + b_ref[...]


def _ada(c, w_ada, b_ada):
    B, D = c.shape
    N = w_ada.shape[1]
    tn = 1024
    return pl.pallas_call(
        _ada_kernel,
        out_shape=jax.ShapeDtypeStruct((B, N), F32),
        grid=(N // tn,),
        in_specs=[pl.BlockSpec((B, D), lambda j: (0, 0)),
                  pl.BlockSpec((D, tn), lambda j: (0, j)),
                  pl.BlockSpec((1, tn), lambda j: (0, j))],
        out_specs=pl.BlockSpec((B, tn), lambda j: (0, j)),
        compiler_params=_cparams(("arbitrary",)),
    )(c, w_ada, b_ada.reshape(1, N))


N_FOX_SEGS = 4


def _inproj_kernel(x_ref, sc_ref, sh_ref, wf_ref, wr_ref,
                   fq_ref, fk_ref, fv_ref, ff_ref, hq_ref, hf_ref, hi_ref, hg_ref, gf_ref, gh_ref,
                   *, segs, q_scale):
    h = (x_ref[...] * (1.0 + sc_ref[...]) + sh_ref[...]).astype(BF16)
    outs = (fq_ref, fk_ref, fv_ref, ff_ref, hq_ref, hf_ref, hi_ref, hg_ref, gf_ref, gh_ref)
    for idx, (o_ref, (a, b)) in enumerate(zip(outs, segs)):
        w_ref = wf_ref if idx < N_FOX_SEGS else wr_ref
        r = jnp.dot(h, w_ref[:, a:b], preferred_element_type=F32)
        if idx == 0:
            r = r * q_scale
        o_ref[...] = r.astype(o_ref.dtype)


def _inproj(x, sc1, sh1, w_fox, w_rest, segs, tm=256):
    B, S, D = x.shape
    widths = [b - a for a, b in segs]
    dtypes = [BF16, BF16, BF16, F32, BF16, F32, BF16, BF16, BF16, BF16]
    out_shape = tuple(jax.ShapeDtypeStruct((B, S, w), dt) for w, dt in zip(widths, dtypes))
    out_specs = tuple(pl.BlockSpec((None, tm, w), lambda b, i: (b, i, 0)) for w in widths)
    vec = pl.BlockSpec((None, 1, D), lambda b, i: (b, 0, 0))
    return pl.pallas_call(
        functools.partial(_inproj_kernel, segs=tuple(segs), q_scale=HEAD_DIM ** -0.5 * LOG2E),
        out_shape=out_shape,
        grid=(B, S // tm),
        in_specs=[pl.BlockSpec((None, tm, D), lambda b, i: (b, i, 0)), vec, vec,
                  pl.BlockSpec(w_fox.shape, lambda b, i: (0, 0)),
                  pl.BlockSpec(w_rest.shape, lambda b, i: (0, 0))],
        out_specs=out_specs,
        compiler_params=_cparams(("parallel", "parallel")),
    )(x, sc1, sh1, w_fox, w_rest)


def _foxcum_kernel(ff_ref, b_ref, o_ref, *, blk):
    S = ff_ref.shape[0]
    r = lax.broadcasted_iota(jnp.int32, (blk, blk), 0)
    c = lax.broadcasted_iota(jnp.int32, (blk, blk), 1)
    lower = (r >= c).astype(F32)
    carry = jnp.zeros((1, LANES), F32)
    for j in range(S // blk):
        z = ff_ref[j * blk:(j + 1) * blk, :] + b_ref[...]
        lf = jnp.minimum(z, 0.0) - jnp.log(1.0 + jnp.exp(-jnp.abs(z)))
        cum = jnp.dot(lower, lf, precision=HIGHEST, preferred_element_type=F32) + carry
        o_ref[j * blk:(j + 1) * blk, :] = cum * LOG2E
        carry = cum[blk - 1:blk, :]


def _foxcum(ffp, bias_p, blk=256):
    B, S, _ = ffp.shape
    return pl.pallas_call(
        functools.partial(_foxcum_kernel, blk=blk),
        out_shape=jax.ShapeDtypeStruct((B, S, LANES), F32),
        grid=(B,),
        in_specs=[pl.BlockSpec((None, S, LANES), lambda b: (b, 0, 0)),
                  pl.BlockSpec((1, LANES), lambda b: (0, 0))],
        out_specs=pl.BlockSpec((None, S, LANES), lambda b: (b, 0, 0)),
        compiler_params=_cparams(("parallel",)),
    )(ffp, bias_p)


NCUM = 3


def _fox_kernel(q_ref, k_ref, v_ref, c_ref, o_ref, ka_sc, kb_sc, va_sc, vb_sc, *, tq, tk):
    p = pl.program_id(1)
    qi = pl.program_id(2)
    S = k_ref.shape[0]

    @pl.when(qi == 0)
    def _():
        lane = lax.broadcasted_iota(jnp.int32, (S, LANES), 1)
        rr = lax.broadcasted_iota(jnp.int32, (LANES, LANES), 0)
        cc = lax.broadcasted_iota(jnp.int32, (LANES, LANES), 1)
        rest = c_ref[...]
        placed = jnp.zeros((S, LANES), F32)
        for i in range(NCUM):
            piece = rest.astype(BF16)
            rest = rest - piece.astype(F32)
            sel = ((rr == 2 * p) & (cc == HEAD_DIM + i)) | ((rr == 2 * p + 1) & (cc == i))
            placed = placed + jnp.dot(piece, jnp.where(sel, 1.0, 0.0).astype(BF16), preferred_element_type=F32)
        k2 = k_ref[...].astype(F32)
        ka_sc[...] = jnp.where(lane < HEAD_DIM, k2, -placed).astype(BF16)
        kb_sc[...] = jnp.where(lane >= HEAD_DIM, k2, -placed).astype(BF16)
        vt = v_ref[...].astype(F32).T
        row = lax.broadcasted_iota(jnp.int32, (LANES, S), 0)
        va_sc[...] = jnp.where(row < HEAD_DIM, vt, jnp.where(row == HEAD_DIM, 1.0, 0.0)).astype(BF16)
        vb_sc[...] = jnp.where(row >= HEAD_DIM, vt, jnp.where(row == 0, 1.0, 0.0)).astype(BF16)

    q2 = q_ref[...].astype(F32)
    qlane = lax.broadcasted_iota(jnp.int32, (tq, LANES), 1)
    qa = jnp.where(qlane < HEAD_DIM, q2, jnp.where(qlane < HEAD_DIM + NCUM, 1.0, 0.0)).astype(BF16)
    qb = jnp.where(qlane >= HEAD_DIM, q2, jnp.where(qlane < NCUM, 1.0, 0.0)).astype(BF16)
    nsub = tq // tk

    def block(k0, carry, diag_off):
        q0 = 0 if diag_off is None else diag_off
        out = []
        for ksc, vsc, qh, (m, acc) in ((ka_sc, va_sc, qa, carry[:2]), (kb_sc, vb_sc, qb, carry[2:])):
            st = lax.dot_general(ksc[pl.ds(k0, tk), :], qh[q0:, :], (((1,), (1,)), ((), ())),
                                 preferred_element_type=F32)
            if diag_off is not None:
                st = jnp.where(lax.broadcasted_iota(jnp.int32, st.shape, 0)
                               <= lax.broadcasted_iota(jnp.int32, st.shape, 1), st, NEG_BIG)
            m_old = m[:, q0:]
            m_new = jnp.maximum(m_old, jnp.max(st, axis=0, keepdims=True))
            pt = jnp.exp2(st - m_new).astype(BF16)
            acc_new = (jnp.exp2(m_old - m_new) * acc[:, q0:]
                       + jnp.dot(vsc[:, pl.ds(k0, tk)], pt, preferred_element_type=F32))
            if q0:
                m_new = jnp.concatenate([m[:, :q0], m_new], axis=1)
                acc_new = jnp.concatenate([acc[:, :q0], acc_new], axis=1)
            out += [m_new, acc_new]
        return tuple(out)

    def group(j, carry):
        k0 = pl.multiple_of(j * (nsub * tk), nsub * tk)
        for u in range(nsub):
            carry = block(k0 + u * tk, carry, None)
        return carry

    m0 = jnp.full((1, tq), NEG_BIG, F32)
    a0 = jnp.zeros((LANES, tq), F32)
    carry = lax.fori_loop(0, qi, group, (m0, a0, m0, a0))
    for d in range(nsub):
        carry = block(pl.multiple_of(qi * tq + d * tk, tk), carry, d * tk)
    _, aa, _, ab = carry
    row = lax.broadcasted_iota(jnp.int32, (LANES, tq), 0)
    ot = jnp.where(row < HEAD_DIM, aa * (1.0 / aa[HEAD_DIM:HEAD_DIM + 1, :]), ab * (1.0 / ab[0:1, :]))
    o_ref[...] = ot.T.astype(o_ref.dtype)


def _fox(fq, fk, fv, cum, tq=2048, tk=512):
    B, S, W = fq.shape
    tq = min(tq, S)
    assert tq % tk == 0 and S % tq == 0
    npairs = W // LANES
    return pl.pallas_call(
        functools.partial(_fox_kernel, tq=tq, tk=tk),
        out_shape=jax.ShapeDtypeStruct((B, S, W), BF16),
        grid=(B, npairs, S // tq),
        in_specs=[pl.BlockSpec((None, tq, LANES), lambda b, p, i: (b, i, p)),
                  pl.BlockSpec((None, S, LANES), lambda b, p, i: (b, 0, p)),
                  pl.BlockSpec((None, S, LANES), lambda b, p, i: (b, 0, p)),
                  pl.BlockSpec((None, S, LANES), lambda b, p, i: (b, 0, 0))],
        out_specs=pl.BlockSpec((None, tq, LANES), lambda b, p, i: (b, i, p)),
        scratch_shapes=[pltpu.VMEM((S, LANES), BF16), pltpu.VMEM((S, LANES), BF16),
                        pltpu.VMEM((LANES, S), BF16), pltpu.VMEM((LANES, S), BF16)],
        compiler_params=_cparams(("parallel", "parallel", "arbitrary")),
    )(fq, fk, fv, cum)


def _hgrn_kernel(hq_ref, hf_ref, hi_ref, hg_ref, lb_ref, nw_ref, o_ref,
                 b_sc, kk_sc, qq_sc, o_sc, w1_sc, w2_sc, w3_sc, w4_sc, w5_sc,
                 p_sc, st16_sc, dec_sc, st64_sc):
    S = hq_ref.shape[0]
    C = HCHUNK
    nchunks = S // C
    BLK = HBLOCK
    nblk = S // BLK

    lg = lb_ref[...]
    e = jnp.exp(lg - jnp.max(lg, axis=0, keepdims=True))
    lb = e[0:1, :] / jnp.sum(e, axis=0, keepdims=True)

    f = lb + (1.0 - lb) * _sigmoid(hf_ref[...])
    lf = jnp.log(f)
    kk_sc[...] = 1.0 - f
    qq_sc[...] = _silu(hq_ref[...].astype(F32))

    row = lax.broadcasted_iota(jnp.int32, (S, LANES), 0)
    rb = 4 * BLK
    tr = lax.broadcasted_iota(jnp.int32, (rb, rb), 0)
    tc = lax.broadcasted_iota(jnp.int32, (rb, rb), 1)
    tri = jnp.where(((tr & -BLK) == (tc & -BLK)) & (tc <= tr), 1.0, 0.0).astype(BF16)
    pieces, rest = [], lf
    for _ in range(NCUM):
        top = pltpu.bitcast(pltpu.bitcast(rest, jnp.uint32) & jnp.uint32(0xFFFF0000), F32)
        pieces.append(top.astype(BF16))
        rest = rest - top
    lf3 = jnp.concatenate(pieces, axis=1)
    for j in range(S // rb):
        c3 = jnp.dot(tri, lf3[j * rb:(j + 1) * rb, :], preferred_element_type=F32)
        b_sc[j * rb:(j + 1) * rb, :] = c3[:, :LANES] + c3[:, LANES:2 * LANES] + c3[:, 2 * LANES:]
    safe = jnp.max(-b_sc[...].reshape(nblk, BLK, LANES)[:, BLK - 1, :]) <= HGRN_SAFE_EXP

    lane = lax.broadcasted_iota(jnp.int32, (C, LANES), 1)
    sr = lax.broadcasted_iota(jnp.int32, (LANES, LANES), 0)
    scn = lax.broadcasted_iota(jnp.int32, (LANES, LANES), 1)
    same_head = (sr // HEAD_DIM) == (scn // HEAD_DIM)

    @pl.when(safe)
    def _factorised():
        qh_sc, kh_sc, ke_sc, qd_sc, k2_sc = w1_sc, w2_sc, w3_sc, w4_sc, w5_sc
        SB = 2 * BLK
        nsb = S // SB
        bb = b_sc[...]
        dblk = jnp.exp(bb.reshape(nblk, BLK, LANES)[:, BLK - 1:BLK, :])
        dfull = jnp.broadcast_to(dblk, (nblk, BLK, LANES)).reshape(S, LANES)
        second = (row & BLK) != 0
        d_prev = pltpu.roll(dfull, BLK, axis=0)
        d_next = pltpu.roll(dfull, S - BLK, axis=0)
        qh = qq_sc[...] * jnp.exp(bb)
        qh_sc[...] = qh.astype(BF16)
        qd_sc[...] = (qh * jnp.where(second, d_prev, 1.0)).astype(BF16)
        kh = kk_sc[...] * jnp.exp(-bb)
        kh_sc[...] = kh.astype(BF16)
        ke = kh * dfull
        ke_sc[...] = ke.astype(BF16)
        k2_sc[...] = (ke * jnp.where(second, 1.0, d_next)).astype(BF16)
        d3 = dfull.reshape(nsb, SB, LANES)
        dec_sc[pl.ds(0, nsb), :] = d3[:, 0, :] * d3[:, BLK, :]
        unroll = 4
        tn = (((0,), (0,)), ((), ()))
        nt = (((1,), (1,)), ((), ()))

        def scan(g, st):
            for u in range(unroll):
                i = g * unroll + u
                r0 = pl.multiple_of(i * SB, SB)
                st64_sc[i] = st.astype(BF16)
                upd = lax.dot_general(hi_ref[pl.ds(r0, SB), :], k2_sc[pl.ds(r0, SB), :], tn,
                                      preferred_element_type=F32)
                st = st * dec_sc[pl.ds(i, 1), :] + jnp.where(same_head, upd, 0.0)
            return st

        lax.fori_loop(0, nsb // unroll, scan, jnp.zeros((LANES, LANES), F32))

        r = lax.broadcasted_iota(jnp.int32, (2 * SB, 2 * SB), 0)
        c = lax.broadcasted_iota(jnp.int32, (2 * SB, 2 * SB), 1)
        t = r & (SB - 1)
        visible = (((c < SB) & ((t & BLK) == (c & BLK)) & ((t & (BLK - 1)) >= (c & (BLK - 1))))
                   | ((c >= SB) & (c < SB + BLK) & (t >= BLK)))
        plane = lax.broadcasted_iota(jnp.int32, (SB, LANES), 1)
        pad = jnp.zeros((BLK, LANES), BF16)

        def readout(g, _):
            for u in range(unroll):
                i = g * unroll + u
                r0 = pl.multiple_of(i * SB, SB)
                vb = hi_ref[pl.ds(r0, SB), :]
                qh2 = qh_sc[pl.ds(r0, SB), :]
                q2 = jnp.concatenate([jnp.where(plane < HEAD_DIM, qh2, jnp.zeros_like(qh2)),
                                      jnp.where(plane >= HEAD_DIM, qh2, jnp.zeros_like(qh2))], axis=0)
                kext = jnp.concatenate([kh_sc[pl.ds(r0, SB), :], ke_sc[pl.ds(r0, BLK), :], pad], axis=0)
                vext = jnp.concatenate([vb, vb[:BLK], pad], axis=0)
                sc = lax.dot_general(q2, kext, nt, preferred_element_type=F32)
                sc = jnp.where(visible, sc, 0.0).astype(BF16)
                out = jnp.dot(sc, vext, preferred_element_type=F32)
                o_inter = lax.dot_general(qd_sc[pl.ds(r0, SB), :], st64_sc[i], nt, preferred_element_type=F32)
                o_sc[pl.ds(r0, SB), :] = jnp.where(plane < HEAD_DIM, out[:SB], out[SB:]) + o_inter
            return 0

        lax.fori_loop(0, nsb // unroll, readout, 0)

    @pl.when(jnp.logical_not(safe))
    def _direct():
        qt_sc, kt_sc, s_sc, a2_sc = w1_sc, w2_sc, w3_sc, b_sc
        bb = b_sc[...]
        cl = jnp.broadcast_to(bb.reshape(nchunks, C, LANES)[:, C - 1:C, :], (nchunks, C, LANES)).reshape(S, LANES)
        aa = bb - jnp.where((row & (BLK - 1)) >= C, pltpu.roll(cl, C, axis=0), 0.0)
        al = jnp.broadcast_to(aa.reshape(nchunks, C, LANES)[:, C - 1:C, :], (nchunks, C, LANES)).reshape(S, LANES)
        qt_sc[...] = (qq_sc[...] * jnp.exp(aa)).astype(BF16)
        kt_sc[...] = (kk_sc[...] * jnp.exp(al - aa)).astype(BF16)
        dec_sc[...] = jnp.exp(aa.reshape(nchunks, C, LANES)[:, C - 1, :])
        a2_sc[...] = aa * LOG2E
        trow = lax.broadcasted_iota(jnp.int32, (C, LANES), 0)

        def gen(c, _):
            r0 = pl.multiple_of(c * C, C)
            ac = a2_sc[pl.ds(r0, C), :]
            qc = qq_sc[pl.ds(r0, C), :]
            kc = kk_sc[pl.ds(r0, C), :]
            half = C // 2
            for s in range(C):
                if s < half:
                    dec = jnp.exp2(jnp.where(trow >= s, ac - ac[s:s + 1, :], NEG_BIG))
                    p = qc * (kc[s:s + 1, :] * dec)
                else:
                    dec = jnp.exp2(jnp.where(trow[half:] >= s, ac[half:] - ac[s:s + 1, :], NEG_BIG))
                    p = jnp.concatenate([jnp.zeros((half, LANES), F32), qc[half:] * (kc[s:s + 1, :] * dec)],
                                        axis=0)
                p_sc[pl.ds(r0, C), s * LANES:(s + 1) * LANES] = p.astype(BF16)
            return 0

        lax.fori_loop(0, nchunks, gen, 0)

        er = lax.broadcasted_iota(jnp.int32, (C * LANES, LANES), 0)
        ec = lax.broadcasted_iota(jnp.int32, (C * LANES, LANES), 1)
        emat = (ec == ((er & (LANES - 1)) // HEAD_DIM) * C + er // LANES).astype(BF16)
        rb = 256

        def red(i, _):
            r0 = pl.multiple_of(i * rb, rb)
            s_sc[pl.ds(r0, rb), :] = jnp.dot(p_sc[pl.ds(r0, rb), :], emat,
                                             preferred_element_type=F32).astype(BF16)
            return 0

        lax.fori_loop(0, S // rb, red, 0)

        unroll = 16

        def scan(g, st):
            for u in range(unroll):
                c = g * unroll + u
                r0 = pl.multiple_of(c * C, C)
                st16_sc[c] = st.astype(BF16)
                upd = lax.dot_general(hi_ref[pl.ds(r0, C), :], kt_sc[pl.ds(r0, C), :],
                                      (((0,), (0,)), ((), ())), preferred_element_type=F32)
                st = st * dec_sc[pl.ds(c, 1), :] + jnp.where(same_head, upd, 0.0)
            return st

        lax.fori_loop(0, nchunks // unroll, scan, jnp.zeros((LANES, LANES), F32))

        def readout(g, _):
            for u in range(unroll):
                c = g * unroll + u
                r0 = pl.multiple_of(c * C, C)
                vc = hi_ref[pl.ds(r0, C), :]
                o_inter = lax.dot_general(qt_sc[pl.ds(r0, C), :], st16_sc[c],
                                          (((1,), (1,)), ((), ())), preferred_element_type=F32)
                v2 = jnp.concatenate([jnp.where(lane < HEAD_DIM, vc, jnp.zeros_like(vc)),
                                      jnp.where(lane >= HEAD_DIM, vc, jnp.zeros_like(vc))], axis=0)
                o_intra = jnp.dot(s_sc[pl.ds(r0, C), :][:, :2 * C], v2, preferred_element_type=F32)
                o_sc[pl.ds(r0, C), :] = o_inter + o_intra
            return 0

        lax.fori_loop(0, nchunks // unroll, readout, 0)

    o = o_sc[...]
    ones_head = jnp.where(same_head, 1.0, 0.0).astype(BF16)
    sq = o * o
    sq_top = pltpu.bitcast(pltpu.bitcast(sq, jnp.uint32) & jnp.uint32(0xFFFF0000), F32)
    ms = (jnp.dot(sq_top.astype(BF16), ones_head, preferred_element_type=F32)
          + jnp.dot((sq - sq_top).astype(BF16), ones_head, preferred_element_type=F32)) * (1.0 / HEAD_DIM)
    y = o * lax.rsqrt(ms + RMS_EPS) * nw_ref[...]
    o_ref[...] = (y * _silu(hg_ref[...].astype(F32))).astype(o_ref.dtype)


def _hgrn(hq, hf, hi, hg, lb_logits, norm_w):
    B, S, W = hq.shape
    npairs = W // LANES
    nrows = lb_logits.shape[0]
    seq = pl.BlockSpec((None, S, LANES), lambda b, p: (b, 0, p))
    return pl.pallas_call(
        _hgrn_kernel,
        out_shape=jax.ShapeDtypeStruct((B, S, W), BF16),
        grid=(B, npairs),
        in_specs=[seq, seq, seq, seq,
                  pl.BlockSpec((nrows, LANES), lambda b, p: (0, p)),
                  pl.BlockSpec((1, LANES), lambda b, p: (0, p))],
        out_specs=seq,
        scratch_shapes=[pltpu.VMEM((S, LANES), F32),
                        pltpu.VMEM((S, LANES), F32),
                        pltpu.VMEM((S, LANES), F32),
                        pltpu.VMEM((S, LANES), F32),
                        pltpu.VMEM((S, LANES), BF16),
                        pltpu.VMEM((S, LANES), BF16),
                        pltpu.VMEM((S, LANES), BF16),
                        pltpu.VMEM((S, LANES), BF16),
                        pltpu.VMEM((S, LANES), BF16),
                        pltpu.VMEM((S, HCHUNK * LANES), BF16),
                        pltpu.VMEM((S // HCHUNK, LANES, LANES), BF16),
                        pltpu.VMEM((S // HCHUNK, LANES), F32),
                        pltpu.VMEM((S // HBLOCK, LANES, LANES), BF16)],
        compiler_params=_cparams(("parallel", "parallel")),
    )(hq, hf, hi, hg, lb_logits, norm_w.reshape(1, W))


def _layer_norm(v, g, b):
    mu = jnp.mean(v, axis=-1, keepdims=True)
    d = v - mu
    var = jnp.mean(d * d, axis=-1, keepdims=True)
    return d * lax.rsqrt(var + LN_EPS) * g + b


def _bf16_bits(x):
    u = pltpu.bitcast(x, jnp.uint32)
    return (u + jnp.uint32(0x7FFF) + ((u >> 16) & jnp.uint32(1))) & jnp.uint32(0xFFFF0000)


def _store_chunks(ref, val):
    n = ref.shape[0]
    for j in range(n):
        lo = _bf16_bits(val[:, j * LANES:(j + 1) * LANES]) >> 16
        hi = _bf16_bits(val[:, (j + n) * LANES:(j + n + 1) * LANES])
        ref[j] = pltpu.bitcast(lo | hi, F32)


def _load_chunks(ref):
    words = [pltpu.bitcast(ref[j], jnp.uint32) for j in range(ref.shape[0])]
    lo = [pltpu.bitcast(w << 16, F32) for w in words]
    hi = [pltpu.bitcast(w & jnp.uint32(0xFFFF0000), F32) for w in words]
    return jnp.concatenate(lo + hi, axis=1)


def _mix_kernel(yf_ref, oh_ref, gf_ref, gh_ref, x_ref, g1_ref, sc2_ref, sh2_ref,
                wuf_ref, wuh_ref, wo_ref, lg_ref, lbias_ref, wr_ref, br_ref,
                x1_ref, h2_ref, ri_ref, rt_ref, cnt_ref, carry_sc, *, alpha, ngroups, nper):
    first = (pl.program_id(0) == 0) & (pl.program_id(1) == 0)

    @pl.when(first)
    def _():
        carry_sc[...] = jnp.zeros_like(carry_sc)

    tm = x_ref.shape[0]
    yf = jnp.dot(yf_ref[...], wuf_ref[...], preferred_element_type=F32)
    yh = jnp.dot(oh_ref[...], wuh_ref[...], preferred_element_type=F32)
    merged = _sigmoid(gf_ref[...].astype(F32)) * yf + _sigmoid(gh_ref[...].astype(F32)) * yh
    y = jnp.dot(merged.astype(BF16), wo_ref[...], preferred_element_type=F32)
    x1 = _layer_norm(alpha * x_ref[...] + g1_ref[...] * y, lg_ref[...], lbias_ref[...])
    x1_ref[...] = x1
    h2 = x1 * (1.0 + sc2_ref[...]) + sh2_ref[...]
    _store_chunks(h2_ref, h2)

    h_top = pltpu.bitcast(pltpu.bitcast(h2, jnp.uint32) & jnp.uint32(0xFFFF0000), F32)
    h_hi = h_top.astype(BF16)
    h_lo = (h2 - h_top).astype(BF16)
    hh = jnp.dot(h_hi, wr_ref[...], preferred_element_type=F32)
    logits = (hh[:, :LANES] + hh[:, LANES:]
              + jnp.dot(h_lo, wr_ref[:, :LANES], preferred_element_type=F32)) + br_ref[...]
    lane = lax.broadcasted_iota(jnp.int32, (tm, LANES), 1)
    big = jnp.int32(1 << 20)

    def argmax_first(vals, mask):
        mx = jnp.max(jnp.where(mask, vals, -jnp.inf), axis=1, keepdims=True)
        idx = jnp.min(jnp.where(mask & (vals == mx), lane, big), axis=1, keepdims=True)
        return mx, idx

    gmask = lane < ngroups
    gmax = jnp.max(jnp.where(gmask, logits, -jnp.inf), axis=1, keepdims=True)
    gexp = jnp.where(gmask, jnp.exp(logits - gmax), 0.0)
    gprob = gexp / jnp.sum(gexp, axis=1, keepdims=True)
    g_w, g_idx = argmax_first(gprob, gmask)

    lo = ngroups + g_idx * nper
    emask = (lane >= lo) & (lane < lo + nper)
    emax = jnp.max(jnp.where(emask, logits, -jnp.inf), axis=1, keepdims=True)
    eexp = jnp.where(emask, jnp.exp(logits - emax), 0.0)
    eprob = eexp / jnp.sum(eexp, axis=1, keepdims=True)
    p0, i0 = argmax_first(eprob, emask)
    p1, i1 = argmax_first(eprob, emask & (lane != i0))
    den = p0 + p1
    w0 = p0 / den * g_w
    w1 = p1 / den * g_w
    e0 = i0 - ngroups
    e1 = i1 - ngroups

    oh = ((lane == e0) | (lane == e1)).astype(F32)
    r = lax.broadcasted_iota(jnp.int32, (tm, tm), 0)
    c = lax.broadcasted_iota(jnp.int32, (tm, tm), 1)
    strict_lower = (c < r).astype(BF16)
    before = jnp.dot(strict_lower, oh.astype(BF16), preferred_element_type=F32) + carry_sc[...]
    rank0 = jnp.sum(jnp.where(lane == e0, before, 0.0), axis=1, keepdims=True)
    rank1 = jnp.sum(jnp.where(lane == e1, before, 0.0), axis=1, keepdims=True)
    carry_sc[...] = carry_sc[...] + jnp.sum(oh, axis=0, keepdims=True)
    cnt_ref[...] = carry_sc[...]

    info = jnp.where(lane == 0, w0, 0.0)
    info = jnp.where(lane == 1, w1, info)
    info = jnp.where(lane == 2, e0.astype(F32), info)
    info = jnp.where(lane == 3, e1.astype(F32), info)
    info = jnp.where(lane == 4, rank0, info)
    info = jnp.where(lane == 5, rank1, info)
    ri_ref[...] = info
    rt_ref[...] = info.T[:ROW_TILE, :]


def _mix(yf, oh, gf, gh, x, g1, sc2, sh2, wuf, wuh, wo, ln_g, ln_b, wr, br, alpha, ngroups, nper, tm=512):
    B, S, D = x.shape
    W = yf.shape[2]
    tok = lambda w: pl.BlockSpec((None, tm, w), lambda b, i: (b, i, 0))
    vec = pl.BlockSpec((None, 1, D), lambda b, i: (b, 0, 0))
    full = lambda a: pl.BlockSpec(a.shape, lambda b, i: (0,) * a.ndim)
    return pl.pallas_call(
        functools.partial(_mix_kernel, alpha=alpha, ngroups=ngroups, nper=nper),
        out_shape=(jax.ShapeDtypeStruct((B, S, D), F32),
                   jax.ShapeDtypeStruct((D // WORD_LANES, B * S, LANES), F32),
                   jax.ShapeDtypeStruct((B, S, LANES), F32),
                   jax.ShapeDtypeStruct((ROW_TILE, B * S), F32),
                   jax.ShapeDtypeStruct((1, LANES), F32)),
        grid=(B, S // tm),
        in_specs=[tok(W), tok(W), tok(D), tok(D), tok(D), vec, vec, vec,
                  full(wuf), full(wuh), full(wo), full(ln_g), full(ln_b), full(wr), full(br)],
        out_specs=(tok(D),
                   pl.BlockSpec((D // WORD_LANES, tm, LANES), lambda b, i: (0, b * (S // tm) + i, 0)),
                   tok(LANES),
                   pl.BlockSpec((ROW_TILE, tm), lambda b, i: (0, b * (S // tm) + i)),
                   pl.BlockSpec((1, LANES), lambda b, i: (0, 0))),
        scratch_shapes=[pltpu.VMEM((1, LANES), F32)],
        compiler_params=_cparams(("arbitrary", "arbitrary")),
    )(yf, oh, gf, gh, x, g1, sc2, sh2, wuf, wuh, wo, ln_g, ln_b, wr, br)


def _sc_mesh():
    return plsc.VectorSubcoreMesh(core_axis_name="core", subcore_axis_name="subcore")


def _sc_pipeline(body, grid, in_specs, out_specs):
    return pltpu.emit_pipeline(body, grid=grid, in_specs=in_specs, out_specs=out_specs,
                               core_axis_name=("core", "subcore"),
                               dimension_semantics=(pltpu.PARALLEL,) * len(grid))


def _sc_scatter_rows(src, rows_a, rows_b, n_out):
    nj, t = rows_a.shape
    nc = t // LANES

    @pl.kernel(out_type=jax.ShapeDtypeStruct((n_out, LANES), src.dtype), mesh=_sc_mesh(), scratch_types=[])
    def scatter(x_hbm, a_hbm, b_hbm, o_hbm):
        def body(x_vmem, a_vmem, b_vmem):
            pltpu.sync_copy(x_vmem, o_hbm.at[a_vmem.at[0]])
            pltpu.sync_copy(x_vmem, o_hbm.at[b_vmem.at[0]])

        idx = pl.BlockSpec((1, LANES), lambda j, c: (j, c))
        _sc_pipeline(body, (nj, nc), [pl.BlockSpec((LANES, LANES), lambda j, c: (j * nc + c, 0)), idx, idx],
                     [])(x_hbm, a_hbm, b_hbm)

    return scatter(src, rows_a, rows_b)


def _sc_gather_rows(table, rows):
    nr, t = rows.shape
    nc = t // LANES

    @pl.kernel(out_type=jax.ShapeDtypeStruct((nr * t, LANES), table.dtype), mesh=_sc_mesh(), scratch_types=[])
    def gather(x_hbm, i_hbm, o_hbm):
        def body(i_vmem, o_vmem):
            pltpu.sync_copy(x_hbm.at[i_vmem.at[0]], o_vmem)

        _sc_pipeline(body, (nr, nc), [pl.BlockSpec((1, LANES), lambda r, c: (r, c))],
                     [pl.BlockSpec((LANES, LANES), lambda r, c: (r * nc + c, 0))])(i_hbm, o_hbm)

    return gather(table, rows)


def _experts_kernel(te_ref, tn_ref, tb_ref, x_ref, wg_ref, wu_ref, wd_ref, o_ref):
    del tb_ref
    nrows = tn_ref[pl.program_id(0)]

    @pl.when(nrows > 0)
    def _():
        x = _load_chunks(x_ref)
        x = jnp.where(lax.broadcasted_iota(jnp.int32, x.shape, 0) < nrows, x, 0.0).astype(BF16)
        g = jnp.dot(x, wg_ref[...].astype(BF16), preferred_element_type=F32)
        u = jnp.dot(x, wu_ref[...].astype(BF16), preferred_element_type=F32)
        hid = (_silu(g) * u).astype(BF16)
        _store_chunks(o_ref, jnp.dot(hid, wd_ref[...].astype(BF16), preferred_element_type=F32))


def _experts(tile_expert, tile_rows, tile_block, xs, wg, wu, wd, tm):
    E, D, FF = wg.shape
    dt = D // WORD_LANES
    ntiles = tile_expert.shape[0]
    rows = pl.BlockSpec((dt, tm, LANES), lambda i, te, tn, tb: (0, tb[i], 0))
    grid_spec = pltpu.PrefetchScalarGridSpec(
        num_scalar_prefetch=3,
        grid=(ntiles,),
        in_specs=[rows,
                  pl.BlockSpec((None, D, FF), lambda i, te, tn, tb: (te[i], 0, 0)),
                  pl.BlockSpec((None, D, FF), lambda i, te, tn, tb: (te[i], 0, 0)),
                  pl.BlockSpec((None, FF, D), lambda i, te, tn, tb: (te[i], 0, 0))],
        out_specs=rows,
    )
    return pl.pallas_call(
        _experts_kernel,
        out_shape=jax.ShapeDtypeStruct((dt, ntiles * tm, LANES), F32),
        grid_spec=grid_spec,
        compiler_params=_cparams(("arbitrary",)),
    )(tile_expert, tile_rows, tile_block, xs, wg, wu, wd)


def _combine_kernel(yg_ref, x1_ref, ri_ref, g2_ref, lg_ref, lb_ref, o_ref, *, alpha):
    ri = ri_ref[...]
    y = ri[:, 0:1] * _load_chunks(yg_ref.at[0]) + ri[:, 1:2] * _load_chunks(yg_ref.at[1])
    o_ref[...] = _layer_norm(alpha * x1_ref[...] + g2_ref[...] * y, lg_ref[...], lb_ref[...])


def _combine(yg, x1, rinfo, g2, ln_g, ln_b, alpha, tm=512):
    B, S, D = x1.shape
    nb = S // tm
    return pl.pallas_call(
        functools.partial(_combine_kernel, alpha=alpha),
        out_shape=jax.ShapeDtypeStruct((B, S, D), F32),
        grid=(B, nb),
        in_specs=[pl.BlockSpec((2, D // WORD_LANES, tm, LANES), lambda b, i: (0, 0, b * nb + i, 0)),
                  pl.BlockSpec((None, tm, D), lambda b, i: (b, i, 0)),
                  pl.BlockSpec((None, tm, LANES), lambda b, i: (b, i, 0)),
                  pl.BlockSpec((None, 1, D), lambda b, i: (b, 0, 0)),
                  pl.BlockSpec((1, D), lambda b, i: (0, 0)),
                  pl.BlockSpec((1, D), lambda b, i: (0, 0))],
        out_specs=pl.BlockSpec((None, tm, D), lambda b, i: (b, i, 0)),
        compiler_params=_cparams(("parallel", "parallel")),
    )(yg, x1, rinfo, g2, ln_g, ln_b)


def kernel(x, c, w_ada, b_ada, w_in, b_fox_forget, hgrn_lb_logits, hgrn_norm_w, w_up_fox, w_up_hgrn, w_out,
           ln1_g, ln1_b, w_router_group, b_router_group, w_router_expert, b_router_expert,
           w_expert_gate, w_expert_up, w_expert_down, ln2_g, ln2_b):
    B, S, D = x.shape
    depth = w_ada.shape[0]
    assert depth == 1, "single-layer block"
    fox_heads = b_fox_forget.shape[1]
    fox_w = fox_heads * HEAD_DIM
    hgrn_w = hgrn_norm_w.shape[1]
    ngroups = w_router_group.shape[2]
    nexp = w_router_expert.shape[2]
    nper = nexp // ngroups
    alpha = (2 * depth) ** 0.25
    T = B * S

    ada = _ada(c, w_ada[0], b_ada[0])
    sh1, sc1, g1, sh2, sc2, g2 = [a.reshape(B, 1, D) for a in jnp.split(ada, 6, axis=-1)]

    wi = w_in[0]
    o_ff = 3 * fox_w
    w_fox = jnp.pad(wi[:, :o_ff + fox_heads], ((0, 0), (0, LANES - fox_heads))).astype(BF16)
    w_rest = wi[:, o_ff + fox_heads:].astype(BF16)
    widths = [fox_w, fox_w, fox_w, LANES, hgrn_w, hgrn_w, hgrn_w, hgrn_w, D, D]
    segs, off = [], 0
    for n, w in enumerate(widths):
        if n == 4:
            off = 0
        segs.append((off, off + w))
        off += w
    fq, fk, fv, ffp, hq, hf, hi, hg, gf, gh = _inproj(x, sc1, sh1, w_fox, w_rest, segs)

    bias_p = jnp.zeros((1, LANES), F32).at[0, :fox_heads].set(b_fox_forget[0])
    cum = _foxcum(ffp, bias_p)
    y_fox = _fox(fq, fk, fv, cum)

    o_h = _hgrn(hq, hf, hi, hg, hgrn_lb_logits, hgrn_norm_w[0])

    wr = jnp.zeros((D, LANES), F32).at[:, :ngroups].set(w_router_group[0]).at[:, ngroups:ngroups + nexp].set(
        w_router_expert[0])
    wr_hi = lax.bitcast_convert_type(lax.bitcast_convert_type(wr, jnp.uint32) & jnp.uint32(0xFFFF0000), F32)
    wr = jnp.concatenate([wr_hi.astype(BF16), (wr - wr_hi).astype(BF16)], axis=1)
    br = jnp.zeros((1, LANES), F32).at[0, :ngroups].set(b_router_group[0]).at[0, ngroups:ngroups + nexp].set(
        b_router_expert[0])
    x1, h2, rinfo, fields, counts = _mix(
        y_fox, o_h, gf, gh, x, g1, sc2, sh2,
        w_up_fox[0].astype(BF16), w_up_hgrn[0].astype(BF16), w_out[0].astype(BF16),
        ln1_g[0].reshape(1, D), ln1_b[0].reshape(1, D), wr, br, alpha, ngroups, nper)

    tm_e = 512
    dt = D // WORD_LANES
    ntiles = (2 * T) // tm_e + nexp
    nslots = ntiles * tm_e
    cnt = counts[0, :nexp].astype(jnp.int32)
    padded = ((cnt + tm_e - 1) // tm_e) * tm_e
    ends = jnp.cumsum(padded)
    starts = ends - padded
    eid = fields[2:4].astype(jnp.int32)
    rank = fields[4:6].astype(jnp.int32)
    first = jnp.sum(jnp.where(eid[None] == jnp.arange(nexp, dtype=jnp.int32)[:, None, None],
                              starts[:, None, None], 0), axis=0)
    pos = first + rank
    tile_start = jnp.arange(ntiles, dtype=jnp.int32) * tm_e
    tile_block = jnp.minimum(jnp.arange(ntiles, dtype=jnp.int32), ends[-1] // tm_e - 1)
    tile_expert = jnp.minimum(jnp.sum((tile_start[:, None] >= ends[None, :]).astype(jnp.int32), axis=1), nexp - 1)
    tile_rows = jnp.clip(starts[tile_expert] + cnt[tile_expert] - tile_start, 0, tm_e)
    tile_expert = tile_expert[tile_block]
    rows = pos[:, None, :] + (jnp.arange(dt, dtype=jnp.int32) * nslots)[None, :, None]

    xs = _sc_scatter_rows(h2.reshape(dt * T, LANES), rows[0], rows[1], dt * nslots)
    ys = _experts(tile_expert, tile_rows, tile_block, xs.reshape(dt, nslots, LANES),
                  w_expert_gate[0], w_expert_up[0], w_expert_down[0], tm_e)
    yg = _sc_gather_rows(ys.reshape(dt * nslots, LANES), rows.reshape(2 * dt, T))
    return _combine(yg.reshape(2, dt, T, LANES), x1, rinfo, g2,
                    ln2_g[0].reshape(1, D), ln2_b[0].reshape(1, D), alpha)
```

```python
import functools

import jax
import jax.numpy as jnp
from jax import lax
from jax.experimental import pallas as pl
from jax.experimental.pallas import tpu as pltpu
from jax.experimental.pallas import tpu_sc as plsc

F32 = jnp.float32
BF16 = jnp.bfloat16
HIGHEST = lax.Precision.HIGHEST

LANES = 128
HEAD_DIM = 64
LN_EPS = 1e-5
RMS_EPS = 1e-6
LOG2E = 1.4426950408889634
NEG_BIG = -1e30
HCHUNK = 16
HBLOCK = 64
HGRN_SAFE_EXP = 60.0
ROW_TILE = 8
WORD_LANES = 2 * LANES
VMEM_LIMIT = 56 * 1024 * 1024


def _cparams(sem, vmem=VMEM_LIMIT):
    return pltpu.CompilerParams(dimension_semantics=sem, vmem_limit_bytes=vmem)


def _sigmoid(x):
    return 0.5 * jnp.tanh(0.5 * x) + 0.5


def _silu(x):
    return x * _sigmoid(x)


def _ada_kernel(c_ref, w_ref, b_ref, o_ref):
    c = c_ref[...]
    o_ref[...] = jnp.dot(_silu(c), w_ref[...], precision=HIGHEST,
                         preferred_element_type=F32) + b_ref[...]


def _ada(c, w_ada, b_ada):
    B, D = c.shape
    N = w_ada.shape[1]
    tn = 1024
    return pl.pallas_call(
        _ada_kernel,
        out_shape=jax.ShapeDtypeStruct((B, N), F32),
        grid=(N // tn,),
        in_specs=[pl.BlockSpec((B, D), lambda j: (0, 0)),
                  pl.BlockSpec((D, tn), lambda j: (0, j)),
                  pl.BlockSpec((1, tn), lambda j: (0, j))],
        out_specs=pl.BlockSpec((B, tn), lambda j: (0, j)),
        compiler_params=_cparams(("arbitrary",)),
    )(c, w_ada, b_ada.reshape(1, N))


N_FOX_SEGS = 4


def _inproj_kernel(x_ref, sc_ref, sh_ref, wf_ref, wr_ref,
                   fq_ref, fk_ref, fv_ref, ff_ref, hq_ref, hf_ref, hi_ref, hg_ref, gf_ref, gh_ref,
                   *, segs, q_scale):
    h = (x_ref[...] * (1.0 + sc_ref[...]) + sh_ref[...]).astype(BF16)
    outs = (fq_ref, fk_ref, fv_ref, ff_ref, hq_ref, hf_ref, hi_ref, hg_ref, gf_ref, gh_ref)
    for idx, (o_ref, (a, b)) in enumerate(zip(outs, segs)):
        w_ref = wf_ref if idx < N_FOX_SEGS else wr_ref
        r = jnp.dot(h, w_ref[:, a:b], preferred_element_type=F32)
        if idx == 0:
            r = r * q_scale
        o_ref[...] = r.astype(o_ref.dtype)


def _inproj(x, sc1, sh1, w_fox, w_rest, segs, tm=256):
    B, S, D = x.shape
    widths = [b - a for a, b in segs]
    dtypes = [BF16, BF16, BF16, F32, BF16, F32, BF16, BF16, BF16, BF16]
    out_shape = tuple(jax.ShapeDtypeStruct((B, S, w), dt) for w, dt in zip(widths, dtypes))
    out_specs = tuple(pl.BlockSpec((None, tm, w), lambda b, i: (b, i, 0)) for w in widths)
    vec = pl.BlockSpec((None, 1, D), lambda b, i: (b, 0, 0))
    return pl.pallas_call(
        functools.partial(_inproj_kernel, segs=tuple(segs), q_scale=HEAD_DIM ** -0.5 * LOG2E),
        out_shape=out_shape,
        grid=(B, S // tm),
        in_specs=[pl.BlockSpec((None, tm, D), lambda b, i: (b, i, 0)), vec, vec,
                  pl.BlockSpec(w_fox.shape, lambda b, i: (0, 0)),
                  pl.BlockSpec(w_rest.shape, lambda b, i: (0, 0))],
        out_specs=out_specs,
        compiler_params=_cparams(("parallel", "parallel")),
    )(x, sc1, sh1, w_fox, w_rest)


def _foxcum_kernel(ff_ref, b_ref, o_ref, *, blk):
    S = ff_ref.shape[0]
    r = lax.broadcasted_iota(jnp.int32, (blk, blk), 0)
    c = lax.broadcasted_iota(jnp.int32, (blk, blk), 1)
    lower = (r >= c).astype(F32)
    carry = jnp.zeros((1, LANES), F32)
    for j in range(S // blk):
        z = ff_ref[j * blk:(j + 1) * blk, :] + b_ref[...]
        lf = jnp.minimum(z, 0.0) - jnp.log(1.0 + jnp.exp(-jnp.abs(z)))
        cum = jnp.dot(lower, lf, precision=HIGHEST, preferred_element_type=F32) + carry
        o_ref[j * blk:(j + 1) * blk, :] = cum * LOG2E
        carry = cum[blk - 1:blk, :]


def _foxcum(ffp, bias_p, blk=256):
    B, S, _ = ffp.shape
    return pl.pallas_call(
        functools.partial(_foxcum_kernel, blk=blk),
        out_shape=jax.ShapeDtypeStruct((B, S, LANES), F32),
        grid=(B,),
        in_specs=[pl.BlockSpec((None, S, LANES), lambda b: (b, 0, 0)),
                  pl.BlockSpec((1, LANES), lambda b: (0, 0))],
        out_specs=pl.BlockSpec((None, S, LANES), lambda b: (b, 0, 0)),
        compiler_params=_cparams(("parallel",)),
    )(ffp, bias_p)


NCUM = 3


def _fox_kernel(q_ref, k_ref, v_ref, c_ref, o_ref, ka_sc, kb_sc, va_sc, vb_sc, *, tq, tk):
    p = pl.program_id(1)
    qi = pl.program_id(2)
    S = k_ref.shape[0]

    @pl.when(qi == 0)
    def _():
        lane = lax.broadcasted_iota(jnp.int32, (S, LANES), 1)
        rr = lax.broadcasted_iota(jnp.int32, (LANES, LANES), 0)
        cc = lax.broadcasted_iota(jnp.int32, (LANES, LANES), 1)
        rest = c_ref[...]
        placed = jnp.zeros((S, LANES), F32)
        for i in range(NCUM):
            piece = rest.astype(BF16)
            rest = rest - piece.astype(F32)
            sel = ((rr == 2 * p) & (cc == HEAD_DIM + i)) | ((rr == 2 * p + 1) & (cc == i))
            placed = placed + jnp.dot(piece, jnp.where(sel, 1.0, 0.0).astype(BF16), preferred_element_type=F32)
        k2 = k_ref[...].astype(F32)
        ka_sc[...] = jnp.where(lane < HEAD_DIM, k2, -placed).astype(BF16)
        kb_sc[...] = jnp.where(lane >= HEAD_DIM, k2, -placed).astype(BF16)
        vt = v_ref[...].astype(F32).T
        row = lax.broadcasted_iota(jnp.int32, (LANES, S), 0)
        va_sc[...] = jnp.where(row < HEAD_DIM, vt, jnp.where(row == HEAD_DIM, 1.0, 0.0)).astype(BF16)
        vb_sc[...] = jnp.where(row >= HEAD_DIM, vt, jnp.where(row == 0, 1.0, 0.0)).astype(BF16)

    q2 = q_ref[...].astype(F32)
    qlane = lax.broadcasted_iota(jnp.int32, (tq, LANES), 1)
    qa = jnp.where(qlane < HEAD_DIM, q2, jnp.where(qlane < HEAD_DIM + NCUM, 1.0, 0.0)).astype(BF16)
    qb = jnp.where(qlane >= HEAD_DIM, q2, jnp.where(qlane < NCUM, 1.0, 0.0)).astype(BF16)
    nsub = tq // tk

    def block(k0, carry, diag_off):
        q0 = 0 if diag_off is None else diag_off
        out = []
        for ksc, vsc, qh, (m, acc) in ((ka_sc, va_sc, qa, carry[:2]), (kb_sc, vb_sc, qb, carry[2:])):
            st = lax.dot_general(ksc[pl.ds(k0, tk), :], qh[q0:, :], (((1,), (1,)), ((), ())),
                                 preferred_element_type=F32)
            if diag_off is not None:
                st = jnp.where(lax.broadcasted_iota(jnp.int32, st.shape, 0)
                               <= lax.broadcasted_iota(jnp.int32, st.shape, 1), st, NEG_BIG)
            m_old = m[:, q0:]
            m_new = jnp.maximum(m_old, jnp.max(st, axis=0, keepdims=True))
            pt = jnp.exp2(st - m_new).astype(BF16)
            acc_new = (jnp.exp2(m_old - m_new) * acc[:, q0:]
                       + jnp.dot(vsc[:, pl.ds(k0, tk)], pt, preferred_element_type=F32))
            if q0:
                m_new = jnp.concatenate([m[:, :q0], m_new], axis=1)
                acc_new = jnp.concatenate([acc[:, :q0], acc_new], axis=1)
            out += [m_new, acc_new]
        return tuple(out)

    def group(j, carry):
        k0 = pl.multiple_of(j * (nsub * tk), nsub * tk)
        for u in range(nsub):
            carry = block(k0 + u * tk, carry, None)
        return carry

    m0 = jnp.full((1, tq), NEG_BIG, F32)
    a0 = jnp.zeros((LANES, tq), F32)
    carry = lax.fori_loop(0, qi, group, (m0, a0, m0, a0))
    for d in range(nsub):
        carry = block(pl.multiple_of(qi * tq + d * tk, tk), carry, d * tk)
    _, aa, _, ab = carry
    row = lax.broadcasted_iota(jnp.int32, (LANES, tq), 0)
    ot = jnp.where(row < HEAD_DIM, aa * (1.0 / aa[HEAD_DIM:HEAD_DIM + 1, :]), ab * (1.0 / ab[0:1, :]))
    o_ref[...] = ot.T.astype(o_ref.dtype)


def _fox(fq, fk, fv, cum, tq=2048, tk=512):
    B, S, W = fq.shape
    tq = min(tq, S)
    assert tq % tk == 0 and S % tq == 0
    npairs = W // LANES
    return pl.pallas_call(
        functools.partial(_fox_kernel, tq=tq, tk=tk),
        out_shape=jax.ShapeDtypeStruct((B, S, W), BF16),
        grid=(B, npairs, S // tq),
        in_specs=[pl.BlockSpec((None, tq, LANES), lambda b, p, i: (b, i, p)),
                  pl.BlockSpec((None, S, LANES), lambda b, p, i: (b, 0, p)),
                  pl.BlockSpec((None, S, LANES), lambda b, p, i: (b, 0, p)),
                  pl.BlockSpec((None, S, LANES), lambda b, p, i: (b, 0, 0))],
        out_specs=pl.BlockSpec((None, tq, LANES), lambda b, p, i: (b, i, p)),
        scratch_shapes=[pltpu.VMEM((S, LANES), BF16), pltpu.VMEM((S, LANES), BF16),
                        pltpu.VMEM((LANES, S), BF16), pltpu.VMEM((LANES, S), BF16)],
        compiler_params=_cparams(("parallel", "parallel", "arbitrary")),
    )(fq, fk, fv, cum)


def _hgrn_kernel(hq_ref, hf_ref, hi_ref, hg_ref, lb_ref, nw_ref, o_ref,
                 b_sc, kk_sc, qq_sc, o_sc, w1_sc, w2_sc, w3_sc, w4_sc, w5_sc,
                 p_sc, st16_sc, dec_sc, st64_sc):
    S = hq_ref.shape[0]
    C = HCHUNK
    nchunks = S // C
    BLK = HBLOCK
    nblk = S // BLK

    lg = lb_ref[...]
    e = jnp.exp(lg - jnp.max(lg, axis=0, keepdims=True))
    lb = e[0:1, :] / jnp.sum(e, axis=0, keepdims=True)

    f = lb + (1.0 - lb) * _sigmoid(hf_ref[...])
    lf = jnp.log(f)
    kk_sc[...] = 1.0 - f
    qq_sc[...] = _silu(hq_ref[...].astype(F32))

    row = lax.broadcasted_iota(jnp.int32, (S, LANES), 0)
    rb = 4 * BLK
    tr = lax.broadcasted_iota(jnp.int32, (rb, rb), 0)
    tc = lax.broadcasted_iota(jnp.int32, (rb, rb), 1)
    tri = jnp.where(((tr & -BLK) == (tc & -BLK)) & (tc <= tr), 1.0, 0.0).astype(BF16)
    pieces, rest = [], lf
    for _ in range(NCUM):
        top = pltpu.bitcast(pltpu.bitcast(rest, jnp.uint32) & jnp.uint32(0xFFFF0000), F32)
        pieces.append(top.astype(BF16))
        rest = rest - top
    lf3 = jnp.concatenate(pieces, axis=1)
    for j in range(S // rb):
        c3 = jnp.dot(tri, lf3[j * rb:(j + 1) * rb, :], preferred_element_type=F32)
        b_sc[j * rb:(j + 1) * rb, :] = c3[:, :LANES] + c3[:, LANES:2 * LANES] + c3[:, 2 * LANES:]
    safe = jnp.max(-b_sc[...].reshape(nblk, BLK, LANES)[:, BLK - 1, :]) <= HGRN_SAFE_EXP

    lane = lax.broadcasted_iota(jnp.int32, (C, LANES), 1)
    sr = lax.broadcasted_iota(jnp.int32, (LANES, LANES), 0)
    scn = lax.broadcasted_iota(jnp.int32, (LANES, LANES), 1)
    same_head = (sr // HEAD_DIM) == (scn // HEAD_DIM)

    @pl.when(safe)
    def _factorised():
        qh_sc, kh_sc, ke_sc, qd_sc, k2_sc = w1_sc, w2_sc, w3_sc, w4_sc, w5_sc
        SB = 2 * BLK
        nsb = S // SB
        bb = b_sc[...]
        dblk = jnp.exp(bb.reshape(nblk, BLK, LANES)[:, BLK - 1:BLK, :])
        dfull = jnp.broadcast_to(dblk, (nblk, BLK, LANES)).reshape(S, LANES)
        second = (row & BLK) != 0
        d_prev = pltpu.roll(dfull, BLK, axis=0)
        d_next = pltpu.roll(dfull, S - BLK, axis=0)
        qh = qq_sc[...] * jnp.exp(bb)
        qh_sc[...] = qh.astype(BF16)
        qd_sc[...] = (qh * jnp.where(second, d_prev, 1.0)).astype(BF16)
        kh = kk_sc[...] * jnp.exp(-bb)
        kh_sc[...] = kh.astype(BF16)
        ke = kh * dfull
        ke_sc[...] = ke.astype(BF16)
        k2_sc[...] = (ke * jnp.where(second, 1.0, d_next)).astype(BF16)
        d3 = dfull.reshape(nsb, SB, LANES)
        dec_sc[pl.ds(0, nsb), :] = d3[:, 0, :] * d3[:, BLK, :]
        unroll = 16
        tn = (((0,), (0,)), ((), ()))
        nt = (((1,), (1,)), ((), ()))

        def scan(g, st):
            for u in range(unroll):
                i = g * unroll + u
                r0 = pl.multiple_of(i * SB, SB)
                st64_sc[i] = st.astype(BF16)
                upd = lax.dot_general(hi_ref[pl.ds(r0, SB), :], k2_sc[pl.ds(r0, SB), :], tn,
                                      preferred_element_type=F32)
                st = st * dec_sc[pl.ds(i, 1), :] + jnp.where(same_head, upd, 0.0)
            return st

        lax.fori_loop(0, nsb // unroll, scan, jnp.zeros((LANES, LANES), F32))

        r = lax.broadcasted_iota(jnp.int32, (2 * SB, 2 * SB), 0)
        c = lax.broadcasted_iota(jnp.int32, (2 * SB, 2 * SB), 1)
        t = r & (SB - 1)
        visible = (((c < SB) & ((t & BLK) == (c & BLK)) & ((t & (BLK - 1)) >= (c & (BLK - 1))))
                   | ((c >= SB) & (c < SB + BLK) & (t >= BLK)))
        plane = lax.broadcasted_iota(jnp.int32, (SB, LANES), 1)
        pad = jnp.zeros((BLK, LANES), BF16)

        def readout(g, _):
            for u in range(unroll):
                i = g * unroll + u
                r0 = pl.multiple_of(i * SB, SB)
                vb = hi_ref[pl.ds(r0, SB), :]
                qh2 = qh_sc[pl.ds(r0, SB), :]
                q2 = jnp.concatenate([jnp.where(plane < HEAD_DIM, qh2, jnp.zeros_like(qh2)),
                                      jnp.where(plane >= HEAD_DIM, qh2, jnp.zeros_like(qh2))], axis=0)
                kext = jnp.concatenate([kh_sc[pl.ds(r0, SB), :], ke_sc[pl.ds(r0, BLK), :], pad], axis=0)
                vext = jnp.concatenate([vb, vb[:BLK], pad], axis=0)
                sc = lax.dot_general(q2, kext, nt, preferred_element_type=F32)
                sc = jnp.where(visible, sc, 0.0).astype(BF16)
                out = jnp.dot(sc, vext, preferred_element_type=F32)
                o_inter = lax.dot_general(qd_sc[pl.ds(r0, SB), :], st64_sc[i], nt, preferred_element_type=F32)
                o_sc[pl.ds(r0, SB), :] = jnp.where(plane < HEAD_DIM, out[:SB], out[SB:]) + o_inter
            return 0

        lax.fori_loop(0, nsb // unroll, readout, 0)

    @pl.when(jnp.logical_not(safe))
    def _direct():
        qt_sc, kt_sc, s_sc, a2_sc = w1_sc, w2_sc, w3_sc, b_sc
        bb = b_sc[...]
        cl = jnp.broadcast_to(bb.reshape(nchunks, C, LANES)[:, C - 1:C, :], (nchunks, C, LANES)).reshape(S, LANES)
        aa = bb - jnp.where((row & (BLK - 1)) >= C, pltpu.roll(cl, C, axis=0), 0.0)
        al = jnp.broadcast_to(aa.reshape(nchunks, C, LANES)[:, C - 1:C, :], (nchunks, C, LANES)).reshape(S, LANES)
        qt_sc[...] = (qq_sc[...] * jnp.exp(aa)).astype(BF16)
        kt_sc[...] = (kk_sc[...] * jnp.exp(al - aa)).astype(BF16)
        dec_sc[...] = jnp.exp(aa.reshape(nchunks, C, LANES)[:, C - 1, :])
        a2_sc[...] = aa * LOG2E
        trow = lax.broadcasted_iota(jnp.int32, (C, LANES), 0)

        def gen(c, _):
            r0 = pl.multiple_of(c * C, C)
            ac = a2_sc[pl.ds(r0, C), :]
            qc = qq_sc[pl.ds(r0, C), :]
            kc = kk_sc[pl.ds(r0, C), :]
            half = C // 2
            for s in range(C):
                if s < half:
                    dec = jnp.exp2(jnp.where(trow >= s, ac - ac[s:s + 1, :], NEG_BIG))
                    p = qc * (kc[s:s + 1, :] * dec)
                else:
                    dec = jnp.exp2(jnp.where(trow[half:] >= s, ac[half:] - ac[s:s + 1, :], NEG_BIG))
                    p = jnp.concatenate([jnp.zeros((half, LANES), F32), qc[half:] * (kc[s:s + 1, :] * dec)],
                                        axis=0)
                p_sc[pl.ds(r0, C), s * LANES:(s + 1) * LANES] = p.astype(BF16)
            return 0

        lax.fori_loop(0, nchunks, gen, 0)

        er = lax.broadcasted_iota(jnp.int32, (C * LANES, LANES), 0)
        ec = lax.broadcasted_iota(jnp.int32, (C * LANES, LANES), 1)
        emat = (ec == ((er & (LANES - 1)) // HEAD_DIM) * C + er // LANES).astype(BF16)
        rb = 256

        def red(i, _):
            r0 = pl.multiple_of(i * rb, rb)
            s_sc[pl.ds(r0, rb), :] = jnp.dot(p_sc[pl.ds(r0, rb), :], emat,
                                             preferred_element_type=F32).astype(BF16)
            return 0

        lax.fori_loop(0, S // rb, red, 0)

        unroll = 16

        def scan(g, st):
            for u in range(unroll):
                c = g * unroll + u
                r0 = pl.multiple_of(c * C, C)
                st16_sc[c] = st.astype(BF16)
                upd = lax.dot_general(hi_ref[pl.ds(r0, C), :], kt_sc[pl.ds(r0, C), :],
                                      (((0,), (0,)), ((), ())), preferred_element_type=F32)
                st = st * dec_sc[pl.ds(c, 1), :] + jnp.where(same_head, upd, 0.0)
            return st

        lax.fori_loop(0, nchunks // unroll, scan, jnp.zeros((LANES, LANES), F32))

        def readout(g, _):
            for u in range(unroll):
                c = g * unroll + u
                r0 = pl.multiple_of(c * C, C)
                vc = hi_ref[pl.ds(r0, C), :]
                o_inter = lax.dot_general(qt_sc[pl.ds(r0, C), :], st16_sc[c],
                                          (((1,), (1,)), ((), ())), preferred_element_type=F32)
                v2 = jnp.concatenate([jnp.where(lane < HEAD_DIM, vc, jnp.zeros_like(vc)),
                                      jnp.where(lane >= HEAD_DIM, vc, jnp.zeros_like(vc))], axis=0)
                o_intra = jnp.dot(s_sc[pl.ds(r0, C), :][:, :2 * C], v2, preferred_element_type=F32)
                o_sc[pl.ds(r0, C), :] = o_inter + o_intra
            return 0

        lax.fori_loop(0, nchunks // unroll, readout, 0)

    o = o_sc[...]
    ones_head = jnp.where(same_head, 1.0, 0.0).astype(BF16)
    sq = o * o
    sq_top = pltpu.bitcast(pltpu.bitcast(sq, jnp.uint32) & jnp.uint32(0xFFFF0000), F32)
    ms = (jnp.dot(sq_top.astype(BF16), ones_head, preferred_element_type=F32)
          + jnp.dot((sq - sq_top).astype(BF16), ones_head, preferred_element_type=F32)) * (1.0 / HEAD_DIM)
    y = o * lax.rsqrt(ms + RMS_EPS) * nw_ref[...]
    o_ref[...] = (y * _silu(hg_ref[...].astype(F32))).astype(o_ref.dtype)


def _hgrn(hq, hf, hi, hg, lb_logits, norm_w):
    B, S, W = hq.shape
    npairs = W // LANES
    nrows = lb_logits.shape[0]
    seq = pl.BlockSpec((None, S, LANES), lambda b, p: (b, 0, p))
    return pl.pallas_call(
        _hgrn_kernel,
        out_shape=jax.ShapeDtypeStruct((B, S, W), BF16),
        grid=(B, npairs),
        in_specs=[seq, seq, seq, seq,
                  pl.BlockSpec((nrows, LANES), lambda b, p: (0, p)),
                  pl.BlockSpec((1, LANES), lambda b, p: (0, p))],
        out_specs=seq,
        scratch_shapes=[pltpu.VMEM((S, LANES), F32),
                        pltpu.VMEM((S, LANES), F32),
                        pltpu.VMEM((S, LANES), F32),
                        pltpu.VMEM((S, LANES), F32),
                        pltpu.VMEM((S, LANES), BF16),
                        pltpu.VMEM((S, LANES), BF16),
                        pltpu.VMEM((S, LANES), BF16),
                        pltpu.VMEM((S, LANES), BF16),
                        pltpu.VMEM((S, LANES), BF16),
                        pltpu.VMEM((S, HCHUNK * LANES), BF16),
                        pltpu.VMEM((S // HCHUNK, LANES, LANES), BF16),
                        pltpu.VMEM((S // HCHUNK, LANES), F32),
                        pltpu.VMEM((S // HBLOCK, LANES, LANES), BF16)],
        compiler_params=_cparams(("parallel", "parallel")),
    )(hq, hf, hi, hg, lb_logits, norm_w.reshape(1, W))


def _layer_norm(v, g, b):
    mu = jnp.mean(v, axis=-1, keepdims=True)
    d = v - mu
    var = jnp.mean(d * d, axis=-1, keepdims=True)
    return d * lax.rsqrt(var + LN_EPS) * g + b


def _bf16_bits(x):
    u = pltpu.bitcast(x, jnp.uint32)
    return (u + jnp.uint32(0x7FFF) + ((u >> 16) & jnp.uint32(1))) & jnp.uint32(0xFFFF0000)


def _store_chunks(ref, val):
    n = ref.shape[0]
    for j in range(n):
        lo = _bf16_bits(val[:, j * LANES:(j + 1) * LANES]) >> 16
        hi = _bf16_bits(val[:, (j + n) * LANES:(j + n + 1) * LANES])
        ref[j] = pltpu.bitcast(lo | hi, F32)


def _load_chunks(ref):
    words = [pltpu.bitcast(ref[j], jnp.uint32) for j in range(ref.shape[0])]
    lo = [pltpu.bitcast(w << 16, F32) for w in words]
    hi = [pltpu.bitcast(w & jnp.uint32(0xFFFF0000), F32) for w in words]
    return jnp.concatenate(lo + hi, axis=1)


def _mix_kernel(yf_ref, oh_ref, gf_ref, gh_ref, x_ref, g1_ref, sc2_ref, sh2_ref,
                wuf_ref, wuh_ref, wo_ref, lg_ref, lbias_ref, wr_ref, br_ref,
                x1_ref, h2_ref, ri_ref, rt_ref, cnt_ref, carry_sc, *, alpha, ngroups, nper):
    first = (pl.program_id(0) == 0) & (pl.program_id(1) == 0)

    @pl.when(first)
    def _():
        carry_sc[...] = jnp.zeros_like(carry_sc)

    tm = x_ref.shape[0]
    yf = jnp.dot(yf_ref[...], wuf_ref[...], preferred_element_type=F32)
    yh = jnp.dot(oh_ref[...], wuh_ref[...], preferred_element_type=F32)
    merged = _sigmoid(gf_ref[...].astype(F32)) * yf + _sigmoid(gh_ref[...].astype(F32)) * yh
    y = jnp.dot(merged.astype(BF16), wo_ref[...], preferred_element_type=F32)
    x1 = _layer_norm(alpha * x_ref[...] + g1_ref[...] * y, lg_ref[...], lbias_ref[...])
    x1_ref[...] = x1
    h2 = x1 * (1.0 + sc2_ref[...]) + sh2_ref[...]
    _store_chunks(h2_ref, h2)

    h_top = pltpu.bitcast(pltpu.bitcast(h2, jnp.uint32) & jnp.uint32(0xFFFF0000), F32)
    h_hi = h_top.astype(BF16)
    h_lo = (h2 - h_top).astype(BF16)
    hh = jnp.dot(h_hi, wr_ref[...], preferred_element_type=F32)
    logits = (hh[:, :LANES] + hh[:, LANES:]
              + jnp.dot(h_lo, wr_ref[:, :LANES], preferred_element_type=F32)) + br_ref[...]
    lane = lax.broadcasted_iota(jnp.int32, (tm, LANES), 1)
    big = jnp.int32(1 << 20)

    def argmax_first(vals, mask):
        mx = jnp.max(jnp.where(mask, vals, -jnp.inf), axis=1, keepdims=True)
        idx = jnp.min(jnp.where(mask & (vals == mx), lane, big), axis=1, keepdims=True)
        return mx, idx

    gmask = lane < ngroups
    gmax = jnp.max(jnp.where(gmask, logits, -jnp.inf), axis=1, keepdims=True)
    gexp = jnp.where(gmask, jnp.exp(logits - gmax), 0.0)
    gprob = gexp / jnp.sum(gexp, axis=1, keepdims=True)
    g_w, g_idx = argmax_first(gprob, gmask)

    lo = ngroups + g_idx * nper
    emask = (lane >= lo) & (lane < lo + nper)
    emax = jnp.max(jnp.where(emask, logits, -jnp.inf), axis=1, keepdims=True)
    eexp = jnp.where(emask, jnp.exp(logits - emax), 0.0)
    eprob = eexp / jnp.sum(eexp, axis=1, keepdims=True)
    p0, i0 = argmax_first(eprob, emask)
    p1, i1 = argmax_first(eprob, emask & (lane != i0))
    den = p0 + p1
    w0 = p0 / den * g_w
    w1 = p1 / den * g_w
    e0 = i0 - ngroups
    e1 = i1 - ngroups

    oh = ((lane == e0) | (lane == e1)).astype(F32)
    r = lax.broadcasted_iota(jnp.int32, (tm, tm), 0)
    c = lax.broadcasted_iota(jnp.int32, (tm, tm), 1)
    strict_lower = (c < r).astype(BF16)
    before = jnp.dot(strict_lower, oh.astype(BF16), preferred_element_type=F32) + carry_sc[...]
    rank0 = jnp.sum(jnp.where(lane == e0, before, 0.0), axis=1, keepdims=True)
    rank1 = jnp.sum(jnp.where(lane == e1, before, 0.0), axis=1, keepdims=True)
    carry_sc[...] = carry_sc[...] + jnp.sum(oh, axis=0, keepdims=True)
    cnt_ref[...] = carry_sc[...]

    info = jnp.where(lane == 0, w0, 0.0)
    info = jnp.where(lane == 1, w1, info)
    info = jnp.where(lane == 2, e0.astype(F32), info)
    info = jnp.where(lane == 3, e1.astype(F32), info)
    info = jnp.where(lane == 4, rank0, info)
    info = jnp.where(lane == 5, rank1, info)
    ri_ref[...] = info
    rt_ref[...] = info.T[:ROW_TILE, :]


def _mix(yf, oh, gf, gh, x, g1, sc2, sh2, wuf, wuh, wo, ln_g, ln_b, wr, br, alpha, ngroups, nper, tm=512):
    B, S, D = x.shape
    W = yf.shape[2]
    tok = lambda w: pl.BlockSpec((None, tm, w), lambda b, i: (b, i, 0))
    vec = pl.BlockSpec((None, 1, D), lambda b, i: (b, 0, 0))
    full = lambda a: pl.BlockSpec(a.shape, lambda b, i: (0,) * a.ndim)
    return pl.pallas_call(
        functools.partial(_mix_kernel, alpha=alpha, ngroups=ngroups, nper=nper),
        out_shape=(jax.ShapeDtypeStruct((B, S, D), F32),
                   jax.ShapeDtypeStruct((D // WORD_LANES, B * S, LANES), F32),
                   jax.ShapeDtypeStruct((B, S, LANES), F32),
                   jax.ShapeDtypeStruct((ROW_TILE, B * S), F32),
                   jax.ShapeDtypeStruct((1, LANES), F32)),
        grid=(B, S // tm),
        in_specs=[tok(W), tok(W), tok(D), tok(D), tok(D), vec, vec, vec,
                  full(wuf), full(wuh), full(wo), full(ln_g), full(ln_b), full(wr), full(br)],
        out_specs=(tok(D),
                   pl.BlockSpec((D // WORD_LANES, tm, LANES), lambda b, i: (0, b * (S // tm) + i, 0)),
                   tok(LANES),
                   pl.BlockSpec((ROW_TILE, tm), lambda b, i: (0, b * (S // tm) + i)),
                   pl.BlockSpec((1, LANES), lambda b, i: (0, 0))),
        scratch_shapes=[pltpu.VMEM((1, LANES), F32)],
        compiler_params=_cparams(("arbitrary", "arbitrary")),
    )(yf, oh, gf, gh, x, g1, sc2, sh2, wuf, wuh, wo, ln_g, ln_b, wr, br)


def _sc_mesh():
    return plsc.VectorSubcoreMesh(core_axis_name="core", subcore_axis_name="subcore")


def _sc_pipeline(body, grid, in_specs, out_specs):
    return pltpu.emit_pipeline(body, grid=grid, in_specs=in_specs, out_specs=out_specs,
                               core_axis_name=("core", "subcore"),
                               dimension_semantics=(pltpu.PARALLEL,) * len(grid))


def _sc_scatter_rows(src, rows_a, rows_b, n_out):
    nj, t = rows_a.shape
    nc = t // LANES

    @pl.kernel(out_type=jax.ShapeDtypeStruct((n_out, LANES), src.dtype), mesh=_sc_mesh(), scratch_types=[])
    def scatter(x_hbm, a_hbm, b_hbm, o_hbm):
        def body(x_vmem, a_vmem, b_vmem):
            pltpu.sync_copy(x_vmem, o_hbm.at[a_vmem.at[0]])
            pltpu.sync_copy(x_vmem, o_hbm.at[b_vmem.at[0]])

        idx = pl.BlockSpec((1, LANES), lambda j, c: (j, c))
        _sc_pipeline(body, (nj, nc), [pl.BlockSpec((LANES, LANES), lambda j, c: (j * nc + c, 0)), idx, idx],
                     [])(x_hbm, a_hbm, b_hbm)

    return scatter(src, rows_a, rows_b)


def _sc_gather_rows(table, rows):
    nr, t = rows.shape
    nc = t // LANES

    @pl.kernel(out_type=jax.ShapeDtypeStruct((nr * t, LANES), table.dtype), mesh=_sc_mesh(), scratch_types=[])
    def gather(x_hbm, i_hbm, o_hbm):
        def body(i_vmem, o_vmem):
            pltpu.sync_copy(x_hbm.at[i_vmem.at[0]], o_vmem)

        _sc_pipeline(body, (nr, nc), [pl.BlockSpec((1, LANES), lambda r, c: (r, c))],
                     [pl.BlockSpec((LANES, LANES), lambda r, c: (r * nc + c, 0))])(i_hbm, o_hbm)

    return gather(table, rows)


def _experts_kernel(te_ref, tn_ref, tb_ref, x_ref, wg_ref, wu_ref, wd_ref, o_ref):
    del tb_ref
    nrows = tn_ref[pl.program_id(0)]

    @pl.when(nrows > 0)
    def _():
        x = _load_chunks(x_ref)
        x = jnp.where(lax.broadcasted_iota(jnp.int32, x.shape, 0) < nrows, x, 0.0).astype(BF16)
        g = jnp.dot(x, wg_ref[...].astype(BF16), preferred_element_type=F32)
        u = jnp.dot(x, wu_ref[...].astype(BF16), preferred_element_type=F32)
        hid = (_silu(g) * u).astype(BF16)
        _store_chunks(o_ref, jnp.dot(hid, wd_ref[...].astype(BF16), preferred_element_type=F32))


def _experts(tile_expert, tile_rows, tile_block, xs, wg, wu, wd, tm):
    E, D, FF = wg.shape
    dt = D // WORD_LANES
    ntiles = tile_expert.shape[0]
    rows = pl.BlockSpec((dt, tm, LANES), lambda i, te, tn, tb: (0, tb[i], 0))
    grid_spec = pltpu.PrefetchScalarGridSpec(
        num_scalar_prefetch=3,
        grid=(ntiles,),
        in_specs=[rows,
                  pl.BlockSpec((None, D, FF), lambda i, te, tn, tb: (te[i], 0, 0)),
                  pl.BlockSpec((None, D, FF), lambda i, te, tn, tb: (te[i], 0, 0)),
                  pl.BlockSpec((None, FF, D), lambda i, te, tn, tb: (te[i], 0, 0))],
        out_specs=rows,
    )
    return pl.pallas_call(
        _experts_kernel,
        out_shape=jax.ShapeDtypeStruct((dt, ntiles * tm, LANES), F32),
        grid_spec=grid_spec,
        compiler_params=_cparams(("arbitrary",)),
    )(tile_expert, tile_rows, tile_block, xs, wg, wu, wd)


def _combine_kernel(yg_ref, x1_ref, ri_ref, g2_ref, lg_ref, lb_ref, o_ref, *, alpha):
    ri = ri_ref[...]
    y = ri[:, 0:1] * _load_chunks(yg_ref.at[0]) + ri[:, 1:2] * _load_chunks(yg_ref.at[1])
    o_ref[...] = _layer_norm(alpha * x1_ref[...] + g2_ref[...] * y, lg_ref[...], lb_ref[...])


def _combine(yg, x1, rinfo, g2, ln_g, ln_b, alpha, tm=512):
    B, S, D = x1.shape
    nb = S // tm
    return pl.pallas_call(
        functools.partial(_combine_kernel, alpha=alpha),
        out_shape=jax.ShapeDtypeStruct((B, S, D), F32),
        grid=(B, nb),
        in_specs=[pl.BlockSpec((2, D // WORD_LANES, tm, LANES), lambda b, i: (0, 0, b * nb + i, 0)),
                  pl.BlockSpec((None, tm, D), lambda b, i: (b, i, 0)),
                  pl.BlockSpec((None, tm, LANES), lambda b, i: (b, i, 0)),
                  pl.BlockSpec((None, 1, D), lambda b, i: (b, 0, 0)),
                  pl.BlockSpec((1, D), lambda b, i: (0, 0)),
                  pl.BlockSpec((1, D), lambda b, i: (0, 0))],
        out_specs=pl.BlockSpec((None, tm, D), lambda b, i: (b, i, 0)),
        compiler_params=_cparams(("parallel", "parallel")),
    )(yg, x1, rinfo, g2, ln_g, ln_b)


def kernel(x, c, w_ada, b_ada, w_in, b_fox_forget, hgrn_lb_logits, hgrn_norm_w, w_up_fox, w_up_hgrn, w_out,
           ln1_g, ln1_b, w_router_group, b_router_group, w_router_expert, b_router_expert,
           w_expert_gate, w_expert_up, w_expert_down, ln2_g, ln2_b):
    B, S, D = x.shape
    depth = w_ada.shape[0]
    assert depth == 1, "single-layer block"
    fox_heads = b_fox_forget.shape[1]
    fox_w = fox_heads * HEAD_DIM
    hgrn_w = hgrn_norm_w.shape[1]
    ngroups = w_router_group.shape[2]
    nexp = w_router_expert.shape[2]
    nper = nexp // ngroups
    alpha = (2 * depth) ** 0.25
    T = B * S

    ada = _ada(c, w_ada[0], b_ada[0])
    sh1, sc1, g1, sh2, sc2, g2 = [a.reshape(B, 1, D) for a in jnp.split(ada, 6, axis=-1)]

    wi = w_in[0]
    o_ff = 3 * fox_w
    w_fox = jnp.pad(wi[:, :o_ff + fox_heads], ((0, 0), (0, LANES - fox_heads))).astype(BF16)
    w_rest = wi[:, o_ff + fox_heads:].astype(BF16)
    widths = [fox_w, fox_w, fox_w, LANES, hgrn_w, hgrn_w, hgrn_w, hgrn_w, D, D]
    segs, off = [], 0
    for n, w in enumerate(widths):
        if n == 4:
            off = 0
        segs.append((off, off + w))
        off += w
    fq, fk, fv, ffp, hq, hf, hi, hg, gf, gh = _inproj(x, sc1, sh1, w_fox, w_rest, segs)

    bias_p = jnp.zeros((1, LANES), F32).at[0, :fox_heads].set(b_fox_forget[0])
    cum = _foxcum(ffp, bias_p)
    y_fox = _fox(fq, fk, fv, cum)

    o_h = _hgrn(hq, hf, hi, hg, hgrn_lb_logits, hgrn_norm_w[0])

    wr = jnp.zeros((D, LANES), F32).at[:, :ngroups].set(w_router_group[0]).at[:, ngroups:ngroups + nexp].set(
        w_router_expert[0])
    wr_hi = lax.bitcast_convert_type(lax.bitcast_convert_type(wr, jnp.uint32) & jnp.uint32(0xFFFF0000), F32)
    wr = jnp.concatenate([wr_hi.astype(BF16), (wr - wr_hi).astype(BF16)], axis=1)
    br = jnp.zeros((1, LANES), F32).at[0, :ngroups].set(b_router_group[0]).at[0, ngroups:ngroups + nexp].set(
        b_router_expert[0])
    x1, h2, rinfo, fields, counts = _mix(
        y_fox, o_h, gf, gh, x, g1, sc2, sh2,
        w_up_fox[0].astype(BF16), w_up_hgrn[0].astype(BF16), w_out[0].astype(BF16),
        ln1_g[0].reshape(1, D), ln1_b[0].reshape(1, D), wr, br, alpha, ngroups, nper)

    tm_e = 512
    dt = D // WORD_LANES
    ntiles = (2 * T) // tm_e + nexp
    nslots = ntiles * tm_e
    cnt = counts[0, :nexp].astype(jnp.int32)
    padded = ((cnt + tm_e - 1) // tm_e) * tm_e
    ends = jnp.cumsum(padded)
    starts = ends - padded
    eid = fields[2:4].astype(jnp.int32)
    rank = fields[4:6].astype(jnp.int32)
    first = jnp.sum(jnp.where(eid[None] == jnp.arange(nexp, dtype=jnp.int32)[:, None, None],
                              starts[:, None, None], 0), axis=0)
    pos = first + rank
    tile_start = jnp.arange(ntiles, dtype=jnp.int32) * tm_e
    tile_block = jnp.minimum(jnp.arange(ntiles, dtype=jnp.int32), ends[-1] // tm_e - 1)
    tile_expert = jnp.minimum(jnp.sum((tile_start[:, None] >= ends[None, :]).astype(jnp.int32), axis=1), nexp - 1)
    tile_rows = jnp.clip(starts[tile_expert] + cnt[tile_expert] - tile_start, 0, tm_e)
    tile_expert = tile_expert[tile_block]
    rows = pos[:, None, :] + (jnp.arange(dt, dtype=jnp.int32) * nslots)[None, :, None]

    xs = _sc_scatter_rows(h2.reshape(dt * T, LANES), rows[0], rows[1], dt * nslots)
    ys = _experts(tile_expert, tile_rows, tile_block, xs.reshape(dt, nslots, LANES),
                  w_expert_gate[0], w_expert_up[0], w_expert_down[0], tm_e)
    yg = _sc_gather_rows(ys.reshape(dt * nslots, LANES), rows.reshape(2 * dt, T))
    return _combine(yg.reshape(2, dt, T, LANES), x1, rinfo, g2,
                    ln2_g[0].reshape(1, D), ln2_b[0].reshape(1, D), alpha)
```

```python
import functools

import jax
import jax.numpy as jnp
from jax import lax
from jax.experimental import pallas as pl
from jax.experimental.pallas import tpu as pltpu
from jax.experimental.pallas import tpu_sc as plsc

F32 = jnp.float32
BF16 = jnp.bfloat16
HIGHEST = lax.Precision.HIGHEST

LANES = 128
HEAD_DIM = 64
LN_EPS = 1e-5
RMS_EPS = 1e-6
LOG2E = 1.4426950408889634
NEG_BIG = -1e30
HCHUNK = 16
HBLOCK = 64
HGRN_SAFE_EXP = 60.0
ROW_TILE = 8
WORD_LANES = 2 * LANES
VMEM_LIMIT = 56 * 1024 * 1024


def _cparams(sem, vmem=VMEM_LIMIT):
    return pltpu.CompilerParams(dimension_semantics=sem, vmem_limit_bytes=vmem)


def _sigmoid(x):
    return 0.5 * jnp.tanh(0.5 * x) + 0.5


def _silu(x):
    return x * _sigmoid(x)


def _ada_kernel(c_ref, w_ref, b_ref, o_ref):
    c = c_ref[...]
    o_ref[...] = jnp.dot(_silu(c), w_ref[...], precision=HIGHEST,
                         preferred_element_type=F32) + b_ref[...]


def _ada(c, w_ada, b_ada):
    B, D = c.shape
    N = w_ada.shape[1]
    tn = 1024
    return pl.pallas_call(
        _ada_kernel,
        out_shape=jax.ShapeDtypeStruct((B, N), F32),
        grid=(N // tn,),
        in_specs=[pl.BlockSpec((B, D), lambda j: (0, 0)),
                  pl.BlockSpec((D, tn), lambda j: (0, j)),
                  pl.BlockSpec((1, tn), lambda j: (0, j))],
        out_specs=pl.BlockSpec((B, tn), lambda j: (0, j)),
        compiler_params=_cparams(("arbitrary",)),
    )(c, w_ada, b_ada.reshape(1, N))


N_FOX_SEGS = 4
SILU_SEGS = (4, 7)
SIGMOID_SEGS = (8, 9)


def _inproj_kernel(x_ref, sc_ref, sh_ref, wf_ref, wr_ref,
                   fq_ref, fk_ref, fv_ref, ff_ref, hq_ref, hf_ref, hi_ref, hg_ref, gf_ref, gh_ref,
                   *, segs, q_scale):
    h = (x_ref[...] * (1.0 + sc_ref[...]) + sh_ref[...]).astype(BF16)
    outs = (fq_ref, fk_ref, fv_ref, ff_ref, hq_ref, hf_ref, hi_ref, hg_ref, gf_ref, gh_ref)
    for idx, (o_ref, (a, b)) in enumerate(zip(outs, segs)):
        w_ref = wf_ref if idx < N_FOX_SEGS else wr_ref
        r = jnp.dot(h, w_ref[:, a:b], preferred_element_type=F32)
        if idx == 0:
            r = r * q_scale
        elif idx in SILU_SEGS:
            r = _silu(r)
        elif idx in SIGMOID_SEGS:
            r = _sigmoid(r)
        o_ref[...] = r.astype(o_ref.dtype)


def _inproj(x, sc1, sh1, w_fox, w_rest, segs, tm=256):
    B, S, D = x.shape
    widths = [b - a for a, b in segs]
    dtypes = [BF16, BF16, BF16, F32, BF16, F32, BF16, BF16, BF16, BF16]
    out_shape = tuple(jax.ShapeDtypeStruct((B, S, w), dt) for w, dt in zip(widths, dtypes))
    out_specs = tuple(pl.BlockSpec((None, tm, w), lambda b, i: (b, i, 0)) for w in widths)
    vec = pl.BlockSpec((None, 1, D), lambda b, i: (b, 0, 0))
    return pl.pallas_call(
        functools.partial(_inproj_kernel, segs=tuple(segs), q_scale=HEAD_DIM ** -0.5 * LOG2E),
        out_shape=out_shape,
        grid=(B, S // tm),
        in_specs=[pl.BlockSpec((None, tm, D), lambda b, i: (b, i, 0)), vec, vec,
                  pl.BlockSpec(w_fox.shape, lambda b, i: (0, 0)),
                  pl.BlockSpec(w_rest.shape, lambda b, i: (0, 0))],
        out_specs=out_specs,
        compiler_params=_cparams(("parallel", "parallel")),
    )(x, sc1, sh1, w_fox, w_rest)


def _foxcum_kernel(ff_ref, b_ref, o_ref, *, blk):
    S = ff_ref.shape[0]
    r = lax.broadcasted_iota(jnp.int32, (blk, blk), 0)
    c = lax.broadcasted_iota(jnp.int32, (blk, blk), 1)
    lower = (r >= c).astype(F32)
    carry = jnp.zeros((1, LANES), F32)
    for j in range(S // blk):
        z = ff_ref[j * blk:(j + 1) * blk, :] + b_ref[...]
        lf = jnp.minimum(z, 0.0) - jnp.log(1.0 + jnp.exp(-jnp.abs(z)))
        cum = jnp.dot(lower, lf, precision=HIGHEST, preferred_element_type=F32) + carry
        o_ref[j * blk:(j + 1) * blk, :] = cum * LOG2E
        carry = cum[blk - 1:blk, :]


def _foxcum(ffp, bias_p, blk=256):
    B, S, _ = ffp.shape
    return pl.pallas_call(
        functools.partial(_foxcum_kernel, blk=blk),
        out_shape=jax.ShapeDtypeStruct((B, S, LANES), F32),
        grid=(B,),
        in_specs=[pl.BlockSpec((None, S, LANES), lambda b: (b, 0, 0)),
                  pl.BlockSpec((1, LANES), lambda b: (0, 0))],
        out_specs=pl.BlockSpec((None, S, LANES), lambda b: (b, 0, 0)),
        compiler_params=_cparams(("parallel",)),
    )(ffp, bias_p)


NCUM = 3


def _fox_kernel(q_ref, k_ref, v_ref, c_ref, o_ref, ka_sc, kb_sc, va_sc, vb_sc, *, tq, tk):
    p = pl.program_id(1)
    qi = pl.program_id(2)
    S = k_ref.shape[0]

    @pl.when(qi == 0)
    def _():
        lane = lax.broadcasted_iota(jnp.int32, (S, LANES), 1)
        rr = lax.broadcasted_iota(jnp.int32, (LANES, LANES), 0)
        cc = lax.broadcasted_iota(jnp.int32, (LANES, LANES), 1)
        rest = c_ref[...]
        placed = jnp.zeros((S, LANES), F32)
        for i in range(NCUM):
            piece = rest.astype(BF16)
            rest = rest - piece.astype(F32)
            sel = ((rr == 2 * p) & (cc == HEAD_DIM + i)) | ((rr == 2 * p + 1) & (cc == i))
            placed = placed + jnp.dot(piece, jnp.where(sel, 1.0, 0.0).astype(BF16), preferred_element_type=F32)
        k2 = k_ref[...].astype(F32)
        ka_sc[...] = jnp.where(lane < HEAD_DIM, k2, -placed).astype(BF16)
        kb_sc[...] = jnp.where(lane >= HEAD_DIM, k2, -placed).astype(BF16)
        vt = v_ref[...].astype(F32).T
        row = lax.broadcasted_iota(jnp.int32, (LANES, S), 0)
        va_sc[...] = jnp.where(row < HEAD_DIM, vt, jnp.where(row == HEAD_DIM, 1.0, 0.0)).astype(BF16)
        vb_sc[...] = jnp.where(row >= HEAD_DIM, vt, jnp.where(row == 0, 1.0, 0.0)).astype(BF16)

    q2 = q_ref[...].astype(F32)
    qlane = lax.broadcasted_iota(jnp.int32, (tq, LANES), 1)
    qa = jnp.where(qlane < HEAD_DIM, q2, jnp.where(qlane < HEAD_DIM + NCUM, 1.0, 0.0)).astype(BF16)
    qb = jnp.where(qlane >= HEAD_DIM, q2, jnp.where(qlane < NCUM, 1.0, 0.0)).astype(BF16)
    nsub = tq // tk

    def block(k0, carry, diag_off):
        q0 = 0 if diag_off is None else diag_off
        out = []
        for ksc, vsc, qh, (m, acc) in ((ka_sc, va_sc, qa, carry[:2]), (kb_sc, vb_sc, qb, carry[2:])):
            st = lax.dot_general(ksc[pl.ds(k0, tk), :], qh[q0:, :], (((1,), (1,)), ((), ())),
                                 preferred_element_type=F32)
            if diag_off is not None:
                st = jnp.where(lax.broadcasted_iota(jnp.int32, st.shape, 0)
                               <= lax.broadcasted_iota(jnp.int32, st.shape, 1), st, NEG_BIG)
            m_old = m[:, q0:]
            m_new = jnp.maximum(m_old, jnp.max(st, axis=0, keepdims=True))
            pt = jnp.exp2(st - m_new).astype(BF16)
            acc_new = (jnp.exp2(m_old - m_new) * acc[:, q0:]
                       + jnp.dot(vsc[:, pl.ds(k0, tk)], pt, preferred_element_type=F32))
            if q0:
                m_new = jnp.concatenate([m[:, :q0], m_new], axis=1)
                acc_new = jnp.concatenate([acc[:, :q0], acc_new], axis=1)
            out += [m_new, acc_new]
        return tuple(out)

    def group(j, carry):
        k0 = pl.multiple_of(j * (nsub * tk), nsub * tk)
        for u in range(nsub):
            carry = block(k0 + u * tk, carry, None)
        return carry

    m0 = jnp.full((1, tq), NEG_BIG, F32)
    a0 = jnp.zeros((LANES, tq), F32)
    carry = lax.fori_loop(0, qi, group, (m0, a0, m0, a0))
    for d in range(nsub):
        carry = block(pl.multiple_of(qi * tq + d * tk, tk), carry, d * tk)
    _, aa, _, ab = carry
    row = lax.broadcasted_iota(jnp.int32, (LANES, tq), 0)
    ot = jnp.where(row < HEAD_DIM, aa * (1.0 / aa[HEAD_DIM:HEAD_DIM + 1, :]), ab * (1.0 / ab[0:1, :]))
    o_ref[...] = ot.T.astype(o_ref.dtype)


def _fox(fq, fk, fv, cum, tq=2048, tk=512):
    B, S, W = fq.shape
    tq = min(tq, S)
    assert tq % tk == 0 and S % tq == 0
    npairs = W // LANES
    return pl.pallas_call(
        functools.partial(_fox_kernel, tq=tq, tk=tk),
        out_shape=jax.ShapeDtypeStruct((B, S, W), BF16),
        grid=(B, npairs, S // tq),
        in_specs=[pl.BlockSpec((None, tq, LANES), lambda b, p, i: (b, i, p)),
                  pl.BlockSpec((None, S, LANES), lambda b, p, i: (b, 0, p)),
                  pl.BlockSpec((None, S, LANES), lambda b, p, i: (b, 0, p)),
                  pl.BlockSpec((None, S, LANES), lambda b, p, i: (b, 0, 0))],
        out_specs=pl.BlockSpec((None, tq, LANES), lambda b, p, i: (b, i, p)),
        scratch_shapes=[pltpu.VMEM((S, LANES), BF16), pltpu.VMEM((S, LANES), BF16),
                        pltpu.VMEM((LANES, S), BF16), pltpu.VMEM((LANES, S), BF16)],
        compiler_params=_cparams(("parallel", "parallel", "arbitrary")),
    )(fq, fk, fv, cum)


def _hgrn_kernel(hq_ref, hf_ref, hi_ref, hg_ref, lb_ref, nw_ref, o_ref,
                 b_sc, kk_sc, qq_sc, o_sc, w1_sc, w2_sc, w3_sc, w4_sc, w5_sc,
                 p_sc, st16_sc, dec_sc, st64_sc):
    S = hq_ref.shape[0]
    C = HCHUNK
    nchunks = S // C
    BLK = HBLOCK
    nblk = S // BLK

    lg = lb_ref[...]
    e = jnp.exp(lg - jnp.max(lg, axis=0, keepdims=True))
    lb = e[0:1, :] / jnp.sum(e, axis=0, keepdims=True)

    f = lb + (1.0 - lb) * _sigmoid(hf_ref[...])
    lf = jnp.log(f)
    kk_sc[...] = 1.0 - f
    qq_sc[...] = hq_ref[...].astype(F32)

    row = lax.broadcasted_iota(jnp.int32, (S, LANES), 0)
    rb = 4 * BLK
    tr = lax.broadcasted_iota(jnp.int32, (rb, rb), 0)
    tc = lax.broadcasted_iota(jnp.int32, (rb, rb), 1)
    tri = jnp.where(((tr & -BLK) == (tc & -BLK)) & (tc <= tr), 1.0, 0.0).astype(BF16)
    pieces, rest = [], lf
    for _ in range(NCUM):
        top = pltpu.bitcast(pltpu.bitcast(rest, jnp.uint32) & jnp.uint32(0xFFFF0000), F32)
        pieces.append(top.astype(BF16))
        rest = rest - top
    lf3 = jnp.concatenate(pieces, axis=1)
    for j in range(S // rb):
        c3 = jnp.dot(tri, lf3[j * rb:(j + 1) * rb, :], preferred_element_type=F32)
        b_sc[j * rb:(j + 1) * rb, :] = c3[:, :LANES] + c3[:, LANES:2 * LANES] + c3[:, 2 * LANES:]
    safe = jnp.max(-b_sc[...].reshape(nblk, BLK, LANES)[:, BLK - 1, :]) <= HGRN_SAFE_EXP

    lane = lax.broadcasted_iota(jnp.int32, (C, LANES), 1)
    sr = lax.broadcasted_iota(jnp.int32, (LANES, LANES), 0)
    scn = lax.broadcasted_iota(jnp.int32, (LANES, LANES), 1)
    same_head = (sr // HEAD_DIM) == (scn // HEAD_DIM)

    @pl.when(safe)
    def _factorised():
        qh_sc, kh_sc, ke_sc, qd_sc, k2_sc = w1_sc, w2_sc, w3_sc, w4_sc, w5_sc
        SB = 2 * BLK
        nsb = S // SB
        bb = b_sc[...]
        dblk = jnp.exp(bb.reshape(nblk, BLK, LANES)[:, BLK - 1:BLK, :])
        dfull = jnp.broadcast_to(dblk, (nblk, BLK, LANES)).reshape(S, LANES)
        second = (row & BLK) != 0
        d_prev = pltpu.roll(dfull, BLK, axis=0)
        d_next = pltpu.roll(dfull, S - BLK, axis=0)
        qh = qq_sc[...] * jnp.exp(bb)
        qh_sc[...] = qh.astype(BF16)
        qd_sc[...] = (qh * jnp.where(second, d_prev, 1.0)).astype(BF16)
        kh = kk_sc[...] * jnp.exp(-bb)
        kh_sc[...] = kh.astype(BF16)
        ke = kh * dfull
        ke_sc[...] = ke.astype(BF16)
        k2_sc[...] = (ke * jnp.where(second, 1.0, d_next)).astype(BF16)
        d3 = dfull.reshape(nsb, SB, LANES)
        dec_sc[pl.ds(0, nsb), :] = d3[:, 0, :] * d3[:, BLK, :]
        unroll = min(16, nsb)
        assert nsb % unroll == 0
        tn = (((0,), (0,)), ((), ()))
        nt = (((1,), (1,)), ((), ()))

        def scan(g, st):
            for u in range(unroll):
                i = g * unroll + u
                r0 = pl.multiple_of(i * SB, SB)
                st64_sc[i] = st.astype(BF16)
                upd = lax.dot_general(hi_ref[pl.ds(r0, SB), :], k2_sc[pl.ds(r0, SB), :], tn,
                                      preferred_element_type=F32)
                st = st * dec_sc[pl.ds(i, 1), :] + jnp.where(same_head, upd, 0.0)
            return st

        lax.fori_loop(0, nsb // unroll, scan, jnp.zeros((LANES, LANES), F32))

        r = lax.broadcasted_iota(jnp.int32, (2 * SB, 2 * SB), 0)
        c = lax.broadcasted_iota(jnp.int32, (2 * SB, 2 * SB), 1)
        t = r & (SB - 1)
        visible = (((c < SB) & ((t & BLK) == (c & BLK)) & ((t & (BLK - 1)) >= (c & (BLK - 1))))
                   | ((c >= SB) & (c < SB + BLK) & (t >= BLK)))
        plane = lax.broadcasted_iota(jnp.int32, (SB, LANES), 1)
        pad = jnp.zeros((BLK, LANES), BF16)

        def readout(g, _):
            for u in range(unroll):
                i = g * unroll + u
                r0 = pl.multiple_of(i * SB, SB)
                vb = hi_ref[pl.ds(r0, SB), :]
                qh2 = qh_sc[pl.ds(r0, SB), :]
                q2 = jnp.concatenate([jnp.where(plane < HEAD_DIM, qh2, jnp.zeros_like(qh2)),
                                      jnp.where(plane >= HEAD_DIM, qh2, jnp.zeros_like(qh2))], axis=0)
                kext = jnp.concatenate([kh_sc[pl.ds(r0, SB), :], ke_sc[pl.ds(r0, BLK), :], pad], axis=0)
                vext = jnp.concatenate([vb, vb[:BLK], pad], axis=0)
                sc = lax.dot_general(q2, kext, nt, preferred_element_type=F32)
                sc = jnp.where(visible, sc, 0.0).astype(BF16)
                out = jnp.dot(sc, vext, preferred_element_type=F32)
                o_inter = lax.dot_general(qd_sc[pl.ds(r0, SB), :], st64_sc[i], nt, preferred_element_type=F32)
                o_sc[pl.ds(r0, SB), :] = jnp.where(plane < HEAD_DIM, out[:SB], out[SB:]) + o_inter
            return 0

        lax.fori_loop(0, nsb // unroll, readout, 0)

    @pl.when(jnp.logical_not(safe))
    def _direct():
        qt_sc, kt_sc, s_sc, a2_sc = w1_sc, w2_sc, w3_sc, b_sc
        bb = b_sc[...]
        cl = jnp.broadcast_to(bb.reshape(nchunks, C, LANES)[:, C - 1:C, :], (nchunks, C, LANES)).reshape(S, LANES)
        aa = bb - jnp.where((row & (BLK - 1)) >= C, pltpu.roll(cl, C, axis=0), 0.0)
        al = jnp.broadcast_to(aa.reshape(nchunks, C, LANES)[:, C - 1:C, :], (nchunks, C, LANES)).reshape(S, LANES)
        qt_sc[...] = (qq_sc[...] * jnp.exp(aa)).astype(BF16)
        kt_sc[...] = (kk_sc[...] * jnp.exp(al - aa)).astype(BF16)
        dec_sc[...] = jnp.exp(aa.reshape(nchunks, C, LANES)[:, C - 1, :])
        a2_sc[...] = aa * LOG2E
        trow = lax.broadcasted_iota(jnp.int32, (C, LANES), 0)

        def gen(c, _):
            r0 = pl.multiple_of(c * C, C)
            ac = a2_sc[pl.ds(r0, C), :]
            qc = qq_sc[pl.ds(r0, C), :]
            kc = kk_sc[pl.ds(r0, C), :]
            half = C // 2
            for s in range(C):
                if s < half:
                    dec = jnp.exp2(jnp.where(trow >= s, ac - ac[s:s + 1, :], NEG_BIG))
                    p = qc * (kc[s:s + 1, :] * dec)
                else:
                    dec = jnp.exp2(jnp.where(trow[half:] >= s, ac[half:] - ac[s:s + 1, :], NEG_BIG))
                    p = jnp.concatenate([jnp.zeros((half, LANES), F32), qc[half:] * (kc[s:s + 1, :] * dec)],
                                        axis=0)
                p_sc[pl.ds(r0, C), s * LANES:(s + 1) * LANES] = p.astype(BF16)
            return 0

        lax.fori_loop(0, nchunks, gen, 0)

        er = lax.broadcasted_iota(jnp.int32, (C * LANES, LANES), 0)
        ec = lax.broadcasted_iota(jnp.int32, (C * LANES, LANES), 1)
        emat = (ec == ((er & (LANES - 1)) // HEAD_DIM) * C + er // LANES).astype(BF16)
        rb = 256

        def red(i, _):
            r0 = pl.multiple_of(i * rb, rb)
            s_sc[pl.ds(r0, rb), :] = jnp.dot(p_sc[pl.ds(r0, rb), :], emat,
                                             preferred_element_type=F32).astype(BF16)
            return 0

        lax.fori_loop(0, S // rb, red, 0)

        unroll = 16
        assert nchunks % unroll == 0

        def scan(g, st):
            for u in range(unroll):
                c = g * unroll + u
                r0 = pl.multiple_of(c * C, C)
                st16_sc[c] = st.astype(BF16)
                upd = lax.dot_general(hi_ref[pl.ds(r0, C), :], kt_sc[pl.ds(r0, C), :],
                                      (((0,), (0,)), ((), ())), preferred_element_type=F32)
                st = st * dec_sc[pl.ds(c, 1), :] + jnp.where(same_head, upd, 0.0)
            return st

        lax.fori_loop(0, nchunks // unroll, scan, jnp.zeros((LANES, LANES), F32))

        def readout(g, _):
            for u in range(unroll):
                c = g * unroll + u
                r0 = pl.multiple_of(c * C, C)
                vc = hi_ref[pl.ds(r0, C), :]
                o_inter = lax.dot_general(qt_sc[pl.ds(r0, C), :], st16_sc[c],
                                          (((1,), (1,)), ((), ())), preferred_element_type=F32)
                v2 = jnp.concatenate([jnp.where(lane < HEAD_DIM, vc, jnp.zeros_like(vc)),
                                      jnp.where(lane >= HEAD_DIM, vc, jnp.zeros_like(vc))], axis=0)
                o_intra = jnp.dot(s_sc[pl.ds(r0, C), :][:, :2 * C], v2, preferred_element_type=F32)
                o_sc[pl.ds(r0, C), :] = o_inter + o_intra
            return 0

        lax.fori_loop(0, nchunks // unroll, readout, 0)

    o = o_sc[...]
    ones_head = jnp.where(same_head, 1.0, 0.0).astype(BF16)
    sq = o * o
    sq_top = pltpu.bitcast(pltpu.bitcast(sq, jnp.uint32) & jnp.uint32(0xFFFF0000), F32)
    ms = (jnp.dot(sq_top.astype(BF16), ones_head, preferred_element_type=F32)
          + jnp.dot((sq - sq_top).astype(BF16), ones_head, preferred_element_type=F32)) * (1.0 / HEAD_DIM)
    y = o * lax.rsqrt(ms + RMS_EPS) * nw_ref[...]
    o_ref[...] = (y * hg_ref[...].astype(F32)).astype(o_ref.dtype)


def _hgrn(hq, hf, hi, hg, lb_logits, norm_w):
    B, S, W = hq.shape
    npairs = W // LANES
    nrows = lb_logits.shape[0]
    seq = pl.BlockSpec((None, S, LANES), lambda b, p: (b, 0, p))
    return pl.pallas_call(
        _hgrn_kernel,
        out_shape=jax.ShapeDtypeStruct((B, S, W), BF16),
        grid=(B, npairs),
        in_specs=[seq, seq, seq, seq,
                  pl.BlockSpec((nrows, LANES), lambda b, p: (0, p)),
                  pl.BlockSpec((1, LANES), lambda b, p: (0, p))],
        out_specs=seq,
        scratch_shapes=[pltpu.VMEM((S, LANES), F32),
                        pltpu.VMEM((S, LANES), F32),
                        pltpu.VMEM((S, LANES), F32),
                        pltpu.VMEM((S, LANES), F32),
                        pltpu.VMEM((S, LANES), BF16),
                        pltpu.VMEM((S, LANES), BF16),
                        pltpu.VMEM((S, LANES), BF16),
                        pltpu.VMEM((S, LANES), BF16),
                        pltpu.VMEM((S, LANES), BF16),
                        pltpu.VMEM((S, HCHUNK * LANES), BF16),
                        pltpu.VMEM((S // HCHUNK, LANES, LANES), BF16),
                        pltpu.VMEM((S // HCHUNK, LANES), F32),
                        pltpu.VMEM((S // HBLOCK, LANES, LANES), BF16)],
        compiler_params=_cparams(("parallel", "parallel")),
    )(hq, hf, hi, hg, lb_logits, norm_w.reshape(1, W))


def _layer_norm(v, g, b):
    mu = jnp.mean(v, axis=-1, keepdims=True)
    d = v - mu
    var = jnp.mean(d * d, axis=-1, keepdims=True)
    return d * lax.rsqrt(var + LN_EPS) * g + b


def _bf16_bits(x):
    return (pltpu.bitcast(x, jnp.uint32) + jnp.uint32(0x8000)) & jnp.uint32(0xFFFF0000)


def _store_chunks(ref, val):
    n = ref.shape[0]
    for j in range(n):
        lo = _bf16_bits(val[:, j * LANES:(j + 1) * LANES]) >> 16
        hi = _bf16_bits(val[:, (j + n) * LANES:(j + n + 1) * LANES])
        ref[j] = pltpu.bitcast(lo | hi, F32)


def _load_chunks(ref):
    words = [pltpu.bitcast(ref[j], jnp.uint32) for j in range(ref.shape[0])]
    lo = [pltpu.bitcast(w << 16, F32) for w in words]
    hi = [pltpu.bitcast(w & jnp.uint32(0xFFFF0000), F32) for w in words]
    return jnp.concatenate(lo + hi, axis=1)


def _mix_kernel(yf_ref, oh_ref, gf_ref, gh_ref, x_ref, g1_ref, sc2_ref, sh2_ref,
                wuf_ref, wuh_ref, wo_ref, lg_ref, lbias_ref, wr_ref, br_ref,
                x1_ref, h2_ref, ri_ref, rt_ref, cnt_ref, carry_sc, *, alpha, ngroups, nper):
    first = (pl.program_id(0) == 0) & (pl.program_id(1) == 0)

    @pl.when(first)
    def _():
        carry_sc[...] = jnp.zeros_like(carry_sc)

    tm = x_ref.shape[0]
    yf = jnp.dot(yf_ref[...], wuf_ref[...], preferred_element_type=F32)
    yh = jnp.dot(oh_ref[...], wuh_ref[...], preferred_element_type=F32)
    merged = gf_ref[...].astype(F32) * yf + gh_ref[...].astype(F32) * yh
    y = jnp.dot(merged.astype(BF16), wo_ref[...], preferred_element_type=F32)
    x1 = _layer_norm(alpha * x_ref[...] + g1_ref[...] * y, lg_ref[...], lbias_ref[...])
    x1_ref[...] = x1
    h2 = x1 * (1.0 + sc2_ref[...]) + sh2_ref[...]
    _store_chunks(h2_ref, h2)

    h_top = pltpu.bitcast(pltpu.bitcast(h2, jnp.uint32) & jnp.uint32(0xFFFF0000), F32)
    h_hi = h_top.astype(BF16)
    h_lo = (h2 - h_top).astype(BF16)
    hh = jnp.dot(h_hi, wr_ref[...], preferred_element_type=F32)
    logits = (hh[:, :LANES] + hh[:, LANES:]
              + jnp.dot(h_lo, wr_ref[:, :LANES], preferred_element_type=F32)) + br_ref[...]
    lane = lax.broadcasted_iota(jnp.int32, (tm, LANES), 1)
    big = jnp.int32(1 << 20)

    def argmax_first(vals, mask):
        mx = jnp.max(jnp.where(mask, vals, -jnp.inf), axis=1, keepdims=True)
        idx = jnp.min(jnp.where(mask & (vals == mx), lane, big), axis=1, keepdims=True)
        return mx, idx

    gmask = lane < ngroups
    gmax = jnp.max(jnp.where(gmask, logits, -jnp.inf), axis=1, keepdims=True)
    gexp = jnp.where(gmask, jnp.exp(logits - gmax), 0.0)
    gprob = gexp / jnp.sum(gexp, axis=1, keepdims=True)
    g_w, g_idx = argmax_first(gprob, gmask)

    lo = ngroups + g_idx * nper
    emask = (lane >= lo) & (lane < lo + nper)
    emax = jnp.max(jnp.where(emask, logits, -jnp.inf), axis=1, keepdims=True)
    eexp = jnp.where(emask, jnp.exp(logits - emax), 0.0)
    eprob = eexp / jnp.sum(eexp, axis=1, keepdims=True)
    p0, i0 = argmax_first(eprob, emask)
    p1, i1 = argmax_first(eprob, emask & (lane != i0))
    den = p0 + p1
    w0 = p0 / den * g_w
    w1 = p1 / den * g_w
    e0 = i0 - ngroups
    e1 = i1 - ngroups

    oh = ((lane == e0) | (lane == e1)).astype(F32)
    r = lax.broadcasted_iota(jnp.int32, (tm, tm), 0)
    c = lax.broadcasted_iota(jnp.int32, (tm, tm), 1)
    strict_lower = (c < r).astype(BF16)
    before = jnp.dot(strict_lower, oh.astype(BF16), preferred_element_type=F32) + carry_sc[...]
    rank0 = jnp.sum(jnp.where(lane == e0, before, 0.0), axis=1, keepdims=True)
    rank1 = jnp.sum(jnp.where(lane == e1, before, 0.0), axis=1, keepdims=True)
    carry_sc[...] = carry_sc[...] + jnp.sum(oh, axis=0, keepdims=True)
    cnt_ref[...] = carry_sc[...]

    info = jnp.where(lane == 0, w0, 0.0)
    info = jnp.where(lane == 1, w1, info)
    info = jnp.where(lane == 2, e0.astype(F32), info)
    info = jnp.where(lane == 3, e1.astype(F32), info)
    info = jnp.where(lane == 4, rank0, info)
    info = jnp.where(lane == 5, rank1, info)
    ri_ref[...] = info
    rt_ref[...] = info.T[:ROW_TILE, :]


def _mix(yf, oh, gf, gh, x, g1, sc2, sh2, wuf, wuh, wo, ln_g, ln_b, wr, br, alpha, ngroups, nper, tm=512):
    B, S, D = x.shape
    W = yf.shape[2]
    tok = lambda w: pl.BlockSpec((None, tm, w), lambda b, i: (b, i, 0))
    vec = pl.BlockSpec((None, 1, D), lambda b, i: (b, 0, 0))
    full = lambda a: pl.BlockSpec(a.shape, lambda b, i: (0,) * a.ndim)
    return pl.pallas_call(
        functools.partial(_mix_kernel, alpha=alpha, ngroups=ngroups, nper=nper),
        out_shape=(jax.ShapeDtypeStruct((B, S, D), F32),
                   jax.ShapeDtypeStruct((D // WORD_LANES, B * S, LANES), F32),
                   jax.ShapeDtypeStruct((B, S, LANES), F32),
                   jax.ShapeDtypeStruct((ROW_TILE, B * S), F32),
                   jax.ShapeDtypeStruct((1, LANES), F32)),
        grid=(B, S // tm),
        in_specs=[tok(W), tok(W), tok(D), tok(D), tok(D), vec, vec, vec,
                  full(wuf), full(wuh), full(wo), full(ln_g), full(ln_b), full(wr), full(br)],
        out_specs=(tok(D),
                   pl.BlockSpec((D // WORD_LANES, tm, LANES), lambda b, i: (0, b * (S // tm) + i, 0)),
                   tok(LANES),
                   pl.BlockSpec((ROW_TILE, tm), lambda b, i: (0, b * (S // tm) + i)),
                   pl.BlockSpec((1, LANES), lambda b, i: (0, 0))),
        scratch_shapes=[pltpu.VMEM((1, LANES), F32)],
        compiler_params=_cparams(("arbitrary", "arbitrary")),
    )(yf, oh, gf, gh, x, g1, sc2, sh2, wuf, wuh, wo, ln_g, ln_b, wr, br)


def _sc_mesh():
    return plsc.VectorSubcoreMesh(core_axis_name="core", subcore_axis_name="subcore")


def _sc_pipeline(body, grid, in_specs, out_specs):
    return pltpu.emit_pipeline(body, grid=grid, in_specs=in_specs, out_specs=out_specs,
                               core_axis_name=("core", "subcore"),
                               dimension_semantics=(pltpu.PARALLEL,) * len(grid))


def _sc_scatter_rows(src, rows_a, rows_b, n_out):
    nj, t = rows_a.shape
    nc = t // LANES

    @pl.kernel(out_type=jax.ShapeDtypeStruct((n_out, LANES), src.dtype), mesh=_sc_mesh(), scratch_types=[])
    def scatter(x_hbm, a_hbm, b_hbm, o_hbm):
        def body(x_vmem, a_vmem, b_vmem):
            pltpu.sync_copy(x_vmem, o_hbm.at[a_vmem.at[0]])
            pltpu.sync_copy(x_vmem, o_hbm.at[b_vmem.at[0]])

        idx = pl.BlockSpec((1, LANES), lambda j, c: (j, c))
        _sc_pipeline(body, (nj, nc), [pl.BlockSpec((LANES, LANES), lambda j, c: (j * nc + c, 0)), idx, idx],
                     [])(x_hbm, a_hbm, b_hbm)

    return scatter(src, rows_a, rows_b)


def _sc_gather_rows(table, rows):
    nr, t = rows.shape
    nc = t // LANES

    @pl.kernel(out_type=jax.ShapeDtypeStruct((nr * t, LANES), table.dtype), mesh=_sc_mesh(), scratch_types=[])
    def gather(x_hbm, i_hbm, o_hbm):
        def body(i_vmem, o_vmem):
            pltpu.sync_copy(x_hbm.at[i_vmem.at[0]], o_vmem)

        _sc_pipeline(body, (nr, nc), [pl.BlockSpec((1, LANES), lambda r, c: (r, c))],
                     [pl.BlockSpec((LANES, LANES), lambda r, c: (r * nc + c, 0))])(i_hbm, o_hbm)

    return gather(table, rows)


def _experts_kernel(te_ref, tn_ref, tb_ref, x_ref, wg_ref, wu_ref, wd_ref, o_ref):
    del tb_ref
    nrows = tn_ref[pl.program_id(0)]

    @pl.when(nrows > 0)
    def _():
        x = _load_chunks(x_ref)
        x = jnp.where(lax.broadcasted_iota(jnp.int32, x.shape, 0) < nrows, x, 0.0).astype(BF16)
        g = jnp.dot(x, wg_ref[...].astype(BF16), preferred_element_type=F32)
        u = jnp.dot(x, wu_ref[...].astype(BF16), preferred_element_type=F32)
        hid = (_silu(g) * u).astype(BF16)
        _store_chunks(o_ref, jnp.dot(hid, wd_ref[...].astype(BF16), preferred_element_type=F32))


def _experts(tile_expert, tile_rows, tile_block, xs, wg, wu, wd, tm):
    E, D, FF = wg.shape
    dt = D // WORD_LANES
    ntiles = tile_expert.shape[0]
    rows = pl.BlockSpec((dt, tm, LANES), lambda i, te, tn, tb: (0, tb[i], 0))
    grid_spec = pltpu.PrefetchScalarGridSpec(
        num_scalar_prefetch=3,
        grid=(ntiles,),
        in_specs=[rows,
                  pl.BlockSpec((None, D, FF), lambda i, te, tn, tb: (te[i], 0, 0)),
                  pl.BlockSpec((None, D, FF), lambda i, te, tn, tb: (te[i], 0, 0)),
                  pl.BlockSpec((None, FF, D), lambda i, te, tn, tb: (te[i], 0, 0))],
        out_specs=rows,
    )
    return pl.pallas_call(
        _experts_kernel,
        out_shape=jax.ShapeDtypeStruct((dt, ntiles * tm, LANES), F32),
        grid_spec=grid_spec,
        compiler_params=_cparams(("arbitrary",)),
    )(tile_expert, tile_rows, tile_block, xs, wg, wu, wd)


def _combine_kernel(yg_ref, x1_ref, ri_ref, g2_ref, lg_ref, lb_ref, o_ref, *, alpha):
    ri = ri_ref[...]
    y = ri[:, 0:1] * _load_chunks(yg_ref.at[0]) + ri[:, 1:2] * _load_chunks(yg_ref.at[1])
    o_ref[...] = _layer_norm(alpha * x1_ref[...] + g2_ref[...] * y, lg_ref[...], lb_ref[...])


def _combine(yg, x1, rinfo, g2, ln_g, ln_b, alpha, tm=512):
    B, S, D = x1.shape
    nb = S // tm
    return pl.pallas_call(
        functools.partial(_combine_kernel, alpha=alpha),
        out_shape=jax.ShapeDtypeStruct((B, S, D), F32),
        grid=(B, nb),
        in_specs=[pl.BlockSpec((2, D // WORD_LANES, tm, LANES), lambda b, i: (0, 0, b * nb + i, 0)),
                  pl.BlockSpec((None, tm, D), lambda b, i: (b, i, 0)),
                  pl.BlockSpec((None, tm, LANES), lambda b, i: (b, i, 0)),
                  pl.BlockSpec((None, 1, D), lambda b, i: (b, 0, 0)),
                  pl.BlockSpec((1, D), lambda b, i: (0, 0)),
                  pl.BlockSpec((1, D), lambda b, i: (0, 0))],
        out_specs=pl.BlockSpec((None, tm, D), lambda b, i: (b, i, 0)),
        compiler_params=_cparams(("parallel", "parallel")),
    )(yg, x1, rinfo, g2, ln_g, ln_b)


def kernel(x, c, w_ada, b_ada, w_in, b_fox_forget, hgrn_lb_logits, hgrn_norm_w, w_up_fox, w_up_hgrn, w_out,
           ln1_g, ln1_b, w_router_group, b_router_group, w_router_expert, b_router_expert,
           w_expert_gate, w_expert_up, w_expert_down, ln2_g, ln2_b):
    B, S, D = x.shape
    depth = w_ada.shape[0]
    assert depth == 1, "single-layer block"
    fox_heads = b_fox_forget.shape[1]
    fox_w = fox_heads * HEAD_DIM
    hgrn_w = hgrn_norm_w.shape[1]
    ngroups = w_router_group.shape[2]
    nexp = w_router_expert.shape[2]
    nper = nexp // ngroups
    alpha = (2 * depth) ** 0.25
    T = B * S

    ada = _ada(c, w_ada[0], b_ada[0])
    sh1, sc1, g1, sh2, sc2, g2 = [a.reshape(B, 1, D) for a in jnp.split(ada, 6, axis=-1)]

    wi = w_in[0]
    o_ff = 3 * fox_w
    w_fox = jnp.pad(wi[:, :o_ff + fox_heads], ((0, 0), (0, LANES - fox_heads))).astype(BF16)
    w_rest = wi[:, o_ff + fox_heads:].astype(BF16)
    widths = [fox_w, fox_w, fox_w, LANES, hgrn_w, hgrn_w, hgrn_w, hgrn_w, D, D]
    segs, off = [], 0
    for n, w in enumerate(widths):
        if n == 4:
            off = 0
        segs.append((off, off + w))
        off += w
    fq, fk, fv, ffp, hq, hf, hi, hg, gf, gh = _inproj(x, sc1, sh1, w_fox, w_rest, segs)

    bias_p = jnp.zeros((1, LANES), F32).at[0, :fox_heads].set(b_fox_forget[0])
    cum = _foxcum(ffp, bias_p)
    y_fox = _fox(fq, fk, fv, cum)

    o_h = _hgrn(hq, hf, hi, hg, hgrn_lb_logits, hgrn_norm_w[0])

    wr = jnp.zeros((D, LANES), F32).at[:, :ngroups].set(w_router_group[0]).at[:, ngroups:ngroups + nexp].set(
        w_router_expert[0])
    wr_hi = lax.bitcast_convert_type(lax.bitcast_convert_type(wr, jnp.uint32) & jnp.uint32(0xFFFF0000), F32)
    wr = jnp.concatenate([wr_hi.astype(BF16), (wr - wr_hi).astype(BF16)], axis=1)
    br = jnp.zeros((1, LANES), F32).at[0, :ngroups].set(b_router_group[0]).at[0, ngroups:ngroups + nexp].set(
        b_router_expert[0])
    x1, h2, rinfo, fields, counts = _mix(
        y_fox, o_h, gf, gh, x, g1, sc2, sh2,
        w_up_fox[0].astype(BF16), w_up_hgrn[0].astype(BF16), w_out[0].astype(BF16),
        ln1_g[0].reshape(1, D), ln1_b[0].reshape(1, D), wr, br, alpha, ngroups, nper)

    tm_e = 512
    dt = D // WORD_LANES
    ntiles = (2 * T) // tm_e + nexp
    nslots = ntiles * tm_e
    cnt = counts[0, :nexp].astype(jnp.int32)
    padded = ((cnt + tm_e - 1) // tm_e) * tm_e
    ends = jnp.cumsum(padded)
    starts = ends - padded
    eid = fields[2:4].astype(jnp.int32)
    rank = fields[4:6].astype(jnp.int32)
    first = jnp.sum(jnp.where(eid[None] == jnp.arange(nexp, dtype=jnp.int32)[:, None, None],
                              starts[:, None, None], 0), axis=0)
    pos = first + rank
    tile_start = jnp.arange(ntiles, dtype=jnp.int32) * tm_e
    tile_block = jnp.minimum(jnp.arange(ntiles, dtype=jnp.int32), ends[-1] // tm_e - 1)
    tile_expert = jnp.minimum(jnp.sum((tile_start[:, None] >= ends[None, :]).astype(jnp.int32), axis=1), nexp - 1)
    tile_rows = jnp.clip(starts[tile_expert] + cnt[tile_expert] - tile_start, 0, tm_e)
    tile_expert = tile_expert[tile_block]
    rows = pos[:, None, :] + (jnp.arange(dt, dtype=jnp.int32) * nslots)[None, :, None]

    xs = _sc_scatter_rows(h2.reshape(dt * T, LANES), rows[0], rows[1], dt * nslots)
    ys = _experts(tile_expert, tile_rows, tile_block, xs.reshape(dt, nslots, LANES),
                  w_expert_gate[0], w_expert_up[0], w_expert_down[0], tm_e)
    yg = _sc_gather_rows(ys.reshape(dt * nslots, LANES), rows.reshape(2 * dt, T))
    return _combine(yg.reshape(2, dt, T, LANES), x1, rinfo, g2,
                    ln2_g[0].reshape(1, D), ln2_b[0].reshape(1, D), alpha)
```

```python
import functools

import jax
import jax.numpy as jnp
from jax import lax
from jax.experimental import pallas as pl
from jax.experimental.pallas import tpu as pltpu
from jax.experimental.pallas import tpu_sc as plsc

F32 = jnp.float32
BF16 = jnp.bfloat16

LANES = 128
HEAD_DIM = 64
LN_EPS = 1e-5
RMS_EPS = 1e-6
LOG2E = 1.4426950408889634
NEG_BIG = -1e30
HCHUNK = 16
HBLOCK = 64
HGRN_SAFE_EXP = 60.0
ROW_TILE = 8
WORD_LANES = 2 * LANES
VMEM_LIMIT = 56 * 1024 * 1024


def _cparams(sem, vmem=VMEM_LIMIT):
    return pltpu.CompilerParams(dimension_semantics=sem, vmem_limit_bytes=vmem)


def _sigmoid(x):
    return 0.5 * jnp.tanh(0.5 * x) + 0.5


def _silu(x):
    return x * _sigmoid(x)


def _bf16_pieces(x, n):
    pieces = []
    for _ in range(n):
        top = pltpu.bitcast(pltpu.bitcast(x, jnp.uint32) & jnp.uint32(0xFFFF0000), F32)
        pieces.append(top.astype(BF16))
        x = x - top
    return pieces


def _exact_matrix_dot(m, x):
    r = jnp.dot(m, jnp.concatenate(_bf16_pieces(x, 3), axis=1), preferred_element_type=F32)
    return r[:, :LANES] + r[:, LANES:2 * LANES] + r[:, 2 * LANES:]


def _ada_kernel(c_ref, w_ref, b_ref, o_ref):
    c_hi, c_lo = _bf16_pieces(_silu(c_ref[...]), 2)
    w_hi, w_lo = _bf16_pieces(w_ref[...], 2)
    o_ref[...] = (jnp.dot(c_hi, w_hi, preferred_element_type=F32) + jnp.dot(c_hi, w_lo, preferred_element_type=F32)
                  + jnp.dot(c_lo, w_hi, preferred_element_type=F32)) + b_ref[...]


def _ada(c, w_ada, b_ada):
    B, D = c.shape
    N = w_ada.shape[1]
    tn = 1024
    return pl.pallas_call(
        _ada_kernel,
        out_shape=jax.ShapeDtypeStruct((B, N), F32),
        grid=(N // tn,),
        in_specs=[pl.BlockSpec((B, D), lambda j: (0, 0)),
                  pl.BlockSpec((D, tn), lambda j: (0, j)),
                  pl.BlockSpec((1, tn), lambda j: (0, j))],
        out_specs=pl.BlockSpec((B, tn), lambda j: (0, j)),
        compiler_params=_cparams(("arbitrary",)),
    )(c, w_ada, b_ada.reshape(1, N))


N_FOX_SEGS = 4
SILU_SEGS = (4, 7)
SIGMOID_SEGS = (8, 9)


def _inproj_kernel(x_ref, sc_ref, sh_ref, wf_ref, wr_ref,
                   fq_ref, fk_ref, fv_ref, ff_ref, hq_ref, hf_ref, hi_ref, hg_ref, gf_ref, gh_ref,
                   *, segs, q_scale):
    h = (x_ref[...] * (1.0 + sc_ref[...]) + sh_ref[...]).astype(BF16)
    outs = (fq_ref, fk_ref, fv_ref, ff_ref, hq_ref, hf_ref, hi_ref, hg_ref, gf_ref, gh_ref)
    for idx, (o_ref, (a, b)) in enumerate(zip(outs, segs)):
        w_ref = wf_ref if idx < N_FOX_SEGS else wr_ref
        r = jnp.dot(h, w_ref[:, a:b], preferred_element_type=F32)
        if idx == 0:
            r = r * q_scale
        elif idx in SILU_SEGS:
            r = _silu(r)
        elif idx in SIGMOID_SEGS:
            r = _sigmoid(r)
        o_ref[...] = r.astype(o_ref.dtype)


def _inproj(x, sc1, sh1, w_fox, w_rest, segs, tm=256):
    B, S, D = x.shape
    widths = [b - a for a, b in segs]
    dtypes = [BF16, BF16, BF16, F32, BF16, F32, BF16, BF16, BF16, BF16]
    out_shape = tuple(jax.ShapeDtypeStruct((B, S, w), dt) for w, dt in zip(widths, dtypes))
    out_specs = tuple(pl.BlockSpec((None, tm, w), lambda b, i: (b, i, 0)) for w in widths)
    vec = pl.BlockSpec((None, 1, D), lambda b, i: (b, 0, 0))
    return pl.pallas_call(
        functools.partial(_inproj_kernel, segs=tuple(segs), q_scale=HEAD_DIM ** -0.5 * LOG2E),
        out_shape=out_shape,
        grid=(B, S // tm),
        in_specs=[pl.BlockSpec((None, tm, D), lambda b, i: (b, i, 0)), vec, vec,
                  pl.BlockSpec(w_fox.shape, lambda b, i: (0, 0)),
                  pl.BlockSpec(w_rest.shape, lambda b, i: (0, 0))],
        out_specs=out_specs,
        compiler_params=_cparams(("parallel", "parallel")),
    )(x, sc1, sh1, w_fox, w_rest)


def _foxcum_kernel(ff_ref, b_ref, o_ref, *, blk):
    S = ff_ref.shape[0]
    r = lax.broadcasted_iota(jnp.int32, (blk, blk), 0)
    c = lax.broadcasted_iota(jnp.int32, (blk, blk), 1)
    lower = jnp.where(r >= c, 1.0, 0.0).astype(BF16)
    carry = jnp.zeros((1, LANES), F32)
    for j in range(S // blk):
        z = ff_ref[j * blk:(j + 1) * blk, :] + b_ref[...]
        lf = jnp.minimum(z, 0.0) - jnp.log(1.0 + jnp.exp(-jnp.abs(z)))
        cum = _exact_matrix_dot(lower, lf) + carry
        o_ref[j * blk:(j + 1) * blk, :] = cum * LOG2E
        carry = cum[blk - 1:blk, :]


def _foxcum(ffp, bias_p, blk=256):
    B, S, _ = ffp.shape
    return pl.pallas_call(
        functools.partial(_foxcum_kernel, blk=blk),
        out_shape=jax.ShapeDtypeStruct((B, S, LANES), F32),
        grid=(B,),
        in_specs=[pl.BlockSpec((None, S, LANES), lambda b: (b, 0, 0)),
                  pl.BlockSpec((1, LANES), lambda b: (0, 0))],
        out_specs=pl.BlockSpec((None, S, LANES), lambda b: (b, 0, 0)),
        compiler_params=_cparams(("parallel",)),
    )(ffp, bias_p)


NCUM = 3


def _fox_kernel(q_ref, k_ref, v_ref, c_ref, o_ref, ka_sc, kb_sc, va_sc, vb_sc, *, tq, tk):
    p = pl.program_id(1)
    qi = pl.program_id(2)
    S = k_ref.shape[0]

    @pl.when(qi == 0)
    def _():
        lane = lax.broadcasted_iota(jnp.int32, (S, LANES), 1)
        rr = lax.broadcasted_iota(jnp.int32, (LANES, LANES), 0)
        cc = lax.broadcasted_iota(jnp.int32, (LANES, LANES), 1)
        rest = c_ref[...]
        placed = jnp.zeros((S, LANES), F32)
        for i in range(NCUM):
            piece = rest.astype(BF16)
            rest = rest - piece.astype(F32)
            sel = ((rr == 2 * p) & (cc == HEAD_DIM + i)) | ((rr == 2 * p + 1) & (cc == i))
            placed = placed + jnp.dot(piece, jnp.where(sel, 1.0, 0.0).astype(BF16), preferred_element_type=F32)
        k2 = k_ref[...].astype(F32)
        ka_sc[...] = jnp.where(lane < HEAD_DIM, k2, -placed).astype(BF16)
        kb_sc[...] = jnp.where(lane >= HEAD_DIM, k2, -placed).astype(BF16)
        vt = v_ref[...].astype(F32).T
        row = lax.broadcasted_iota(jnp.int32, (LANES, S), 0)
        va_sc[...] = jnp.where(row < HEAD_DIM, vt, jnp.where(row == HEAD_DIM, 1.0, 0.0)).astype(BF16)
        vb_sc[...] = jnp.where(row >= HEAD_DIM, vt, jnp.where(row == 0, 1.0, 0.0)).astype(BF16)

    q2 = q_ref[...].astype(F32)
    qlane = lax.broadcasted_iota(jnp.int32, (tq, LANES), 1)
    qa = jnp.where(qlane < HEAD_DIM, q2, jnp.where(qlane < HEAD_DIM + NCUM, 1.0, 0.0)).astype(BF16)
    qb = jnp.where(qlane >= HEAD_DIM, q2, jnp.where(qlane < NCUM, 1.0, 0.0)).astype(BF16)
    nsub = tq // tk

    def block(k0, carry, diag_off):
        q0 = 0 if diag_off is None else diag_off
        out = []
        for ksc, vsc, qh, (m, acc) in ((ka_sc, va_sc, qa, carry[:2]), (kb_sc, vb_sc, qb, carry[2:])):
            st = lax.dot_general(ksc[pl.ds(k0, tk), :], qh[q0:, :], (((1,), (1,)), ((), ())),
                                 preferred_element_type=F32)
            if diag_off is not None:
                st = jnp.where(lax.broadcasted_iota(jnp.int32, st.shape, 0)
                               <= lax.broadcasted_iota(jnp.int32, st.shape, 1), st, NEG_BIG)
            m_old = m[:, q0:]
            m_new = jnp.maximum(m_old, jnp.max(st, axis=0, keepdims=True))
            pt = jnp.exp2(st - m_new).astype(BF16)
            acc_new = (jnp.exp2(m_old - m_new) * acc[:, q0:]
                       + jnp.dot(vsc[:, pl.ds(k0, tk)], pt, preferred_element_type=F32))
            if q0:
                m_new = jnp.concatenate([m[:, :q0], m_new], axis=1)
                acc_new = jnp.concatenate([acc[:, :q0], acc_new], axis=1)
            out += [m_new, acc_new]
        return tuple(out)

    def group(j, carry):
        k0 = pl.multiple_of(j * (nsub * tk), nsub * tk)
        for u in range(nsub):
            carry = block(k0 + u * tk, carry, None)
        return carry

    m0 = jnp.full((1, tq), NEG_BIG, F32)
    a0 = jnp.zeros((LANES, tq), F32)
    carry = lax.fori_loop(0, qi, group, (m0, a0, m0, a0))
    for d in range(nsub):
        carry = block(pl.multiple_of(qi * tq + d * tk, tk), carry, d * tk)
    _, aa, _, ab = carry
    row = lax.broadcasted_iota(jnp.int32, (LANES, tq), 0)
    ot = jnp.where(row < HEAD_DIM, aa * (1.0 / aa[HEAD_DIM:HEAD_DIM + 1, :]), ab * (1.0 / ab[0:1, :]))
    o_ref[...] = ot.T.astype(o_ref.dtype)


def _fox(fq, fk, fv, cum, tq=2048, tk=512):
    B, S, W = fq.shape
    tq = min(tq, S)
    assert tq % tk == 0 and S % tq == 0
    npairs = W // LANES
    return pl.pallas_call(
        functools.partial(_fox_kernel, tq=tq, tk=tk),
        out_shape=jax.ShapeDtypeStruct((B, S, W), BF16),
        grid=(B, npairs, S // tq),
        in_specs=[pl.BlockSpec((None, tq, LANES), lambda b, p, i: (b, i, p)),
                  pl.BlockSpec((None, S, LANES), lambda b, p, i: (b, 0, p)),
                  pl.BlockSpec((None, S, LANES), lambda b, p, i: (b, 0, p)),
                  pl.BlockSpec((None, S, LANES), lambda b, p, i: (b, 0, 0))],
        out_specs=pl.BlockSpec((None, tq, LANES), lambda b, p, i: (b, i, p)),
        scratch_shapes=[pltpu.VMEM((S, LANES), BF16), pltpu.VMEM((S, LANES), BF16),
                        pltpu.VMEM((LANES, S), BF16), pltpu.VMEM((LANES, S), BF16)],
        compiler_params=_cparams(("parallel", "parallel", "arbitrary")),
    )(fq, fk, fv, cum)


def _hgrn_kernel(hq_ref, hf_ref, hi_ref, hg_ref, lb_ref, nw_ref, o_ref,
                 b_sc, kk_sc, qq_sc, o_sc, w1_sc, w2_sc, w3_sc, w4_sc, w5_sc,
                 p_sc, st16_sc, dec_sc, st64_sc):
    S = hq_ref.shape[0]
    C = HCHUNK
    nchunks = S // C
    BLK = HBLOCK
    nblk = S // BLK

    lg = lb_ref[...]
    e = jnp.exp(lg - jnp.max(lg, axis=0, keepdims=True))
    lb = e[0:1, :] / jnp.sum(e, axis=0, keepdims=True)

    f = lb + (1.0 - lb) * _sigmoid(hf_ref[...])
    lf = jnp.log(f)
    kk_sc[...] = 1.0 - f
    qq_sc[...] = hq_ref[...].astype(F32)

    row = lax.broadcasted_iota(jnp.int32, (S, LANES), 0)
    rb = 4 * BLK
    tr = lax.broadcasted_iota(jnp.int32, (rb, rb), 0)
    tc = lax.broadcasted_iota(jnp.int32, (rb, rb), 1)
    tri = jnp.where(((tr & -BLK) == (tc & -BLK)) & (tc <= tr), 1.0, 0.0).astype(BF16)
    lf3 = jnp.concatenate(_bf16_pieces(lf, 3), axis=1)
    for j in range(S // rb):
        c3 = jnp.dot(tri, lf3[j * rb:(j + 1) * rb, :], preferred_element_type=F32)
        b_sc[j * rb:(j + 1) * rb, :] = c3[:, :LANES] + c3[:, LANES:2 * LANES] + c3[:, 2 * LANES:]
    safe = jnp.max(-b_sc[...].reshape(nblk, BLK, LANES)[:, BLK - 1, :]) <= HGRN_SAFE_EXP

    lane = lax.broadcasted_iota(jnp.int32, (C, LANES), 1)
    sr = lax.broadcasted_iota(jnp.int32, (LANES, LANES), 0)
    scn = lax.broadcasted_iota(jnp.int32, (LANES, LANES), 1)
    same_head = (sr // HEAD_DIM) == (scn // HEAD_DIM)

    @pl.when(safe)
    def _factorised():
        qh_sc, kh_sc, ke_sc, qd_sc, k2_sc = w1_sc, w2_sc, w3_sc, w4_sc, w5_sc
        SB = 2 * BLK
        nsb = S // SB
        bb = b_sc[...]
        dblk = jnp.exp(bb.reshape(nblk, BLK, LANES)[:, BLK - 1:BLK, :])
        dfull = jnp.broadcast_to(dblk, (nblk, BLK, LANES)).reshape(S, LANES)
        second = (row & BLK) != 0
        d_prev = pltpu.roll(dfull, BLK, axis=0)
        d_next = pltpu.roll(dfull, S - BLK, axis=0)
        qh = qq_sc[...] * jnp.exp(bb)
        qh_sc[...] = qh.astype(BF16)
        qd_sc[...] = (qh * jnp.where(second, d_prev, 1.0)).astype(BF16)
        kh = kk_sc[...] * jnp.exp(-bb)
        kh_sc[...] = kh.astype(BF16)
        ke = kh * dfull
        ke_sc[...] = ke.astype(BF16)
        k2_sc[...] = (ke * jnp.where(second, 1.0, d_next)).astype(BF16)
        d3 = dfull.reshape(nsb, SB, LANES)
        dec_sc[pl.ds(0, nsb), :] = d3[:, 0, :] * d3[:, BLK, :]
        unroll = min(16, nsb)
        assert nsb % unroll == 0
        tn = (((0,), (0,)), ((), ()))
        nt = (((1,), (1,)), ((), ()))

        def scan(g, st):
            for u in range(unroll):
                i = g * unroll + u
                r0 = pl.multiple_of(i * SB, SB)
                st64_sc[i] = st.astype(BF16)
                upd = lax.dot_general(hi_ref[pl.ds(r0, SB), :], k2_sc[pl.ds(r0, SB), :], tn,
                                      preferred_element_type=F32)
                st = st * dec_sc[pl.ds(i, 1), :] + jnp.where(same_head, upd, 0.0)
            return st

        lax.fori_loop(0, nsb // unroll, scan, jnp.zeros((LANES, LANES), F32))

        r = lax.broadcasted_iota(jnp.int32, (2 * SB, 2 * SB), 0)
        c = lax.broadcasted_iota(jnp.int32, (2 * SB, 2 * SB), 1)
        t = r & (SB - 1)
        visible = (((c < SB) & ((t & BLK) == (c & BLK)) & ((t & (BLK - 1)) >= (c & (BLK - 1))))
                   | ((c >= SB) & (c < SB + BLK) & (t >= BLK)))
        plane = lax.broadcasted_iota(jnp.int32, (SB, LANES), 1)
        pad = jnp.zeros((BLK, LANES), BF16)

        def readout(g, _):
            for u in range(unroll):
                i = g * unroll + u
                r0 = pl.multiple_of(i * SB, SB)
                vb = hi_ref[pl.ds(r0, SB), :]
                qh2 = qh_sc[pl.ds(r0, SB), :]
                q2 = jnp.concatenate([jnp.where(plane < HEAD_DIM, qh2, jnp.zeros_like(qh2)),
                                      jnp.where(plane >= HEAD_DIM, qh2, jnp.zeros_like(qh2))], axis=0)
                kext = jnp.concatenate([kh_sc[pl.ds(r0, SB), :], ke_sc[pl.ds(r0, BLK), :], pad], axis=0)
                vext = jnp.concatenate([vb, vb[:BLK], pad], axis=0)
                sc = lax.dot_general(q2, kext, nt, preferred_element_type=F32)
                sc = jnp.where(visible, sc, 0.0).astype(BF16)
                out = jnp.dot(sc, vext, preferred_element_type=F32)
                o_inter = lax.dot_general(qd_sc[pl.ds(r0, SB), :], st64_sc[i], nt, preferred_element_type=F32)
                o_sc[pl.ds(r0, SB), :] = jnp.where(plane < HEAD_DIM, out[:SB], out[SB:]) + o_inter
            return 0

        lax.fori_loop(0, nsb // unroll, readout, 0)

    @pl.when(jnp.logical_not(safe))
    def _direct():
        qt_sc, kt_sc, s_sc, a2_sc = w1_sc, w2_sc, w3_sc, b_sc
        bb = b_sc[...]
        cl = jnp.broadcast_to(bb.reshape(nchunks, C, LANES)[:, C - 1:C, :], (nchunks, C, LANES)).reshape(S, LANES)
        aa = bb - jnp.where((row & (BLK - 1)) >= C, pltpu.roll(cl, C, axis=0), 0.0)
        al = jnp.broadcast_to(aa.reshape(nchunks, C, LANES)[:, C - 1:C, :], (nchunks, C, LANES)).reshape(S, LANES)
        qt_sc[...] = (qq_sc[...] * jnp.exp(aa)).astype(BF16)
        kt_sc[...] = (kk_sc[...] * jnp.exp(al - aa)).astype(BF16)
        dec_sc[...] = jnp.exp(aa.reshape(nchunks, C, LANES)[:, C - 1, :])
        a2_sc[...] = aa * LOG2E
        trow = lax.broadcasted_iota(jnp.int32, (C, LANES), 0)

        def gen(c, _):
            r0 = pl.multiple_of(c * C, C)
            ac = a2_sc[pl.ds(r0, C), :]
            qc = qq_sc[pl.ds(r0, C), :]
            kc = kk_sc[pl.ds(r0, C), :]
            half = C // 2
            for s in range(C):
                if s < half:
                    dec = jnp.exp2(jnp.where(trow >= s, ac - ac[s:s + 1, :], NEG_BIG))
                    p = qc * (kc[s:s + 1, :] * dec)
                else:
                    dec = jnp.exp2(jnp.where(trow[half:] >= s, ac[half:] - ac[s:s + 1, :], NEG_BIG))
                    p = jnp.concatenate([jnp.zeros((half, LANES), F32), qc[half:] * (kc[s:s + 1, :] * dec)],
                                        axis=0)
                p_sc[pl.ds(r0, C), s * LANES:(s + 1) * LANES] = p.astype(BF16)
            return 0

        lax.fori_loop(0, nchunks, gen, 0)

        er = lax.broadcasted_iota(jnp.int32, (C * LANES, LANES), 0)
        ec = lax.broadcasted_iota(jnp.int32, (C * LANES, LANES), 1)
        emat = (ec == ((er & (LANES - 1)) // HEAD_DIM) * C + er // LANES).astype(BF16)
        rb = 256

        def red(i, _):
            r0 = pl.multiple_of(i * rb, rb)
            s_sc[pl.ds(r0, rb), :] = jnp.dot(p_sc[pl.ds(r0, rb), :], emat,
                                             preferred_element_type=F32).astype(BF16)
            return 0

        lax.fori_loop(0, S // rb, red, 0)

        unroll = 16
        assert nchunks % unroll == 0

        def scan(g, st):
            for u in range(unroll):
                c = g * unroll + u
                r0 = pl.multiple_of(c * C, C)
                st16_sc[c] = st.astype(BF16)
                upd = lax.dot_general(hi_ref[pl.ds(r0, C), :], kt_sc[pl.ds(r0, C), :],
                                      (((0,), (0,)), ((), ())), preferred_element_type=F32)
                st = st * dec_sc[pl.ds(c, 1), :] + jnp.where(same_head, upd, 0.0)
            return st

        lax.fori_loop(0, nchunks // unroll, scan, jnp.zeros((LANES, LANES), F32))

        def readout(g, _):
            for u in range(unroll):
                c = g * unroll + u
                r0 = pl.multiple_of(c * C, C)
                vc = hi_ref[pl.ds(r0, C), :]
                o_inter = lax.dot_general(qt_sc[pl.ds(r0, C), :], st16_sc[c],
                                          (((1,), (1,)), ((), ())), preferred_element_type=F32)
                v2 = jnp.concatenate([jnp.where(lane < HEAD_DIM, vc, jnp.zeros_like(vc)),
                                      jnp.where(lane >= HEAD_DIM, vc, jnp.zeros_like(vc))], axis=0)
                o_intra = jnp.dot(s_sc[pl.ds(r0, C), :][:, :2 * C], v2, preferred_element_type=F32)
                o_sc[pl.ds(r0, C), :] = o_inter + o_intra
            return 0

        lax.fori_loop(0, nchunks // unroll, readout, 0)

    o = o_sc[...]
    ones_head = jnp.where(same_head, 1.0, 0.0).astype(BF16)
    sq_hi, sq_lo = _bf16_pieces(o * o, 2)
    ms = (jnp.dot(sq_hi, ones_head, preferred_element_type=F32)
          + jnp.dot(sq_lo, ones_head, preferred_element_type=F32)) * (1.0 / HEAD_DIM)
    y = o * lax.rsqrt(ms + RMS_EPS) * nw_ref[...]
    o_ref[...] = (y * hg_ref[...].astype(F32)).astype(o_ref.dtype)


def _hgrn(hq, hf, hi, hg, lb_logits, norm_w):
    B, S, W = hq.shape
    npairs = W // LANES
    nrows = lb_logits.shape[0]
    seq = pl.BlockSpec((None, S, LANES), lambda b, p: (b, 0, p))
    return pl.pallas_call(
        _hgrn_kernel,
        out_shape=jax.ShapeDtypeStruct((B, S, W), BF16),
        grid=(B, npairs),
        in_specs=[seq, seq, seq, seq,
                  pl.BlockSpec((nrows, LANES), lambda b, p: (0, p)),
                  pl.BlockSpec((1, LANES), lambda b, p: (0, p))],
        out_specs=seq,
        scratch_shapes=[pltpu.VMEM((S, LANES), F32),
                        pltpu.VMEM((S, LANES), F32),
                        pltpu.VMEM((S, LANES), F32),
                        pltpu.VMEM((S, LANES), F32),
                        pltpu.VMEM((S, LANES), BF16),
                        pltpu.VMEM((S, LANES), BF16),
                        pltpu.VMEM((S, LANES), BF16),
                        pltpu.VMEM((S, LANES), BF16),
                        pltpu.VMEM((S, LANES), BF16),
                        pltpu.VMEM((S, HCHUNK * LANES), BF16),
                        pltpu.VMEM((S // HCHUNK, LANES, LANES), BF16),
                        pltpu.VMEM((S // HCHUNK, LANES), F32),
                        pltpu.VMEM((S // HBLOCK, LANES, LANES), BF16)],
        compiler_params=_cparams(("parallel", "parallel")),
    )(hq, hf, hi, hg, lb_logits, norm_w.reshape(1, W))


def _layer_norm(v, g, b):
    mu = jnp.mean(v, axis=-1, keepdims=True)
    d = v - mu
    var = jnp.mean(d * d, axis=-1, keepdims=True)
    return d * lax.rsqrt(var + LN_EPS) * g + b


def _bf16_bits(x):
    return (pltpu.bitcast(x, jnp.uint32) + jnp.uint32(0x8000)) & jnp.uint32(0xFFFF0000)


def _store_chunks(ref, val):
    n = ref.shape[0]
    for j in range(n):
        lo = _bf16_bits(val[:, j * LANES:(j + 1) * LANES]) >> 16
        hi = _bf16_bits(val[:, (j + n) * LANES:(j + n + 1) * LANES])
        ref[j] = pltpu.bitcast(lo | hi, F32)


def _load_chunks(ref):
    words = [pltpu.bitcast(ref[j], jnp.uint32) for j in range(ref.shape[0])]
    lo = [pltpu.bitcast(w << 16, F32) for w in words]
    hi = [pltpu.bitcast(w & jnp.uint32(0xFFFF0000), F32) for w in words]
    return jnp.concatenate(lo + hi, axis=1)


def _mix_kernel(yf_ref, oh_ref, gf_ref, gh_ref, x_ref, g1_ref, sc2_ref, sh2_ref,
                wuf_ref, wuh_ref, wo_ref, lg_ref, lbias_ref, wr_ref, br_ref,
                x1_ref, h2_ref, ri_ref, rt_ref, cnt_ref, carry_sc, *, alpha, ngroups, nper):
    first = (pl.program_id(0) == 0) & (pl.program_id(1) == 0)

    @pl.when(first)
    def _():
        carry_sc[...] = jnp.zeros_like(carry_sc)

    tm = x_ref.shape[0]
    yf = jnp.dot(yf_ref[...], wuf_ref[...], preferred_element_type=F32)
    yh = jnp.dot(oh_ref[...], wuh_ref[...], preferred_element_type=F32)
    merged = gf_ref[...].astype(F32) * yf + gh_ref[...].astype(F32) * yh
    y = jnp.dot(merged.astype(BF16), wo_ref[...], preferred_element_type=F32)
    x1 = _layer_norm(alpha * x_ref[...] + g1_ref[...] * y, lg_ref[...], lbias_ref[...])
    x1_ref[...] = x1
    h2 = x1 * (1.0 + sc2_ref[...]) + sh2_ref[...]
    _store_chunks(h2_ref, h2)

    h_hi, h_lo = _bf16_pieces(h2, 2)
    hh = jnp.dot(h_hi, wr_ref[...], preferred_element_type=F32)
    logits = (hh[:, :LANES] + hh[:, LANES:]
              + jnp.dot(h_lo, wr_ref[:, :LANES], preferred_element_type=F32)) + br_ref[...]
    lane = lax.broadcasted_iota(jnp.int32, (tm, LANES), 1)
    big = jnp.int32(1 << 20)

    def argmax_first(vals, mask):
        mx = jnp.max(jnp.where(mask, vals, -jnp.inf), axis=1, keepdims=True)
        idx = jnp.min(jnp.where(mask & (vals == mx), lane, big), axis=1, keepdims=True)
        return mx, idx

    gmask = lane < ngroups
    gmax = jnp.max(jnp.where(gmask, logits, -jnp.inf), axis=1, keepdims=True)
    gexp = jnp.where(gmask, jnp.exp(logits - gmax), 0.0)
    gprob = gexp / jnp.sum(gexp, axis=1, keepdims=True)
    g_w, g_idx = argmax_first(gprob, gmask)

    lo = ngroups + g_idx * nper
    emask = (lane >= lo) & (lane < lo + nper)
    emax = jnp.max(jnp.where(emask, logits, -jnp.inf), axis=1, keepdims=True)
    eexp = jnp.where(emask, jnp.exp(logits - emax), 0.0)
    eprob = eexp / jnp.sum(eexp, axis=1, keepdims=True)
    p0, i0 = argmax_first(eprob, emask)
    p1, i1 = argmax_first(eprob, emask & (lane != i0))
    den = p0 + p1
    w0 = p0 / den * g_w
    w1 = p1 / den * g_w
    e0 = i0 - ngroups
    e1 = i1 - ngroups

    oh = ((lane == e0) | (lane == e1)).astype(F32)
    r = lax.broadcasted_iota(jnp.int32, (tm, tm), 0)
    c = lax.broadcasted_iota(jnp.int32, (tm, tm), 1)
    strict_lower = (c < r).astype(BF16)
    before = jnp.dot(strict_lower, oh.astype(BF16), preferred_element_type=F32) + carry_sc[...]
    rank0 = jnp.sum(jnp.where(lane == e0, before, 0.0), axis=1, keepdims=True)
    rank1 = jnp.sum(jnp.where(lane == e1, before, 0.0), axis=1, keepdims=True)
    carry_sc[...] = carry_sc[...] + jnp.sum(oh, axis=0, keepdims=True)
    cnt_ref[...] = carry_sc[...]

    info = jnp.where(lane == 0, w0, 0.0)
    info = jnp.where(lane == 1, w1, info)
    info = jnp.where(lane == 2, e0.astype(F32), info)
    info = jnp.where(lane == 3, e1.astype(F32), info)
    info = jnp.where(lane == 4, rank0, info)
    info = jnp.where(lane == 5, rank1, info)
    ri_ref[...] = info
    rt_ref[...] = info.T[:ROW_TILE, :]


def _mix(yf, oh, gf, gh, x, g1, sc2, sh2, wuf, wuh, wo, ln_g, ln_b, wr, br, alpha, ngroups, nper, tm=512):
    B, S, D = x.shape
    W = yf.shape[2]
    tok = lambda w: pl.BlockSpec((None, tm, w), lambda b, i: (b, i, 0))
    vec = pl.BlockSpec((None, 1, D), lambda b, i: (b, 0, 0))
    full = lambda a: pl.BlockSpec(a.shape, lambda b, i: (0,) * a.ndim)
    return pl.pallas_call(
        functools.partial(_mix_kernel, alpha=alpha, ngroups=ngroups, nper=nper),
        out_shape=(jax.ShapeDtypeStruct((B, S, D), F32),
                   jax.ShapeDtypeStruct((D // WORD_LANES, B * S, LANES), F32),
                   jax.ShapeDtypeStruct((B, S, LANES), F32),
                   jax.ShapeDtypeStruct((ROW_TILE, B * S), F32),
                   jax.ShapeDtypeStruct((1, LANES), F32)),
        grid=(B, S // tm),
        in_specs=[tok(W), tok(W), tok(D), tok(D), tok(D), vec, vec, vec,
                  full(wuf), full(wuh), full(wo), full(ln_g), full(ln_b), full(wr), full(br)],
        out_specs=(tok(D),
                   pl.BlockSpec((D // WORD_LANES, tm, LANES), lambda b, i: (0, b * (S // tm) + i, 0)),
                   tok(LANES),
                   pl.BlockSpec((ROW_TILE, tm), lambda b, i: (0, b * (S // tm) + i)),
                   pl.BlockSpec((1, LANES), lambda b, i: (0, 0))),
        scratch_shapes=[pltpu.VMEM((1, LANES), F32)],
        compiler_params=_cparams(("arbitrary", "arbitrary")),
    )(yf, oh, gf, gh, x, g1, sc2, sh2, wuf, wuh, wo, ln_g, ln_b, wr, br)


def _sc_mesh():
    return plsc.VectorSubcoreMesh(core_axis_name="core", subcore_axis_name="subcore")


def _sc_pipeline(body, grid, in_specs, out_specs):
    return pltpu.emit_pipeline(body, grid=grid, in_specs=in_specs, out_specs=out_specs,
                               core_axis_name=("core", "subcore"),
                               dimension_semantics=(pltpu.PARALLEL,) * len(grid))


def _sc_scatter_rows(src, rows_a, rows_b, n_out):
    nj, t = rows_a.shape
    nc = t // LANES

    @pl.kernel(out_type=jax.ShapeDtypeStruct((n_out, LANES), src.dtype), mesh=_sc_mesh(), scratch_types=[])
    def scatter(x_hbm, a_hbm, b_hbm, o_hbm):
        def body(x_vmem, a_vmem, b_vmem):
            pltpu.sync_copy(x_vmem, o_hbm.at[a_vmem.at[0]])
            pltpu.sync_copy(x_vmem, o_hbm.at[b_vmem.at[0]])

        idx = pl.BlockSpec((1, LANES), lambda j, c: (j, c))
        _sc_pipeline(body, (nj, nc), [pl.BlockSpec((LANES, LANES), lambda j, c: (j * nc + c, 0)), idx, idx],
                     [])(x_hbm, a_hbm, b_hbm)

    return scatter(src, rows_a, rows_b)


def _sc_gather_rows(table, rows):
    nr, t = rows.shape
    nc = t // LANES

    @pl.kernel(out_type=jax.ShapeDtypeStruct((nr * t, LANES), table.dtype), mesh=_sc_mesh(), scratch_types=[])
    def gather(x_hbm, i_hbm, o_hbm):
        def body(i_vmem, o_vmem):
            pltpu.sync_copy(x_hbm.at[i_vmem.at[0]], o_vmem)

        _sc_pipeline(body, (nr, nc), [pl.BlockSpec((1, LANES), lambda r, c: (r, c))],
                     [pl.BlockSpec((LANES, LANES), lambda r, c: (r * nc + c, 0))])(i_hbm, o_hbm)

    return gather(table, rows)


def _experts_kernel(te_ref, tn_ref, tb_ref, x_ref, wg_ref, wu_ref, wd_ref, o_ref):
    del tb_ref
    nrows = tn_ref[pl.program_id(0)]

    @pl.when(nrows > 0)
    def _():
        x = _load_chunks(x_ref)
        x = jnp.where(lax.broadcasted_iota(jnp.int32, x.shape, 0) < nrows, x, 0.0).astype(BF16)
        g = jnp.dot(x, wg_ref[...].astype(BF16), preferred_element_type=F32)
        u = jnp.dot(x, wu_ref[...].astype(BF16), preferred_element_type=F32)
        hid = (_silu(g) * u).astype(BF16)
        _store_chunks(o_ref, jnp.dot(hid, wd_ref[...].astype(BF16), preferred_element_type=F32))


def _experts(tile_expert, tile_rows, tile_block, xs, wg, wu, wd, tm):
    E, D, FF = wg.shape
    dt = D // WORD_LANES
    ntiles = tile_expert.shape[0]
    rows = pl.BlockSpec((dt, tm, LANES), lambda i, te, tn, tb: (0, tb[i], 0))
    grid_spec = pltpu.PrefetchScalarGridSpec(
        num_scalar_prefetch=3,
        grid=(ntiles,),
        in_specs=[rows,
                  pl.BlockSpec((None, D, FF), lambda i, te, tn, tb: (te[i], 0, 0)),
                  pl.BlockSpec((None, D, FF), lambda i, te, tn, tb: (te[i], 0, 0)),
                  pl.BlockSpec((None, FF, D), lambda i, te, tn, tb: (te[i], 0, 0))],
        out_specs=rows,
    )
    return pl.pallas_call(
        _experts_kernel,
        out_shape=jax.ShapeDtypeStruct((dt, ntiles * tm, LANES), F32),
        grid_spec=grid_spec,
        compiler_params=_cparams(("arbitrary",)),
    )(tile_expert, tile_rows, tile_block, xs, wg, wu, wd)


def _combine_kernel(yg_ref, x1_ref, ri_ref, g2_ref, lg_ref, lb_ref, o_ref, *, alpha):
    ri = ri_ref[...]
    y = ri[:, 0:1] * _load_chunks(yg_ref.at[0]) + ri[:, 1:2] * _load_chunks(yg_ref.at[1])
    o_ref[...] = _layer_norm(alpha * x1_ref[...] + g2_ref[...] * y, lg_ref[...], lb_ref[...])


def _combine(yg, x1, rinfo, g2, ln_g, ln_b, alpha, tm=512):
    B, S, D = x1.shape
    nb = S // tm
    return pl.pallas_call(
        functools.partial(_combine_kernel, alpha=alpha),
        out_shape=jax.ShapeDtypeStruct((B, S, D), F32),
        grid=(B, nb),
        in_specs=[pl.BlockSpec((2, D // WORD_LANES, tm, LANES), lambda b, i: (0, 0, b * nb + i, 0)),
                  pl.BlockSpec((None, tm, D), lambda b, i: (b, i, 0)),
                  pl.BlockSpec((None, tm, LANES), lambda b, i: (b, i, 0)),
                  pl.BlockSpec((None, 1, D), lambda b, i: (b, 0, 0)),
                  pl.BlockSpec((1, D), lambda b, i: (0, 0)),
                  pl.BlockSpec((1, D), lambda b, i: (0, 0))],
        out_specs=pl.BlockSpec((None, tm, D), lambda b, i: (b, i, 0)),
        compiler_params=_cparams(("parallel", "parallel")),
    )(yg, x1, rinfo, g2, ln_g, ln_b)


def kernel(x, c, w_ada, b_ada, w_in, b_fox_forget, hgrn_lb_logits, hgrn_norm_w, w_up_fox, w_up_hgrn, w_out,
           ln1_g, ln1_b, w_router_group, b_router_group, w_router_expert, b_router_expert,
           w_expert_gate, w_expert_up, w_expert_down, ln2_g, ln2_b):
    B, S, D = x.shape
    depth = w_ada.shape[0]
    assert depth == 1, "single-layer block"
    fox_heads = b_fox_forget.shape[1]
    fox_w = fox_heads * HEAD_DIM
    hgrn_w = hgrn_norm_w.shape[1]
    ngroups = w_router_group.shape[2]
    nexp = w_router_expert.shape[2]
    nper = nexp // ngroups
    alpha = (2 * depth) ** 0.25
    T = B * S

    ada = _ada(c, w_ada[0], b_ada[0])
    sh1, sc1, g1, sh2, sc2, g2 = [a.reshape(B, 1, D) for a in jnp.split(ada, 6, axis=-1)]

    wi = w_in[0]
    o_ff = 3 * fox_w
    w_fox = jnp.pad(wi[:, :o_ff + fox_heads], ((0, 0), (0, LANES - fox_heads))).astype(BF16)
    w_rest = wi[:, o_ff + fox_heads:].astype(BF16)
    widths = [fox_w, fox_w, fox_w, LANES, hgrn_w, hgrn_w, hgrn_w, hgrn_w, D, D]
    segs, off = [], 0
    for n, w in enumerate(widths):
        if n == 4:
            off = 0
        segs.append((off, off + w))
        off += w
    fq, fk, fv, ffp, hq, hf, hi, hg, gf, gh = _inproj(x, sc1, sh1, w_fox, w_rest, segs)

    bias_p = jnp.zeros((1, LANES), F32).at[0, :fox_heads].set(b_fox_forget[0])
    cum = _foxcum(ffp, bias_p)
    y_fox = _fox(fq, fk, fv, cum)

    o_h = _hgrn(hq, hf, hi, hg, hgrn_lb_logits, hgrn_norm_w[0])

    wr = jnp.zeros((D, LANES), F32).at[:, :ngroups].set(w_router_group[0]).at[:, ngroups:ngroups + nexp].set(
        w_router_expert[0])
    wr_hi = lax.bitcast_convert_type(lax.bitcast_convert_type(wr, jnp.uint32) & jnp.uint32(0xFFFF0000), F32)
    wr = jnp.concatenate([wr_hi.astype(BF16), (wr - wr_hi).astype(BF16)], axis=1)
    br = jnp.zeros((1, LANES), F32).at[0, :ngroups].set(b_router_group[0]).at[0, ngroups:ngroups + nexp].set(
        b_router_expert[0])
    x1, h2, rinfo, fields, counts = _mix(
        y_fox, o_h, gf, gh, x, g1, sc2, sh2,
        w_up_fox[0].astype(BF16), w_up_hgrn[0].astype(BF16), w_out[0].astype(BF16),
        ln1_g[0].reshape(1, D), ln1_b[0].reshape(1, D), wr, br, alpha, ngroups, nper)

    tm_e = 512
    dt = D // WORD_LANES
    ntiles = (2 * T) // tm_e + nexp
    nslots = ntiles * tm_e
    cnt = counts[0, :nexp].astype(jnp.int32)
    padded = ((cnt + tm_e - 1) // tm_e) * tm_e
    ends = jnp.cumsum(padded)
    starts = ends - padded
    eid = fields[2:4].astype(jnp.int32)
    rank = fields[4:6].astype(jnp.int32)
    first = jnp.sum(jnp.where(eid[None] == jnp.arange(nexp, dtype=jnp.int32)[:, None, None],
                              starts[:, None, None], 0), axis=0)
    pos = first + rank
    tile_start = jnp.arange(ntiles, dtype=jnp.int32) * tm_e
    tile_block = jnp.minimum(jnp.arange(ntiles, dtype=jnp.int32), ends[-1] // tm_e - 1)
    tile_expert = jnp.minimum(jnp.sum((tile_start[:, None] >= ends[None, :]).astype(jnp.int32), axis=1), nexp - 1)
    tile_rows = jnp.clip(starts[tile_expert] + cnt[tile_expert] - tile_start, 0, tm_e)
    tile_expert = tile_expert[tile_block]
    rows = pos[:, None, :] + (jnp.arange(dt, dtype=jnp.int32) * nslots)[None, :, None]

    xs = _sc_scatter_rows(h2.reshape(dt * T, LANES), rows[0], rows[1], dt * nslots)
    ys = _experts(tile_expert, tile_rows, tile_block, xs.reshape(dt, nslots, LANES),
                  w_expert_gate[0], w_expert_up[0], w_expert_down[0], tm_e)
    yg = _sc_gather_rows(ys.reshape(dt * nslots, LANES), rows.reshape(2 * dt, T))
    return _combine(yg.reshape(2, dt, T, LANES), x1, rinfo, g2,
                    ln2_g[0].reshape(1, D), ln2_b[0].reshape(1, D), alpha)
```

```python
import functools

import jax
import jax.numpy as jnp
from jax import lax
from jax.experimental import pallas as pl
from jax.experimental.pallas import tpu as pltpu
from jax.experimental.pallas import tpu_sc as plsc

F32 = jnp.float32
BF16 = jnp.bfloat16

LANES = 128
HEAD_DIM = 64
LN_EPS = 1e-5
RMS_EPS = 1e-6
LOG2E = 1.4426950408889634
NEG_BIG = -1e30
HCHUNK = 16
HBLOCK = 64
HGRN_SAFE_EXP = 60.0
ROW_TILE = 8
WORD_LANES = 2 * LANES
SC_WINDOW = 256
VMEM_LIMIT = 56 * 1024 * 1024


def _cparams(sem, vmem=VMEM_LIMIT):
    return pltpu.CompilerParams(dimension_semantics=sem, vmem_limit_bytes=vmem)


def _sigmoid(x):
    return 0.5 * jnp.tanh(0.5 * x) + 0.5


def _silu(x):
    return x * _sigmoid(x)


def _bf16_pieces(x, n):
    pieces = []
    for _ in range(n):
        top = pltpu.bitcast(pltpu.bitcast(x, jnp.uint32) & jnp.uint32(0xFFFF0000), F32)
        pieces.append(top.astype(BF16))
        x = x - top
    return pieces


def _exact_matrix_dot(m, x):
    r = jnp.dot(m, jnp.concatenate(_bf16_pieces(x, 3), axis=1), preferred_element_type=F32)
    return r[:, :LANES] + r[:, LANES:2 * LANES] + r[:, 2 * LANES:]


def _ada_kernel(c_ref, w_ref, b_ref, o_ref):
    c_hi, c_lo = _bf16_pieces(_silu(c_ref[...]), 2)
    w_hi, w_lo = _bf16_pieces(w_ref[...], 2)
    o_ref[...] = (jnp.dot(c_hi, w_hi, preferred_element_type=F32) + jnp.dot(c_hi, w_lo, preferred_element_type=F32)
                  + jnp.dot(c_lo, w_hi, preferred_element_type=F32)) + b_ref[...]


def _ada(c, w_ada, b_ada):
    B, D = c.shape
    N = w_ada.shape[1]
    tn = 1024
    return pl.pallas_call(
        _ada_kernel,
        out_shape=jax.ShapeDtypeStruct((B, N), F32),
        grid=(N // tn,),
        in_specs=[pl.BlockSpec((B, D), lambda j: (0, 0)),
                  pl.BlockSpec((D, tn), lambda j: (0, j)),
                  pl.BlockSpec((1, tn), lambda j: (0, j))],
        out_specs=pl.BlockSpec((B, tn), lambda j: (0, j)),
        compiler_params=_cparams(("arbitrary",)),
    )(c, w_ada, b_ada.reshape(1, N))


N_FOX_SEGS = 4
SILU_SEGS = (4, 7)
SIGMOID_SEGS = (8, 9)


def _inproj_kernel(x_ref, sc_ref, sh_ref, wf_ref, wr_ref,
                   fq_ref, fk_ref, fv_ref, ff_ref, hq_ref, hf_ref, hi_ref, hg_ref, gf_ref, gh_ref,
                   *, segs, q_scale):
    h = (x_ref[...] * (1.0 + sc_ref[...]) + sh_ref[...]).astype(BF16)
    outs = (fq_ref, fk_ref, fv_ref, ff_ref, hq_ref, hf_ref, hi_ref, hg_ref, gf_ref, gh_ref)
    for idx, (o_ref, (a, b)) in enumerate(zip(outs, segs)):
        w_ref = wf_ref if idx < N_FOX_SEGS else wr_ref
        r = jnp.dot(h, w_ref[:, a:b], preferred_element_type=F32)
        if idx == 0:
            r = r * q_scale
        elif idx in SILU_SEGS:
            r = _silu(r)
        elif idx in SIGMOID_SEGS:
            r = _sigmoid(r)
        o_ref[...] = r.astype(o_ref.dtype)


def _inproj(x, sc1, sh1, w_fox, w_rest, segs, tm=256):
    B, S, D = x.shape
    widths = [b - a for a, b in segs]
    dtypes = [BF16, BF16, BF16, F32, BF16, F32, BF16, BF16, BF16, BF16]
    out_shape = tuple(jax.ShapeDtypeStruct((B, S, w), dt) for w, dt in zip(widths, dtypes))
    out_specs = tuple(pl.BlockSpec((None, tm, w), lambda b, i: (b, i, 0)) for w in widths)
    vec = pl.BlockSpec((None, 1, D), lambda b, i: (b, 0, 0))
    return pl.pallas_call(
        functools.partial(_inproj_kernel, segs=tuple(segs), q_scale=HEAD_DIM ** -0.5 * LOG2E),
        out_shape=out_shape,
        grid=(B, S // tm),
        in_specs=[pl.BlockSpec((None, tm, D), lambda b, i: (b, i, 0)), vec, vec,
                  pl.BlockSpec(w_fox.shape, lambda b, i: (0, 0)),
                  pl.BlockSpec(w_rest.shape, lambda b, i: (0, 0))],
        out_specs=out_specs,
        compiler_params=_cparams(("parallel", "parallel")),
    )(x, sc1, sh1, w_fox, w_rest)


def _foxcum_kernel(ff_ref, b_ref, o_ref, *, blk):
    S = ff_ref.shape[0]
    r = lax.broadcasted_iota(jnp.int32, (blk, blk), 0)
    c = lax.broadcasted_iota(jnp.int32, (blk, blk), 1)
    lower = jnp.where(r >= c, 1.0, 0.0).astype(BF16)
    carry = jnp.zeros((1, LANES), F32)
    for j in range(S // blk):
        z = ff_ref[j * blk:(j + 1) * blk, :] + b_ref[...]
        lf = jnp.minimum(z, 0.0) - jnp.log(1.0 + jnp.exp(-jnp.abs(z)))
        cum = _exact_matrix_dot(lower, lf) + carry
        o_ref[j * blk:(j + 1) * blk, :] = cum * LOG2E
        carry = cum[blk - 1:blk, :]


def _foxcum(ffp, bias_p, blk=256):
    B, S, _ = ffp.shape
    return pl.pallas_call(
        functools.partial(_foxcum_kernel, blk=blk),
        out_shape=jax.ShapeDtypeStruct((B, S, LANES), F32),
        grid=(B,),
        in_specs=[pl.BlockSpec((None, S, LANES), lambda b: (b, 0, 0)),
                  pl.BlockSpec((1, LANES), lambda b: (0, 0))],
        out_specs=pl.BlockSpec((None, S, LANES), lambda b: (b, 0, 0)),
        compiler_params=_cparams(("parallel",)),
    )(ffp, bias_p)


NCUM = 3


def _fox_kernel(q_ref, k_ref, v_ref, c_ref, o_ref, ka_sc, kb_sc, va_sc, vb_sc, *, tq, tk):
    p = pl.program_id(1)
    qi = pl.program_id(2)
    S = k_ref.shape[0]

    @pl.when(qi == 0)
    def _():
        lane = lax.broadcasted_iota(jnp.int32, (S, LANES), 1)
        rr = lax.broadcasted_iota(jnp.int32, (LANES, LANES), 0)
        cc = lax.broadcasted_iota(jnp.int32, (LANES, LANES), 1)
        rest = c_ref[...]
        placed = jnp.zeros((S, LANES), F32)
        for i in range(NCUM):
            piece = rest.astype(BF16)
            rest = rest - piece.astype(F32)
            sel = ((rr == 2 * p) & (cc == HEAD_DIM + i)) | ((rr == 2 * p + 1) & (cc == i))
            placed = placed + jnp.dot(piece, jnp.where(sel, 1.0, 0.0).astype(BF16), preferred_element_type=F32)
        k2 = k_ref[...].astype(F32)
        ka_sc[...] = jnp.where(lane < HEAD_DIM, k2, -placed).astype(BF16)
        kb_sc[...] = jnp.where(lane >= HEAD_DIM, k2, -placed).astype(BF16)
        vt = v_ref[...].astype(F32).T
        row = lax.broadcasted_iota(jnp.int32, (LANES, S), 0)
        va_sc[...] = jnp.where(row < HEAD_DIM, vt, jnp.where(row == HEAD_DIM, 1.0, 0.0)).astype(BF16)
        vb_sc[...] = jnp.where(row >= HEAD_DIM, vt, jnp.where(row == 0, 1.0, 0.0)).astype(BF16)

    q2 = q_ref[...].astype(F32)
    qlane = lax.broadcasted_iota(jnp.int32, (tq, LANES), 1)
    qa = jnp.where(qlane < HEAD_DIM, q2, jnp.where(qlane < HEAD_DIM + NCUM, 1.0, 0.0)).astype(BF16)
    qb = jnp.where(qlane >= HEAD_DIM, q2, jnp.where(qlane < NCUM, 1.0, 0.0)).astype(BF16)
    nsub = tq // tk

    def block(k0, carry, diag_off):
        q0 = 0 if diag_off is None else diag_off
        out = []
        for ksc, vsc, qh, (m, acc) in ((ka_sc, va_sc, qa, carry[:2]), (kb_sc, vb_sc, qb, carry[2:])):
            st = lax.dot_general(ksc[pl.ds(k0, tk), :], qh[q0:, :], (((1,), (1,)), ((), ())),
                                 preferred_element_type=F32)
            if diag_off is not None:
                st = jnp.where(lax.broadcasted_iota(jnp.int32, st.shape, 0)
                               <= lax.broadcasted_iota(jnp.int32, st.shape, 1), st, NEG_BIG)
            m_old = m[:, q0:]
            m_new = jnp.maximum(m_old, jnp.max(st, axis=0, keepdims=True))
            pt = jnp.exp2(st - m_new).astype(BF16)
            acc_new = (jnp.exp2(m_old - m_new) * acc[:, q0:]
                       + jnp.dot(vsc[:, pl.ds(k0, tk)], pt, preferred_element_type=F32))
            if q0:
                m_new = jnp.concatenate([m[:, :q0], m_new], axis=1)
                acc_new = jnp.concatenate([acc[:, :q0], acc_new], axis=1)
            out += [m_new, acc_new]
        return tuple(out)

    def group(j, carry):
        k0 = pl.multiple_of(j * (nsub * tk), nsub * tk)
        for u in range(nsub):
            carry = block(k0 + u * tk, carry, None)
        return carry

    m0 = jnp.full((1, tq), NEG_BIG, F32)
    a0 = jnp.zeros((LANES, tq), F32)
    carry = lax.fori_loop(0, qi, group, (m0, a0, m0, a0))
    for d in range(nsub):
        carry = block(pl.multiple_of(qi * tq + d * tk, tk), carry, d * tk)
    _, aa, _, ab = carry
    row = lax.broadcasted_iota(jnp.int32, (LANES, tq), 0)
    ot = jnp.where(row < HEAD_DIM, aa * (1.0 / aa[HEAD_DIM:HEAD_DIM + 1, :]), ab * (1.0 / ab[0:1, :]))
    o_ref[...] = ot.T.astype(o_ref.dtype)


def _fox(fq, fk, fv, cum, tq=2048, tk=512):
    B, S, W = fq.shape
    tq = min(tq, S)
    assert tq % tk == 0 and S % tq == 0
    npairs = W // LANES
    return pl.pallas_call(
        functools.partial(_fox_kernel, tq=tq, tk=tk),
        out_shape=jax.ShapeDtypeStruct((B, S, W), BF16),
        grid=(B, npairs, S // tq),
        in_specs=[pl.BlockSpec((None, tq, LANES), lambda b, p, i: (b, i, p)),
                  pl.BlockSpec((None, S, LANES), lambda b, p, i: (b, 0, p)),
                  pl.BlockSpec((None, S, LANES), lambda b, p, i: (b, 0, p)),
                  pl.BlockSpec((None, S, LANES), lambda b, p, i: (b, 0, 0))],
        out_specs=pl.BlockSpec((None, tq, LANES), lambda b, p, i: (b, i, p)),
        scratch_shapes=[pltpu.VMEM((S, LANES), BF16), pltpu.VMEM((S, LANES), BF16),
                        pltpu.VMEM((LANES, S), BF16), pltpu.VMEM((LANES, S), BF16)],
        compiler_params=_cparams(("parallel", "parallel", "arbitrary")),
    )(fq, fk, fv, cum)


def _hgrn_kernel(hq_ref, hf_ref, hi_ref, hg_ref, lb_ref, nw_ref, o_ref,
                 b_sc, kk_sc, qq_sc, o_sc, w1_sc, w2_sc, w3_sc, w4_sc, w5_sc,
                 p_sc, st16_sc, dec_sc, st64_sc):
    S = hq_ref.shape[0]
    C = HCHUNK
    nchunks = S // C
    BLK = HBLOCK
    nblk = S // BLK

    lg = lb_ref[...]
    e = jnp.exp(lg - jnp.max(lg, axis=0, keepdims=True))
    lb = e[0:1, :] / jnp.sum(e, axis=0, keepdims=True)

    f = lb + (1.0 - lb) * (1.0 / (1.0 + jnp.exp(-hf_ref[...])))
    lf = jnp.log(f)
    kk_sc[...] = 1.0 - f
    qq_sc[...] = hq_ref[...].astype(F32)

    row = lax.broadcasted_iota(jnp.int32, (S, LANES), 0)
    rb = 4 * BLK
    tr = lax.broadcasted_iota(jnp.int32, (rb, rb), 0)
    tc = lax.broadcasted_iota(jnp.int32, (rb, rb), 1)
    tri = jnp.where(((tr & -BLK) == (tc & -BLK)) & (tc <= tr), 1.0, 0.0).astype(BF16)
    lf3 = jnp.concatenate(_bf16_pieces(lf, 3), axis=1)
    for j in range(S // rb):
        c3 = jnp.dot(tri, lf3[j * rb:(j + 1) * rb, :], preferred_element_type=F32)
        b_sc[j * rb:(j + 1) * rb, :] = c3[:, :LANES] + c3[:, LANES:2 * LANES] + c3[:, 2 * LANES:]
    safe = jnp.max(-b_sc[...].reshape(nblk, BLK, LANES)[:, BLK - 1, :]) <= HGRN_SAFE_EXP

    lane = lax.broadcasted_iota(jnp.int32, (C, LANES), 1)
    sr = lax.broadcasted_iota(jnp.int32, (LANES, LANES), 0)
    scn = lax.broadcasted_iota(jnp.int32, (LANES, LANES), 1)
    same_head = (sr // HEAD_DIM) == (scn // HEAD_DIM)

    @pl.when(safe)
    def _factorised():
        qh_sc, kh_sc, ke_sc, qd_sc, k2_sc = w1_sc, w2_sc, w3_sc, w4_sc, w5_sc
        SB = 2 * BLK
        nsb = S // SB
        bb = b_sc[...]
        dblk = jnp.exp(bb.reshape(nblk, BLK, LANES)[:, BLK - 1:BLK, :])
        dfull = jnp.broadcast_to(dblk, (nblk, BLK, LANES)).reshape(S, LANES)
        second = (row & BLK) != 0
        d_prev = pltpu.roll(dfull, BLK, axis=0)
        d_next = pltpu.roll(dfull, S - BLK, axis=0)
        qh = qq_sc[...] * jnp.exp(bb)
        qh_sc[...] = qh.astype(BF16)
        qd_sc[...] = (qh * jnp.where(second, d_prev, 1.0)).astype(BF16)
        kh = kk_sc[...] * jnp.exp(-bb)
        kh_sc[...] = kh.astype(BF16)
        ke = kh * dfull
        ke_sc[...] = ke.astype(BF16)
        k2_sc[...] = (ke * jnp.where(second, 1.0, d_next)).astype(BF16)
        d3 = dfull.reshape(nsb, SB, LANES)
        dec_sc[pl.ds(0, nsb), :] = d3[:, 0, :] * d3[:, BLK, :]
        unroll = min(16, nsb)
        assert nsb % unroll == 0
        tn = (((0,), (0,)), ((), ()))
        nt = (((1,), (1,)), ((), ()))

        def scan(g, st):
            for u in range(unroll):
                i = g * unroll + u
                r0 = pl.multiple_of(i * SB, SB)
                st64_sc[i] = st.astype(BF16)
                upd = lax.dot_general(hi_ref[pl.ds(r0, SB), :], k2_sc[pl.ds(r0, SB), :], tn,
                                      preferred_element_type=F32)
                st = st * dec_sc[pl.ds(i, 1), :] + jnp.where(same_head, upd, 0.0)
            return st

        lax.fori_loop(0, nsb // unroll, scan, jnp.zeros((LANES, LANES), F32))

        r = lax.broadcasted_iota(jnp.int32, (2 * SB, 2 * SB), 0)
        c = lax.broadcasted_iota(jnp.int32, (2 * SB, 2 * SB), 1)
        t = r & (SB - 1)
        visible = (((c < SB) & ((t & BLK) == (c & BLK)) & ((t & (BLK - 1)) >= (c & (BLK - 1))))
                   | ((c >= SB) & (c < SB + BLK) & (t >= BLK)))
        plane = lax.broadcasted_iota(jnp.int32, (SB, LANES), 1)
        pad = jnp.zeros((BLK, LANES), BF16)

        def readout(g, _):
            for u in range(unroll):
                i = g * unroll + u
                r0 = pl.multiple_of(i * SB, SB)
                vb = hi_ref[pl.ds(r0, SB), :]
                qh2 = qh_sc[pl.ds(r0, SB), :]
                q2 = jnp.concatenate([jnp.where(plane < HEAD_DIM, qh2, jnp.zeros_like(qh2)),
                                      jnp.where(plane >= HEAD_DIM, qh2, jnp.zeros_like(qh2))], axis=0)
                kext = jnp.concatenate([kh_sc[pl.ds(r0, SB), :], ke_sc[pl.ds(r0, BLK), :], pad], axis=0)
                vext = jnp.concatenate([vb, vb[:BLK], pad], axis=0)
                sc = lax.dot_general(q2, kext, nt, preferred_element_type=F32)
                sc = jnp.where(visible, sc, 0.0).astype(BF16)
                out = jnp.dot(sc, vext, preferred_element_type=F32)
                o_inter = lax.dot_general(qd_sc[pl.ds(r0, SB), :], st64_sc[i], nt, preferred_element_type=F32)
                o_sc[pl.ds(r0, SB), :] = jnp.where(plane < HEAD_DIM, out[:SB], out[SB:]) + o_inter
            return 0

        lax.fori_loop(0, nsb // unroll, readout, 0)

    @pl.when(jnp.logical_not(safe))
    def _direct():
        qt_sc, kt_sc, s_sc, a2_sc = w1_sc, w2_sc, w3_sc, b_sc
        bb = b_sc[...]
        cl = jnp.broadcast_to(bb.reshape(nchunks, C, LANES)[:, C - 1:C, :], (nchunks, C, LANES)).reshape(S, LANES)
        aa = bb - jnp.where((row & (BLK - 1)) >= C, pltpu.roll(cl, C, axis=0), 0.0)
        al = jnp.broadcast_to(aa.reshape(nchunks, C, LANES)[:, C - 1:C, :], (nchunks, C, LANES)).reshape(S, LANES)
        qt_sc[...] = (qq_sc[...] * jnp.exp(aa)).astype(BF16)
        kt_sc[...] = (kk_sc[...] * jnp.exp(al - aa)).astype(BF16)
        dec_sc[...] = jnp.exp(aa.reshape(nchunks, C, LANES)[:, C - 1, :])
        a2_sc[...] = aa * LOG2E
        trow = lax.broadcasted_iota(jnp.int32, (C, LANES), 0)

        def gen(c, _):
            r0 = pl.multiple_of(c * C, C)
            ac = a2_sc[pl.ds(r0, C), :]
            qc = qq_sc[pl.ds(r0, C), :]
            kc = kk_sc[pl.ds(r0, C), :]
            half = C // 2
            for s in range(C):
                if s < half:
                    dec = jnp.exp2(jnp.where(trow >= s, ac - ac[s:s + 1, :], NEG_BIG))
                    p = qc * (kc[s:s + 1, :] * dec)
                else:
                    dec = jnp.exp2(jnp.where(trow[half:] >= s, ac[half:] - ac[s:s + 1, :], NEG_BIG))
                    p = jnp.concatenate([jnp.zeros((half, LANES), F32), qc[half:] * (kc[s:s + 1, :] * dec)],
                                        axis=0)
                p_sc[pl.ds(r0, C), s * LANES:(s + 1) * LANES] = p.astype(BF16)
            return 0

        lax.fori_loop(0, nchunks, gen, 0)

        er = lax.broadcasted_iota(jnp.int32, (C * LANES, LANES), 0)
        ec = lax.broadcasted_iota(jnp.int32, (C * LANES, LANES), 1)
        emat = (ec == ((er & (LANES - 1)) // HEAD_DIM) * C + er // LANES).astype(BF16)
        rb = 256

        def red(i, _):
            r0 = pl.multiple_of(i * rb, rb)
            s_sc[pl.ds(r0, rb), :] = jnp.dot(p_sc[pl.ds(r0, rb), :], emat,
                                             preferred_element_type=F32).astype(BF16)
            return 0

        lax.fori_loop(0, S // rb, red, 0)

        unroll = 16
        assert nchunks % unroll == 0

        def scan(g, st):
            for u in range(unroll):
                c = g * unroll + u
                r0 = pl.multiple_of(c * C, C)
                st16_sc[c] = st.astype(BF16)
                upd = lax.dot_general(hi_ref[pl.ds(r0, C), :], kt_sc[pl.ds(r0, C), :],
                                      (((0,), (0,)), ((), ())), preferred_element_type=F32)
                st = st * dec_sc[pl.ds(c, 1), :] + jnp.where(same_head, upd, 0.0)
            return st

        lax.fori_loop(0, nchunks // unroll, scan, jnp.zeros((LANES, LANES), F32))

        def readout(g, _):
            for u in range(unroll):
                c = g * unroll + u
                r0 = pl.multiple_of(c * C, C)
                vc = hi_ref[pl.ds(r0, C), :]
                o_inter = lax.dot_general(qt_sc[pl.ds(r0, C), :], st16_sc[c],
                                          (((1,), (1,)), ((), ())), preferred_element_type=F32)
                v2 = jnp.concatenate([jnp.where(lane < HEAD_DIM, vc, jnp.zeros_like(vc)),
                                      jnp.where(lane >= HEAD_DIM, vc, jnp.zeros_like(vc))], axis=0)
                o_intra = jnp.dot(s_sc[pl.ds(r0, C), :][:, :2 * C], v2, preferred_element_type=F32)
                o_sc[pl.ds(r0, C), :] = o_inter + o_intra
            return 0

        lax.fori_loop(0, nchunks // unroll, readout, 0)

    o = o_sc[...]
    ones_head = jnp.where(same_head, 1.0, 0.0).astype(BF16)
    sq_hi, sq_lo = _bf16_pieces(o * o, 2)
    ms = (jnp.dot(sq_hi, ones_head, preferred_element_type=F32)
          + jnp.dot(sq_lo, ones_head, preferred_element_type=F32)) * (1.0 / HEAD_DIM)
    y = o * lax.rsqrt(ms + RMS_EPS) * nw_ref[...]
    o_ref[...] = (y * hg_ref[...].astype(F32)).astype(o_ref.dtype)


def _hgrn(hq, hf, hi, hg, lb_logits, norm_w):
    B, S, W = hq.shape
    npairs = W // LANES
    nrows = lb_logits.shape[0]
    seq = pl.BlockSpec((None, S, LANES), lambda b, p: (b, 0, p))
    return pl.pallas_call(
        _hgrn_kernel,
        out_shape=jax.ShapeDtypeStruct((B, S, W), BF16),
        grid=(B, npairs),
        in_specs=[seq, seq, seq, seq,
                  pl.BlockSpec((nrows, LANES), lambda b, p: (0, p)),
                  pl.BlockSpec((1, LANES), lambda b, p: (0, p))],
        out_specs=seq,
        scratch_shapes=[pltpu.VMEM((S, LANES), F32),
                        pltpu.VMEM((S, LANES), F32),
                        pltpu.VMEM((S, LANES), F32),
                        pltpu.VMEM((S, LANES), F32),
                        pltpu.VMEM((S, LANES), BF16),
                        pltpu.VMEM((S, LANES), BF16),
                        pltpu.VMEM((S, LANES), BF16),
                        pltpu.VMEM((S, LANES), BF16),
                        pltpu.VMEM((S, LANES), BF16),
                        pltpu.VMEM((S, HCHUNK * LANES), BF16),
                        pltpu.VMEM((S // HCHUNK, LANES, LANES), BF16),
                        pltpu.VMEM((S // HCHUNK, LANES), F32),
                        pltpu.VMEM((S // HBLOCK, LANES, LANES), BF16)],
        compiler_params=_cparams(("parallel", "parallel")),
    )(hq, hf, hi, hg, lb_logits, norm_w.reshape(1, W))


def _layer_norm(v, g, b):
    mu = jnp.mean(v, axis=-1, keepdims=True)
    d = v - mu
    var = jnp.mean(d * d, axis=-1, keepdims=True)
    return d * lax.rsqrt(var + LN_EPS) * g + b


def _bf16_bits(x):
    return (pltpu.bitcast(x, jnp.uint32) + jnp.uint32(0x8000)) & jnp.uint32(0xFFFF0000)


def _store_chunks(ref, val):
    n = ref.shape[0]
    for j in range(n):
        lo = _bf16_bits(val[:, j * LANES:(j + 1) * LANES]) >> 16
        hi = _bf16_bits(val[:, (j + n) * LANES:(j + n + 1) * LANES])
        ref[j] = pltpu.bitcast(lo | hi, F32)


def _load_chunks(ref):
    words = [pltpu.bitcast(ref[j], jnp.uint32) for j in range(ref.shape[0])]
    lo = [pltpu.bitcast(w << 16, F32) for w in words]
    hi = [pltpu.bitcast(w & jnp.uint32(0xFFFF0000), F32) for w in words]
    return jnp.concatenate(lo + hi, axis=1)


def _mix_kernel(yf_ref, oh_ref, gf_ref, gh_ref, x_ref, g1_ref, sc2_ref, sh2_ref,
                wuf_ref, wuh_ref, wo_ref, lg_ref, lbias_ref, wr_ref, br_ref,
                x1_ref, h2_ref, ri_ref, rt_ref, cnt_ref, carry_sc, *, alpha, ngroups, nper):
    first = (pl.program_id(0) == 0) & (pl.program_id(1) == 0)

    @pl.when(first)
    def _():
        carry_sc[...] = jnp.zeros_like(carry_sc)

    tm = x_ref.shape[0]
    yf = jnp.dot(yf_ref[...], wuf_ref[...], preferred_element_type=F32)
    yh = jnp.dot(oh_ref[...], wuh_ref[...], preferred_element_type=F32)
    merged = gf_ref[...].astype(F32) * yf + gh_ref[...].astype(F32) * yh
    y = jnp.dot(merged.astype(BF16), wo_ref[...], preferred_element_type=F32)
    x1 = _layer_norm(alpha * x_ref[...] + g1_ref[...] * y, lg_ref[...], lbias_ref[...])
    x1_ref[...] = x1
    h2 = x1 * (1.0 + sc2_ref[...]) + sh2_ref[...]
    _store_chunks(h2_ref, h2)

    h_hi, h_lo = _bf16_pieces(h2, 2)
    hh = jnp.dot(h_hi, wr_ref[...], preferred_element_type=F32)
    logits = (hh[:, :LANES] + hh[:, LANES:]
              + jnp.dot(h_lo, wr_ref[:, :LANES], preferred_element_type=F32)) + br_ref[...]
    lane = lax.broadcasted_iota(jnp.int32, (tm, LANES), 1)
    big = jnp.int32(1 << 20)

    def argmax_first(vals, mask):
        mx = jnp.max(jnp.where(mask, vals, -jnp.inf), axis=1, keepdims=True)
        idx = jnp.min(jnp.where(mask & (vals == mx), lane, big), axis=1, keepdims=True)
        return mx, idx

    gmask = lane < ngroups
    gmax = jnp.max(jnp.where(gmask, logits, -jnp.inf), axis=1, keepdims=True)
    gexp = jnp.where(gmask, jnp.exp(logits - gmax), 0.0)
    gprob = gexp / jnp.sum(gexp, axis=1, keepdims=True)
    g_w, g_idx = argmax_first(gprob, gmask)

    lo = ngroups + g_idx * nper
    emask = (lane >= lo) & (lane < lo + nper)
    emax = jnp.max(jnp.where(emask, logits, -jnp.inf), axis=1, keepdims=True)
    eexp = jnp.where(emask, jnp.exp(logits - emax), 0.0)
    eprob = eexp / jnp.sum(eexp, axis=1, keepdims=True)
    p0, i0 = argmax_first(eprob, emask)
    p1, i1 = argmax_first(eprob, emask & (lane != i0))
    den = p0 + p1
    w0 = p0 / den * g_w
    w1 = p1 / den * g_w
    e0 = i0 - ngroups
    e1 = i1 - ngroups

    oh = ((lane == e0) | (lane == e1)).astype(F32)
    r = lax.broadcasted_iota(jnp.int32, (tm, tm), 0)
    c = lax.broadcasted_iota(jnp.int32, (tm, tm), 1)
    strict_lower = (c < r).astype(BF16)
    before = jnp.dot(strict_lower, oh.astype(BF16), preferred_element_type=F32) + carry_sc[...]
    rank0 = jnp.sum(jnp.where(lane == e0, before, 0.0), axis=1, keepdims=True)
    rank1 = jnp.sum(jnp.where(lane == e1, before, 0.0), axis=1, keepdims=True)
    carry_sc[...] = carry_sc[...] + jnp.sum(oh, axis=0, keepdims=True)
    cnt_ref[...] = carry_sc[...]

    info = jnp.where(lane == 0, w0, 0.0)
    info = jnp.where(lane == 1, w1, info)
    info = jnp.where(lane == 2, e0.astype(F32), info)
    info = jnp.where(lane == 3, e1.astype(F32), info)
    info = jnp.where(lane == 4, rank0, info)
    info = jnp.where(lane == 5, rank1, info)
    ri_ref[...] = info
    rt_ref[...] = info.T[:ROW_TILE, :]


def _mix(yf, oh, gf, gh, x, g1, sc2, sh2, wuf, wuh, wo, ln_g, ln_b, wr, br, alpha, ngroups, nper, tm=512):
    B, S, D = x.shape
    W = yf.shape[2]
    tok = lambda w: pl.BlockSpec((None, tm, w), lambda b, i: (b, i, 0))
    vec = pl.BlockSpec((None, 1, D), lambda b, i: (b, 0, 0))
    full = lambda a: pl.BlockSpec(a.shape, lambda b, i: (0,) * a.ndim)
    return pl.pallas_call(
        functools.partial(_mix_kernel, alpha=alpha, ngroups=ngroups, nper=nper),
        out_shape=(jax.ShapeDtypeStruct((B, S, D), F32),
                   jax.ShapeDtypeStruct((D // WORD_LANES, B * S, LANES), F32),
                   jax.ShapeDtypeStruct((B, S, LANES), F32),
                   jax.ShapeDtypeStruct((ROW_TILE, B * S), F32),
                   jax.ShapeDtypeStruct((1, LANES), F32)),
        grid=(B, S // tm),
        in_specs=[tok(W), tok(W), tok(D), tok(D), tok(D), vec, vec, vec,
                  full(wuf), full(wuh), full(wo), full(ln_g), full(ln_b), full(wr), full(br)],
        out_specs=(tok(D),
                   pl.BlockSpec((D // WORD_LANES, tm, LANES), lambda b, i: (0, b * (S // tm) + i, 0)),
                   tok(LANES),
                   pl.BlockSpec((ROW_TILE, tm), lambda b, i: (0, b * (S // tm) + i)),
                   pl.BlockSpec((1, LANES), lambda b, i: (0, 0))),
        scratch_shapes=[pltpu.VMEM((1, LANES), F32)],
        compiler_params=_cparams(("arbitrary", "arbitrary")),
    )(yf, oh, gf, gh, x, g1, sc2, sh2, wuf, wuh, wo, ln_g, ln_b, wr, br)


def _sc_mesh():
    return plsc.VectorSubcoreMesh(core_axis_name="core", subcore_axis_name="subcore")


def _sc_pipeline(body, grid, in_specs, out_specs):
    return pltpu.emit_pipeline(body, grid=grid, in_specs=in_specs, out_specs=out_specs,
                               core_axis_name=("core", "subcore"),
                               dimension_semantics=(pltpu.PARALLEL,) * len(grid))


def _sc_scatter_rows(src, rows_a, rows_b, n_out):
    nj, t = rows_a.shape
    win = SC_WINDOW
    nc = t // win

    @pl.kernel(out_type=jax.ShapeDtypeStruct((n_out, LANES), src.dtype), mesh=_sc_mesh(), scratch_types=[])
    def scatter(x_hbm, a_hbm, b_hbm, o_hbm):
        def body(x_vmem, a_vmem, b_vmem):
            pltpu.sync_copy(x_vmem, o_hbm.at[a_vmem.at[0]])
            pltpu.sync_copy(x_vmem, o_hbm.at[b_vmem.at[0]])

        idx = pl.BlockSpec((1, win), lambda j, c: (j, c))
        _sc_pipeline(body, (nj, nc), [pl.BlockSpec((win, LANES), lambda j, c: (j * nc + c, 0)), idx, idx],
                     [])(x_hbm, a_hbm, b_hbm)

    return scatter(src, rows_a, rows_b)


def _sc_gather_rows(table, rows):
    nr, t = rows.shape
    win = SC_WINDOW
    nc = t // win

    @pl.kernel(out_type=jax.ShapeDtypeStruct((nr * t, LANES), table.dtype), mesh=_sc_mesh(), scratch_types=[])
    def gather(x_hbm, i_hbm, o_hbm):
        def body(i_vmem, o_vmem):
            pltpu.sync_copy(x_hbm.at[i_vmem.at[0]], o_vmem)

        _sc_pipeline(body, (nr, nc), [pl.BlockSpec((1, win), lambda r, c: (r, c))],
                     [pl.BlockSpec((win, LANES), lambda r, c: (r * nc + c, 0))])(i_hbm, o_hbm)

    return gather(table, rows)


def _experts_kernel(te_ref, tn_ref, tb_ref, x_ref, wg_ref, wu_ref, wd_ref, o_ref):
    del tb_ref
    nrows = tn_ref[pl.program_id(0)]

    @pl.when(nrows > 0)
    def _():
        x = _load_chunks(x_ref)
        x = jnp.where(lax.broadcasted_iota(jnp.int32, x.shape, 0) < nrows, x, 0.0).astype(BF16)
        g = jnp.dot(x, wg_ref[...].astype(BF16), preferred_element_type=F32)
        u = jnp.dot(x, wu_ref[...].astype(BF16), preferred_element_type=F32)
        hid = (_silu(g) * u).astype(BF16)
        _store_chunks(o_ref, jnp.dot(hid, wd_ref[...].astype(BF16), preferred_element_type=F32))


def _experts(tile_expert, tile_rows, tile_block, xs, wg, wu, wd, tm):
    E, D, FF = wg.shape
    dt = D // WORD_LANES
    ntiles = tile_expert.shape[0]
    rows = pl.BlockSpec((dt, tm, LANES), lambda i, te, tn, tb: (0, tb[i], 0))
    grid_spec = pltpu.PrefetchScalarGridSpec(
        num_scalar_prefetch=3,
        grid=(ntiles,),
        in_specs=[rows,
                  pl.BlockSpec((None, D, FF), lambda i, te, tn, tb: (te[i], 0, 0)),
                  pl.BlockSpec((None, D, FF), lambda i, te, tn, tb: (te[i], 0, 0)),
                  pl.BlockSpec((None, FF, D), lambda i, te, tn, tb: (te[i], 0, 0))],
        out_specs=rows,
    )
    return pl.pallas_call(
        _experts_kernel,
        out_shape=jax.ShapeDtypeStruct((dt, ntiles * tm, LANES), F32),
        grid_spec=grid_spec,
        compiler_params=_cparams(("arbitrary",)),
    )(tile_expert, tile_rows, tile_block, xs, wg, wu, wd)


def _combine_kernel(yg_ref, x1_ref, ri_ref, g2_ref, lg_ref, lb_ref, o_ref, *, alpha):
    ri = ri_ref[...]
    y = ri[:, 0:1] * _load_chunks(yg_ref.at[0]) + ri[:, 1:2] * _load_chunks(yg_ref.at[1])
    o_ref[...] = _layer_norm(alpha * x1_ref[...] + g2_ref[...] * y, lg_ref[...], lb_ref[...])


def _combine(yg, x1, rinfo, g2, ln_g, ln_b, alpha, tm=1024):
    B, S, D = x1.shape
    nb = S // tm
    return pl.pallas_call(
        functools.partial(_combine_kernel, alpha=alpha),
        out_shape=jax.ShapeDtypeStruct((B, S, D), F32),
        grid=(B, nb),
        in_specs=[pl.BlockSpec((2, D // WORD_LANES, tm, LANES), lambda b, i: (0, 0, b * nb + i, 0)),
                  pl.BlockSpec((None, tm, D), lambda b, i: (b, i, 0)),
                  pl.BlockSpec((None, tm, LANES), lambda b, i: (b, i, 0)),
                  pl.BlockSpec((None, 1, D), lambda b, i: (b, 0, 0)),
                  pl.BlockSpec((1, D), lambda b, i: (0, 0)),
                  pl.BlockSpec((1, D), lambda b, i: (0, 0))],
        out_specs=pl.BlockSpec((None, tm, D), lambda b, i: (b, i, 0)),
        compiler_params=_cparams(("parallel", "parallel")),
    )(yg, x1, rinfo, g2, ln_g, ln_b)


def kernel(x, c, w_ada, b_ada, w_in, b_fox_forget, hgrn_lb_logits, hgrn_norm_w, w_up_fox, w_up_hgrn, w_out,
           ln1_g, ln1_b, w_router_group, b_router_group, w_router_expert, b_router_expert,
           w_expert_gate, w_expert_up, w_expert_down, ln2_g, ln2_b):
    B, S, D = x.shape
    depth = w_ada.shape[0]
    assert depth == 1, "single-layer block"
    fox_heads = b_fox_forget.shape[1]
    fox_w = fox_heads * HEAD_DIM
    hgrn_w = hgrn_norm_w.shape[1]
    ngroups = w_router_group.shape[2]
    nexp = w_router_expert.shape[2]
    nper = nexp // ngroups
    alpha = (2 * depth) ** 0.25
    T = B * S

    ada = _ada(c, w_ada[0], b_ada[0])
    sh1, sc1, g1, sh2, sc2, g2 = [a.reshape(B, 1, D) for a in jnp.split(ada, 6, axis=-1)]

    wi = w_in[0]
    o_ff = 3 * fox_w
    w_fox = jnp.pad(wi[:, :o_ff + fox_heads], ((0, 0), (0, LANES - fox_heads))).astype(BF16)
    w_rest = wi[:, o_ff + fox_heads:].astype(BF16)
    widths = [fox_w, fox_w, fox_w, LANES, hgrn_w, hgrn_w, hgrn_w, hgrn_w, D, D]
    segs, off = [], 0
    for n, w in enumerate(widths):
        if n == 4:
            off = 0
        segs.append((off, off + w))
        off += w
    fq, fk, fv, ffp, hq, hf, hi, hg, gf, gh = _inproj(x, sc1, sh1, w_fox, w_rest, segs)

    bias_p = jnp.zeros((1, LANES), F32).at[0, :fox_heads].set(b_fox_forget[0])
    cum = _foxcum(ffp, bias_p)
    y_fox = _fox(fq, fk, fv, cum)

    o_h = _hgrn(hq, hf, hi, hg, hgrn_lb_logits, hgrn_norm_w[0])

    wr = jnp.zeros((D, LANES), F32).at[:, :ngroups].set(w_router_group[0]).at[:, ngroups:ngroups + nexp].set(
        w_router_expert[0])
    wr_hi = lax.bitcast_convert_type(lax.bitcast_convert_type(wr, jnp.uint32) & jnp.uint32(0xFFFF0000), F32)
    wr = jnp.concatenate([wr_hi.astype(BF16), (wr - wr_hi).astype(BF16)], axis=1)
    br = jnp.zeros((1, LANES), F32).at[0, :ngroups].set(b_router_group[0]).at[0, ngroups:ngroups + nexp].set(
        b_router_expert[0])
    x1, h2, rinfo, fields, counts = _mix(
        y_fox, o_h, gf, gh, x, g1, sc2, sh2,
        w_up_fox[0].astype(BF16), w_up_hgrn[0].astype(BF16), w_out[0].astype(BF16),
        ln1_g[0].reshape(1, D), ln1_b[0].reshape(1, D), wr, br, alpha, ngroups, nper)

    tm_e = 512
    dt = D // WORD_LANES
    ntiles = (2 * T) // tm_e + nexp
    nslots = ntiles * tm_e
    cnt = counts[0, :nexp].astype(jnp.int32)
    padded = ((cnt + tm_e - 1) // tm_e) * tm_e
    ends = jnp.cumsum(padded)
    starts = ends - padded
    eid = fields[2:4].astype(jnp.int32)
    rank = fields[4:6].astype(jnp.int32)
    first = jnp.sum(jnp.where(eid[None] == jnp.arange(nexp, dtype=jnp.int32)[:, None, None],
                              starts[:, None, None], 0), axis=0)
    pos = first + rank
    tile_start = jnp.arange(ntiles, dtype=jnp.int32) * tm_e
    tile_block = jnp.minimum(jnp.arange(ntiles, dtype=jnp.int32), ends[-1] // tm_e - 1)
    tile_expert = jnp.minimum(jnp.sum((tile_start[:, None] >= ends[None, :]).astype(jnp.int32), axis=1), nexp - 1)
    tile_rows = jnp.clip(starts[tile_expert] + cnt[tile_expert] - tile_start, 0, tm_e)
    tile_expert = tile_expert[tile_block]
    rows = pos[:, None, :] + (jnp.arange(dt, dtype=jnp.int32) * nslots)[None, :, None]

    xs = _sc_scatter_rows(h2.reshape(dt * T, LANES), rows[0], rows[1], dt * nslots)
    ys = _experts(tile_expert, tile_rows, tile_block, xs.reshape(dt, nslots, LANES),
                  w_expert_gate[0], w_expert_up[0], w_expert_down[0], tm_e)
    yg = _sc_gather_rows(ys.reshape(dt * nslots, LANES), rows.reshape(2 * dt, T))
    return _combine(yg.reshape(2, dt, T, LANES), x1, rinfo, g2,
                    ln2_g[0].reshape(1, D), ln2_b[0].reshape(1, D), alpha)
```

```python
import functools

import jax
import jax.numpy as jnp
from jax import lax
from jax.experimental import pallas as pl
from jax.experimental.pallas import tpu as pltpu
from jax.experimental.pallas import tpu_sc as plsc

F32 = jnp.float32
BF16 = jnp.bfloat16

LANES = 128
HEAD_DIM = 64
LN_EPS = 1e-5
RMS_EPS = 1e-6
LOG2E = 1.4426950408889634
NEG_BIG = -1e30
HCHUNK = 16
HBLOCK = 64
HGRN_SAFE_EXP = 60.0
ROW_TILE = 8
WORD_LANES = 2 * LANES
SC_WINDOW = 256
VMEM_LIMIT = 56 * 1024 * 1024


def _cparams(sem, vmem=VMEM_LIMIT):
    return pltpu.CompilerParams(dimension_semantics=sem, vmem_limit_bytes=vmem)


def _sigmoid(x):
    return 0.5 * jnp.tanh(0.5 * x) + 0.5


def _silu(x):
    return x * _sigmoid(x)


def _bf16_pieces(x, n):
    pieces = []
    for _ in range(n):
        top = pltpu.bitcast(pltpu.bitcast(x, jnp.uint32) & jnp.uint32(0xFFFF0000), F32)
        pieces.append(top.astype(BF16))
        x = x - top
    return pieces


def _exact_matrix_dot(m, x):
    r = jnp.dot(m, jnp.concatenate(_bf16_pieces(x, 3), axis=1), preferred_element_type=F32)
    return r[:, :LANES] + r[:, LANES:2 * LANES] + r[:, 2 * LANES:]


def _ada_kernel(c_ref, w_ref, b_ref, o_ref):
    c_hi, c_lo = _bf16_pieces(_silu(c_ref[...]), 2)
    w_hi, w_lo = _bf16_pieces(w_ref[...], 2)
    o_ref[...] = (jnp.dot(c_hi, w_hi, preferred_element_type=F32) + jnp.dot(c_hi, w_lo, preferred_element_type=F32)
                  + jnp.dot(c_lo, w_hi, preferred_element_type=F32)) + b_ref[...]


def _ada(c, w_ada, b_ada):
    B, D = c.shape
    N = w_ada.shape[1]
    tn = 1024
    return pl.pallas_call(
        _ada_kernel,
        out_shape=jax.ShapeDtypeStruct((B, N), F32),
        grid=(N // tn,),
        in_specs=[pl.BlockSpec((B, D), lambda j: (0, 0)),
                  pl.BlockSpec((D, tn), lambda j: (0, j)),
                  pl.BlockSpec((1, tn), lambda j: (0, j))],
        out_specs=pl.BlockSpec((B, tn), lambda j: (0, j)),
        compiler_params=_cparams(("arbitrary",)),
    )(c, w_ada, b_ada.reshape(1, N))


N_FOX_SEGS = 4
SILU_SEGS = (4, 7)
SIGMOID_SEGS = (8, 9)


def _inproj_kernel(x_ref, sc_ref, sh_ref, wf_ref, wr_ref,
                   fq_ref, fk_ref, fv_ref, ff_ref, hq_ref, hf_ref, hi_ref, hg_ref, gf_ref, gh_ref,
                   *, segs, q_scale):
    h = (x_ref[...] * (1.0 + sc_ref[...]) + sh_ref[...]).astype(BF16)
    outs = (fq_ref, fk_ref, fv_ref, ff_ref, hq_ref, hf_ref, hi_ref, hg_ref, gf_ref, gh_ref)
    for idx, (o_ref, (a, b)) in enumerate(zip(outs, segs)):
        w_ref = wf_ref if idx < N_FOX_SEGS else wr_ref
        r = jnp.dot(h, w_ref[:, a:b], preferred_element_type=F32)
        if idx == 0:
            r = r * q_scale
        elif idx in SILU_SEGS:
            r = _silu(r)
        elif idx in SIGMOID_SEGS:
            r = _sigmoid(r)
        o_ref[...] = r.astype(o_ref.dtype)


def _inproj(x, sc1, sh1, w_fox, w_rest, segs, tm=256):
    B, S, D = x.shape
    widths = [b - a for a, b in segs]
    dtypes = [BF16, BF16, BF16, F32, BF16, F32, BF16, BF16, BF16, BF16]
    out_shape = tuple(jax.ShapeDtypeStruct((B, S, w), dt) for w, dt in zip(widths, dtypes))
    out_specs = tuple(pl.BlockSpec((None, tm, w), lambda b, i: (b, i, 0)) for w in widths)
    vec = pl.BlockSpec((None, 1, D), lambda b, i: (b, 0, 0))
    return pl.pallas_call(
        functools.partial(_inproj_kernel, segs=tuple(segs), q_scale=HEAD_DIM ** -0.5 * LOG2E),
        out_shape=out_shape,
        grid=(B, S // tm),
        in_specs=[pl.BlockSpec((None, tm, D), lambda b, i: (b, i, 0)), vec, vec,
                  pl.BlockSpec(w_fox.shape, lambda b, i: (0, 0)),
                  pl.BlockSpec(w_rest.shape, lambda b, i: (0, 0))],
        out_specs=out_specs,
        compiler_params=_cparams(("parallel", "parallel")),
    )(x, sc1, sh1, w_fox, w_rest)


def _foxcum_kernel(ff_ref, b_ref, o_ref, *, blk):
    S = ff_ref.shape[0]
    r = lax.broadcasted_iota(jnp.int32, (blk, blk), 0)
    c = lax.broadcasted_iota(jnp.int32, (blk, blk), 1)
    lower = jnp.where(r >= c, 1.0, 0.0).astype(BF16)
    carry = jnp.zeros((1, LANES), F32)
    for j in range(S // blk):
        z = ff_ref[j * blk:(j + 1) * blk, :] + b_ref[...]
        lf = jnp.minimum(z, 0.0) - jnp.log(1.0 + jnp.exp(-jnp.abs(z)))
        cum = _exact_matrix_dot(lower, lf) + carry
        o_ref[j * blk:(j + 1) * blk, :] = cum * LOG2E
        carry = cum[blk - 1:blk, :]


def _foxcum(ffp, bias_p, blk=256):
    B, S, _ = ffp.shape
    return pl.pallas_call(
        functools.partial(_foxcum_kernel, blk=blk),
        out_shape=jax.ShapeDtypeStruct((B, S, LANES), F32),
        grid=(B,),
        in_specs=[pl.BlockSpec((None, S, LANES), lambda b: (b, 0, 0)),
                  pl.BlockSpec((1, LANES), lambda b: (0, 0))],
        out_specs=pl.BlockSpec((None, S, LANES), lambda b: (b, 0, 0)),
        compiler_params=_cparams(("parallel",)),
    )(ffp, bias_p)


NCUM = 3


def _fox_kernel(q_ref, k_ref, v_ref, c_ref, o_ref, ka_sc, kb_sc, va_sc, vb_sc, *, tq, tk):
    p = pl.program_id(1)
    qi = pl.program_id(2)
    S = k_ref.shape[0]

    @pl.when(qi == 0)
    def _():
        lane = lax.broadcasted_iota(jnp.int32, (S, LANES), 1)
        rr = lax.broadcasted_iota(jnp.int32, (LANES, LANES), 0)
        cc = lax.broadcasted_iota(jnp.int32, (LANES, LANES), 1)
        rest = c_ref[...]
        placed = jnp.zeros((S, LANES), F32)
        for i in range(NCUM):
            piece = rest.astype(BF16)
            rest = rest - piece.astype(F32)
            sel = ((rr == 2 * p) & (cc == HEAD_DIM + i)) | ((rr == 2 * p + 1) & (cc == i))
            placed = placed + jnp.dot(piece, jnp.where(sel, 1.0, 0.0).astype(BF16), preferred_element_type=F32)
        k2 = k_ref[...].astype(F32)
        ka_sc[...] = jnp.where(lane < HEAD_DIM, k2, -placed).astype(BF16)
        kb_sc[...] = jnp.where(lane >= HEAD_DIM, k2, -placed).astype(BF16)
        vt = v_ref[...].astype(F32).T
        row = lax.broadcasted_iota(jnp.int32, (LANES, S), 0)
        va_sc[...] = jnp.where(row < HEAD_DIM, vt, jnp.where(row == HEAD_DIM, 1.0, 0.0)).astype(BF16)
        vb_sc[...] = jnp.where(row >= HEAD_DIM, vt, jnp.where(row == 0, 1.0, 0.0)).astype(BF16)

    q2 = q_ref[...].astype(F32)
    qlane = lax.broadcasted_iota(jnp.int32, (tq, LANES), 1)
    qa = jnp.where(qlane < HEAD_DIM, q2, jnp.where(qlane < HEAD_DIM + NCUM, 1.0, 0.0)).astype(BF16)
    qb = jnp.where(qlane >= HEAD_DIM, q2, jnp.where(qlane < NCUM, 1.0, 0.0)).astype(BF16)
    nsub = tq // tk

    def block(k0, carry, diag_off):
        q0 = 0 if diag_off is None else diag_off
        out = []
        for ksc, vsc, qh, (m, acc) in ((ka_sc, va_sc, qa, carry[:2]), (kb_sc, vb_sc, qb, carry[2:])):
            st = lax.dot_general(ksc[pl.ds(k0, tk), :], qh[q0:, :], (((1,), (1,)), ((), ())),
                                 preferred_element_type=F32)
            if diag_off is not None:
                st = jnp.where(lax.broadcasted_iota(jnp.int32, st.shape, 0)
                               <= lax.broadcasted_iota(jnp.int32, st.shape, 1), st, NEG_BIG)
            m_old = m[:, q0:]
            m_new = jnp.maximum(m_old, jnp.max(st, axis=0, keepdims=True))
            pt = jnp.exp2(st - m_new).astype(BF16)
            acc_new = (jnp.exp2(m_old - m_new) * acc[:, q0:]
                       + jnp.dot(vsc[:, pl.ds(k0, tk)], pt, preferred_element_type=F32))
            if q0:
                m_new = jnp.concatenate([m[:, :q0], m_new], axis=1)
                acc_new = jnp.concatenate([acc[:, :q0], acc_new], axis=1)
            out += [m_new, acc_new]
        return tuple(out)

    def group(j, carry):
        k0 = pl.multiple_of(j * (nsub * tk), nsub * tk)
        for u in range(nsub):
            carry = block(k0 + u * tk, carry, None)
        return carry

    m0 = jnp.full((1, tq), NEG_BIG, F32)
    a0 = jnp.zeros((LANES, tq), F32)
    carry = lax.fori_loop(0, qi, group, (m0, a0, m0, a0))
    for d in range(nsub):
        carry = block(pl.multiple_of(qi * tq + d * tk, tk), carry, d * tk)
    _, aa, _, ab = carry
    row = lax.broadcasted_iota(jnp.int32, (LANES, tq), 0)
    ot = jnp.where(row < HEAD_DIM, aa * (1.0 / aa[HEAD_DIM:HEAD_DIM + 1, :]), ab * (1.0 / ab[0:1, :]))
    o_ref[...] = ot.T.astype(o_ref.dtype)


def _fox(fq, fk, fv, cum, tq=2048, tk=512):
    B, S, W = fq.shape
    tq = min(tq, S)
    assert tq % tk == 0 and S % tq == 0
    npairs = W // LANES
    return pl.pallas_call(
        functools.partial(_fox_kernel, tq=tq, tk=tk),
        out_shape=jax.ShapeDtypeStruct((B, S, W), BF16),
        grid=(B, npairs, S // tq),
        in_specs=[pl.BlockSpec((None, tq, LANES), lambda b, p, i: (b, i, p)),
                  pl.BlockSpec((None, S, LANES), lambda b, p, i: (b, 0, p)),
                  pl.BlockSpec((None, S, LANES), lambda b, p, i: (b, 0, p)),
                  pl.BlockSpec((None, S, LANES), lambda b, p, i: (b, 0, 0))],
        out_specs=pl.BlockSpec((None, tq, LANES), lambda b, p, i: (b, i, p)),
        scratch_shapes=[pltpu.VMEM((S, LANES), BF16), pltpu.VMEM((S, LANES), BF16),
                        pltpu.VMEM((LANES, S), BF16), pltpu.VMEM((LANES, S), BF16)],
        compiler_params=_cparams(("parallel", "parallel", "arbitrary")),
    )(fq, fk, fv, cum)


def _hgrn_kernel(hq_ref, hf_ref, hi_ref, hg_ref, lb_ref, nw_ref, o_ref,
                 b_sc, kk_sc, qq_sc, o_sc, w1_sc, w2_sc, w3_sc, w4_sc, w5_sc,
                 p_sc, st16_sc, dec_sc, st64_sc):
    S = hq_ref.shape[0]
    C = HCHUNK
    nchunks = S // C
    BLK = HBLOCK
    nblk = S // BLK

    lg = lb_ref[...]
    e = jnp.exp(lg - jnp.max(lg, axis=0, keepdims=True))
    lb = e[0:1, :] / jnp.sum(e, axis=0, keepdims=True)

    f = lb + (1.0 - lb) * (1.0 / (1.0 + jnp.exp(-hf_ref[...])))
    lf = jnp.log(f)
    kk_sc[...] = 1.0 - f
    qq_sc[...] = hq_ref[...].astype(F32)

    row = lax.broadcasted_iota(jnp.int32, (S, LANES), 0)
    rb = 4 * BLK
    tr = lax.broadcasted_iota(jnp.int32, (rb, rb), 0)
    tc = lax.broadcasted_iota(jnp.int32, (rb, rb), 1)
    tri = jnp.where(((tr & -BLK) == (tc & -BLK)) & (tc <= tr), 1.0, 0.0).astype(BF16)
    lf3 = jnp.concatenate(_bf16_pieces(lf, 3), axis=1)
    for j in range(S // rb):
        c3 = jnp.dot(tri, lf3[j * rb:(j + 1) * rb, :], preferred_element_type=F32)
        b_sc[j * rb:(j + 1) * rb, :] = c3[:, :LANES] + c3[:, LANES:2 * LANES] + c3[:, 2 * LANES:]
    safe = jnp.max(-b_sc[...].reshape(nblk, BLK, LANES)[:, BLK - 1, :]) <= HGRN_SAFE_EXP

    lane = lax.broadcasted_iota(jnp.int32, (C, LANES), 1)
    sr = lax.broadcasted_iota(jnp.int32, (LANES, LANES), 0)
    scn = lax.broadcasted_iota(jnp.int32, (LANES, LANES), 1)
    same_head = (sr // HEAD_DIM) == (scn // HEAD_DIM)

    @pl.when(safe)
    def _factorised():
        qh_sc, kh_sc, ke_sc, qd_sc, k2_sc = w1_sc, w2_sc, w3_sc, w4_sc, w5_sc
        SB = 2 * BLK
        nsb = S // SB
        bb = b_sc[...]
        dblk = jnp.exp(bb.reshape(nblk, BLK, LANES)[:, BLK - 1:BLK, :])
        dfull = jnp.broadcast_to(dblk, (nblk, BLK, LANES)).reshape(S, LANES)
        second = (row & BLK) != 0
        d_prev = pltpu.roll(dfull, BLK, axis=0)
        d_next = pltpu.roll(dfull, S - BLK, axis=0)
        qh = qq_sc[...] * jnp.exp(bb)
        qh_sc[...] = qh.astype(BF16)
        qd_sc[...] = (qh * jnp.where(second, d_prev, 1.0)).astype(BF16)
        kh = kk_sc[...] * jnp.exp(-bb)
        kh_sc[...] = kh.astype(BF16)
        ke = kh * dfull
        ke_sc[...] = ke.astype(BF16)
        k2_sc[...] = (ke * jnp.where(second, 1.0, d_next)).astype(BF16)
        d3 = dfull.reshape(nsb, SB, LANES)
        dec_sc[pl.ds(0, nsb), :] = d3[:, 0, :] * d3[:, BLK, :]
        unroll = min(16, nsb)
        assert nsb % unroll == 0
        tn = (((0,), (0,)), ((), ()))
        nt = (((1,), (1,)), ((), ()))

        def scan(g, st):
            for u in range(unroll):
                i = g * unroll + u
                r0 = pl.multiple_of(i * SB, SB)
                st64_sc[i] = st.astype(BF16)
                upd = lax.dot_general(hi_ref[pl.ds(r0, SB), :], k2_sc[pl.ds(r0, SB), :], tn,
                                      preferred_element_type=F32)
                st = st * dec_sc[pl.ds(i, 1), :] + jnp.where(same_head, upd, 0.0)
            return st

        lax.fori_loop(0, nsb // unroll, scan, jnp.zeros((LANES, LANES), F32))

        r = lax.broadcasted_iota(jnp.int32, (2 * SB, 2 * SB), 0)
        c = lax.broadcasted_iota(jnp.int32, (2 * SB, 2 * SB), 1)
        t = r & (SB - 1)
        visible = (((c < SB) & ((t & BLK) == (c & BLK)) & ((t & (BLK - 1)) >= (c & (BLK - 1))))
                   | ((c >= SB) & (c < SB + BLK) & (t >= BLK)))
        plane = lax.broadcasted_iota(jnp.int32, (SB, LANES), 1)
        pad = jnp.zeros((BLK, LANES), BF16)

        def readout(g, _):
            for u in range(unroll):
                i = g * unroll + u
                r0 = pl.multiple_of(i * SB, SB)
                vb = hi_ref[pl.ds(r0, SB), :]
                qh2 = qh_sc[pl.ds(r0, SB), :]
                q2 = jnp.concatenate([jnp.where(plane < HEAD_DIM, qh2, jnp.zeros_like(qh2)),
                                      jnp.where(plane >= HEAD_DIM, qh2, jnp.zeros_like(qh2))], axis=0)
                kext = jnp.concatenate([kh_sc[pl.ds(r0, SB), :], ke_sc[pl.ds(r0, BLK), :], pad], axis=0)
                vext = jnp.concatenate([vb, vb[:BLK], pad], axis=0)
                sc = lax.dot_general(q2, kext, nt, preferred_element_type=F32)
                sc = jnp.where(visible, sc, 0.0).astype(BF16)
                out = jnp.dot(sc, vext, preferred_element_type=F32)
                o_inter = lax.dot_general(qd_sc[pl.ds(r0, SB), :], st64_sc[i], nt, preferred_element_type=F32)
                o_sc[pl.ds(r0, SB), :] = jnp.where(plane < HEAD_DIM, out[:SB], out[SB:]) + o_inter
            return 0

        lax.fori_loop(0, nsb // unroll, readout, 0)

    @pl.when(jnp.logical_not(safe))
    def _direct():
        qt_sc, kt_sc, s_sc, a2_sc = w1_sc, w2_sc, w3_sc, b_sc
        bb = b_sc[...]
        cl = jnp.broadcast_to(bb.reshape(nchunks, C, LANES)[:, C - 1:C, :], (nchunks, C, LANES)).reshape(S, LANES)
        aa = bb - jnp.where((row & (BLK - 1)) >= C, pltpu.roll(cl, C, axis=0), 0.0)
        al = jnp.broadcast_to(aa.reshape(nchunks, C, LANES)[:, C - 1:C, :], (nchunks, C, LANES)).reshape(S, LANES)
        qt_sc[...] = (qq_sc[...] * jnp.exp(aa)).astype(BF16)
        kt_sc[...] = (kk_sc[...] * jnp.exp(al - aa)).astype(BF16)
        dec_sc[...] = jnp.exp(aa.reshape(nchunks, C, LANES)[:, C - 1, :])
        a2_sc[...] = aa * LOG2E
        trow = lax.broadcasted_iota(jnp.int32, (C, LANES), 0)

        def gen(c, _):
            r0 = pl.multiple_of(c * C, C)
            ac = a2_sc[pl.ds(r0, C), :]
            qc = qq_sc[pl.ds(r0, C), :]
            kc = kk_sc[pl.ds(r0, C), :]
            half = C // 2
            for s in range(C):
                if s < half:
                    dec = jnp.exp2(jnp.where(trow >= s, ac - ac[s:s + 1, :], NEG_BIG))
                    p = qc * (kc[s:s + 1, :] * dec)
                else:
                    dec = jnp.exp2(jnp.where(trow[half:] >= s, ac[half:] - ac[s:s + 1, :], NEG_BIG))
                    p = jnp.concatenate([jnp.zeros((half, LANES), F32), qc[half:] * (kc[s:s + 1, :] * dec)],
                                        axis=0)
                p_sc[pl.ds(r0, C), s * LANES:(s + 1) * LANES] = p.astype(BF16)
            return 0

        lax.fori_loop(0, nchunks, gen, 0)

        er = lax.broadcasted_iota(jnp.int32, (C * LANES, LANES), 0)
        ec = lax.broadcasted_iota(jnp.int32, (C * LANES, LANES), 1)
        emat = (ec == ((er & (LANES - 1)) // HEAD_DIM) * C + er // LANES).astype(BF16)
        rb = 256

        def red(i, _):
            r0 = pl.multiple_of(i * rb, rb)
            s_sc[pl.ds(r0, rb), :] = jnp.dot(p_sc[pl.ds(r0, rb), :], emat,
                                             preferred_element_type=F32).astype(BF16)
            return 0

        lax.fori_loop(0, S // rb, red, 0)

        unroll = 16
        assert nchunks % unroll == 0

        def scan(g, st):
            for u in range(unroll):
                c = g * unroll + u
                r0 = pl.multiple_of(c * C, C)
                st16_sc[c] = st.astype(BF16)
                upd = lax.dot_general(hi_ref[pl.ds(r0, C), :], kt_sc[pl.ds(r0, C), :],
                                      (((0,), (0,)), ((), ())), preferred_element_type=F32)
                st = st * dec_sc[pl.ds(c, 1), :] + jnp.where(same_head, upd, 0.0)
            return st

        lax.fori_loop(0, nchunks // unroll, scan, jnp.zeros((LANES, LANES), F32))

        def readout(g, _):
            for u in range(unroll):
                c = g * unroll + u
                r0 = pl.multiple_of(c * C, C)
                vc = hi_ref[pl.ds(r0, C), :]
                o_inter = lax.dot_general(qt_sc[pl.ds(r0, C), :], st16_sc[c],
                                          (((1,), (1,)), ((), ())), preferred_element_type=F32)
                v2 = jnp.concatenate([jnp.where(lane < HEAD_DIM, vc, jnp.zeros_like(vc)),
                                      jnp.where(lane >= HEAD_DIM, vc, jnp.zeros_like(vc))], axis=0)
                o_intra = jnp.dot(s_sc[pl.ds(r0, C), :][:, :2 * C], v2, preferred_element_type=F32)
                o_sc[pl.ds(r0, C), :] = o_inter + o_intra
            return 0

        lax.fori_loop(0, nchunks // unroll, readout, 0)

    o = o_sc[...]
    ones_head = jnp.where(same_head, 1.0, 0.0).astype(BF16)
    sq_hi, sq_lo = _bf16_pieces(o * o, 2)
    ms = (jnp.dot(sq_hi, ones_head, preferred_element_type=F32)
          + jnp.dot(sq_lo, ones_head, preferred_element_type=F32)) * (1.0 / HEAD_DIM)
    y = o * lax.rsqrt(ms + RMS_EPS) * nw_ref[...]
    o_ref[...] = (y * hg_ref[...].astype(F32)).astype(o_ref.dtype)


def _hgrn(hq, hf, hi, hg, lb_logits, norm_w):
    B, S, W = hq.shape
    npairs = W // LANES
    nrows = lb_logits.shape[0]
    seq = pl.BlockSpec((None, S, LANES), lambda b, p: (b, 0, p))
    return pl.pallas_call(
        _hgrn_kernel,
        out_shape=jax.ShapeDtypeStruct((B, S, W), BF16),
        grid=(B, npairs),
        in_specs=[seq, seq, seq, seq,
                  pl.BlockSpec((nrows, LANES), lambda b, p: (0, p)),
                  pl.BlockSpec((1, LANES), lambda b, p: (0, p))],
        out_specs=seq,
        scratch_shapes=[pltpu.VMEM((S, LANES), F32),
                        pltpu.VMEM((S, LANES), F32),
                        pltpu.VMEM((S, LANES), F32),
                        pltpu.VMEM((S, LANES), F32),
                        pltpu.VMEM((S, LANES), BF16),
                        pltpu.VMEM((S, LANES), BF16),
                        pltpu.VMEM((S, LANES), BF16),
                        pltpu.VMEM((S, LANES), BF16),
                        pltpu.VMEM((S, LANES), BF16),
                        pltpu.VMEM((S, HCHUNK * LANES), BF16),
                        pltpu.VMEM((S // HCHUNK, LANES, LANES), BF16),
                        pltpu.VMEM((S // HCHUNK, LANES), F32),
                        pltpu.VMEM((S // HBLOCK, LANES, LANES), BF16)],
        compiler_params=_cparams(("parallel", "parallel")),
    )(hq, hf, hi, hg, lb_logits, norm_w.reshape(1, W))


def _layer_norm(v, g, b):
    mu = jnp.mean(v, axis=-1, keepdims=True)
    d = v - mu
    var = jnp.mean(d * d, axis=-1, keepdims=True)
    return d * lax.rsqrt(var + LN_EPS) * g + b


def _bf16_bits(x):
    return (pltpu.bitcast(x, jnp.uint32) + jnp.uint32(0x8000)) & jnp.uint32(0xFFFF0000)


def _store_chunks(ref, val):
    n = ref.shape[0]
    for j in range(n):
        lo = _bf16_bits(val[:, j * LANES:(j + 1) * LANES]) >> 16
        hi = _bf16_bits(val[:, (j + n) * LANES:(j + n + 1) * LANES])
        ref[j] = pltpu.bitcast(lo | hi, F32)


def _load_chunks(ref):
    words = [pltpu.bitcast(ref[j], jnp.uint32) for j in range(ref.shape[0])]
    lo = [pltpu.bitcast(w << 16, F32) for w in words]
    hi = [pltpu.bitcast(w & jnp.uint32(0xFFFF0000), F32) for w in words]
    return jnp.concatenate(lo + hi, axis=1)


def _mix_kernel(yf_ref, oh_ref, gf_ref, gh_ref, x_ref, g1_ref, sc2_ref, sh2_ref,
                wuf_ref, wuh_ref, wo_ref, lg_ref, lbias_ref, wr_ref, br_ref,
                x1_ref, h2_ref, ri_ref, rt_ref, cnt_ref, carry_sc, *, alpha, ngroups, nper):
    first = (pl.program_id(0) == 0) & (pl.program_id(1) == 0)

    @pl.when(first)
    def _():
        carry_sc[...] = jnp.zeros_like(carry_sc)

    tm = x_ref.shape[0]
    yf = jnp.dot(yf_ref[...], wuf_ref[...], preferred_element_type=F32)
    yh = jnp.dot(oh_ref[...], wuh_ref[...], preferred_element_type=F32)
    merged = gf_ref[...].astype(F32) * yf + gh_ref[...].astype(F32) * yh
    y = jnp.dot(merged.astype(BF16), wo_ref[...], preferred_element_type=F32)
    x1 = _layer_norm(alpha * x_ref[...] + g1_ref[...] * y, lg_ref[...], lbias_ref[...])
    x1_ref[...] = x1
    h2 = x1 * (1.0 + sc2_ref[...]) + sh2_ref[...]
    _store_chunks(h2_ref, h2)

    h_hi, h_lo = _bf16_pieces(h2, 2)
    hh = jnp.dot(h_hi, wr_ref[...], preferred_element_type=F32)
    logits = (hh[:, :LANES] + hh[:, LANES:]
              + jnp.dot(h_lo, wr_ref[:, :LANES], preferred_element_type=F32)) + br_ref[...]
    lane = lax.broadcasted_iota(jnp.int32, (tm, LANES), 1)
    big = jnp.int32(1 << 20)

    def argmax_first(vals, mask):
        mx = jnp.max(jnp.where(mask, vals, -jnp.inf), axis=1, keepdims=True)
        idx = jnp.min(jnp.where(mask & (vals == mx), lane, big), axis=1, keepdims=True)
        return mx, idx

    gmask = lane < ngroups
    gmax = jnp.max(jnp.where(gmask, logits, -jnp.inf), axis=1, keepdims=True)
    gexp = jnp.where(gmask, jnp.exp(logits - gmax), 0.0)
    gprob = gexp / jnp.sum(gexp, axis=1, keepdims=True)
    g_w, g_idx = argmax_first(gprob, gmask)

    lo = ngroups + g_idx * nper
    emask = (lane >= lo) & (lane < lo + nper)
    emax = jnp.max(jnp.where(emask, logits, -jnp.inf), axis=1, keepdims=True)
    eexp = jnp.where(emask, jnp.exp(logits - emax), 0.0)
    eprob = eexp / jnp.sum(eexp, axis=1, keepdims=True)
    p0, i0 = argmax_first(eprob, emask)
    p1, i1 = argmax_first(eprob, emask & (lane != i0))
    den = p0 + p1
    w0 = p0 / den * g_w
    w1 = p1 / den * g_w
    e0 = i0 - ngroups
    e1 = i1 - ngroups

    oh = ((lane == e0) | (lane == e1)).astype(F32)
    r = lax.broadcasted_iota(jnp.int32, (tm, tm), 0)
    c = lax.broadcasted_iota(jnp.int32, (tm, tm), 1)
    strict_lower = (c < r).astype(BF16)
    before = jnp.dot(strict_lower, oh.astype(BF16), preferred_element_type=F32) + carry_sc[...]
    rank0 = jnp.sum(jnp.where(lane == e0, before, 0.0), axis=1, keepdims=True)
    rank1 = jnp.sum(jnp.where(lane == e1, before, 0.0), axis=1, keepdims=True)
    carry_sc[...] = carry_sc[...] + jnp.sum(oh, axis=0, keepdims=True)
    cnt_ref[...] = carry_sc[...]

    info = jnp.where(lane == 0, w0, 0.0)
    info = jnp.where(lane == 1, w1, info)
    info = jnp.where(lane == 2, e0.astype(F32), info)
    info = jnp.where(lane == 3, e1.astype(F32), info)
    info = jnp.where(lane == 4, rank0, info)
    info = jnp.where(lane == 5, rank1, info)
    ri_ref[...] = info
    rt_ref[...] = info.T[:ROW_TILE, :]


def _mix(yf, oh, gf, gh, x, g1, sc2, sh2, wuf, wuh, wo, ln_g, ln_b, wr, br, alpha, ngroups, nper, tm=512):
    B, S, D = x.shape
    W = yf.shape[2]
    tok = lambda w: pl.BlockSpec((None, tm, w), lambda b, i: (b, i, 0))
    vec = pl.BlockSpec((None, 1, D), lambda b, i: (b, 0, 0))
    full = lambda a: pl.BlockSpec(a.shape, lambda b, i: (0,) * a.ndim)
    return pl.pallas_call(
        functools.partial(_mix_kernel, alpha=alpha, ngroups=ngroups, nper=nper),
        out_shape=(jax.ShapeDtypeStruct((B, S, D), F32),
                   jax.ShapeDtypeStruct((D // WORD_LANES, B * S, LANES), F32),
                   jax.ShapeDtypeStruct((B, S, LANES), F32),
                   jax.ShapeDtypeStruct((ROW_TILE, B * S), F32),
                   jax.ShapeDtypeStruct((1, LANES), F32)),
        grid=(B, S // tm),
        in_specs=[tok(W), tok(W), tok(D), tok(D), tok(D), vec, vec, vec,
                  full(wuf), full(wuh), full(wo), full(ln_g), full(ln_b), full(wr), full(br)],
        out_specs=(tok(D),
                   pl.BlockSpec((D // WORD_LANES, tm, LANES), lambda b, i: (0, b * (S // tm) + i, 0)),
                   tok(LANES),
                   pl.BlockSpec((ROW_TILE, tm), lambda b, i: (0, b * (S // tm) + i)),
                   pl.BlockSpec((1, LANES), lambda b, i: (0, 0))),
        scratch_shapes=[pltpu.VMEM((1, LANES), F32)],
        compiler_params=_cparams(("arbitrary", "arbitrary")),
    )(yf, oh, gf, gh, x, g1, sc2, sh2, wuf, wuh, wo, ln_g, ln_b, wr, br)


def _sc_mesh():
    return plsc.VectorSubcoreMesh(core_axis_name="core", subcore_axis_name="subcore")


def _sc_pipeline(body, grid, in_specs, out_specs):
    return pltpu.emit_pipeline(body, grid=grid, in_specs=in_specs, out_specs=out_specs,
                               core_axis_name=("core", "subcore"),
                               dimension_semantics=(pltpu.PARALLEL,) * len(grid))


def _sc_scatter_rows(src, rows_a, rows_b, n_out):
    nj, t = rows_a.shape
    win = SC_WINDOW
    nc = t // win

    @pl.kernel(out_type=jax.ShapeDtypeStruct((n_out, LANES), src.dtype), mesh=_sc_mesh(), scratch_types=[])
    def scatter(x_hbm, a_hbm, b_hbm, o_hbm):
        def body(x_vmem, a_vmem, b_vmem):
            pltpu.sync_copy(x_vmem, o_hbm.at[a_vmem.at[0]])
            pltpu.sync_copy(x_vmem, o_hbm.at[b_vmem.at[0]])

        idx = pl.BlockSpec((1, win), lambda j, c: (j, c))
        _sc_pipeline(body, (nj, nc), [pl.BlockSpec((win, LANES), lambda j, c: (j * nc + c, 0)), idx, idx],
                     [])(x_hbm, a_hbm, b_hbm)

    return scatter(src, rows_a, rows_b)


def _sc_gather_rows(table, rows):
    nr, t = rows.shape
    win = SC_WINDOW
    nc = t // win

    @pl.kernel(out_type=jax.ShapeDtypeStruct((nr * t, LANES), table.dtype), mesh=_sc_mesh(), scratch_types=[])
    def gather(x_hbm, i_hbm, o_hbm):
        def body(i_vmem, o_vmem):
            pltpu.sync_copy(x_hbm.at[i_vmem.at[0]], o_vmem)

        _sc_pipeline(body, (nr, nc), [pl.BlockSpec((1, win), lambda r, c: (r, c))],
                     [pl.BlockSpec((win, LANES), lambda r, c: (r * nc + c, 0))])(i_hbm, o_hbm)

    return gather(table, rows)


SC_LANES = 16


def _sc_pack_weights(w):
    e, k, n = w.shape
    flat = w.reshape(e * k * n // LANES, LANES)
    half = k * n // LANES // 2
    win = LANES
    hb = half // win
    nblk = e * hb

    @pl.kernel(out_type=jax.ShapeDtypeStruct((e * half, LANES), F32), mesh=_sc_mesh(), scratch_types=[],
               compiler_params=pltpu.CompilerParams(needs_layout_passes=False))
    def pack(w_hbm, o_hbm):
        def body(a_vmem, b_vmem, o_vmem):
            @pl.loop(0, win)
            def _(r):
                @pl.loop(0, LANES, step=SC_LANES)
                def _(c):
                    a = lax.bitcast_convert_type(a_vmem[r, pl.ds(c, SC_LANES)], jnp.uint32)
                    b = lax.bitcast_convert_type(b_vmem[r, pl.ds(c, SC_LANES)], jnp.uint32)
                    lo = (a + jnp.uint32(0x8000)) >> 16
                    hi = (b + jnp.uint32(0x8000)) & jnp.uint32(0xFFFF0000)
                    o_vmem[r, pl.ds(c, SC_LANES)] = lax.bitcast_convert_type(lo | hi, F32)

        _sc_pipeline(body, (nblk,),
                     [pl.BlockSpec((win, LANES), lambda g: ((g // hb) * 2 * hb + g % hb, 0)),
                      pl.BlockSpec((win, LANES), lambda g: ((g // hb) * 2 * hb + hb + g % hb, 0))],
                     [pl.BlockSpec((win, LANES), lambda g: (g, 0))])(w_hbm, w_hbm, o_hbm)

    return pack(flat).reshape(e, k // 2, n)


def _unpack_weight(ref):
    w = pltpu.bitcast(ref[...], jnp.uint32)
    return jnp.concatenate([pltpu.bitcast(w << 16, F32), pltpu.bitcast(w & jnp.uint32(0xFFFF0000), F32)],
                           axis=0).astype(BF16)


def _experts_kernel(te_ref, tn_ref, tb_ref, x_ref, wg_ref, wu_ref, wd_ref, o_ref):
    del tb_ref
    nrows = tn_ref[pl.program_id(0)]

    @pl.when(nrows > 0)
    def _():
        x = _load_chunks(x_ref)
        x = jnp.where(lax.broadcasted_iota(jnp.int32, x.shape, 0) < nrows, x, 0.0).astype(BF16)
        g = jnp.dot(x, _unpack_weight(wg_ref), preferred_element_type=F32)
        u = jnp.dot(x, _unpack_weight(wu_ref), preferred_element_type=F32)
        hid = (_silu(g) * u).astype(BF16)
        _store_chunks(o_ref, jnp.dot(hid, _unpack_weight(wd_ref), preferred_element_type=F32))


def _experts(tile_expert, tile_rows, tile_block, xs, wg, wu, wd, tm):
    E, DH, FF = wg.shape
    D = 2 * DH
    dt = D // WORD_LANES
    ntiles = tile_expert.shape[0]
    rows = pl.BlockSpec((dt, tm, LANES), lambda i, te, tn, tb: (0, tb[i], 0))
    grid_spec = pltpu.PrefetchScalarGridSpec(
        num_scalar_prefetch=3,
        grid=(ntiles,),
        in_specs=[rows,
                  pl.BlockSpec((None, DH, FF), lambda i, te, tn, tb: (te[i], 0, 0)),
                  pl.BlockSpec((None, DH, FF), lambda i, te, tn, tb: (te[i], 0, 0)),
                  pl.BlockSpec((None, FF // 2, D), lambda i, te, tn, tb: (te[i], 0, 0))],
        out_specs=rows,
    )
    return pl.pallas_call(
        _experts_kernel,
        out_shape=jax.ShapeDtypeStruct((dt, ntiles * tm, LANES), F32),
        grid_spec=grid_spec,
        compiler_params=_cparams(("arbitrary",)),
    )(tile_expert, tile_rows, tile_block, xs, wg, wu, wd)


def _combine_kernel(yg_ref, x1_ref, ri_ref, g2_ref, lg_ref, lb_ref, o_ref, *, alpha):
    ri = ri_ref[...]
    y = ri[:, 0:1] * _load_chunks(yg_ref.at[0]) + ri[:, 1:2] * _load_chunks(yg_ref.at[1])
    o_ref[...] = _layer_norm(alpha * x1_ref[...] + g2_ref[...] * y, lg_ref[...], lb_ref[...])


def _combine(yg, x1, rinfo, g2, ln_g, ln_b, alpha, tm=1024):
    B, S, D = x1.shape
    nb = S // tm
    return pl.pallas_call(
        functools.partial(_combine_kernel, alpha=alpha),
        out_shape=jax.ShapeDtypeStruct((B, S, D), F32),
        grid=(B, nb),
        in_specs=[pl.BlockSpec((2, D // WORD_LANES, tm, LANES), lambda b, i: (0, 0, b * nb + i, 0)),
                  pl.BlockSpec((None, tm, D), lambda b, i: (b, i, 0)),
                  pl.BlockSpec((None, tm, LANES), lambda b, i: (b, i, 0)),
                  pl.BlockSpec((None, 1, D), lambda b, i: (b, 0, 0)),
                  pl.BlockSpec((1, D), lambda b, i: (0, 0)),
                  pl.BlockSpec((1, D), lambda b, i: (0, 0))],
        out_specs=pl.BlockSpec((None, tm, D), lambda b, i: (b, i, 0)),
        compiler_params=_cparams(("parallel", "parallel")),
    )(yg, x1, rinfo, g2, ln_g, ln_b)


def kernel(x, c, w_ada, b_ada, w_in, b_fox_forget, hgrn_lb_logits, hgrn_norm_w, w_up_fox, w_up_hgrn, w_out,
           ln1_g, ln1_b, w_router_group, b_router_group, w_router_expert, b_router_expert,
           w_expert_gate, w_expert_up, w_expert_down, ln2_g, ln2_b):
    B, S, D = x.shape
    depth = w_ada.shape[0]
    assert depth == 1, "single-layer block"
    fox_heads = b_fox_forget.shape[1]
    fox_w = fox_heads * HEAD_DIM
    hgrn_w = hgrn_norm_w.shape[1]
    ngroups = w_router_group.shape[2]
    nexp = w_router_expert.shape[2]
    nper = nexp // ngroups
    alpha = (2 * depth) ** 0.25
    T = B * S

    ada = _ada(c, w_ada[0], b_ada[0])
    sh1, sc1, g1, sh2, sc2, g2 = [a.reshape(B, 1, D) for a in jnp.split(ada, 6, axis=-1)]

    wi = w_in[0]
    o_ff = 3 * fox_w
    w_fox = jnp.pad(wi[:, :o_ff + fox_heads], ((0, 0), (0, LANES - fox_heads))).astype(BF16)
    w_rest = wi[:, o_ff + fox_heads:].astype(BF16)
    widths = [fox_w, fox_w, fox_w, LANES, hgrn_w, hgrn_w, hgrn_w, hgrn_w, D, D]
    segs, off = [], 0
    for n, w in enumerate(widths):
        if n == 4:
            off = 0
        segs.append((off, off + w))
        off += w
    fq, fk, fv, ffp, hq, hf, hi, hg, gf, gh = _inproj(x, sc1, sh1, w_fox, w_rest, segs)

    bias_p = jnp.zeros((1, LANES), F32).at[0, :fox_heads].set(b_fox_forget[0])
    cum = _foxcum(ffp, bias_p)
    y_fox = _fox(fq, fk, fv, cum)

    o_h = _hgrn(hq, hf, hi, hg, hgrn_lb_logits, hgrn_norm_w[0])

    wr = jnp.zeros((D, LANES), F32).at[:, :ngroups].set(w_router_group[0]).at[:, ngroups:ngroups + nexp].set(
        w_router_expert[0])
    wr_hi = lax.bitcast_convert_type(lax.bitcast_convert_type(wr, jnp.uint32) & jnp.uint32(0xFFFF0000), F32)
    wr = jnp.concatenate([wr_hi.astype(BF16), (wr - wr_hi).astype(BF16)], axis=1)
    br = jnp.zeros((1, LANES), F32).at[0, :ngroups].set(b_router_group[0]).at[0, ngroups:ngroups + nexp].set(
        b_router_expert[0])
    x1, h2, rinfo, fields, counts = _mix(
        y_fox, o_h, gf, gh, x, g1, sc2, sh2,
        w_up_fox[0].astype(BF16), w_up_hgrn[0].astype(BF16), w_out[0].astype(BF16),
        ln1_g[0].reshape(1, D), ln1_b[0].reshape(1, D), wr, br, alpha, ngroups, nper)

    tm_e = 512
    dt = D // WORD_LANES
    ntiles = (2 * T) // tm_e + nexp
    nslots = ntiles * tm_e
    cnt = counts[0, :nexp].astype(jnp.int32)
    padded = ((cnt + tm_e - 1) // tm_e) * tm_e
    ends = jnp.cumsum(padded)
    starts = ends - padded
    eid = fields[2:4].astype(jnp.int32)
    rank = fields[4:6].astype(jnp.int32)
    first = jnp.sum(jnp.where(eid[None] == jnp.arange(nexp, dtype=jnp.int32)[:, None, None],
                              starts[:, None, None], 0), axis=0)
    pos = first + rank
    tile_start = jnp.arange(ntiles, dtype=jnp.int32) * tm_e
    tile_block = jnp.minimum(jnp.arange(ntiles, dtype=jnp.int32), ends[-1] // tm_e - 1)
    tile_expert = jnp.minimum(jnp.sum((tile_start[:, None] >= ends[None, :]).astype(jnp.int32), axis=1), nexp - 1)
    tile_rows = jnp.clip(starts[tile_expert] + cnt[tile_expert] - tile_start, 0, tm_e)
    tile_expert = tile_expert[tile_block]
    rows = pos[:, None, :] + (jnp.arange(dt, dtype=jnp.int32) * nslots)[None, :, None]

    expert_w = [_sc_pack_weights(w[0]) for w in (w_expert_gate, w_expert_up, w_expert_down)]
    xs = _sc_scatter_rows(h2.reshape(dt * T, LANES), rows[0], rows[1], dt * nslots)
    ys = _experts(tile_expert, tile_rows, tile_block, xs.reshape(dt, nslots, LANES), *expert_w, tm_e)
    yg = _sc_gather_rows(ys.reshape(dt * nslots, LANES), rows.reshape(2 * dt, T))
    return _combine(yg.reshape(2, dt, T, LANES), x1, rinfo, g2,
                    ln2_g[0].reshape(1, D), ln2_b[0].reshape(1, D), alpha)
```

```python
import functools

import jax
import jax.numpy as jnp
from jax import lax
from jax.experimental import pallas as pl
from jax.experimental.pallas import tpu as pltpu
from jax.experimental.pallas import tpu_sc as plsc

F32 = jnp.float32
BF16 = jnp.bfloat16

LANES = 128
HEAD_DIM = 64
LN_EPS = 1e-5
RMS_EPS = 1e-6
LOG2E = 1.4426950408889634
NEG_BIG = -1e30
HCHUNK = 16
HBLOCK = 64
HGRN_SAFE_EXP = 60.0
ROW_TILE = 8
WORD_LANES = 2 * LANES
SC_WINDOW = 256
VMEM_LIMIT = 56 * 1024 * 1024


def _cparams(sem, vmem=VMEM_LIMIT):
    return pltpu.CompilerParams(dimension_semantics=sem, vmem_limit_bytes=vmem)


def _sigmoid(x):
    return 0.5 * jnp.tanh(0.5 * x) + 0.5


def _silu(x):
    return x * _sigmoid(x)


def _bf16_pieces(x, n):
    pieces = []
    for _ in range(n):
        top = pltpu.bitcast(pltpu.bitcast(x, jnp.uint32) & jnp.uint32(0xFFFF0000), F32)
        pieces.append(top.astype(BF16))
        x = x - top
    return pieces


def _exact_matrix_dot(m, x):
    r = jnp.dot(m, jnp.concatenate(_bf16_pieces(x, 3), axis=1), preferred_element_type=F32)
    return r[:, :LANES] + r[:, LANES:2 * LANES] + r[:, 2 * LANES:]


def _ada_kernel(c_ref, w_ref, b_ref, o_ref):
    c_hi, c_lo = _bf16_pieces(_silu(c_ref[...]), 2)
    w_hi, w_lo = _bf16_pieces(w_ref[...], 2)
    o_ref[...] = (jnp.dot(c_hi, w_hi, preferred_element_type=F32) + jnp.dot(c_hi, w_lo, preferred_element_type=F32)
                  + jnp.dot(c_lo, w_hi, preferred_element_type=F32)) + b_ref[...]


def _ada(c, w_ada, b_ada):
    B, D = c.shape
    N = w_ada.shape[1]
    tn = 1024
    return pl.pallas_call(
        _ada_kernel,
        out_shape=jax.ShapeDtypeStruct((B, N), F32),
        grid=(N // tn,),
        in_specs=[pl.BlockSpec((B, D), lambda j: (0, 0)),
                  pl.BlockSpec((D, tn), lambda j: (0, j)),
                  pl.BlockSpec((1, tn), lambda j: (0, j))],
        out_specs=pl.BlockSpec((B, tn), lambda j: (0, j)),
        compiler_params=_cparams(("arbitrary",)),
    )(c, w_ada, b_ada.reshape(1, N))


N_FOX_SEGS = 4
SILU_SEGS = (4, 7)
SIGMOID_SEGS = (8, 9)


def _inproj_kernel(x_ref, sc_ref, sh_ref, wf_ref, wr_ref,
                   fq_ref, fk_ref, fv_ref, ff_ref, hq_ref, hf_ref, hi_ref, hg_ref, gf_ref, gh_ref,
                   *, segs, q_scale):
    h = (x_ref[...] * (1.0 + sc_ref[...]) + sh_ref[...]).astype(BF16)
    outs = (fq_ref, fk_ref, fv_ref, ff_ref, hq_ref, hf_ref, hi_ref, hg_ref, gf_ref, gh_ref)
    for idx, (o_ref, (a, b)) in enumerate(zip(outs, segs)):
        w_ref = wf_ref if idx < N_FOX_SEGS else wr_ref
        r = jnp.dot(h, w_ref[:, a:b], preferred_element_type=F32)
        if idx == 0:
            r = r * q_scale
        elif idx in SILU_SEGS:
            r = _silu(r)
        elif idx in SIGMOID_SEGS:
            r = _sigmoid(r)
        o_ref[...] = r.astype(o_ref.dtype)


def _inproj(x, sc1, sh1, w_fox, w_rest, segs, tm=256):
    B, S, D = x.shape
    widths = [b - a for a, b in segs]
    dtypes = [BF16, BF16, BF16, F32, BF16, F32, BF16, BF16, BF16, BF16]
    out_shape = tuple(jax.ShapeDtypeStruct((B, S, w), dt) for w, dt in zip(widths, dtypes))
    out_specs = tuple(pl.BlockSpec((None, tm, w), lambda b, i: (b, i, 0)) for w in widths)
    vec = pl.BlockSpec((None, 1, D), lambda b, i: (b, 0, 0))
    return pl.pallas_call(
        functools.partial(_inproj_kernel, segs=tuple(segs), q_scale=HEAD_DIM ** -0.5 * LOG2E),
        out_shape=out_shape,
        grid=(B, S // tm),
        in_specs=[pl.BlockSpec((None, tm, D), lambda b, i: (b, i, 0)), vec, vec,
                  pl.BlockSpec(w_fox.shape, lambda b, i: (0, 0)),
                  pl.BlockSpec(w_rest.shape, lambda b, i: (0, 0))],
        out_specs=out_specs,
        compiler_params=_cparams(("parallel", "parallel")),
    )(x, sc1, sh1, w_fox, w_rest)


def _foxcum_kernel(ff_ref, b_ref, o_ref, *, blk):
    S = ff_ref.shape[0]
    r = lax.broadcasted_iota(jnp.int32, (blk, blk), 0)
    c = lax.broadcasted_iota(jnp.int32, (blk, blk), 1)
    lower = jnp.where(r >= c, 1.0, 0.0).astype(BF16)
    carry = jnp.zeros((1, LANES), F32)
    for j in range(S // blk):
        z = ff_ref[j * blk:(j + 1) * blk, :] + b_ref[...]
        lf = jnp.minimum(z, 0.0) - jnp.log(1.0 + jnp.exp(-jnp.abs(z)))
        cum = _exact_matrix_dot(lower, lf) + carry
        o_ref[j * blk:(j + 1) * blk, :] = cum * LOG2E
        carry = cum[blk - 1:blk, :]


def _foxcum(ffp, bias_p, blk=256):
    B, S, _ = ffp.shape
    return pl.pallas_call(
        functools.partial(_foxcum_kernel, blk=blk),
        out_shape=jax.ShapeDtypeStruct((B, S, LANES), F32),
        grid=(B,),
        in_specs=[pl.BlockSpec((None, S, LANES), lambda b: (b, 0, 0)),
                  pl.BlockSpec((1, LANES), lambda b: (0, 0))],
        out_specs=pl.BlockSpec((None, S, LANES), lambda b: (b, 0, 0)),
        compiler_params=_cparams(("parallel",)),
    )(ffp, bias_p)


NCUM = 3


def _fox_kernel(q_ref, k_ref, v_ref, c_ref, o_ref, ka_sc, kb_sc, va_sc, vb_sc, *, tq, tk):
    p = pl.program_id(1)
    qi = pl.program_id(2)
    S = k_ref.shape[0]

    @pl.when(qi == 0)
    def _():
        lane = lax.broadcasted_iota(jnp.int32, (S, LANES), 1)
        rr = lax.broadcasted_iota(jnp.int32, (LANES, LANES), 0)
        cc = lax.broadcasted_iota(jnp.int32, (LANES, LANES), 1)
        rest = c_ref[...]
        placed = jnp.zeros((S, LANES), F32)
        for i in range(NCUM):
            piece = rest.astype(BF16)
            rest = rest - piece.astype(F32)
            sel = ((rr == 2 * p) & (cc == HEAD_DIM + i)) | ((rr == 2 * p + 1) & (cc == i))
            placed = placed + jnp.dot(piece, jnp.where(sel, 1.0, 0.0).astype(BF16), preferred_element_type=F32)
        k2 = k_ref[...].astype(F32)
        ka_sc[...] = jnp.where(lane < HEAD_DIM, k2, -placed).astype(BF16)
        kb_sc[...] = jnp.where(lane >= HEAD_DIM, k2, -placed).astype(BF16)
        vt = v_ref[...].astype(F32).T
        row = lax.broadcasted_iota(jnp.int32, (LANES, S), 0)
        va_sc[...] = jnp.where(row < HEAD_DIM, vt, jnp.where(row == HEAD_DIM, 1.0, 0.0)).astype(BF16)
        vb_sc[...] = jnp.where(row >= HEAD_DIM, vt, jnp.where(row == 0, 1.0, 0.0)).astype(BF16)

    q2 = q_ref[...].astype(F32)
    qlane = lax.broadcasted_iota(jnp.int32, (tq, LANES), 1)
    qa = jnp.where(qlane < HEAD_DIM, q2, jnp.where(qlane < HEAD_DIM + NCUM, 1.0, 0.0)).astype(BF16)
    qb = jnp.where(qlane >= HEAD_DIM, q2, jnp.where(qlane < NCUM, 1.0, 0.0)).astype(BF16)
    nsub = tq // tk

    def block(k0, carry, diag_off):
        q0 = 0 if diag_off is None else diag_off
        out = []
        for ksc, vsc, qh, (m, acc) in ((ka_sc, va_sc, qa, carry[:2]), (kb_sc, vb_sc, qb, carry[2:])):
            st = lax.dot_general(ksc[pl.ds(k0, tk), :], qh[q0:, :], (((1,), (1,)), ((), ())),
                                 preferred_element_type=F32)
            if diag_off is not None:
                st = jnp.where(lax.broadcasted_iota(jnp.int32, st.shape, 0)
                               <= lax.broadcasted_iota(jnp.int32, st.shape, 1), st, NEG_BIG)
            m_old = m[:, q0:]
            m_new = jnp.maximum(m_old, jnp.max(st, axis=0, keepdims=True))
            pt = jnp.exp2(st - m_new).astype(BF16)
            acc_new = (jnp.exp2(m_old - m_new) * acc[:, q0:]
                       + jnp.dot(vsc[:, pl.ds(k0, tk)], pt, preferred_element_type=F32))
            if q0:
                m_new = jnp.concatenate([m[:, :q0], m_new], axis=1)
                acc_new = jnp.concatenate([acc[:, :q0], acc_new], axis=1)
            out += [m_new, acc_new]
        return tuple(out)

    def group(j, carry):
        k0 = pl.multiple_of(j * (nsub * tk), nsub * tk)
        for u in range(nsub):
            carry = block(k0 + u * tk, carry, None)
        return carry

    m0 = jnp.full((1, tq), NEG_BIG, F32)
    a0 = jnp.zeros((LANES, tq), F32)
    carry = lax.fori_loop(0, qi, group, (m0, a0, m0, a0))
    for d in range(nsub):
        carry = block(pl.multiple_of(qi * tq + d * tk, tk), carry, d * tk)
    _, aa, _, ab = carry
    row = lax.broadcasted_iota(jnp.int32, (LANES, tq), 0)
    ot = jnp.where(row < HEAD_DIM, aa * (1.0 / aa[HEAD_DIM:HEAD_DIM + 1, :]), ab * (1.0 / ab[0:1, :]))
    o_ref[...] = ot.T.astype(o_ref.dtype)


def _fox(fq, fk, fv, cum, tq=2048, tk=512):
    B, S, W = fq.shape
    tq = min(tq, S)
    assert tq % tk == 0 and S % tq == 0
    npairs = W // LANES
    return pl.pallas_call(
        functools.partial(_fox_kernel, tq=tq, tk=tk),
        out_shape=jax.ShapeDtypeStruct((B, S, W), BF16),
        grid=(B, npairs, S // tq),
        in_specs=[pl.BlockSpec((None, tq, LANES), lambda b, p, i: (b, i, p)),
                  pl.BlockSpec((None, S, LANES), lambda b, p, i: (b, 0, p)),
                  pl.BlockSpec((None, S, LANES), lambda b, p, i: (b, 0, p)),
                  pl.BlockSpec((None, S, LANES), lambda b, p, i: (b, 0, 0))],
        out_specs=pl.BlockSpec((None, tq, LANES), lambda b, p, i: (b, i, p)),
        scratch_shapes=[pltpu.VMEM((S, LANES), BF16), pltpu.VMEM((S, LANES), BF16),
                        pltpu.VMEM((LANES, S), BF16), pltpu.VMEM((LANES, S), BF16)],
        compiler_params=_cparams(("parallel", "parallel", "arbitrary")),
    )(fq, fk, fv, cum)


def _hgrn_kernel(hq_ref, hf_ref, hi_ref, hg_ref, lb_ref, nw_ref, o_ref,
                 b_sc, kk_sc, qq_sc, o_sc, w1_sc, w2_sc, w3_sc, w4_sc, w5_sc,
                 p_sc, st16_sc, dec_sc, st64_sc):
    S = hq_ref.shape[0]
    C = HCHUNK
    nchunks = S // C
    BLK = HBLOCK
    nblk = S // BLK

    lg = lb_ref[...]
    e = jnp.exp(lg - jnp.max(lg, axis=0, keepdims=True))
    lb = e[0:1, :] / jnp.sum(e, axis=0, keepdims=True)

    f = lb + (1.0 - lb) * (1.0 / (1.0 + jnp.exp(-hf_ref[...])))
    lf = jnp.log(f)
    kk_sc[...] = 1.0 - f
    qq_sc[...] = hq_ref[...].astype(F32)

    row = lax.broadcasted_iota(jnp.int32, (S, LANES), 0)
    rb = 4 * BLK
    tr = lax.broadcasted_iota(jnp.int32, (rb, rb), 0)
    tc = lax.broadcasted_iota(jnp.int32, (rb, rb), 1)
    tri = jnp.where(((tr & -BLK) == (tc & -BLK)) & (tc <= tr), 1.0, 0.0).astype(BF16)
    lf3 = jnp.concatenate(_bf16_pieces(lf, 3), axis=1)
    for j in range(S // rb):
        c3 = jnp.dot(tri, lf3[j * rb:(j + 1) * rb, :], preferred_element_type=F32)
        b_sc[j * rb:(j + 1) * rb, :] = c3[:, :LANES] + c3[:, LANES:2 * LANES] + c3[:, 2 * LANES:]
    safe = jnp.max(-b_sc[...].reshape(nblk, BLK, LANES)[:, BLK - 1, :]) <= HGRN_SAFE_EXP

    lane = lax.broadcasted_iota(jnp.int32, (C, LANES), 1)
    sr = lax.broadcasted_iota(jnp.int32, (LANES, LANES), 0)
    scn = lax.broadcasted_iota(jnp.int32, (LANES, LANES), 1)
    same_head = (sr // HEAD_DIM) == (scn // HEAD_DIM)

    @pl.when(safe)
    def _factorised():
        qh_sc, kh_sc, ke_sc, qd_sc, k2_sc = w1_sc, w2_sc, w3_sc, w4_sc, w5_sc
        SB = 2 * BLK
        nsb = S // SB
        bb = b_sc[...]
        dblk = jnp.exp(bb.reshape(nblk, BLK, LANES)[:, BLK - 1:BLK, :])
        dfull = jnp.broadcast_to(dblk, (nblk, BLK, LANES)).reshape(S, LANES)
        second = (row & BLK) != 0
        d_prev = pltpu.roll(dfull, BLK, axis=0)
        d_next = pltpu.roll(dfull, S - BLK, axis=0)
        qh = qq_sc[...] * jnp.exp(bb)
        qh_sc[...] = qh.astype(BF16)
        qd_sc[...] = (qh * jnp.where(second, d_prev, 1.0)).astype(BF16)
        kh = kk_sc[...] * jnp.exp(-bb)
        kh_sc[...] = kh.astype(BF16)
        ke = kh * dfull
        ke_sc[...] = ke.astype(BF16)
        k2_sc[...] = (ke * jnp.where(second, 1.0, d_next)).astype(BF16)
        d3 = dfull.reshape(nsb, SB, LANES)
        dec_sc[pl.ds(0, nsb), :] = d3[:, 0, :] * d3[:, BLK, :]
        unroll = min(16, nsb)
        assert nsb % unroll == 0
        tn = (((0,), (0,)), ((), ()))
        nt = (((1,), (1,)), ((), ()))

        def scan(g, st):
            for u in range(unroll):
                i = g * unroll + u
                r0 = pl.multiple_of(i * SB, SB)
                st64_sc[i] = st.astype(BF16)
                upd = lax.dot_general(hi_ref[pl.ds(r0, SB), :], k2_sc[pl.ds(r0, SB), :], tn,
                                      preferred_element_type=F32)
                st = st * dec_sc[pl.ds(i, 1), :] + jnp.where(same_head, upd, 0.0)
            return st

        lax.fori_loop(0, nsb // unroll, scan, jnp.zeros((LANES, LANES), F32))

        r = lax.broadcasted_iota(jnp.int32, (2 * SB, 2 * SB), 0)
        c = lax.broadcasted_iota(jnp.int32, (2 * SB, 2 * SB), 1)
        t = r & (SB - 1)
        visible = (((c < SB) & ((t & BLK) == (c & BLK)) & ((t & (BLK - 1)) >= (c & (BLK - 1))))
                   | ((c >= SB) & (c < SB + BLK) & (t >= BLK)))
        plane = lax.broadcasted_iota(jnp.int32, (SB, LANES), 1)
        pad = jnp.zeros((BLK, LANES), BF16)

        def readout(g, _):
            for u in range(unroll):
                i = g * unroll + u
                r0 = pl.multiple_of(i * SB, SB)
                vb = hi_ref[pl.ds(r0, SB), :]
                qh2 = qh_sc[pl.ds(r0, SB), :]
                q2 = jnp.concatenate([jnp.where(plane < HEAD_DIM, qh2, jnp.zeros_like(qh2)),
                                      jnp.where(plane >= HEAD_DIM, qh2, jnp.zeros_like(qh2))], axis=0)
                kext = jnp.concatenate([kh_sc[pl.ds(r0, SB), :], ke_sc[pl.ds(r0, BLK), :], pad], axis=0)
                vext = jnp.concatenate([vb, vb[:BLK], pad], axis=0)
                sc = lax.dot_general(q2, kext, nt, preferred_element_type=F32)
                sc = jnp.where(visible, sc, 0.0).astype(BF16)
                out = jnp.dot(sc, vext, preferred_element_type=F32)
                o_inter = lax.dot_general(qd_sc[pl.ds(r0, SB), :], st64_sc[i], nt, preferred_element_type=F32)
                o_sc[pl.ds(r0, SB), :] = jnp.where(plane < HEAD_DIM, out[:SB], out[SB:]) + o_inter
            return 0

        lax.fori_loop(0, nsb // unroll, readout, 0)

    @pl.when(jnp.logical_not(safe))
    def _direct():
        qt_sc, kt_sc, s_sc, a2_sc = w1_sc, w2_sc, w3_sc, b_sc
        bb = b_sc[...]
        cl = jnp.broadcast_to(bb.reshape(nchunks, C, LANES)[:, C - 1:C, :], (nchunks, C, LANES)).reshape(S, LANES)
        aa = bb - jnp.where((row & (BLK - 1)) >= C, pltpu.roll(cl, C, axis=0), 0.0)
        al = jnp.broadcast_to(aa.reshape(nchunks, C, LANES)[:, C - 1:C, :], (nchunks, C, LANES)).reshape(S, LANES)
        qt_sc[...] = (qq_sc[...] * jnp.exp(aa)).astype(BF16)
        kt_sc[...] = (kk_sc[...] * jnp.exp(al - aa)).astype(BF16)
        dec_sc[...] = jnp.exp(aa.reshape(nchunks, C, LANES)[:, C - 1, :])
        a2_sc[...] = aa * LOG2E
        trow = lax.broadcasted_iota(jnp.int32, (C, LANES), 0)

        def gen(c, _):
            r0 = pl.multiple_of(c * C, C)
            ac = a2_sc[pl.ds(r0, C), :]
            qc = qq_sc[pl.ds(r0, C), :]
            kc = kk_sc[pl.ds(r0, C), :]
            half = C // 2
            for s in range(C):
                if s < half:
                    dec = jnp.exp2(jnp.where(trow >= s, ac - ac[s:s + 1, :], NEG_BIG))
                    p = qc * (kc[s:s + 1, :] * dec)
                else:
                    dec = jnp.exp2(jnp.where(trow[half:] >= s, ac[half:] - ac[s:s + 1, :], NEG_BIG))
                    p = jnp.concatenate([jnp.zeros((half, LANES), F32), qc[half:] * (kc[s:s + 1, :] * dec)],
                                        axis=0)
                p_sc[pl.ds(r0, C), s * LANES:(s + 1) * LANES] = p.astype(BF16)
            return 0

        lax.fori_loop(0, nchunks, gen, 0)

        er = lax.broadcasted_iota(jnp.int32, (C * LANES, LANES), 0)
        ec = lax.broadcasted_iota(jnp.int32, (C * LANES, LANES), 1)
        emat = (ec == ((er & (LANES - 1)) // HEAD_DIM) * C + er // LANES).astype(BF16)
        rb = 256

        def red(i, _):
            r0 = pl.multiple_of(i * rb, rb)
            s_sc[pl.ds(r0, rb), :] = jnp.dot(p_sc[pl.ds(r0, rb), :], emat,
                                             preferred_element_type=F32).astype(BF16)
            return 0

        lax.fori_loop(0, S // rb, red, 0)

        unroll = 16
        assert nchunks % unroll == 0

        def scan(g, st):
            for u in range(unroll):
                c = g * unroll + u
                r0 = pl.multiple_of(c * C, C)
                st16_sc[c] = st.astype(BF16)
                upd = lax.dot_general(hi_ref[pl.ds(r0, C), :], kt_sc[pl.ds(r0, C), :],
                                      (((0,), (0,)), ((), ())), preferred_element_type=F32)
                st = st * dec_sc[pl.ds(c, 1), :] + jnp.where(same_head, upd, 0.0)
            return st

        lax.fori_loop(0, nchunks // unroll, scan, jnp.zeros((LANES, LANES), F32))

        def readout(g, _):
            for u in range(unroll):
                c = g * unroll + u
                r0 = pl.multiple_of(c * C, C)
                vc = hi_ref[pl.ds(r0, C), :]
                o_inter = lax.dot_general(qt_sc[pl.ds(r0, C), :], st16_sc[c],
                                          (((1,), (1,)), ((), ())), preferred_element_type=F32)
                v2 = jnp.concatenate([jnp.where(lane < HEAD_DIM, vc, jnp.zeros_like(vc)),
                                      jnp.where(lane >= HEAD_DIM, vc, jnp.zeros_like(vc))], axis=0)
                o_intra = jnp.dot(s_sc[pl.ds(r0, C), :][:, :2 * C], v2, preferred_element_type=F32)
                o_sc[pl.ds(r0, C), :] = o_inter + o_intra
            return 0

        lax.fori_loop(0, nchunks // unroll, readout, 0)

    o = o_sc[...]
    ones_head = jnp.where(same_head, 1.0, 0.0).astype(BF16)
    sq_hi, sq_lo = _bf16_pieces(o * o, 2)
    ms = (jnp.dot(sq_hi, ones_head, preferred_element_type=F32)
          + jnp.dot(sq_lo, ones_head, preferred_element_type=F32)) * (1.0 / HEAD_DIM)
    y = o * lax.rsqrt(ms + RMS_EPS) * nw_ref[...]
    o_ref[...] = (y * hg_ref[...].astype(F32)).astype(o_ref.dtype)


def _hgrn(hq, hf, hi, hg, lb_logits, norm_w):
    B, S, W = hq.shape
    npairs = W // LANES
    nrows = lb_logits.shape[0]
    seq = pl.BlockSpec((None, S, LANES), lambda b, p: (b, 0, p))
    return pl.pallas_call(
        _hgrn_kernel,
        out_shape=jax.ShapeDtypeStruct((B, S, W), BF16),
        grid=(B, npairs),
        in_specs=[seq, seq, seq, seq,
                  pl.BlockSpec((nrows, LANES), lambda b, p: (0, p)),
                  pl.BlockSpec((1, LANES), lambda b, p: (0, p))],
        out_specs=seq,
        scratch_shapes=[pltpu.VMEM((S, LANES), F32),
                        pltpu.VMEM((S, LANES), F32),
                        pltpu.VMEM((S, LANES), F32),
                        pltpu.VMEM((S, LANES), F32),
                        pltpu.VMEM((S, LANES), BF16),
                        pltpu.VMEM((S, LANES), BF16),
                        pltpu.VMEM((S, LANES), BF16),
                        pltpu.VMEM((S, LANES), BF16),
                        pltpu.VMEM((S, LANES), BF16),
                        pltpu.VMEM((S, HCHUNK * LANES), BF16),
                        pltpu.VMEM((S // HCHUNK, LANES, LANES), BF16),
                        pltpu.VMEM((S // HCHUNK, LANES), F32),
                        pltpu.VMEM((S // HBLOCK, LANES, LANES), BF16)],
        compiler_params=_cparams(("parallel", "parallel")),
    )(hq, hf, hi, hg, lb_logits, norm_w.reshape(1, W))


def _layer_norm(v, g, b):
    mu = jnp.mean(v, axis=-1, keepdims=True)
    d = v - mu
    var = jnp.mean(d * d, axis=-1, keepdims=True)
    return d * lax.rsqrt(var + LN_EPS) * g + b


def _bf16_bits(x):
    return (pltpu.bitcast(x, jnp.uint32) + jnp.uint32(0x8000)) & jnp.uint32(0xFFFF0000)


def _store_chunks(ref, val):
    n = ref.shape[0]
    for j in range(n):
        lo = _bf16_bits(val[:, j * LANES:(j + 1) * LANES]) >> 16
        hi = _bf16_bits(val[:, (j + n) * LANES:(j + n + 1) * LANES])
        ref[j] = pltpu.bitcast(lo | hi, F32)


def _load_chunks(ref):
    words = [pltpu.bitcast(ref[j], jnp.uint32) for j in range(ref.shape[0])]
    lo = [pltpu.bitcast(w << 16, F32) for w in words]
    hi = [pltpu.bitcast(w & jnp.uint32(0xFFFF0000), F32) for w in words]
    return jnp.concatenate(lo + hi, axis=1)


def _mix_kernel(yf_ref, oh_ref, gf_ref, gh_ref, x_ref, g1_ref, sc2_ref, sh2_ref,
                wuf_ref, wuh_ref, wo_ref, lg_ref, lbias_ref, wr_ref, br_ref,
                x1_ref, h2_ref, ri_ref, rt_ref, cnt_ref, carry_sc, *, alpha, ngroups, nper):
    first = (pl.program_id(0) == 0) & (pl.program_id(1) == 0)

    @pl.when(first)
    def _():
        carry_sc[...] = jnp.zeros_like(carry_sc)

    tm = x_ref.shape[0]
    yf = jnp.dot(yf_ref[...], wuf_ref[...], preferred_element_type=F32)
    yh = jnp.dot(oh_ref[...], wuh_ref[...], preferred_element_type=F32)
    merged = gf_ref[...].astype(F32) * yf + gh_ref[...].astype(F32) * yh
    y = jnp.dot(merged.astype(BF16), wo_ref[...], preferred_element_type=F32)
    x1 = _layer_norm(alpha * x_ref[...] + g1_ref[...] * y, lg_ref[...], lbias_ref[...])
    x1_ref[...] = x1
    h2 = x1 * (1.0 + sc2_ref[...]) + sh2_ref[...]
    _store_chunks(h2_ref, h2)

    h_hi, h_lo = _bf16_pieces(h2, 2)
    hh = jnp.dot(h_hi, wr_ref[...], preferred_element_type=F32)
    logits = (hh[:, :LANES] + hh[:, LANES:]
              + jnp.dot(h_lo, wr_ref[:, :LANES], preferred_element_type=F32)) + br_ref[...]
    lane = lax.broadcasted_iota(jnp.int32, (tm, LANES), 1)
    big = jnp.int32(1 << 20)

    def argmax_first(vals, mask):
        mx = jnp.max(jnp.where(mask, vals, -jnp.inf), axis=1, keepdims=True)
        idx = jnp.min(jnp.where(mask & (vals == mx), lane, big), axis=1, keepdims=True)
        return mx, idx

    gmask = lane < ngroups
    gmax = jnp.max(jnp.where(gmask, logits, -jnp.inf), axis=1, keepdims=True)
    gexp = jnp.where(gmask, jnp.exp(logits - gmax), 0.0)
    gprob = gexp / jnp.sum(gexp, axis=1, keepdims=True)
    g_w, g_idx = argmax_first(gprob, gmask)

    lo = ngroups + g_idx * nper
    emask = (lane >= lo) & (lane < lo + nper)
    emax = jnp.max(jnp.where(emask, logits, -jnp.inf), axis=1, keepdims=True)
    eexp = jnp.where(emask, jnp.exp(logits - emax), 0.0)
    eprob = eexp / jnp.sum(eexp, axis=1, keepdims=True)
    p0, i0 = argmax_first(eprob, emask)
    p1, i1 = argmax_first(eprob, emask & (lane != i0))
    den = p0 + p1
    w0 = p0 / den * g_w
    w1 = p1 / den * g_w
    e0 = i0 - ngroups
    e1 = i1 - ngroups

    oh = ((lane == e0) | (lane == e1)).astype(F32)
    r = lax.broadcasted_iota(jnp.int32, (tm, tm), 0)
    c = lax.broadcasted_iota(jnp.int32, (tm, tm), 1)
    strict_lower = (c < r).astype(BF16)
    before = jnp.dot(strict_lower, oh.astype(BF16), preferred_element_type=F32) + carry_sc[...]
    rank0 = jnp.sum(jnp.where(lane == e0, before, 0.0), axis=1, keepdims=True)
    rank1 = jnp.sum(jnp.where(lane == e1, before, 0.0), axis=1, keepdims=True)
    carry_sc[...] = carry_sc[...] + jnp.sum(oh, axis=0, keepdims=True)
    cnt_ref[...] = carry_sc[...]

    info = jnp.where(lane == 0, w0, 0.0)
    info = jnp.where(lane == 1, w1, info)
    info = jnp.where(lane == 2, e0.astype(F32), info)
    info = jnp.where(lane == 3, e1.astype(F32), info)
    info = jnp.where(lane == 4, rank0, info)
    info = jnp.where(lane == 5, rank1, info)
    ri_ref[...] = info
    rt_ref[...] = info.T[:ROW_TILE, :]


def _mix(yf, oh, gf, gh, x, g1, sc2, sh2, wuf, wuh, wo, ln_g, ln_b, wr, br, alpha, ngroups, nper, tm=512):
    B, S, D = x.shape
    W = yf.shape[2]
    tok = lambda w: pl.BlockSpec((None, tm, w), lambda b, i: (b, i, 0))
    vec = pl.BlockSpec((None, 1, D), lambda b, i: (b, 0, 0))
    full = lambda a: pl.BlockSpec(a.shape, lambda b, i: (0,) * a.ndim)
    return pl.pallas_call(
        functools.partial(_mix_kernel, alpha=alpha, ngroups=ngroups, nper=nper),
        out_shape=(jax.ShapeDtypeStruct((B, S, D), F32),
                   jax.ShapeDtypeStruct((D // WORD_LANES, B * S, LANES), F32),
                   jax.ShapeDtypeStruct((B, S, LANES), F32),
                   jax.ShapeDtypeStruct((ROW_TILE, B * S), F32),
                   jax.ShapeDtypeStruct((1, LANES), F32)),
        grid=(B, S // tm),
        in_specs=[tok(W), tok(W), tok(D), tok(D), tok(D), vec, vec, vec,
                  full(wuf), full(wuh), full(wo), full(ln_g), full(ln_b), full(wr), full(br)],
        out_specs=(tok(D),
                   pl.BlockSpec((D // WORD_LANES, tm, LANES), lambda b, i: (0, b * (S // tm) + i, 0)),
                   tok(LANES),
                   pl.BlockSpec((ROW_TILE, tm), lambda b, i: (0, b * (S // tm) + i)),
                   pl.BlockSpec((1, LANES), lambda b, i: (0, 0))),
        scratch_shapes=[pltpu.VMEM((1, LANES), F32)],
        compiler_params=_cparams(("arbitrary", "arbitrary")),
    )(yf, oh, gf, gh, x, g1, sc2, sh2, wuf, wuh, wo, ln_g, ln_b, wr, br)


def _sc_mesh():
    return plsc.VectorSubcoreMesh(core_axis_name="core", subcore_axis_name="subcore")


def _sc_pipeline(body, grid, in_specs, out_specs):
    return pltpu.emit_pipeline(body, grid=grid, in_specs=in_specs, out_specs=out_specs,
                               core_axis_name=("core", "subcore"),
                               dimension_semantics=(pltpu.PARALLEL,) * len(grid))


def _sc_scatter_rows(src, rows_a, rows_b, n_out):
    nj, t = rows_a.shape
    win = SC_WINDOW
    nc = t // win

    @pl.kernel(out_type=jax.ShapeDtypeStruct((n_out, LANES), src.dtype), mesh=_sc_mesh(), scratch_types=[])
    def scatter(x_hbm, a_hbm, b_hbm, o_hbm):
        def body(x_vmem, a_vmem, b_vmem):
            pltpu.sync_copy(x_vmem, o_hbm.at[a_vmem.at[0]])
            pltpu.sync_copy(x_vmem, o_hbm.at[b_vmem.at[0]])

        idx = pl.BlockSpec((1, win), lambda j, c: (j, c))
        _sc_pipeline(body, (nj, nc), [pl.BlockSpec((win, LANES), lambda j, c: (j * nc + c, 0)), idx, idx],
                     [])(x_hbm, a_hbm, b_hbm)

    return scatter(src, rows_a, rows_b)


def _sc_gather_rows(table, rows):
    nr, t = rows.shape
    win = SC_WINDOW
    nc = t // win

    @pl.kernel(out_type=jax.ShapeDtypeStruct((nr * t, LANES), table.dtype), mesh=_sc_mesh(), scratch_types=[])
    def gather(x_hbm, i_hbm, o_hbm):
        def body(i_vmem, o_vmem):
            pltpu.sync_copy(x_hbm.at[i_vmem.at[0]], o_vmem)

        _sc_pipeline(body, (nr, nc), [pl.BlockSpec((1, win), lambda r, c: (r, c))],
                     [pl.BlockSpec((win, LANES), lambda r, c: (r * nc + c, 0))])(i_hbm, o_hbm)

    return gather(table, rows)


SC_LANES = 16


def _sc_pack_weights(w):
    e, k, n = w.shape
    flat = w.reshape(e * k, n)
    half = k // 2
    win = LANES
    hb = half // win
    nblk = e * hb
    ncol = n // LANES

    @pl.kernel(out_type=jax.ShapeDtypeStruct((e * half, n), F32), mesh=_sc_mesh(), scratch_types=[],
               compiler_params=pltpu.CompilerParams(needs_layout_passes=False, use_tc_tiling_on_sc=True))
    def pack(w_hbm, o_hbm):
        def body(a_vmem, b_vmem, o_vmem):
            @pl.loop(0, win)
            def _(r):
                @pl.loop(0, LANES, step=SC_LANES)
                def _(c):
                    a = lax.bitcast_convert_type(a_vmem[r, pl.ds(c, SC_LANES)], jnp.uint32)
                    b = lax.bitcast_convert_type(b_vmem[r, pl.ds(c, SC_LANES)], jnp.uint32)
                    lo = (a + jnp.uint32(0x8000)) >> 16
                    hi = (b + jnp.uint32(0x8000)) & jnp.uint32(0xFFFF0000)
                    o_vmem[r, pl.ds(c, SC_LANES)] = lax.bitcast_convert_type(lo | hi, F32)

        _sc_pipeline(body, (nblk, ncol),
                     [pl.BlockSpec((win, LANES), lambda g, c: ((g // hb) * 2 * hb + g % hb, c)),
                      pl.BlockSpec((win, LANES), lambda g, c: ((g // hb) * 2 * hb + hb + g % hb, c))],
                     [pl.BlockSpec((win, LANES), lambda g, c: (g, c))])(w_hbm, w_hbm, o_hbm)

    return pack(flat).reshape(e, k // 2, n)


def _unpack_weight(ref):
    w = pltpu.bitcast(ref[...], jnp.uint32)
    return jnp.concatenate([pltpu.bitcast(w << 16, F32), pltpu.bitcast(w & jnp.uint32(0xFFFF0000), F32)],
                           axis=0).astype(BF16)


def _experts_kernel(te_ref, tn_ref, tb_ref, x_ref, wg_ref, wu_ref, wd_ref, o_ref):
    del tb_ref
    nrows = tn_ref[pl.program_id(0)]

    @pl.when(nrows > 0)
    def _():
        x = _load_chunks(x_ref)
        x = jnp.where(lax.broadcasted_iota(jnp.int32, x.shape, 0) < nrows, x, 0.0).astype(BF16)
        g = jnp.dot(x, _unpack_weight(wg_ref), preferred_element_type=F32)
        u = jnp.dot(x, _unpack_weight(wu_ref), preferred_element_type=F32)
        hid = (_silu(g) * u).astype(BF16)
        _store_chunks(o_ref, jnp.dot(hid, _unpack_weight(wd_ref), preferred_element_type=F32))


def _experts(tile_expert, tile_rows, tile_block, xs, wg, wu, wd, tm):
    E, DH, FF = wg.shape
    D = 2 * DH
    dt = D // WORD_LANES
    ntiles = tile_expert.shape[0]
    rows = pl.BlockSpec((dt, tm, LANES), lambda i, te, tn, tb: (0, tb[i], 0))
    grid_spec = pltpu.PrefetchScalarGridSpec(
        num_scalar_prefetch=3,
        grid=(ntiles,),
        in_specs=[rows,
                  pl.BlockSpec((None, DH, FF), lambda i, te, tn, tb: (te[i], 0, 0)),
                  pl.BlockSpec((None, DH, FF), lambda i, te, tn, tb: (te[i], 0, 0)),
                  pl.BlockSpec((None, FF // 2, D), lambda i, te, tn, tb: (te[i], 0, 0))],
        out_specs=rows,
    )
    return pl.pallas_call(
        _experts_kernel,
        out_shape=jax.ShapeDtypeStruct((dt, ntiles * tm, LANES), F32),
        grid_spec=grid_spec,
        compiler_params=_cparams(("arbitrary",)),
    )(tile_expert, tile_rows, tile_block, xs, wg, wu, wd)


def _combine_kernel(yg_ref, x1_ref, ri_ref, g2_ref, lg_ref, lb_ref, o_ref, *, alpha):
    ri = ri_ref[...]
    y = ri[:, 0:1] * _load_chunks(yg_ref.at[0]) + ri[:, 1:2] * _load_chunks(yg_ref.at[1])
    o_ref[...] = _layer_norm(alpha * x1_ref[...] + g2_ref[...] * y, lg_ref[...], lb_ref[...])


def _combine(yg, x1, rinfo, g2, ln_g, ln_b, alpha, tm=1024):
    B, S, D = x1.shape
    nb = S // tm
    return pl.pallas_call(
        functools.partial(_combine_kernel, alpha=alpha),
        out_shape=jax.ShapeDtypeStruct((B, S, D), F32),
        grid=(B, nb),
        in_specs=[pl.BlockSpec((2, D // WORD_LANES, tm, LANES), lambda b, i: (0, 0, b * nb + i, 0)),
                  pl.BlockSpec((None, tm, D), lambda b, i: (b, i, 0)),
                  pl.BlockSpec((None, tm, LANES), lambda b, i: (b, i, 0)),
                  pl.BlockSpec((None, 1, D), lambda b, i: (b, 0, 0)),
                  pl.BlockSpec((1, D), lambda b, i: (0, 0)),
                  pl.BlockSpec((1, D), lambda b, i: (0, 0))],
        out_specs=pl.BlockSpec((None, tm, D), lambda b, i: (b, i, 0)),
        compiler_params=_cparams(("parallel", "parallel")),
    )(yg, x1, rinfo, g2, ln_g, ln_b)


def kernel(x, c, w_ada, b_ada, w_in, b_fox_forget, hgrn_lb_logits, hgrn_norm_w, w_up_fox, w_up_hgrn, w_out,
           ln1_g, ln1_b, w_router_group, b_router_group, w_router_expert, b_router_expert,
           w_expert_gate, w_expert_up, w_expert_down, ln2_g, ln2_b):
    B, S, D = x.shape
    depth = w_ada.shape[0]
    assert depth == 1, "single-layer block"
    fox_heads = b_fox_forget.shape[1]
    fox_w = fox_heads * HEAD_DIM
    hgrn_w = hgrn_norm_w.shape[1]
    ngroups = w_router_group.shape[2]
    nexp = w_router_expert.shape[2]
    nper = nexp // ngroups
    alpha = (2 * depth) ** 0.25
    T = B * S

    ada = _ada(c, w_ada[0], b_ada[0])
    sh1, sc1, g1, sh2, sc2, g2 = [a.reshape(B, 1, D) for a in jnp.split(ada, 6, axis=-1)]

    wi = w_in[0]
    o_ff = 3 * fox_w
    w_fox = jnp.pad(wi[:, :o_ff + fox_heads], ((0, 0), (0, LANES - fox_heads))).astype(BF16)
    w_rest = wi[:, o_ff + fox_heads:].astype(BF16)
    widths = [fox_w, fox_w, fox_w, LANES, hgrn_w, hgrn_w, hgrn_w, hgrn_w, D, D]
    segs, off = [], 0
    for n, w in enumerate(widths):
        if n == 4:
            off = 0
        segs.append((off, off + w))
        off += w
    fq, fk, fv, ffp, hq, hf, hi, hg, gf, gh = _inproj(x, sc1, sh1, w_fox, w_rest, segs)

    bias_p = jnp.zeros((1, LANES), F32).at[0, :fox_heads].set(b_fox_forget[0])
    cum = _foxcum(ffp, bias_p)
    y_fox = _fox(fq, fk, fv, cum)

    o_h = _hgrn(hq, hf, hi, hg, hgrn_lb_logits, hgrn_norm_w[0])

    wr = jnp.zeros((D, LANES), F32).at[:, :ngroups].set(w_router_group[0]).at[:, ngroups:ngroups + nexp].set(
        w_router_expert[0])
    wr_hi = lax.bitcast_convert_type(lax.bitcast_convert_type(wr, jnp.uint32) & jnp.uint32(0xFFFF0000), F32)
    wr = jnp.concatenate([wr_hi.astype(BF16), (wr - wr_hi).astype(BF16)], axis=1)
    br = jnp.zeros((1, LANES), F32).at[0, :ngroups].set(b_router_group[0]).at[0, ngroups:ngroups + nexp].set(
        b_router_expert[0])
    x1, h2, rinfo, fields, counts = _mix(
        y_fox, o_h, gf, gh, x, g1, sc2, sh2,
        w_up_fox[0].astype(BF16), w_up_hgrn[0].astype(BF16), w_out[0].astype(BF16),
        ln1_g[0].reshape(1, D), ln1_b[0].reshape(1, D), wr, br, alpha, ngroups, nper)

    tm_e = 512
    dt = D // WORD_LANES
    ntiles = (2 * T) // tm_e + nexp
    nslots = ntiles * tm_e
    cnt = counts[0, :nexp].astype(jnp.int32)
    padded = ((cnt + tm_e - 1) // tm_e) * tm_e
    ends = jnp.cumsum(padded)
    starts = ends - padded
    eid = fields[2:4].astype(jnp.int32)
    rank = fields[4:6].astype(jnp.int32)
    first = jnp.sum(jnp.where(eid[None] == jnp.arange(nexp, dtype=jnp.int32)[:, None, None],
                              starts[:, None, None], 0), axis=0)
    pos = first + rank
    tile_start = jnp.arange(ntiles, dtype=jnp.int32) * tm_e
    tile_block = jnp.minimum(jnp.arange(ntiles, dtype=jnp.int32), ends[-1] // tm_e - 1)
    tile_expert = jnp.minimum(jnp.sum((tile_start[:, None] >= ends[None, :]).astype(jnp.int32), axis=1), nexp - 1)
    tile_rows = jnp.clip(starts[tile_expert] + cnt[tile_expert] - tile_start, 0, tm_e)
    tile_expert = tile_expert[tile_block]
    rows = pos[:, None, :] + (jnp.arange(dt, dtype=jnp.int32) * nslots)[None, :, None]

    expert_w = [_sc_pack_weights(w[0]) for w in (w_expert_gate, w_expert_up, w_expert_down)]
    xs = _sc_scatter_rows(h2.reshape(dt * T, LANES), rows[0], rows[1], dt * nslots)
    ys = _experts(tile_expert, tile_rows, tile_block, xs.reshape(dt, nslots, LANES), *expert_w, tm_e)
    yg = _sc_gather_rows(ys.reshape(dt * nslots, LANES), rows.reshape(2 * dt, T))
    return _combine(yg.reshape(2, dt, T, LANES), x1, rinfo, g2,
                    ln2_g[0].reshape(1, D), ln2_b[0].reshape(1, D), alpha)
```

```python
import functools

import jax
import jax.numpy as jnp
from jax import lax
from jax.experimental import pallas as pl
from jax.experimental.pallas import tpu as pltpu
from jax.experimental.pallas import tpu_sc as plsc

F32 = jnp.float32
BF16 = jnp.bfloat16

LANES = 128
HEAD_DIM = 64
LN_EPS = 1e-5
RMS_EPS = 1e-6
LOG2E = 1.4426950408889634
NEG_BIG = -1e30
HCHUNK = 16
HBLOCK = 64
HGRN_SAFE_EXP = 60.0
ROW_TILE = 8
WORD_LANES = 2 * LANES
SC_WINDOW = 256
VMEM_LIMIT = 56 * 1024 * 1024


def _cparams(sem, vmem=VMEM_LIMIT):
    return pltpu.CompilerParams(dimension_semantics=sem, vmem_limit_bytes=vmem)


def _sigmoid(x):
    return 0.5 * jnp.tanh(0.5 * x) + 0.5


def _silu(x):
    return x * _sigmoid(x)


def _bf16_pieces(x, n):
    pieces = []
    for _ in range(n):
        top = pltpu.bitcast(pltpu.bitcast(x, jnp.uint32) & jnp.uint32(0xFFFF0000), F32)
        pieces.append(top.astype(BF16))
        x = x - top
    return pieces


def _exact_matrix_dot(m, x):
    r = jnp.dot(m, jnp.concatenate(_bf16_pieces(x, 3), axis=1), preferred_element_type=F32)
    return r[:, :LANES] + r[:, LANES:2 * LANES] + r[:, 2 * LANES:]


def _ada_kernel(c_ref, w_ref, b_ref, o_ref):
    c_hi, c_lo = _bf16_pieces(_silu(c_ref[...]), 2)
    w_hi, w_lo = _bf16_pieces(w_ref[...], 2)
    o_ref[...] = (jnp.dot(c_hi, w_hi, preferred_element_type=F32) + jnp.dot(c_hi, w_lo, preferred_element_type=F32)
                  + jnp.dot(c_lo, w_hi, preferred_element_type=F32)) + b_ref[...]


def _ada(c, w_ada, b_ada):
    B, D = c.shape
    N = w_ada.shape[1]
    tn = 1024
    return pl.pallas_call(
        _ada_kernel,
        out_shape=jax.ShapeDtypeStruct((B, N), F32),
        grid=(N // tn,),
        in_specs=[pl.BlockSpec((B, D), lambda j: (0, 0)),
                  pl.BlockSpec((D, tn), lambda j: (0, j)),
                  pl.BlockSpec((1, tn), lambda j: (0, j))],
        out_specs=pl.BlockSpec((B, tn), lambda j: (0, j)),
        compiler_params=_cparams(("arbitrary",)),
    )(c, w_ada, b_ada.reshape(1, N))


N_FOX_SEGS = 4
SILU_SEGS = (4, 7)
SIGMOID_SEGS = (8, 9)


def _inproj_kernel(x_ref, sc_ref, sh_ref, wf_ref, wr_ref,
                   fq_ref, fk_ref, fv_ref, ff_ref, hq_ref, hf_ref, hi_ref, hg_ref, gf_ref, gh_ref,
                   *, segs, q_scale):
    h = (x_ref[...] * (1.0 + sc_ref[...]) + sh_ref[...]).astype(BF16)
    outs = (fq_ref, fk_ref, fv_ref, ff_ref, hq_ref, hf_ref, hi_ref, hg_ref, gf_ref, gh_ref)
    for idx, (o_ref, (a, b)) in enumerate(zip(outs, segs)):
        w_ref = wf_ref if idx < N_FOX_SEGS else wr_ref
        r = jnp.dot(h, w_ref[:, a:b], preferred_element_type=F32)
        if idx == 0:
            r = r * q_scale
        elif idx in SILU_SEGS:
            r = _silu(r)
        elif idx in SIGMOID_SEGS:
            r = _sigmoid(r)
        o_ref[...] = r.astype(o_ref.dtype)


def _inproj(x, sc1, sh1, w_fox, w_rest, segs, tm=256):
    B, S, D = x.shape
    widths = [b - a for a, b in segs]
    dtypes = [BF16, BF16, BF16, F32, BF16, F32, BF16, BF16, BF16, BF16]
    out_shape = tuple(jax.ShapeDtypeStruct((B, S, w), dt) for w, dt in zip(widths, dtypes))
    out_specs = tuple(pl.BlockSpec((None, tm, w), lambda b, i: (b, i, 0)) for w in widths)
    vec = pl.BlockSpec((None, 1, D), lambda b, i: (b, 0, 0))
    return pl.pallas_call(
        functools.partial(_inproj_kernel, segs=tuple(segs), q_scale=HEAD_DIM ** -0.5 * LOG2E),
        out_shape=out_shape,
        grid=(B, S // tm),
        in_specs=[pl.BlockSpec((None, tm, D), lambda b, i: (b, i, 0)), vec, vec,
                  pl.BlockSpec(w_fox.shape, lambda b, i: (0, 0)),
                  pl.BlockSpec(w_rest.shape, lambda b, i: (0, 0))],
        out_specs=out_specs,
        compiler_params=_cparams(("parallel", "parallel")),
    )(x, sc1, sh1, w_fox, w_rest)


def _foxcum_kernel(ff_ref, b_ref, o_ref, *, blk):
    S = ff_ref.shape[0]
    r = lax.broadcasted_iota(jnp.int32, (blk, blk), 0)
    c = lax.broadcasted_iota(jnp.int32, (blk, blk), 1)
    lower = jnp.where(r >= c, 1.0, 0.0).astype(BF16)
    carry = jnp.zeros((1, LANES), F32)
    for j in range(S // blk):
        z = ff_ref[j * blk:(j + 1) * blk, :] + b_ref[...]
        lf = jnp.minimum(z, 0.0) - jnp.log(1.0 + jnp.exp(-jnp.abs(z)))
        cum = _exact_matrix_dot(lower, lf) + carry
        o_ref[j * blk:(j + 1) * blk, :] = cum * LOG2E
        carry = cum[blk - 1:blk, :]


def _foxcum(ffp, bias_p, blk=256):
    B, S, _ = ffp.shape
    return pl.pallas_call(
        functools.partial(_foxcum_kernel, blk=blk),
        out_shape=jax.ShapeDtypeStruct((B, S, LANES), F32),
        grid=(B,),
        in_specs=[pl.BlockSpec((None, S, LANES), lambda b: (b, 0, 0)),
                  pl.BlockSpec((1, LANES), lambda b: (0, 0))],
        out_specs=pl.BlockSpec((None, S, LANES), lambda b: (b, 0, 0)),
        compiler_params=_cparams(("parallel",)),
    )(ffp, bias_p)


NCUM = 3


def _fox_kernel(q_ref, k_ref, v_ref, c_ref, wg_ref, wu_ref, wd_ref, o_ref, wgb_ref, wub_ref, wdb_ref,
                ka_sc, kb_sc, va_sc, vb_sc, *, tq, tk):
    wgb_ref[...] = wg_ref[...].astype(BF16)
    wub_ref[...] = wu_ref[...].astype(BF16)
    wdb_ref[...] = wd_ref[...].astype(BF16)

    p = pl.program_id(1)
    qi = pl.program_id(2)
    S = k_ref.shape[0]

    @pl.when(qi == 0)
    def _():
        lane = lax.broadcasted_iota(jnp.int32, (S, LANES), 1)
        rr = lax.broadcasted_iota(jnp.int32, (LANES, LANES), 0)
        cc = lax.broadcasted_iota(jnp.int32, (LANES, LANES), 1)
        rest = c_ref[...]
        placed = jnp.zeros((S, LANES), F32)
        for i in range(NCUM):
            piece = rest.astype(BF16)
            rest = rest - piece.astype(F32)
            sel = ((rr == 2 * p) & (cc == HEAD_DIM + i)) | ((rr == 2 * p + 1) & (cc == i))
            placed = placed + jnp.dot(piece, jnp.where(sel, 1.0, 0.0).astype(BF16), preferred_element_type=F32)
        k2 = k_ref[...].astype(F32)
        ka_sc[...] = jnp.where(lane < HEAD_DIM, k2, -placed).astype(BF16)
        kb_sc[...] = jnp.where(lane >= HEAD_DIM, k2, -placed).astype(BF16)
        vt = v_ref[...].astype(F32).T
        row = lax.broadcasted_iota(jnp.int32, (LANES, S), 0)
        va_sc[...] = jnp.where(row < HEAD_DIM, vt, jnp.where(row == HEAD_DIM, 1.0, 0.0)).astype(BF16)
        vb_sc[...] = jnp.where(row >= HEAD_DIM, vt, jnp.where(row == 0, 1.0, 0.0)).astype(BF16)

    q2 = q_ref[...].astype(F32)
    qlane = lax.broadcasted_iota(jnp.int32, (tq, LANES), 1)
    qa = jnp.where(qlane < HEAD_DIM, q2, jnp.where(qlane < HEAD_DIM + NCUM, 1.0, 0.0)).astype(BF16)
    qb = jnp.where(qlane >= HEAD_DIM, q2, jnp.where(qlane < NCUM, 1.0, 0.0)).astype(BF16)
    nsub = tq // tk

    def block(k0, carry, diag_off):
        q0 = 0 if diag_off is None else diag_off
        out = []
        for ksc, vsc, qh, (m, acc) in ((ka_sc, va_sc, qa, carry[:2]), (kb_sc, vb_sc, qb, carry[2:])):
            st = lax.dot_general(ksc[pl.ds(k0, tk), :], qh[q0:, :], (((1,), (1,)), ((), ())),
                                 preferred_element_type=F32)
            if diag_off is not None:
                st = jnp.where(lax.broadcasted_iota(jnp.int32, st.shape, 0)
                               <= lax.broadcasted_iota(jnp.int32, st.shape, 1), st, NEG_BIG)
            m_old = m[:, q0:]
            m_new = jnp.maximum(m_old, jnp.max(st, axis=0, keepdims=True))
            pt = jnp.exp2(st - m_new).astype(BF16)
            acc_new = (jnp.exp2(m_old - m_new) * acc[:, q0:]
                       + jnp.dot(vsc[:, pl.ds(k0, tk)], pt, preferred_element_type=F32))
            if q0:
                m_new = jnp.concatenate([m[:, :q0], m_new], axis=1)
                acc_new = jnp.concatenate([acc[:, :q0], acc_new], axis=1)
            out += [m_new, acc_new]
        return tuple(out)

    def group(j, carry):
        k0 = pl.multiple_of(j * (nsub * tk), nsub * tk)
        for u in range(nsub):
            carry = block(k0 + u * tk, carry, None)
        return carry

    m0 = jnp.full((1, tq), NEG_BIG, F32)
    a0 = jnp.zeros((LANES, tq), F32)
    carry = lax.fori_loop(0, qi, group, (m0, a0, m0, a0))
    for d in range(nsub):
        carry = block(pl.multiple_of(qi * tq + d * tk, tk), carry, d * tk)
    _, aa, _, ab = carry
    row = lax.broadcasted_iota(jnp.int32, (LANES, tq), 0)
    ot = jnp.where(row < HEAD_DIM, aa * (1.0 / aa[HEAD_DIM:HEAD_DIM + 1, :]), ab * (1.0 / ab[0:1, :]))
    o_ref[...] = ot.T.astype(o_ref.dtype)


def _fox(fq, fk, fv, cum, expert_w, tq=2048, tk=512):
    B, S, W = fq.shape
    tq = min(tq, S)
    assert tq % tk == 0 and S % tq == 0
    npairs = W // LANES
    nq = S // tq
    nsteps = B * npairs * nq
    nexp = expert_w[0].shape[0]
    assert nexp % nsteps == 0
    eb = nexp // nsteps
    wspec = lambda w: pl.BlockSpec((eb,) + w.shape[1:], lambda b, p, i: ((b * npairs + p) * nq + i, 0, 0))
    return pl.pallas_call(
        functools.partial(_fox_kernel, tq=tq, tk=tk),
        out_shape=(jax.ShapeDtypeStruct((B, S, W), BF16),) + tuple(
            jax.ShapeDtypeStruct(w.shape, BF16) for w in expert_w),
        grid=(B, npairs, nq),
        in_specs=[pl.BlockSpec((None, tq, LANES), lambda b, p, i: (b, i, p)),
                  pl.BlockSpec((None, S, LANES), lambda b, p, i: (b, 0, p)),
                  pl.BlockSpec((None, S, LANES), lambda b, p, i: (b, 0, p)),
                  pl.BlockSpec((None, S, LANES), lambda b, p, i: (b, 0, 0))] + [wspec(w) for w in expert_w],
        out_specs=(pl.BlockSpec((None, tq, LANES), lambda b, p, i: (b, i, p)),) + tuple(
            wspec(w) for w in expert_w),
        scratch_shapes=[pltpu.VMEM((S, LANES), BF16), pltpu.VMEM((S, LANES), BF16),
                        pltpu.VMEM((LANES, S), BF16), pltpu.VMEM((LANES, S), BF16)],
        compiler_params=_cparams(("parallel", "parallel", "arbitrary")),
    )(fq, fk, fv, cum, *expert_w)


def _hgrn_kernel(hq_ref, hf_ref, hi_ref, hg_ref, lb_ref, nw_ref, o_ref,
                 b_sc, kk_sc, qq_sc, o_sc, w1_sc, w2_sc, w3_sc, w4_sc, w5_sc,
                 p_sc, st16_sc, dec_sc, st64_sc):
    S = hq_ref.shape[0]
    C = HCHUNK
    nchunks = S // C
    BLK = HBLOCK
    nblk = S // BLK

    lg = lb_ref[...]
    e = jnp.exp(lg - jnp.max(lg, axis=0, keepdims=True))
    lb = e[0:1, :] / jnp.sum(e, axis=0, keepdims=True)

    f = lb + (1.0 - lb) * (1.0 / (1.0 + jnp.exp(-hf_ref[...])))
    lf = jnp.log(f)
    kk_sc[...] = 1.0 - f
    qq_sc[...] = hq_ref[...].astype(F32)

    row = lax.broadcasted_iota(jnp.int32, (S, LANES), 0)
    rb = 4 * BLK
    tr = lax.broadcasted_iota(jnp.int32, (rb, rb), 0)
    tc = lax.broadcasted_iota(jnp.int32, (rb, rb), 1)
    tri = jnp.where(((tr & -BLK) == (tc & -BLK)) & (tc <= tr), 1.0, 0.0).astype(BF16)
    lf3 = jnp.concatenate(_bf16_pieces(lf, 3), axis=1)
    for j in range(S // rb):
        c3 = jnp.dot(tri, lf3[j * rb:(j + 1) * rb, :], preferred_element_type=F32)
        b_sc[j * rb:(j + 1) * rb, :] = c3[:, :LANES] + c3[:, LANES:2 * LANES] + c3[:, 2 * LANES:]
    safe = jnp.max(-b_sc[...].reshape(nblk, BLK, LANES)[:, BLK - 1, :]) <= HGRN_SAFE_EXP

    lane = lax.broadcasted_iota(jnp.int32, (C, LANES), 1)
    sr = lax.broadcasted_iota(jnp.int32, (LANES, LANES), 0)
    scn = lax.broadcasted_iota(jnp.int32, (LANES, LANES), 1)
    same_head = (sr // HEAD_DIM) == (scn // HEAD_DIM)

    @pl.when(safe)
    def _factorised():
        qh_sc, kh_sc, ke_sc, qd_sc, k2_sc = w1_sc, w2_sc, w3_sc, w4_sc, w5_sc
        SB = 2 * BLK
        nsb = S // SB
        bb = b_sc[...]
        dblk = jnp.exp(bb.reshape(nblk, BLK, LANES)[:, BLK - 1:BLK, :])
        dfull = jnp.broadcast_to(dblk, (nblk, BLK, LANES)).reshape(S, LANES)
        second = (row & BLK) != 0
        d_prev = pltpu.roll(dfull, BLK, axis=0)
        d_next = pltpu.roll(dfull, S - BLK, axis=0)
        qh = qq_sc[...] * jnp.exp(bb)
        qh_sc[...] = qh.astype(BF16)
        qd_sc[...] = (qh * jnp.where(second, d_prev, 1.0)).astype(BF16)
        kh = kk_sc[...] * jnp.exp(-bb)
        kh_sc[...] = kh.astype(BF16)
        ke = kh * dfull
        ke_sc[...] = ke.astype(BF16)
        k2_sc[...] = (ke * jnp.where(second, 1.0, d_next)).astype(BF16)
        d3 = dfull.reshape(nsb, SB, LANES)
        dec_sc[pl.ds(0, nsb), :] = d3[:, 0, :] * d3[:, BLK, :]
        unroll = min(16, nsb)
        assert nsb % unroll == 0
        tn = (((0,), (0,)), ((), ()))
        nt = (((1,), (1,)), ((), ()))

        def scan(g, st):
            for u in range(unroll):
                i = g * unroll + u
                r0 = pl.multiple_of(i * SB, SB)
                st64_sc[i] = st.astype(BF16)
                upd = lax.dot_general(hi_ref[pl.ds(r0, SB), :], k2_sc[pl.ds(r0, SB), :], tn,
                                      preferred_element_type=F32)
                st = st * dec_sc[pl.ds(i, 1), :] + jnp.where(same_head, upd, 0.0)
            return st

        lax.fori_loop(0, nsb // unroll, scan, jnp.zeros((LANES, LANES), F32))

        r = lax.broadcasted_iota(jnp.int32, (2 * SB, 2 * SB), 0)
        c = lax.broadcasted_iota(jnp.int32, (2 * SB, 2 * SB), 1)
        t = r & (SB - 1)
        visible = (((c < SB) & ((t & BLK) == (c & BLK)) & ((t & (BLK - 1)) >= (c & (BLK - 1))))
                   | ((c >= SB) & (c < SB + BLK) & (t >= BLK)))
        plane = lax.broadcasted_iota(jnp.int32, (SB, LANES), 1)
        pad = jnp.zeros((BLK, LANES), BF16)

        def readout(g, _):
            for u in range(unroll):
                i = g * unroll + u
                r0 = pl.multiple_of(i * SB, SB)
                vb = hi_ref[pl.ds(r0, SB), :]
                qh2 = qh_sc[pl.ds(r0, SB), :]
                q2 = jnp.concatenate([jnp.where(plane < HEAD_DIM, qh2, jnp.zeros_like(qh2)),
                                      jnp.where(plane >= HEAD_DIM, qh2, jnp.zeros_like(qh2))], axis=0)
                kext = jnp.concatenate([kh_sc[pl.ds(r0, SB), :], ke_sc[pl.ds(r0, BLK), :], pad], axis=0)
                vext = jnp.concatenate([vb, vb[:BLK], pad], axis=0)
                sc = lax.dot_general(q2, kext, nt, preferred_element_type=F32)
                sc = jnp.where(visible, sc, 0.0).astype(BF16)
                out = jnp.dot(sc, vext, preferred_element_type=F32)
                o_inter = lax.dot_general(qd_sc[pl.ds(r0, SB), :], st64_sc[i], nt, preferred_element_type=F32)
                o_sc[pl.ds(r0, SB), :] = jnp.where(plane < HEAD_DIM, out[:SB], out[SB:]) + o_inter
            return 0

        lax.fori_loop(0, nsb // unroll, readout, 0)

    @pl.when(jnp.logical_not(safe))
    def _direct():
        qt_sc, kt_sc, s_sc, a2_sc = w1_sc, w2_sc, w3_sc, b_sc
        bb = b_sc[...]
        cl = jnp.broadcast_to(bb.reshape(nchunks, C, LANES)[:, C - 1:C, :], (nchunks, C, LANES)).reshape(S, LANES)
        aa = bb - jnp.where((row & (BLK - 1)) >= C, pltpu.roll(cl, C, axis=0), 0.0)
        al = jnp.broadcast_to(aa.reshape(nchunks, C, LANES)[:, C - 1:C, :], (nchunks, C, LANES)).reshape(S, LANES)
        qt_sc[...] = (qq_sc[...] * jnp.exp(aa)).astype(BF16)
        kt_sc[...] = (kk_sc[...] * jnp.exp(al - aa)).astype(BF16)
        dec_sc[...] = jnp.exp(aa.reshape(nchunks, C, LANES)[:, C - 1, :])
        a2_sc[...] = aa * LOG2E
        trow = lax.broadcasted_iota(jnp.int32, (C, LANES), 0)

        def gen(c, _):
            r0 = pl.multiple_of(c * C, C)
            ac = a2_sc[pl.ds(r0, C), :]
            qc = qq_sc[pl.ds(r0, C), :]
            kc = kk_sc[pl.ds(r0, C), :]
            half = C // 2
            for s in range(C):
                if s < half:
                    dec = jnp.exp2(jnp.where(trow >= s, ac - ac[s:s + 1, :], NEG_BIG))
                    p = qc * (kc[s:s + 1, :] * dec)
                else:
                    dec = jnp.exp2(jnp.where(trow[half:] >= s, ac[half:] - ac[s:s + 1, :], NEG_BIG))
                    p = jnp.concatenate([jnp.zeros((half, LANES), F32), qc[half:] * (kc[s:s + 1, :] * dec)],
                                        axis=0)
                p_sc[pl.ds(r0, C), s * LANES:(s + 1) * LANES] = p.astype(BF16)
            return 0

        lax.fori_loop(0, nchunks, gen, 0)

        er = lax.broadcasted_iota(jnp.int32, (C * LANES, LANES), 0)
        ec = lax.broadcasted_iota(jnp.int32, (C * LANES, LANES), 1)
        emat = (ec == ((er & (LANES - 1)) // HEAD_DIM) * C + er // LANES).astype(BF16)
        rb = 256

        def red(i, _):
            r0 = pl.multiple_of(i * rb, rb)
            s_sc[pl.ds(r0, rb), :] = jnp.dot(p_sc[pl.ds(r0, rb), :], emat,
                                             preferred_element_type=F32).astype(BF16)
            return 0

        lax.fori_loop(0, S // rb, red, 0)

        unroll = 16
        assert nchunks % unroll == 0

        def scan(g, st):
            for u in range(unroll):
                c = g * unroll + u
                r0 = pl.multiple_of(c * C, C)
                st16_sc[c] = st.astype(BF16)
                upd = lax.dot_general(hi_ref[pl.ds(r0, C), :], kt_sc[pl.ds(r0, C), :],
                                      (((0,), (0,)), ((), ())), preferred_element_type=F32)
                st = st * dec_sc[pl.ds(c, 1), :] + jnp.where(same_head, upd, 0.0)
            return st

        lax.fori_loop(0, nchunks // unroll, scan, jnp.zeros((LANES, LANES), F32))

        def readout(g, _):
            for u in range(unroll):
                c = g * unroll + u
                r0 = pl.multiple_of(c * C, C)
                vc = hi_ref[pl.ds(r0, C), :]
                o_inter = lax.dot_general(qt_sc[pl.ds(r0, C), :], st16_sc[c],
                                          (((1,), (1,)), ((), ())), preferred_element_type=F32)
                v2 = jnp.concatenate([jnp.where(lane < HEAD_DIM, vc, jnp.zeros_like(vc)),
                                      jnp.where(lane >= HEAD_DIM, vc, jnp.zeros_like(vc))], axis=0)
                o_intra = jnp.dot(s_sc[pl.ds(r0, C), :][:, :2 * C], v2, preferred_element_type=F32)
                o_sc[pl.ds(r0, C), :] = o_inter + o_intra
            return 0

        lax.fori_loop(0, nchunks // unroll, readout, 0)

    o = o_sc[...]
    ones_head = jnp.where(same_head, 1.0, 0.0).astype(BF16)
    sq_hi, sq_lo = _bf16_pieces(o * o, 2)
    ms = (jnp.dot(sq_hi, ones_head, preferred_element_type=F32)
          + jnp.dot(sq_lo, ones_head, preferred_element_type=F32)) * (1.0 / HEAD_DIM)
    y = o * lax.rsqrt(ms + RMS_EPS) * nw_ref[...]
    o_ref[...] = (y * hg_ref[...].astype(F32)).astype(o_ref.dtype)


def _hgrn(hq, hf, hi, hg, lb_logits, norm_w):
    B, S, W = hq.shape
    npairs = W // LANES
    nrows = lb_logits.shape[0]
    seq = pl.BlockSpec((None, S, LANES), lambda b, p: (b, 0, p))
    return pl.pallas_call(
        _hgrn_kernel,
        out_shape=jax.ShapeDtypeStruct((B, S, W), BF16),
        grid=(B, npairs),
        in_specs=[seq, seq, seq, seq,
                  pl.BlockSpec((nrows, LANES), lambda b, p: (0, p)),
                  pl.BlockSpec((1, LANES), lambda b, p: (0, p))],
        out_specs=seq,
        scratch_shapes=[pltpu.VMEM((S, LANES), F32),
                        pltpu.VMEM((S, LANES), F32),
                        pltpu.VMEM((S, LANES), F32),
                        pltpu.VMEM((S, LANES), F32),
                        pltpu.VMEM((S, LANES), BF16),
                        pltpu.VMEM((S, LANES), BF16),
                        pltpu.VMEM((S, LANES), BF16),
                        pltpu.VMEM((S, LANES), BF16),
                        pltpu.VMEM((S, LANES), BF16),
                        pltpu.VMEM((S, HCHUNK * LANES), BF16),
                        pltpu.VMEM((S // HCHUNK, LANES, LANES), BF16),
                        pltpu.VMEM((S // HCHUNK, LANES), F32),
                        pltpu.VMEM((S // HBLOCK, LANES, LANES), BF16)],
        compiler_params=_cparams(("parallel", "parallel")),
    )(hq, hf, hi, hg, lb_logits, norm_w.reshape(1, W))


def _layer_norm(v, g, b):
    mu = jnp.mean(v, axis=-1, keepdims=True)
    d = v - mu
    var = jnp.mean(d * d, axis=-1, keepdims=True)
    return d * lax.rsqrt(var + LN_EPS) * g + b


def _bf16_bits(x):
    return (pltpu.bitcast(x, jnp.uint32) + jnp.uint32(0x8000)) & jnp.uint32(0xFFFF0000)


def _store_chunks(ref, val):
    n = ref.shape[0]
    for j in range(n):
        lo = _bf16_bits(val[:, j * LANES:(j + 1) * LANES]) >> 16
        hi = _bf16_bits(val[:, (j + n) * LANES:(j + n + 1) * LANES])
        ref[j] = pltpu.bitcast(lo | hi, F32)


def _load_chunks(ref):
    words = [pltpu.bitcast(ref[j], jnp.uint32) for j in range(ref.shape[0])]
    lo = [pltpu.bitcast(w << 16, F32) for w in words]
    hi = [pltpu.bitcast(w & jnp.uint32(0xFFFF0000), F32) for w in words]
    return jnp.concatenate(lo + hi, axis=1)


def _mix_kernel(yf_ref, oh_ref, gf_ref, gh_ref, x_ref, g1_ref, sc2_ref, sh2_ref,
                wuf_ref, wuh_ref, wo_ref, lg_ref, lbias_ref, wr_ref, br_ref,
                x1_ref, h2_ref, ri_ref, rt_ref, cnt_ref, carry_sc, *, alpha, ngroups, nper):
    first = (pl.program_id(0) == 0) & (pl.program_id(1) == 0)

    @pl.when(first)
    def _():
        carry_sc[...] = jnp.zeros_like(carry_sc)

    tm = x_ref.shape[0]
    yf = jnp.dot(yf_ref[...], wuf_ref[...], preferred_element_type=F32)
    yh = jnp.dot(oh_ref[...], wuh_ref[...], preferred_element_type=F32)
    merged = gf_ref[...].astype(F32) * yf + gh_ref[...].astype(F32) * yh
    y = jnp.dot(merged.astype(BF16), wo_ref[...], preferred_element_type=F32)
    x1 = _layer_norm(alpha * x_ref[...] + g1_ref[...] * y, lg_ref[...], lbias_ref[...])
    x1_ref[...] = x1
    h2 = x1 * (1.0 + sc2_ref[...]) + sh2_ref[...]
    _store_chunks(h2_ref, h2)

    h_hi, h_lo = _bf16_pieces(h2, 2)
    hh = jnp.dot(h_hi, wr_ref[...], preferred_element_type=F32)
    logits = (hh[:, :LANES] + hh[:, LANES:]
              + jnp.dot(h_lo, wr_ref[:, :LANES], preferred_element_type=F32)) + br_ref[...]
    lane = lax.broadcasted_iota(jnp.int32, (tm, LANES), 1)
    big = jnp.int32(1 << 20)

    def argmax_first(vals, mask):
        mx = jnp.max(jnp.where(mask, vals, -jnp.inf), axis=1, keepdims=True)
        idx = jnp.min(jnp.where(mask & (vals == mx), lane, big), axis=1, keepdims=True)
        return mx, idx

    gmask = lane < ngroups
    gmax = jnp.max(jnp.where(gmask, logits, -jnp.inf), axis=1, keepdims=True)
    gexp = jnp.where(gmask, jnp.exp(logits - gmax), 0.0)
    gprob = gexp / jnp.sum(gexp, axis=1, keepdims=True)
    g_w, g_idx = argmax_first(gprob, gmask)

    lo = ngroups + g_idx * nper
    emask = (lane >= lo) & (lane < lo + nper)
    emax = jnp.max(jnp.where(emask, logits, -jnp.inf), axis=1, keepdims=True)
    eexp = jnp.where(emask, jnp.exp(logits - emax), 0.0)
    eprob = eexp / jnp.sum(eexp, axis=1, keepdims=True)
    p0, i0 = argmax_first(eprob, emask)
    p1, i1 = argmax_first(eprob, emask & (lane != i0))
    den = p0 + p1
    w0 = p0 / den * g_w
    w1 = p1 / den * g_w
    e0 = i0 - ngroups
    e1 = i1 - ngroups

    oh = ((lane == e0) | (lane == e1)).astype(F32)
    r = lax.broadcasted_iota(jnp.int32, (tm, tm), 0)
    c = lax.broadcasted_iota(jnp.int32, (tm, tm), 1)
    strict_lower = (c < r).astype(BF16)
    before = jnp.dot(strict_lower, oh.astype(BF16), preferred_element_type=F32) + carry_sc[...]
    rank0 = jnp.sum(jnp.where(lane == e0, before, 0.0), axis=1, keepdims=True)
    rank1 = jnp.sum(jnp.where(lane == e1, before, 0.0), axis=1, keepdims=True)
    carry_sc[...] = carry_sc[...] + jnp.sum(oh, axis=0, keepdims=True)
    cnt_ref[...] = carry_sc[...]

    info = jnp.where(lane == 0, w0, 0.0)
    info = jnp.where(lane == 1, w1, info)
    info = jnp.where(lane == 2, e0.astype(F32), info)
    info = jnp.where(lane == 3, e1.astype(F32), info)
    info = jnp.where(lane == 4, rank0, info)
    info = jnp.where(lane == 5, rank1, info)
    ri_ref[...] = info
    rt_ref[...] = info.T[:ROW_TILE, :]


def _mix(yf, oh, gf, gh, x, g1, sc2, sh2, wuf, wuh, wo, ln_g, ln_b, wr, br, alpha, ngroups, nper, tm=512):
    B, S, D = x.shape
    W = yf.shape[2]
    tok = lambda w: pl.BlockSpec((None, tm, w), lambda b, i: (b, i, 0))
    vec = pl.BlockSpec((None, 1, D), lambda b, i: (b, 0, 0))
    full = lambda a: pl.BlockSpec(a.shape, lambda b, i: (0,) * a.ndim)
    return pl.pallas_call(
        functools.partial(_mix_kernel, alpha=alpha, ngroups=ngroups, nper=nper),
        out_shape=(jax.ShapeDtypeStruct((B, S, D), F32),
                   jax.ShapeDtypeStruct((D // WORD_LANES, B * S, LANES), F32),
                   jax.ShapeDtypeStruct((B, S, LANES), F32),
                   jax.ShapeDtypeStruct((ROW_TILE, B * S), F32),
                   jax.ShapeDtypeStruct((1, LANES), F32)),
        grid=(B, S // tm),
        in_specs=[tok(W), tok(W), tok(D), tok(D), tok(D), vec, vec, vec,
                  full(wuf), full(wuh), full(wo), full(ln_g), full(ln_b), full(wr), full(br)],
        out_specs=(tok(D),
                   pl.BlockSpec((D // WORD_LANES, tm, LANES), lambda b, i: (0, b * (S // tm) + i, 0)),
                   tok(LANES),
                   pl.BlockSpec((ROW_TILE, tm), lambda b, i: (0, b * (S // tm) + i)),
                   pl.BlockSpec((1, LANES), lambda b, i: (0, 0))),
        scratch_shapes=[pltpu.VMEM((1, LANES), F32)],
        compiler_params=_cparams(("arbitrary", "arbitrary")),
    )(yf, oh, gf, gh, x, g1, sc2, sh2, wuf, wuh, wo, ln_g, ln_b, wr, br)


def _sc_mesh():
    return plsc.VectorSubcoreMesh(core_axis_name="core", subcore_axis_name="subcore")


def _sc_pipeline(body, grid, in_specs, out_specs):
    return pltpu.emit_pipeline(body, grid=grid, in_specs=in_specs, out_specs=out_specs,
                               core_axis_name=("core", "subcore"),
                               dimension_semantics=(pltpu.PARALLEL,) * len(grid))


def _sc_scatter_rows(src, rows_a, rows_b, n_out):
    nj, t = rows_a.shape
    win = SC_WINDOW
    nc = t // win

    @pl.kernel(out_type=jax.ShapeDtypeStruct((n_out, LANES), src.dtype), mesh=_sc_mesh(), scratch_types=[])
    def scatter(x_hbm, a_hbm, b_hbm, o_hbm):
        def body(x_vmem, a_vmem, b_vmem):
            pltpu.sync_copy(x_vmem, o_hbm.at[a_vmem.at[0]])
            pltpu.sync_copy(x_vmem, o_hbm.at[b_vmem.at[0]])

        idx = pl.BlockSpec((1, win), lambda j, c: (j, c))
        _sc_pipeline(body, (nj, nc), [pl.BlockSpec((win, LANES), lambda j, c: (j * nc + c, 0)), idx, idx],
                     [])(x_hbm, a_hbm, b_hbm)

    return scatter(src, rows_a, rows_b)


def _sc_gather_rows(table, rows):
    nr, t = rows.shape
    win = SC_WINDOW
    nc = t // win

    @pl.kernel(out_type=jax.ShapeDtypeStruct((nr * t, LANES), table.dtype), mesh=_sc_mesh(), scratch_types=[])
    def gather(x_hbm, i_hbm, o_hbm):
        def body(i_vmem, o_vmem):
            pltpu.sync_copy(x_hbm.at[i_vmem.at[0]], o_vmem)

        _sc_pipeline(body, (nr, nc), [pl.BlockSpec((1, win), lambda r, c: (r, c))],
                     [pl.BlockSpec((win, LANES), lambda r, c: (r * nc + c, 0))])(i_hbm, o_hbm)

    return gather(table, rows)


W_SLOTS = 3


def _experts_kernel(tn_ref, tb_ref, run_ref, first_ref, rexp_ref, nrun_ref,
                    x_ref, wg_hbm, wu_hbm, wd_hbm, o_ref, wg_sc, wu_sc, wd_sc, sems):
    del tb_ref
    i = pl.program_id(0)
    nrows = tn_ref[i]
    run = run_ref[i]
    nruns = nrun_ref[0]

    def copies(r, slot):
        e = rexp_ref[r]
        return [pltpu.make_async_copy(hbm.at[e], buf.at[slot], sems.at[slot])
                for hbm, buf in ((wg_hbm, wg_sc), (wu_hbm, wu_sc), (wd_hbm, wd_sc))]

    def fetch(r):
        for s in range(W_SLOTS):
            @pl.when(r % W_SLOTS == s)
            def _(s=s):
                for cp in copies(r, s):
                    cp.start()

    @pl.when(i == 0)
    def _():
        fetch(0)

        @pl.when(nruns > 1)
        def _():
            fetch(1)

    for s in range(W_SLOTS):
        @pl.when((nrows > 0) & (run % W_SLOTS == s))
        def _(s=s):
            @pl.when(first_ref[i] != 0)
            def _():
                for cp in copies(run, s):
                    cp.wait()

                @pl.when(run + 2 < nruns)
                def _():
                    fetch(run + 2)

            x = _load_chunks(x_ref)
            x = jnp.where(lax.broadcasted_iota(jnp.int32, x.shape, 0) < nrows, x, 0.0).astype(BF16)
            g = jnp.dot(x, wg_sc[s], preferred_element_type=F32)
            u = jnp.dot(x, wu_sc[s], preferred_element_type=F32)
            hid = (_silu(g) * u).astype(BF16)
            _store_chunks(o_ref, jnp.dot(hid, wd_sc[s], preferred_element_type=F32))


def _experts(tile_rows, tile_block, tile_run, tile_first, run_expert, nruns, xs, wg, wu, wd, tm):
    E, D, FF = wg.shape
    dt = D // WORD_LANES
    ntiles = tile_rows.shape[0]
    rows = pl.BlockSpec((dt, tm, LANES), lambda i, tn, tb, *_: (0, tb[i], 0))
    hbm = pl.BlockSpec(memory_space=pl.ANY)
    grid_spec = pltpu.PrefetchScalarGridSpec(
        num_scalar_prefetch=6,
        grid=(ntiles,),
        in_specs=[rows, hbm, hbm, hbm],
        out_specs=rows,
        scratch_shapes=[pltpu.VMEM((W_SLOTS, D, FF), BF16), pltpu.VMEM((W_SLOTS, D, FF), BF16),
                        pltpu.VMEM((W_SLOTS, FF, D), BF16), pltpu.SemaphoreType.DMA((W_SLOTS,))],
    )
    return pl.pallas_call(
        _experts_kernel,
        out_shape=jax.ShapeDtypeStruct((dt, ntiles * tm, LANES), F32),
        grid_spec=grid_spec,
        compiler_params=_cparams(("arbitrary",)),
    )(tile_rows, tile_block, tile_run, tile_first, run_expert, nruns, xs, wg, wu, wd)


def _combine_kernel(yg_ref, x1_ref, ri_ref, g2_ref, lg_ref, lb_ref, o_ref, *, alpha):
    ri = ri_ref[...]
    y = ri[:, 0:1] * _load_chunks(yg_ref.at[0]) + ri[:, 1:2] * _load_chunks(yg_ref.at[1])
    o_ref[...] = _layer_norm(alpha * x1_ref[...] + g2_ref[...] * y, lg_ref[...], lb_ref[...])


def _combine(yg, x1, rinfo, g2, ln_g, ln_b, alpha, tm=1024):
    B, S, D = x1.shape
    nb = S // tm
    return pl.pallas_call(
        functools.partial(_combine_kernel, alpha=alpha),
        out_shape=jax.ShapeDtypeStruct((B, S, D), F32),
        grid=(B, nb),
        in_specs=[pl.BlockSpec((2, D // WORD_LANES, tm, LANES), lambda b, i: (0, 0, b * nb + i, 0)),
                  pl.BlockSpec((None, tm, D), lambda b, i: (b, i, 0)),
                  pl.BlockSpec((None, tm, LANES), lambda b, i: (b, i, 0)),
                  pl.BlockSpec((None, 1, D), lambda b, i: (b, 0, 0)),
                  pl.BlockSpec((1, D), lambda b, i: (0, 0)),
                  pl.BlockSpec((1, D), lambda b, i: (0, 0))],
        out_specs=pl.BlockSpec((None, tm, D), lambda b, i: (b, i, 0)),
        compiler_params=_cparams(("parallel", "parallel")),
    )(yg, x1, rinfo, g2, ln_g, ln_b)


def kernel(x, c, w_ada, b_ada, w_in, b_fox_forget, hgrn_lb_logits, hgrn_norm_w, w_up_fox, w_up_hgrn, w_out,
           ln1_g, ln1_b, w_router_group, b_router_group, w_router_expert, b_router_expert,
           w_expert_gate, w_expert_up, w_expert_down, ln2_g, ln2_b):
    B, S, D = x.shape
    depth = w_ada.shape[0]
    assert depth == 1, "single-layer block"
    fox_heads = b_fox_forget.shape[1]
    fox_w = fox_heads * HEAD_DIM
    hgrn_w = hgrn_norm_w.shape[1]
    ngroups = w_router_group.shape[2]
    nexp = w_router_expert.shape[2]
    nper = nexp // ngroups
    alpha = (2 * depth) ** 0.25
    T = B * S

    ada = _ada(c, w_ada[0], b_ada[0])
    sh1, sc1, g1, sh2, sc2, g2 = [a.reshape(B, 1, D) for a in jnp.split(ada, 6, axis=-1)]

    wi = w_in[0]
    o_ff = 3 * fox_w
    w_fox = jnp.pad(wi[:, :o_ff + fox_heads], ((0, 0), (0, LANES - fox_heads))).astype(BF16)
    w_rest = wi[:, o_ff + fox_heads:].astype(BF16)
    widths = [fox_w, fox_w, fox_w, LANES, hgrn_w, hgrn_w, hgrn_w, hgrn_w, D, D]
    segs, off = [], 0
    for n, w in enumerate(widths):
        if n == 4:
            off = 0
        segs.append((off, off + w))
        off += w
    fq, fk, fv, ffp, hq, hf, hi, hg, gf, gh = _inproj(x, sc1, sh1, w_fox, w_rest, segs)

    bias_p = jnp.zeros((1, LANES), F32).at[0, :fox_heads].set(b_fox_forget[0])
    cum = _foxcum(ffp, bias_p)
    y_fox, wg_b, wu_b, wd_b = _fox(fq, fk, fv, cum, (w_expert_gate[0], w_expert_up[0], w_expert_down[0]))

    o_h = _hgrn(hq, hf, hi, hg, hgrn_lb_logits, hgrn_norm_w[0])

    wr = jnp.zeros((D, LANES), F32).at[:, :ngroups].set(w_router_group[0]).at[:, ngroups:ngroups + nexp].set(
        w_router_expert[0])
    wr_hi = lax.bitcast_convert_type(lax.bitcast_convert_type(wr, jnp.uint32) & jnp.uint32(0xFFFF0000), F32)
    wr = jnp.concatenate([wr_hi.astype(BF16), (wr - wr_hi).astype(BF16)], axis=1)
    br = jnp.zeros((1, LANES), F32).at[0, :ngroups].set(b_router_group[0]).at[0, ngroups:ngroups + nexp].set(
        b_router_expert[0])
    x1, h2, rinfo, fields, counts = _mix(
        y_fox, o_h, gf, gh, x, g1, sc2, sh2,
        w_up_fox[0].astype(BF16), w_up_hgrn[0].astype(BF16), w_out[0].astype(BF16),
        ln1_g[0].reshape(1, D), ln1_b[0].reshape(1, D), wr, br, alpha, ngroups, nper)

    tm_e = 512
    dt = D // WORD_LANES
    ntiles = (2 * T) // tm_e + nexp
    nslots = ntiles * tm_e
    cnt = counts[0, :nexp].astype(jnp.int32)
    padded = ((cnt + tm_e - 1) // tm_e) * tm_e
    ends = jnp.cumsum(padded)
    starts = ends - padded
    eid = fields[2:4].astype(jnp.int32)
    rank = fields[4:6].astype(jnp.int32)
    first = jnp.sum(jnp.where(eid[None] == jnp.arange(nexp, dtype=jnp.int32)[:, None, None],
                              starts[:, None, None], 0), axis=0)
    pos = first + rank
    tile_start = jnp.arange(ntiles, dtype=jnp.int32) * tm_e
    tile_block = jnp.minimum(jnp.arange(ntiles, dtype=jnp.int32), ends[-1] // tm_e - 1)
    tile_expert = jnp.minimum(jnp.sum((tile_start[:, None] >= ends[None, :]).astype(jnp.int32), axis=1), nexp - 1)
    tile_rows = jnp.clip(starts[tile_expert] + cnt[tile_expert] - tile_start, 0, tm_e)
    used = jnp.cumsum((cnt > 0).astype(jnp.int32))
    nruns = used[-1:]
    run_expert = jnp.sum((used[None, :] <= jnp.arange(nexp + 2, dtype=jnp.int32)[:, None]).astype(jnp.int32), axis=1)
    run_expert = jnp.minimum(run_expert, nexp - 1)
    tile_run = used[tile_expert] - 1
    prev_expert = jnp.concatenate([jnp.full((1,), -1, jnp.int32), tile_expert[:-1]])
    tile_first = ((tile_rows > 0) & (tile_expert != prev_expert)).astype(jnp.int32)
    rows = pos[:, None, :] + (jnp.arange(dt, dtype=jnp.int32) * nslots)[None, :, None]

    xs = _sc_scatter_rows(h2.reshape(dt * T, LANES), rows[0], rows[1], dt * nslots)
    ys = _experts(tile_rows, tile_block, tile_run, tile_first, run_expert, nruns,
                  xs.reshape(dt, nslots, LANES), wg_b, wu_b, wd_b, tm_e)
    yg = _sc_gather_rows(ys.reshape(dt * nslots, LANES), rows.reshape(2 * dt, T))
    return _combine(yg.reshape(2, dt, T, LANES), x1, rinfo, g2,
                    ln2_g[0].reshape(1, D), ln2_b[0].reshape(1, D), alpha)
```

```python
import functools

import jax
import jax.numpy as jnp
from jax import lax
from jax.experimental import pallas as pl
from jax.experimental.pallas import tpu as pltpu
from jax.experimental.pallas import tpu_sc as plsc

F32 = jnp.float32
BF16 = jnp.bfloat16

LANES = 128
HEAD_DIM = 64
LN_EPS = 1e-5
RMS_EPS = 1e-6
LOG2E = 1.4426950408889634
NEG_BIG = -1e30
HCHUNK = 16
HBLOCK = 64
HGRN_SAFE_EXP = 60.0
ROW_TILE = 8
WORD_LANES = 2 * LANES
SC_WINDOW = 256
VMEM_LIMIT = 56 * 1024 * 1024


def _cparams(sem, vmem=VMEM_LIMIT):
    return pltpu.CompilerParams(dimension_semantics=sem, vmem_limit_bytes=vmem)


def _sigmoid(x):
    return 0.5 * jnp.tanh(0.5 * x) + 0.5


def _silu(x):
    return x * _sigmoid(x)


def _bf16_pieces(x, n):
    pieces = []
    for _ in range(n):
        top = pltpu.bitcast(pltpu.bitcast(x, jnp.uint32) & jnp.uint32(0xFFFF0000), F32)
        pieces.append(top.astype(BF16))
        x = x - top
    return pieces


def _exact_matrix_dot(m, x):
    r = jnp.dot(m, jnp.concatenate(_bf16_pieces(x, 3), axis=1), preferred_element_type=F32)
    return r[:, :LANES] + r[:, LANES:2 * LANES] + r[:, 2 * LANES:]


def _ada_kernel(c_ref, w_ref, b_ref, o_ref):
    c_hi, c_lo = _bf16_pieces(_silu(c_ref[...]), 2)
    w_hi, w_lo = _bf16_pieces(w_ref[...], 2)
    o_ref[...] = (jnp.dot(c_hi, w_hi, preferred_element_type=F32) + jnp.dot(c_hi, w_lo, preferred_element_type=F32)
                  + jnp.dot(c_lo, w_hi, preferred_element_type=F32)) + b_ref[...]


def _ada(c, w_ada, b_ada):
    B, D = c.shape
    N = w_ada.shape[1]
    tn = 1024
    return pl.pallas_call(
        _ada_kernel,
        out_shape=jax.ShapeDtypeStruct((B, N), F32),
        grid=(N // tn,),
        in_specs=[pl.BlockSpec((B, D), lambda j: (0, 0)),
                  pl.BlockSpec((D, tn), lambda j: (0, j)),
                  pl.BlockSpec((1, tn), lambda j: (0, j))],
        out_specs=pl.BlockSpec((B, tn), lambda j: (0, j)),
        compiler_params=_cparams(("arbitrary",)),
    )(c, w_ada, b_ada.reshape(1, N))


def _cast_shifted_kernel(a_ref, b_ref, o_ref, *, shift):
    both = jnp.concatenate([a_ref[...], b_ref[...]], axis=1)
    o_ref[...] = both[:, shift:shift + o_ref.shape[1]].astype(BF16)


def _cast_columns(w, start, width, blk=512):
    rows = w.shape[0]
    assert width % blk == 0 and (start - start % LANES) % blk == 0
    base = (start - start % LANES) // blk
    return pl.pallas_call(
        functools.partial(_cast_shifted_kernel, shift=start % LANES),
        out_shape=jax.ShapeDtypeStruct((rows, width), BF16),
        grid=(width // blk,),
        in_specs=[pl.BlockSpec((rows, blk), lambda j: (0, base + j)),
                  pl.BlockSpec((rows, LANES), lambda j: (0, (base + j + 1) * (blk // LANES)))],
        out_specs=pl.BlockSpec((rows, blk), lambda j: (0, j)),
        compiler_params=_cparams(("parallel",)),
    )(w, w)


N_FOX_SEGS = 4
SILU_SEGS = (4, 7)
SIGMOID_SEGS = (8, 9)


def _inproj_kernel(x_ref, sc_ref, sh_ref, wf_ref, wr_ref,
                   fq_ref, fk_ref, fv_ref, ff_ref, hq_ref, hf_ref, hi_ref, hg_ref, gf_ref, gh_ref,
                   *, segs, q_scale):
    h = (x_ref[...] * (1.0 + sc_ref[...]) + sh_ref[...]).astype(BF16)
    outs = (fq_ref, fk_ref, fv_ref, ff_ref, hq_ref, hf_ref, hi_ref, hg_ref, gf_ref, gh_ref)
    for idx, (o_ref, (a, b)) in enumerate(zip(outs, segs)):
        w_ref = wf_ref if idx < N_FOX_SEGS else wr_ref
        r = jnp.dot(h, w_ref[:, a:b], preferred_element_type=F32)
        if idx == 0:
            r = r * q_scale
        elif idx in SILU_SEGS:
            r = _silu(r)
        elif idx in SIGMOID_SEGS:
            r = _sigmoid(r)
        o_ref[...] = r.astype(o_ref.dtype)


def _inproj(x, sc1, sh1, w_fox, w_rest, segs, tm=256):
    B, S, D = x.shape
    widths = [b - a for a, b in segs]
    dtypes = [BF16, BF16, BF16, F32, BF16, F32, BF16, BF16, BF16, BF16]
    out_shape = tuple(jax.ShapeDtypeStruct((B, S, w), dt) for w, dt in zip(widths, dtypes))
    out_specs = tuple(pl.BlockSpec((None, tm, w), lambda b, i: (b, i, 0)) for w in widths)
    vec = pl.BlockSpec((None, 1, D), lambda b, i: (b, 0, 0))
    return pl.pallas_call(
        functools.partial(_inproj_kernel, segs=tuple(segs), q_scale=HEAD_DIM ** -0.5 * LOG2E),
        out_shape=out_shape,
        grid=(B, S // tm),
        in_specs=[pl.BlockSpec((None, tm, D), lambda b, i: (b, i, 0)), vec, vec,
                  pl.BlockSpec(w_fox.shape, lambda b, i: (0, 0)),
                  pl.BlockSpec(w_rest.shape, lambda b, i: (0, 0))],
        out_specs=out_specs,
        compiler_params=_cparams(("parallel", "parallel")),
    )(x, sc1, sh1, w_fox, w_rest)


def _foxcum_kernel(ff_ref, b_ref, o_ref, *, blk):
    S = ff_ref.shape[0]
    r = lax.broadcasted_iota(jnp.int32, (blk, blk), 0)
    c = lax.broadcasted_iota(jnp.int32, (blk, blk), 1)
    lower = jnp.where(r >= c, 1.0, 0.0).astype(BF16)
    carry = jnp.zeros((1, LANES), F32)
    for j in range(S // blk):
        z = ff_ref[j * blk:(j + 1) * blk, :] + b_ref[...]
        lf = jnp.minimum(z, 0.0) - jnp.log(1.0 + jnp.exp(-jnp.abs(z)))
        cum = _exact_matrix_dot(lower, lf) + carry
        o_ref[j * blk:(j + 1) * blk, :] = cum * LOG2E
        carry = cum[blk - 1:blk, :]


def _foxcum(ffp, bias_p, blk=256):
    B, S, _ = ffp.shape
    return pl.pallas_call(
        functools.partial(_foxcum_kernel, blk=blk),
        out_shape=jax.ShapeDtypeStruct((B, S, LANES), F32),
        grid=(B,),
        in_specs=[pl.BlockSpec((None, S, LANES), lambda b: (b, 0, 0)),
                  pl.BlockSpec((1, LANES), lambda b: (0, 0))],
        out_specs=pl.BlockSpec((None, S, LANES), lambda b: (b, 0, 0)),
        compiler_params=_cparams(("parallel",)),
    )(ffp, bias_p)


NCUM = 3


def _fox_kernel(q_ref, k_ref, v_ref, c_ref, wg_ref, wu_ref, wd_ref, o_ref, wgb_ref, wub_ref, wdb_ref,
                ka_sc, kb_sc, va_sc, vb_sc, *, tq, tk):
    wgb_ref[...] = wg_ref[...].astype(BF16)
    wub_ref[...] = wu_ref[...].astype(BF16)
    wdb_ref[...] = wd_ref[...].astype(BF16)

    p = pl.program_id(1)
    qi = pl.program_id(2)
    S = k_ref.shape[0]

    @pl.when(qi == 0)
    def _():
        lane = lax.broadcasted_iota(jnp.int32, (S, LANES), 1)
        rr = lax.broadcasted_iota(jnp.int32, (LANES, LANES), 0)
        cc = lax.broadcasted_iota(jnp.int32, (LANES, LANES), 1)
        rest = c_ref[...]
        placed = jnp.zeros((S, LANES), F32)
        for i in range(NCUM):
            piece = rest.astype(BF16)
            rest = rest - piece.astype(F32)
            sel = ((rr == 2 * p) & (cc == HEAD_DIM + i)) | ((rr == 2 * p + 1) & (cc == i))
            placed = placed + jnp.dot(piece, jnp.where(sel, 1.0, 0.0).astype(BF16), preferred_element_type=F32)
        k2 = k_ref[...].astype(F32)
        ka_sc[...] = jnp.where(lane < HEAD_DIM, k2, -placed).astype(BF16)
        kb_sc[...] = jnp.where(lane >= HEAD_DIM, k2, -placed).astype(BF16)
        vt = v_ref[...].astype(F32).T
        row = lax.broadcasted_iota(jnp.int32, (LANES, S), 0)
        va_sc[...] = jnp.where(row < HEAD_DIM, vt, jnp.where(row == HEAD_DIM, 1.0, 0.0)).astype(BF16)
        vb_sc[...] = jnp.where(row >= HEAD_DIM, vt, jnp.where(row == 0, 1.0, 0.0)).astype(BF16)

    q2 = q_ref[...].astype(F32)
    qlane = lax.broadcasted_iota(jnp.int32, (tq, LANES), 1)
    qa = jnp.where(qlane < HEAD_DIM, q2, jnp.where(qlane < HEAD_DIM + NCUM, 1.0, 0.0)).astype(BF16)
    qb = jnp.where(qlane >= HEAD_DIM, q2, jnp.where(qlane < NCUM, 1.0, 0.0)).astype(BF16)
    nsub = tq // tk

    def block(k0, carry, diag_off):
        q0 = 0 if diag_off is None else diag_off
        out = []
        for ksc, vsc, qh, (m, acc) in ((ka_sc, va_sc, qa, carry[:2]), (kb_sc, vb_sc, qb, carry[2:])):
            st = lax.dot_general(ksc[pl.ds(k0, tk), :], qh[q0:, :], (((1,), (1,)), ((), ())),
                                 preferred_element_type=F32)
            if diag_off is not None:
                st = jnp.where(lax.broadcasted_iota(jnp.int32, st.shape, 0)
                               <= lax.broadcasted_iota(jnp.int32, st.shape, 1), st, NEG_BIG)
            m_old = m[:, q0:]
            m_new = jnp.maximum(m_old, jnp.max(st, axis=0, keepdims=True))
            pt = jnp.exp2(st - m_new).astype(BF16)
            acc_new = (jnp.exp2(m_old - m_new) * acc[:, q0:]
                       + jnp.dot(vsc[:, pl.ds(k0, tk)], pt, preferred_element_type=F32))
            if q0:
                m_new = jnp.concatenate([m[:, :q0], m_new], axis=1)
                acc_new = jnp.concatenate([acc[:, :q0], acc_new], axis=1)
            out += [m_new, acc_new]
        return tuple(out)

    def group(j, carry):
        k0 = pl.multiple_of(j * (nsub * tk), nsub * tk)
        for u in range(nsub):
            carry = block(k0 + u * tk, carry, None)
        return carry

    m0 = jnp.full((1, tq), NEG_BIG, F32)
    a0 = jnp.zeros((LANES, tq), F32)
    carry = lax.fori_loop(0, qi, group, (m0, a0, m0, a0))
    for d in range(nsub):
        carry = block(pl.multiple_of(qi * tq + d * tk, tk), carry, d * tk)
    _, aa, _, ab = carry
    row = lax.broadcasted_iota(jnp.int32, (LANES, tq), 0)
    ot = jnp.where(row < HEAD_DIM, aa * (1.0 / aa[HEAD_DIM:HEAD_DIM + 1, :]), ab * (1.0 / ab[0:1, :]))
    o_ref[...] = ot.T.astype(o_ref.dtype)


def _fox(fq, fk, fv, cum, expert_w, tq=2048, tk=512):
    B, S, W = fq.shape
    tq = min(tq, S)
    assert tq % tk == 0 and S % tq == 0
    npairs = W // LANES
    nq = S // tq
    nsteps = B * npairs * nq
    nexp = expert_w[0].shape[0]
    assert nexp % nsteps == 0
    eb = nexp // nsteps
    wspec = lambda w: pl.BlockSpec((eb,) + w.shape[1:], lambda b, p, i: ((b * npairs + p) * nq + i, 0, 0))
    return pl.pallas_call(
        functools.partial(_fox_kernel, tq=tq, tk=tk),
        out_shape=(jax.ShapeDtypeStruct((B, S, W), BF16),) + tuple(
            jax.ShapeDtypeStruct(w.shape, BF16) for w in expert_w),
        grid=(B, npairs, nq),
        in_specs=[pl.BlockSpec((None, tq, LANES), lambda b, p, i: (b, i, p)),
                  pl.BlockSpec((None, S, LANES), lambda b, p, i: (b, 0, p)),
                  pl.BlockSpec((None, S, LANES), lambda b, p, i: (b, 0, p)),
                  pl.BlockSpec((None, S, LANES), lambda b, p, i: (b, 0, 0))] + [wspec(w) for w in expert_w],
        out_specs=(pl.BlockSpec((None, tq, LANES), lambda b, p, i: (b, i, p)),) + tuple(
            wspec(w) for w in expert_w),
        scratch_shapes=[pltpu.VMEM((S, LANES), BF16), pltpu.VMEM((S, LANES), BF16),
                        pltpu.VMEM((LANES, S), BF16), pltpu.VMEM((LANES, S), BF16)],
        compiler_params=_cparams(("parallel", "parallel", "arbitrary")),
    )(fq, fk, fv, cum, *expert_w)


def _hgrn_kernel(hq_ref, hf_ref, hi_ref, hg_ref, lb_ref, nw_ref, o_ref,
                 b_sc, kk_sc, qq_sc, o_sc, w1_sc, w2_sc, w3_sc, w4_sc, w5_sc,
                 p_sc, st16_sc, dec_sc, st64_sc):
    S = hq_ref.shape[0]
    C = HCHUNK
    nchunks = S // C
    BLK = HBLOCK
    nblk = S // BLK

    lg = lb_ref[...]
    e = jnp.exp(lg - jnp.max(lg, axis=0, keepdims=True))
    lb = e[0:1, :] / jnp.sum(e, axis=0, keepdims=True)

    f = lb + (1.0 - lb) * (1.0 / (1.0 + jnp.exp(-hf_ref[...])))
    lf = jnp.log(f)
    kk_sc[...] = 1.0 - f
    qq_sc[...] = hq_ref[...].astype(F32)

    row = lax.broadcasted_iota(jnp.int32, (S, LANES), 0)
    rb = 4 * BLK
    tr = lax.broadcasted_iota(jnp.int32, (rb, rb), 0)
    tc = lax.broadcasted_iota(jnp.int32, (rb, rb), 1)
    tri = jnp.where(((tr & -BLK) == (tc & -BLK)) & (tc <= tr), 1.0, 0.0).astype(BF16)
    lf3 = jnp.concatenate(_bf16_pieces(lf, 3), axis=1)
    for j in range(S // rb):
        c3 = jnp.dot(tri, lf3[j * rb:(j + 1) * rb, :], preferred_element_type=F32)
        b_sc[j * rb:(j + 1) * rb, :] = c3[:, :LANES] + c3[:, LANES:2 * LANES] + c3[:, 2 * LANES:]
    safe = jnp.max(-b_sc[...].reshape(nblk, BLK, LANES)[:, BLK - 1, :]) <= HGRN_SAFE_EXP

    lane = lax.broadcasted_iota(jnp.int32, (C, LANES), 1)
    sr = lax.broadcasted_iota(jnp.int32, (LANES, LANES), 0)
    scn = lax.broadcasted_iota(jnp.int32, (LANES, LANES), 1)
    same_head = (sr // HEAD_DIM) == (scn // HEAD_DIM)

    @pl.when(safe)
    def _factorised():
        qh_sc, kh_sc, ke_sc, qd_sc, k2_sc = w1_sc, w2_sc, w3_sc, w4_sc, w5_sc
        SB = 2 * BLK
        nsb = S // SB
        bb = b_sc[...]
        dblk = jnp.exp(bb.reshape(nblk, BLK, LANES)[:, BLK - 1:BLK, :])
        dfull = jnp.broadcast_to(dblk, (nblk, BLK, LANES)).reshape(S, LANES)
        second = (row & BLK) != 0
        d_prev = pltpu.roll(dfull, BLK, axis=0)
        d_next = pltpu.roll(dfull, S - BLK, axis=0)
        qh = qq_sc[...] * jnp.exp(bb)
        qh_sc[...] = qh.astype(BF16)
        qd_sc[...] = (qh * jnp.where(second, d_prev, 1.0)).astype(BF16)
        kh = kk_sc[...] * jnp.exp(-bb)
        kh_sc[...] = kh.astype(BF16)
        ke = kh * dfull
        ke_sc[...] = ke.astype(BF16)
        k2_sc[...] = (ke * jnp.where(second, 1.0, d_next)).astype(BF16)
        d3 = dfull.reshape(nsb, SB, LANES)
        dec_sc[pl.ds(0, nsb), :] = d3[:, 0, :] * d3[:, BLK, :]
        unroll = min(16, nsb)
        assert nsb % unroll == 0
        tn = (((0,), (0,)), ((), ()))
        nt = (((1,), (1,)), ((), ()))

        def scan(g, st):
            for u in range(unroll):
                i = g * unroll + u
                r0 = pl.multiple_of(i * SB, SB)
                st64_sc[i] = st.astype(BF16)
                upd = lax.dot_general(hi_ref[pl.ds(r0, SB), :], k2_sc[pl.ds(r0, SB), :], tn,
                                      preferred_element_type=F32)
                st = st * dec_sc[pl.ds(i, 1), :] + jnp.where(same_head, upd, 0.0)
            return st

        lax.fori_loop(0, nsb // unroll, scan, jnp.zeros((LANES, LANES), F32))

        r = lax.broadcasted_iota(jnp.int32, (2 * SB, 2 * SB), 0)
        c = lax.broadcasted_iota(jnp.int32, (2 * SB, 2 * SB), 1)
        t = r & (SB - 1)
        visible = (((c < SB) & ((t & BLK) == (c & BLK)) & ((t & (BLK - 1)) >= (c & (BLK - 1))))
                   | ((c >= SB) & (c < SB + BLK) & (t >= BLK)))
        plane = lax.broadcasted_iota(jnp.int32, (SB, LANES), 1)
        pad = jnp.zeros((BLK, LANES), BF16)

        def readout(g, _):
            for u in range(unroll):
                i = g * unroll + u
                r0 = pl.multiple_of(i * SB, SB)
                vb = hi_ref[pl.ds(r0, SB), :]
                qh2 = qh_sc[pl.ds(r0, SB), :]
                q2 = jnp.concatenate([jnp.where(plane < HEAD_DIM, qh2, jnp.zeros_like(qh2)),
                                      jnp.where(plane >= HEAD_DIM, qh2, jnp.zeros_like(qh2))], axis=0)
                kext = jnp.concatenate([kh_sc[pl.ds(r0, SB), :], ke_sc[pl.ds(r0, BLK), :], pad], axis=0)
                vext = jnp.concatenate([vb, vb[:BLK], pad], axis=0)
                sc = lax.dot_general(q2, kext, nt, preferred_element_type=F32)
                sc = jnp.where(visible, sc, 0.0).astype(BF16)
                out = jnp.dot(sc, vext, preferred_element_type=F32)
                o_inter = lax.dot_general(qd_sc[pl.ds(r0, SB), :], st64_sc[i], nt, preferred_element_type=F32)
                o_sc[pl.ds(r0, SB), :] = jnp.where(plane < HEAD_DIM, out[:SB], out[SB:]) + o_inter
            return 0

        lax.fori_loop(0, nsb // unroll, readout, 0)

    @pl.when(jnp.logical_not(safe))
    def _direct():
        qt_sc, kt_sc, s_sc, a2_sc = w1_sc, w2_sc, w3_sc, b_sc
        bb = b_sc[...]
        cl = jnp.broadcast_to(bb.reshape(nchunks, C, LANES)[:, C - 1:C, :], (nchunks, C, LANES)).reshape(S, LANES)
        aa = bb - jnp.where((row & (BLK - 1)) >= C, pltpu.roll(cl, C, axis=0), 0.0)
        al = jnp.broadcast_to(aa.reshape(nchunks, C, LANES)[:, C - 1:C, :], (nchunks, C, LANES)).reshape(S, LANES)
        qt_sc[...] = (qq_sc[...] * jnp.exp(aa)).astype(BF16)
        kt_sc[...] = (kk_sc[...] * jnp.exp(al - aa)).astype(BF16)
        dec_sc[...] = jnp.exp(aa.reshape(nchunks, C, LANES)[:, C - 1, :])
        a2_sc[...] = aa * LOG2E
        trow = lax.broadcasted_iota(jnp.int32, (C, LANES), 0)

        def gen(c, _):
            r0 = pl.multiple_of(c * C, C)
            ac = a2_sc[pl.ds(r0, C), :]
            qc = qq_sc[pl.ds(r0, C), :]
            kc = kk_sc[pl.ds(r0, C), :]
            half = C // 2
            for s in range(C):
                if s < half:
                    dec = jnp.exp2(jnp.where(trow >= s, ac - ac[s:s + 1, :], NEG_BIG))
                    p = qc * (kc[s:s + 1, :] * dec)
                else:
                    dec = jnp.exp2(jnp.where(trow[half:] >= s, ac[half:] - ac[s:s + 1, :], NEG_BIG))
                    p = jnp.concatenate([jnp.zeros((half, LANES), F32), qc[half:] * (kc[s:s + 1, :] * dec)],
                                        axis=0)
                p_sc[pl.ds(r0, C), s * LANES:(s + 1) * LANES] = p.astype(BF16)
            return 0

        lax.fori_loop(0, nchunks, gen, 0)

        er = lax.broadcasted_iota(jnp.int32, (C * LANES, LANES), 0)
        ec = lax.broadcasted_iota(jnp.int32, (C * LANES, LANES), 1)
        emat = (ec == ((er & (LANES - 1)) // HEAD_DIM) * C + er // LANES).astype(BF16)
        rb = 256

        def red(i, _):
            r0 = pl.multiple_of(i * rb, rb)
            s_sc[pl.ds(r0, rb), :] = jnp.dot(p_sc[pl.ds(r0, rb), :], emat,
                                             preferred_element_type=F32).astype(BF16)
            return 0

        lax.fori_loop(0, S // rb, red, 0)

        unroll = 16
        assert nchunks % unroll == 0

        def scan(g, st):
            for u in range(unroll):
                c = g * unroll + u
                r0 = pl.multiple_of(c * C, C)
                st16_sc[c] = st.astype(BF16)
                upd = lax.dot_general(hi_ref[pl.ds(r0, C), :], kt_sc[pl.ds(r0, C), :],
                                      (((0,), (0,)), ((), ())), preferred_element_type=F32)
                st = st * dec_sc[pl.ds(c, 1), :] + jnp.where(same_head, upd, 0.0)
            return st

        lax.fori_loop(0, nchunks // unroll, scan, jnp.zeros((LANES, LANES), F32))

        def readout(g, _):
            for u in range(unroll):
                c = g * unroll + u
                r0 = pl.multiple_of(c * C, C)
                vc = hi_ref[pl.ds(r0, C), :]
                o_inter = lax.dot_general(qt_sc[pl.ds(r0, C), :], st16_sc[c],
                                          (((1,), (1,)), ((), ())), preferred_element_type=F32)
                v2 = jnp.concatenate([jnp.where(lane < HEAD_DIM, vc, jnp.zeros_like(vc)),
                                      jnp.where(lane >= HEAD_DIM, vc, jnp.zeros_like(vc))], axis=0)
                o_intra = jnp.dot(s_sc[pl.ds(r0, C), :][:, :2 * C], v2, preferred_element_type=F32)
                o_sc[pl.ds(r0, C), :] = o_inter + o_intra
            return 0

        lax.fori_loop(0, nchunks // unroll, readout, 0)

    o = o_sc[...]
    ones_head = jnp.where(same_head, 1.0, 0.0).astype(BF16)
    sq_hi, sq_lo = _bf16_pieces(o * o, 2)
    ms = (jnp.dot(sq_hi, ones_head, preferred_element_type=F32)
          + jnp.dot(sq_lo, ones_head, preferred_element_type=F32)) * (1.0 / HEAD_DIM)
    y = o * lax.rsqrt(ms + RMS_EPS) * nw_ref[...]
    o_ref[...] = (y * hg_ref[...].astype(F32)).astype(o_ref.dtype)


def _hgrn(hq, hf, hi, hg, lb_logits, norm_w):
    B, S, W = hq.shape
    npairs = W // LANES
    nrows = lb_logits.shape[0]
    seq = pl.BlockSpec((None, S, LANES), lambda b, p: (b, 0, p))
    return pl.pallas_call(
        _hgrn_kernel,
        out_shape=jax.ShapeDtypeStruct((B, S, W), BF16),
        grid=(B, npairs),
        in_specs=[seq, seq, seq, seq,
                  pl.BlockSpec((nrows, LANES), lambda b, p: (0, p)),
                  pl.BlockSpec((1, LANES), lambda b, p: (0, p))],
        out_specs=seq,
        scratch_shapes=[pltpu.VMEM((S, LANES), F32),
                        pltpu.VMEM((S, LANES), F32),
                        pltpu.VMEM((S, LANES), F32),
                        pltpu.VMEM((S, LANES), F32),
                        pltpu.VMEM((S, LANES), BF16),
                        pltpu.VMEM((S, LANES), BF16),
                        pltpu.VMEM((S, LANES), BF16),
                        pltpu.VMEM((S, LANES), BF16),
                        pltpu.VMEM((S, LANES), BF16),
                        pltpu.VMEM((S, HCHUNK * LANES), BF16),
                        pltpu.VMEM((S // HCHUNK, LANES, LANES), BF16),
                        pltpu.VMEM((S // HCHUNK, LANES), F32),
                        pltpu.VMEM((S // HBLOCK, LANES, LANES), BF16)],
        compiler_params=_cparams(("parallel", "parallel")),
    )(hq, hf, hi, hg, lb_logits, norm_w.reshape(1, W))


def _layer_norm(v, g, b):
    mu = jnp.mean(v, axis=-1, keepdims=True)
    d = v - mu
    var = jnp.mean(d * d, axis=-1, keepdims=True)
    return d * lax.rsqrt(var + LN_EPS) * g + b


def _bf16_bits(x):
    return (pltpu.bitcast(x, jnp.uint32) + jnp.uint32(0x8000)) & jnp.uint32(0xFFFF0000)


def _store_chunks(ref, val):
    n = ref.shape[0]
    for j in range(n):
        lo = _bf16_bits(val[:, j * LANES:(j + 1) * LANES]) >> 16
        hi = _bf16_bits(val[:, (j + n) * LANES:(j + n + 1) * LANES])
        ref[j] = pltpu.bitcast(lo | hi, F32)


def _load_chunks(ref):
    words = [pltpu.bitcast(ref[j], jnp.uint32) for j in range(ref.shape[0])]
    lo = [pltpu.bitcast(w << 16, F32) for w in words]
    hi = [pltpu.bitcast(w & jnp.uint32(0xFFFF0000), F32) for w in words]
    return jnp.concatenate(lo + hi, axis=1)


def _mix_kernel(yf_ref, oh_ref, gf_ref, gh_ref, x_ref, g1_ref, sc2_ref, sh2_ref,
                wuf_ref, wuh_ref, wo_ref, lg_ref, lbias_ref, wr_ref, br_ref,
                x1_ref, h2_ref, ri_ref, rt_ref, cnt_ref, carry_sc, *, alpha, ngroups, nper):
    first = (pl.program_id(0) == 0) & (pl.program_id(1) == 0)

    @pl.when(first)
    def _():
        carry_sc[...] = jnp.zeros_like(carry_sc)

    tm = x_ref.shape[0]
    yf = jnp.dot(yf_ref[...], wuf_ref[...], preferred_element_type=F32)
    yh = jnp.dot(oh_ref[...], wuh_ref[...], preferred_element_type=F32)
    merged = gf_ref[...].astype(F32) * yf + gh_ref[...].astype(F32) * yh
    y = jnp.dot(merged.astype(BF16), wo_ref[...], preferred_element_type=F32)
    x1 = _layer_norm(alpha * x_ref[...] + g1_ref[...] * y, lg_ref[...], lbias_ref[...])
    x1_ref[...] = x1
    h2 = x1 * (1.0 + sc2_ref[...]) + sh2_ref[...]
    _store_chunks(h2_ref, h2)

    h_hi, h_lo = _bf16_pieces(h2, 2)
    hh = jnp.dot(h_hi, wr_ref[...], preferred_element_type=F32)
    logits = (hh[:, :LANES] + hh[:, LANES:]
              + jnp.dot(h_lo, wr_ref[:, :LANES], preferred_element_type=F32)) + br_ref[...]
    lane = lax.broadcasted_iota(jnp.int32, (tm, LANES), 1)
    big = jnp.int32(1 << 20)

    def argmax_first(vals, mask):
        mx = jnp.max(jnp.where(mask, vals, -jnp.inf), axis=1, keepdims=True)
        idx = jnp.min(jnp.where(mask & (vals == mx), lane, big), axis=1, keepdims=True)
        return mx, idx

    gmask = lane < ngroups
    gmax = jnp.max(jnp.where(gmask, logits, -jnp.inf), axis=1, keepdims=True)
    gexp = jnp.where(gmask, jnp.exp(logits - gmax), 0.0)
    gprob = gexp / jnp.sum(gexp, axis=1, keepdims=True)
    g_w, g_idx = argmax_first(gprob, gmask)

    lo = ngroups + g_idx * nper
    emask = (lane >= lo) & (lane < lo + nper)
    emax = jnp.max(jnp.where(emask, logits, -jnp.inf), axis=1, keepdims=True)
    eexp = jnp.where(emask, jnp.exp(logits - emax), 0.0)
    eprob = eexp / jnp.sum(eexp, axis=1, keepdims=True)
    p0, i0 = argmax_first(eprob, emask)
    p1, i1 = argmax_first(eprob, emask & (lane != i0))
    den = p0 + p1
    w0 = p0 / den * g_w
    w1 = p1 / den * g_w
    e0 = i0 - ngroups
    e1 = i1 - ngroups

    oh = ((lane == e0) | (lane == e1)).astype(F32)
    r = lax.broadcasted_iota(jnp.int32, (tm, tm), 0)
    c = lax.broadcasted_iota(jnp.int32, (tm, tm), 1)
    strict_lower = (c < r).astype(BF16)
    before = jnp.dot(strict_lower, oh.astype(BF16), preferred_element_type=F32) + carry_sc[...]
    rank0 = jnp.sum(jnp.where(lane == e0, before, 0.0), axis=1, keepdims=True)
    rank1 = jnp.sum(jnp.where(lane == e1, before, 0.0), axis=1, keepdims=True)
    carry_sc[...] = carry_sc[...] + jnp.sum(oh, axis=0, keepdims=True)
    cnt_ref[...] = carry_sc[...]

    info = jnp.where(lane == 0, w0, 0.0)
    info = jnp.where(lane == 1, w1, info)
    info = jnp.where(lane == 2, e0.astype(F32), info)
    info = jnp.where(lane == 3, e1.astype(F32), info)
    info = jnp.where(lane == 4, rank0, info)
    info = jnp.where(lane == 5, rank1, info)
    ri_ref[...] = info
    rt_ref[...] = info.T[:ROW_TILE, :]


def _mix(yf, oh, gf, gh, x, g1, sc2, sh2, wuf, wuh, wo, ln_g, ln_b, wr, br, alpha, ngroups, nper, tm=512):
    B, S, D = x.shape
    W = yf.shape[2]
    tok = lambda w: pl.BlockSpec((None, tm, w), lambda b, i: (b, i, 0))
    vec = pl.BlockSpec((None, 1, D), lambda b, i: (b, 0, 0))
    full = lambda a: pl.BlockSpec(a.shape, lambda b, i: (0,) * a.ndim)
    return pl.pallas_call(
        functools.partial(_mix_kernel, alpha=alpha, ngroups=ngroups, nper=nper),
        out_shape=(jax.ShapeDtypeStruct((B, S, D), F32),
                   jax.ShapeDtypeStruct((D // WORD_LANES, B * S, LANES), F32),
                   jax.ShapeDtypeStruct((B, S, LANES), F32),
                   jax.ShapeDtypeStruct((ROW_TILE, B * S), F32),
                   jax.ShapeDtypeStruct((1, LANES), F32)),
        grid=(B, S // tm),
        in_specs=[tok(W), tok(W), tok(D), tok(D), tok(D), vec, vec, vec,
                  full(wuf), full(wuh), full(wo), full(ln_g), full(ln_b), full(wr), full(br)],
        out_specs=(tok(D),
                   pl.BlockSpec((D // WORD_LANES, tm, LANES), lambda b, i: (0, b * (S // tm) + i, 0)),
                   tok(LANES),
                   pl.BlockSpec((ROW_TILE, tm), lambda b, i: (0, b * (S // tm) + i)),
                   pl.BlockSpec((1, LANES), lambda b, i: (0, 0))),
        scratch_shapes=[pltpu.VMEM((1, LANES), F32)],
        compiler_params=_cparams(("arbitrary", "arbitrary")),
    )(yf, oh, gf, gh, x, g1, sc2, sh2, wuf, wuh, wo, ln_g, ln_b, wr, br)


def _sc_mesh():
    return plsc.VectorSubcoreMesh(core_axis_name="core", subcore_axis_name="subcore")


def _sc_pipeline(body, grid, in_specs, out_specs):
    return pltpu.emit_pipeline(body, grid=grid, in_specs=in_specs, out_specs=out_specs,
                               core_axis_name=("core", "subcore"),
                               dimension_semantics=(pltpu.PARALLEL,) * len(grid))


def _sc_scatter_rows(src, rows_a, rows_b, n_out):
    nj, t = rows_a.shape
    win = SC_WINDOW
    nc = t // win

    @pl.kernel(out_type=jax.ShapeDtypeStruct((n_out, LANES), src.dtype), mesh=_sc_mesh(), scratch_types=[])
    def scatter(x_hbm, a_hbm, b_hbm, o_hbm):
        def body(x_vmem, a_vmem, b_vmem):
            pltpu.sync_copy(x_vmem, o_hbm.at[a_vmem.at[0]])
            pltpu.sync_copy(x_vmem, o_hbm.at[b_vmem.at[0]])

        idx = pl.BlockSpec((1, win), lambda j, c: (j, c))
        _sc_pipeline(body, (nj, nc), [pl.BlockSpec((win, LANES), lambda j, c: (j * nc + c, 0)), idx, idx],
                     [])(x_hbm, a_hbm, b_hbm)

    return scatter(src, rows_a, rows_b)


def _sc_gather_rows(table, rows):
    nr, t = rows.shape
    win = SC_WINDOW
    nc = t // win

    @pl.kernel(out_type=jax.ShapeDtypeStruct((nr * t, LANES), table.dtype), mesh=_sc_mesh(), scratch_types=[])
    def gather(x_hbm, i_hbm, o_hbm):
        def body(i_vmem, o_vmem):
            pltpu.sync_copy(x_hbm.at[i_vmem.at[0]], o_vmem)

        _sc_pipeline(body, (nr, nc), [pl.BlockSpec((1, win), lambda r, c: (r, c))],
                     [pl.BlockSpec((win, LANES), lambda r, c: (r * nc + c, 0))])(i_hbm, o_hbm)

    return gather(table, rows)


W_SLOTS = 3


def _experts_kernel(tn_ref, tb_ref, run_ref, first_ref, rexp_ref, nrun_ref,
                    x_ref, wg_hbm, wu_hbm, wd_hbm, o_ref, wg_sc, wu_sc, wd_sc, sems):
    del tb_ref
    i = pl.program_id(0)
    nrows = tn_ref[i]
    run = run_ref[i]
    nruns = nrun_ref[0]

    def copies(r, slot):
        e = rexp_ref[r]
        return [pltpu.make_async_copy(hbm.at[e], buf.at[slot], sems.at[slot])
                for hbm, buf in ((wg_hbm, wg_sc), (wu_hbm, wu_sc), (wd_hbm, wd_sc))]

    def fetch(r):
        for s in range(W_SLOTS):
            @pl.when(r % W_SLOTS == s)
            def _(s=s):
                for cp in copies(r, s):
                    cp.start()

    @pl.when(i == 0)
    def _():
        fetch(0)

        @pl.when(nruns > 1)
        def _():
            fetch(1)

    for s in range(W_SLOTS):
        @pl.when((nrows > 0) & (run % W_SLOTS == s))
        def _(s=s):
            @pl.when(first_ref[i] != 0)
            def _():
                for cp in copies(run, s):
                    cp.wait()

                @pl.when(run + 2 < nruns)
                def _():
                    fetch(run + 2)

            x = _load_chunks(x_ref)
            x = jnp.where(lax.broadcasted_iota(jnp.int32, x.shape, 0) < nrows, x, 0.0).astype(BF16)
            g = jnp.dot(x, wg_sc[s], preferred_element_type=F32)
            u = jnp.dot(x, wu_sc[s], preferred_element_type=F32)
            hid = (_silu(g) * u).astype(BF16)
            _store_chunks(o_ref, jnp.dot(hid, wd_sc[s], preferred_element_type=F32))


def _experts(tile_rows, tile_block, tile_run, tile_first, run_expert, nruns, xs, wg, wu, wd, tm):
    E, D, FF = wg.shape
    dt = D // WORD_LANES
    ntiles = tile_rows.shape[0]
    rows = pl.BlockSpec((dt, tm, LANES), lambda i, tn, tb, *_: (0, tb[i], 0))
    hbm = pl.BlockSpec(memory_space=pl.ANY)
    grid_spec = pltpu.PrefetchScalarGridSpec(
        num_scalar_prefetch=6,
        grid=(ntiles,),
        in_specs=[rows, hbm, hbm, hbm],
        out_specs=rows,
        scratch_shapes=[pltpu.VMEM((W_SLOTS, D, FF), BF16), pltpu.VMEM((W_SLOTS, D, FF), BF16),
                        pltpu.VMEM((W_SLOTS, FF, D), BF16), pltpu.SemaphoreType.DMA((W_SLOTS,))],
    )
    return pl.pallas_call(
        _experts_kernel,
        out_shape=jax.ShapeDtypeStruct((dt, ntiles * tm, LANES), F32),
        grid_spec=grid_spec,
        compiler_params=_cparams(("arbitrary",)),
    )(tile_rows, tile_block, tile_run, tile_first, run_expert, nruns, xs, wg, wu, wd)


def _combine_kernel(yg_ref, x1_ref, ri_ref, g2_ref, lg_ref, lb_ref, o_ref, *, alpha):
    ri = ri_ref[...]
    y = ri[:, 0:1] * _load_chunks(yg_ref.at[0]) + ri[:, 1:2] * _load_chunks(yg_ref.at[1])
    o_ref[...] = _layer_norm(alpha * x1_ref[...] + g2_ref[...] * y, lg_ref[...], lb_ref[...])


def _combine(yg, x1, rinfo, g2, ln_g, ln_b, alpha, tm=1024):
    B, S, D = x1.shape
    nb = S // tm
    return pl.pallas_call(
        functools.partial(_combine_kernel, alpha=alpha),
        out_shape=jax.ShapeDtypeStruct((B, S, D), F32),
        grid=(B, nb),
        in_specs=[pl.BlockSpec((2, D // WORD_LANES, tm, LANES), lambda b, i: (0, 0, b * nb + i, 0)),
                  pl.BlockSpec((None, tm, D), lambda b, i: (b, i, 0)),
                  pl.BlockSpec((None, tm, LANES), lambda b, i: (b, i, 0)),
                  pl.BlockSpec((None, 1, D), lambda b, i: (b, 0, 0)),
                  pl.BlockSpec((1, D), lambda b, i: (0, 0)),
                  pl.BlockSpec((1, D), lambda b, i: (0, 0))],
        out_specs=pl.BlockSpec((None, tm, D), lambda b, i: (b, i, 0)),
        compiler_params=_cparams(("parallel", "parallel")),
    )(yg, x1, rinfo, g2, ln_g, ln_b)


def kernel(x, c, w_ada, b_ada, w_in, b_fox_forget, hgrn_lb_logits, hgrn_norm_w, w_up_fox, w_up_hgrn, w_out,
           ln1_g, ln1_b, w_router_group, b_router_group, w_router_expert, b_router_expert,
           w_expert_gate, w_expert_up, w_expert_down, ln2_g, ln2_b):
    B, S, D = x.shape
    depth = w_ada.shape[0]
    assert depth == 1, "single-layer block"
    fox_heads = b_fox_forget.shape[1]
    fox_w = fox_heads * HEAD_DIM
    hgrn_w = hgrn_norm_w.shape[1]
    ngroups = w_router_group.shape[2]
    nexp = w_router_expert.shape[2]
    nper = nexp // ngroups
    alpha = (2 * depth) ** 0.25
    T = B * S

    ada = _ada(c, w_ada[0], b_ada[0])
    sh1, sc1, g1, sh2, sc2, g2 = [a.reshape(B, 1, D) for a in jnp.split(ada, 6, axis=-1)]

    wi = w_in[0]
    o_ff = 3 * fox_w
    w_fox = jnp.pad(wi[:, :o_ff + fox_heads], ((0, 0), (0, LANES - fox_heads))).astype(BF16)
    w_rest = _cast_columns(wi, o_ff + fox_heads, wi.shape[1] - o_ff - fox_heads)
    widths = [fox_w, fox_w, fox_w, LANES, hgrn_w, hgrn_w, hgrn_w, hgrn_w, D, D]
    segs, off = [], 0
    for n, w in enumerate(widths):
        if n == 4:
            off = 0
        segs.append((off, off + w))
        off += w
    fq, fk, fv, ffp, hq, hf, hi, hg, gf, gh = _inproj(x, sc1, sh1, w_fox, w_rest, segs)

    bias_p = jnp.zeros((1, LANES), F32).at[0, :fox_heads].set(b_fox_forget[0])
    cum = _foxcum(ffp, bias_p)
    y_fox, wg_b, wu_b, wd_b = _fox(fq, fk, fv, cum, (w_expert_gate[0], w_expert_up[0], w_expert_down[0]))

    o_h = _hgrn(hq, hf, hi, hg, hgrn_lb_logits, hgrn_norm_w[0])

    wr = jnp.zeros((D, LANES), F32).at[:, :ngroups].set(w_router_group[0]).at[:, ngroups:ngroups + nexp].set(
        w_router_expert[0])
    wr_hi = lax.bitcast_convert_type(lax.bitcast_convert_type(wr, jnp.uint32) & jnp.uint32(0xFFFF0000), F32)
    wr = jnp.concatenate([wr_hi.astype(BF16), (wr - wr_hi).astype(BF16)], axis=1)
    br = jnp.zeros((1, LANES), F32).at[0, :ngroups].set(b_router_group[0]).at[0, ngroups:ngroups + nexp].set(
        b_router_expert[0])
    x1, h2, rinfo, fields, counts = _mix(
        y_fox, o_h, gf, gh, x, g1, sc2, sh2,
        w_up_fox[0].astype(BF16), w_up_hgrn[0].astype(BF16), w_out[0].astype(BF16),
        ln1_g[0].reshape(1, D), ln1_b[0].reshape(1, D), wr, br, alpha, ngroups, nper)

    tm_e = 512
    dt = D // WORD_LANES
    ntiles = (2 * T) // tm_e + nexp
    nslots = ntiles * tm_e
    cnt = counts[0, :nexp].astype(jnp.int32)
    padded = ((cnt + tm_e - 1) // tm_e) * tm_e
    ends = jnp.cumsum(padded)
    starts = ends - padded
    eid = fields[2:4].astype(jnp.int32)
    rank = fields[4:6].astype(jnp.int32)
    first = jnp.sum(jnp.where(eid[None] == jnp.arange(nexp, dtype=jnp.int32)[:, None, None],
                              starts[:, None, None], 0), axis=0)
    pos = first + rank
    tile_start = jnp.arange(ntiles, dtype=jnp.int32) * tm_e
    tile_block = jnp.minimum(jnp.arange(ntiles, dtype=jnp.int32), ends[-1] // tm_e - 1)
    tile_expert = jnp.minimum(jnp.sum((tile_start[:, None] >= ends[None, :]).astype(jnp.int32), axis=1), nexp - 1)
    tile_rows = jnp.clip(starts[tile_expert] + cnt[tile_expert] - tile_start, 0, tm_e)
    used = jnp.cumsum((cnt > 0).astype(jnp.int32))
    nruns = used[-1:]
    run_expert = jnp.sum((used[None, :] <= jnp.arange(nexp + 2, dtype=jnp.int32)[:, None]).astype(jnp.int32), axis=1)
    run_expert = jnp.minimum(run_expert, nexp - 1)
    tile_run = used[tile_expert] - 1
    prev_expert = jnp.concatenate([jnp.full((1,), -1, jnp.int32), tile_expert[:-1]])
    tile_first = ((tile_rows > 0) & (tile_expert != prev_expert)).astype(jnp.int32)
    rows = pos[:, None, :] + (jnp.arange(dt, dtype=jnp.int32) * nslots)[None, :, None]

    xs = _sc_scatter_rows(h2.reshape(dt * T, LANES), rows[0], rows[1], dt * nslots)
    ys = _experts(tile_rows, tile_block, tile_run, tile_first, run_expert, nruns,
                  xs.reshape(dt, nslots, LANES), wg_b, wu_b, wd_b, tm_e)
    yg = _sc_gather_rows(ys.reshape(dt * nslots, LANES), rows.reshape(2 * dt, T))
    return _combine(yg.reshape(2, dt, T, LANES), x1, rinfo, g2,
                    ln2_g[0].reshape(1, D), ln2_b[0].reshape(1, D), alpha)
```

```python
import functools

import jax
import jax.numpy as jnp
from jax import lax
from jax.experimental import pallas as pl
from jax.experimental.pallas import tpu as pltpu
from jax.experimental.pallas import tpu_sc as plsc

F32 = jnp.float32
BF16 = jnp.bfloat16

LANES = 128
HEAD_DIM = 64
LN_EPS = 1e-5
RMS_EPS = 1e-6
LOG2E = 1.4426950408889634
NEG_BIG = -1e30
HCHUNK = 16
HBLOCK = 64
HGRN_SAFE_EXP = 60.0
ROW_TILE = 8
WORD_LANES = 2 * LANES
SC_WINDOW = 256
VMEM_LIMIT = 56 * 1024 * 1024


def _cparams(sem, vmem=VMEM_LIMIT):
    return pltpu.CompilerParams(dimension_semantics=sem, vmem_limit_bytes=vmem)


def _sigmoid(x):
    return 0.5 * jnp.tanh(0.5 * x) + 0.5


def _silu(x):
    return x * _sigmoid(x)


def _bf16_pieces(x, n):
    pieces = []
    for _ in range(n):
        top = pltpu.bitcast(pltpu.bitcast(x, jnp.uint32) & jnp.uint32(0xFFFF0000), F32)
        pieces.append(top.astype(BF16))
        x = x - top
    return pieces


def _exact_matrix_dot(m, x):
    r = jnp.dot(m, jnp.concatenate(_bf16_pieces(x, 3), axis=1), preferred_element_type=F32)
    return r[:, :LANES] + r[:, LANES:2 * LANES] + r[:, 2 * LANES:]


def _ada_kernel(c_ref, w_ref, b_ref, o_ref):
    c_hi, c_lo = _bf16_pieces(_silu(c_ref[...]), 2)
    w_hi, w_lo = _bf16_pieces(w_ref[...], 2)
    o_ref[...] = (jnp.dot(c_hi, w_hi, preferred_element_type=F32) + jnp.dot(c_hi, w_lo, preferred_element_type=F32)
                  + jnp.dot(c_lo, w_hi, preferred_element_type=F32)) + b_ref[...]


def _ada(c, w_ada, b_ada):
    B, D = c.shape
    N = w_ada.shape[1]
    tn = 1024
    return pl.pallas_call(
        _ada_kernel,
        out_shape=jax.ShapeDtypeStruct((B, N), F32),
        grid=(N // tn,),
        in_specs=[pl.BlockSpec((B, D), lambda j: (0, 0)),
                  pl.BlockSpec((D, tn), lambda j: (0, j)),
                  pl.BlockSpec((1, tn), lambda j: (0, j))],
        out_specs=pl.BlockSpec((B, tn), lambda j: (0, j)),
        compiler_params=_cparams(("arbitrary",)),
    )(c, w_ada, b_ada.reshape(1, N))


N_FOX_SEGS = 4
SILU_SEGS = (4, 7)
SIGMOID_SEGS = (8, 9)


def _inproj_kernel(x_ref, sc_ref, sh_ref, wf_ref, wr_ref,
                   fq_ref, fk_ref, fv_ref, ff_ref, hq_ref, hf_ref, hi_ref, hg_ref, gf_ref, gh_ref,
                   *, segs, q_scale):
    h = (x_ref[...] * (1.0 + sc_ref[...]) + sh_ref[...]).astype(BF16)
    outs = (fq_ref, fk_ref, fv_ref, ff_ref, hq_ref, hf_ref, hi_ref, hg_ref, gf_ref, gh_ref)
    for idx, (o_ref, (a, b)) in enumerate(zip(outs, segs)):
        w_ref = wf_ref if idx < N_FOX_SEGS else wr_ref
        r = jnp.dot(h, w_ref[:, a:b], preferred_element_type=F32)
        if idx == 0:
            r = r * q_scale
        elif idx in SILU_SEGS:
            r = _silu(r)
        elif idx in SIGMOID_SEGS:
            r = _sigmoid(r)
        o_ref[...] = r.astype(o_ref.dtype)


def _inproj(x, sc1, sh1, w_fox, w_rest, segs, tm=256):
    B, S, D = x.shape
    widths = [b - a for a, b in segs]
    dtypes = [BF16, BF16, BF16, F32, BF16, F32, BF16, BF16, BF16, BF16]
    out_shape = tuple(jax.ShapeDtypeStruct((B, S, w), dt) for w, dt in zip(widths, dtypes))
    out_specs = tuple(pl.BlockSpec((None, tm, w), lambda b, i: (b, i, 0)) for w in widths)
    vec = pl.BlockSpec((None, 1, D), lambda b, i: (b, 0, 0))
    return pl.pallas_call(
        functools.partial(_inproj_kernel, segs=tuple(segs), q_scale=HEAD_DIM ** -0.5 * LOG2E),
        out_shape=out_shape,
        grid=(B, S // tm),
        in_specs=[pl.BlockSpec((None, tm, D), lambda b, i: (b, i, 0)), vec, vec,
                  pl.BlockSpec(w_fox.shape, lambda b, i: (0, 0)),
                  pl.BlockSpec(w_rest.shape, lambda b, i: (0, 0))],
        out_specs=out_specs,
        compiler_params=_cparams(("parallel", "parallel")),
    )(x, sc1, sh1, w_fox, w_rest)


def _foxcum_kernel(ff_ref, b_ref, o_ref, *, blk):
    S = ff_ref.shape[0]
    r = lax.broadcasted_iota(jnp.int32, (blk, blk), 0)
    c = lax.broadcasted_iota(jnp.int32, (blk, blk), 1)
    lower = jnp.where(r >= c, 1.0, 0.0).astype(BF16)
    carry = jnp.zeros((1, LANES), F32)
    for j in range(S // blk):
        z = ff_ref[j * blk:(j + 1) * blk, :] + b_ref[...]
        lf = jnp.minimum(z, 0.0) - jnp.log(1.0 + jnp.exp(-jnp.abs(z)))
        cum = _exact_matrix_dot(lower, lf) + carry
        o_ref[j * blk:(j + 1) * blk, :] = cum * LOG2E
        carry = cum[blk - 1:blk, :]


def _foxcum(ffp, bias_p, blk=256):
    B, S, _ = ffp.shape
    return pl.pallas_call(
        functools.partial(_foxcum_kernel, blk=blk),
        out_shape=jax.ShapeDtypeStruct((B, S, LANES), F32),
        grid=(B,),
        in_specs=[pl.BlockSpec((None, S, LANES), lambda b: (b, 0, 0)),
                  pl.BlockSpec((1, LANES), lambda b: (0, 0))],
        out_specs=pl.BlockSpec((None, S, LANES), lambda b: (b, 0, 0)),
        compiler_params=_cparams(("parallel",)),
    )(ffp, bias_p)


NCUM = 3


def _fox_kernel(q_ref, k_ref, v_ref, c_ref, wg_ref, wu_ref, wd_ref, o_ref, wgb_ref, wub_ref, wdb_ref,
                ka_sc, kb_sc, va_sc, vb_sc, *, tq, tk):
    wgb_ref[...] = wg_ref[...].astype(BF16)
    wub_ref[...] = wu_ref[...].astype(BF16)
    wdb_ref[...] = wd_ref[...].astype(BF16)

    p = pl.program_id(1)
    qi = pl.program_id(2)
    S = k_ref.shape[0]

    @pl.when(qi == 0)
    def _():
        lane = lax.broadcasted_iota(jnp.int32, (S, LANES), 1)
        rr = lax.broadcasted_iota(jnp.int32, (LANES, LANES), 0)
        cc = lax.broadcasted_iota(jnp.int32, (LANES, LANES), 1)
        rest = c_ref[...]
        placed = jnp.zeros((S, LANES), F32)
        for i in range(NCUM):
            piece = rest.astype(BF16)
            rest = rest - piece.astype(F32)
            sel = ((rr == 2 * p) & (cc == HEAD_DIM + i)) | ((rr == 2 * p + 1) & (cc == i))
            placed = placed + jnp.dot(piece, jnp.where(sel, 1.0, 0.0).astype(BF16), preferred_element_type=F32)
        k2 = k_ref[...].astype(F32)
        ka_sc[...] = jnp.where(lane < HEAD_DIM, k2, -placed).astype(BF16)
        kb_sc[...] = jnp.where(lane >= HEAD_DIM, k2, -placed).astype(BF16)
        vt = v_ref[...].astype(F32).T
        row = lax.broadcasted_iota(jnp.int32, (LANES, S), 0)
        va_sc[...] = jnp.where(row < HEAD_DIM, vt, jnp.where(row == HEAD_DIM, 1.0, 0.0)).astype(BF16)
        vb_sc[...] = jnp.where(row >= HEAD_DIM, vt, jnp.where(row == 0, 1.0, 0.0)).astype(BF16)

    q2 = q_ref[...].astype(F32)
    qlane = lax.broadcasted_iota(jnp.int32, (tq, LANES), 1)
    qa = jnp.where(qlane < HEAD_DIM, q2, jnp.where(qlane < HEAD_DIM + NCUM, 1.0, 0.0)).astype(BF16)
    qb = jnp.where(qlane >= HEAD_DIM, q2, jnp.where(qlane < NCUM, 1.0, 0.0)).astype(BF16)
    nsub = tq // tk

    def block(k0, carry, diag_off):
        q0 = 0 if diag_off is None else diag_off
        out = []
        for ksc, vsc, qh, (m, acc) in ((ka_sc, va_sc, qa, carry[:2]), (kb_sc, vb_sc, qb, carry[2:])):
            st = lax.dot_general(ksc[pl.ds(k0, tk), :], qh[q0:, :], (((1,), (1,)), ((), ())),
                                 preferred_element_type=F32)
            if diag_off is not None:
                st = jnp.where(lax.broadcasted_iota(jnp.int32, st.shape, 0)
                               <= lax.broadcasted_iota(jnp.int32, st.shape, 1), st, NEG_BIG)
            m_old = m[:, q0:]
            m_new = jnp.maximum(m_old, jnp.max(st, axis=0, keepdims=True))
            pt = jnp.exp2(st - m_new).astype(BF16)
            acc_new = (jnp.exp2(m_old - m_new) * acc[:, q0:]
                       + jnp.dot(vsc[:, pl.ds(k0, tk)], pt, preferred_element_type=F32))
            if q0:
                m_new = jnp.concatenate([m[:, :q0], m_new], axis=1)
                acc_new = jnp.concatenate([acc[:, :q0], acc_new], axis=1)
            out += [m_new, acc_new]
        return tuple(out)

    def group(j, carry):
        k0 = pl.multiple_of(j * (nsub * tk), nsub * tk)
        for u in range(nsub):
            carry = block(k0 + u * tk, carry, None)
        return carry

    m0 = jnp.full((1, tq), NEG_BIG, F32)
    a0 = jnp.zeros((LANES, tq), F32)
    carry = lax.fori_loop(0, qi, group, (m0, a0, m0, a0))
    for d in range(nsub):
        carry = block(pl.multiple_of(qi * tq + d * tk, tk), carry, d * tk)
    _, aa, _, ab = carry
    row = lax.broadcasted_iota(jnp.int32, (LANES, tq), 0)
    ot = jnp.where(row < HEAD_DIM, aa * (1.0 / aa[HEAD_DIM:HEAD_DIM + 1, :]), ab * (1.0 / ab[0:1, :]))
    o_ref[...] = ot.T.astype(o_ref.dtype)


def _fox(fq, fk, fv, cum, expert_w, tq=2048, tk=512):
    B, S, W = fq.shape
    tq = min(tq, S)
    assert tq % tk == 0 and S % tq == 0
    npairs = W // LANES
    nq = S // tq
    nsteps = B * npairs * nq
    nexp = expert_w[0].shape[0]
    assert nexp % nsteps == 0
    eb = nexp // nsteps
    wspec = lambda w: pl.BlockSpec((eb,) + w.shape[1:], lambda b, p, i: ((b * npairs + p) * nq + i, 0, 0))
    return pl.pallas_call(
        functools.partial(_fox_kernel, tq=tq, tk=tk),
        out_shape=(jax.ShapeDtypeStruct((B, S, W), BF16),) + tuple(
            jax.ShapeDtypeStruct(w.shape, BF16) for w in expert_w),
        grid=(B, npairs, nq),
        in_specs=[pl.BlockSpec((None, tq, LANES), lambda b, p, i: (b, i, p)),
                  pl.BlockSpec((None, S, LANES), lambda b, p, i: (b, 0, p)),
                  pl.BlockSpec((None, S, LANES), lambda b, p, i: (b, 0, p)),
                  pl.BlockSpec((None, S, LANES), lambda b, p, i: (b, 0, 0))] + [wspec(w) for w in expert_w],
        out_specs=(pl.BlockSpec((None, tq, LANES), lambda b, p, i: (b, i, p)),) + tuple(
            wspec(w) for w in expert_w),
        scratch_shapes=[pltpu.VMEM((S, LANES), BF16), pltpu.VMEM((S, LANES), BF16),
                        pltpu.VMEM((LANES, S), BF16), pltpu.VMEM((LANES, S), BF16)],
        compiler_params=_cparams(("parallel", "parallel", "arbitrary")),
    )(fq, fk, fv, cum, *expert_w)


def _hgrn_kernel(hq_ref, hf_ref, hi_ref, hg_ref, lb_ref, nw_ref, o_ref,
                 b_sc, kk_sc, qq_sc, o_sc, w1_sc, w2_sc, w3_sc, w4_sc, w5_sc,
                 p_sc, st16_sc, dec_sc, st64_sc):
    S = hq_ref.shape[0]
    C = HCHUNK
    nchunks = S // C
    BLK = HBLOCK
    nblk = S // BLK

    lg = lb_ref[...]
    e = jnp.exp(lg - jnp.max(lg, axis=0, keepdims=True))
    lb = e[0:1, :] / jnp.sum(e, axis=0, keepdims=True)

    f = lb + (1.0 - lb) * (1.0 / (1.0 + jnp.exp(-hf_ref[...])))
    lf = jnp.log(f)
    kk_sc[...] = 1.0 - f
    qq_sc[...] = hq_ref[...].astype(F32)

    row = lax.broadcasted_iota(jnp.int32, (S, LANES), 0)
    rb = 4 * BLK
    tr = lax.broadcasted_iota(jnp.int32, (rb, rb), 0)
    tc = lax.broadcasted_iota(jnp.int32, (rb, rb), 1)
    tri = jnp.where(((tr & -BLK) == (tc & -BLK)) & (tc <= tr), 1.0, 0.0).astype(BF16)
    lf3 = jnp.concatenate(_bf16_pieces(lf, 3), axis=1)
    for j in range(S // rb):
        c3 = jnp.dot(tri, lf3[j * rb:(j + 1) * rb, :], preferred_element_type=F32)
        b_sc[j * rb:(j + 1) * rb, :] = c3[:, :LANES] + c3[:, LANES:2 * LANES] + c3[:, 2 * LANES:]
    safe = jnp.max(-b_sc[...].reshape(nblk, BLK, LANES)[:, BLK - 1, :]) <= HGRN_SAFE_EXP

    lane = lax.broadcasted_iota(jnp.int32, (C, LANES), 1)
    sr = lax.broadcasted_iota(jnp.int32, (LANES, LANES), 0)
    scn = lax.broadcasted_iota(jnp.int32, (LANES, LANES), 1)
    same_head = (sr // HEAD_DIM) == (scn // HEAD_DIM)

    @pl.when(safe)
    def _factorised():
        qh_sc, kh_sc, ke_sc, qd_sc, k2_sc = w1_sc, w2_sc, w3_sc, w4_sc, w5_sc
        SB = 2 * BLK
        nsb = S // SB
        bb = b_sc[...]
        dblk = jnp.exp(bb.reshape(nblk, BLK, LANES)[:, BLK - 1:BLK, :])
        dfull = jnp.broadcast_to(dblk, (nblk, BLK, LANES)).reshape(S, LANES)
        second = (row & BLK) != 0
        d_prev = pltpu.roll(dfull, BLK, axis=0)
        d_next = pltpu.roll(dfull, S - BLK, axis=0)
        qh = qq_sc[...] * jnp.exp(bb)
        qh_sc[...] = qh.astype(BF16)
        qd_sc[...] = (qh * jnp.where(second, d_prev, 1.0)).astype(BF16)
        kh = kk_sc[...] * jnp.exp(-bb)
        kh_sc[...] = kh.astype(BF16)
        ke = kh * dfull
        ke_sc[...] = ke.astype(BF16)
        k2_sc[...] = (ke * jnp.where(second, 1.0, d_next)).astype(BF16)
        d3 = dfull.reshape(nsb, SB, LANES)
        dec_sc[pl.ds(0, nsb), :] = d3[:, 0, :] * d3[:, BLK, :]
        unroll = min(16, nsb)
        assert nsb % unroll == 0
        tn = (((0,), (0,)), ((), ()))
        nt = (((1,), (1,)), ((), ()))

        def scan(g, st):
            for u in range(unroll):
                i = g * unroll + u
                r0 = pl.multiple_of(i * SB, SB)
                st64_sc[i] = st.astype(BF16)
                upd = lax.dot_general(hi_ref[pl.ds(r0, SB), :], k2_sc[pl.ds(r0, SB), :], tn,
                                      preferred_element_type=F32)
                st = st * dec_sc[pl.ds(i, 1), :] + jnp.where(same_head, upd, 0.0)
            return st

        lax.fori_loop(0, nsb // unroll, scan, jnp.zeros((LANES, LANES), F32))

        r = lax.broadcasted_iota(jnp.int32, (2 * SB, 2 * SB), 0)
        c = lax.broadcasted_iota(jnp.int32, (2 * SB, 2 * SB), 1)
        t = r & (SB - 1)
        visible = (((c < SB) & ((t & BLK) == (c & BLK)) & ((t & (BLK - 1)) >= (c & (BLK - 1))))
                   | ((c >= SB) & (c < SB + BLK) & (t >= BLK)))
        plane = lax.broadcasted_iota(jnp.int32, (SB, LANES), 1)
        pad = jnp.zeros((BLK, LANES), BF16)

        def readout(g, _):
            for u in range(unroll):
                i = g * unroll + u
                r0 = pl.multiple_of(i * SB, SB)
                vb = hi_ref[pl.ds(r0, SB), :]
                qh2 = qh_sc[pl.ds(r0, SB), :]
                q2 = jnp.concatenate([jnp.where(plane < HEAD_DIM, qh2, jnp.zeros_like(qh2)),
                                      jnp.where(plane >= HEAD_DIM, qh2, jnp.zeros_like(qh2))], axis=0)
                kext = jnp.concatenate([kh_sc[pl.ds(r0, SB), :], ke_sc[pl.ds(r0, BLK), :], pad], axis=0)
                vext = jnp.concatenate([vb, vb[:BLK], pad], axis=0)
                sc = lax.dot_general(q2, kext, nt, preferred_element_type=F32)
                sc = jnp.where(visible, sc, 0.0).astype(BF16)
                out = jnp.dot(sc, vext, preferred_element_type=F32)
                o_inter = lax.dot_general(qd_sc[pl.ds(r0, SB), :], st64_sc[i], nt, preferred_element_type=F32)
                o_sc[pl.ds(r0, SB), :] = jnp.where(plane < HEAD_DIM, out[:SB], out[SB:]) + o_inter
            return 0

        lax.fori_loop(0, nsb // unroll, readout, 0)

    @pl.when(jnp.logical_not(safe))
    def _direct():
        qt_sc, kt_sc, s_sc, a2_sc = w1_sc, w2_sc, w3_sc, b_sc
        bb = b_sc[...]
        cl = jnp.broadcast_to(bb.reshape(nchunks, C, LANES)[:, C - 1:C, :], (nchunks, C, LANES)).reshape(S, LANES)
        aa = bb - jnp.where((row & (BLK - 1)) >= C, pltpu.roll(cl, C, axis=0), 0.0)
        al = jnp.broadcast_to(aa.reshape(nchunks, C, LANES)[:, C - 1:C, :], (nchunks, C, LANES)).reshape(S, LANES)
        qt_sc[...] = (qq_sc[...] * jnp.exp(aa)).astype(BF16)
        kt_sc[...] = (kk_sc[...] * jnp.exp(al - aa)).astype(BF16)
        dec_sc[...] = jnp.exp(aa.reshape(nchunks, C, LANES)[:, C - 1, :])
        a2_sc[...] = aa * LOG2E
        trow = lax.broadcasted_iota(jnp.int32, (C, LANES), 0)

        def gen(c, _):
            r0 = pl.multiple_of(c * C, C)
            ac = a2_sc[pl.ds(r0, C), :]
            qc = qq_sc[pl.ds(r0, C), :]
            kc = kk_sc[pl.ds(r0, C), :]
            half = C // 2
            for s in range(C):
                if s < half:
                    dec = jnp.exp2(jnp.where(trow >= s, ac - ac[s:s + 1, :], NEG_BIG))
                    p = qc * (kc[s:s + 1, :] * dec)
                else:
                    dec = jnp.exp2(jnp.where(trow[half:] >= s, ac[half:] - ac[s:s + 1, :], NEG_BIG))
                    p = jnp.concatenate([jnp.zeros((half, LANES), F32), qc[half:] * (kc[s:s + 1, :] * dec)],
                                        axis=0)
                p_sc[pl.ds(r0, C), s * LANES:(s + 1) * LANES] = p.astype(BF16)
            return 0

        lax.fori_loop(0, nchunks, gen, 0)

        er = lax.broadcasted_iota(jnp.int32, (C * LANES, LANES), 0)
        ec = lax.broadcasted_iota(jnp.int32, (C * LANES, LANES), 1)
        emat = (ec == ((er & (LANES - 1)) // HEAD_DIM) * C + er // LANES).astype(BF16)
        rb = 256

        def red(i, _):
            r0 = pl.multiple_of(i * rb, rb)
            s_sc[pl.ds(r0, rb), :] = jnp.dot(p_sc[pl.ds(r0, rb), :], emat,
                                             preferred_element_type=F32).astype(BF16)
            return 0

        lax.fori_loop(0, S // rb, red, 0)

        unroll = 16
        assert nchunks % unroll == 0

        def scan(g, st):
            for u in range(unroll):
                c = g * unroll + u
                r0 = pl.multiple_of(c * C, C)
                st16_sc[c] = st.astype(BF16)
                upd = lax.dot_general(hi_ref[pl.ds(r0, C), :], kt_sc[pl.ds(r0, C), :],
                                      (((0,), (0,)), ((), ())), preferred_element_type=F32)
                st = st * dec_sc[pl.ds(c, 1), :] + jnp.where(same_head, upd, 0.0)
            return st

        lax.fori_loop(0, nchunks // unroll, scan, jnp.zeros((LANES, LANES), F32))

        def readout(g, _):
            for u in range(unroll):
                c = g * unroll + u
                r0 = pl.multiple_of(c * C, C)
                vc = hi_ref[pl.ds(r0, C), :]
                o_inter = lax.dot_general(qt_sc[pl.ds(r0, C), :], st16_sc[c],
                                          (((1,), (1,)), ((), ())), preferred_element_type=F32)
                v2 = jnp.concatenate([jnp.where(lane < HEAD_DIM, vc, jnp.zeros_like(vc)),
                                      jnp.where(lane >= HEAD_DIM, vc, jnp.zeros_like(vc))], axis=0)
                o_intra = jnp.dot(s_sc[pl.ds(r0, C), :][:, :2 * C], v2, preferred_element_type=F32)
                o_sc[pl.ds(r0, C), :] = o_inter + o_intra
            return 0

        lax.fori_loop(0, nchunks // unroll, readout, 0)

    o = o_sc[...]
    ones_head = jnp.where(same_head, 1.0, 0.0).astype(BF16)
    sq_hi, sq_lo = _bf16_pieces(o * o, 2)
    ms = (jnp.dot(sq_hi, ones_head, preferred_element_type=F32)
          + jnp.dot(sq_lo, ones_head, preferred_element_type=F32)) * (1.0 / HEAD_DIM)
    y = o * lax.rsqrt(ms + RMS_EPS) * nw_ref[...]
    o_ref[...] = (y * hg_ref[...].astype(F32)).astype(o_ref.dtype)


def _hgrn(hq, hf, hi, hg, lb_logits, norm_w):
    B, S, W = hq.shape
    npairs = W // LANES
    nrows = lb_logits.shape[0]
    seq = pl.BlockSpec((None, S, LANES), lambda b, p: (b, 0, p))
    return pl.pallas_call(
        _hgrn_kernel,
        out_shape=jax.ShapeDtypeStruct((B, S, W), BF16),
        grid=(B, npairs),
        in_specs=[seq, seq, seq, seq,
                  pl.BlockSpec((nrows, LANES), lambda b, p: (0, p)),
                  pl.BlockSpec((1, LANES), lambda b, p: (0, p))],
        out_specs=seq,
        scratch_shapes=[pltpu.VMEM((S, LANES), F32),
                        pltpu.VMEM((S, LANES), F32),
                        pltpu.VMEM((S, LANES), F32),
                        pltpu.VMEM((S, LANES), F32),
                        pltpu.VMEM((S, LANES), BF16),
                        pltpu.VMEM((S, LANES), BF16),
                        pltpu.VMEM((S, LANES), BF16),
                        pltpu.VMEM((S, LANES), BF16),
                        pltpu.VMEM((S, LANES), BF16),
                        pltpu.VMEM((S, HCHUNK * LANES), BF16),
                        pltpu.VMEM((S // HCHUNK, LANES, LANES), BF16),
                        pltpu.VMEM((S // HCHUNK, LANES), F32),
                        pltpu.VMEM((S // HBLOCK, LANES, LANES), BF16)],
        compiler_params=_cparams(("parallel", "parallel")),
    )(hq, hf, hi, hg, lb_logits, norm_w.reshape(1, W))


def _layer_norm(v, g, b):
    mu = jnp.mean(v, axis=-1, keepdims=True)
    d = v - mu
    var = jnp.mean(d * d, axis=-1, keepdims=True)
    return d * lax.rsqrt(var + LN_EPS) * g + b


def _bf16_bits(x):
    return (pltpu.bitcast(x, jnp.uint32) + jnp.uint32(0x8000)) & jnp.uint32(0xFFFF0000)


def _store_chunks(ref, val):
    n = ref.shape[0]
    for j in range(n):
        lo = _bf16_bits(val[:, j * LANES:(j + 1) * LANES]) >> 16
        hi = _bf16_bits(val[:, (j + n) * LANES:(j + n + 1) * LANES])
        ref[j] = pltpu.bitcast(lo | hi, F32)


def _load_chunks(ref):
    words = [pltpu.bitcast(ref[j], jnp.uint32) for j in range(ref.shape[0])]
    lo = [pltpu.bitcast(w << 16, F32) for w in words]
    hi = [pltpu.bitcast(w & jnp.uint32(0xFFFF0000), F32) for w in words]
    return jnp.concatenate(lo + hi, axis=1)


def _mix_kernel(yf_ref, oh_ref, gf_ref, gh_ref, x_ref, g1_ref, sc2_ref, sh2_ref,
                wuf_ref, wuh_ref, wo_ref, lg_ref, lbias_ref, wr_ref, br_ref,
                x1_ref, h2_ref, ri_ref, rt_ref, cnt_ref, carry_sc, *, alpha, ngroups, nper):
    first = (pl.program_id(0) == 0) & (pl.program_id(1) == 0)

    @pl.when(first)
    def _():
        carry_sc[...] = jnp.zeros_like(carry_sc)

    tm = x_ref.shape[0]
    yf = jnp.dot(yf_ref[...], wuf_ref[...], preferred_element_type=F32)
    yh = jnp.dot(oh_ref[...], wuh_ref[...], preferred_element_type=F32)
    merged = gf_ref[...].astype(F32) * yf + gh_ref[...].astype(F32) * yh
    y = jnp.dot(merged.astype(BF16), wo_ref[...], preferred_element_type=F32)
    x1 = _layer_norm(alpha * x_ref[...] + g1_ref[...] * y, lg_ref[...], lbias_ref[...])
    x1_ref[...] = x1
    h2 = x1 * (1.0 + sc2_ref[...]) + sh2_ref[...]
    _store_chunks(h2_ref, h2)

    h_hi, h_lo = _bf16_pieces(h2, 2)
    hh = jnp.dot(h_hi, wr_ref[...], preferred_element_type=F32)
    logits = (hh[:, :LANES] + hh[:, LANES:]
              + jnp.dot(h_lo, wr_ref[:, :LANES], preferred_element_type=F32)) + br_ref[...]
    lane = lax.broadcasted_iota(jnp.int32, (tm, LANES), 1)
    big = jnp.int32(1 << 20)

    def argmax_first(vals, mask):
        mx = jnp.max(jnp.where(mask, vals, -jnp.inf), axis=1, keepdims=True)
        idx = jnp.min(jnp.where(mask & (vals == mx), lane, big), axis=1, keepdims=True)
        return mx, idx

    gmask = lane < ngroups
    gmax = jnp.max(jnp.where(gmask, logits, -jnp.inf), axis=1, keepdims=True)
    gexp = jnp.where(gmask, jnp.exp(logits - gmax), 0.0)
    gprob = gexp / jnp.sum(gexp, axis=1, keepdims=True)
    g_w, g_idx = argmax_first(gprob, gmask)

    lo = ngroups + g_idx * nper
    emask = (lane >= lo) & (lane < lo + nper)
    emax = jnp.max(jnp.where(emask, logits, -jnp.inf), axis=1, keepdims=True)
    eexp = jnp.where(emask, jnp.exp(logits - emax), 0.0)
    eprob = eexp / jnp.sum(eexp, axis=1, keepdims=True)
    p0, i0 = argmax_first(eprob, emask)
    p1, i1 = argmax_first(eprob, emask & (lane != i0))
    den = p0 + p1
    w0 = p0 / den * g_w
    w1 = p1 / den * g_w
    e0 = i0 - ngroups
    e1 = i1 - ngroups

    oh = ((lane == e0) | (lane == e1)).astype(F32)
    r = lax.broadcasted_iota(jnp.int32, (tm, tm), 0)
    c = lax.broadcasted_iota(jnp.int32, (tm, tm), 1)
    strict_lower = (c < r).astype(BF16)
    before = jnp.dot(strict_lower, oh.astype(BF16), preferred_element_type=F32) + carry_sc[...]
    rank0 = jnp.sum(jnp.where(lane == e0, before, 0.0), axis=1, keepdims=True)
    rank1 = jnp.sum(jnp.where(lane == e1, before, 0.0), axis=1, keepdims=True)
    carry_sc[...] = carry_sc[...] + jnp.sum(oh, axis=0, keepdims=True)
    cnt_ref[...] = carry_sc[...]

    info = jnp.where(lane == 0, w0, 0.0)
    info = jnp.where(lane == 1, w1, info)
    info = jnp.where(lane == 2, e0.astype(F32), info)
    info = jnp.where(lane == 3, e1.astype(F32), info)
    info = jnp.where(lane == 4, rank0, info)
    info = jnp.where(lane == 5, rank1, info)
    ri_ref[...] = info
    rt_ref[...] = info.T[:ROW_TILE, :]


def _mix(yf, oh, gf, gh, x, g1, sc2, sh2, wuf, wuh, wo, ln_g, ln_b, wr, br, alpha, ngroups, nper, tm=512):
    B, S, D = x.shape
    W = yf.shape[2]
    tok = lambda w: pl.BlockSpec((None, tm, w), lambda b, i: (b, i, 0))
    vec = pl.BlockSpec((None, 1, D), lambda b, i: (b, 0, 0))
    full = lambda a: pl.BlockSpec(a.shape, lambda b, i: (0,) * a.ndim)
    return pl.pallas_call(
        functools.partial(_mix_kernel, alpha=alpha, ngroups=ngroups, nper=nper),
        out_shape=(jax.ShapeDtypeStruct((B, S, D), F32),
                   jax.ShapeDtypeStruct((D // WORD_LANES, B * S, LANES), F32),
                   jax.ShapeDtypeStruct((B, S, LANES), F32),
                   jax.ShapeDtypeStruct((ROW_TILE, B * S), F32),
                   jax.ShapeDtypeStruct((1, LANES), F32)),
        grid=(B, S // tm),
        in_specs=[tok(W), tok(W), tok(D), tok(D), tok(D), vec, vec, vec,
                  full(wuf), full(wuh), full(wo), full(ln_g), full(ln_b), full(wr), full(br)],
        out_specs=(tok(D),
                   pl.BlockSpec((D // WORD_LANES, tm, LANES), lambda b, i: (0, b * (S // tm) + i, 0)),
                   tok(LANES),
                   pl.BlockSpec((ROW_TILE, tm), lambda b, i: (0, b * (S // tm) + i)),
                   pl.BlockSpec((1, LANES), lambda b, i: (0, 0))),
        scratch_shapes=[pltpu.VMEM((1, LANES), F32)],
        compiler_params=_cparams(("arbitrary", "arbitrary")),
    )(yf, oh, gf, gh, x, g1, sc2, sh2, wuf, wuh, wo, ln_g, ln_b, wr, br)


def _sc_mesh():
    return plsc.VectorSubcoreMesh(core_axis_name="core", subcore_axis_name="subcore")


def _sc_pipeline(body, grid, in_specs, out_specs):
    return pltpu.emit_pipeline(body, grid=grid, in_specs=in_specs, out_specs=out_specs,
                               core_axis_name=("core", "subcore"),
                               dimension_semantics=(pltpu.PARALLEL,) * len(grid))


def _sc_scatter_rows(src, rows_a, rows_b, n_out):
    nj, t = rows_a.shape
    win = SC_WINDOW
    nc = t // win

    @pl.kernel(out_type=jax.ShapeDtypeStruct((n_out, LANES), src.dtype), mesh=_sc_mesh(), scratch_types=[])
    def scatter(x_hbm, a_hbm, b_hbm, o_hbm):
        def body(x_vmem, a_vmem, b_vmem):
            pltpu.sync_copy(x_vmem, o_hbm.at[a_vmem.at[0]])
            pltpu.sync_copy(x_vmem, o_hbm.at[b_vmem.at[0]])

        idx = pl.BlockSpec((1, win), lambda j, c: (j, c))
        _sc_pipeline(body, (nj, nc), [pl.BlockSpec((win, LANES), lambda j, c: (j * nc + c, 0)), idx, idx],
                     [])(x_hbm, a_hbm, b_hbm)

    return scatter(src, rows_a, rows_b)


def _sc_gather_rows(table, rows):
    nr, t = rows.shape
    win = SC_WINDOW
    nc = t // win

    @pl.kernel(out_type=jax.ShapeDtypeStruct((nr * t, LANES), table.dtype), mesh=_sc_mesh(), scratch_types=[])
    def gather(x_hbm, i_hbm, o_hbm):
        def body(i_vmem, o_vmem):
            pltpu.sync_copy(x_hbm.at[i_vmem.at[0]], o_vmem)

        _sc_pipeline(body, (nr, nc), [pl.BlockSpec((1, win), lambda r, c: (r, c))],
                     [pl.BlockSpec((win, LANES), lambda r, c: (r * nc + c, 0))])(i_hbm, o_hbm)

    return gather(table, rows)


W_SLOTS = 3


def _experts_kernel(tn_ref, tb_ref, run_ref, first_ref, rexp_ref, nrun_ref,
                    x_ref, wg_hbm, wu_hbm, wd_hbm, o_ref, wg_sc, wu_sc, wd_sc, sems):
    del tb_ref
    i = pl.program_id(0)
    nrows = tn_ref[i]
    run = run_ref[i]
    nruns = nrun_ref[0]

    def copies(r, slot):
        e = rexp_ref[r]
        return [pltpu.make_async_copy(hbm.at[e], buf.at[slot], sems.at[slot])
                for hbm, buf in ((wg_hbm, wg_sc), (wu_hbm, wu_sc), (wd_hbm, wd_sc))]

    def fetch(r):
        if isinstance(r, int):
            for cp in copies(r, r % W_SLOTS):
                cp.start()
            return
        for s in range(W_SLOTS):
            @pl.when(r % W_SLOTS == s)
            def _(s=s):
                for cp in copies(r, s):
                    cp.start()

    ahead = W_SLOTS - 1

    @pl.when(i == 0)
    def _():
        for r in range(ahead):
            pl.when(r < nruns)(functools.partial(fetch, r))

    for s in range(W_SLOTS):
        @pl.when((nrows > 0) & (run % W_SLOTS == s))
        def _(s=s):
            @pl.when(first_ref[i] != 0)
            def _():
                for cp in copies(run, s):
                    cp.wait()

                @pl.when(run + ahead < nruns)
                def _():
                    fetch(run + ahead)

            x = _load_chunks(x_ref)
            x = jnp.where(lax.broadcasted_iota(jnp.int32, x.shape, 0) < nrows, x, 0.0).astype(BF16)
            g = jnp.dot(x, wg_sc[s], preferred_element_type=F32)
            u = jnp.dot(x, wu_sc[s], preferred_element_type=F32)
            hid = (_silu(g) * u).astype(BF16)
            _store_chunks(o_ref, jnp.dot(hid, wd_sc[s], preferred_element_type=F32))


def _experts(tile_rows, tile_block, tile_run, tile_first, run_expert, nruns, xs, wg, wu, wd, tm):
    E, D, FF = wg.shape
    dt = D // WORD_LANES
    ntiles = tile_rows.shape[0]
    rows = pl.BlockSpec((dt, tm, LANES), lambda i, tn, tb, *_: (0, tb[i], 0))
    hbm = pl.BlockSpec(memory_space=pl.ANY)
    grid_spec = pltpu.PrefetchScalarGridSpec(
        num_scalar_prefetch=6,
        grid=(ntiles,),
        in_specs=[rows, hbm, hbm, hbm],
        out_specs=rows,
        scratch_shapes=[pltpu.VMEM((W_SLOTS, D, FF), BF16), pltpu.VMEM((W_SLOTS, D, FF), BF16),
                        pltpu.VMEM((W_SLOTS, FF, D), BF16), pltpu.SemaphoreType.DMA((W_SLOTS,))],
    )
    return pl.pallas_call(
        _experts_kernel,
        out_shape=jax.ShapeDtypeStruct((dt, ntiles * tm, LANES), F32),
        grid_spec=grid_spec,
        compiler_params=_cparams(("arbitrary",)),
    )(tile_rows, tile_block, tile_run, tile_first, run_expert, nruns, xs, wg, wu, wd)


def _combine_kernel(yg_ref, x1_ref, ri_ref, g2_ref, lg_ref, lb_ref, o_ref, *, alpha):
    ri = ri_ref[...]
    y = ri[:, 0:1] * _load_chunks(yg_ref.at[0]) + ri[:, 1:2] * _load_chunks(yg_ref.at[1])
    o_ref[...] = _layer_norm(alpha * x1_ref[...] + g2_ref[...] * y, lg_ref[...], lb_ref[...])


def _combine(yg, x1, rinfo, g2, ln_g, ln_b, alpha, tm=1024):
    B, S, D = x1.shape
    nb = S // tm
    return pl.pallas_call(
        functools.partial(_combine_kernel, alpha=alpha),
        out_shape=jax.ShapeDtypeStruct((B, S, D), F32),
        grid=(B, nb),
        in_specs=[pl.BlockSpec((2, D // WORD_LANES, tm, LANES), lambda b, i: (0, 0, b * nb + i, 0)),
                  pl.BlockSpec((None, tm, D), lambda b, i: (b, i, 0)),
                  pl.BlockSpec((None, tm, LANES), lambda b, i: (b, i, 0)),
                  pl.BlockSpec((None, 1, D), lambda b, i: (b, 0, 0)),
                  pl.BlockSpec((1, D), lambda b, i: (0, 0)),
                  pl.BlockSpec((1, D), lambda b, i: (0, 0))],
        out_specs=pl.BlockSpec((None, tm, D), lambda b, i: (b, i, 0)),
        compiler_params=_cparams(("parallel", "parallel")),
    )(yg, x1, rinfo, g2, ln_g, ln_b)


def kernel(x, c, w_ada, b_ada, w_in, b_fox_forget, hgrn_lb_logits, hgrn_norm_w, w_up_fox, w_up_hgrn, w_out,
           ln1_g, ln1_b, w_router_group, b_router_group, w_router_expert, b_router_expert,
           w_expert_gate, w_expert_up, w_expert_down, ln2_g, ln2_b):
    B, S, D = x.shape
    depth = w_ada.shape[0]
    assert depth == 1, "single-layer block"
    fox_heads = b_fox_forget.shape[1]
    fox_w = fox_heads * HEAD_DIM
    hgrn_w = hgrn_norm_w.shape[1]
    ngroups = w_router_group.shape[2]
    nexp = w_router_expert.shape[2]
    nper = nexp // ngroups
    alpha = (2 * depth) ** 0.25
    T = B * S

    ada = _ada(c, w_ada[0], b_ada[0])
    sh1, sc1, g1, sh2, sc2, g2 = [a.reshape(B, 1, D) for a in jnp.split(ada, 6, axis=-1)]

    wi = w_in[0]
    o_ff = 3 * fox_w
    w_fox = jnp.pad(wi[:, :o_ff + fox_heads], ((0, 0), (0, LANES - fox_heads))).astype(BF16)
    w_rest = wi[:, o_ff + fox_heads:].astype(BF16)
    widths = [fox_w, fox_w, fox_w, LANES, hgrn_w, hgrn_w, hgrn_w, hgrn_w, D, D]
    segs, off = [], 0
    for n, w in enumerate(widths):
        if n == 4:
            off = 0
        segs.append((off, off + w))
        off += w
    fq, fk, fv, ffp, hq, hf, hi, hg, gf, gh = _inproj(x, sc1, sh1, w_fox, w_rest, segs)

    bias_p = jnp.zeros((1, LANES), F32).at[0, :fox_heads].set(b_fox_forget[0])
    cum = _foxcum(ffp, bias_p)
    y_fox, wg_b, wu_b, wd_b = _fox(fq, fk, fv, cum, (w_expert_gate[0], w_expert_up[0], w_expert_down[0]))

    o_h = _hgrn(hq, hf, hi, hg, hgrn_lb_logits, hgrn_norm_w[0])

    wr = jnp.zeros((D, LANES), F32).at[:, :ngroups].set(w_router_group[0]).at[:, ngroups:ngroups + nexp].set(
        w_router_expert[0])
    wr_hi = lax.bitcast_convert_type(lax.bitcast_convert_type(wr, jnp.uint32) & jnp.uint32(0xFFFF0000), F32)
    wr = jnp.concatenate([wr_hi.astype(BF16), (wr - wr_hi).astype(BF16)], axis=1)
    br = jnp.zeros((1, LANES), F32).at[0, :ngroups].set(b_router_group[0]).at[0, ngroups:ngroups + nexp].set(
        b_router_expert[0])
    x1, h2, rinfo, fields, counts = _mix(
        y_fox, o_h, gf, gh, x, g1, sc2, sh2,
        w_up_fox[0].astype(BF16), w_up_hgrn[0].astype(BF16), w_out[0].astype(BF16),
        ln1_g[0].reshape(1, D), ln1_b[0].reshape(1, D), wr, br, alpha, ngroups, nper)

    tm_e = 512
    dt = D // WORD_LANES
    ntiles = (2 * T) // tm_e + nexp
    nslots = ntiles * tm_e
    cnt = counts[0, :nexp].astype(jnp.int32)
    padded = ((cnt + tm_e - 1) // tm_e) * tm_e
    ends = jnp.cumsum(padded)
    starts = ends - padded
    eid = fields[2:4].astype(jnp.int32)
    rank = fields[4:6].astype(jnp.int32)
    first = jnp.sum(jnp.where(eid[None] == jnp.arange(nexp, dtype=jnp.int32)[:, None, None],
                              starts[:, None, None], 0), axis=0)
    pos = first + rank
    tile_start = jnp.arange(ntiles, dtype=jnp.int32) * tm_e
    tile_block = jnp.minimum(jnp.arange(ntiles, dtype=jnp.int32), ends[-1] // tm_e - 1)
    tile_expert = jnp.minimum(jnp.sum((tile_start[:, None] >= ends[None, :]).astype(jnp.int32), axis=1), nexp - 1)
    tile_rows = jnp.clip(starts[tile_expert] + cnt[tile_expert] - tile_start, 0, tm_e)
    used = jnp.cumsum((cnt > 0).astype(jnp.int32))
    nruns = used[-1:]
    run_expert = jnp.sum((used[None, :] <= jnp.arange(nexp + 2, dtype=jnp.int32)[:, None]).astype(jnp.int32), axis=1)
    run_expert = jnp.minimum(run_expert, nexp - 1)
    tile_run = used[tile_expert] - 1
    prev_expert = jnp.concatenate([jnp.full((1,), -1, jnp.int32), tile_expert[:-1]])
    tile_first = ((tile_rows > 0) & (tile_expert != prev_expert)).astype(jnp.int32)
    rows = pos[:, None, :] + (jnp.arange(dt, dtype=jnp.int32) * nslots)[None, :, None]

    xs = _sc_scatter_rows(h2.reshape(dt * T, LANES), rows[0], rows[1], dt * nslots)
    ys = _experts(tile_rows, tile_block, tile_run, tile_first, run_expert, nruns,
                  xs.reshape(dt, nslots, LANES), wg_b, wu_b, wd_b, tm_e)
    yg = _sc_gather_rows(ys.reshape(dt * nslots, LANES), rows.reshape(2 * dt, T))
    return _combine(yg.reshape(2, dt, T, LANES), x1, rinfo, g2,
                    ln2_g[0].reshape(1, D), ln2_b[0].reshape(1, D), alpha)
```

```python
import functools

import jax
import jax.numpy as jnp
from jax import lax
from jax.experimental import pallas as pl
from jax.experimental.pallas import tpu as pltpu
from jax.experimental.pallas import tpu_sc as plsc

F32 = jnp.float32
BF16 = jnp.bfloat16

LANES = 128
HEAD_DIM = 64
LN_EPS = 1e-5
RMS_EPS = 1e-6
LOG2E = 1.4426950408889634
NEG_BIG = -1e30
HCHUNK = 16
HBLOCK = 64
HGRN_SAFE_EXP = 60.0
ROW_TILE = 8
WORD_LANES = 2 * LANES
SC_WINDOW = 256
VMEM_LIMIT = 56 * 1024 * 1024


def _cparams(sem, vmem=VMEM_LIMIT):
    return pltpu.CompilerParams(dimension_semantics=sem, vmem_limit_bytes=vmem)


def _sigmoid(x):
    return 0.5 * jnp.tanh(0.5 * x) + 0.5


def _silu(x):
    return x * _sigmoid(x)


def _bf16_pieces(x, n):
    pieces = []
    for _ in range(n):
        top = pltpu.bitcast(pltpu.bitcast(x, jnp.uint32) & jnp.uint32(0xFFFF0000), F32)
        pieces.append(top.astype(BF16))
        x = x - top
    return pieces


def _exact_matrix_dot(m, x):
    r = jnp.dot(m, jnp.concatenate(_bf16_pieces(x, 3), axis=1), preferred_element_type=F32)
    return r[:, :LANES] + r[:, LANES:2 * LANES] + r[:, 2 * LANES:]


def _ada_kernel(c_ref, w_ref, b_ref, o_ref):
    c_hi, c_lo = _bf16_pieces(_silu(c_ref[...]), 2)
    w_hi, w_lo = _bf16_pieces(w_ref[...], 2)
    o_ref[...] = (jnp.dot(c_hi, w_hi, preferred_element_type=F32) + jnp.dot(c_hi, w_lo, preferred_element_type=F32)
                  + jnp.dot(c_lo, w_hi, preferred_element_type=F32)) + b_ref[...]


def _ada(c, w_ada, b_ada):
    B, D = c.shape
    N = w_ada.shape[1]
    tn = 1024
    return pl.pallas_call(
        _ada_kernel,
        out_shape=jax.ShapeDtypeStruct((B, N), F32),
        grid=(N // tn,),
        in_specs=[pl.BlockSpec((B, D), lambda j: (0, 0)),
                  pl.BlockSpec((D, tn), lambda j: (0, j)),
                  pl.BlockSpec((1, tn), lambda j: (0, j))],
        out_specs=pl.BlockSpec((B, tn), lambda j: (0, j)),
        compiler_params=_cparams(("arbitrary",)),
    )(c, w_ada, b_ada.reshape(1, N))


N_FOX_SEGS = 4
SILU_SEGS = (4, 7)
SIGMOID_SEGS = (8, 9)


def _inproj_kernel(x_ref, sc_ref, sh_ref, wf_ref, wr_ref,
                   fq_ref, fk_ref, fv_ref, ff_ref, hq_ref, hf_ref, hi_ref, hg_ref, gf_ref, gh_ref,
                   *, segs, q_scale):
    h = (x_ref[...] * (1.0 + sc_ref[...]) + sh_ref[...]).astype(BF16)
    outs = (fq_ref, fk_ref, fv_ref, ff_ref, hq_ref, hf_ref, hi_ref, hg_ref, gf_ref, gh_ref)
    for idx, (o_ref, (a, b)) in enumerate(zip(outs, segs)):
        w_ref = wf_ref if idx < N_FOX_SEGS else wr_ref
        r = jnp.dot(h, w_ref[:, a:b], preferred_element_type=F32)
        if idx == 0:
            r = r * q_scale
        elif idx in SILU_SEGS:
            r = _silu(r)
        elif idx in SIGMOID_SEGS:
            r = _sigmoid(r)
        o_ref[...] = r.astype(o_ref.dtype)


def _inproj(x, sc1, sh1, w_fox, w_rest, segs, tm=256):
    B, S, D = x.shape
    widths = [b - a for a, b in segs]
    dtypes = [BF16, BF16, BF16, F32, BF16, F32, BF16, BF16, BF16, BF16]
    out_shape = tuple(jax.ShapeDtypeStruct((B, S, w), dt) for w, dt in zip(widths, dtypes))
    out_specs = tuple(pl.BlockSpec((None, tm, w), lambda b, i: (b, i, 0)) for w in widths)
    vec = pl.BlockSpec((None, 1, D), lambda b, i: (b, 0, 0))
    return pl.pallas_call(
        functools.partial(_inproj_kernel, segs=tuple(segs), q_scale=HEAD_DIM ** -0.5 * LOG2E),
        out_shape=out_shape,
        grid=(B, S // tm),
        in_specs=[pl.BlockSpec((None, tm, D), lambda b, i: (b, i, 0)), vec, vec,
                  pl.BlockSpec(w_fox.shape, lambda b, i: (0, 0)),
                  pl.BlockSpec(w_rest.shape, lambda b, i: (0, 0))],
        out_specs=out_specs,
        compiler_params=_cparams(("parallel", "parallel")),
    )(x, sc1, sh1, w_fox, w_rest)


def _foxcum_kernel(ff_ref, b_ref, o_ref, *, blk):
    S = ff_ref.shape[0]
    r = lax.broadcasted_iota(jnp.int32, (blk, blk), 0)
    c = lax.broadcasted_iota(jnp.int32, (blk, blk), 1)
    lower = jnp.where(r >= c, 1.0, 0.0).astype(BF16)
    carry = jnp.zeros((1, LANES), F32)
    for j in range(S // blk):
        z = ff_ref[j * blk:(j + 1) * blk, :] + b_ref[...]
        lf = jnp.minimum(z, 0.0) - jnp.log(1.0 + jnp.exp(-jnp.abs(z)))
        cum = _exact_matrix_dot(lower, lf) + carry
        o_ref[j * blk:(j + 1) * blk, :] = cum * LOG2E
        carry = cum[blk - 1:blk, :]


def _foxcum(ffp, bias_p, blk=256):
    B, S, _ = ffp.shape
    return pl.pallas_call(
        functools.partial(_foxcum_kernel, blk=blk),
        out_shape=jax.ShapeDtypeStruct((B, S, LANES), F32),
        grid=(B,),
        in_specs=[pl.BlockSpec((None, S, LANES), lambda b: (b, 0, 0)),
                  pl.BlockSpec((1, LANES), lambda b: (0, 0))],
        out_specs=pl.BlockSpec((None, S, LANES), lambda b: (b, 0, 0)),
        compiler_params=_cparams(("parallel",)),
    )(ffp, bias_p)


NCUM = 3


def _fox_kernel(q_ref, k_ref, v_ref, c_ref, wg_ref, wu_ref, wd_ref, o_ref, wgb_ref, wub_ref, wdb_ref,
                ka_sc, kb_sc, va_sc, vb_sc, *, tq, tk):
    wgb_ref[...] = wg_ref[...].astype(BF16)
    wub_ref[...] = wu_ref[...].astype(BF16)
    wdb_ref[...] = wd_ref[...].astype(BF16)

    p = pl.program_id(1)
    qi = pl.program_id(2)
    S = k_ref.shape[0]

    @pl.when(qi == 0)
    def _():
        lane = lax.broadcasted_iota(jnp.int32, (S, LANES), 1)
        rr = lax.broadcasted_iota(jnp.int32, (LANES, LANES), 0)
        cc = lax.broadcasted_iota(jnp.int32, (LANES, LANES), 1)
        rest = c_ref[...]
        placed = jnp.zeros((S, LANES), F32)
        for i in range(NCUM):
            piece = rest.astype(BF16)
            rest = rest - piece.astype(F32)
            sel = ((rr == 2 * p) & (cc == HEAD_DIM + i)) | ((rr == 2 * p + 1) & (cc == i))
            placed = placed + jnp.dot(piece, jnp.where(sel, 1.0, 0.0).astype(BF16), preferred_element_type=F32)
        k2 = k_ref[...].astype(F32)
        ka_sc[...] = jnp.where(lane < HEAD_DIM, k2, -placed).astype(BF16)
        kb_sc[...] = jnp.where(lane >= HEAD_DIM, k2, -placed).astype(BF16)
        vt = v_ref[...].astype(F32).T
        row = lax.broadcasted_iota(jnp.int32, (LANES, S), 0)
        va_sc[...] = jnp.where(row < HEAD_DIM, vt, jnp.where(row == HEAD_DIM, 1.0, 0.0)).astype(BF16)
        vb_sc[...] = jnp.where(row >= HEAD_DIM, vt, jnp.where(row == 0, 1.0, 0.0)).astype(BF16)

    q2 = q_ref[...].astype(F32)
    qlane = lax.broadcasted_iota(jnp.int32, (tq, LANES), 1)
    qa = jnp.where(qlane < HEAD_DIM, q2, jnp.where(qlane < HEAD_DIM + NCUM, 1.0, 0.0)).astype(BF16)
    qb = jnp.where(qlane >= HEAD_DIM, q2, jnp.where(qlane < NCUM, 1.0, 0.0)).astype(BF16)
    nsub = tq // tk

    def block(k0, carry, diag_off):
        q0 = 0 if diag_off is None else diag_off
        out = []
        for ksc, vsc, qh, (m, acc) in ((ka_sc, va_sc, qa, carry[:2]), (kb_sc, vb_sc, qb, carry[2:])):
            st = lax.dot_general(ksc[pl.ds(k0, tk), :], qh[q0:, :], (((1,), (1,)), ((), ())),
                                 preferred_element_type=F32)
            if diag_off is not None:
                st = jnp.where(lax.broadcasted_iota(jnp.int32, st.shape, 0)
                               <= lax.broadcasted_iota(jnp.int32, st.shape, 1), st, NEG_BIG)
            m_old = m[:, q0:]
            m_new = jnp.maximum(m_old, jnp.max(st, axis=0, keepdims=True))
            pt = jnp.exp2(st - m_new).astype(BF16)
            acc_new = (jnp.exp2(m_old - m_new) * acc[:, q0:]
                       + jnp.dot(vsc[:, pl.ds(k0, tk)], pt, preferred_element_type=F32))
            if q0:
                m_new = jnp.concatenate([m[:, :q0], m_new], axis=1)
                acc_new = jnp.concatenate([acc[:, :q0], acc_new], axis=1)
            out += [m_new, acc_new]
        return tuple(out)

    def group(j, carry):
        k0 = pl.multiple_of(j * (nsub * tk), nsub * tk)
        for u in range(nsub):
            carry = block(k0 + u * tk, carry, None)
        return carry

    m0 = jnp.full((1, tq), NEG_BIG, F32)
    a0 = jnp.zeros((LANES, tq), F32)
    carry = lax.fori_loop(0, qi, group, (m0, a0, m0, a0))
    for d in range(nsub):
        carry = block(pl.multiple_of(qi * tq + d * tk, tk), carry, d * tk)
    _, aa, _, ab = carry
    row = lax.broadcasted_iota(jnp.int32, (LANES, tq), 0)
    ot = jnp.where(row < HEAD_DIM, aa * (1.0 / aa[HEAD_DIM:HEAD_DIM + 1, :]), ab * (1.0 / ab[0:1, :]))
    o_ref[...] = ot.T.astype(o_ref.dtype)


def _fox(fq, fk, fv, cum, expert_w, tq=2048, tk=512):
    B, S, W = fq.shape
    tq = min(tq, S)
    assert tq % tk == 0 and S % tq == 0
    npairs = W // LANES
    nq = S // tq
    nsteps = B * npairs * nq
    nexp = expert_w[0].shape[0]
    assert nexp % nsteps == 0
    eb = nexp // nsteps
    wspec = lambda w: pl.BlockSpec((eb,) + w.shape[1:], lambda b, p, i: ((b * npairs + p) * nq + i, 0, 0))
    return pl.pallas_call(
        functools.partial(_fox_kernel, tq=tq, tk=tk),
        out_shape=(jax.ShapeDtypeStruct((B, S, W), BF16),) + tuple(
            jax.ShapeDtypeStruct(w.shape, BF16) for w in expert_w),
        grid=(B, npairs, nq),
        in_specs=[pl.BlockSpec((None, tq, LANES), lambda b, p, i: (b, i, p)),
                  pl.BlockSpec((None, S, LANES), lambda b, p, i: (b, 0, p)),
                  pl.BlockSpec((None, S, LANES), lambda b, p, i: (b, 0, p)),
                  pl.BlockSpec((None, S, LANES), lambda b, p, i: (b, 0, 0))] + [wspec(w) for w in expert_w],
        out_specs=(pl.BlockSpec((None, tq, LANES), lambda b, p, i: (b, i, p)),) + tuple(
            wspec(w) for w in expert_w),
        scratch_shapes=[pltpu.VMEM((S, LANES), BF16), pltpu.VMEM((S, LANES), BF16),
                        pltpu.VMEM((LANES, S), BF16), pltpu.VMEM((LANES, S), BF16)],
        compiler_params=_cparams(("parallel", "parallel", "arbitrary")),
    )(fq, fk, fv, cum, *expert_w)


def _hgrn_kernel(hq_ref, hf_ref, hi_ref, hg_ref, lb_ref, nw_ref, o_ref,
                 b_sc, kk_sc, qq_sc, o_sc, w1_sc, w2_sc, w3_sc, w4_sc, w5_sc,
                 p_sc, st16_sc, dec_sc, st64_sc):
    S = hq_ref.shape[0]
    C = HCHUNK
    nchunks = S // C
    BLK = HBLOCK
    nblk = S // BLK

    lg = lb_ref[...]
    e = jnp.exp(lg - jnp.max(lg, axis=0, keepdims=True))
    lb = e[0:1, :] / jnp.sum(e, axis=0, keepdims=True)

    f = lb + (1.0 - lb) * (1.0 / (1.0 + jnp.exp(-hf_ref[...])))
    lf = jnp.log(f)
    kk_sc[...] = 1.0 - f
    qq_sc[...] = hq_ref[...].astype(F32)

    row = lax.broadcasted_iota(jnp.int32, (S, LANES), 0)
    rb = 4 * BLK
    tr = lax.broadcasted_iota(jnp.int32, (rb, rb), 0)
    tc = lax.broadcasted_iota(jnp.int32, (rb, rb), 1)
    tri = jnp.where(((tr & -BLK) == (tc & -BLK)) & (tc <= tr), 1.0, 0.0).astype(BF16)
    lf3 = jnp.concatenate(_bf16_pieces(lf, 3), axis=1)
    for j in range(S // rb):
        c3 = jnp.dot(tri, lf3[j * rb:(j + 1) * rb, :], preferred_element_type=F32)
        b_sc[j * rb:(j + 1) * rb, :] = c3[:, :LANES] + c3[:, LANES:2 * LANES] + c3[:, 2 * LANES:]
    safe = jnp.max(-b_sc[...].reshape(nblk, BLK, LANES)[:, BLK - 1, :]) <= HGRN_SAFE_EXP

    lane = lax.broadcasted_iota(jnp.int32, (C, LANES), 1)
    sr = lax.broadcasted_iota(jnp.int32, (LANES, LANES), 0)
    scn = lax.broadcasted_iota(jnp.int32, (LANES, LANES), 1)
    same_head = (sr // HEAD_DIM) == (scn // HEAD_DIM)

    @pl.when(safe)
    def _factorised():
        qh_sc, kh_sc, ke_sc, qd_sc, k2_sc = w1_sc, w2_sc, w3_sc, w4_sc, w5_sc
        SB = 2 * BLK
        nsb = S // SB
        bb = b_sc[...]
        dblk = jnp.exp(bb.reshape(nblk, BLK, LANES)[:, BLK - 1:BLK, :])
        dfull = jnp.broadcast_to(dblk, (nblk, BLK, LANES)).reshape(S, LANES)
        second = (row & BLK) != 0
        d_prev = pltpu.roll(dfull, BLK, axis=0)
        d_next = pltpu.roll(dfull, S - BLK, axis=0)
        qh = qq_sc[...] * jnp.exp(bb)
        qh_sc[...] = qh.astype(BF16)
        qd_sc[...] = (qh * jnp.where(second, d_prev, 1.0)).astype(BF16)
        kh = kk_sc[...] * jnp.exp(-bb)
        kh_sc[...] = kh.astype(BF16)
        ke = kh * dfull
        ke_sc[...] = ke.astype(BF16)
        k2_sc[...] = (ke * jnp.where(second, 1.0, d_next)).astype(BF16)
        d3 = dfull.reshape(nsb, SB, LANES)
        dec_sc[pl.ds(0, nsb), :] = d3[:, 0, :] * d3[:, BLK, :]
        unroll = min(16, nsb)
        assert nsb % unroll == 0
        tn = (((0,), (0,)), ((), ()))
        nt = (((1,), (1,)), ((), ()))

        def scan(g, st):
            for u in range(unroll):
                i = g * unroll + u
                r0 = pl.multiple_of(i * SB, SB)
                st64_sc[i] = st.astype(BF16)
                upd = lax.dot_general(hi_ref[pl.ds(r0, SB), :], k2_sc[pl.ds(r0, SB), :], tn,
                                      preferred_element_type=F32)
                st = st * dec_sc[pl.ds(i, 1), :] + jnp.where(same_head, upd, 0.0)
            return st

        lax.fori_loop(0, nsb // unroll, scan, jnp.zeros((LANES, LANES), F32))

        r = lax.broadcasted_iota(jnp.int32, (2 * SB, 2 * SB), 0)
        c = lax.broadcasted_iota(jnp.int32, (2 * SB, 2 * SB), 1)
        t = r & (SB - 1)
        visible = (((c < SB) & ((t & BLK) == (c & BLK)) & ((t & (BLK - 1)) >= (c & (BLK - 1))))
                   | ((c >= SB) & (c < SB + BLK) & (t >= BLK)))
        plane = lax.broadcasted_iota(jnp.int32, (SB, LANES), 1)
        pad = jnp.zeros((BLK, LANES), BF16)

        def readout(g, _):
            for u in range(unroll):
                i = g * unroll + u
                r0 = pl.multiple_of(i * SB, SB)
                vb = hi_ref[pl.ds(r0, SB), :]
                qh2 = qh_sc[pl.ds(r0, SB), :]
                q2 = jnp.concatenate([jnp.where(plane < HEAD_DIM, qh2, jnp.zeros_like(qh2)),
                                      jnp.where(plane >= HEAD_DIM, qh2, jnp.zeros_like(qh2))], axis=0)
                kext = jnp.concatenate([kh_sc[pl.ds(r0, SB), :], ke_sc[pl.ds(r0, BLK), :], pad], axis=0)
                vext = jnp.concatenate([vb, vb[:BLK], pad], axis=0)
                sc = lax.dot_general(q2, kext, nt, preferred_element_type=F32)
                sc = jnp.where(visible, sc, 0.0).astype(BF16)
                out = jnp.dot(sc, vext, preferred_element_type=F32)
                o_inter = lax.dot_general(qd_sc[pl.ds(r0, SB), :], st64_sc[i], nt, preferred_element_type=F32)
                o_sc[pl.ds(r0, SB), :] = jnp.where(plane < HEAD_DIM, out[:SB], out[SB:]) + o_inter
            return 0

        lax.fori_loop(0, nsb // unroll, readout, 0)

    @pl.when(jnp.logical_not(safe))
    def _direct():
        qt_sc, kt_sc, s_sc, a2_sc = w1_sc, w2_sc, w3_sc, b_sc
        bb = b_sc[...]
        cl = jnp.broadcast_to(bb.reshape(nchunks, C, LANES)[:, C - 1:C, :], (nchunks, C, LANES)).reshape(S, LANES)
        aa = bb - jnp.where((row & (BLK - 1)) >= C, pltpu.roll(cl, C, axis=0), 0.0)
        al = jnp.broadcast_to(aa.reshape(nchunks, C, LANES)[:, C - 1:C, :], (nchunks, C, LANES)).reshape(S, LANES)
        qt_sc[...] = (qq_sc[...] * jnp.exp(aa)).astype(BF16)
        kt_sc[...] = (kk_sc[...] * jnp.exp(al - aa)).astype(BF16)
        dec_sc[...] = jnp.exp(aa.reshape(nchunks, C, LANES)[:, C - 1, :])
        a2_sc[...] = aa * LOG2E
        trow = lax.broadcasted_iota(jnp.int32, (C, LANES), 0)

        def gen(c, _):
            r0 = pl.multiple_of(c * C, C)
            ac = a2_sc[pl.ds(r0, C), :]
            qc = qq_sc[pl.ds(r0, C), :]
            kc = kk_sc[pl.ds(r0, C), :]
            half = C // 2
            for s in range(C):
                if s < half:
                    dec = jnp.exp2(jnp.where(trow >= s, ac - ac[s:s + 1, :], NEG_BIG))
                    p = qc * (kc[s:s + 1, :] * dec)
                else:
                    dec = jnp.exp2(jnp.where(trow[half:] >= s, ac[half:] - ac[s:s + 1, :], NEG_BIG))
                    p = jnp.concatenate([jnp.zeros((half, LANES), F32), qc[half:] * (kc[s:s + 1, :] * dec)],
                                        axis=0)
                p_sc[pl.ds(r0, C), s * LANES:(s + 1) * LANES] = p.astype(BF16)
            return 0

        lax.fori_loop(0, nchunks, gen, 0)

        er = lax.broadcasted_iota(jnp.int32, (C * LANES, LANES), 0)
        ec = lax.broadcasted_iota(jnp.int32, (C * LANES, LANES), 1)
        emat = (ec == ((er & (LANES - 1)) // HEAD_DIM) * C + er // LANES).astype(BF16)
        rb = 256

        def red(i, _):
            r0 = pl.multiple_of(i * rb, rb)
            s_sc[pl.ds(r0, rb), :] = jnp.dot(p_sc[pl.ds(r0, rb), :], emat,
                                             preferred_element_type=F32).astype(BF16)
            return 0

        lax.fori_loop(0, S // rb, red, 0)

        unroll = 16
        assert nchunks % unroll == 0

        def scan(g, st):
            for u in range(unroll):
                c = g * unroll + u
                r0 = pl.multiple_of(c * C, C)
                st16_sc[c] = st.astype(BF16)
                upd = lax.dot_general(hi_ref[pl.ds(r0, C), :], kt_sc[pl.ds(r0, C), :],
                                      (((0,), (0,)), ((), ())), preferred_element_type=F32)
                st = st * dec_sc[pl.ds(c, 1), :] + jnp.where(same_head, upd, 0.0)
            return st

        lax.fori_loop(0, nchunks // unroll, scan, jnp.zeros((LANES, LANES), F32))

        def readout(g, _):
            for u in range(unroll):
                c = g * unroll + u
                r0 = pl.multiple_of(c * C, C)
                vc = hi_ref[pl.ds(r0, C), :]
                o_inter = lax.dot_general(qt_sc[pl.ds(r0, C), :], st16_sc[c],
                                          (((1,), (1,)), ((), ())), preferred_element_type=F32)
                v2 = jnp.concatenate([jnp.where(lane < HEAD_DIM, vc, jnp.zeros_like(vc)),
                                      jnp.where(lane >= HEAD_DIM, vc, jnp.zeros_like(vc))], axis=0)
                o_intra = jnp.dot(s_sc[pl.ds(r0, C), :][:, :2 * C], v2, preferred_element_type=F32)
                o_sc[pl.ds(r0, C), :] = o_inter + o_intra
            return 0

        lax.fori_loop(0, nchunks // unroll, readout, 0)

    o = o_sc[...]
    ones_head = jnp.where(same_head, 1.0, 0.0).astype(BF16)
    sq_hi, sq_lo = _bf16_pieces(o * o, 2)
    ms = (jnp.dot(sq_hi, ones_head, preferred_element_type=F32)
          + jnp.dot(sq_lo, ones_head, preferred_element_type=F32)) * (1.0 / HEAD_DIM)
    y = o * lax.rsqrt(ms + RMS_EPS) * nw_ref[...]
    o_ref[...] = (y * hg_ref[...].astype(F32)).astype(o_ref.dtype)


def _hgrn(hq, hf, hi, hg, lb_logits, norm_w):
    B, S, W = hq.shape
    npairs = W // LANES
    nrows = lb_logits.shape[0]
    seq = pl.BlockSpec((None, S, LANES), lambda b, p: (b, 0, p))
    return pl.pallas_call(
        _hgrn_kernel,
        out_shape=jax.ShapeDtypeStruct((B, S, W), BF16),
        grid=(B, npairs),
        in_specs=[seq, seq, seq, seq,
                  pl.BlockSpec((nrows, LANES), lambda b, p: (0, p)),
                  pl.BlockSpec((1, LANES), lambda b, p: (0, p))],
        out_specs=seq,
        scratch_shapes=[pltpu.VMEM((S, LANES), F32),
                        pltpu.VMEM((S, LANES), F32),
                        pltpu.VMEM((S, LANES), F32),
                        pltpu.VMEM((S, LANES), F32),
                        pltpu.VMEM((S, LANES), BF16),
                        pltpu.VMEM((S, LANES), BF16),
                        pltpu.VMEM((S, LANES), BF16),
                        pltpu.VMEM((S, LANES), BF16),
                        pltpu.VMEM((S, LANES), BF16),
                        pltpu.VMEM((S, HCHUNK * LANES), BF16),
                        pltpu.VMEM((S // HCHUNK, LANES, LANES), BF16),
                        pltpu.VMEM((S // HCHUNK, LANES), F32),
                        pltpu.VMEM((S // HBLOCK, LANES, LANES), BF16)],
        compiler_params=_cparams(("parallel", "parallel")),
    )(hq, hf, hi, hg, lb_logits, norm_w.reshape(1, W))


def _layer_norm(v, g, b):
    mu = jnp.mean(v, axis=-1, keepdims=True)
    d = v - mu
    var = jnp.mean(d * d, axis=-1, keepdims=True)
    return d * lax.rsqrt(var + LN_EPS) * g + b


def _bf16_bits(x):
    return (pltpu.bitcast(x, jnp.uint32) + jnp.uint32(0x8000)) & jnp.uint32(0xFFFF0000)


def _store_chunks(ref, val):
    n = ref.shape[0]
    for j in range(n):
        lo = _bf16_bits(val[:, j * LANES:(j + 1) * LANES]) >> 16
        hi = _bf16_bits(val[:, (j + n) * LANES:(j + n + 1) * LANES])
        ref[j] = pltpu.bitcast(lo | hi, F32)


def _load_chunks(ref):
    words = [pltpu.bitcast(ref[j], jnp.uint32) for j in range(ref.shape[0])]
    lo = [pltpu.bitcast(w << 16, F32) for w in words]
    hi = [pltpu.bitcast(w & jnp.uint32(0xFFFF0000), F32) for w in words]
    return jnp.concatenate(lo + hi, axis=1)


def _mix_kernel(yf_ref, oh_ref, gf_ref, gh_ref, x_ref, g1_ref, sc2_ref, sh2_ref,
                wuf_ref, wuh_ref, wo_ref, lg_ref, lbias_ref, wr_ref, br_ref,
                x1_ref, h2_ref, ri_ref, rt_ref, cnt_ref, carry_sc, *, alpha, ngroups, nper):
    first = (pl.program_id(0) == 0) & (pl.program_id(1) == 0)

    @pl.when(first)
    def _():
        carry_sc[...] = jnp.zeros_like(carry_sc)

    tm = x_ref.shape[0]
    yf = jnp.dot(yf_ref[...], wuf_ref[...], preferred_element_type=F32)
    yh = jnp.dot(oh_ref[...], wuh_ref[...], preferred_element_type=F32)
    merged = gf_ref[...].astype(F32) * yf + gh_ref[...].astype(F32) * yh
    y = jnp.dot(merged.astype(BF16), wo_ref[...], preferred_element_type=F32)
    x1 = _layer_norm(alpha * x_ref[...] + g1_ref[...] * y, lg_ref[...], lbias_ref[...])
    x1_ref[...] = x1
    h2 = x1 * (1.0 + sc2_ref[...]) + sh2_ref[...]
    _store_chunks(h2_ref, h2)

    h_hi, h_lo = _bf16_pieces(h2, 2)
    hh = jnp.dot(h_hi, wr_ref[...], preferred_element_type=F32)
    logits = (hh[:, :LANES] + hh[:, LANES:]
              + jnp.dot(h_lo, wr_ref[:, :LANES], preferred_element_type=F32)) + br_ref[...]
    lt = logits.T
    rowi = lax.broadcasted_iota(jnp.int32, (LANES, tm), 0)
    big = jnp.int32(1 << 20)

    def argmax_first(vals, mask):
        mx = jnp.max(jnp.where(mask, vals, -jnp.inf), axis=0, keepdims=True)
        idx = jnp.min(jnp.where(mask & (vals == mx), rowi, big), axis=0, keepdims=True)
        return mx, idx

    gmask = rowi < ngroups
    gmax = jnp.max(jnp.where(gmask, lt, -jnp.inf), axis=0, keepdims=True)
    gexp = jnp.where(gmask, jnp.exp(lt - gmax), 0.0)
    gprob = gexp / jnp.sum(gexp, axis=0, keepdims=True)
    g_w, g_idx = argmax_first(gprob, gmask)

    lo = ngroups + g_idx * nper
    emask = (rowi >= lo) & (rowi < lo + nper)
    emax = jnp.max(jnp.where(emask, lt, -jnp.inf), axis=0, keepdims=True)
    eexp = jnp.where(emask, jnp.exp(lt - emax), 0.0)
    eprob = eexp / jnp.sum(eexp, axis=0, keepdims=True)
    p0, i0 = argmax_first(eprob, emask)
    p1, i1 = argmax_first(eprob, emask & (rowi != i0))
    den = p0 + p1
    w0 = p0 / den * g_w
    w1 = p1 / den * g_w
    e0 = i0 - ngroups
    e1 = i1 - ngroups

    oht = jnp.where((rowi == e0) | (rowi == e1), 1.0, 0.0)
    r = lax.broadcasted_iota(jnp.int32, (tm, tm), 0)
    c = lax.broadcasted_iota(jnp.int32, (tm, tm), 1)
    earlier = jnp.where(r < c, 1.0, 0.0).astype(BF16)
    er = lax.broadcasted_iota(jnp.int32, (LANES, LANES), 0)
    ec = lax.broadcasted_iota(jnp.int32, (LANES, LANES), 1)
    carry_col = jnp.sum(jnp.where(er == ec, carry_sc[...], 0.0), axis=1, keepdims=True)
    before = jnp.dot(oht.astype(BF16), earlier, preferred_element_type=F32) + carry_col
    rank0 = jnp.sum(jnp.where(rowi == e0, before, 0.0), axis=0, keepdims=True)
    rank1 = jnp.sum(jnp.where(rowi == e1, before, 0.0), axis=0, keepdims=True)
    inc_col = jnp.sum(oht, axis=1, keepdims=True)
    carry_sc[...] = carry_sc[...] + jnp.sum(jnp.where(er == ec, inc_col, 0.0), axis=0, keepdims=True)
    cnt_ref[...] = carry_sc[...]

    info = jnp.where(rowi == 0, w0, 0.0)
    info = jnp.where(rowi == 1, w1, info)
    info = jnp.where(rowi == 2, e0.astype(F32), info)
    info = jnp.where(rowi == 3, e1.astype(F32), info)
    info = jnp.where(rowi == 4, rank0, info)
    info = jnp.where(rowi == 5, rank1, info)
    rt_ref[...] = info[:ROW_TILE, :]
    ri_ref[...] = info.T


def _mix(yf, oh, gf, gh, x, g1, sc2, sh2, wuf, wuh, wo, ln_g, ln_b, wr, br, alpha, ngroups, nper, tm=512):
    B, S, D = x.shape
    W = yf.shape[2]
    tok = lambda w: pl.BlockSpec((None, tm, w), lambda b, i: (b, i, 0))
    vec = pl.BlockSpec((None, 1, D), lambda b, i: (b, 0, 0))
    full = lambda a: pl.BlockSpec(a.shape, lambda b, i: (0,) * a.ndim)
    return pl.pallas_call(
        functools.partial(_mix_kernel, alpha=alpha, ngroups=ngroups, nper=nper),
        out_shape=(jax.ShapeDtypeStruct((B, S, D), F32),
                   jax.ShapeDtypeStruct((D // WORD_LANES, B * S, LANES), F32),
                   jax.ShapeDtypeStruct((B, S, LANES), F32),
                   jax.ShapeDtypeStruct((ROW_TILE, B * S), F32),
                   jax.ShapeDtypeStruct((1, LANES), F32)),
        grid=(B, S // tm),
        in_specs=[tok(W), tok(W), tok(D), tok(D), tok(D), vec, vec, vec,
                  full(wuf), full(wuh), full(wo), full(ln_g), full(ln_b), full(wr), full(br)],
        out_specs=(tok(D),
                   pl.BlockSpec((D // WORD_LANES, tm, LANES), lambda b, i: (0, b * (S // tm) + i, 0)),
                   tok(LANES),
                   pl.BlockSpec((ROW_TILE, tm), lambda b, i: (0, b * (S // tm) + i)),
                   pl.BlockSpec((1, LANES), lambda b, i: (0, 0))),
        scratch_shapes=[pltpu.VMEM((1, LANES), F32)],
        compiler_params=_cparams(("arbitrary", "arbitrary")),
    )(yf, oh, gf, gh, x, g1, sc2, sh2, wuf, wuh, wo, ln_g, ln_b, wr, br)


def _sc_mesh():
    return plsc.VectorSubcoreMesh(core_axis_name="core", subcore_axis_name="subcore")


def _sc_pipeline(body, grid, in_specs, out_specs):
    return pltpu.emit_pipeline(body, grid=grid, in_specs=in_specs, out_specs=out_specs,
                               core_axis_name=("core", "subcore"),
                               dimension_semantics=(pltpu.PARALLEL,) * len(grid))


def _sc_scatter_rows(src, rows_a, rows_b, n_out):
    nj, t = rows_a.shape
    win = SC_WINDOW
    nc = t // win

    @pl.kernel(out_type=jax.ShapeDtypeStruct((n_out, LANES), src.dtype), mesh=_sc_mesh(), scratch_types=[])
    def scatter(x_hbm, a_hbm, b_hbm, o_hbm):
        def body(x_vmem, a_vmem, b_vmem):
            pltpu.sync_copy(x_vmem, o_hbm.at[a_vmem.at[0]])
            pltpu.sync_copy(x_vmem, o_hbm.at[b_vmem.at[0]])

        idx = pl.BlockSpec((1, win), lambda j, c: (j, c))
        _sc_pipeline(body, (nj, nc), [pl.BlockSpec((win, LANES), lambda j, c: (j * nc + c, 0)), idx, idx],
                     [])(x_hbm, a_hbm, b_hbm)

    return scatter(src, rows_a, rows_b)


def _sc_gather_rows(table, rows):
    nr, t = rows.shape
    win = SC_WINDOW
    nc = t // win

    @pl.kernel(out_type=jax.ShapeDtypeStruct((nr * t, LANES), table.dtype), mesh=_sc_mesh(), scratch_types=[])
    def gather(x_hbm, i_hbm, o_hbm):
        def body(i_vmem, o_vmem):
            pltpu.sync_copy(x_hbm.at[i_vmem.at[0]], o_vmem)

        _sc_pipeline(body, (nr, nc), [pl.BlockSpec((1, win), lambda r, c: (r, c))],
                     [pl.BlockSpec((win, LANES), lambda r, c: (r * nc + c, 0))])(i_hbm, o_hbm)

    return gather(table, rows)


W_SLOTS = 3


def _experts_kernel(tn_ref, tb_ref, run_ref, first_ref, rexp_ref, nrun_ref,
                    x_ref, wg_hbm, wu_hbm, wd_hbm, o_ref, wg_sc, wu_sc, wd_sc, sems):
    del tb_ref
    i = pl.program_id(0)
    nrows = tn_ref[i]
    run = run_ref[i]
    nruns = nrun_ref[0]

    def copies(r, slot):
        e = rexp_ref[r]
        return [pltpu.make_async_copy(hbm.at[e], buf.at[slot], sems.at[slot])
                for hbm, buf in ((wg_hbm, wg_sc), (wu_hbm, wu_sc), (wd_hbm, wd_sc))]

    def fetch(r):
        if isinstance(r, int):
            for cp in copies(r, r % W_SLOTS):
                cp.start()
            return
        for s in range(W_SLOTS):
            @pl.when(r % W_SLOTS == s)
            def _(s=s):
                for cp in copies(r, s):
                    cp.start()

    ahead = W_SLOTS - 1

    @pl.when(i == 0)
    def _():
        for r in range(ahead):
            pl.when(r < nruns)(functools.partial(fetch, r))

    for s in range(W_SLOTS):
        @pl.when((nrows > 0) & (run % W_SLOTS == s))
        def _(s=s):
            @pl.when(first_ref[i] != 0)
            def _():
                for cp in copies(run, s):
                    cp.wait()

                @pl.when(run + ahead < nruns)
                def _():
                    fetch(run + ahead)

            x = _load_chunks(x_ref)
            x = jnp.where(lax.broadcasted_iota(jnp.int32, x.shape, 0) < nrows, x, 0.0).astype(BF16)
            g = jnp.dot(x, wg_sc[s], preferred_element_type=F32)
            u = jnp.dot(x, wu_sc[s], preferred_element_type=F32)
            hid = (_silu(g) * u).astype(BF16)
            _store_chunks(o_ref, jnp.dot(hid, wd_sc[s], preferred_element_type=F32))


def _experts(tile_rows, tile_block, tile_run, tile_first, run_expert, nruns, xs, wg, wu, wd, tm):
    E, D, FF = wg.shape
    dt = D // WORD_LANES
    ntiles = tile_rows.shape[0]
    rows = pl.BlockSpec((dt, tm, LANES), lambda i, tn, tb, *_: (0, tb[i], 0))
    hbm = pl.BlockSpec(memory_space=pl.ANY)
    grid_spec = pltpu.PrefetchScalarGridSpec(
        num_scalar_prefetch=6,
        grid=(ntiles,),
        in_specs=[rows, hbm, hbm, hbm],
        out_specs=rows,
        scratch_shapes=[pltpu.VMEM((W_SLOTS, D, FF), BF16), pltpu.VMEM((W_SLOTS, D, FF), BF16),
                        pltpu.VMEM((W_SLOTS, FF, D), BF16), pltpu.SemaphoreType.DMA((W_SLOTS,))],
    )
    return pl.pallas_call(
        _experts_kernel,
        out_shape=jax.ShapeDtypeStruct((dt, ntiles * tm, LANES), F32),
        grid_spec=grid_spec,
        compiler_params=_cparams(("arbitrary",)),
    )(tile_rows, tile_block, tile_run, tile_first, run_expert, nruns, xs, wg, wu, wd)


def _combine_kernel(yg_ref, x1_ref, ri_ref, g2_ref, lg_ref, lb_ref, o_ref, *, alpha):
    ri = ri_ref[...]
    y = ri[:, 0:1] * _load_chunks(yg_ref.at[0]) + ri[:, 1:2] * _load_chunks(yg_ref.at[1])
    o_ref[...] = _layer_norm(alpha * x1_ref[...] + g2_ref[...] * y, lg_ref[...], lb_ref[...])


def _combine(yg, x1, rinfo, g2, ln_g, ln_b, alpha, tm=1024):
    B, S, D = x1.shape
    nb = S // tm
    return pl.pallas_call(
        functools.partial(_combine_kernel, alpha=alpha),
        out_shape=jax.ShapeDtypeStruct((B, S, D), F32),
        grid=(B, nb),
        in_specs=[pl.BlockSpec((2, D // WORD_LANES, tm, LANES), lambda b, i: (0, 0, b * nb + i, 0)),
                  pl.BlockSpec((None, tm, D), lambda b, i: (b, i, 0)),
                  pl.BlockSpec((None, tm, LANES), lambda b, i: (b, i, 0)),
                  pl.BlockSpec((None, 1, D), lambda b, i: (b, 0, 0)),
                  pl.BlockSpec((1, D), lambda b, i: (0, 0)),
                  pl.BlockSpec((1, D), lambda b, i: (0, 0))],
        out_specs=pl.BlockSpec((None, tm, D), lambda b, i: (b, i, 0)),
        compiler_params=_cparams(("parallel", "parallel")),
    )(yg, x1, rinfo, g2, ln_g, ln_b)


def kernel(x, c, w_ada, b_ada, w_in, b_fox_forget, hgrn_lb_logits, hgrn_norm_w, w_up_fox, w_up_hgrn, w_out,
           ln1_g, ln1_b, w_router_group, b_router_group, w_router_expert, b_router_expert,
           w_expert_gate, w_expert_up, w_expert_down, ln2_g, ln2_b):
    B, S, D = x.shape
    depth = w_ada.shape[0]
    assert depth == 1, "single-layer block"
    fox_heads = b_fox_forget.shape[1]
    fox_w = fox_heads * HEAD_DIM
    hgrn_w = hgrn_norm_w.shape[1]
    ngroups = w_router_group.shape[2]
    nexp = w_router_expert.shape[2]
    nper = nexp // ngroups
    alpha = (2 * depth) ** 0.25
    T = B * S

    ada = _ada(c, w_ada[0], b_ada[0])
    sh1, sc1, g1, sh2, sc2, g2 = [a.reshape(B, 1, D) for a in jnp.split(ada, 6, axis=-1)]

    wi = w_in[0]
    o_ff = 3 * fox_w
    w_fox = jnp.pad(wi[:, :o_ff + fox_heads], ((0, 0), (0, LANES - fox_heads))).astype(BF16)
    w_rest = wi[:, o_ff + fox_heads:].astype(BF16)
    widths = [fox_w, fox_w, fox_w, LANES, hgrn_w, hgrn_w, hgrn_w, hgrn_w, D, D]
    segs, off = [], 0
    for n, w in enumerate(widths):
        if n == 4:
            off = 0
        segs.append((off, off + w))
        off += w
    fq, fk, fv, ffp, hq, hf, hi, hg, gf, gh = _inproj(x, sc1, sh1, w_fox, w_rest, segs)

    bias_p = jnp.zeros((1, LANES), F32).at[0, :fox_heads].set(b_fox_forget[0])
    cum = _foxcum(ffp, bias_p)
    y_fox, wg_b, wu_b, wd_b = _fox(fq, fk, fv, cum, (w_expert_gate[0], w_expert_up[0], w_expert_down[0]))

    o_h = _hgrn(hq, hf, hi, hg, hgrn_lb_logits, hgrn_norm_w[0])

    wr = jnp.zeros((D, LANES), F32).at[:, :ngroups].set(w_router_group[0]).at[:, ngroups:ngroups + nexp].set(
        w_router_expert[0])
    wr_hi = lax.bitcast_convert_type(lax.bitcast_convert_type(wr, jnp.uint32) & jnp.uint32(0xFFFF0000), F32)
    wr = jnp.concatenate([wr_hi.astype(BF16), (wr - wr_hi).astype(BF16)], axis=1)
    br = jnp.zeros((1, LANES), F32).at[0, :ngroups].set(b_router_group[0]).at[0, ngroups:ngroups + nexp].set(
        b_router_expert[0])
    x1, h2, rinfo, fields, counts = _mix(
        y_fox, o_h, gf, gh, x, g1, sc2, sh2,
        w_up_fox[0].astype(BF16), w_up_hgrn[0].astype(BF16), w_out[0].astype(BF16),
        ln1_g[0].reshape(1, D), ln1_b[0].reshape(1, D), wr, br, alpha, ngroups, nper)

    tm_e = 512
    dt = D // WORD_LANES
    ntiles = (2 * T) // tm_e + nexp
    nslots = ntiles * tm_e
    cnt = counts[0, :nexp].astype(jnp.int32)
    padded = ((cnt + tm_e - 1) // tm_e) * tm_e
    ends = jnp.cumsum(padded)
    starts = ends - padded
    eid = fields[2:4].astype(jnp.int32)
    rank = fields[4:6].astype(jnp.int32)
    first = jnp.sum(jnp.where(eid[None] == jnp.arange(nexp, dtype=jnp.int32)[:, None, None],
                              starts[:, None, None], 0), axis=0)
    pos = first + rank
    tile_start = jnp.arange(ntiles, dtype=jnp.int32) * tm_e
    tile_block = jnp.minimum(jnp.arange(ntiles, dtype=jnp.int32), ends[-1] // tm_e - 1)
    tile_expert = jnp.minimum(jnp.sum((tile_start[:, None] >= ends[None, :]).astype(jnp.int32), axis=1), nexp - 1)
    tile_rows = jnp.clip(starts[tile_expert] + cnt[tile_expert] - tile_start, 0, tm_e)
    used = jnp.cumsum((cnt > 0).astype(jnp.int32))
    nruns = used[-1:]
    run_expert = jnp.sum((used[None, :] <= jnp.arange(nexp + 2, dtype=jnp.int32)[:, None]).astype(jnp.int32), axis=1)
    run_expert = jnp.minimum(run_expert, nexp - 1)
    tile_run = used[tile_expert] - 1
    prev_expert = jnp.concatenate([jnp.full((1,), -1, jnp.int32), tile_expert[:-1]])
    tile_first = ((tile_rows > 0) & (tile_expert != prev_expert)).astype(jnp.int32)
    rows = pos[:, None, :] + (jnp.arange(dt, dtype=jnp.int32) * nslots)[None, :, None]

    xs = _sc_scatter_rows(h2.reshape(dt * T, LANES), rows[0], rows[1], dt * nslots)
    ys = _experts(tile_rows, tile_block, tile_run, tile_first, run_expert, nruns,
                  xs.reshape(dt, nslots, LANES), wg_b, wu_b, wd_b, tm_e)
    yg = _sc_gather_rows(ys.reshape(dt * nslots, LANES), rows.reshape(2 * dt, T))
    return _combine(yg.reshape(2, dt, T, LANES), x1, rinfo, g2,
                    ln2_g[0].reshape(1, D), ln2_b[0].reshape(1, D), alpha)
```

```python
import functools

import jax
import jax.numpy as jnp
from jax import lax
from jax.experimental import pallas as pl
from jax.experimental.pallas import tpu as pltpu
from jax.experimental.pallas import tpu_sc as plsc

F32 = jnp.float32
BF16 = jnp.bfloat16

LANES = 128
HEAD_DIM = 64
LN_EPS = 1e-5
RMS_EPS = 1e-6
LOG2E = 1.4426950408889634
NEG_BIG = -1e30
HCHUNK = 16
HBLOCK = 64
HGRN_SAFE_EXP = 60.0
ROW_TILE = 8
WORD_LANES = 2 * LANES
SC_WINDOW = 256
VMEM_LIMIT = 56 * 1024 * 1024


def _cparams(sem, vmem=VMEM_LIMIT):
    return pltpu.CompilerParams(dimension_semantics=sem, vmem_limit_bytes=vmem)


def _sigmoid(x):
    return 0.5 * jnp.tanh(0.5 * x) + 0.5


def _silu(x):
    return x * _sigmoid(x)


def _bf16_pieces(x, n):
    pieces = []
    for _ in range(n):
        top = pltpu.bitcast(pltpu.bitcast(x, jnp.uint32) & jnp.uint32(0xFFFF0000), F32)
        pieces.append(top.astype(BF16))
        x = x - top
    return pieces


def _exact_matrix_dot(m, x):
    r = jnp.dot(m, jnp.concatenate(_bf16_pieces(x, 3), axis=1), preferred_element_type=F32)
    return r[:, :LANES] + r[:, LANES:2 * LANES] + r[:, 2 * LANES:]


def _ada_kernel(c_ref, w_ref, b_ref, o_ref):
    c_hi, c_lo = _bf16_pieces(_silu(c_ref[...]), 2)
    w_hi, w_lo = _bf16_pieces(w_ref[...], 2)
    o_ref[...] = (jnp.dot(c_hi, w_hi, preferred_element_type=F32) + jnp.dot(c_hi, w_lo, preferred_element_type=F32)
                  + jnp.dot(c_lo, w_hi, preferred_element_type=F32)) + b_ref[...]


def _ada(c, w_ada, b_ada):
    B, D = c.shape
    N = w_ada.shape[1]
    tn = 1024
    return pl.pallas_call(
        _ada_kernel,
        out_shape=jax.ShapeDtypeStruct((B, N), F32),
        grid=(N // tn,),
        in_specs=[pl.BlockSpec((B, D), lambda j: (0, 0)),
                  pl.BlockSpec((D, tn), lambda j: (0, j)),
                  pl.BlockSpec((1, tn), lambda j: (0, j))],
        out_specs=pl.BlockSpec((B, tn), lambda j: (0, j)),
        compiler_params=_cparams(("arbitrary",)),
    )(c, w_ada, b_ada.reshape(1, N))


N_FOX_SEGS = 4
SILU_SEGS = (4, 7)
SIGMOID_SEGS = (8, 9)


def _inproj_kernel(x_ref, sc_ref, sh_ref, wf_ref, wr_ref,
                   fq_ref, fk_ref, fv_ref, ff_ref, hq_ref, hf_ref, hi_ref, hg_ref, gf_ref, gh_ref,
                   *, segs, q_scale):
    h = (x_ref[...] * (1.0 + sc_ref[...]) + sh_ref[...]).astype(BF16)
    outs = (fq_ref, fk_ref, fv_ref, ff_ref, hq_ref, hf_ref, hi_ref, hg_ref, gf_ref, gh_ref)
    for idx, (o_ref, (a, b)) in enumerate(zip(outs, segs)):
        w_ref = wf_ref if idx < N_FOX_SEGS else wr_ref
        r = jnp.dot(h, w_ref[:, a:b], preferred_element_type=F32)
        if idx == 0:
            r = r * q_scale
        elif idx in SILU_SEGS:
            r = _silu(r)
        elif idx in SIGMOID_SEGS:
            r = _sigmoid(r)
        o_ref[...] = r.astype(o_ref.dtype)


def _inproj(x, sc1, sh1, w_fox, w_rest, segs, tm=256):
    B, S, D = x.shape
    widths = [b - a for a, b in segs]
    dtypes = [BF16, BF16, BF16, F32, BF16, F32, BF16, BF16, BF16, BF16]
    out_shape = tuple(jax.ShapeDtypeStruct((B, S, w), dt) for w, dt in zip(widths, dtypes))
    out_specs = tuple(pl.BlockSpec((None, tm, w), lambda b, i: (b, i, 0)) for w in widths)
    vec = pl.BlockSpec((None, 1, D), lambda b, i: (b, 0, 0))
    return pl.pallas_call(
        functools.partial(_inproj_kernel, segs=tuple(segs), q_scale=HEAD_DIM ** -0.5 * LOG2E),
        out_shape=out_shape,
        grid=(B, S // tm),
        in_specs=[pl.BlockSpec((None, tm, D), lambda b, i: (b, i, 0)), vec, vec,
                  pl.BlockSpec(w_fox.shape, lambda b, i: (0, 0)),
                  pl.BlockSpec(w_rest.shape, lambda b, i: (0, 0))],
        out_specs=out_specs,
        compiler_params=_cparams(("parallel", "parallel")),
    )(x, sc1, sh1, w_fox, w_rest)


NCUM = 3


def _foxcum_kernel(ff_ref, b_ref, o_ref, *, blk, npairs):
    S = ff_ref.shape[0]
    r = lax.broadcasted_iota(jnp.int32, (blk, blk), 0)
    c = lax.broadcasted_iota(jnp.int32, (blk, blk), 1)
    lower = jnp.where(r >= c, 1.0, 0.0).astype(BF16)
    pr = lax.broadcasted_iota(jnp.int32, (NCUM * LANES, npairs * LANES), 0)
    pc = lax.broadcasted_iota(jnp.int32, (NCUM * LANES, npairs * LANES), 1)
    piece, head = pr // LANES, pr % LANES
    pair, lane = pc // LANES, pc % LANES
    place = jnp.where(((head == 2 * pair) & (lane == HEAD_DIM + piece)) | ((head == 2 * pair + 1) & (lane == piece)),
                      1.0, 0.0).astype(BF16)
    carry = jnp.zeros((1, LANES), F32)
    for j in range(S // blk):
        z = ff_ref[j * blk:(j + 1) * blk, :] + b_ref[...]
        lf = jnp.minimum(z, 0.0) - jnp.log(1.0 + jnp.exp(-jnp.abs(z)))
        cum = _exact_matrix_dot(lower, lf) + carry
        carry = cum[blk - 1:blk, :]
        pieces, rest = [], cum * LOG2E
        for _ in range(NCUM):
            pieces.append(rest.astype(BF16))
            rest = rest - pieces[-1].astype(F32)
        placed = jnp.dot(jnp.concatenate(pieces, axis=1), place, preferred_element_type=F32)
        o_ref[j * blk:(j + 1) * blk, :] = (-placed).astype(BF16)


def _foxcum(ffp, bias_p, npairs, blk=256):
    B, S, _ = ffp.shape
    return pl.pallas_call(
        functools.partial(_foxcum_kernel, blk=blk, npairs=npairs),
        out_shape=jax.ShapeDtypeStruct((B, S, npairs * LANES), BF16),
        grid=(B,),
        in_specs=[pl.BlockSpec((None, S, LANES), lambda b: (b, 0, 0)),
                  pl.BlockSpec((1, LANES), lambda b: (0, 0))],
        out_specs=pl.BlockSpec((None, S, npairs * LANES), lambda b: (b, 0, 0)),
        compiler_params=_cparams(("parallel",)),
    )(ffp, bias_p)


def _fox_kernel(q_ref, k_ref, v_ref, c_ref, wg_ref, wu_ref, wd_ref, o_ref, wgb_ref, wub_ref, wdb_ref,
                ka_sc, kb_sc, va_sc, vb_sc, *, tq, tk):
    wgb_ref[...] = wg_ref[...].astype(BF16)
    wub_ref[...] = wu_ref[...].astype(BF16)
    wdb_ref[...] = wd_ref[...].astype(BF16)

    qi = pl.program_id(2)
    S = k_ref.shape[0]

    @pl.when(qi == 0)
    def _():
        lane = lax.broadcasted_iota(jnp.int32, (S, LANES), 1)
        negcum = c_ref[...].astype(F32)
        k2 = k_ref[...].astype(F32)
        ka_sc[...] = jnp.where(lane < HEAD_DIM, k2, negcum).astype(BF16)
        kb_sc[...] = jnp.where(lane >= HEAD_DIM, k2, negcum).astype(BF16)
        vt = v_ref[...].astype(F32).T
        row = lax.broadcasted_iota(jnp.int32, (LANES, S), 0)
        va_sc[...] = jnp.where(row < HEAD_DIM, vt, jnp.where(row == HEAD_DIM, 1.0, 0.0)).astype(BF16)
        vb_sc[...] = jnp.where(row >= HEAD_DIM, vt, jnp.where(row == 0, 1.0, 0.0)).astype(BF16)

    q2 = q_ref[...].astype(F32)
    qlane = lax.broadcasted_iota(jnp.int32, (tq, LANES), 1)
    qa = jnp.where(qlane < HEAD_DIM, q2, jnp.where(qlane < HEAD_DIM + NCUM, 1.0, 0.0)).astype(BF16)
    qb = jnp.where(qlane >= HEAD_DIM, q2, jnp.where(qlane < NCUM, 1.0, 0.0)).astype(BF16)
    nsub = tq // tk

    def block(k0, carry, diag_off):
        q0 = 0 if diag_off is None else diag_off
        out = []
        for ksc, vsc, qh, (m, acc) in ((ka_sc, va_sc, qa, carry[:2]), (kb_sc, vb_sc, qb, carry[2:])):
            st = lax.dot_general(ksc[pl.ds(k0, tk), :], qh[q0:, :], (((1,), (1,)), ((), ())),
                                 preferred_element_type=F32)
            if diag_off is not None:
                st = jnp.where(lax.broadcasted_iota(jnp.int32, st.shape, 0)
                               <= lax.broadcasted_iota(jnp.int32, st.shape, 1), st, NEG_BIG)
            m_old = m[:, q0:]
            m_new = jnp.maximum(m_old, jnp.max(st, axis=0, keepdims=True))
            pt = jnp.exp2(st - m_new).astype(BF16)
            acc_new = (jnp.exp2(m_old - m_new) * acc[:, q0:]
                       + jnp.dot(vsc[:, pl.ds(k0, tk)], pt, preferred_element_type=F32))
            if q0:
                m_new = jnp.concatenate([m[:, :q0], m_new], axis=1)
                acc_new = jnp.concatenate([acc[:, :q0], acc_new], axis=1)
            out += [m_new, acc_new]
        return tuple(out)

    def group(j, carry):
        k0 = pl.multiple_of(j * (nsub * tk), nsub * tk)
        for u in range(nsub):
            carry = block(k0 + u * tk, carry, None)
        return carry

    m0 = jnp.full((1, tq), NEG_BIG, F32)
    a0 = jnp.zeros((LANES, tq), F32)
    carry = lax.fori_loop(0, qi, group, (m0, a0, m0, a0))
    for d in range(nsub):
        carry = block(pl.multiple_of(qi * tq + d * tk, tk), carry, d * tk)
    _, aa, _, ab = carry
    row = lax.broadcasted_iota(jnp.int32, (LANES, tq), 0)
    ot = jnp.where(row < HEAD_DIM, aa * (1.0 / aa[HEAD_DIM:HEAD_DIM + 1, :]), ab * (1.0 / ab[0:1, :]))
    o_ref[...] = ot.T.astype(o_ref.dtype)


def _fox(fq, fk, fv, cum, expert_w, tq=2048, tk=512):
    B, S, W = fq.shape
    tq = min(tq, S)
    assert tq % tk == 0 and S % tq == 0
    npairs = W // LANES
    nq = S // tq
    nsteps = B * npairs * nq
    nexp = expert_w[0].shape[0]
    assert nexp % nsteps == 0
    eb = nexp // nsteps
    wspec = lambda w: pl.BlockSpec((eb,) + w.shape[1:], lambda b, p, i: ((b * npairs + p) * nq + i, 0, 0))
    return pl.pallas_call(
        functools.partial(_fox_kernel, tq=tq, tk=tk),
        out_shape=(jax.ShapeDtypeStruct((B, S, W), BF16),) + tuple(
            jax.ShapeDtypeStruct(w.shape, BF16) for w in expert_w),
        grid=(B, npairs, nq),
        in_specs=[pl.BlockSpec((None, tq, LANES), lambda b, p, i: (b, i, p)),
                  pl.BlockSpec((None, S, LANES), lambda b, p, i: (b, 0, p)),
                  pl.BlockSpec((None, S, LANES), lambda b, p, i: (b, 0, p)),
                  pl.BlockSpec((None, S, LANES), lambda b, p, i: (b, 0, p))] + [wspec(w) for w in expert_w],
        out_specs=(pl.BlockSpec((None, tq, LANES), lambda b, p, i: (b, i, p)),) + tuple(
            wspec(w) for w in expert_w),
        scratch_shapes=[pltpu.VMEM((S, LANES), BF16), pltpu.VMEM((S, LANES), BF16),
                        pltpu.VMEM((LANES, S), BF16), pltpu.VMEM((LANES, S), BF16)],
        compiler_params=_cparams(("parallel", "parallel", "arbitrary")),
    )(fq, fk, fv, cum, *expert_w)


def _hgrn_kernel(hq_ref, hf_ref, hi_ref, hg_ref, lb_ref, nw_ref, o_ref,
                 b_sc, kk_sc, qq_sc, o_sc, w1_sc, w2_sc, w3_sc, w4_sc, w5_sc,
                 p_sc, st16_sc, dec_sc, st64_sc):
    S = hq_ref.shape[0]
    C = HCHUNK
    nchunks = S // C
    BLK = HBLOCK
    nblk = S // BLK

    lg = lb_ref[...]
    e = jnp.exp(lg - jnp.max(lg, axis=0, keepdims=True))
    lb = e[0:1, :] / jnp.sum(e, axis=0, keepdims=True)

    f = lb + (1.0 - lb) * (1.0 / (1.0 + jnp.exp(-hf_ref[...])))
    lf = jnp.log(f)
    kk_sc[...] = 1.0 - f
    qq_sc[...] = hq_ref[...].astype(F32)

    row = lax.broadcasted_iota(jnp.int32, (S, LANES), 0)
    rb = 4 * BLK
    tr = lax.broadcasted_iota(jnp.int32, (rb, rb), 0)
    tc = lax.broadcasted_iota(jnp.int32, (rb, rb), 1)
    tri = jnp.where(((tr & -BLK) == (tc & -BLK)) & (tc <= tr), 1.0, 0.0).astype(BF16)
    lf3 = jnp.concatenate(_bf16_pieces(lf, 3), axis=1)
    for j in range(S // rb):
        c3 = jnp.dot(tri, lf3[j * rb:(j + 1) * rb, :], preferred_element_type=F32)
        b_sc[j * rb:(j + 1) * rb, :] = c3[:, :LANES] + c3[:, LANES:2 * LANES] + c3[:, 2 * LANES:]
    safe = jnp.max(-b_sc[...].reshape(nblk, BLK, LANES)[:, BLK - 1, :]) <= HGRN_SAFE_EXP

    lane = lax.broadcasted_iota(jnp.int32, (C, LANES), 1)
    sr = lax.broadcasted_iota(jnp.int32, (LANES, LANES), 0)
    scn = lax.broadcasted_iota(jnp.int32, (LANES, LANES), 1)
    same_head = (sr // HEAD_DIM) == (scn // HEAD_DIM)

    @pl.when(safe)
    def _factorised():
        qh_sc, kh_sc, ke_sc, qd_sc, k2_sc = w1_sc, w2_sc, w3_sc, w4_sc, w5_sc
        SB = 2 * BLK
        nsb = S // SB
        bb = b_sc[...]
        dblk = jnp.exp(bb.reshape(nblk, BLK, LANES)[:, BLK - 1:BLK, :])
        dfull = jnp.broadcast_to(dblk, (nblk, BLK, LANES)).reshape(S, LANES)
        second = (row & BLK) != 0
        d_prev = pltpu.roll(dfull, BLK, axis=0)
        d_next = pltpu.roll(dfull, S - BLK, axis=0)
        qh = qq_sc[...] * jnp.exp(bb)
        qh_sc[...] = qh.astype(BF16)
        qd_sc[...] = (qh * jnp.where(second, d_prev, 1.0)).astype(BF16)
        kh = kk_sc[...] * jnp.exp(-bb)
        kh_sc[...] = kh.astype(BF16)
        ke = kh * dfull
        ke_sc[...] = ke.astype(BF16)
        k2_sc[...] = (ke * jnp.where(second, 1.0, d_next)).astype(BF16)
        d3 = dfull.reshape(nsb, SB, LANES)
        dec_sc[pl.ds(0, nsb), :] = d3[:, 0, :] * d3[:, BLK, :]
        unroll = min(16, nsb)
        assert nsb % unroll == 0
        tn = (((0,), (0,)), ((), ()))
        nt = (((1,), (1,)), ((), ()))

        def scan(g, st):
            for u in range(unroll):
                i = g * unroll + u
                r0 = pl.multiple_of(i * SB, SB)
                st64_sc[i] = st.astype(BF16)
                upd = lax.dot_general(hi_ref[pl.ds(r0, SB), :], k2_sc[pl.ds(r0, SB), :], tn,
                                      preferred_element_type=F32)
                st = st * dec_sc[pl.ds(i, 1), :] + jnp.where(same_head, upd, 0.0)
            return st

        lax.fori_loop(0, nsb // unroll, scan, jnp.zeros((LANES, LANES), F32))

        r = lax.broadcasted_iota(jnp.int32, (2 * SB, 2 * SB), 0)
        c = lax.broadcasted_iota(jnp.int32, (2 * SB, 2 * SB), 1)
        t = r & (SB - 1)
        visible = (((c < SB) & ((t & BLK) == (c & BLK)) & ((t & (BLK - 1)) >= (c & (BLK - 1))))
                   | ((c >= SB) & (c < SB + BLK) & (t >= BLK)))
        plane = lax.broadcasted_iota(jnp.int32, (SB, LANES), 1)
        pad = jnp.zeros((BLK, LANES), BF16)

        def readout(g, _):
            for u in range(unroll):
                i = g * unroll + u
                r0 = pl.multiple_of(i * SB, SB)
                vb = hi_ref[pl.ds(r0, SB), :]
                qh2 = qh_sc[pl.ds(r0, SB), :]
                q2 = jnp.concatenate([jnp.where(plane < HEAD_DIM, qh2, jnp.zeros_like(qh2)),
                                      jnp.where(plane >= HEAD_DIM, qh2, jnp.zeros_like(qh2))], axis=0)
                kext = jnp.concatenate([kh_sc[pl.ds(r0, SB), :], ke_sc[pl.ds(r0, BLK), :], pad], axis=0)
                vext = jnp.concatenate([vb, vb[:BLK], pad], axis=0)
                sc = lax.dot_general(q2, kext, nt, preferred_element_type=F32)
                sc = jnp.where(visible, sc, 0.0).astype(BF16)
                out = jnp.dot(sc, vext, preferred_element_type=F32)
                o_inter = lax.dot_general(qd_sc[pl.ds(r0, SB), :], st64_sc[i], nt, preferred_element_type=F32)
                o_sc[pl.ds(r0, SB), :] = jnp.where(plane < HEAD_DIM, out[:SB], out[SB:]) + o_inter
            return 0

        lax.fori_loop(0, nsb // unroll, readout, 0)

    @pl.when(jnp.logical_not(safe))
    def _direct():
        qt_sc, kt_sc, s_sc, a2_sc = w1_sc, w2_sc, w3_sc, b_sc
        bb = b_sc[...]
        cl = jnp.broadcast_to(bb.reshape(nchunks, C, LANES)[:, C - 1:C, :], (nchunks, C, LANES)).reshape(S, LANES)
        aa = bb - jnp.where((row & (BLK - 1)) >= C, pltpu.roll(cl, C, axis=0), 0.0)
        al = jnp.broadcast_to(aa.reshape(nchunks, C, LANES)[:, C - 1:C, :], (nchunks, C, LANES)).reshape(S, LANES)
        qt_sc[...] = (qq_sc[...] * jnp.exp(aa)).astype(BF16)
        kt_sc[...] = (kk_sc[...] * jnp.exp(al - aa)).astype(BF16)
        dec_sc[...] = jnp.exp(aa.reshape(nchunks, C, LANES)[:, C - 1, :])
        a2_sc[...] = aa * LOG2E
        trow = lax.broadcasted_iota(jnp.int32, (C, LANES), 0)

        def gen(c, _):
            r0 = pl.multiple_of(c * C, C)
            ac = a2_sc[pl.ds(r0, C), :]
            qc = qq_sc[pl.ds(r0, C), :]
            kc = kk_sc[pl.ds(r0, C), :]
            half = C // 2
            for s in range(C):
                if s < half:
                    dec = jnp.exp2(jnp.where(trow >= s, ac - ac[s:s + 1, :], NEG_BIG))
                    p = qc * (kc[s:s + 1, :] * dec)
                else:
                    dec = jnp.exp2(jnp.where(trow[half:] >= s, ac[half:] - ac[s:s + 1, :], NEG_BIG))
                    p = jnp.concatenate([jnp.zeros((half, LANES), F32), qc[half:] * (kc[s:s + 1, :] * dec)],
                                        axis=0)
                p_sc[pl.ds(r0, C), s * LANES:(s + 1) * LANES] = p.astype(BF16)
            return 0

        lax.fori_loop(0, nchunks, gen, 0)

        er = lax.broadcasted_iota(jnp.int32, (C * LANES, LANES), 0)
        ec = lax.broadcasted_iota(jnp.int32, (C * LANES, LANES), 1)
        emat = (ec == ((er & (LANES - 1)) // HEAD_DIM) * C + er // LANES).astype(BF16)
        rb = 256

        def red(i, _):
            r0 = pl.multiple_of(i * rb, rb)
            s_sc[pl.ds(r0, rb), :] = jnp.dot(p_sc[pl.ds(r0, rb), :], emat,
                                             preferred_element_type=F32).astype(BF16)
            return 0

        lax.fori_loop(0, S // rb, red, 0)

        unroll = 16
        assert nchunks % unroll == 0

        def scan(g, st):
            for u in range(unroll):
                c = g * unroll + u
                r0 = pl.multiple_of(c * C, C)
                st16_sc[c] = st.astype(BF16)
                upd = lax.dot_general(hi_ref[pl.ds(r0, C), :], kt_sc[pl.ds(r0, C), :],
                                      (((0,), (0,)), ((), ())), preferred_element_type=F32)
                st = st * dec_sc[pl.ds(c, 1), :] + jnp.where(same_head, upd, 0.0)
            return st

        lax.fori_loop(0, nchunks // unroll, scan, jnp.zeros((LANES, LANES), F32))

        def readout(g, _):
            for u in range(unroll):
                c = g * unroll + u
                r0 = pl.multiple_of(c * C, C)
                vc = hi_ref[pl.ds(r0, C), :]
                o_inter = lax.dot_general(qt_sc[pl.ds(r0, C), :], st16_sc[c],
                                          (((1,), (1,)), ((), ())), preferred_element_type=F32)
                v2 = jnp.concatenate([jnp.where(lane < HEAD_DIM, vc, jnp.zeros_like(vc)),
                                      jnp.where(lane >= HEAD_DIM, vc, jnp.zeros_like(vc))], axis=0)
                o_intra = jnp.dot(s_sc[pl.ds(r0, C), :][:, :2 * C], v2, preferred_element_type=F32)
                o_sc[pl.ds(r0, C), :] = o_inter + o_intra
            return 0

        lax.fori_loop(0, nchunks // unroll, readout, 0)

    o = o_sc[...]
    ones_head = jnp.where(same_head, 1.0, 0.0).astype(BF16)
    sq_hi, sq_lo = _bf16_pieces(o * o, 2)
    ms = (jnp.dot(sq_hi, ones_head, preferred_element_type=F32)
          + jnp.dot(sq_lo, ones_head, preferred_element_type=F32)) * (1.0 / HEAD_DIM)
    y = o * lax.rsqrt(ms + RMS_EPS) * nw_ref[...]
    o_ref[...] = (y * hg_ref[...].astype(F32)).astype(o_ref.dtype)


def _hgrn(hq, hf, hi, hg, lb_logits, norm_w):
    B, S, W = hq.shape
    npairs = W // LANES
    nrows = lb_logits.shape[0]
    seq = pl.BlockSpec((None, S, LANES), lambda b, p: (b, 0, p))
    return pl.pallas_call(
        _hgrn_kernel,
        out_shape=jax.ShapeDtypeStruct((B, S, W), BF16),
        grid=(B, npairs),
        in_specs=[seq, seq, seq, seq,
                  pl.BlockSpec((nrows, LANES), lambda b, p: (0, p)),
                  pl.BlockSpec((1, LANES), lambda b, p: (0, p))],
        out_specs=seq,
        scratch_shapes=[pltpu.VMEM((S, LANES), F32),
                        pltpu.VMEM((S, LANES), F32),
                        pltpu.VMEM((S, LANES), F32),
                        pltpu.VMEM((S, LANES), F32),
                        pltpu.VMEM((S, LANES), BF16),
                        pltpu.VMEM((S, LANES), BF16),
                        pltpu.VMEM((S, LANES), BF16),
                        pltpu.VMEM((S, LANES), BF16),
                        pltpu.VMEM((S, LANES), BF16),
                        pltpu.VMEM((S, HCHUNK * LANES), BF16),
                        pltpu.VMEM((S // HCHUNK, LANES, LANES), BF16),
                        pltpu.VMEM((S // HCHUNK, LANES), F32),
                        pltpu.VMEM((S // HBLOCK, LANES, LANES), BF16)],
        compiler_params=_cparams(("parallel", "parallel")),
    )(hq, hf, hi, hg, lb_logits, norm_w.reshape(1, W))


def _layer_norm(v, g, b):
    mu = jnp.mean(v, axis=-1, keepdims=True)
    d = v - mu
    var = jnp.mean(d * d, axis=-1, keepdims=True)
    return d * lax.rsqrt(var + LN_EPS) * g + b


def _bf16_bits(x):
    return (pltpu.bitcast(x, jnp.uint32) + jnp.uint32(0x8000)) & jnp.uint32(0xFFFF0000)


def _store_chunks(ref, val):
    n = ref.shape[0]
    for j in range(n):
        lo = _bf16_bits(val[:, j * LANES:(j + 1) * LANES]) >> 16
        hi = _bf16_bits(val[:, (j + n) * LANES:(j + n + 1) * LANES])
        ref[j] = pltpu.bitcast(lo | hi, F32)


def _load_chunks(ref):
    words = [pltpu.bitcast(ref[j], jnp.uint32) for j in range(ref.shape[0])]
    lo = [pltpu.bitcast(w << 16, F32) for w in words]
    hi = [pltpu.bitcast(w & jnp.uint32(0xFFFF0000), F32) for w in words]
    return jnp.concatenate(lo + hi, axis=1)


def _mix_kernel(yf_ref, oh_ref, gf_ref, gh_ref, x_ref, g1_ref, sc2_ref, sh2_ref,
                wuf_ref, wuh_ref, wo_ref, lg_ref, lbias_ref, wr_ref, br_ref,
                x1_ref, h2_ref, ri_ref, rt_ref, cnt_ref, carry_sc, *, alpha, ngroups, nper):
    first = (pl.program_id(0) == 0) & (pl.program_id(1) == 0)

    @pl.when(first)
    def _():
        carry_sc[...] = jnp.zeros_like(carry_sc)

    tm = x_ref.shape[0]
    yf = jnp.dot(yf_ref[...], wuf_ref[...], preferred_element_type=F32)
    yh = jnp.dot(oh_ref[...], wuh_ref[...], preferred_element_type=F32)
    merged = gf_ref[...].astype(F32) * yf + gh_ref[...].astype(F32) * yh
    y = jnp.dot(merged.astype(BF16), wo_ref[...], preferred_element_type=F32)
    x1 = _layer_norm(alpha * x_ref[...] + g1_ref[...] * y, lg_ref[...], lbias_ref[...])
    x1_ref[...] = x1
    h2 = x1 * (1.0 + sc2_ref[...]) + sh2_ref[...]
    _store_chunks(h2_ref, h2)

    h_hi, h_lo = _bf16_pieces(h2, 2)
    hh = jnp.dot(h_hi, wr_ref[...], preferred_element_type=F32)
    logits = (hh[:, :LANES] + hh[:, LANES:]
              + jnp.dot(h_lo, wr_ref[:, :LANES], preferred_element_type=F32)) + br_ref[...]
    lt = logits.T
    rowi = lax.broadcasted_iota(jnp.int32, (LANES, tm), 0)
    big = jnp.int32(1 << 20)

    def argmax_first(vals, mask):
        mx = jnp.max(jnp.where(mask, vals, -jnp.inf), axis=0, keepdims=True)
        idx = jnp.min(jnp.where(mask & (vals == mx), rowi, big), axis=0, keepdims=True)
        return mx, idx

    gmask = rowi < ngroups
    gmax = jnp.max(jnp.where(gmask, lt, -jnp.inf), axis=0, keepdims=True)
    gexp = jnp.where(gmask, jnp.exp(lt - gmax), 0.0)
    gprob = gexp / jnp.sum(gexp, axis=0, keepdims=True)
    g_w, g_idx = argmax_first(gprob, gmask)

    lo = ngroups + g_idx * nper
    emask = (rowi >= lo) & (rowi < lo + nper)
    emax = jnp.max(jnp.where(emask, lt, -jnp.inf), axis=0, keepdims=True)
    eexp = jnp.where(emask, jnp.exp(lt - emax), 0.0)
    eprob = eexp / jnp.sum(eexp, axis=0, keepdims=True)
    p0, i0 = argmax_first(eprob, emask)
    p1, i1 = argmax_first(eprob, emask & (rowi != i0))
    den = p0 + p1
    w0 = p0 / den * g_w
    w1 = p1 / den * g_w
    e0 = i0 - ngroups
    e1 = i1 - ngroups

    oht = jnp.where((rowi == e0) | (rowi == e1), 1.0, 0.0)
    r = lax.broadcasted_iota(jnp.int32, (tm, tm), 0)
    c = lax.broadcasted_iota(jnp.int32, (tm, tm), 1)
    earlier = jnp.where(r < c, 1.0, 0.0).astype(BF16)
    er = lax.broadcasted_iota(jnp.int32, (LANES, LANES), 0)
    ec = lax.broadcasted_iota(jnp.int32, (LANES, LANES), 1)
    carry_col = jnp.sum(jnp.where(er == ec, carry_sc[...], 0.0), axis=1, keepdims=True)
    before = jnp.dot(oht.astype(BF16), earlier, preferred_element_type=F32) + carry_col
    rank0 = jnp.sum(jnp.where(rowi == e0, before, 0.0), axis=0, keepdims=True)
    rank1 = jnp.sum(jnp.where(rowi == e1, before, 0.0), axis=0, keepdims=True)
    inc_col = jnp.sum(oht, axis=1, keepdims=True)
    carry_sc[...] = carry_sc[...] + jnp.sum(jnp.where(er == ec, inc_col, 0.0), axis=0, keepdims=True)
    cnt_ref[...] = carry_sc[...]

    info = jnp.where(rowi == 0, w0, 0.0)
    info = jnp.where(rowi == 1, w1, info)
    info = jnp.where(rowi == 2, e0.astype(F32), info)
    info = jnp.where(rowi == 3, e1.astype(F32), info)
    info = jnp.where(rowi == 4, rank0, info)
    info = jnp.where(rowi == 5, rank1, info)
    rt_ref[...] = info[:ROW_TILE, :]
    ri_ref[...] = info.T


def _mix(yf, oh, gf, gh, x, g1, sc2, sh2, wuf, wuh, wo, ln_g, ln_b, wr, br, alpha, ngroups, nper, tm=512):
    B, S, D = x.shape
    W = yf.shape[2]
    tok = lambda w: pl.BlockSpec((None, tm, w), lambda b, i: (b, i, 0))
    vec = pl.BlockSpec((None, 1, D), lambda b, i: (b, 0, 0))
    full = lambda a: pl.BlockSpec(a.shape, lambda b, i: (0,) * a.ndim)
    return pl.pallas_call(
        functools.partial(_mix_kernel, alpha=alpha, ngroups=ngroups, nper=nper),
        out_shape=(jax.ShapeDtypeStruct((B, S, D), F32),
                   jax.ShapeDtypeStruct((D // WORD_LANES, B * S, LANES), F32),
                   jax.ShapeDtypeStruct((B, S, LANES), F32),
                   jax.ShapeDtypeStruct((ROW_TILE, B * S), F32),
                   jax.ShapeDtypeStruct((1, LANES), F32)),
        grid=(B, S // tm),
        in_specs=[tok(W), tok(W), tok(D), tok(D), tok(D), vec, vec, vec,
                  full(wuf), full(wuh), full(wo), full(ln_g), full(ln_b), full(wr), full(br)],
        out_specs=(tok(D),
                   pl.BlockSpec((D // WORD_LANES, tm, LANES), lambda b, i: (0, b * (S // tm) + i, 0)),
                   tok(LANES),
                   pl.BlockSpec((ROW_TILE, tm), lambda b, i: (0, b * (S // tm) + i)),
                   pl.BlockSpec((1, LANES), lambda b, i: (0, 0))),
        scratch_shapes=[pltpu.VMEM((1, LANES), F32)],
        compiler_params=_cparams(("arbitrary", "arbitrary")),
    )(yf, oh, gf, gh, x, g1, sc2, sh2, wuf, wuh, wo, ln_g, ln_b, wr, br)


def _sc_mesh():
    return plsc.VectorSubcoreMesh(core_axis_name="core", subcore_axis_name="subcore")


def _sc_pipeline(body, grid, in_specs, out_specs):
    return pltpu.emit_pipeline(body, grid=grid, in_specs=in_specs, out_specs=out_specs,
                               core_axis_name=("core", "subcore"),
                               dimension_semantics=(pltpu.PARALLEL,) * len(grid))


def _sc_scatter_rows(src, rows_a, rows_b, n_out):
    nj, t = rows_a.shape
    win = SC_WINDOW
    nc = t // win

    @pl.kernel(out_type=jax.ShapeDtypeStruct((n_out, LANES), src.dtype), mesh=_sc_mesh(), scratch_types=[])
    def scatter(x_hbm, a_hbm, b_hbm, o_hbm):
        def body(x_vmem, a_vmem, b_vmem):
            pltpu.sync_copy(x_vmem, o_hbm.at[a_vmem.at[0]])
            pltpu.sync_copy(x_vmem, o_hbm.at[b_vmem.at[0]])

        idx = pl.BlockSpec((1, win), lambda j, c: (j, c))
        _sc_pipeline(body, (nj, nc), [pl.BlockSpec((win, LANES), lambda j, c: (j * nc + c, 0)), idx, idx],
                     [])(x_hbm, a_hbm, b_hbm)

    return scatter(src, rows_a, rows_b)


def _sc_gather_rows(table, rows):
    nr, t = rows.shape
    win = SC_WINDOW
    nc = t // win

    @pl.kernel(out_type=jax.ShapeDtypeStruct((nr * t, LANES), table.dtype), mesh=_sc_mesh(), scratch_types=[])
    def gather(x_hbm, i_hbm, o_hbm):
        def body(i_vmem, o_vmem):
            pltpu.sync_copy(x_hbm.at[i_vmem.at[0]], o_vmem)

        _sc_pipeline(body, (nr, nc), [pl.BlockSpec((1, win), lambda r, c: (r, c))],
                     [pl.BlockSpec((win, LANES), lambda r, c: (r * nc + c, 0))])(i_hbm, o_hbm)

    return gather(table, rows)


W_SLOTS = 3


def _experts_kernel(tn_ref, tb_ref, run_ref, first_ref, rexp_ref, nrun_ref,
                    x_ref, wg_hbm, wu_hbm, wd_hbm, o_ref, wg_sc, wu_sc, wd_sc, sems):
    del tb_ref
    i = pl.program_id(0)
    nrows = tn_ref[i]
    run = run_ref[i]
    nruns = nrun_ref[0]

    def copies(r, slot):
        e = rexp_ref[r]
        return [pltpu.make_async_copy(hbm.at[e], buf.at[slot], sems.at[slot])
                for hbm, buf in ((wg_hbm, wg_sc), (wu_hbm, wu_sc), (wd_hbm, wd_sc))]

    def fetch(r):
        if isinstance(r, int):
            for cp in copies(r, r % W_SLOTS):
                cp.start()
            return
        for s in range(W_SLOTS):
            @pl.when(r % W_SLOTS == s)
            def _(s=s):
                for cp in copies(r, s):
                    cp.start()

    ahead = W_SLOTS - 1

    @pl.when(i == 0)
    def _():
        for r in range(ahead):
            pl.when(r < nruns)(functools.partial(fetch, r))

    for s in range(W_SLOTS):
        @pl.when((nrows > 0) & (run % W_SLOTS == s))
        def _(s=s):
            @pl.when(first_ref[i] != 0)
            def _():
                for cp in copies(run, s):
                    cp.wait()

                @pl.when(run + ahead < nruns)
                def _():
                    fetch(run + ahead)

            x = _load_chunks(x_ref)
            x = jnp.where(lax.broadcasted_iota(jnp.int32, x.shape, 0) < nrows, x, 0.0).astype(BF16)
            g = jnp.dot(x, wg_sc[s], preferred_element_type=F32)
            u = jnp.dot(x, wu_sc[s], preferred_element_type=F32)
            hid = (_silu(g) * u).astype(BF16)
            _store_chunks(o_ref, jnp.dot(hid, wd_sc[s], preferred_element_type=F32))


def _experts(tile_rows, tile_block, tile_run, tile_first, run_expert, nruns, xs, wg, wu, wd, tm):
    E, D, FF = wg.shape
    dt = D // WORD_LANES
    ntiles = tile_rows.shape[0]
    rows = pl.BlockSpec((dt, tm, LANES), lambda i, tn, tb, *_: (0, tb[i], 0))
    hbm = pl.BlockSpec(memory_space=pl.ANY)
    grid_spec = pltpu.PrefetchScalarGridSpec(
        num_scalar_prefetch=6,
        grid=(ntiles,),
        in_specs=[rows, hbm, hbm, hbm],
        out_specs=rows,
        scratch_shapes=[pltpu.VMEM((W_SLOTS, D, FF), BF16), pltpu.VMEM((W_SLOTS, D, FF), BF16),
                        pltpu.VMEM((W_SLOTS, FF, D), BF16), pltpu.SemaphoreType.DMA((W_SLOTS,))],
    )
    return pl.pallas_call(
        _experts_kernel,
        out_shape=jax.ShapeDtypeStruct((dt, ntiles * tm, LANES), F32),
        grid_spec=grid_spec,
        compiler_params=_cparams(("arbitrary",)),
    )(tile_rows, tile_block, tile_run, tile_first, run_expert, nruns, xs, wg, wu, wd)


def _combine_kernel(yg_ref, x1_ref, ri_ref, g2_ref, lg_ref, lb_ref, o_ref, *, alpha):
    ri = ri_ref[...]
    y = ri[:, 0:1] * _load_chunks(yg_ref.at[0]) + ri[:, 1:2] * _load_chunks(yg_ref.at[1])
    o_ref[...] = _layer_norm(alpha * x1_ref[...] + g2_ref[...] * y, lg_ref[...], lb_ref[...])


def _combine(yg, x1, rinfo, g2, ln_g, ln_b, alpha, tm=1024):
    B, S, D = x1.shape
    nb = S // tm
    return pl.pallas_call(
        functools.partial(_combine_kernel, alpha=alpha),
        out_shape=jax.ShapeDtypeStruct((B, S, D), F32),
        grid=(B, nb),
        in_specs=[pl.BlockSpec((2, D // WORD_LANES, tm, LANES), lambda b, i: (0, 0, b * nb + i, 0)),
                  pl.BlockSpec((None, tm, D), lambda b, i: (b, i, 0)),
                  pl.BlockSpec((None, tm, LANES), lambda b, i: (b, i, 0)),
                  pl.BlockSpec((None, 1, D), lambda b, i: (b, 0, 0)),
                  pl.BlockSpec((1, D), lambda b, i: (0, 0)),
                  pl.BlockSpec((1, D), lambda b, i: (0, 0))],
        out_specs=pl.BlockSpec((None, tm, D), lambda b, i: (b, i, 0)),
        compiler_params=_cparams(("parallel", "parallel")),
    )(yg, x1, rinfo, g2, ln_g, ln_b)


def kernel(x, c, w_ada, b_ada, w_in, b_fox_forget, hgrn_lb_logits, hgrn_norm_w, w_up_fox, w_up_hgrn, w_out,
           ln1_g, ln1_b, w_router_group, b_router_group, w_router_expert, b_router_expert,
           w_expert_gate, w_expert_up, w_expert_down, ln2_g, ln2_b):
    B, S, D = x.shape
    depth = w_ada.shape[0]
    assert depth == 1, "single-layer block"
    fox_heads = b_fox_forget.shape[1]
    fox_w = fox_heads * HEAD_DIM
    hgrn_w = hgrn_norm_w.shape[1]
    ngroups = w_router_group.shape[2]
    nexp = w_router_expert.shape[2]
    nper = nexp // ngroups
    alpha = (2 * depth) ** 0.25
    T = B * S

    ada = _ada(c, w_ada[0], b_ada[0])
    sh1, sc1, g1, sh2, sc2, g2 = [a.reshape(B, 1, D) for a in jnp.split(ada, 6, axis=-1)]

    wi = w_in[0]
    o_ff = 3 * fox_w
    w_fox = jnp.pad(wi[:, :o_ff + fox_heads], ((0, 0), (0, LANES - fox_heads))).astype(BF16)
    w_rest = wi[:, o_ff + fox_heads:].astype(BF16)
    widths = [fox_w, fox_w, fox_w, LANES, hgrn_w, hgrn_w, hgrn_w, hgrn_w, D, D]
    segs, off = [], 0
    for n, w in enumerate(widths):
        if n == 4:
            off = 0
        segs.append((off, off + w))
        off += w
    fq, fk, fv, ffp, hq, hf, hi, hg, gf, gh = _inproj(x, sc1, sh1, w_fox, w_rest, segs)

    bias_p = jnp.zeros((1, LANES), F32).at[0, :fox_heads].set(b_fox_forget[0])
    cum = _foxcum(ffp, bias_p, fox_w // LANES)
    y_fox, wg_b, wu_b, wd_b = _fox(fq, fk, fv, cum, (w_expert_gate[0], w_expert_up[0], w_expert_down[0]))

    o_h = _hgrn(hq, hf, hi, hg, hgrn_lb_logits, hgrn_norm_w[0])

    wr = jnp.zeros((D, LANES), F32).at[:, :ngroups].set(w_router_group[0]).at[:, ngroups:ngroups + nexp].set(
        w_router_expert[0])
    wr_hi = lax.bitcast_convert_type(lax.bitcast_convert_type(wr, jnp.uint32) & jnp.uint32(0xFFFF0000), F32)
    wr = jnp.concatenate([wr_hi.astype(BF16), (wr - wr_hi).astype(BF16)], axis=1)
    br = jnp.zeros((1, LANES), F32).at[0, :ngroups].set(b_router_group[0]).at[0, ngroups:ngroups + nexp].set(
        b_router_expert[0])
    x1, h2, rinfo, fields, counts = _mix(
        y_fox, o_h, gf, gh, x, g1, sc2, sh2,
        w_up_fox[0].astype(BF16), w_up_hgrn[0].astype(BF16), w_out[0].astype(BF16),
        ln1_g[0].reshape(1, D), ln1_b[0].reshape(1, D), wr, br, alpha, ngroups, nper)

    tm_e = 512
    dt = D // WORD_LANES
    ntiles = (2 * T) // tm_e + nexp
    nslots = ntiles * tm_e
    cnt = counts[0, :nexp].astype(jnp.int32)
    padded = ((cnt + tm_e - 1) // tm_e) * tm_e
    ends = jnp.cumsum(padded)
    starts = ends - padded
    eid = fields[2:4].astype(jnp.int32)
    rank = fields[4:6].astype(jnp.int32)
    first = jnp.sum(jnp.where(eid[None] == jnp.arange(nexp, dtype=jnp.int32)[:, None, None],
                              starts[:, None, None], 0), axis=0)
    pos = first + rank
    tile_start = jnp.arange(ntiles, dtype=jnp.int32) * tm_e
    tile_block = jnp.minimum(jnp.arange(ntiles, dtype=jnp.int32), ends[-1] // tm_e - 1)
    tile_expert = jnp.minimum(jnp.sum((tile_start[:, None] >= ends[None, :]).astype(jnp.int32), axis=1), nexp - 1)
    tile_rows = jnp.clip(starts[tile_expert] + cnt[tile_expert] - tile_start, 0, tm_e)
    used = jnp.cumsum((cnt > 0).astype(jnp.int32))
    nruns = used[-1:]
    run_expert = jnp.sum((used[None, :] <= jnp.arange(nexp + 2, dtype=jnp.int32)[:, None]).astype(jnp.int32), axis=1)
    run_expert = jnp.minimum(run_expert, nexp - 1)
    tile_run = used[tile_expert] - 1
    prev_expert = jnp.concatenate([jnp.full((1,), -1, jnp.int32), tile_expert[:-1]])
    tile_first = ((tile_rows > 0) & (tile_expert != prev_expert)).astype(jnp.int32)
    rows = pos[:, None, :] + (jnp.arange(dt, dtype=jnp.int32) * nslots)[None, :, None]

    xs = _sc_scatter_rows(h2.reshape(dt * T, LANES), rows[0], rows[1], dt * nslots)
    ys = _experts(tile_rows, tile_block, tile_run, tile_first, run_expert, nruns,
                  xs.reshape(dt, nslots, LANES), wg_b, wu_b, wd_b, tm_e)
    yg = _sc_gather_rows(ys.reshape(dt * nslots, LANES), rows.reshape(2 * dt, T))
    return _combine(yg.reshape(2, dt, T, LANES), x1, rinfo, g2,
                    ln2_g[0].reshape(1, D), ln2_b[0].reshape(1, D), alpha)
```

```python
import functools

import jax
import jax.numpy as jnp
from jax import lax
from jax.experimental import pallas as pl
from jax.experimental.pallas import tpu as pltpu
from jax.experimental.pallas import tpu_sc as plsc

F32 = jnp.float32
BF16 = jnp.bfloat16

LANES = 128
HEAD_DIM = 64
LN_EPS = 1e-5
RMS_EPS = 1e-6
LOG2E = 1.4426950408889634
NEG_BIG = -1e30
HCHUNK = 16
HBLOCK = 64
HGRN_SAFE_EXP = 60.0
ROW_TILE = 8
WORD_LANES = 2 * LANES
SC_WINDOW = 256
VMEM_LIMIT = 56 * 1024 * 1024


def _cparams(sem, vmem=VMEM_LIMIT):
    return pltpu.CompilerParams(dimension_semantics=sem, vmem_limit_bytes=vmem)


def _sigmoid(x):
    return 0.5 * jnp.tanh(0.5 * x) + 0.5


def _silu(x):
    return x * _sigmoid(x)


def _bf16_pieces(x, n):
    pieces = []
    for _ in range(n):
        top = pltpu.bitcast(pltpu.bitcast(x, jnp.uint32) & jnp.uint32(0xFFFF0000), F32)
        pieces.append(top.astype(BF16))
        x = x - top
    return pieces


def _exact_matrix_dot(m, x):
    r = jnp.dot(m, jnp.concatenate(_bf16_pieces(x, 3), axis=1), preferred_element_type=F32)
    return r[:, :LANES] + r[:, LANES:2 * LANES] + r[:, 2 * LANES:]


def _ada_kernel(c_ref, w_ref, b_ref, o_ref):
    c_hi, c_lo = _bf16_pieces(_silu(c_ref[...]), 2)
    w_hi, w_lo = _bf16_pieces(w_ref[...], 2)
    o_ref[...] = (jnp.dot(c_hi, w_hi, preferred_element_type=F32) + jnp.dot(c_hi, w_lo, preferred_element_type=F32)
                  + jnp.dot(c_lo, w_hi, preferred_element_type=F32)) + b_ref[...]


def _ada(c, w_ada, b_ada):
    B, D = c.shape
    N = w_ada.shape[1]
    tn = 1024
    return pl.pallas_call(
        _ada_kernel,
        out_shape=jax.ShapeDtypeStruct((B, N), F32),
        grid=(N // tn,),
        in_specs=[pl.BlockSpec((B, D), lambda j: (0, 0)),
                  pl.BlockSpec((D, tn), lambda j: (0, j)),
                  pl.BlockSpec((1, tn), lambda j: (0, j))],
        out_specs=pl.BlockSpec((B, tn), lambda j: (0, j)),
        compiler_params=_cparams(("arbitrary",)),
    )(c, w_ada, b_ada.reshape(1, N))


N_FOX_SEGS = 4
SILU_SEGS = (4, 7)
SIGMOID_SEGS = (8, 9)


def _inproj_kernel(x_ref, sc_ref, sh_ref, wf_ref, wr_ref,
                   fq_ref, fk_ref, fv_ref, ff_ref, hq_ref, hf_ref, hi_ref, hg_ref, gf_ref, gh_ref,
                   *, segs, q_scale):
    h = (x_ref[...] * (1.0 + sc_ref[...]) + sh_ref[...]).astype(BF16)
    outs = (fq_ref, fk_ref, fv_ref, ff_ref, hq_ref, hf_ref, hi_ref, hg_ref, gf_ref, gh_ref)
    for idx, (o_ref, (a, b)) in enumerate(zip(outs, segs)):
        w_ref = wf_ref if idx < N_FOX_SEGS else wr_ref
        r = jnp.dot(h, w_ref[:, a:b], preferred_element_type=F32)
        if idx == 0:
            r = r * q_scale
        elif idx in SILU_SEGS:
            r = _silu(r)
        elif idx in SIGMOID_SEGS:
            r = _sigmoid(r)
        o_ref[...] = r.astype(o_ref.dtype)


def _inproj(x, sc1, sh1, w_fox, w_rest, segs, tm=512):
    B, S, D = x.shape
    widths = [b - a for a, b in segs]
    dtypes = [BF16, BF16, BF16, F32, BF16, F32, BF16, BF16, BF16, BF16]
    out_shape = tuple(jax.ShapeDtypeStruct((B, S, w), dt) for w, dt in zip(widths, dtypes))
    out_specs = tuple(pl.BlockSpec((None, tm, w), lambda b, i: (b, i, 0)) for w in widths)
    vec = pl.BlockSpec((None, 1, D), lambda b, i: (b, 0, 0))
    return pl.pallas_call(
        functools.partial(_inproj_kernel, segs=tuple(segs), q_scale=HEAD_DIM ** -0.5 * LOG2E),
        out_shape=out_shape,
        grid=(B, S // tm),
        in_specs=[pl.BlockSpec((None, tm, D), lambda b, i: (b, i, 0)), vec, vec,
                  pl.BlockSpec(w_fox.shape, lambda b, i: (0, 0)),
                  pl.BlockSpec(w_rest.shape, lambda b, i: (0, 0))],
        out_specs=out_specs,
        compiler_params=_cparams(("parallel", "parallel")),
    )(x, sc1, sh1, w_fox, w_rest)


def _foxcum_kernel(ff_ref, b_ref, o_ref, *, blk):
    S = ff_ref.shape[0]
    r = lax.broadcasted_iota(jnp.int32, (blk, blk), 0)
    c = lax.broadcasted_iota(jnp.int32, (blk, blk), 1)
    lower = jnp.where(r >= c, 1.0, 0.0).astype(BF16)
    carry = jnp.zeros((1, LANES), F32)
    for j in range(S // blk):
        z = ff_ref[j * blk:(j + 1) * blk, :] + b_ref[...]
        lf = jnp.minimum(z, 0.0) - jnp.log(1.0 + jnp.exp(-jnp.abs(z)))
        cum = _exact_matrix_dot(lower, lf) + carry
        o_ref[j * blk:(j + 1) * blk, :] = cum * LOG2E
        carry = cum[blk - 1:blk, :]


def _foxcum(ffp, bias_p, blk=256):
    B, S, _ = ffp.shape
    return pl.pallas_call(
        functools.partial(_foxcum_kernel, blk=blk),
        out_shape=jax.ShapeDtypeStruct((B, S, LANES), F32),
        grid=(B,),
        in_specs=[pl.BlockSpec((None, S, LANES), lambda b: (b, 0, 0)),
                  pl.BlockSpec((1, LANES), lambda b: (0, 0))],
        out_specs=pl.BlockSpec((None, S, LANES), lambda b: (b, 0, 0)),
        compiler_params=_cparams(("parallel",)),
    )(ffp, bias_p)


NCUM = 3


def _fox_kernel(q_ref, k_ref, v_ref, c_ref, wg_ref, wu_ref, wd_ref, o_ref, wgb_ref, wub_ref, wdb_ref,
                ka_sc, kb_sc, va_sc, vb_sc, *, tq, tk):
    wgb_ref[...] = wg_ref[...].astype(BF16)
    wub_ref[...] = wu_ref[...].astype(BF16)
    wdb_ref[...] = wd_ref[...].astype(BF16)

    p = pl.program_id(1)
    qi = pl.program_id(2)
    S = k_ref.shape[0]

    @pl.when(qi == 0)
    def _():
        lane = lax.broadcasted_iota(jnp.int32, (S, LANES), 1)
        rr = lax.broadcasted_iota(jnp.int32, (LANES, LANES), 0)
        cc = lax.broadcasted_iota(jnp.int32, (LANES, LANES), 1)
        rest = c_ref[...]
        placed = jnp.zeros((S, LANES), F32)
        for i in range(NCUM):
            piece = rest.astype(BF16)
            rest = rest - piece.astype(F32)
            sel = ((rr == 2 * p) & (cc == HEAD_DIM + i)) | ((rr == 2 * p + 1) & (cc == i))
            placed = placed + jnp.dot(piece, jnp.where(sel, 1.0, 0.0).astype(BF16), preferred_element_type=F32)
        k2 = k_ref[...].astype(F32)
        ka_sc[...] = jnp.where(lane < HEAD_DIM, k2, -placed).astype(BF16)
        kb_sc[...] = jnp.where(lane >= HEAD_DIM, k2, -placed).astype(BF16)
        vt = v_ref[...].astype(F32).T
        row = lax.broadcasted_iota(jnp.int32, (LANES, S), 0)
        va_sc[...] = jnp.where(row < HEAD_DIM, vt, jnp.where(row == HEAD_DIM, 1.0, 0.0)).astype(BF16)
        vb_sc[...] = jnp.where(row >= HEAD_DIM, vt, jnp.where(row == 0, 1.0, 0.0)).astype(BF16)

    q2 = q_ref[...].astype(F32)
    qlane = lax.broadcasted_iota(jnp.int32, (tq, LANES), 1)
    qa = jnp.where(qlane < HEAD_DIM, q2, jnp.where(qlane < HEAD_DIM + NCUM, 1.0, 0.0)).astype(BF16)
    qb = jnp.where(qlane >= HEAD_DIM, q2, jnp.where(qlane < NCUM, 1.0, 0.0)).astype(BF16)
    nsub = tq // tk

    def block(k0, carry, diag_off):
        q0 = 0 if diag_off is None else diag_off
        out = []
        for ksc, vsc, qh, (m, acc) in ((ka_sc, va_sc, qa, carry[:2]), (kb_sc, vb_sc, qb, carry[2:])):
            st = lax.dot_general(ksc[pl.ds(k0, tk), :], qh[q0:, :], (((1,), (1,)), ((), ())),
                                 preferred_element_type=F32)
            if diag_off is not None:
                st = jnp.where(lax.broadcasted_iota(jnp.int32, st.shape, 0)
                               <= lax.broadcasted_iota(jnp.int32, st.shape, 1), st, NEG_BIG)
            m_old = m[:, q0:]
            m_new = jnp.maximum(m_old, jnp.max(st, axis=0, keepdims=True))
            pt = jnp.exp2(st - m_new).astype(BF16)
            acc_new = (jnp.exp2(m_old - m_new) * acc[:, q0:]
                       + jnp.dot(vsc[:, pl.ds(k0, tk)], pt, preferred_element_type=F32))
            if q0:
                m_new = jnp.concatenate([m[:, :q0], m_new], axis=1)
                acc_new = jnp.concatenate([acc[:, :q0], acc_new], axis=1)
            out += [m_new, acc_new]
        return tuple(out)

    def group(j, carry):
        k0 = pl.multiple_of(j * (nsub * tk), nsub * tk)
        for u in range(nsub):
            carry = block(k0 + u * tk, carry, None)
        return carry

    m0 = jnp.full((1, tq), NEG_BIG, F32)
    a0 = jnp.zeros((LANES, tq), F32)
    carry = lax.fori_loop(0, qi, group, (m0, a0, m0, a0))
    for d in range(nsub):
        carry = block(pl.multiple_of(qi * tq + d * tk, tk), carry, d * tk)
    _, aa, _, ab = carry
    row = lax.broadcasted_iota(jnp.int32, (LANES, tq), 0)
    ot = jnp.where(row < HEAD_DIM, aa * (1.0 / aa[HEAD_DIM:HEAD_DIM + 1, :]), ab * (1.0 / ab[0:1, :]))
    o_ref[...] = ot.T.astype(o_ref.dtype)


def _fox(fq, fk, fv, cum, expert_w, tq=2048, tk=512):
    B, S, W = fq.shape
    tq = min(tq, S)
    assert tq % tk == 0 and S % tq == 0
    npairs = W // LANES
    nq = S // tq
    nsteps = B * npairs * nq
    nexp = expert_w[0].shape[0]
    assert nexp % nsteps == 0
    eb = nexp // nsteps
    wspec = lambda w: pl.BlockSpec((eb,) + w.shape[1:], lambda b, p, i: ((b * npairs + p) * nq + i, 0, 0))
    return pl.pallas_call(
        functools.partial(_fox_kernel, tq=tq, tk=tk),
        out_shape=(jax.ShapeDtypeStruct((B, S, W), BF16),) + tuple(
            jax.ShapeDtypeStruct(w.shape, BF16) for w in expert_w),
        grid=(B, npairs, nq),
        in_specs=[pl.BlockSpec((None, tq, LANES), lambda b, p, i: (b, i, p)),
                  pl.BlockSpec((None, S, LANES), lambda b, p, i: (b, 0, p)),
                  pl.BlockSpec((None, S, LANES), lambda b, p, i: (b, 0, p)),
                  pl.BlockSpec((None, S, LANES), lambda b, p, i: (b, 0, 0))] + [wspec(w) for w in expert_w],
        out_specs=(pl.BlockSpec((None, tq, LANES), lambda b, p, i: (b, i, p)),) + tuple(
            wspec(w) for w in expert_w),
        scratch_shapes=[pltpu.VMEM((S, LANES), BF16), pltpu.VMEM((S, LANES), BF16),
                        pltpu.VMEM((LANES, S), BF16), pltpu.VMEM((LANES, S), BF16)],
        compiler_params=_cparams(("parallel", "parallel", "arbitrary")),
    )(fq, fk, fv, cum, *expert_w)


def _hgrn_kernel(hq_ref, hf_ref, hi_ref, hg_ref, lb_ref, nw_ref, o_ref,
                 b_sc, kk_sc, qq_sc, o_sc, w1_sc, w2_sc, w3_sc, w4_sc, w5_sc,
                 p_sc, st16_sc, dec_sc, st64_sc):
    S = hq_ref.shape[0]
    C = HCHUNK
    nchunks = S // C
    BLK = HBLOCK
    nblk = S // BLK

    lg = lb_ref[...]
    e = jnp.exp(lg - jnp.max(lg, axis=0, keepdims=True))
    lb = e[0:1, :] / jnp.sum(e, axis=0, keepdims=True)

    f = lb + (1.0 - lb) * (1.0 / (1.0 + jnp.exp(-hf_ref[...])))
    lf = jnp.log(f)
    kk_sc[...] = 1.0 - f
    qq_sc[...] = hq_ref[...].astype(F32)

    row = lax.broadcasted_iota(jnp.int32, (S, LANES), 0)
    rb = 4 * BLK
    tr = lax.broadcasted_iota(jnp.int32, (rb, rb), 0)
    tc = lax.broadcasted_iota(jnp.int32, (rb, rb), 1)
    tri = jnp.where(((tr & -BLK) == (tc & -BLK)) & (tc <= tr), 1.0, 0.0).astype(BF16)
    lf3 = jnp.concatenate(_bf16_pieces(lf, 3), axis=1)
    for j in range(S // rb):
        c3 = jnp.dot(tri, lf3[j * rb:(j + 1) * rb, :], preferred_element_type=F32)
        b_sc[j * rb:(j + 1) * rb, :] = c3[:, :LANES] + c3[:, LANES:2 * LANES] + c3[:, 2 * LANES:]
    safe = jnp.max(-b_sc[...].reshape(nblk, BLK, LANES)[:, BLK - 1, :]) <= HGRN_SAFE_EXP

    lane = lax.broadcasted_iota(jnp.int32, (C, LANES), 1)
    sr = lax.broadcasted_iota(jnp.int32, (LANES, LANES), 0)
    scn = lax.broadcasted_iota(jnp.int32, (LANES, LANES), 1)
    same_head = (sr // HEAD_DIM) == (scn // HEAD_DIM)

    @pl.when(safe)
    def _factorised():
        qh_sc, kh_sc, ke_sc, qd_sc, k2_sc = w1_sc, w2_sc, w3_sc, w4_sc, w5_sc
        SB = 2 * BLK
        nsb = S // SB
        bb = b_sc[...]
        dblk = jnp.exp(bb.reshape(nblk, BLK, LANES)[:, BLK - 1:BLK, :])
        dfull = jnp.broadcast_to(dblk, (nblk, BLK, LANES)).reshape(S, LANES)
        second = (row & BLK) != 0
        d_prev = pltpu.roll(dfull, BLK, axis=0)
        d_next = pltpu.roll(dfull, S - BLK, axis=0)
        qh = qq_sc[...] * jnp.exp(bb)
        qh_sc[...] = qh.astype(BF16)
        qd_sc[...] = (qh * jnp.where(second, d_prev, 1.0)).astype(BF16)
        kh = kk_sc[...] * jnp.exp(-bb)
        kh_sc[...] = kh.astype(BF16)
        ke = kh * dfull
        ke_sc[...] = ke.astype(BF16)
        k2_sc[...] = (ke * jnp.where(second, 1.0, d_next)).astype(BF16)
        d3 = dfull.reshape(nsb, SB, LANES)
        dec_sc[pl.ds(0, nsb), :] = d3[:, 0, :] * d3[:, BLK, :]
        unroll = min(16, nsb)
        assert nsb % unroll == 0
        tn = (((0,), (0,)), ((), ()))
        nt = (((1,), (1,)), ((), ()))

        def scan(g, st):
            for u in range(unroll):
                i = g * unroll + u
                r0 = pl.multiple_of(i * SB, SB)
                st64_sc[i] = st.astype(BF16)
                upd = lax.dot_general(hi_ref[pl.ds(r0, SB), :], k2_sc[pl.ds(r0, SB), :], tn,
                                      preferred_element_type=F32)
                st = st * dec_sc[pl.ds(i, 1), :] + jnp.where(same_head, upd, 0.0)
            return st

        lax.fori_loop(0, nsb // unroll, scan, jnp.zeros((LANES, LANES), F32))

        r = lax.broadcasted_iota(jnp.int32, (2 * SB, 2 * SB), 0)
        c = lax.broadcasted_iota(jnp.int32, (2 * SB, 2 * SB), 1)
        t = r & (SB - 1)
        visible = (((c < SB) & ((t & BLK) == (c & BLK)) & ((t & (BLK - 1)) >= (c & (BLK - 1))))
                   | ((c >= SB) & (c < SB + BLK) & (t >= BLK)))
        plane = lax.broadcasted_iota(jnp.int32, (SB, LANES), 1)
        pad = jnp.zeros((BLK, LANES), BF16)

        def readout(g, _):
            for u in range(unroll):
                i = g * unroll + u
                r0 = pl.multiple_of(i * SB, SB)
                vb = hi_ref[pl.ds(r0, SB), :]
                qh2 = qh_sc[pl.ds(r0, SB), :]
                q2 = jnp.concatenate([jnp.where(plane < HEAD_DIM, qh2, jnp.zeros_like(qh2)),
                                      jnp.where(plane >= HEAD_DIM, qh2, jnp.zeros_like(qh2))], axis=0)
                kext = jnp.concatenate([kh_sc[pl.ds(r0, SB), :], ke_sc[pl.ds(r0, BLK), :], pad], axis=0)
                vext = jnp.concatenate([vb, vb[:BLK], pad], axis=0)
                sc = lax.dot_general(q2, kext, nt, preferred_element_type=F32)
                sc = jnp.where(visible, sc, 0.0).astype(BF16)
                out = jnp.dot(sc, vext, preferred_element_type=F32)
                o_inter = lax.dot_general(qd_sc[pl.ds(r0, SB), :], st64_sc[i], nt, preferred_element_type=F32)
                o_sc[pl.ds(r0, SB), :] = jnp.where(plane < HEAD_DIM, out[:SB], out[SB:]) + o_inter
            return 0

        lax.fori_loop(0, nsb // unroll, readout, 0)

    @pl.when(jnp.logical_not(safe))
    def _direct():
        qt_sc, kt_sc, s_sc, a2_sc = w1_sc, w2_sc, w3_sc, b_sc
        bb = b_sc[...]
        cl = jnp.broadcast_to(bb.reshape(nchunks, C, LANES)[:, C - 1:C, :], (nchunks, C, LANES)).reshape(S, LANES)
        aa = bb - jnp.where((row & (BLK - 1)) >= C, pltpu.roll(cl, C, axis=0), 0.0)
        al = jnp.broadcast_to(aa.reshape(nchunks, C, LANES)[:, C - 1:C, :], (nchunks, C, LANES)).reshape(S, LANES)
        qt_sc[...] = (qq_sc[...] * jnp.exp(aa)).astype(BF16)
        kt_sc[...] = (kk_sc[...] * jnp.exp(al - aa)).astype(BF16)
        dec_sc[...] = jnp.exp(aa.reshape(nchunks, C, LANES)[:, C - 1, :])
        a2_sc[...] = aa * LOG2E
        trow = lax.broadcasted_iota(jnp.int32, (C, LANES), 0)

        def gen(c, _):
            r0 = pl.multiple_of(c * C, C)
            ac = a2_sc[pl.ds(r0, C), :]
            qc = qq_sc[pl.ds(r0, C), :]
            kc = kk_sc[pl.ds(r0, C), :]
            half = C // 2
            for s in range(C):
                if s < half:
                    dec = jnp.exp2(jnp.where(trow >= s, ac - ac[s:s + 1, :], NEG_BIG))
                    p = qc * (kc[s:s + 1, :] * dec)
                else:
                    dec = jnp.exp2(jnp.where(trow[half:] >= s, ac[half:] - ac[s:s + 1, :], NEG_BIG))
                    p = jnp.concatenate([jnp.zeros((half, LANES), F32), qc[half:] * (kc[s:s + 1, :] * dec)],
                                        axis=0)
                p_sc[pl.ds(r0, C), s * LANES:(s + 1) * LANES] = p.astype(BF16)
            return 0

        lax.fori_loop(0, nchunks, gen, 0)

        er = lax.broadcasted_iota(jnp.int32, (C * LANES, LANES), 0)
        ec = lax.broadcasted_iota(jnp.int32, (C * LANES, LANES), 1)
        emat = (ec == ((er & (LANES - 1)) // HEAD_DIM) * C + er // LANES).astype(BF16)
        rb = 256

        def red(i, _):
            r0 = pl.multiple_of(i * rb, rb)
            s_sc[pl.ds(r0, rb), :] = jnp.dot(p_sc[pl.ds(r0, rb), :], emat,
                                             preferred_element_type=F32).astype(BF16)
            return 0

        lax.fori_loop(0, S // rb, red, 0)

        unroll = 16
        assert nchunks % unroll == 0

        def scan(g, st):
            for u in range(unroll):
                c = g * unroll + u
                r0 = pl.multiple_of(c * C, C)
                st16_sc[c] = st.astype(BF16)
                upd = lax.dot_general(hi_ref[pl.ds(r0, C), :], kt_sc[pl.ds(r0, C), :],
                                      (((0,), (0,)), ((), ())), preferred_element_type=F32)
                st = st * dec_sc[pl.ds(c, 1), :] + jnp.where(same_head, upd, 0.0)
            return st

        lax.fori_loop(0, nchunks // unroll, scan, jnp.zeros((LANES, LANES), F32))

        def readout(g, _):
            for u in range(unroll):
                c = g * unroll + u
                r0 = pl.multiple_of(c * C, C)
                vc = hi_ref[pl.ds(r0, C), :]
                o_inter = lax.dot_general(qt_sc[pl.ds(r0, C), :], st16_sc[c],
                                          (((1,), (1,)), ((), ())), preferred_element_type=F32)
                v2 = jnp.concatenate([jnp.where(lane < HEAD_DIM, vc, jnp.zeros_like(vc)),
                                      jnp.where(lane >= HEAD_DIM, vc, jnp.zeros_like(vc))], axis=0)
                o_intra = jnp.dot(s_sc[pl.ds(r0, C), :][:, :2 * C], v2, preferred_element_type=F32)
                o_sc[pl.ds(r0, C), :] = o_inter + o_intra
            return 0

        lax.fori_loop(0, nchunks // unroll, readout, 0)

    o = o_sc[...]
    ones_head = jnp.where(same_head, 1.0, 0.0).astype(BF16)
    sq_hi, sq_lo = _bf16_pieces(o * o, 2)
    ms = (jnp.dot(sq_hi, ones_head, preferred_element_type=F32)
          + jnp.dot(sq_lo, ones_head, preferred_element_type=F32)) * (1.0 / HEAD_DIM)
    y = o * lax.rsqrt(ms + RMS_EPS) * nw_ref[...]
    o_ref[...] = (y * hg_ref[...].astype(F32)).astype(o_ref.dtype)


def _hgrn(hq, hf, hi, hg, lb_logits, norm_w):
    B, S, W = hq.shape
    npairs = W // LANES
    nrows = lb_logits.shape[0]
    seq = pl.BlockSpec((None, S, LANES), lambda b, p: (b, 0, p))
    return pl.pallas_call(
        _hgrn_kernel,
        out_shape=jax.ShapeDtypeStruct((B, S, W), BF16),
        grid=(B, npairs),
        in_specs=[seq, seq, seq, seq,
                  pl.BlockSpec((nrows, LANES), lambda b, p: (0, p)),
                  pl.BlockSpec((1, LANES), lambda b, p: (0, p))],
        out_specs=seq,
        scratch_shapes=[pltpu.VMEM((S, LANES), F32),
                        pltpu.VMEM((S, LANES), F32),
                        pltpu.VMEM((S, LANES), F32),
                        pltpu.VMEM((S, LANES), F32),
                        pltpu.VMEM((S, LANES), BF16),
                        pltpu.VMEM((S, LANES), BF16),
                        pltpu.VMEM((S, LANES), BF16),
                        pltpu.VMEM((S, LANES), BF16),
                        pltpu.VMEM((S, LANES), BF16),
                        pltpu.VMEM((S, HCHUNK * LANES), BF16),
                        pltpu.VMEM((S // HCHUNK, LANES, LANES), BF16),
                        pltpu.VMEM((S // HCHUNK, LANES), F32),
                        pltpu.VMEM((S // HBLOCK, LANES, LANES), BF16)],
        compiler_params=_cparams(("parallel", "parallel")),
    )(hq, hf, hi, hg, lb_logits, norm_w.reshape(1, W))


def _layer_norm(v, g, b):
    mu = jnp.mean(v, axis=-1, keepdims=True)
    d = v - mu
    var = jnp.mean(d * d, axis=-1, keepdims=True)
    return d * lax.rsqrt(var + LN_EPS) * g + b


def _bf16_bits(x):
    return (pltpu.bitcast(x, jnp.uint32) + jnp.uint32(0x8000)) & jnp.uint32(0xFFFF0000)


def _store_chunks(ref, val):
    n = ref.shape[0]
    for j in range(n):
        lo = _bf16_bits(val[:, j * LANES:(j + 1) * LANES]) >> 16
        hi = _bf16_bits(val[:, (j + n) * LANES:(j + n + 1) * LANES])
        ref[j] = pltpu.bitcast(lo | hi, F32)


def _load_chunks(ref):
    words = [pltpu.bitcast(ref[j], jnp.uint32) for j in range(ref.shape[0])]
    lo = [pltpu.bitcast(w << 16, F32) for w in words]
    hi = [pltpu.bitcast(w & jnp.uint32(0xFFFF0000), F32) for w in words]
    return jnp.concatenate(lo + hi, axis=1)


def _mix_kernel(yf_ref, oh_ref, gf_ref, gh_ref, x_ref, g1_ref, sc2_ref, sh2_ref,
                wuf_ref, wuh_ref, wo_ref, lg_ref, lbias_ref, wr_ref, br_ref,
                x1_ref, h2_ref, ri_ref, rt_ref, cnt_ref, carry_sc, *, alpha, ngroups, nper):
    first = (pl.program_id(0) == 0) & (pl.program_id(1) == 0)

    @pl.when(first)
    def _():
        carry_sc[...] = jnp.zeros_like(carry_sc)

    tm = x_ref.shape[0]
    yf = jnp.dot(yf_ref[...], wuf_ref[...], preferred_element_type=F32)
    yh = jnp.dot(oh_ref[...], wuh_ref[...], preferred_element_type=F32)
    merged = gf_ref[...].astype(F32) * yf + gh_ref[...].astype(F32) * yh
    y = jnp.dot(merged.astype(BF16), wo_ref[...], preferred_element_type=F32)
    x1 = _layer_norm(alpha * x_ref[...] + g1_ref[...] * y, lg_ref[...], lbias_ref[...])
    x1_ref[...] = x1
    h2 = x1 * (1.0 + sc2_ref[...]) + sh2_ref[...]
    _store_chunks(h2_ref, h2)

    h_hi, h_lo = _bf16_pieces(h2, 2)
    hh = jnp.dot(h_hi, wr_ref[...], preferred_element_type=F32)
    logits = (hh[:, :LANES] + hh[:, LANES:]
              + jnp.dot(h_lo, wr_ref[:, :LANES], preferred_element_type=F32)) + br_ref[...]
    lt = logits.T
    rowi = lax.broadcasted_iota(jnp.int32, (LANES, tm), 0)
    big = jnp.int32(1 << 20)

    def argmax_first(vals, mask):
        mx = jnp.max(jnp.where(mask, vals, -jnp.inf), axis=0, keepdims=True)
        idx = jnp.min(jnp.where(mask & (vals == mx), rowi, big), axis=0, keepdims=True)
        return mx, idx

    gmask = rowi < ngroups
    gmax = jnp.max(jnp.where(gmask, lt, -jnp.inf), axis=0, keepdims=True)
    gexp = jnp.where(gmask, jnp.exp(lt - gmax), 0.0)
    gprob = gexp / jnp.sum(gexp, axis=0, keepdims=True)
    g_w, g_idx = argmax_first(gprob, gmask)

    lo = ngroups + g_idx * nper
    emask = (rowi >= lo) & (rowi < lo + nper)
    emax = jnp.max(jnp.where(emask, lt, -jnp.inf), axis=0, keepdims=True)
    eexp = jnp.where(emask, jnp.exp(lt - emax), 0.0)
    eprob = eexp / jnp.sum(eexp, axis=0, keepdims=True)
    p0, i0 = argmax_first(eprob, emask)
    p1, i1 = argmax_first(eprob, emask & (rowi != i0))
    den = p0 + p1
    w0 = p0 / den * g_w
    w1 = p1 / den * g_w
    e0 = i0 - ngroups
    e1 = i1 - ngroups

    oht = jnp.where((rowi == e0) | (rowi == e1), 1.0, 0.0)
    r = lax.broadcasted_iota(jnp.int32, (tm, tm), 0)
    c = lax.broadcasted_iota(jnp.int32, (tm, tm), 1)
    earlier = jnp.where(r < c, 1.0, 0.0).astype(BF16)
    er = lax.broadcasted_iota(jnp.int32, (LANES, LANES), 0)
    ec = lax.broadcasted_iota(jnp.int32, (LANES, LANES), 1)
    carry_col = jnp.sum(jnp.where(er == ec, carry_sc[...], 0.0), axis=1, keepdims=True)
    before = jnp.dot(oht.astype(BF16), earlier, preferred_element_type=F32) + carry_col
    rank0 = jnp.sum(jnp.where(rowi == e0, before, 0.0), axis=0, keepdims=True)
    rank1 = jnp.sum(jnp.where(rowi == e1, before, 0.0), axis=0, keepdims=True)
    inc_col = jnp.sum(oht, axis=1, keepdims=True)
    carry_sc[...] = carry_sc[...] + jnp.sum(jnp.where(er == ec, inc_col, 0.0), axis=0, keepdims=True)
    cnt_ref[...] = carry_sc[...]

    info = jnp.where(rowi == 0, w0, 0.0)
    info = jnp.where(rowi == 1, w1, info)
    info = jnp.where(rowi == 2, e0.astype(F32), info)
    info = jnp.where(rowi == 3, e1.astype(F32), info)
    info = jnp.where(rowi == 4, rank0, info)
    info = jnp.where(rowi == 5, rank1, info)
    rt_ref[...] = info[:ROW_TILE, :]
    ri_ref[...] = info.T


def _mix(yf, oh, gf, gh, x, g1, sc2, sh2, wuf, wuh, wo, ln_g, ln_b, wr, br, alpha, ngroups, nper, tm=512):
    B, S, D = x.shape
    W = yf.shape[2]
    tok = lambda w: pl.BlockSpec((None, tm, w), lambda b, i: (b, i, 0))
    vec = pl.BlockSpec((None, 1, D), lambda b, i: (b, 0, 0))
    full = lambda a: pl.BlockSpec(a.shape, lambda b, i: (0,) * a.ndim)
    return pl.pallas_call(
        functools.partial(_mix_kernel, alpha=alpha, ngroups=ngroups, nper=nper),
        out_shape=(jax.ShapeDtypeStruct((B, S, D), F32),
                   jax.ShapeDtypeStruct((D // WORD_LANES, B * S, LANES), F32),
                   jax.ShapeDtypeStruct((B, S, LANES), F32),
                   jax.ShapeDtypeStruct((ROW_TILE, B * S), F32),
                   jax.ShapeDtypeStruct((1, LANES), F32)),
        grid=(B, S // tm),
        in_specs=[tok(W), tok(W), tok(D), tok(D), tok(D), vec, vec, vec,
                  full(wuf), full(wuh), full(wo), full(ln_g), full(ln_b), full(wr), full(br)],
        out_specs=(tok(D),
                   pl.BlockSpec((D // WORD_LANES, tm, LANES), lambda b, i: (0, b * (S // tm) + i, 0)),
                   tok(LANES),
                   pl.BlockSpec((ROW_TILE, tm), lambda b, i: (0, b * (S // tm) + i)),
                   pl.BlockSpec((1, LANES), lambda b, i: (0, 0))),
        scratch_shapes=[pltpu.VMEM((1, LANES), F32)],
        compiler_params=_cparams(("arbitrary", "arbitrary")),
    )(yf, oh, gf, gh, x, g1, sc2, sh2, wuf, wuh, wo, ln_g, ln_b, wr, br)


def _sc_mesh():
    return plsc.VectorSubcoreMesh(core_axis_name="core", subcore_axis_name="subcore")


def _sc_pipeline(body, grid, in_specs, out_specs):
    return pltpu.emit_pipeline(body, grid=grid, in_specs=in_specs, out_specs=out_specs,
                               core_axis_name=("core", "subcore"),
                               dimension_semantics=(pltpu.PARALLEL,) * len(grid))


def _sc_scatter_rows(src, rows_a, rows_b, n_out):
    nj, t = rows_a.shape
    win = SC_WINDOW
    nc = t // win

    @pl.kernel(out_type=jax.ShapeDtypeStruct((n_out, LANES), src.dtype), mesh=_sc_mesh(), scratch_types=[])
    def scatter(x_hbm, a_hbm, b_hbm, o_hbm):
        def body(x_vmem, a_vmem, b_vmem):
            pltpu.sync_copy(x_vmem, o_hbm.at[a_vmem.at[0]])
            pltpu.sync_copy(x_vmem, o_hbm.at[b_vmem.at[0]])

        idx = pl.BlockSpec((1, win), lambda j, c: (j, c))
        _sc_pipeline(body, (nj, nc), [pl.BlockSpec((win, LANES), lambda j, c: (j * nc + c, 0)), idx, idx],
                     [])(x_hbm, a_hbm, b_hbm)

    return scatter(src, rows_a, rows_b)


def _sc_gather_rows(table, rows):
    nr, t = rows.shape
    win = SC_WINDOW
    nc = t // win

    @pl.kernel(out_type=jax.ShapeDtypeStruct((nr * t, LANES), table.dtype), mesh=_sc_mesh(), scratch_types=[])
    def gather(x_hbm, i_hbm, o_hbm):
        def body(i_vmem, o_vmem):
            pltpu.sync_copy(x_hbm.at[i_vmem.at[0]], o_vmem)

        _sc_pipeline(body, (nr, nc), [pl.BlockSpec((1, win), lambda r, c: (r, c))],
                     [pl.BlockSpec((win, LANES), lambda r, c: (r * nc + c, 0))])(i_hbm, o_hbm)

    return gather(table, rows)


W_SLOTS = 3


def _experts_kernel(tn_ref, tb_ref, run_ref, first_ref, rexp_ref, nrun_ref,
                    x_ref, wg_hbm, wu_hbm, wd_hbm, o_ref, wg_sc, wu_sc, wd_sc, sems):
    del tb_ref
    i = pl.program_id(0)
    nrows = tn_ref[i]
    run = run_ref[i]
    nruns = nrun_ref[0]

    def copies(r, slot):
        e = rexp_ref[r]
        return [pltpu.make_async_copy(hbm.at[e], buf.at[slot], sems.at[slot])
                for hbm, buf in ((wg_hbm, wg_sc), (wu_hbm, wu_sc), (wd_hbm, wd_sc))]

    def fetch(r):
        if isinstance(r, int):
            for cp in copies(r, r % W_SLOTS):
                cp.start()
            return
        for s in range(W_SLOTS):
            @pl.when(r % W_SLOTS == s)
            def _(s=s):
                for cp in copies(r, s):
                    cp.start()

    ahead = W_SLOTS - 1

    @pl.when(i == 0)
    def _():
        for r in range(ahead):
            pl.when(r < nruns)(functools.partial(fetch, r))

    for s in range(W_SLOTS):
        @pl.when((nrows > 0) & (run % W_SLOTS == s))
        def _(s=s):
            @pl.when(first_ref[i] != 0)
            def _():
                for cp in copies(run, s):
                    cp.wait()

                @pl.when(run + ahead < nruns)
                def _():
                    fetch(run + ahead)

            x = _load_chunks(x_ref)
            x = jnp.where(lax.broadcasted_iota(jnp.int32, x.shape, 0) < nrows, x, 0.0).astype(BF16)
            g = jnp.dot(x, wg_sc[s], preferred_element_type=F32)
            u = jnp.dot(x, wu_sc[s], preferred_element_type=F32)
            hid = (_silu(g) * u).astype(BF16)
            _store_chunks(o_ref, jnp.dot(hid, wd_sc[s], preferred_element_type=F32))


def _experts(tile_rows, tile_block, tile_run, tile_first, run_expert, nruns, xs, wg, wu, wd, tm):
    E, D, FF = wg.shape
    dt = D // WORD_LANES
    ntiles = tile_rows.shape[0]
    rows = pl.BlockSpec((dt, tm, LANES), lambda i, tn, tb, *_: (0, tb[i], 0))
    hbm = pl.BlockSpec(memory_space=pl.ANY)
    grid_spec = pltpu.PrefetchScalarGridSpec(
        num_scalar_prefetch=6,
        grid=(ntiles,),
        in_specs=[rows, hbm, hbm, hbm],
        out_specs=rows,
        scratch_shapes=[pltpu.VMEM((W_SLOTS, D, FF), BF16), pltpu.VMEM((W_SLOTS, D, FF), BF16),
                        pltpu.VMEM((W_SLOTS, FF, D), BF16), pltpu.SemaphoreType.DMA((W_SLOTS,))],
    )
    return pl.pallas_call(
        _experts_kernel,
        out_shape=jax.ShapeDtypeStruct((dt, ntiles * tm, LANES), F32),
        grid_spec=grid_spec,
        compiler_params=_cparams(("arbitrary",)),
    )(tile_rows, tile_block, tile_run, tile_first, run_expert, nruns, xs, wg, wu, wd)


def _combine_kernel(yg_ref, x1_ref, ri_ref, g2_ref, lg_ref, lb_ref, o_ref, *, alpha):
    ri = ri_ref[...]
    y = ri[:, 0:1] * _load_chunks(yg_ref.at[0]) + ri[:, 1:2] * _load_chunks(yg_ref.at[1])
    o_ref[...] = _layer_norm(alpha * x1_ref[...] + g2_ref[...] * y, lg_ref[...], lb_ref[...])


def _combine(yg, x1, rinfo, g2, ln_g, ln_b, alpha, tm=1024):
    B, S, D = x1.shape
    nb = S // tm
    return pl.pallas_call(
        functools.partial(_combine_kernel, alpha=alpha),
        out_shape=jax.ShapeDtypeStruct((B, S, D), F32),
        grid=(B, nb),
        in_specs=[pl.BlockSpec((2, D // WORD_LANES, tm, LANES), lambda b, i: (0, 0, b * nb + i, 0)),
                  pl.BlockSpec((None, tm, D), lambda b, i: (b, i, 0)),
                  pl.BlockSpec((None, tm, LANES), lambda b, i: (b, i, 0)),
                  pl.BlockSpec((None, 1, D), lambda b, i: (b, 0, 0)),
                  pl.BlockSpec((1, D), lambda b, i: (0, 0)),
                  pl.BlockSpec((1, D), lambda b, i: (0, 0))],
        out_specs=pl.BlockSpec((None, tm, D), lambda b, i: (b, i, 0)),
        compiler_params=_cparams(("parallel", "parallel")),
    )(yg, x1, rinfo, g2, ln_g, ln_b)


def kernel(x, c, w_ada, b_ada, w_in, b_fox_forget, hgrn_lb_logits, hgrn_norm_w, w_up_fox, w_up_hgrn, w_out,
           ln1_g, ln1_b, w_router_group, b_router_group, w_router_expert, b_router_expert,
           w_expert_gate, w_expert_up, w_expert_down, ln2_g, ln2_b):
    B, S, D = x.shape
    depth = w_ada.shape[0]
    assert depth == 1, "single-layer block"
    fox_heads = b_fox_forget.shape[1]
    fox_w = fox_heads * HEAD_DIM
    hgrn_w = hgrn_norm_w.shape[1]
    ngroups = w_router_group.shape[2]
    nexp = w_router_expert.shape[2]
    nper = nexp // ngroups
    alpha = (2 * depth) ** 0.25
    T = B * S

    ada = _ada(c, w_ada[0], b_ada[0])
    sh1, sc1, g1, sh2, sc2, g2 = [a.reshape(B, 1, D) for a in jnp.split(ada, 6, axis=-1)]

    wi = w_in[0]
    o_ff = 3 * fox_w
    w_fox = jnp.pad(wi[:, :o_ff + fox_heads], ((0, 0), (0, LANES - fox_heads))).astype(BF16)
    w_rest = wi[:, o_ff + fox_heads:].astype(BF16)
    widths = [fox_w, fox_w, fox_w, LANES, hgrn_w, hgrn_w, hgrn_w, hgrn_w, D, D]
    segs, off = [], 0
    for n, w in enumerate(widths):
        if n == 4:
            off = 0
        segs.append((off, off + w))
        off += w
    fq, fk, fv, ffp, hq, hf, hi, hg, gf, gh = _inproj(x, sc1, sh1, w_fox, w_rest, segs)

    bias_p = jnp.zeros((1, LANES), F32).at[0, :fox_heads].set(b_fox_forget[0])
    cum = _foxcum(ffp, bias_p)
    y_fox, wg_b, wu_b, wd_b = _fox(fq, fk, fv, cum, (w_expert_gate[0], w_expert_up[0], w_expert_down[0]))

    o_h = _hgrn(hq, hf, hi, hg, hgrn_lb_logits, hgrn_norm_w[0])

    wr = jnp.zeros((D, LANES), F32).at[:, :ngroups].set(w_router_group[0]).at[:, ngroups:ngroups + nexp].set(
        w_router_expert[0])
    wr_hi = lax.bitcast_convert_type(lax.bitcast_convert_type(wr, jnp.uint32) & jnp.uint32(0xFFFF0000), F32)
    wr = jnp.concatenate([wr_hi.astype(BF16), (wr - wr_hi).astype(BF16)], axis=1)
    br = jnp.zeros((1, LANES), F32).at[0, :ngroups].set(b_router_group[0]).at[0, ngroups:ngroups + nexp].set(
        b_router_expert[0])
    x1, h2, rinfo, fields, counts = _mix(
        y_fox, o_h, gf, gh, x, g1, sc2, sh2,
        w_up_fox[0].astype(BF16), w_up_hgrn[0].astype(BF16), w_out[0].astype(BF16),
        ln1_g[0].reshape(1, D), ln1_b[0].reshape(1, D), wr, br, alpha, ngroups, nper)

    tm_e = 512
    dt = D // WORD_LANES
    ntiles = (2 * T) // tm_e + nexp
    nslots = ntiles * tm_e
    cnt = counts[0, :nexp].astype(jnp.int32)
    padded = ((cnt + tm_e - 1) // tm_e) * tm_e
    ends = jnp.cumsum(padded)
    starts = ends - padded
    eid = fields[2:4].astype(jnp.int32)
    rank = fields[4:6].astype(jnp.int32)
    first = jnp.sum(jnp.where(eid[None] == jnp.arange(nexp, dtype=jnp.int32)[:, None, None],
                              starts[:, None, None], 0), axis=0)
    pos = first + rank
    tile_start = jnp.arange(ntiles, dtype=jnp.int32) * tm_e
    tile_block = jnp.minimum(jnp.arange(ntiles, dtype=jnp.int32), ends[-1] // tm_e - 1)
    tile_expert = jnp.minimum(jnp.sum((tile_start[:, None] >= ends[None, :]).astype(jnp.int32), axis=1), nexp - 1)
    tile_rows = jnp.clip(starts[tile_expert] + cnt[tile_expert] - tile_start, 0, tm_e)
    used = jnp.cumsum((cnt > 0).astype(jnp.int32))
    nruns = used[-1:]
    run_expert = jnp.sum((used[None, :] <= jnp.arange(nexp + 2, dtype=jnp.int32)[:, None]).astype(jnp.int32), axis=1)
    run_expert = jnp.minimum(run_expert, nexp - 1)
    tile_run = used[tile_expert] - 1
    prev_expert = jnp.concatenate([jnp.full((1,), -1, jnp.int32), tile_expert[:-1]])
    tile_first = ((tile_rows > 0) & (tile_expert != prev_expert)).astype(jnp.int32)
    rows = pos[:, None, :] + (jnp.arange(dt, dtype=jnp.int32) * nslots)[None, :, None]

    xs = _sc_scatter_rows(h2.reshape(dt * T, LANES), rows[0], rows[1], dt * nslots)
    ys = _experts(tile_rows, tile_block, tile_run, tile_first, run_expert, nruns,
                  xs.reshape(dt, nslots, LANES), wg_b, wu_b, wd_b, tm_e)
    yg = _sc_gather_rows(ys.reshape(dt * nslots, LANES), rows.reshape(2 * dt, T))
    return _combine(yg.reshape(2, dt, T, LANES), x1, rinfo, g2,
                    ln2_g[0].reshape(1, D), ln2_b[0].reshape(1, D), alpha)
```

```python
import functools

import jax
import jax.numpy as jnp
from jax import lax
from jax.experimental import pallas as pl
from jax.experimental.pallas import tpu as pltpu
from jax.experimental.pallas import tpu_sc as plsc

F32 = jnp.float32
BF16 = jnp.bfloat16

LANES = 128
HEAD_DIM = 64
LN_EPS = 1e-5
RMS_EPS = 1e-6
LOG2E = 1.4426950408889634
NEG_BIG = -1e30
HCHUNK = 16
HBLOCK = 64
HGRN_SAFE_EXP = 60.0
ROW_TILE = 8
WORD_LANES = 2 * LANES
SC_WINDOW = 256
VMEM_LIMIT = 56 * 1024 * 1024


def _cparams(sem, vmem=VMEM_LIMIT):
    return pltpu.CompilerParams(dimension_semantics=sem, vmem_limit_bytes=vmem)


def _sigmoid(x):
    return 0.5 * jnp.tanh(0.5 * x) + 0.5


def _silu(x):
    return x * _sigmoid(x)


def _bf16_pieces(x, n):
    pieces = []
    for _ in range(n):
        top = pltpu.bitcast(pltpu.bitcast(x, jnp.uint32) & jnp.uint32(0xFFFF0000), F32)
        pieces.append(top.astype(BF16))
        x = x - top
    return pieces


def _exact_matrix_dot(m, x):
    r = jnp.dot(m, jnp.concatenate(_bf16_pieces(x, 3), axis=1), preferred_element_type=F32)
    return r[:, :LANES] + r[:, LANES:2 * LANES] + r[:, 2 * LANES:]


def _ada_kernel(c_ref, w_ref, b_ref, o_ref):
    c_hi, c_lo = _bf16_pieces(_silu(c_ref[...]), 2)
    w_hi, w_lo = _bf16_pieces(w_ref[...], 2)
    o_ref[...] = (jnp.dot(c_hi, w_hi, preferred_element_type=F32) + jnp.dot(c_hi, w_lo, preferred_element_type=F32)
                  + jnp.dot(c_lo, w_hi, preferred_element_type=F32)) + b_ref[...]


def _ada(c, w_ada, b_ada):
    B, D = c.shape
    N = w_ada.shape[1]
    tn = 1024
    return pl.pallas_call(
        _ada_kernel,
        out_shape=jax.ShapeDtypeStruct((B, N), F32),
        grid=(N // tn,),
        in_specs=[pl.BlockSpec((B, D), lambda j: (0, 0)),
                  pl.BlockSpec((D, tn), lambda j: (0, j)),
                  pl.BlockSpec((1, tn), lambda j: (0, j))],
        out_specs=pl.BlockSpec((B, tn), lambda j: (0, j)),
        compiler_params=_cparams(("arbitrary",)),
    )(c, w_ada, b_ada.reshape(1, N))


N_FOX_SEGS = 4
SILU_SEGS = (4, 7)
SIGMOID_SEGS = (8, 9)


def _inproj_kernel(x_ref, sc_ref, sh_ref, wf_ref, wr_ref,
                   fq_ref, fk_ref, fv_ref, ff_ref, hq_ref, hf_ref, hi_ref, hg_ref, gf_ref, gh_ref,
                   *, segs, q_scale):
    h = (x_ref[...] * (1.0 + sc_ref[...]) + sh_ref[...]).astype(BF16)
    outs = (fq_ref, fk_ref, fv_ref, ff_ref, hq_ref, hf_ref, hi_ref, hg_ref, gf_ref, gh_ref)
    for idx, (o_ref, (a, b)) in enumerate(zip(outs, segs)):
        w_ref = wf_ref if idx < N_FOX_SEGS else wr_ref
        r = jnp.dot(h, w_ref[:, a:b], preferred_element_type=F32)
        if idx == 0:
            r = r * q_scale
        elif idx in SILU_SEGS:
            r = _silu(r)
        elif idx in SIGMOID_SEGS:
            r = _sigmoid(r)
        o_ref[...] = r.astype(o_ref.dtype)


def _inproj(x, sc1, sh1, w_fox, w_rest, segs, tm=512):
    B, S, D = x.shape
    widths = [b - a for a, b in segs]
    dtypes = [BF16, BF16, BF16, F32, BF16, F32, BF16, BF16, BF16, BF16]
    out_shape = tuple(jax.ShapeDtypeStruct((B, S, w), dt) for w, dt in zip(widths, dtypes))
    out_specs = tuple(pl.BlockSpec((None, tm, w), lambda b, i: (b, i, 0)) for w in widths)
    vec = pl.BlockSpec((None, 1, D), lambda b, i: (b, 0, 0))
    return pl.pallas_call(
        functools.partial(_inproj_kernel, segs=tuple(segs), q_scale=HEAD_DIM ** -0.5 * LOG2E),
        out_shape=out_shape,
        grid=(B, S // tm),
        in_specs=[pl.BlockSpec((None, tm, D), lambda b, i: (b, i, 0)), vec, vec,
                  pl.BlockSpec(w_fox.shape, lambda b, i: (0, 0)),
                  pl.BlockSpec(w_rest.shape, lambda b, i: (0, 0))],
        out_specs=out_specs,
        compiler_params=_cparams(("parallel", "parallel")),
    )(x, sc1, sh1, w_fox, w_rest)


def _foxcum_kernel(ff_ref, b_ref, o_ref, *, blk):
    S = ff_ref.shape[0]
    r = lax.broadcasted_iota(jnp.int32, (blk, blk), 0)
    c = lax.broadcasted_iota(jnp.int32, (blk, blk), 1)
    lower = jnp.where(r >= c, 1.0, 0.0).astype(BF16)
    carry = jnp.zeros((1, LANES), F32)
    for j in range(S // blk):
        z = ff_ref[j * blk:(j + 1) * blk, :] + b_ref[...]
        lf = jnp.minimum(z, 0.0) - jnp.log(1.0 + jnp.exp(-jnp.abs(z)))
        cum = _exact_matrix_dot(lower, lf) + carry
        o_ref[j * blk:(j + 1) * blk, :] = cum * LOG2E
        carry = cum[blk - 1:blk, :]


def _foxcum(ffp, bias_p, blk=256):
    B, S, _ = ffp.shape
    return pl.pallas_call(
        functools.partial(_foxcum_kernel, blk=blk),
        out_shape=jax.ShapeDtypeStruct((B, S, LANES), F32),
        grid=(B,),
        in_specs=[pl.BlockSpec((None, S, LANES), lambda b: (b, 0, 0)),
                  pl.BlockSpec((1, LANES), lambda b: (0, 0))],
        out_specs=pl.BlockSpec((None, S, LANES), lambda b: (b, 0, 0)),
        compiler_params=_cparams(("parallel",)),
    )(ffp, bias_p)


NCUM = 3


def _fox_kernel(q_ref, k_ref, v_ref, c_ref, wg_ref, wu_ref, wd_ref, o_ref, wgb_ref, wub_ref, wdb_ref,
                ka_sc, kb_sc, va_sc, vb_sc, *, tq, tk):
    wgb_ref[...] = wg_ref[...].astype(BF16)
    wub_ref[...] = wu_ref[...].astype(BF16)
    wdb_ref[...] = wd_ref[...].astype(BF16)

    p = pl.program_id(1)
    qi = pl.program_id(2)
    S = k_ref.shape[0]

    @pl.when(qi == 0)
    def _():
        lane = lax.broadcasted_iota(jnp.int32, (S, LANES), 1)
        rr = lax.broadcasted_iota(jnp.int32, (LANES, LANES), 0)
        cc = lax.broadcasted_iota(jnp.int32, (LANES, LANES), 1)
        rest = c_ref[...]
        placed = jnp.zeros((S, LANES), F32)
        for i in range(NCUM):
            piece = rest.astype(BF16)
            rest = rest - piece.astype(F32)
            sel = ((rr == 2 * p) & (cc == HEAD_DIM + i)) | ((rr == 2 * p + 1) & (cc == i))
            placed = placed + jnp.dot(piece, jnp.where(sel, 1.0, 0.0).astype(BF16), preferred_element_type=F32)
        k2 = k_ref[...].astype(F32)
        ka_sc[...] = jnp.where(lane < HEAD_DIM, k2, -placed).astype(BF16)
        kb_sc[...] = jnp.where(lane >= HEAD_DIM, k2, -placed).astype(BF16)
        vt = v_ref[...].astype(F32).T
        row = lax.broadcasted_iota(jnp.int32, (LANES, S), 0)
        va_sc[...] = jnp.where(row < HEAD_DIM, vt, jnp.where(row == HEAD_DIM, 1.0, 0.0)).astype(BF16)
        vb_sc[...] = jnp.where(row >= HEAD_DIM, vt, jnp.where(row == 0, 1.0, 0.0)).astype(BF16)

    q2 = q_ref[...].astype(F32)
    qlane = lax.broadcasted_iota(jnp.int32, (tq, LANES), 1)
    qa = jnp.where(qlane < HEAD_DIM, q2, jnp.where(qlane < HEAD_DIM + NCUM, 1.0, 0.0)).astype(BF16)
    qb = jnp.where(qlane >= HEAD_DIM, q2, jnp.where(qlane < NCUM, 1.0, 0.0)).astype(BF16)
    nsub = tq // tk

    def block(k0, carry, diag_off):
        q0 = 0 if diag_off is None else diag_off
        out = []
        for ksc, vsc, qh, (m, acc) in ((ka_sc, va_sc, qa, carry[:2]), (kb_sc, vb_sc, qb, carry[2:])):
            st = lax.dot_general(ksc[pl.ds(k0, tk), :], qh[q0:, :], (((1,), (1,)), ((), ())),
                                 preferred_element_type=F32)
            if diag_off is not None:
                st = jnp.where(lax.broadcasted_iota(jnp.int32, st.shape, 0)
                               <= lax.broadcasted_iota(jnp.int32, st.shape, 1), st, NEG_BIG)
            m_old = m[:, q0:]
            m_new = jnp.maximum(m_old, jnp.max(st, axis=0, keepdims=True))
            pt = jnp.exp2(st - m_new).astype(BF16)
            acc_new = (jnp.exp2(m_old - m_new) * acc[:, q0:]
                       + jnp.dot(vsc[:, pl.ds(k0, tk)], pt, preferred_element_type=F32))
            if q0:
                m_new = jnp.concatenate([m[:, :q0], m_new], axis=1)
                acc_new = jnp.concatenate([acc[:, :q0], acc_new], axis=1)
            out += [m_new, acc_new]
        return tuple(out)

    def group(j, carry):
        k0 = pl.multiple_of(j * (nsub * tk), nsub * tk)
        for u in range(nsub):
            carry = block(k0 + u * tk, carry, None)
        return carry

    m0 = jnp.full((1, tq), NEG_BIG, F32)
    a0 = jnp.zeros((LANES, tq), F32)
    carry = lax.fori_loop(0, qi, group, (m0, a0, m0, a0))
    for d in range(nsub):
        carry = block(pl.multiple_of(qi * tq + d * tk, tk), carry, d * tk)
    _, aa, _, ab = carry
    row = lax.broadcasted_iota(jnp.int32, (LANES, tq), 0)
    ot = jnp.where(row < HEAD_DIM, aa * (1.0 / aa[HEAD_DIM:HEAD_DIM + 1, :]), ab * (1.0 / ab[0:1, :]))
    o_ref[...] = ot.T.astype(o_ref.dtype)


def _fox(fq, fk, fv, cum, expert_w, tq=2048, tk=512):
    B, S, W = fq.shape
    tq = min(tq, S)
    assert tq % tk == 0 and S % tq == 0
    npairs = W // LANES
    nq = S // tq
    nsteps = B * npairs * nq
    nexp = expert_w[0].shape[0]
    assert nexp % nsteps == 0
    eb = nexp // nsteps
    wspec = lambda w: pl.BlockSpec((eb,) + w.shape[1:], lambda b, p, i: ((b * npairs + p) * nq + i, 0, 0))
    return pl.pallas_call(
        functools.partial(_fox_kernel, tq=tq, tk=tk),
        out_shape=(jax.ShapeDtypeStruct((B, S, W), BF16),) + tuple(
            jax.ShapeDtypeStruct(w.shape, BF16) for w in expert_w),
        grid=(B, npairs, nq),
        in_specs=[pl.BlockSpec((None, tq, LANES), lambda b, p, i: (b, i, p)),
                  pl.BlockSpec((None, S, LANES), lambda b, p, i: (b, 0, p)),
                  pl.BlockSpec((None, S, LANES), lambda b, p, i: (b, 0, p)),
                  pl.BlockSpec((None, S, LANES), lambda b, p, i: (b, 0, 0))] + [wspec(w) for w in expert_w],
        out_specs=(pl.BlockSpec((None, tq, LANES), lambda b, p, i: (b, i, p)),) + tuple(
            wspec(w) for w in expert_w),
        scratch_shapes=[pltpu.VMEM((S, LANES), BF16), pltpu.VMEM((S, LANES), BF16),
                        pltpu.VMEM((LANES, S), BF16), pltpu.VMEM((LANES, S), BF16)],
        compiler_params=_cparams(("parallel", "parallel", "arbitrary")),
    )(fq, fk, fv, cum, *expert_w)


def _hgrn_kernel(hq_ref, hf_ref, hi_ref, hg_ref, lb_ref, nw_ref, o_ref,
                 b_sc, kk_sc, qq_sc, o_sc, w1_sc, w2_sc, w3_sc, w4_sc, w5_sc,
                 p_sc, st16_sc, dec_sc, st64_sc):
    S = hq_ref.shape[0]
    C = HCHUNK
    nchunks = S // C
    BLK = HBLOCK
    nblk = S // BLK

    lg = lb_ref[...]
    e = jnp.exp(lg - jnp.max(lg, axis=0, keepdims=True))
    lb = e[0:1, :] / jnp.sum(e, axis=0, keepdims=True)

    f = lb + (1.0 - lb) * (1.0 / (1.0 + jnp.exp(-hf_ref[...])))
    lf = jnp.log(f)
    kk_sc[...] = 1.0 - f
    qq_sc[...] = hq_ref[...].astype(F32)

    row = lax.broadcasted_iota(jnp.int32, (S, LANES), 0)
    rb = 4 * BLK
    tr = lax.broadcasted_iota(jnp.int32, (rb, rb), 0)
    tc = lax.broadcasted_iota(jnp.int32, (rb, rb), 1)
    tri = jnp.where(((tr & -BLK) == (tc & -BLK)) & (tc <= tr), 1.0, 0.0).astype(BF16)
    lf3 = jnp.concatenate(_bf16_pieces(lf, 3), axis=1)
    for j in range(S // rb):
        c3 = jnp.dot(tri, lf3[j * rb:(j + 1) * rb, :], preferred_element_type=F32)
        b_sc[j * rb:(j + 1) * rb, :] = c3[:, :LANES] + c3[:, LANES:2 * LANES] + c3[:, 2 * LANES:]
    safe = jnp.max(-b_sc[...].reshape(nblk, BLK, LANES)[:, BLK - 1, :]) <= HGRN_SAFE_EXP

    lane = lax.broadcasted_iota(jnp.int32, (C, LANES), 1)
    sr = lax.broadcasted_iota(jnp.int32, (LANES, LANES), 0)
    scn = lax.broadcasted_iota(jnp.int32, (LANES, LANES), 1)
    same_head = (sr // HEAD_DIM) == (scn // HEAD_DIM)

    @pl.when(safe)
    def _factorised():
        qh_sc, kh_sc, ke_sc, qd_sc, k2_sc = w1_sc, w2_sc, w3_sc, w4_sc, w5_sc
        SB = 2 * BLK
        nsb = S // SB
        bb = b_sc[...]
        dblk = jnp.exp(bb.reshape(nblk, BLK, LANES)[:, BLK - 1:BLK, :])
        dfull = jnp.broadcast_to(dblk, (nblk, BLK, LANES)).reshape(S, LANES)
        second = (row & BLK) != 0
        d_prev = pltpu.roll(dfull, BLK, axis=0)
        d_next = pltpu.roll(dfull, S - BLK, axis=0)
        qh = qq_sc[...] * jnp.exp(bb)
        qh_sc[...] = qh.astype(BF16)
        qd_sc[...] = (qh * jnp.where(second, d_prev, 1.0)).astype(BF16)
        kh = kk_sc[...] * jnp.exp(-bb)
        kh_sc[...] = kh.astype(BF16)
        ke = kh * dfull
        ke_sc[...] = ke.astype(BF16)
        k2_sc[...] = (ke * jnp.where(second, 1.0, d_next)).astype(BF16)
        d3 = dfull.reshape(nsb, SB, LANES)
        dec_sc[pl.ds(0, nsb), :] = d3[:, 0, :] * d3[:, BLK, :]
        unroll = min(16, nsb)
        assert nsb % unroll == 0
        tn = (((0,), (0,)), ((), ()))
        nt = (((1,), (1,)), ((), ()))

        def scan(g, st):
            for u in range(unroll):
                i = g * unroll + u
                r0 = pl.multiple_of(i * SB, SB)
                st64_sc[i] = st.astype(BF16)
                upd = lax.dot_general(hi_ref[pl.ds(r0, SB), :], k2_sc[pl.ds(r0, SB), :], tn,
                                      preferred_element_type=F32)
                st = st * dec_sc[pl.ds(i, 1), :] + jnp.where(same_head, upd, 0.0)
            return st

        lax.fori_loop(0, nsb // unroll, scan, jnp.zeros((LANES, LANES), F32))

        r = lax.broadcasted_iota(jnp.int32, (2 * SB, 2 * SB), 0)
        c = lax.broadcasted_iota(jnp.int32, (2 * SB, 2 * SB), 1)
        t = r & (SB - 1)
        visible = (((c < SB) & ((t & BLK) == (c & BLK)) & ((t & (BLK - 1)) >= (c & (BLK - 1))))
                   | ((c >= SB) & (c < SB + BLK) & (t >= BLK)))
        plane = lax.broadcasted_iota(jnp.int32, (SB, LANES), 1)
        pad = jnp.zeros((BLK, LANES), BF16)

        def readout(g, _):
            for u in range(unroll):
                i = g * unroll + u
                r0 = pl.multiple_of(i * SB, SB)
                vb = hi_ref[pl.ds(r0, SB), :]
                qh2 = qh_sc[pl.ds(r0, SB), :]
                q2 = jnp.concatenate([jnp.where(plane < HEAD_DIM, qh2, jnp.zeros_like(qh2)),
                                      jnp.where(plane >= HEAD_DIM, qh2, jnp.zeros_like(qh2))], axis=0)
                kext = jnp.concatenate([kh_sc[pl.ds(r0, SB), :], ke_sc[pl.ds(r0, BLK), :], pad], axis=0)
                vext = jnp.concatenate([vb, vb[:BLK], pad], axis=0)
                sc = lax.dot_general(q2, kext, nt, preferred_element_type=F32)
                sc = jnp.where(visible, sc, 0.0).astype(BF16)
                out = jnp.dot(sc, vext, preferred_element_type=F32)
                o_inter = lax.dot_general(qd_sc[pl.ds(r0, SB), :], st64_sc[i], nt, preferred_element_type=F32)
                o_sc[pl.ds(r0, SB), :] = jnp.where(plane < HEAD_DIM, out[:SB], out[SB:]) + o_inter
            return 0

        lax.fori_loop(0, nsb // unroll, readout, 0)

    @pl.when(jnp.logical_not(safe))
    def _direct():
        qt_sc, kt_sc, s_sc, a2_sc = w1_sc, w2_sc, w3_sc, b_sc
        bb = b_sc[...]
        cl = jnp.broadcast_to(bb.reshape(nchunks, C, LANES)[:, C - 1:C, :], (nchunks, C, LANES)).reshape(S, LANES)
        aa = bb - jnp.where((row & (BLK - 1)) >= C, pltpu.roll(cl, C, axis=0), 0.0)
        al = jnp.broadcast_to(aa.reshape(nchunks, C, LANES)[:, C - 1:C, :], (nchunks, C, LANES)).reshape(S, LANES)
        qt_sc[...] = (qq_sc[...] * jnp.exp(aa)).astype(BF16)
        kt_sc[...] = (kk_sc[...] * jnp.exp(al - aa)).astype(BF16)
        dec_sc[...] = jnp.exp(aa.reshape(nchunks, C, LANES)[:, C - 1, :])
        a2_sc[...] = aa * LOG2E
        trow = lax.broadcasted_iota(jnp.int32, (C, LANES), 0)

        def gen(c, _):
            r0 = pl.multiple_of(c * C, C)
            ac = a2_sc[pl.ds(r0, C), :]
            qc = qq_sc[pl.ds(r0, C), :]
            kc = kk_sc[pl.ds(r0, C), :]
            half = C // 2
            for s in range(C):
                if s < half:
                    dec = jnp.exp2(jnp.where(trow >= s, ac - ac[s:s + 1, :], NEG_BIG))
                    p = qc * (kc[s:s + 1, :] * dec)
                else:
                    dec = jnp.exp2(jnp.where(trow[half:] >= s, ac[half:] - ac[s:s + 1, :], NEG_BIG))
                    p = jnp.concatenate([jnp.zeros((half, LANES), F32), qc[half:] * (kc[s:s + 1, :] * dec)],
                                        axis=0)
                p_sc[pl.ds(r0, C), s * LANES:(s + 1) * LANES] = p.astype(BF16)
            return 0

        lax.fori_loop(0, nchunks, gen, 0)

        er = lax.broadcasted_iota(jnp.int32, (C * LANES, LANES), 0)
        ec = lax.broadcasted_iota(jnp.int32, (C * LANES, LANES), 1)
        emat = (ec == ((er & (LANES - 1)) // HEAD_DIM) * C + er // LANES).astype(BF16)
        rb = 256

        def red(i, _):
            r0 = pl.multiple_of(i * rb, rb)
            s_sc[pl.ds(r0, rb), :] = jnp.dot(p_sc[pl.ds(r0, rb), :], emat,
                                             preferred_element_type=F32).astype(BF16)
            return 0

        lax.fori_loop(0, S // rb, red, 0)

        unroll = 16
        assert nchunks % unroll == 0

        def scan(g, st):
            for u in range(unroll):
                c = g * unroll + u
                r0 = pl.multiple_of(c * C, C)
                st16_sc[c] = st.astype(BF16)
                upd = lax.dot_general(hi_ref[pl.ds(r0, C), :], kt_sc[pl.ds(r0, C), :],
                                      (((0,), (0,)), ((), ())), preferred_element_type=F32)
                st = st * dec_sc[pl.ds(c, 1), :] + jnp.where(same_head, upd, 0.0)
            return st

        lax.fori_loop(0, nchunks // unroll, scan, jnp.zeros((LANES, LANES), F32))

        def readout(g, _):
            for u in range(unroll):
                c = g * unroll + u
                r0 = pl.multiple_of(c * C, C)
                vc = hi_ref[pl.ds(r0, C), :]
                o_inter = lax.dot_general(qt_sc[pl.ds(r0, C), :], st16_sc[c],
                                          (((1,), (1,)), ((), ())), preferred_element_type=F32)
                v2 = jnp.concatenate([jnp.where(lane < HEAD_DIM, vc, jnp.zeros_like(vc)),
                                      jnp.where(lane >= HEAD_DIM, vc, jnp.zeros_like(vc))], axis=0)
                o_intra = jnp.dot(s_sc[pl.ds(r0, C), :][:, :2 * C], v2, preferred_element_type=F32)
                o_sc[pl.ds(r0, C), :] = o_inter + o_intra
            return 0

        lax.fori_loop(0, nchunks // unroll, readout, 0)

    o = o_sc[...]
    ones_head = jnp.where(same_head, 1.0, 0.0).astype(BF16)
    sq_hi, sq_lo = _bf16_pieces(o * o, 2)
    ms = (jnp.dot(sq_hi, ones_head, preferred_element_type=F32)
          + jnp.dot(sq_lo, ones_head, preferred_element_type=F32)) * (1.0 / HEAD_DIM)
    y = o * lax.rsqrt(ms + RMS_EPS) * nw_ref[...]
    o_ref[...] = (y * hg_ref[...].astype(F32)).astype(o_ref.dtype)


def _hgrn(hq, hf, hi, hg, lb_logits, norm_w):
    B, S, W = hq.shape
    npairs = W // LANES
    nrows = lb_logits.shape[0]
    seq = pl.BlockSpec((None, S, LANES), lambda b, p: (b, 0, p))
    return pl.pallas_call(
        _hgrn_kernel,
        out_shape=jax.ShapeDtypeStruct((B, S, W), BF16),
        grid=(B, npairs),
        in_specs=[seq, seq, seq, seq,
                  pl.BlockSpec((nrows, LANES), lambda b, p: (0, p)),
                  pl.BlockSpec((1, LANES), lambda b, p: (0, p))],
        out_specs=seq,
        scratch_shapes=[pltpu.VMEM((S, LANES), F32),
                        pltpu.VMEM((S, LANES), F32),
                        pltpu.VMEM((S, LANES), F32),
                        pltpu.VMEM((S, LANES), F32),
                        pltpu.VMEM((S, LANES), BF16),
                        pltpu.VMEM((S, LANES), BF16),
                        pltpu.VMEM((S, LANES), BF16),
                        pltpu.VMEM((S, LANES), BF16),
                        pltpu.VMEM((S, LANES), BF16),
                        pltpu.VMEM((S, HCHUNK * LANES), BF16),
                        pltpu.VMEM((S // HCHUNK, LANES, LANES), BF16),
                        pltpu.VMEM((S // HCHUNK, LANES), F32),
                        pltpu.VMEM((S // HBLOCK, LANES, LANES), BF16)],
        compiler_params=_cparams(("parallel", "parallel")),
    )(hq, hf, hi, hg, lb_logits, norm_w.reshape(1, W))


def _layer_norm(v, g, b):
    mu = jnp.mean(v, axis=-1, keepdims=True)
    d = v - mu
    var = jnp.mean(d * d, axis=-1, keepdims=True)
    return d * lax.rsqrt(var + LN_EPS) * g + b


def _bf16_bits(x):
    return (pltpu.bitcast(x, jnp.uint32) + jnp.uint32(0x8000)) & jnp.uint32(0xFFFF0000)


def _store_chunks(ref, val):
    n = ref.shape[0]
    for j in range(n):
        lo = _bf16_bits(val[:, j * LANES:(j + 1) * LANES]) >> 16
        hi = _bf16_bits(val[:, (j + n) * LANES:(j + n + 1) * LANES])
        ref[j] = pltpu.bitcast(lo | hi, F32)


def _load_chunks(ref):
    words = [pltpu.bitcast(ref[j], jnp.uint32) for j in range(ref.shape[0])]
    lo = [pltpu.bitcast(w << 16, F32) for w in words]
    hi = [pltpu.bitcast(w & jnp.uint32(0xFFFF0000), F32) for w in words]
    return jnp.concatenate(lo + hi, axis=1)


def _mix_kernel(yf_ref, oh_ref, gf_ref, gh_ref, x_ref, g1_ref, sc2_ref, sh2_ref,
                wuf_ref, wuh_ref, wo_ref, lg_ref, lbias_ref, wr_ref, br_ref,
                x1_ref, h2_ref, ri_ref, rt_ref, cnt_ref, carry_sc, *, alpha, ngroups, nper):
    first = (pl.program_id(0) == 0) & (pl.program_id(1) == 0)

    @pl.when(first)
    def _():
        carry_sc[...] = jnp.zeros_like(carry_sc)

    tm = x_ref.shape[0]
    yf = jnp.dot(yf_ref[...], wuf_ref[...], preferred_element_type=F32)
    yh = jnp.dot(oh_ref[...], wuh_ref[...], preferred_element_type=F32)
    merged = gf_ref[...].astype(F32) * yf + gh_ref[...].astype(F32) * yh
    y = jnp.dot(merged.astype(BF16), wo_ref[...], preferred_element_type=F32)
    x1 = _layer_norm(alpha * x_ref[...] + g1_ref[...] * y, lg_ref[...], lbias_ref[...])
    x1_ref[...] = x1
    h2 = x1 * (1.0 + sc2_ref[...]) + sh2_ref[...]
    _store_chunks(h2_ref, h2)

    h_hi, h_lo = _bf16_pieces(h2, 2)
    hh = jnp.dot(h_hi, wr_ref[...], preferred_element_type=F32)
    logits = (hh[:, :LANES] + hh[:, LANES:]
              + jnp.dot(h_lo, wr_ref[:, :LANES], preferred_element_type=F32)) + br_ref[...]
    lt = logits.T
    rowi = lax.broadcasted_iota(jnp.int32, (LANES, tm), 0)
    big = jnp.int32(1 << 20)

    def argmax_first(vals, mask):
        mx = jnp.max(jnp.where(mask, vals, -jnp.inf), axis=0, keepdims=True)
        idx = jnp.min(jnp.where(mask & (vals == mx), rowi, big), axis=0, keepdims=True)
        return mx, idx

    gmask = rowi < ngroups
    gmax = jnp.max(jnp.where(gmask, lt, -jnp.inf), axis=0, keepdims=True)
    gexp = jnp.where(gmask, jnp.exp(lt - gmax), 0.0)
    gprob = gexp / jnp.sum(gexp, axis=0, keepdims=True)
    g_w, g_idx = argmax_first(gprob, gmask)

    lo = ngroups + g_idx * nper
    emask = (rowi >= lo) & (rowi < lo + nper)
    emax = jnp.max(jnp.where(emask, lt, -jnp.inf), axis=0, keepdims=True)
    eexp = jnp.where(emask, jnp.exp(lt - emax), 0.0)
    eprob = eexp / jnp.sum(eexp, axis=0, keepdims=True)
    p0, i0 = argmax_first(eprob, emask)
    p1, i1 = argmax_first(eprob, emask & (rowi != i0))
    den = p0 + p1
    w0 = p0 / den * g_w
    w1 = p1 / den * g_w
    e0 = i0 - ngroups
    e1 = i1 - ngroups

    oht = jnp.where((rowi == e0) | (rowi == e1), 1.0, 0.0)
    r = lax.broadcasted_iota(jnp.int32, (tm, tm), 0)
    c = lax.broadcasted_iota(jnp.int32, (tm, tm), 1)
    earlier = jnp.where(r < c, 1.0, 0.0).astype(BF16)
    er = lax.broadcasted_iota(jnp.int32, (LANES, LANES), 0)
    ec = lax.broadcasted_iota(jnp.int32, (LANES, LANES), 1)
    carry_col = jnp.sum(jnp.where(er == ec, carry_sc[...], 0.0), axis=1, keepdims=True)
    before = jnp.dot(oht.astype(BF16), earlier, preferred_element_type=F32) + carry_col
    rank0 = jnp.sum(jnp.where(rowi == e0, before, 0.0), axis=0, keepdims=True)
    rank1 = jnp.sum(jnp.where(rowi == e1, before, 0.0), axis=0, keepdims=True)
    inc_col = jnp.sum(oht, axis=1, keepdims=True)
    carry_sc[...] = carry_sc[...] + jnp.sum(jnp.where(er == ec, inc_col, 0.0), axis=0, keepdims=True)
    cnt_ref[...] = carry_sc[...]

    info = jnp.where(rowi == 0, w0, 0.0)
    info = jnp.where(rowi == 1, w1, info)
    info = jnp.where(rowi == 2, e0.astype(F32), info)
    info = jnp.where(rowi == 3, e1.astype(F32), info)
    info = jnp.where(rowi == 4, rank0, info)
    info = jnp.where(rowi == 5, rank1, info)
    rt_ref[...] = info[:ROW_TILE, :]
    ri_ref[...] = info.T


def _mix(yf, oh, gf, gh, x, g1, sc2, sh2, wuf, wuh, wo, ln_g, ln_b, wr, br, alpha, ngroups, nper, tm=512):
    B, S, D = x.shape
    W = yf.shape[2]
    tok = lambda w: pl.BlockSpec((None, tm, w), lambda b, i: (b, i, 0))
    vec = pl.BlockSpec((None, 1, D), lambda b, i: (b, 0, 0))
    full = lambda a: pl.BlockSpec(a.shape, lambda b, i: (0,) * a.ndim)
    return pl.pallas_call(
        functools.partial(_mix_kernel, alpha=alpha, ngroups=ngroups, nper=nper),
        out_shape=(jax.ShapeDtypeStruct((B, S, D), F32),
                   jax.ShapeDtypeStruct((D // WORD_LANES, B * S, LANES), F32),
                   jax.ShapeDtypeStruct((B, S, LANES), F32),
                   jax.ShapeDtypeStruct((ROW_TILE, B * S), F32),
                   jax.ShapeDtypeStruct((1, LANES), F32)),
        grid=(B, S // tm),
        in_specs=[tok(W), tok(W), tok(D), tok(D), tok(D), vec, vec, vec,
                  full(wuf), full(wuh), full(wo), full(ln_g), full(ln_b), full(wr), full(br)],
        out_specs=(tok(D),
                   pl.BlockSpec((D // WORD_LANES, tm, LANES), lambda b, i: (0, b * (S // tm) + i, 0)),
                   tok(LANES),
                   pl.BlockSpec((ROW_TILE, tm), lambda b, i: (0, b * (S // tm) + i)),
                   pl.BlockSpec((1, LANES), lambda b, i: (0, 0))),
        scratch_shapes=[pltpu.VMEM((1, LANES), F32)],
        compiler_params=_cparams(("arbitrary", "arbitrary")),
    )(yf, oh, gf, gh, x, g1, sc2, sh2, wuf, wuh, wo, ln_g, ln_b, wr, br)


def _sc_mesh():
    return plsc.VectorSubcoreMesh(core_axis_name="core", subcore_axis_name="subcore")


def _sc_pipeline(body, grid, in_specs, out_specs):
    return pltpu.emit_pipeline(body, grid=grid, in_specs=in_specs, out_specs=out_specs,
                               core_axis_name=("core", "subcore"),
                               dimension_semantics=(pltpu.PARALLEL,) * len(grid))


def _sc_scatter_rows(src, rows_a, rows_b, n_out):
    nj, t = rows_a.shape
    win = SC_WINDOW
    nc = t // win

    @pl.kernel(out_type=jax.ShapeDtypeStruct((n_out, LANES), src.dtype), mesh=_sc_mesh(), scratch_types=[])
    def scatter(x_hbm, a_hbm, b_hbm, o_hbm):
        def body(x_vmem, a_vmem, b_vmem):
            pltpu.sync_copy(x_vmem, o_hbm.at[a_vmem.at[0]])
            pltpu.sync_copy(x_vmem, o_hbm.at[b_vmem.at[0]])

        idx = pl.BlockSpec((1, win), lambda j, c: (j, c))
        _sc_pipeline(body, (nj, nc), [pl.BlockSpec((win, LANES), lambda j, c: (j * nc + c, 0)), idx, idx],
                     [])(x_hbm, a_hbm, b_hbm)

    return scatter(src, rows_a, rows_b)


def _sc_gather_rows(table, rows):
    nr, t = rows.shape
    win = SC_WINDOW
    nc = t // win

    @pl.kernel(out_type=jax.ShapeDtypeStruct((nr * t, LANES), table.dtype), mesh=_sc_mesh(), scratch_types=[])
    def gather(x_hbm, i_hbm, o_hbm):
        def body(i_vmem, o_vmem):
            pltpu.sync_copy(x_hbm.at[i_vmem.at[0]], o_vmem)

        _sc_pipeline(body, (nr, nc), [pl.BlockSpec((1, win), lambda r, c: (r, c))],
                     [pl.BlockSpec((win, LANES), lambda r, c: (r * nc + c, 0))])(i_hbm, o_hbm)

    return gather(table, rows)


W_SLOTS = 3
EXPERT_SUBTILE = 128


def _experts_kernel(tn_ref, tb_ref, run_ref, first_ref, rexp_ref, nrun_ref,
                    x_ref, wg_hbm, wu_hbm, wd_hbm, o_ref, wg_sc, wu_sc, wd_sc, sems):
    del tb_ref
    i = pl.program_id(0)
    nrows = tn_ref[i]
    run = run_ref[i]
    nruns = nrun_ref[0]

    def copies(r, slot):
        e = rexp_ref[r]
        return [pltpu.make_async_copy(hbm.at[e], buf.at[slot], sems.at[slot])
                for hbm, buf in ((wg_hbm, wg_sc), (wu_hbm, wu_sc), (wd_hbm, wd_sc))]

    def fetch(r):
        if isinstance(r, int):
            for cp in copies(r, r % W_SLOTS):
                cp.start()
            return
        for s in range(W_SLOTS):
            @pl.when(r % W_SLOTS == s)
            def _(s=s):
                for cp in copies(r, s):
                    cp.start()

    ahead = W_SLOTS - 1

    @pl.when(i == 0)
    def _():
        for r in range(ahead):
            pl.when(r < nruns)(functools.partial(fetch, r))

    for s in range(W_SLOTS):
        @pl.when((nrows > 0) & (run % W_SLOTS == s))
        def _(s=s):
            @pl.when(first_ref[i] != 0)
            def _():
                for cp in copies(run, s):
                    cp.wait()

                @pl.when(run + ahead < nruns)
                def _():
                    fetch(run + ahead)

            def rows(xr, yr):
                x = _load_chunks(xr)
                x = jnp.where(lax.broadcasted_iota(jnp.int32, x.shape, 0) < nrows, x, 0.0).astype(BF16)
                g = jnp.dot(x, wg_sc[s], preferred_element_type=F32)
                u = jnp.dot(x, wu_sc[s], preferred_element_type=F32)
                hid = (_silu(g) * u).astype(BF16)
                _store_chunks(yr, jnp.dot(hid, wd_sc[s], preferred_element_type=F32))

            sub = EXPERT_SUBTILE

            @pl.when(nrows > sub)
            def _():
                rows(x_ref, o_ref)

            @pl.when(nrows <= sub)
            def _():
                rows(x_ref.at[:, pl.ds(0, sub), :], o_ref.at[:, pl.ds(0, sub), :])
                o_ref[:, sub:, :] = jnp.zeros((o_ref.shape[0], o_ref.shape[1] - sub, LANES), F32)


def _experts(tile_rows, tile_block, tile_run, tile_first, run_expert, nruns, xs, wg, wu, wd, tm):
    E, D, FF = wg.shape
    dt = D // WORD_LANES
    ntiles = tile_rows.shape[0]
    rows = pl.BlockSpec((dt, tm, LANES), lambda i, tn, tb, *_: (0, tb[i], 0))
    hbm = pl.BlockSpec(memory_space=pl.ANY)
    grid_spec = pltpu.PrefetchScalarGridSpec(
        num_scalar_prefetch=6,
        grid=(ntiles,),
        in_specs=[rows, hbm, hbm, hbm],
        out_specs=rows,
        scratch_shapes=[pltpu.VMEM((W_SLOTS, D, FF), BF16), pltpu.VMEM((W_SLOTS, D, FF), BF16),
                        pltpu.VMEM((W_SLOTS, FF, D), BF16), pltpu.SemaphoreType.DMA((W_SLOTS,))],
    )
    return pl.pallas_call(
        _experts_kernel,
        out_shape=jax.ShapeDtypeStruct((dt, ntiles * tm, LANES), F32),
        grid_spec=grid_spec,
        compiler_params=_cparams(("arbitrary",)),
    )(tile_rows, tile_block, tile_run, tile_first, run_expert, nruns, xs, wg, wu, wd)


def _combine_kernel(yg_ref, x1_ref, ri_ref, g2_ref, lg_ref, lb_ref, o_ref, *, alpha):
    ri = ri_ref[...]
    y = ri[:, 0:1] * _load_chunks(yg_ref.at[0]) + ri[:, 1:2] * _load_chunks(yg_ref.at[1])
    o_ref[...] = _layer_norm(alpha * x1_ref[...] + g2_ref[...] * y, lg_ref[...], lb_ref[...])


def _combine(yg, x1, rinfo, g2, ln_g, ln_b, alpha, tm=1024):
    B, S, D = x1.shape
    nb = S // tm
    return pl.pallas_call(
        functools.partial(_combine_kernel, alpha=alpha),
        out_shape=jax.ShapeDtypeStruct((B, S, D), F32),
        grid=(B, nb),
        in_specs=[pl.BlockSpec((2, D // WORD_LANES, tm, LANES), lambda b, i: (0, 0, b * nb + i, 0)),
                  pl.BlockSpec((None, tm, D), lambda b, i: (b, i, 0)),
                  pl.BlockSpec((None, tm, LANES), lambda b, i: (b, i, 0)),
                  pl.BlockSpec((None, 1, D), lambda b, i: (b, 0, 0)),
                  pl.BlockSpec((1, D), lambda b, i: (0, 0)),
                  pl.BlockSpec((1, D), lambda b, i: (0, 0))],
        out_specs=pl.BlockSpec((None, tm, D), lambda b, i: (b, i, 0)),
        compiler_params=_cparams(("parallel", "parallel")),
    )(yg, x1, rinfo, g2, ln_g, ln_b)


def kernel(x, c, w_ada, b_ada, w_in, b_fox_forget, hgrn_lb_logits, hgrn_norm_w, w_up_fox, w_up_hgrn, w_out,
           ln1_g, ln1_b, w_router_group, b_router_group, w_router_expert, b_router_expert,
           w_expert_gate, w_expert_up, w_expert_down, ln2_g, ln2_b):
    B, S, D = x.shape
    depth = w_ada.shape[0]
    assert depth == 1, "single-layer block"
    fox_heads = b_fox_forget.shape[1]
    fox_w = fox_heads * HEAD_DIM
    hgrn_w = hgrn_norm_w.shape[1]
    ngroups = w_router_group.shape[2]
    nexp = w_router_expert.shape[2]
    nper = nexp // ngroups
    alpha = (2 * depth) ** 0.25
    T = B * S

    ada = _ada(c, w_ada[0], b_ada[0])
    sh1, sc1, g1, sh2, sc2, g2 = [a.reshape(B, 1, D) for a in jnp.split(ada, 6, axis=-1)]

    wi = w_in[0]
    o_ff = 3 * fox_w
    w_fox = jnp.pad(wi[:, :o_ff + fox_heads], ((0, 0), (0, LANES - fox_heads))).astype(BF16)
    w_rest = wi[:, o_ff + fox_heads:].astype(BF16)
    widths = [fox_w, fox_w, fox_w, LANES, hgrn_w, hgrn_w, hgrn_w, hgrn_w, D, D]
    segs, off = [], 0
    for n, w in enumerate(widths):
        if n == 4:
            off = 0
        segs.append((off, off + w))
        off += w
    fq, fk, fv, ffp, hq, hf, hi, hg, gf, gh = _inproj(x, sc1, sh1, w_fox, w_rest, segs)

    bias_p = jnp.zeros((1, LANES), F32).at[0, :fox_heads].set(b_fox_forget[0])
    cum = _foxcum(ffp, bias_p)
    y_fox, wg_b, wu_b, wd_b = _fox(fq, fk, fv, cum, (w_expert_gate[0], w_expert_up[0], w_expert_down[0]))

    o_h = _hgrn(hq, hf, hi, hg, hgrn_lb_logits, hgrn_norm_w[0])

    wr = jnp.zeros((D, LANES), F32).at[:, :ngroups].set(w_router_group[0]).at[:, ngroups:ngroups + nexp].set(
        w_router_expert[0])
    wr_hi = lax.bitcast_convert_type(lax.bitcast_convert_type(wr, jnp.uint32) & jnp.uint32(0xFFFF0000), F32)
    wr = jnp.concatenate([wr_hi.astype(BF16), (wr - wr_hi).astype(BF16)], axis=1)
    br = jnp.zeros((1, LANES), F32).at[0, :ngroups].set(b_router_group[0]).at[0, ngroups:ngroups + nexp].set(
        b_router_expert[0])
    x1, h2, rinfo, fields, counts = _mix(
        y_fox, o_h, gf, gh, x, g1, sc2, sh2,
        w_up_fox[0].astype(BF16), w_up_hgrn[0].astype(BF16), w_out[0].astype(BF16),
        ln1_g[0].reshape(1, D), ln1_b[0].reshape(1, D), wr, br, alpha, ngroups, nper)

    tm_e = 512
    dt = D // WORD_LANES
    ntiles = (2 * T) // tm_e + nexp
    nslots = ntiles * tm_e
    cnt = counts[0, :nexp].astype(jnp.int32)
    padded = ((cnt + tm_e - 1) // tm_e) * tm_e
    ends = jnp.cumsum(padded)
    starts = ends - padded
    eid = fields[2:4].astype(jnp.int32)
    rank = fields[4:6].astype(jnp.int32)
    first = jnp.sum(jnp.where(eid[None] == jnp.arange(nexp, dtype=jnp.int32)[:, None, None],
                              starts[:, None, None], 0), axis=0)
    pos = first + rank
    tile_start = jnp.arange(ntiles, dtype=jnp.int32) * tm_e
    tile_block = jnp.minimum(jnp.arange(ntiles, dtype=jnp.int32), ends[-1] // tm_e - 1)
    tile_expert = jnp.minimum(jnp.sum((tile_start[:, None] >= ends[None, :]).astype(jnp.int32), axis=1), nexp - 1)
    tile_rows = jnp.clip(starts[tile_expert] + cnt[tile_expert] - tile_start, 0, tm_e)
    used = jnp.cumsum((cnt > 0).astype(jnp.int32))
    nruns = used[-1:]
    run_expert = jnp.sum((used[None, :] <= jnp.arange(nexp + 2, dtype=jnp.int32)[:, None]).astype(jnp.int32), axis=1)
    run_expert = jnp.minimum(run_expert, nexp - 1)
    tile_run = used[tile_expert] - 1
    prev_expert = jnp.concatenate([jnp.full((1,), -1, jnp.int32), tile_expert[:-1]])
    tile_first = ((tile_rows > 0) & (tile_expert != prev_expert)).astype(jnp.int32)
    rows = pos[:, None, :] + (jnp.arange(dt, dtype=jnp.int32) * nslots)[None, :, None]

    xs = _sc_scatter_rows(h2.reshape(dt * T, LANES), rows[0], rows[1], dt * nslots)
    ys = _experts(tile_rows, tile_block, tile_run, tile_first, run_expert, nruns,
                  xs.reshape(dt, nslots, LANES), wg_b, wu_b, wd_b, tm_e)
    yg = _sc_gather_rows(ys.reshape(dt * nslots, LANES), rows.reshape(2 * dt, T))
    return _combine(yg.reshape(2, dt, T, LANES), x1, rinfo, g2,
                    ln2_g[0].reshape(1, D), ln2_b[0].reshape(1, D), alpha)
```
